```python
import math
import jax, jax.numpy as jnp
from jax import lax
import numpy as np

D_MODEL = 1024
BATCH = 8
SEQ = 2048
DEPTH = 2
DEC_BATCH = 8
DEC_SEQ = 64
PAST_LEN = 4096

CHUNK = 64
HEAD_DIM = 64
N_EVEN = (DEPTH + 1) // 2
N_ODD = DEPTH // 2
EPS = 1e-6
A_HEADS = 8
A_KV_HEADS = 2
A_GROUP = A_HEADS // A_KV_HEADS
WINDOW = 128
A_PREV_CHUNKS = WINDOW // CHUNK
T5_BUCKETS = 32
T5_MAX_DIST = 128
B_HEADS = 8
B_PREV_CHUNKS = 8
B_REACH = B_PREV_CHUNKS * CHUNK
B_MAX_REL = 128
C_WIDTH = 512
CONV_W = 3
D_HEADS = 8
D_Q_LORA = 256
D_KV_LORA = 128
D_NOPE = 64
D_ROPE = 32
D_V = 64
ROPE_THETA = 10000.0
Q_BLOCK = 128
FFN_DIM = 2816
A_Q = A_HEADS * HEAD_DIM
A_KV = A_KV_HEADS * HEAD_DIM
B_QKV = B_HEADS * HEAD_DIM
EVEN_IN = A_Q + 2 * A_KV + 3 * B_QKV
EVEN_MIX = A_Q + B_QKV
ODD_IN = 3 * C_WIDTH + D_Q_LORA + D_KV_LORA + D_ROPE
ODD_MIX = C_WIDTH + D_HEADS * D_V

kernel_name = "hybrid_streaming_encoder_step"


def rms_norm(x, g):
    xf = x.astype(jnp.float32)
    y = xf * lax.rsqrt(jnp.mean(xf * xf, axis=-1, keepdims=True) + EPS)
    return (y * g.astype(jnp.float32)).astype(x.dtype)


def split_cols(z, sizes):
    idx, acc = [], 0
    for s in sizes[:-1]:
        acc += s
        idx.append(acc)
    return jnp.split(z, idx, axis=-1)


def swiglu(h, w_gu, w_down):
    g, u = jnp.split(h @ w_gu, 2, axis=-1)
    return (jax.nn.silu(g) * u) @ w_down


def rope(x, pos):
    half = D_ROPE // 2
    inv = 1.0 / (ROPE_THETA ** (jnp.arange(half, dtype=jnp.float32) / half))
    ang = pos.astype(jnp.float32)[:, None] * inv[None, :]
    cos = jnp.cos(ang)[:, None, :]
    sin = jnp.sin(ang)[:, None, :]
    xf = x.astype(jnp.float32)
    x1, x2 = xf[..., :half], xf[..., half:]
    return jnp.concatenate([x1 * cos - x2 * sin, x1 * sin + x2 * cos], axis=-1).astype(x.dtype)


def t5_bucket(rel):
    nb = T5_BUCKETS // 2
    max_exact = nb // 2
    n = -rel
    ret = jnp.where(n < 0, nb, 0)
    n = jnp.abs(n)
    nf = jnp.maximum(n, 1).astype(jnp.float32)
    large = max_exact + (jnp.log(nf / max_exact) / math.log(T5_MAX_DIST / max_exact) * (nb - max_exact)).astype(jnp.int32)
    large = jnp.minimum(large, nb - 1)
    return ret + jnp.where(n < max_exact, n, large)


def t5_bias(table, q_pos, k_pos):
    b = table.astype(jnp.float32)[t5_bucket(k_pos[None, :] - q_pos[:, None])]
    return b.transpose(2, 0, 1).reshape(A_KV_HEADS, A_GROUP, q_pos.shape[0], k_pos.shape[0])


def rel_bias(table, q_pos, k_pos):
    idx = jnp.clip(k_pos[None, :] - q_pos[:, None], -B_MAX_REL, B_MAX_REL) + B_MAX_REL
    return table.astype(jnp.float32)[:, idx][:, None]


def chunk_visible(q_pos, k_pos, n_prev):
    qc = q_pos[:, None] // CHUNK
    kc = k_pos[None, :] // CHUNK
    m = (kc <= qc) & (k_pos[None, :] >= 0)
    if n_prev is not None:
        m = m & (kc >= qc - n_prev)
    return m


def attend(q, k, v, q_pos, k_pos, n_prev, bias=None, sinks=None):
    b, tq = q.shape[0], q.shape[1]
    s = jnp.einsum('bqkgd,bskd->bkgqs', q, k, preferred_element_type=jnp.float32) * (q.shape[-1] ** -0.5)
    if bias is not None:
        s = s + bias
    s = jnp.where(chunk_visible(q_pos, k_pos, n_prev), s, -jnp.inf)
    if sinks is None:
        p = jax.nn.softmax(s, axis=-1)
    else:
        sk = sinks.astype(jnp.float32)[None, :, :, None, None]
        m = jnp.maximum(jnp.max(s, axis=-1, keepdims=True), sk)
        e = jnp.exp(s - m)
        p = e / (jnp.sum(e, axis=-1, keepdims=True) + jnp.exp(sk - m))
    o = jnp.einsum('bkgqs,bskd->bqkgd', p.astype(v.dtype), v)
    return o.reshape(b, tq, -1)


def band_attention_prompt(q, k, v, n_prev, bias_fn, sinks):
    b, s = q.shape[0], q.shape[1]
    pad = n_prev * CHUNK
    band = pad + CHUNK
    kp = jnp.pad(k, ((0, 0), (pad, 0), (0, 0), (0, 0)))
    vp = jnp.pad(v, ((0, 0), (pad, 0), (0, 0), (0, 0)))

    def one_chunk(c):
        start = c * CHUNK
        qb = lax.dynamic_slice_in_dim(q, start, CHUNK, axis=1)
        kb = lax.dynamic_slice_in_dim(kp, start, band, axis=1)
        vb = lax.dynamic_slice_in_dim(vp, start, band, axis=1)
        q_pos = start + jnp.arange(CHUNK)
        k_pos = start - pad + jnp.arange(band)
        return attend(qb, kb, vb, q_pos, k_pos, n_prev, bias_fn(q_pos, k_pos), sinks)

    out = lax.map(one_chunk, jnp.arange(s // CHUNK))
    return out.transpose(1, 0, 2, 3).reshape(b, s, -1)


def even_qkv(h, p):
    b, t, _ = h.shape
    aq, ak, av, bq, bk, bv = split_cols(h @ p['w_in'], (A_Q, A_KV, A_KV, B_QKV, B_QKV, B_QKV))
    aq = rms_norm(aq.reshape(b, t, A_KV_HEADS, A_GROUP, HEAD_DIM), p['a_qn'])
    ak = rms_norm(ak.reshape(b, t, A_KV_HEADS, HEAD_DIM), p['a_kn'])
    av = av.reshape(b, t, A_KV_HEADS, HEAD_DIM)
    bq = rms_norm(bq.reshape(b, t, B_HEADS, 1, HEAD_DIM), p['b_qn'])
    bk = rms_norm(bk.reshape(b, t, B_HEADS, HEAD_DIM), p['b_kn'])
    bv = bv.reshape(b, t, B_HEADS, HEAD_DIM)
    return aq, ak, av, bq, bk, bv


def even_prompt(h, p, t5_table):
    s = h.shape[1]
    aq, ak, av, bq, bk, bv = even_qkv(h, p)
    sinks = p['a_sinks'].reshape(A_KV_HEADS, A_GROUP)
    ya = band_attention_prompt(aq, ak, av, A_PREV_CHUNKS, lambda qp, kp: t5_bias(t5_table, qp, kp), sinks)
    yb = band_attention_prompt(bq, bk, bv, B_PREV_CHUNKS, lambda qp, kp: rel_bias(p['b_rel'], qp, kp), None)
    y = jnp.concatenate([ya, yb], axis=-1) @ p['w_out']
    la, lb = min(WINDOW, s), min(B_REACH, s)
    return y, (ak[:, s - la:], av[:, s - la:], bk[:, s - lb:], bv[:, s - lb:])


def even_sample(h, p, t5_table, ck_a, cv_a, ck_b, cv_b):
    t = h.shape[1]
    la, lb = ck_a.shape[1], ck_b.shape[1]
    aq, ak, av, bq, bk, bv = even_qkv(h, p)
    q_pos = PAST_LEN + jnp.arange(t)
    ka = jnp.concatenate([ck_a, ak], axis=1)
    va = jnp.concatenate([cv_a, av], axis=1)
    kpa = jnp.concatenate([PAST_LEN - la + jnp.arange(la), q_pos])
    ya = attend(aq, ka, va, q_pos, kpa, A_PREV_CHUNKS, t5_bias(t5_table, q_pos, kpa), p['a_sinks'].reshape(A_KV_HEADS, A_GROUP))
    kb = jnp.concatenate([ck_b, bk], axis=1)
    vb = jnp.concatenate([cv_b, bv], axis=1)
    kpb = jnp.concatenate([PAST_LEN - lb + jnp.arange(lb), q_pos])
    yb = attend(bq, kb, vb, q_pos, kpb, B_PREV_CHUNKS, rel_bias(p['b_rel'], q_pos, kpb), None)
    y = jnp.concatenate([ya, yb], axis=-1) @ p['w_out']
    return y, (ka[:, -la:], va[:, -la:], kb[:, -lb:], vb[:, -lb:])


def odd_project(h, p, pos):
    b, t, _ = h.shape
    cb, cc, ch, qa, kva, kr = split_cols(h @ p['w_in'], (C_WIDTH, C_WIDTH, C_WIDTH, D_Q_LORA, D_KV_LORA, D_ROPE))
    u = cc * ch
    q = (rms_norm(qa, p['q_a_norm']) @ p['w_q_b']).reshape(b, t, D_HEADS, D_NOPE + D_ROPE)
    q = jnp.concatenate([rms_norm(q[..., :D_NOPE], p['qn_nope']),
                         rope(rms_norm(q[..., D_NOPE:], p['qn_rope']), pos)], axis=-1)
    ckv = rms_norm(kva, p['kv_a_norm'])
    kpe = rope(rms_norm(kr, p['kn_rope'])[:, :, None, :], pos)[:, :, 0, :]
    return cb, u, q[:, :, :, None, :], ckv, kpe


def short_conv(u, prev, w):
    t = u.shape[1]
    up = jnp.concatenate([prev, u], axis=1)
    y = w[0] * up[:, 0:t]
    for j in range(1, CONV_W):
        y = y + w[j] * up[:, j:j + t]
    return y, up[:, -(CONV_W - 1):]


def mla_keys(ckv, kpe, p):
    b, tk, _ = ckv.shape
    kv = (ckv @ p['w_kv_b']).reshape(b, tk, D_HEADS, D_NOPE + D_V)
    k = jnp.concatenate([rms_norm(kv[..., :D_NOPE], p['kn_nope']),
                         jnp.broadcast_to(kpe[:, :, None, :], (b, tk, D_HEADS, D_ROPE))], axis=-1)
    return k, kv[..., D_NOPE:]


def odd_prompt(h, p):
    b, s, _ = h.shape
    pos = jnp.arange(s)
    cb, u, q, ckv, kpe = odd_project(h, p, pos)
    yconv, conv_state = short_conv(u, jnp.zeros((b, CONV_W - 1, C_WIDTH), u.dtype), p['conv_w'])
    yc = cb * yconv
    k, v = mla_keys(ckv, kpe, p)

    def one_block(i):
        start = i * Q_BLOCK
        qb = lax.dynamic_slice_in_dim(q, start, Q_BLOCK, axis=1)
        return attend(qb, k, v, start + jnp.arange(Q_BLOCK), pos, None)

    yd = lax.map(one_block, jnp.arange(s // Q_BLOCK))
    yd = yd.transpose(1, 0, 2, 3).reshape(b, s, -1)
    y = jnp.concatenate([yc, yd], axis=-1) @ p['w_out']
    return y, (conv_state, ckv, kpe)


def odd_sample(h, p, conv_prev, c_ckv, c_kpe):
    t = h.shape[1]
    pos = PAST_LEN + jnp.arange(t)
    cb, u, q, ckv, kpe = odd_project(h, p, pos)
    yconv, conv_state = short_conv(u, conv_prev, p['conv_w'])
    yc = cb * yconv
    k, v = mla_keys(jnp.concatenate([c_ckv, ckv], axis=1), jnp.concatenate([c_kpe, kpe], axis=1), p)
    yd = attend(q, k, v, pos, jnp.arange(PAST_LEN + t), None)
    y = jnp.concatenate([yc, yd], axis=-1) @ p['w_out']
    return y, (conv_state, ckv, kpe)


def setup_inputs(seed: int = 0) -> dict:
    key = jax.random.key(seed)
    ks = jax.random.split(key, 48)
    cnt = [0]

    def nk():
        cnt[0] += 1
        return ks[cnt[0] - 1]

    def nrm(shape, scale):
        return jax.random.normal(nk(), shape, jnp.float32) * scale

    def gain(shape):
        return 1.0 + 0.02 * jax.random.normal(nk(), shape, jnp.float32)

    la = min(WINDOW, PAST_LEN)
    lb = min(B_REACH, PAST_LEN)
    return {
        'x_prompt': nrm((BATCH, SEQ, D_MODEL), 1.0),
        'x_sample': nrm((DEC_BATCH, DEC_SEQ, D_MODEL), 1.0),
        'cache_a_k': nrm((N_EVEN, DEC_BATCH, la, A_KV_HEADS, HEAD_DIM), 1.0),
        'cache_a_v': nrm((N_EVEN, DEC_BATCH, la, A_KV_HEADS, HEAD_DIM), 1.0),
        'cache_b_k': nrm((N_EVEN, DEC_BATCH, lb, B_HEADS, HEAD_DIM), 1.0),
        'cache_b_v': nrm((N_EVEN, DEC_BATCH, lb, B_HEADS, HEAD_DIM), 1.0),
        'state_c_conv': nrm((N_ODD, DEC_BATCH, CONV_W - 1, C_WIDTH), 1.0),
        'cache_d_ckv': nrm((N_ODD, DEC_BATCH, PAST_LEN, D_KV_LORA), 1.0),
        'cache_d_kpe': nrm((N_ODD, DEC_BATCH, PAST_LEN, D_ROPE), 1.0),
        'ff1_norm': gain((DEPTH, D_MODEL)),
        'ff1_w_gu': nrm((DEPTH, D_MODEL, 2 * FFN_DIM), D_MODEL ** -0.5),
        'ff1_w_down': nrm((DEPTH, FFN_DIM, D_MODEL), FFN_DIM ** -0.5),
        'mix_norm': gain((DEPTH, D_MODEL)),
        'ff2_norm': gain((DEPTH, D_MODEL)),
        'ff2_w_gu': nrm((DEPTH, D_MODEL, 2 * FFN_DIM), D_MODEL ** -0.5),
        'ff2_w_down': nrm((DEPTH, FFN_DIM, D_MODEL), FFN_DIM ** -0.5),
        't5_bias_table': nrm((T5_BUCKETS, A_HEADS), 0.5),
        'ev_w_in': nrm((N_EVEN, D_MODEL, EVEN_IN), D_MODEL ** -0.5),
        'ev_w_out': nrm((N_EVEN, EVEN_MIX, D_MODEL), EVEN_MIX ** -0.5),
        'a_q_norm': gain((N_EVEN, HEAD_DIM)),
        'a_k_norm': gain((N_EVEN, HEAD_DIM)),
        'a_sinks': nrm((N_EVEN, A_HEADS), 1.0),
        'b_q_norm': gain((N_EVEN, HEAD_DIM)),
        'b_k_norm': gain((N_EVEN, HEAD_DIM)),
        'b_rel_bias': nrm((N_EVEN, B_HEADS, 2 * B_MAX_REL + 1), 0.5),
        'od_w_in': nrm((N_ODD, D_MODEL, ODD_IN), D_MODEL ** -0.5),
        'od_w_out': nrm((N_ODD, ODD_MIX, D_MODEL), ODD_MIX ** -0.5),
        'c_conv_w': nrm((N_ODD, CONV_W, C_WIDTH), CONV_W ** -0.5),
        'd_q_a_norm': gain((N_ODD, D_Q_LORA)),
        'd_w_q_b': nrm((N_ODD, D_Q_LORA, D_HEADS * (D_NOPE + D_ROPE)), D_Q_LORA ** -0.5),
        'd_kv_a_norm': gain((N_ODD, D_KV_LORA)),
        'd_w_kv_b': nrm((N_ODD, D_KV_LORA, D_HEADS * (D_NOPE + D_V)), D_KV_LORA ** -0.5),
        'd_q_nope_norm': gain((N_ODD, D_NOPE)),
        'd_q_rope_norm': gain((N_ODD, D_ROPE)),
        'd_k_nope_norm': gain((N_ODD, D_NOPE)),
        'd_k_rope_norm': gain((N_ODD, D_ROPE)),
    }


def reference(x_prompt, x_sample, cache_a_k, cache_a_v, cache_b_k, cache_b_v, state_c_conv, cache_d_ckv, cache_d_kpe,
              ff1_norm, ff1_w_gu, ff1_w_down, mix_norm, ff2_norm, ff2_w_gu, ff2_w_down, t5_bias_table,
              ev_w_in, ev_w_out, a_q_norm, a_k_norm, a_sinks, b_q_norm, b_k_norm, b_rel_bias,
              od_w_in, od_w_out, c_conv_w, d_q_a_norm, d_w_q_b, d_kv_a_norm, d_w_kv_b,
              d_q_nope_norm, d_q_rope_norm, d_k_nope_norm, d_k_rope_norm):
    xp, xs = x_prompt, x_sample
    st = {n: [] for n in ('akp', 'avp', 'bkp', 'bvp', 'cp', 'ckvp', 'kpep',
                          'aks', 'avs', 'bks', 'bvs', 'cs', 'ckvs', 'kpes')}
    for l in range(DEPTH):
        i = l // 2
        xp = xp + 0.5 * swiglu(rms_norm(xp, ff1_norm[l]), ff1_w_gu[l], ff1_w_down[l])
        xs = xs + 0.5 * swiglu(rms_norm(xs, ff1_norm[l]), ff1_w_gu[l], ff1_w_down[l])
        hp = rms_norm(xp, mix_norm[l])
        hs = rms_norm(xs, mix_norm[l])
        if l % 2 == 0:
            p = {'w_in': ev_w_in[i], 'w_out': ev_w_out[i], 'a_qn': a_q_norm[i], 'a_kn': a_k_norm[i],
                 'a_sinks': a_sinks[i], 'b_qn': b_q_norm[i], 'b_kn': b_k_norm[i], 'b_rel': b_rel_bias[i]}
            yp, (akp, avp, bkp, bvp) = even_prompt(hp, p, t5_bias_table)
            ys, (aks, avs, bks, bvs) = even_sample(hs, p, t5_bias_table, cache_a_k[i], cache_a_v[i],
                                                   cache_b_k[i], cache_b_v[i])
            for n, a in (('akp', akp), ('avp', avp), ('bkp', bkp), ('bvp', bvp),
                         ('aks', aks), ('avs', avs), ('bks', bks), ('bvs', bvs)):
                st[n].append(a)
        else:
            p = {'w_in': od_w_in[i], 'w_out': od_w_out[i], 'conv_w': c_conv_w[i], 'q_a_norm': d_q_a_norm[i],
                 'w_q_b': d_w_q_b[i], 'kv_a_norm': d_kv_a_norm[i], 'w_kv_b': d_w_kv_b[i],
                 'qn_nope': d_q_nope_norm[i], 'qn_rope': d_q_rope_norm[i],
                 'kn_nope': d_k_nope_norm[i], 'kn_rope': d_k_rope_norm[i]}
            yp, (cp, ckvp, kpep) = odd_prompt(hp, p)
            ys, (cs, ckvs, kpes) = odd_sample(hs, p, state_c_conv[i], cache_d_ckv[i], cache_d_kpe[i])
            for n, a in (('cp', cp), ('ckvp', ckvp), ('kpep', kpep), ('cs', cs), ('ckvs', ckvs), ('kpes', kpes)):
                st[n].append(a)
        xp = xp + yp
        xs = xs + ys
        xp = xp + 0.5 * swiglu(rms_norm(xp, ff2_norm[l]), ff2_w_gu[l], ff2_w_down[l])
        xs = xs + 0.5 * swiglu(rms_norm(xs, ff2_norm[l]), ff2_w_gu[l], ff2_w_down[l])
    return (xp, xs,
            jnp.stack(st['akp']), jnp.stack(st['avp']), jnp.stack(st['bkp']), jnp.stack(st['bvp']),
            jnp.stack(st['cp']), jnp.stack(st['ckvp']), jnp.stack(st['kpep']),
            jnp.stack(st['aks']), jnp.stack(st['avs']), jnp.stack(st['bks']), jnp.stack(st['bvs']),
            jnp.stack(st['cs']), jnp.stack(st['ckvs']), jnp.stack(st['kpes']))
```

```python
import functools
import math

import jax
import jax.numpy as jnp
from jax import lax
from jax.experimental import pallas as pl
from jax.experimental.pallas import tpu as pltpu

F32 = jnp.float32
BF16 = jnp.bfloat16

D_MODEL = 1024
CHUNK = 64
HEAD_DIM = 64
EPS = 1e-6
A_HEADS = 8
A_KV_HEADS = 2
WINDOW = 128
T5_BUCKETS = 32
T5_MAX_DIST = 128
B_HEADS = 8
B_REACH = 512
B_MAX_REL = 128
C_WIDTH = 512
CONV_W = 3
D_HEADS = 8
D_Q_LORA = 256
D_KV_LORA = 128
D_NOPE = 64
D_ROPE = 32
D_V = 64
ROPE_THETA = 10000.0
FFN_DIM = 2816
A_Q = A_HEADS * HEAD_DIM
A_KV = A_KV_HEADS * HEAD_DIM
B_QKV = B_HEADS * HEAD_DIM
EVEN_IN = A_Q + 2 * A_KV + 3 * B_QKV

LANES = 128
TOKEN_TILE = 512
FFN_CHUNK = 256
PAIR_ROWS = 2 * CHUNK
MLA_QBLOCK = 256
VMEM_LIMIT_BYTES = 56 * 1024 * 1024
MLA_SCALE = (D_NOPE + D_ROPE) ** -0.5
NEG_INF = float("-inf")

_NT = (((1,), (1,)), ((), ()))


def _cparams(*sem):
    return pltpu.CompilerParams(dimension_semantics=sem, vmem_limit_bytes=VMEM_LIMIT_BYTES)


def _resident(shape):
    zeros = (0,) * len(shape)
    return pl.BlockSpec(shape, lambda *_: zeros, pipeline_mode=pl.Buffered(1))


def _rms(x, g):
    ms = jnp.mean(x * x, axis=-1, keepdims=True)
    return (x * lax.rsqrt(ms + EPS)) * g


def _dot(a, b):
    return jnp.dot(a, b, preferred_element_type=F32)


def _ffn_kernel(*refs, n_chunks, fused_out):
    if fused_out:
        (x_ref, ya_ref, yb_ref, woa_ref, wob_ref, g_ref, wg_ref, wu_ref, wd_ref,
         o_ref, h_scr, acc_scr) = refs
        y = _dot(ya_ref[...], woa_ref[...]) + _dot(yb_ref[...], wob_ref[...])
        o_ref[...] = x_ref[...] + y
    else:
        x_ref, g_ref, wg_ref, wu_ref, wd_ref, o_ref, h_scr, acc_scr = refs
        o_ref[...] = x_ref[...]
    h_scr[...] = _rms(o_ref[...], g_ref[...]).astype(BF16)
    acc_scr[...] = jnp.zeros_like(acc_scr)

    def step(j, carry):
        h = h_scr[...]
        g = _dot(h, wg_ref[j])
        u = _dot(h, wu_ref[j])
        a = (g * jax.nn.sigmoid(g)) * u
        acc_scr[...] += _dot(a.astype(BF16), wd_ref[j])
        return carry

    lax.fori_loop(0, n_chunks, step, 0)
    o_ref[...] = o_ref[...] + 0.5 * acc_scr[...]


def _ffn(x, norm_g, wg3, wu3, wd3, out_proj=None):
    t = x.shape[0]
    tm = min(TOKEN_TILE, t)
    n_chunks = wg3.shape[0]
    row = lambda w: pl.BlockSpec((tm, w), lambda i: (i, 0))
    in_specs = [row(D_MODEL)]
    args = [x]
    if out_proj is not None:
        ya, yb, woa, wob = out_proj
        in_specs += [row(ya.shape[1]), row(yb.shape[1]), _resident(woa.shape), _resident(wob.shape)]
        args += [ya, yb, woa, wob]
    in_specs += [_resident((1, D_MODEL)), _resident(wg3.shape), _resident(wu3.shape), _resident(wd3.shape)]
    args += [norm_g.reshape(1, D_MODEL), wg3, wu3, wd3]
    return pl.pallas_call(
        functools.partial(_ffn_kernel, n_chunks=n_chunks, fused_out=out_proj is not None),
        grid=(t // tm,),
        in_specs=in_specs,
        out_specs=row(D_MODEL),
        out_shape=jax.ShapeDtypeStruct((t, D_MODEL), F32),
        scratch_shapes=[pltpu.VMEM((tm, D_MODEL), BF16), pltpu.VMEM((tm, D_MODEL), F32)],
        compiler_params=_cparams("arbitrary"),
        name="ffn_out" if out_proj is not None else "ffn",
    )(*args)


def _ffn_weights(w_gu, w_down):
    n = FFN_DIM // FFN_CHUNK
    wg = w_gu[:, :FFN_DIM].astype(BF16).reshape(D_MODEL, n, FFN_CHUNK).transpose(1, 0, 2)
    wu = w_gu[:, FFN_DIM:].astype(BF16).reshape(D_MODEL, n, FFN_CHUNK).transpose(1, 0, 2)
    wd = w_down.astype(BF16).reshape(n, FFN_CHUNK, D_MODEL)
    return wg, wu, wd


def _norm_halves(blk, gain):
    sq = blk * blk
    lo = lax.broadcasted_iota(jnp.int32, blk.shape, 1) < HEAD_DIM
    s_lo = jnp.sum(jnp.where(lo, sq, 0.0), axis=-1, keepdims=True)
    s_hi = jnp.sum(jnp.where(lo, 0.0, sq), axis=-1, keepdims=True)
    inv = jnp.where(lo, lax.rsqrt(s_lo * (1.0 / HEAD_DIM) + EPS), lax.rsqrt(s_hi * (1.0 / HEAD_DIM) + EPS))
    return (blk * inv) * gain


def _even_inproj_kernel(x_ref, g_ref, w_ref, gain_ref,
                        aq_ref, ak_ref, av_ref, bq_ref, bk_ref, bv_ref,
                        cak_ref, cav_ref, cbk_ref, cbv_ref, z_scr, *, tm, tpb, rows_a, rows_b):
    i = pl.program_id(0)
    last = (i % tpb) == (tpb - 1)
    z_scr[...] = _dot(_rms(x_ref[...], g_ref[...]).astype(BF16), w_ref[...])

    def block(c):
        return z_scr[:, c * LANES:(c + 1) * LANES]

    def gain(c):
        return gain_ref[:, c * LANES:(c + 1) * LANES]

    def put_cache(ref, rows, c, val):
        @pl.when(last)
        def _():
            ref[0, :, c * LANES:(c + 1) * LANES] = val[tm - rows:, :]

    for c in range(4):
        aq_ref[:, c * LANES:(c + 1) * LANES] = _norm_halves(block(c), gain(c)).astype(BF16)
    kn = _norm_halves(block(4), gain(4))
    ak_ref[...] = kn.astype(BF16)
    put_cache(cak_ref, rows_a, 0, kn)
    vv = block(5)
    av_ref[...] = vv.astype(BF16)
    put_cache(cav_ref, rows_a, 0, vv)
    for c in range(4):
        bq_ref[:, c * LANES:(c + 1) * LANES] = _norm_halves(block(6 + c), gain(6 + c)).astype(BF16)
        kn = _norm_halves(block(10 + c), gain(10 + c))
        bk_ref[:, c * LANES:(c + 1) * LANES] = kn.astype(BF16)
        put_cache(cbk_ref, rows_b, c, kn)
        vv = block(14 + c)
        bv_ref[:, c * LANES:(c + 1) * LANES] = vv.astype(BF16)
        put_cache(cbv_ref, rows_b, c, vv)


def _even_inproj(x, norm_g, w_in, gain_row, n_batch, rows_a, rows_b):
    t = x.shape[0]
    tm = min(TOKEN_TILE, t)
    tpb = t // n_batch // tm
    assert tpb * tm * n_batch == t and rows_a <= tm and rows_b <= tm
    row = lambda w: pl.BlockSpec((tm, w), lambda i: (i, 0))
    cache = lambda r, w: pl.BlockSpec((1, r, w), lambda i: (i // tpb, 0, 0))
    bshape = lambda w: jax.ShapeDtypeStruct((t, w), BF16)
    cshape = lambda r, w: jax.ShapeDtypeStruct((n_batch, r, w), F32)
    return pl.pallas_call(
        functools.partial(_even_inproj_kernel, tm=tm, tpb=tpb, rows_a=rows_a, rows_b=rows_b),
        grid=(t // tm,),
        in_specs=[row(D_MODEL), _resident((1, D_MODEL)), _resident(w_in.shape), _resident(gain_row.shape)],
        out_specs=[row(A_Q), row(A_KV), row(A_KV), row(B_QKV), row(B_QKV), row(B_QKV),
                   cache(rows_a, A_KV), cache(rows_a, A_KV), cache(rows_b, B_QKV), cache(rows_b, B_QKV)],
        out_shape=[bshape(A_Q), bshape(A_KV), bshape(A_KV), bshape(B_QKV), bshape(B_QKV), bshape(B_QKV),
                   cshape(rows_a, A_KV), cshape(rows_a, A_KV), cshape(rows_b, B_QKV), cshape(rows_b, B_QKV)],
        scratch_shapes=[pltpu.VMEM((tm, EVEN_IN), F32)],
        compiler_params=_cparams("arbitrary"),
        name="even_inproj",
    )(x, norm_g.reshape(1, D_MODEL), w_in, gain_row)


def _pair_perm():
    idx = []
    for p in range(4):
        for half in range(2):
            head = p + 4 * half
            idx.extend(range(head * HEAD_DIM, (head + 1) * HEAD_DIM))
    return jnp.asarray(idx, dtype=jnp.int32)


def _fill_padded(buf, src_ref, ctx):
    buf[0:ctx, :] = jnp.zeros((ctx, buf.shape[1]), buf.dtype)
    buf[ctx:, :] = src_ref[0]


def _attn_a_kernel(q_ref, k_ref, v_ref, bias_ref, sink_ref, o_ref, *scr, nq, ctx, pad_front):
    win = ctx + PAIR_ROWS
    if pad_front:
        kbuf, vbuf = scr
        _fill_padded(kbuf, k_ref, ctx)
        _fill_padded(vbuf, v_ref, ctx)
        kwin = lambda r0: kbuf[pl.ds(r0, win), :]
        vwin = lambda r0: vbuf[pl.ds(r0, win), :]
    else:
        kwin = lambda r0: k_ref[0, pl.ds(r0, win), :]
        vwin = lambda r0: v_ref[0, pl.ds(r0, win), :]
    lo = lax.broadcasted_iota(jnp.int32, (PAIR_ROWS, LANES), 1) < HEAD_DIM
    n_rows = A_HEADS * PAIR_ROWS

    def step(t, carry):
        r0 = pl.multiple_of(t * PAIR_ROWS, PAIR_ROWS)
        q_lo, q_hi = [], []
        for p in range(4):
            qp = q_ref[0, pl.ds(r0, PAIR_ROWS), p * LANES:(p + 1) * LANES]
            zero = jnp.zeros_like(qp)
            q_lo.append(jnp.where(lo, qp, zero))
            q_hi.append(jnp.where(lo, zero, qp))
        q2 = jnp.concatenate(q_lo + q_hi, axis=0)
        s = lax.dot_general(q2, kwin(r0), _NT, preferred_element_type=F32) + bias_ref[...]
        if pad_front:
            col = lax.broadcasted_iota(jnp.int32, (n_rows, win), 1)
            s = jnp.where(col + r0 >= ctx, s, NEG_INF)
        sk = sink_ref[...]
        m = jnp.maximum(jnp.max(s, axis=-1, keepdims=True), sk)
        e = jnp.exp(s - m)
        den = jnp.sum(e, axis=-1, keepdims=True) + jnp.exp(sk - m)
        o2 = _dot(e.astype(BF16), vwin(r0)) / den
        half = 4 * PAIR_ROWS
        for p in range(4):
            o = jnp.where(lo, o2[p * PAIR_ROWS:(p + 1) * PAIR_ROWS],
                          o2[half + p * PAIR_ROWS:half + (p + 1) * PAIR_ROWS])
            o_ref[0, pl.ds(r0, PAIR_ROWS), p * LANES:(p + 1) * LANES] = o.astype(BF16)
        return carry

    lax.fori_loop(0, nq // PAIR_ROWS, step, 0)


def _attn_a(q, k, v, bias, sink, ctx, pad_front):
    nb, nq, _ = q.shape
    nk = k.shape[1]
    assert nq % PAIR_ROWS == 0 and nk == (nq if pad_front else ctx + nq)
    per_b = lambda r, w: pl.BlockSpec((1, r, w), lambda b: (b, 0, 0))
    scratch = [pltpu.VMEM((ctx + nq, A_KV), BF16)] * 2 if pad_front else []
    return pl.pallas_call(
        functools.partial(_attn_a_kernel, nq=nq, ctx=ctx, pad_front=pad_front),
        grid=(nb,),
        in_specs=[per_b(nq, A_Q), per_b(nk, A_KV), per_b(nk, A_KV), _resident(bias.shape), _resident(sink.shape)],
        out_specs=per_b(nq, A_Q),
        out_shape=jax.ShapeDtypeStruct((nb, nq, A_Q), BF16),
        scratch_shapes=scratch,
        compiler_params=_cparams("arbitrary"),
        name="attn_a",
    )(q, k, v, bias, sink)


def _attn_b_kernel(q_ref, k_ref, v_ref, bias_ref, o_ref, *scr, nq, ctx, pad_front):
    win = ctx + PAIR_ROWS
    if pad_front:
        kbuf, vbuf = scr
        _fill_padded(kbuf, k_ref, ctx)
        _fill_padded(vbuf, v_ref, ctx)
        kwin = lambda r0: kbuf[pl.ds(r0, win), :]
        vwin = lambda r0: vbuf[pl.ds(r0, win), :]
    else:
        kwin = lambda r0: k_ref[0, pl.ds(r0, win), :]
        vwin = lambda r0: v_ref[0, pl.ds(r0, win), :]
    lo = lax.broadcasted_iota(jnp.int32, (PAIR_ROWS, LANES), 1) < HEAD_DIM

    def step(t, carry):
        r0 = pl.multiple_of(t * PAIR_ROWS, PAIR_ROWS)
        qp = q_ref[0, pl.ds(r0, PAIR_ROWS), :]
        zero = jnp.zeros_like(qp)
        q2 = jnp.concatenate([jnp.where(lo, qp, zero), jnp.where(lo, zero, qp)], axis=0)
        s = lax.dot_general(q2, kwin(r0), _NT, preferred_element_type=F32) + bias_ref[0]
        if pad_front:
            col = lax.broadcasted_iota(jnp.int32, (2 * PAIR_ROWS, win), 1)
            s = jnp.where(col + r0 >= ctx, s, NEG_INF)
        m = jnp.max(s, axis=-1, keepdims=True)
        e = jnp.exp(s - m)
        den = jnp.sum(e, axis=-1, keepdims=True)
        o2 = _dot(e.astype(BF16), vwin(r0)) / den
        o = jnp.where(lo, o2[:PAIR_ROWS], o2[PAIR_ROWS:])
        o_ref[0, pl.ds(r0, PAIR_ROWS), :] = o.astype(BF16)
        return carry

    lax.fori_loop(0, nq // PAIR_ROWS, step, 0)


def _attn_b(q, k, v, bias, ctx, pad_front):
    nb, nq, _ = q.shape
    nk = k.shape[1]
    assert nq % PAIR_ROWS == 0 and nk == (nq if pad_front else ctx + nq)
    blk = lambda r: pl.BlockSpec((1, r, LANES), lambda b, p: (b, 0, p))
    scratch = [pltpu.VMEM((ctx + nq, LANES), BF16)] * 2 if pad_front else []
    return pl.pallas_call(
        functools.partial(_attn_b_kernel, nq=nq, ctx=ctx, pad_front=pad_front),
        grid=(nb, B_HEADS // 2),
        in_specs=[blk(nq), blk(nk), blk(nk),
                  pl.BlockSpec((1,) + bias.shape[1:], lambda b, p: (p, 0, 0))],
        out_specs=blk(nq),
        out_shape=jax.ShapeDtypeStruct((nb, nq, B_QKV), BF16),
        scratch_shapes=scratch,
        compiler_params=_cparams("arbitrary", "arbitrary"),
        name="attn_b",
    )(q, k, v, bias)


def _t5_bucket(rel):
    nb = T5_BUCKETS // 2
    max_exact = nb // 2
    n = -rel
    ret = jnp.where(n < 0, nb, 0)
    n = jnp.abs(n)
    nf = jnp.maximum(n, 1).astype(F32)
    large = max_exact + (jnp.log(nf / max_exact) / math.log(T5_MAX_DIST / max_exact)
                         * (nb - max_exact)).astype(jnp.int32)
    large = jnp.minimum(large, nb - 1)
    return ret + jnp.where(n < max_exact, n, large)


def _band_geometry(ctx):
    i = jnp.arange(PAIR_ROWS)[:, None]
    j = jnp.arange(ctx + PAIR_ROWS)[None, :]
    rel = (j - ctx) - i
    visible = jnp.where(i < CHUNK, j < ctx + CHUNK, j >= CHUNK)
    return rel, visible


def _bias_a(t5_table):
    rel, visible = _band_geometry(WINDOW)
    tb = t5_table.astype(F32)[_t5_bucket(rel)]
    tb = jnp.where(visible[:, :, None], tb, NEG_INF)
    return tb.transpose(2, 0, 1).reshape(A_HEADS * PAIR_ROWS, WINDOW + PAIR_ROWS)


def _bias_b(b_rel):
    rel, visible = _band_geometry(B_REACH)
    idx = jnp.clip(rel, -B_MAX_REL, B_MAX_REL) + B_MAX_REL
    tb = b_rel.astype(F32)[:, idx]
    tb = jnp.where(visible[None], tb, NEG_INF)
    return tb.reshape(B_HEADS // 2, 2 * PAIR_ROWS, B_REACH + PAIR_ROWS)


ODD_IN_PAD = 2048
ROPE_LO = D_NOPE
ROPE_HI = D_NOPE + D_ROPE
ROPE_HALF = D_ROPE // 2


def _lane_masks(shape):
    lane = lax.broadcasted_iota(jnp.int32, shape, 1)
    return lane < ROPE_LO, (lane >= ROPE_LO) & (lane < ROPE_HI), lane < ROPE_LO + ROPE_HALF


def _rope(y, cosf, sinf, first_half):
    rot = jnp.where(first_half, pltpu.roll(y, LANES - ROPE_HALF, 1), pltpu.roll(y, ROPE_HALF, 1))
    return y * cosf + rot * sinf


def _expand_kv(ckv_b, kpe_blk, wkv_ref, kgain_ref, kv_scr, k_ref, v_ref):
    kv_scr[...] = _dot(ckv_b, wkv_ref[...])
    for h in range(D_HEADS):
        kb = kv_scr[:, h * LANES:(h + 1) * LANES]
        ms = jnp.sum(kb * kb, axis=-1, keepdims=True) * (1.0 / D_NOPE)
        kn = (kb * lax.rsqrt(ms + EPS)) * kgain_ref[...]
        k_ref[:, h * LANES:(h + 1) * LANES] = (kn + kpe_blk).astype(BF16)
    v_ref[...] = kv_scr[:, D_HEADS * LANES:].astype(BF16)


def _odd_inproj_kernel(x_ref, g_ref, w_ref, cinit_ref, convw_ref, qan_ref, wqb_ref, qgain_ref,
                       kvan_ref, krgain_ref, invf_ref, sgn_ref, wkv_ref, kgain_ref,
                       yc_ref, q_ref, k_ref, v_ref, ckv_ref, kpe_ref, cs_ref,
                       z_scr, uext_scr, q_scr, kv_scr, *, tm, tpb, pos0):
    i = pl.program_id(0)
    tile = i % tpb
    z_scr[...] = _dot(_rms(x_ref[...], g_ref[...]).astype(BF16), w_ref[...])

    @pl.when(tile == 0)
    def _():
        uext_scr[0:8, :] = cinit_ref[0]

    u = z_scr[:, C_WIDTH:2 * C_WIDTH] * z_scr[:, 2 * C_WIDTH:3 * C_WIDTH]
    uext_scr[8:8 + tm, :] = u
    yconv = convw_ref[0:1, :] * uext_scr[6:6 + tm, :]
    yconv = yconv + convw_ref[1:2, :] * uext_scr[7:7 + tm, :]
    yconv = yconv + convw_ref[2:3, :] * uext_scr[8:8 + tm, :]
    yc_ref[...] = (z_scr[:, 0:C_WIDTH] * yconv).astype(BF16)

    @pl.when(tile == tpb - 1)
    def _():
        cs_ref[0] = uext_scr[tm + 6:tm + 8, :]

    uext_scr[0:8, :] = uext_scr[tm:tm + 8, :]

    nope, ropem, first_half = _lane_masks((tm, LANES))
    row = lax.broadcasted_iota(jnp.int32, (tm, LANES), 0)
    pos = (row + (tile * tm + pos0)).astype(F32)
    ang = pos * invf_ref[...]
    cosf = jnp.cos(ang)
    sinf = jnp.sin(ang) * sgn_ref[...]

    qn = _rms(z_scr[:, 1536:1792], qan_ref[...]).astype(BF16)
    q_scr[...] = _dot(qn, wqb_ref[...])
    for h in range(D_HEADS):
        blk = q_scr[:, h * LANES:(h + 1) * LANES]
        sq = blk * blk
        s_n = jnp.sum(jnp.where(nope, sq, 0.0), axis=-1, keepdims=True)
        s_r = jnp.sum(jnp.where(ropem, sq, 0.0), axis=-1, keepdims=True)
        inv = jnp.where(nope, lax.rsqrt(s_n * (1.0 / D_NOPE) + EPS), lax.rsqrt(s_r * (1.0 / D_ROPE) + EPS))
        y = _rope((blk * inv) * qgain_ref[...], cosf, sinf, first_half)
        q_ref[:, h * LANES:(h + 1) * LANES] = y.astype(BF16)

    kb = z_scr[:, 1920:2048]
    ms = jnp.sum(kb * kb, axis=-1, keepdims=True) * (1.0 / D_ROPE)
    kpe_blk = _rope((kb * lax.rsqrt(ms + EPS)) * krgain_ref[...], cosf, sinf, first_half)
    kpe_ref[...] = pltpu.roll(kpe_blk, LANES - ROPE_LO, 1)[:, 0:D_ROPE]

    ckv = _rms(z_scr[:, 1792:1920], kvan_ref[...])
    ckv_ref[...] = ckv
    _expand_kv(ckv.astype(BF16), kpe_blk, wkv_ref, kgain_ref, kv_scr, k_ref, v_ref)


def _odd_inproj(x, norm_g, ow, conv_init, n_batch, pos0):
    t = x.shape[0]
    tm = min(TOKEN_TILE, t // n_batch)
    tpb = t // n_batch // tm
    assert tpb * tm * n_batch == t
    row = lambda w: pl.BlockSpec((tm, w), lambda i: (i, 0))
    per_b = lambda r, w: pl.BlockSpec((1, r, w), lambda i: (i // tpb, 0, 0))
    small = [ow["conv_w"], ow["qan"], ow["wqb"], ow["qgain"], ow["kvan"], ow["krgain"],
             ow["invf"], ow["sgn"], ow["wkv"], ow["kgain"]]
    kv_w = D_HEADS * LANES
    v_w = D_HEADS * D_V
    return pl.pallas_call(
        functools.partial(_odd_inproj_kernel, tm=tm, tpb=tpb, pos0=pos0),
        grid=(t // tm,),
        in_specs=[row(D_MODEL), _resident((1, D_MODEL)), _resident(ow["w_in"].shape), per_b(8, C_WIDTH)]
                 + [_resident(a.shape) for a in small],
        out_specs=[row(C_WIDTH), row(kv_w), row(kv_w), row(v_w), row(D_KV_LORA), row(D_ROPE),
                   per_b(CONV_W - 1, C_WIDTH)],
        out_shape=[jax.ShapeDtypeStruct((t, C_WIDTH), BF16), jax.ShapeDtypeStruct((t, kv_w), BF16),
                   jax.ShapeDtypeStruct((t, kv_w), BF16), jax.ShapeDtypeStruct((t, v_w), BF16),
                   jax.ShapeDtypeStruct((t, D_KV_LORA), F32), jax.ShapeDtypeStruct((t, D_ROPE), F32),
                   jax.ShapeDtypeStruct((n_batch, CONV_W - 1, C_WIDTH), F32)],
        scratch_shapes=[pltpu.VMEM((tm, ODD_IN_PAD), F32), pltpu.VMEM((tm + 8, C_WIDTH), F32),
                        pltpu.VMEM((tm, kv_w), F32), pltpu.VMEM((tm, kv_w + v_w), F32)],
        compiler_params=_cparams("arbitrary"),
        name="odd_inproj",
    )(x, norm_g.reshape(1, D_MODEL), ow["w_in"], conv_init, *small)


def _kv_expand_kernel(ckv_ref, kpe_ref, wkv_ref, kgain_ref, k_ref, v_ref, kv_scr):
    _expand_kv(ckv_ref[...].astype(BF16), kpe_ref[...], wkv_ref, kgain_ref, kv_scr, k_ref, v_ref)


def _kv_expand(ckv, kpe_blk, ow):
    t = ckv.shape[0]
    tm = min(TOKEN_TILE, t)
    row = lambda w: pl.BlockSpec((tm, w), lambda i: (i, 0))
    kv_w = D_HEADS * LANES
    v_w = D_HEADS * D_V
    return pl.pallas_call(
        _kv_expand_kernel,
        grid=(t // tm,),
        in_specs=[row(D_KV_LORA), row(LANES), _resident(ow["wkv"].shape), _resident(ow["kgain"].shape)],
        out_specs=[row(kv_w), row(v_w)],
        out_shape=[jax.ShapeDtypeStruct((t, kv_w), BF16), jax.ShapeDtypeStruct((t, v_w), BF16)],
        scratch_shapes=[pltpu.VMEM((tm, kv_w + v_w), F32)],
        compiler_params=_cparams("arbitrary"),
        name="kv_expand",
    )(ckv, kpe_blk, ow["wkv"], ow["kgain"])


def _odd_weights(w_in, conv_w, q_a_norm, w_q_b, kv_a_norm, w_kv_b, qn_nope, qn_rope, kn_nope, kn_rope):
    z = lambda n: jnp.zeros((n,), F32)
    w_pad = jnp.zeros((D_MODEL, ODD_IN_PAD), F32)
    w_pad = w_pad.at[:, :1920].set(w_in[:, :1920])
    w_pad = w_pad.at[:, 1920 + ROPE_LO:1920 + ROPE_HI].set(w_in[:, 1920:])
    qk = D_NOPE + D_ROPE
    wqb = jnp.zeros((D_Q_LORA, D_HEADS, LANES), F32)
    wqb = wqb.at[:, :, :qk].set(w_q_b.reshape(D_Q_LORA, D_HEADS, qk)).reshape(D_Q_LORA, D_HEADS * LANES)
    kvb = w_kv_b.reshape(D_KV_LORA, D_HEADS, D_NOPE + D_V)
    wk = jnp.zeros((D_KV_LORA, D_HEADS, LANES), F32).at[:, :, :D_NOPE].set(kvb[:, :, :D_NOPE])
    wkv = jnp.concatenate([wk.reshape(D_KV_LORA, D_HEADS * LANES),
                           kvb[:, :, D_NOPE:].reshape(D_KV_LORA, D_HEADS * D_V)], axis=1)
    inv = 1.0 / (ROPE_THETA ** (jnp.arange(ROPE_HALF, dtype=F32) / ROPE_HALF))
    ones = jnp.ones((ROPE_HALF,), F32)
    return {
        "w_in": w_pad.astype(BF16),
        "conv_w": conv_w.astype(F32),
        "qan": q_a_norm.reshape(1, D_Q_LORA),
        "wqb": wqb.astype(BF16),
        "qgain": jnp.concatenate([qn_nope, qn_rope, z(LANES - qk)]).reshape(1, LANES),
        "kvan": kv_a_norm.reshape(1, D_KV_LORA),
        "krgain": jnp.concatenate([z(ROPE_LO), kn_rope, z(LANES - ROPE_HI)]).reshape(1, LANES),
        "invf": jnp.concatenate([z(ROPE_LO), inv, inv, z(LANES - ROPE_HI)]).reshape(1, LANES),
        "sgn": jnp.concatenate([z(ROPE_LO), -ones, ones, z(LANES - ROPE_HI)]).reshape(1, LANES),
        "wkv": wkv.astype(BF16),
        "kgain": jnp.concatenate([kn_nope, z(LANES - D_NOPE)]).reshape(1, LANES),
    }


def _mla_prompt_kernel(q_ref, k_ref, v_ref, o_ref, *, seq):
    qb = min(MLA_QBLOCK, seq)
    lo = lax.broadcasted_iota(jnp.int32, (qb, LANES), 1) < D_V
    rchunk = lax.broadcasted_iota(jnp.int32, (qb, qb), 0) >> 6
    cchunk = lax.broadcasted_iota(jnp.int32, (qb, qb), 1) >> 6
    diag_visible = cchunk <= rchunk
    for i in range(seq // qb):
        tk = qb * (i + 1)
        outs = []
        for h in range(2):
            q = q_ref[0, i * qb:(i + 1) * qb, h * LANES:(h + 1) * LANES]
            k = k_ref[0, 0:tk, h * LANES:(h + 1) * LANES]
            s = lax.dot_general(q, k, _NT, preferred_element_type=F32) * MLA_SCALE
            s_d = jnp.where(diag_visible, s[:, tk - qb:], NEG_INF)
            m = jnp.max(s_d, axis=-1, keepdims=True)
            if i > 0:
                s_o = s[:, :tk - qb]
                m = jnp.maximum(m, jnp.max(s_o, axis=-1, keepdims=True))
                e = jnp.concatenate([jnp.exp(s_o - m), jnp.exp(s_d - m)], axis=1)
            else:
                e = jnp.exp(s_d - m)
            den = jnp.sum(e, axis=-1, keepdims=True)
            outs.append(_dot(e.astype(BF16), v_ref[0, 0:tk, :]) / den)
        o_ref[0, i * qb:(i + 1) * qb, :] = jnp.where(lo, outs[0], outs[1]).astype(BF16)


def _mla_prompt(q, k, v):
    nb, seq, _ = q.shape
    return pl.pallas_call(
        functools.partial(_mla_prompt_kernel, seq=seq),
        grid=(nb, D_HEADS // 2),
        in_specs=[pl.BlockSpec((1, seq, 2 * LANES), lambda b, p: (b, 0, p)),
                  pl.BlockSpec((1, seq, 2 * LANES), lambda b, p: (b, 0, p)),
                  pl.BlockSpec((1, seq, LANES), lambda b, p: (b, 0, p))],
        out_specs=pl.BlockSpec((1, seq, LANES), lambda b, p: (b, 0, p)),
        out_shape=jax.ShapeDtypeStruct((nb, seq, D_HEADS * D_V), BF16),
        compiler_params=_cparams("arbitrary", "arbitrary"),
        name="mla_prompt",
    )(q, k, v)


def _mla_sample_kernel(q_ref, kc_ref, vc_ref, kn_ref, vn_ref, o_ref):
    nq = q_ref.shape[1]
    lo = lax.broadcasted_iota(jnp.int32, (nq, LANES), 1) < D_V
    outs = []
    for h in range(2):
        cols = slice(h * LANES, (h + 1) * LANES)
        q = q_ref[0, :, cols]
        s1 = lax.dot_general(q, kc_ref[0, :, cols], _NT, preferred_element_type=F32) * MLA_SCALE
        s2 = lax.dot_general(q, kn_ref[0, :, cols], _NT, preferred_element_type=F32) * MLA_SCALE
        m = jnp.maximum(jnp.max(s1, axis=-1, keepdims=True), jnp.max(s2, axis=-1, keepdims=True))
        e1 = jnp.exp(s1 - m)
        e2 = jnp.exp(s2 - m)
        den = jnp.sum(e1, axis=-1, keepdims=True) + jnp.sum(e2, axis=-1, keepdims=True)
        outs.append((_dot(e1.astype(BF16), vc_ref[0]) + _dot(e2.astype(BF16), vn_ref[0])) / den)
    o_ref[0] = jnp.where(lo, outs[0], outs[1]).astype(BF16)


def _mla_sample(q, kc, vc, kn, vn):
    nb, nq, _ = q.shape
    past = kc.shape[1]
    two = lambda r: pl.BlockSpec((1, r, 2 * LANES), lambda b, p: (b, 0, p))
    one = lambda r: pl.BlockSpec((1, r, LANES), lambda b, p: (b, 0, p))
    return pl.pallas_call(
        _mla_sample_kernel,
        grid=(nb, D_HEADS // 2),
        in_specs=[two(nq), two(past), one(past), two(nq), one(nq)],
        out_specs=one(nq),
        out_shape=jax.ShapeDtypeStruct((nb, nq, D_HEADS * D_V), BF16),
        compiler_params=_cparams("arbitrary", "arbitrary"),
        name="mla_sample",
    )(q, kc, vc, kn, vn)


def _even_layer(xp, xs, nb, seq, ndb, dseq, norm_g, w_in, w_out, a_qn, a_kn, a_sinks, b_qn, b_kn, b_rel,
                t5_table, ck_a, cv_a, ck_b, cv_b, ffn2):
    perm = _pair_perm()
    w_in_p = jnp.concatenate([w_in[:, :A_Q][:, perm], w_in[:, A_Q:]], axis=1).astype(BF16)
    woa = w_out[:A_Q][perm].astype(BF16)
    wob = w_out[A_Q:].astype(BF16)
    ones = lambda n: jnp.ones((n,), F32)
    scale = HEAD_DIM ** -0.5
    gain_row = jnp.concatenate([jnp.tile(a_qn, A_HEADS) * scale, jnp.tile(a_kn, A_KV_HEADS), ones(A_KV),
                                jnp.tile(b_qn, B_HEADS) * scale, jnp.tile(b_kn, B_HEADS), ones(B_QKV)]
                               ).reshape(1, EVEN_IN).astype(F32)
    bias_a = _bias_a(t5_table)
    bias_b = _bias_b(b_rel)
    sink = jnp.repeat(a_sinks.astype(F32), PAIR_ROWS).reshape(A_HEADS * PAIR_ROWS, 1)
    la, lb = min(WINDOW, seq), min(B_REACH, seq)

    aq, ak, av, bq, bk, bv, cak, cav, cbk, cbv = _even_inproj(xp, norm_g, w_in_p, gain_row, nb, la, lb)
    r3 = lambda a: a.reshape(nb, seq, a.shape[-1])
    ya = _attn_a(r3(aq), r3(ak), r3(av), bias_a, sink, WINDOW, True)
    yb = _attn_b(r3(bq), r3(bk), r3(bv), bias_b, B_REACH, True)
    xp = _ffn(xp, *ffn2, out_proj=(ya.reshape(nb * seq, A_Q), yb.reshape(nb * seq, B_QKV), woa, wob))
    st_p = (cak.reshape(nb, la, A_KV_HEADS, HEAD_DIM), cav.reshape(nb, la, A_KV_HEADS, HEAD_DIM),
            cbk.reshape(nb, lb, B_HEADS, HEAD_DIM), cbv.reshape(nb, lb, B_HEADS, HEAD_DIM))

    ts = ndb * dseq
    aq, ak, av, bq, bk, bv, nak, nav, nbk, nbv = _even_inproj(xs, norm_g, w_in_p, gain_row, 1, ts, ts)
    pad_q = lambda a: jnp.pad(a.reshape(ndb, dseq, a.shape[-1]), ((0, 0), (0, PAIR_ROWS - dseq), (0, 0)))

    def window(cache, new):
        w = cache.shape[-2] * cache.shape[-1]
        full = jnp.concatenate([cache.reshape(ndb, -1, w), new.reshape(ndb, dseq, w)], axis=1)
        buf = jnp.pad(full, ((0, 0), (0, PAIR_ROWS - dseq), (0, 0))).astype(BF16)
        return full[:, dseq:].reshape(cache.shape), buf

    st_ak, kbuf_a = window(ck_a, nak)
    st_av, vbuf_a = window(cv_a, nav)
    st_bk, kbuf_b = window(ck_b, nbk)
    st_bv, vbuf_b = window(cv_b, nbv)
    ya = _attn_a(pad_q(aq), kbuf_a, vbuf_a, bias_a, sink, WINDOW, False)[:, :dseq]
    yb = _attn_b(pad_q(bq), kbuf_b, vbuf_b, bias_b, B_REACH, False)[:, :dseq]
    xs = _ffn(xs, *ffn2, out_proj=(ya.reshape(ts, A_Q), yb.reshape(ts, B_QKV), woa, wob))
    return xp, xs, st_p, (st_ak, st_av, st_bk, st_bv)


def _odd_layer(xp, xs, nb, seq, ndb, dseq, past, norm_g, ow, w_out, conv_prev, c_ckv, c_kpe, ffn2):
    woc = w_out[:C_WIDTH].astype(BF16)
    wod = w_out[C_WIDTH:].astype(BF16)

    zero_init = jnp.zeros((nb, 8, C_WIDTH), F32)
    yc, q, k, v, ckv, kpe, cs = _odd_inproj(xp, norm_g, ow, zero_init, nb, 0)
    r3 = lambda a: a.reshape(nb, seq, a.shape[-1])
    yd = _mla_prompt(r3(q), r3(k), r3(v))
    xp = _ffn(xp, *ffn2, out_proj=(yc, yd.reshape(nb * seq, D_HEADS * D_V), woc, wod))
    st_p = (cs, ckv.reshape(nb, seq, D_KV_LORA), kpe.reshape(nb, seq, D_ROPE))

    ts = ndb * dseq
    init = jnp.pad(conv_prev.astype(F32), ((0, 0), (8 - (CONV_W - 1), 0), (0, 0)))
    yc, q, kn, vn, ckv, kpe, cs = _odd_inproj(xs, norm_g, ow, init, ndb, past)
    kpe_blk = jnp.pad(c_kpe.reshape(ndb * past, D_ROPE), ((0, 0), (ROPE_LO, LANES - ROPE_HI)))
    kc, vc = _kv_expand(c_ckv.reshape(ndb * past, D_KV_LORA), kpe_blk, ow)
    s3 = lambda a: a.reshape(ndb, dseq, a.shape[-1])
    yd = _mla_sample(s3(q), kc.reshape(ndb, past, -1), vc.reshape(ndb, past, -1), s3(kn), s3(vn))
    xs = _ffn(xs, *ffn2, out_proj=(yc, yd.reshape(ts, D_HEADS * D_V), woc, wod))
    st_s = (cs, ckv.reshape(ndb, dseq, D_KV_LORA), kpe.reshape(ndb, dseq, D_ROPE))
    return xp, xs, st_p, st_s


def kernel(x_prompt, x_sample, cache_a_k, cache_a_v, cache_b_k, cache_b_v, state_c_conv, cache_d_ckv, cache_d_kpe, ff1_norm, ff1_w_gu, ff1_w_down, mix_norm, ff2_norm, ff2_w_gu, ff2_w_down, t5_bias_table, ev_w_in, ev_w_out, a_q_norm, a_k_norm, a_sinks, b_q_norm, b_k_norm, b_rel_bias, od_w_in, od_w_out, c_conv_w, d_q_a_norm, d_w_q_b, d_kv_a_norm, d_w_kv_b, d_q_nope_norm, d_q_rope_norm, d_k_nope_norm, d_k_rope_norm):
    nb, seq, _ = x_prompt.shape
    ndb, dseq, _ = x_sample.shape
    past = cache_d_ckv.shape[2]
    depth = ff1_norm.shape[0]
    assert seq % TOKEN_TILE == 0 and dseq == CHUNK and past % CHUNK == 0
    xp = x_prompt.reshape(nb * seq, D_MODEL)
    xs = x_sample.reshape(ndb * dseq, D_MODEL)
    even_p, even_s, odd_p, odd_s = [], [], [], []
    for l in range(depth):
        i = l // 2
        ff1 = _ffn_weights(ff1_w_gu[l], ff1_w_down[l])
        ffn2 = (ff2_norm[l],) + _ffn_weights(ff2_w_gu[l], ff2_w_down[l])
        xp = _ffn(xp, ff1_norm[l], *ff1)
        xs = _ffn(xs, ff1_norm[l], *ff1)
        if l % 2 == 0:
            xp, xs, sp, ss = _even_layer(
                xp, xs, nb, seq, ndb, dseq, mix_norm[l], ev_w_in[i], ev_w_out[i], a_q_norm[i], a_k_norm[i],
                a_sinks[i], b_q_norm[i], b_k_norm[i], b_rel_bias[i], t5_bias_table,
                cache_a_k[i], cache_a_v[i], cache_b_k[i], cache_b_v[i], ffn2)
            even_p.append(sp)
            even_s.append(ss)
        else:
            ow = _odd_weights(od_w_in[i], c_conv_w[i], d_q_a_norm[i], d_w_q_b[i], d_kv_a_norm[i], d_w_kv_b[i],
                              d_q_nope_norm[i], d_q_rope_norm[i], d_k_nope_norm[i], d_k_rope_norm[i])
            xp, xs, sp, ss = _odd_layer(xp, xs, nb, seq, ndb, dseq, past, mix_norm[l], ow, od_w_out[i],
                                        state_c_conv[i], cache_d_ckv[i], cache_d_kpe[i], ffn2)
            odd_p.append(sp)
            odd_s.append(ss)
    stack = lambda group, j: jnp.stack([g[j] for g in group])
    return (xp.reshape(nb, seq, D_MODEL), xs.reshape(ndb, dseq, D_MODEL),
            stack(even_p, 0), stack(even_p, 1), stack(even_p, 2), stack(even_p, 3),
            stack(odd_p, 0), stack(odd_p, 1), stack(odd_p, 2),
            stack(even_s, 0), stack(even_s, 1), stack(even_s, 2), stack(even_s, 3),
            stack(odd_s, 0), stack(odd_s, 1), stack(odd_s, 2))
```

```python
import functools
import math

import jax
import jax.numpy as jnp
from jax import lax
from jax.experimental import pallas as pl
from jax.experimental.pallas import tpu as pltpu

F32 = jnp.float32
BF16 = jnp.bfloat16

D_MODEL = 1024
CHUNK = 64
HEAD_DIM = 64
EPS = 1e-6
A_HEADS = 8
A_KV_HEADS = 2
WINDOW = 128
T5_BUCKETS = 32
T5_MAX_DIST = 128
B_HEADS = 8
B_REACH = 512
B_MAX_REL = 128
C_WIDTH = 512
CONV_W = 3
D_HEADS = 8
D_Q_LORA = 256
D_KV_LORA = 128
D_NOPE = 64
D_ROPE = 32
D_V = 64
ROPE_THETA = 10000.0
FFN_DIM = 2816
A_Q = A_HEADS * HEAD_DIM
A_KV = A_KV_HEADS * HEAD_DIM
B_QKV = B_HEADS * HEAD_DIM
EVEN_IN = A_Q + 2 * A_KV + 3 * B_QKV

LANES = 128
TOKEN_TILE = 512
MXU_COLS = 256
FFN_CHUNK = MXU_COLS
PAIR_ROWS = 2 * CHUNK
MLA_QBLOCK = 256
VMEM_LIMIT_BYTES = 56 * 1024 * 1024
MLA_SCALE = (D_NOPE + D_ROPE) ** -0.5
NEG_INF = float("-inf")

_NT = (((1,), (1,)), ((), ()))


def _cparams(*sem):
    return pltpu.CompilerParams(dimension_semantics=sem, vmem_limit_bytes=VMEM_LIMIT_BYTES)


def _resident(shape):
    zeros = (0,) * len(shape)
    return pl.BlockSpec(shape, lambda *_: zeros, pipeline_mode=pl.Buffered(1))


def _rms(x, g):
    ms = jnp.mean(x * x, axis=-1, keepdims=True)
    return (x * lax.rsqrt(ms + EPS)) * g


def _dot(a, b):
    return jnp.dot(a, b, preferred_element_type=F32)


def _ffn_kernel(*refs, n_chunks, fused_out):
    if fused_out:
        (x_ref, ya_ref, yb_ref, woa_ref, wob_ref, g_ref, wg_ref, wu_ref, wd_ref,
         o_ref, h_scr, a_scr) = refs
        y = _dot(ya_ref[...], woa_ref[...]) + _dot(yb_ref[...], wob_ref[...])
        o_ref[...] = x_ref[...] + y
        res_ref = o_ref
    else:
        x_ref, g_ref, wg_ref, wu_ref, wd_ref, o_ref, h_scr, a_scr = refs
        res_ref = x_ref
    h_scr[...] = _rms(res_ref[...], g_ref[...]).astype(BF16)
    for j in range(n_chunks):
        h = h_scr[...]
        g = _dot(h, wg_ref[j])
        u = _dot(h, wu_ref[j])
        a_scr[:, j * FFN_CHUNK:(j + 1) * FFN_CHUNK] = ((g * jax.nn.sigmoid(g)) * u).astype(BF16)
    o_ref[...] = res_ref[...] + 0.5 * _dot(a_scr[...], wd_ref[...])


def _ffn(x, norm_g, wg3, wu3, wd3, out_proj=None):
    t = x.shape[0]
    tm = min(TOKEN_TILE, t)
    n_chunks = wg3.shape[0]
    row = lambda w: pl.BlockSpec((tm, w), lambda i: (i, 0))
    in_specs = [row(D_MODEL)]
    args = [x]
    if out_proj is not None:
        ya, yb, woa, wob = out_proj
        in_specs += [row(ya.shape[1]), row(yb.shape[1]), _resident(woa.shape), _resident(wob.shape)]
        args += [ya, yb, woa, wob]
    in_specs += [_resident((1, D_MODEL)), _resident(wg3.shape), _resident(wu3.shape), _resident(wd3.shape)]
    args += [norm_g.reshape(1, D_MODEL), wg3, wu3, wd3]
    return pl.pallas_call(
        functools.partial(_ffn_kernel, n_chunks=n_chunks, fused_out=out_proj is not None),
        grid=(t // tm,),
        in_specs=in_specs,
        out_specs=row(D_MODEL),
        out_shape=jax.ShapeDtypeStruct((t, D_MODEL), F32),
        scratch_shapes=[pltpu.VMEM((tm, D_MODEL), BF16), pltpu.VMEM((tm, FFN_DIM), BF16)],
        compiler_params=_cparams("arbitrary"),
        name="ffn_out" if out_proj is not None else "ffn",
    )(*args)


def _ffn_weights(w_gu, w_down):
    n = FFN_DIM // FFN_CHUNK
    wg = w_gu[:, :FFN_DIM].astype(BF16).reshape(D_MODEL, n, FFN_CHUNK).transpose(1, 0, 2)
    wu = w_gu[:, FFN_DIM:].astype(BF16).reshape(D_MODEL, n, FFN_CHUNK).transpose(1, 0, 2)
    return wg, wu, w_down.astype(BF16)


def _norm_halves(blk, gain):
    sq = blk * blk
    lo = lax.broadcasted_iota(jnp.int32, blk.shape, 1) < HEAD_DIM
    s_lo = jnp.sum(jnp.where(lo, sq, 0.0), axis=-1, keepdims=True)
    s_hi = jnp.sum(jnp.where(lo, 0.0, sq), axis=-1, keepdims=True)
    inv = jnp.where(lo, lax.rsqrt(s_lo * (1.0 / HEAD_DIM) + EPS), lax.rsqrt(s_hi * (1.0 / HEAD_DIM) + EPS))
    return (blk * inv) * gain


def _even_inproj_kernel(x_ref, g_ref, w_ref, gain_ref,
                        aq_ref, ak_ref, av_ref, bq_ref, bk_ref, bv_ref,
                        cak_ref, cav_ref, cbk_ref, cbv_ref, h_scr, z_scr, *, tm, tpb, rows_a, rows_b):
    i = pl.program_id(0)
    last = (i % tpb) == (tpb - 1)
    h_scr[...] = _rms(x_ref[...], g_ref[...]).astype(BF16)

    def lanes(c):
        return slice(c * LANES, (c + 1) * LANES)

    def put_cache(ref, rows, c, val):
        @pl.when(last)
        def _():
            ref[0, :, lanes(c)] = val[tm - rows:, :]

    def emit(c):
        blk = z_scr[:, lanes(c)]
        gain = gain_ref[:, lanes(c)]
        if c < 4:
            aq_ref[:, lanes(c)] = _norm_halves(blk, gain).astype(BF16)
        elif c == 4:
            kn = _norm_halves(blk, gain)
            ak_ref[...] = kn.astype(BF16)
            put_cache(cak_ref, rows_a, 0, kn)
        elif c == 5:
            av_ref[...] = blk.astype(BF16)
            put_cache(cav_ref, rows_a, 0, blk)
        elif c < 10:
            bq_ref[:, lanes(c - 6)] = _norm_halves(blk, gain).astype(BF16)
        elif c < 14:
            kn = _norm_halves(blk, gain)
            bk_ref[:, lanes(c - 10)] = kn.astype(BF16)
            put_cache(cbk_ref, rows_b, c - 10, kn)
        else:
            bv_ref[:, lanes(c - 14)] = blk.astype(BF16)
            put_cache(cbv_ref, rows_b, c - 14, blk)

    for grp in range(EVEN_IN // MXU_COLS):
        cols = slice(grp * MXU_COLS, (grp + 1) * MXU_COLS)
        z_scr[:, cols] = _dot(h_scr[...], w_ref[:, cols])
        emit(2 * grp)
        emit(2 * grp + 1)


def _even_inproj(x, norm_g, w_in, gain_row, n_batch, rows_a, rows_b):
    t = x.shape[0]
    tm = min(TOKEN_TILE, t)
    tpb = t // n_batch // tm
    assert tpb * tm * n_batch == t and rows_a <= tm and rows_b <= tm
    row = lambda w: pl.BlockSpec((tm, w), lambda i: (i, 0))
    cache = lambda r, w: pl.BlockSpec((1, r, w), lambda i: (i // tpb, 0, 0))
    bshape = lambda w: jax.ShapeDtypeStruct((t, w), BF16)
    cshape = lambda r, w: jax.ShapeDtypeStruct((n_batch, r, w), F32)
    return pl.pallas_call(
        functools.partial(_even_inproj_kernel, tm=tm, tpb=tpb, rows_a=rows_a, rows_b=rows_b),
        grid=(t // tm,),
        in_specs=[row(D_MODEL), _resident((1, D_MODEL)), _resident(w_in.shape), _resident(gain_row.shape)],
        out_specs=[row(A_Q), row(A_KV), row(A_KV), row(B_QKV), row(B_QKV), row(B_QKV),
                   cache(rows_a, A_KV), cache(rows_a, A_KV), cache(rows_b, B_QKV), cache(rows_b, B_QKV)],
        out_shape=[bshape(A_Q), bshape(A_KV), bshape(A_KV), bshape(B_QKV), bshape(B_QKV), bshape(B_QKV),
                   cshape(rows_a, A_KV), cshape(rows_a, A_KV), cshape(rows_b, B_QKV), cshape(rows_b, B_QKV)],
        scratch_shapes=[pltpu.VMEM((tm, D_MODEL), BF16), pltpu.VMEM((tm, EVEN_IN), F32)],
        compiler_params=_cparams("arbitrary"),
        name="even_inproj",
    )(x, norm_g.reshape(1, D_MODEL), w_in, gain_row)


def _pair_perm():
    idx = []
    for p in range(4):
        for half in range(2):
            head = p + 4 * half
            idx.extend(range(head * HEAD_DIM, (head + 1) * HEAD_DIM))
    return jnp.asarray(idx, dtype=jnp.int32)


def _fill_padded(buf, src_ref, ctx):
    buf[0:ctx, :] = jnp.zeros((ctx, buf.shape[1]), buf.dtype)
    buf[ctx:, :] = src_ref[0]


def _windows(k_ref, v_ref, pad_scr, ctx, pad_front):
    win = ctx + PAIR_ROWS
    if pad_front:
        kbuf, vbuf = pad_scr
        _fill_padded(kbuf, k_ref, ctx)
        _fill_padded(vbuf, v_ref, ctx)
        return (lambda r0: kbuf[pl.ds(r0, win), :]), (lambda r0: vbuf[pl.ds(r0, win), :])
    return (lambda r0: k_ref[0, pl.ds(r0, win), :]), (lambda r0: v_ref[0, pl.ds(r0, win), :])


def _fold_lanes(x, op):
    acc = x[:, :LANES]
    for c in range(1, x.shape[1] // LANES):
        acc = op(acc, x[:, c * LANES:(c + 1) * LANES])
    return acc


def _row_stat(reduce_fn, x):
    return jnp.broadcast_to(reduce_fn(x, axis=-1, keepdims=True), x.shape)


def _run_pairs(pair, n_steps, n_front):
    for t in range(n_front):
        pair(t * PAIR_ROWS, t % 2, True)
    n_rest = n_steps - n_front

    def two(it, carry):
        r0 = pl.multiple_of((n_front + 2 * it) * PAIR_ROWS, PAIR_ROWS)
        pair(r0, 0, False)
        pair(r0 + PAIR_ROWS, 1, False)
        return carry

    if n_rest // 2 > 0:
        lax.fori_loop(0, n_rest // 2, two, 0)
    if n_rest % 2:
        pair((n_steps - 1) * PAIR_ROWS, 0, False)


def _attn_a_kernel(q_ref, k_ref, v_ref, bias_ref, sink_ref, o_ref, s_scr, e_scr, *pad_scr, nq, ctx, pad_front):
    win = ctx + PAIR_ROWS
    kwin, vwin = _windows(k_ref, v_ref, pad_scr, ctx, pad_front)
    lo = lax.broadcasted_iota(jnp.int32, (PAIR_ROWS, LANES), 1) < HEAD_DIM

    def pair(r0, slot, masked):
        q_lo, q_hi = [], []
        for p in range(4):
            qp = q_ref[0, pl.ds(r0, PAIR_ROWS), p * LANES:(p + 1) * LANES]
            zero = jnp.zeros_like(qp)
            q_lo.append(jnp.where(lo, qp, zero))
            q_hi.append(jnp.where(lo, zero, qp))
        q2 = jnp.concatenate(q_lo + q_hi, axis=0)
        s_scr[slot] = lax.dot_general(q2, kwin(r0), _NT, preferred_element_type=F32)
        for head in range(A_HEADS):
            rows = slice(head * PAIR_ROWS, (head + 1) * PAIR_ROWS)
            s = s_scr[slot, rows, :] + bias_ref[rows, :]
            if masked:
                col = lax.broadcasted_iota(jnp.int32, (PAIR_ROWS, win), 1)
                s = jnp.where(col + r0 >= ctx, s, NEG_INF)
            sk = sink_ref[rows, :]
            m = jnp.maximum(_row_stat(jnp.max, _fold_lanes(s, jnp.maximum)), sk)
            e = [jnp.exp(s[:, c * LANES:(c + 1) * LANES] - m) for c in range(win // LANES)]
            den = _row_stat(jnp.sum, functools.reduce(jnp.add, e)) + jnp.exp(sk - m)
            inv = 1.0 / den
            for c in range(win // LANES):
                e_scr[slot, rows, c * LANES:(c + 1) * LANES] = (e[c] * inv).astype(BF16)
        o2 = _dot(e_scr[slot], vwin(r0))
        half = 4 * PAIR_ROWS
        for p in range(4):
            o = jnp.where(lo, o2[p * PAIR_ROWS:(p + 1) * PAIR_ROWS],
                          o2[half + p * PAIR_ROWS:half + (p + 1) * PAIR_ROWS])
            o_ref[0, pl.ds(r0, PAIR_ROWS), p * LANES:(p + 1) * LANES] = o.astype(BF16)

    _run_pairs(pair, nq // PAIR_ROWS, ctx // PAIR_ROWS if pad_front else 0)


def _attn_a(q, k, v, bias, sink, ctx, pad_front):
    nb, nq, _ = q.shape
    nk = k.shape[1]
    assert nq % PAIR_ROWS == 0 and nk == (nq if pad_front else ctx + nq)
    per_b = lambda r, w: pl.BlockSpec((1, r, w), lambda b: (b, 0, 0))
    rows, win = A_HEADS * PAIR_ROWS, ctx + PAIR_ROWS
    scratch = [pltpu.VMEM((2, rows, win), F32), pltpu.VMEM((2, rows, win), BF16)]
    if pad_front:
        scratch += [pltpu.VMEM((ctx + nq, A_KV), BF16)] * 2
    return pl.pallas_call(
        functools.partial(_attn_a_kernel, nq=nq, ctx=ctx, pad_front=pad_front),
        grid=(nb,),
        in_specs=[per_b(nq, A_Q), per_b(nk, A_KV), per_b(nk, A_KV), _resident(bias.shape), _resident(sink.shape)],
        out_specs=per_b(nq, A_Q),
        out_shape=jax.ShapeDtypeStruct((nb, nq, A_Q), BF16),
        scratch_shapes=scratch,
        compiler_params=_cparams("arbitrary"),
        name="attn_a",
    )(q, k, v, bias, sink)


def _attn_b_kernel(q_ref, k_ref, v_ref, bias_ref, o_ref, s_scr, e_scr, r_scr, *pad_scr, nq, ctx, pad_front):
    win = ctx + PAIR_ROWS
    kwin, vwin = _windows(k_ref, v_ref, pad_scr, ctx, pad_front)
    lo = lax.broadcasted_iota(jnp.int32, (PAIR_ROWS, LANES), 1) < HEAD_DIM

    def pair(r0, slot, masked):
        qp = q_ref[0, pl.ds(r0, PAIR_ROWS), :]
        zero = jnp.zeros_like(qp)
        q2 = jnp.concatenate([jnp.where(lo, qp, zero), jnp.where(lo, zero, qp)], axis=0)
        s_scr[slot] = lax.dot_general(q2, kwin(r0), _NT, preferred_element_type=F32)
        for rb in range(2 * PAIR_ROWS // CHUNK):
            rows = slice(rb * CHUNK, (rb + 1) * CHUNK)
            s = s_scr[slot, rows, :] + bias_ref[0, rows, :]
            if masked:
                col = lax.broadcasted_iota(jnp.int32, (CHUNK, win), 1)
                s = jnp.where(col + r0 >= ctx, s, NEG_INF)
            m = _row_stat(jnp.max, _fold_lanes(s, jnp.maximum))
            e = [jnp.exp(s[:, c * LANES:(c + 1) * LANES] - m) for c in range(win // LANES)]
            for c in range(win // LANES):
                e_scr[slot, rows, c * LANES:(c + 1) * LANES] = e[c].astype(BF16)
            r_scr[slot, rows, :] = 1.0 / _row_stat(jnp.sum, functools.reduce(jnp.add, e))
        o2 = _dot(e_scr[slot], vwin(r0)) * r_scr[slot]
        o = jnp.where(lo, o2[:PAIR_ROWS], o2[PAIR_ROWS:])
        o_ref[0, pl.ds(r0, PAIR_ROWS), :] = o.astype(BF16)

    _run_pairs(pair, nq // PAIR_ROWS, ctx // PAIR_ROWS if pad_front else 0)


def _attn_b(q, k, v, bias, ctx, pad_front):
    nb, nq, _ = q.shape
    nk = k.shape[1]
    assert nq % PAIR_ROWS == 0 and nk == (nq if pad_front else ctx + nq)
    blk = lambda r: pl.BlockSpec((1, r, LANES), lambda b, p: (b, 0, p))
    rows, win = 2 * PAIR_ROWS, ctx + PAIR_ROWS
    scratch = [pltpu.VMEM((2, rows, win), F32), pltpu.VMEM((2, rows, win), BF16),
               pltpu.VMEM((2, rows, LANES), F32)]
    if pad_front:
        scratch += [pltpu.VMEM((ctx + nq, LANES), BF16)] * 2
    return pl.pallas_call(
        functools.partial(_attn_b_kernel, nq=nq, ctx=ctx, pad_front=pad_front),
        grid=(nb, B_HEADS // 2),
        in_specs=[blk(nq), blk(nk), blk(nk),
                  pl.BlockSpec((1,) + bias.shape[1:], lambda b, p: (p, 0, 0))],
        out_specs=blk(nq),
        out_shape=jax.ShapeDtypeStruct((nb, nq, B_QKV), BF16),
        scratch_shapes=scratch,
        compiler_params=_cparams("arbitrary", "arbitrary"),
        name="attn_b",
    )(q, k, v, bias)


def _t5_bucket(rel):
    nb = T5_BUCKETS // 2
    max_exact = nb // 2
    n = -rel
    ret = jnp.where(n < 0, nb, 0)
    n = jnp.abs(n)
    nf = jnp.maximum(n, 1).astype(F32)
    large = max_exact + (jnp.log(nf / max_exact) / math.log(T5_MAX_DIST / max_exact)
                         * (nb - max_exact)).astype(jnp.int32)
    large = jnp.minimum(large, nb - 1)
    return ret + jnp.where(n < max_exact, n, large)


def _band_tables(ctx, value_of_rel):
    n_cols = ctx + PAIR_ROWS
    period = n_cols + PAIR_ROWS - 1
    d = jnp.concatenate([jnp.arange(0, n_cols), jnp.arange(-(PAIR_ROWS - 1), 0)])
    ext = value_of_rel(d - ctx)
    heads = ext.shape[0]
    flat = jnp.tile(ext, (1, PAIR_ROWS))[:, :PAIR_ROWS * (period - 1)]
    tb = flat.reshape(heads, PAIR_ROWS, period - 1)[:, :, :n_cols]
    i = jnp.arange(PAIR_ROWS)[:, None]
    j = jnp.arange(n_cols)[None, :]
    visible = jnp.where(i < CHUNK, j < ctx + CHUNK, j >= CHUNK)
    return jnp.where(visible[None], tb, NEG_INF)


def _bias_a(t5_table):
    tb = _band_tables(WINDOW, lambda rel: t5_table.astype(F32)[_t5_bucket(rel)].T)
    return tb.reshape(A_HEADS * PAIR_ROWS, WINDOW + PAIR_ROWS)


def _bias_b(b_rel):
    tb = _band_tables(B_REACH, lambda rel: b_rel.astype(F32)[:, jnp.clip(rel, -B_MAX_REL, B_MAX_REL) + B_MAX_REL])
    return tb.reshape(B_HEADS // 2, 2 * PAIR_ROWS, B_REACH + PAIR_ROWS)


ODD_IN_PAD = 2048
ROPE_LO = D_NOPE
ROPE_HI = D_NOPE + D_ROPE
ROPE_HALF = D_ROPE // 2


def _lane_masks(shape):
    lane = lax.broadcasted_iota(jnp.int32, shape, 1)
    return lane < ROPE_LO, (lane >= ROPE_LO) & (lane < ROPE_HI), lane < ROPE_LO + ROPE_HALF


def _rope(y, cosf, sinf, first_half):
    rot = jnp.where(first_half, pltpu.roll(y, LANES - ROPE_HALF, 1), pltpu.roll(y, ROPE_HALF, 1))
    return y * cosf + rot * sinf


def _expand_kv(ckv_b, kpe_blk, wkv_ref, kgain_ref, kv_scr, k_ref, v_ref):
    k_cols = D_HEADS * LANES
    for grp in range(kv_scr.shape[1] // MXU_COLS):
        cols = slice(grp * MXU_COLS, (grp + 1) * MXU_COLS)
        kv_scr[:, cols] = _dot(ckv_b, wkv_ref[:, cols])
        if cols.start < k_cols:
            for h in (2 * grp, 2 * grp + 1):
                kb = kv_scr[:, h * LANES:(h + 1) * LANES]
                ms = jnp.sum(kb * kb, axis=-1, keepdims=True) * (1.0 / D_NOPE)
                kn = (kb * lax.rsqrt(ms + EPS)) * kgain_ref[...]
                k_ref[:, h * LANES:(h + 1) * LANES] = (kn + kpe_blk).astype(BF16)
        else:
            v_ref[:, cols.start - k_cols:cols.stop - k_cols] = kv_scr[:, cols].astype(BF16)


def _odd_inproj_kernel(x_ref, g_ref, w_ref, cinit_ref, convw_ref, qan_ref, wqb_ref, qgain_ref,
                       kvan_ref, krgain_ref, invf_ref, sgn_ref, wkv_ref, kgain_ref,
                       yc_ref, q_ref, k_ref, v_ref, ckv_ref, kpe_ref, cs_ref,
                       h_scr, z_scr, uext_scr, qn_scr, q_scr, kv_scr, rot_scr, *, tm, tpb, pos0):
    i = pl.program_id(0)
    tile = i % tpb
    h_scr[...] = _rms(x_ref[...], g_ref[...]).astype(BF16)

    def proj(c0, c1):
        z_scr[:, c0:c1] = _dot(h_scr[...], w_ref[:, c0:c1])

    @pl.when(tile == 0)
    def _():
        uext_scr[0:8, :] = cinit_ref[0]

    for grp in range(C_WIDTH // MXU_COLS):
        c0 = grp * MXU_COLS
        cols = slice(c0, c0 + MXU_COLS)
        for base in (0, C_WIDTH, 2 * C_WIDTH):
            proj(base + c0, base + c0 + MXU_COLS)
        u = z_scr[:, C_WIDTH + c0:C_WIDTH + c0 + MXU_COLS] * z_scr[:, 2 * C_WIDTH + c0:2 * C_WIDTH + c0 + MXU_COLS]
        uext_scr[8:8 + tm, cols] = u
        yconv = convw_ref[0:1, cols] * uext_scr[6:6 + tm, cols]
        yconv = yconv + convw_ref[1:2, cols] * uext_scr[7:7 + tm, cols]
        yconv = yconv + convw_ref[2:3, cols] * uext_scr[8:8 + tm, cols]
        yc_ref[:, cols] = (z_scr[:, cols] * yconv).astype(BF16)

    @pl.when(tile == tpb - 1)
    def _():
        cs_ref[0] = uext_scr[tm + 6:tm + 8, :]

    uext_scr[0:8, :] = uext_scr[tm:tm + 8, :]

    off = pl.multiple_of(tile * tm, tm)

    @pl.when(i < tpb)
    def _():
        row = lax.broadcasted_iota(jnp.int32, (tm, LANES), 0)
        ang = (row + (tile * tm + pos0)).astype(F32) * invf_ref[...]
        rot_scr[0, pl.ds(off, tm), :] = jnp.cos(ang)
        rot_scr[1, pl.ds(off, tm), :] = jnp.sin(ang) * sgn_ref[...]

    cosf = rot_scr[0, pl.ds(off, tm), :]
    sinf = rot_scr[1, pl.ds(off, tm), :]
    nope, ropem, first_half = _lane_masks((tm, LANES))

    proj(1536, 1792)
    qn_scr[...] = _rms(z_scr[:, 1536:1792], qan_ref[...]).astype(BF16)
    for grp in range(D_HEADS * LANES // MXU_COLS):
        cols = slice(grp * MXU_COLS, (grp + 1) * MXU_COLS)
        q_scr[:, cols] = _dot(qn_scr[...], wqb_ref[:, cols])
        for h in (2 * grp, 2 * grp + 1):
            blk = q_scr[:, h * LANES:(h + 1) * LANES]
            sq = blk * blk
            s_n = jnp.sum(jnp.where(nope, sq, 0.0), axis=-1, keepdims=True)
            s_r = jnp.sum(jnp.where(ropem, sq, 0.0), axis=-1, keepdims=True)
            inv = jnp.where(nope, lax.rsqrt(s_n * (1.0 / D_NOPE) + EPS), lax.rsqrt(s_r * (1.0 / D_ROPE) + EPS))
            y = _rope((blk * inv) * qgain_ref[...], cosf, sinf, first_half)
            q_ref[:, h * LANES:(h + 1) * LANES] = y.astype(BF16)

    proj(1792, 2048)
    kb = z_scr[:, 1920:2048]
    ms = jnp.sum(kb * kb, axis=-1, keepdims=True) * (1.0 / D_ROPE)
    kpe_blk = _rope((kb * lax.rsqrt(ms + EPS)) * krgain_ref[...], cosf, sinf, first_half)
    kpe_ref[...] = pltpu.roll(kpe_blk, LANES - ROPE_LO, 1)[:, 0:D_ROPE]
    ckv = _rms(z_scr[:, 1792:1920], kvan_ref[...])
    ckv_ref[...] = ckv
    _expand_kv(ckv.astype(BF16), kpe_blk, wkv_ref, kgain_ref, kv_scr, k_ref, v_ref)


def _odd_inproj(x, norm_g, ow, conv_init, n_batch, pos0):
    t = x.shape[0]
    tm = min(TOKEN_TILE, t // n_batch)
    tpb = t // n_batch // tm
    assert tpb * tm * n_batch == t
    row = lambda w: pl.BlockSpec((tm, w), lambda i: (i, 0))
    per_b = lambda r, w: pl.BlockSpec((1, r, w), lambda i: (i // tpb, 0, 0))
    small = [ow["conv_w"], ow["qan"], ow["wqb"], ow["qgain"], ow["kvan"], ow["krgain"],
             ow["invf"], ow["sgn"], ow["wkv"], ow["kgain"]]
    kv_w = D_HEADS * LANES
    v_w = D_HEADS * D_V
    return pl.pallas_call(
        functools.partial(_odd_inproj_kernel, tm=tm, tpb=tpb, pos0=pos0),
        grid=(t // tm,),
        in_specs=[row(D_MODEL), _resident((1, D_MODEL)), _resident(ow["w_in"].shape), per_b(8, C_WIDTH)]
                 + [_resident(a.shape) for a in small],
        out_specs=[row(C_WIDTH), row(kv_w), row(kv_w), row(v_w), row(D_KV_LORA), row(D_ROPE),
                   per_b(CONV_W - 1, C_WIDTH)],
        out_shape=[jax.ShapeDtypeStruct((t, C_WIDTH), BF16), jax.ShapeDtypeStruct((t, kv_w), BF16),
                   jax.ShapeDtypeStruct((t, kv_w), BF16), jax.ShapeDtypeStruct((t, v_w), BF16),
                   jax.ShapeDtypeStruct((t, D_KV_LORA), F32), jax.ShapeDtypeStruct((t, D_ROPE), F32),
                   jax.ShapeDtypeStruct((n_batch, CONV_W - 1, C_WIDTH), F32)],
        scratch_shapes=[pltpu.VMEM((tm, D_MODEL), BF16), pltpu.VMEM((tm, ODD_IN_PAD), F32),
                        pltpu.VMEM((tm + 8, C_WIDTH), F32), pltpu.VMEM((tm, D_Q_LORA), BF16),
                        pltpu.VMEM((tm, kv_w), F32), pltpu.VMEM((tm, kv_w + v_w), F32),
                        pltpu.VMEM((2, tpb * tm, LANES), F32)],
        compiler_params=_cparams("arbitrary"),
        name="odd_inproj",
    )(x, norm_g.reshape(1, D_MODEL), ow["w_in"], conv_init, *small)


def _kv_expand_kernel(ckv_ref, kpe_ref, wkv_ref, kgain_ref, k_ref, v_ref, kv_scr):
    _expand_kv(ckv_ref[...].astype(BF16), kpe_ref[...], wkv_ref, kgain_ref, kv_scr, k_ref, v_ref)


def _kv_expand(ckv, kpe_blk, ow):
    t = ckv.shape[0]
    tm = min(TOKEN_TILE, t)
    row = lambda w: pl.BlockSpec((tm, w), lambda i: (i, 0))
    kv_w = D_HEADS * LANES
    v_w = D_HEADS * D_V
    return pl.pallas_call(
        _kv_expand_kernel,
        grid=(t // tm,),
        in_specs=[row(D_KV_LORA), row(LANES), _resident(ow["wkv"].shape), _resident(ow["kgain"].shape)],
        out_specs=[row(kv_w), row(v_w)],
        out_shape=[jax.ShapeDtypeStruct((t, kv_w), BF16), jax.ShapeDtypeStruct((t, v_w), BF16)],
        scratch_shapes=[pltpu.VMEM((tm, kv_w + v_w), F32)],
        compiler_params=_cparams("arbitrary"),
        name="kv_expand",
    )(ckv, kpe_blk, ow["wkv"], ow["kgain"])


def _odd_weights(w_in, conv_w, q_a_norm, w_q_b, kv_a_norm, w_kv_b, qn_nope, qn_rope, kn_nope, kn_rope):
    z = lambda n: jnp.zeros((n,), F32)
    w_pad = jnp.zeros((D_MODEL, ODD_IN_PAD), F32)
    w_pad = w_pad.at[:, :1920].set(w_in[:, :1920])
    w_pad = w_pad.at[:, 1920 + ROPE_LO:1920 + ROPE_HI].set(w_in[:, 1920:])
    qk = D_NOPE + D_ROPE
    wqb = jnp.zeros((D_Q_LORA, D_HEADS, LANES), F32)
    wqb = wqb.at[:, :, :qk].set(w_q_b.reshape(D_Q_LORA, D_HEADS, qk)).reshape(D_Q_LORA, D_HEADS * LANES)
    kvb = w_kv_b.reshape(D_KV_LORA, D_HEADS, D_NOPE + D_V)
    wk = jnp.zeros((D_KV_LORA, D_HEADS, LANES), F32).at[:, :, :D_NOPE].set(kvb[:, :, :D_NOPE])
    wkv = jnp.concatenate([wk.reshape(D_KV_LORA, D_HEADS * LANES),
                           kvb[:, :, D_NOPE:].reshape(D_KV_LORA, D_HEADS * D_V)], axis=1)
    inv = 1.0 / (ROPE_THETA ** (jnp.arange(ROPE_HALF, dtype=F32) / ROPE_HALF))
    ones = jnp.ones((ROPE_HALF,), F32)
    return {
        "w_in": w_pad.astype(BF16),
        "conv_w": conv_w.astype(F32),
        "qan": q_a_norm.reshape(1, D_Q_LORA),
        "wqb": wqb.astype(BF16),
        "qgain": jnp.concatenate([qn_nope, qn_rope, z(LANES - qk)]).reshape(1, LANES),
        "kvan": kv_a_norm.reshape(1, D_KV_LORA),
        "krgain": jnp.concatenate([z(ROPE_LO), kn_rope, z(LANES - ROPE_HI)]).reshape(1, LANES),
        "invf": jnp.concatenate([z(ROPE_LO), inv, inv, z(LANES - ROPE_HI)]).reshape(1, LANES),
        "sgn": jnp.concatenate([z(ROPE_LO), -ones, ones, z(LANES - ROPE_HI)]).reshape(1, LANES),
        "wkv": wkv.astype(BF16),
        "kgain": jnp.concatenate([kn_nope, z(LANES - D_NOPE)]).reshape(1, LANES),
    }


def _mla_prompt_kernel(q_ref, k_ref, v_ref, o_ref, s_scr, e_scr, r_scr, *, seq):
    qb = min(MLA_QBLOCK, seq)
    lo = lax.broadcasted_iota(jnp.int32, (qb, LANES), 1) < D_V
    first_half = lax.broadcasted_iota(jnp.int32, (CHUNK, LANES), 1) < CHUNK
    for i in range(seq // qb):
        tk = qb * (i + 1)
        outs = []
        for h in range(2):
            q = q_ref[0, i * qb:(i + 1) * qb, h * LANES:(h + 1) * LANES]
            k = k_ref[0, 0:tk, h * LANES:(h + 1) * LANES]
            s_scr[h, :, 0:tk] = lax.dot_general(q, k, _NT, preferred_element_type=F32) * MLA_SCALE
            for rb in range(qb // CHUNK):
                rows = slice(rb * CHUNK, (rb + 1) * CHUNK)
                visible = tk - qb + CHUNK * (rb + 1)
                n_full, ragged = visible // LANES, visible % LANES != 0
                blk = lambda c: s_scr[h, rows, c * LANES:(c + 1) * LANES]
                cols = [blk(c) for c in range(n_full)]
                if ragged:
                    cols.append(jnp.where(first_half, blk(n_full), NEG_INF))
                mm = cols[0]
                for c in cols[1:]:
                    mm = jnp.maximum(mm, c)
                m = _row_stat(jnp.max, mm)
                acc = None
                for c in range(tk // LANES):
                    if c < len(cols):
                        e = jnp.exp(blk(c) - m)
                        if c >= n_full:
                            e = jnp.where(first_half, e, 0.0)
                        acc = e if acc is None else acc + e
                    else:
                        e = jnp.zeros((CHUNK, LANES), F32)
                    e_scr[h, rows, c * LANES:(c + 1) * LANES] = e.astype(BF16)
                r_scr[h, rows, :] = 1.0 / _row_stat(jnp.sum, acc)
            outs.append(_dot(e_scr[h, :, 0:tk], v_ref[0, 0:tk, :]) * r_scr[h])
        o_ref[0, i * qb:(i + 1) * qb, :] = jnp.where(lo, outs[0], outs[1]).astype(BF16)


def _mla_prompt(q, k, v):
    nb, seq, _ = q.shape
    qb = min(MLA_QBLOCK, seq)
    return pl.pallas_call(
        functools.partial(_mla_prompt_kernel, seq=seq),
        grid=(nb, D_HEADS // 2),
        in_specs=[pl.BlockSpec((1, seq, 2 * LANES), lambda b, p: (b, 0, p)),
                  pl.BlockSpec((1, seq, 2 * LANES), lambda b, p: (b, 0, p)),
                  pl.BlockSpec((1, seq, LANES), lambda b, p: (b, 0, p))],
        out_specs=pl.BlockSpec((1, seq, LANES), lambda b, p: (b, 0, p)),
        out_shape=jax.ShapeDtypeStruct((nb, seq, D_HEADS * D_V), BF16),
        scratch_shapes=[pltpu.VMEM((2, qb, seq), F32), pltpu.VMEM((2, qb, seq), BF16),
                        pltpu.VMEM((2, qb, LANES), F32)],
        compiler_params=_cparams("arbitrary", "arbitrary"),
        name="mla_prompt",
    )(q, k, v)


def _mla_sample_kernel(q_ref, kc_ref, vc_ref, kn_ref, vn_ref, o_ref):
    nq = q_ref.shape[1]
    lo = lax.broadcasted_iota(jnp.int32, (nq, LANES), 1) < D_V
    outs = []
    for h in range(2):
        cols = slice(h * LANES, (h + 1) * LANES)
        q = q_ref[0, :, cols]
        s1 = lax.dot_general(q, kc_ref[0, :, cols], _NT, preferred_element_type=F32) * MLA_SCALE
        s2 = lax.dot_general(q, kn_ref[0, :, cols], _NT, preferred_element_type=F32) * MLA_SCALE
        m = jnp.maximum(jnp.max(s1, axis=-1, keepdims=True), jnp.max(s2, axis=-1, keepdims=True))
        e1 = jnp.exp(s1 - m)
        e2 = jnp.exp(s2 - m)
        den = jnp.sum(e1, axis=-1, keepdims=True) + jnp.sum(e2, axis=-1, keepdims=True)
        outs.append((_dot(e1.astype(BF16), vc_ref[0]) + _dot(e2.astype(BF16), vn_ref[0])) / den)
    o_ref[0] = jnp.where(lo, outs[0], outs[1]).astype(BF16)


def _mla_sample(q, kc, vc, kn, vn):
    nb, nq, _ = q.shape
    past = kc.shape[1]
    two = lambda r: pl.BlockSpec((1, r, 2 * LANES), lambda b, p: (b, 0, p))
    one = lambda r: pl.BlockSpec((1, r, LANES), lambda b, p: (b, 0, p))
    return pl.pallas_call(
        _mla_sample_kernel,
        grid=(nb, D_HEADS // 2),
        in_specs=[two(nq), two(past), one(past), two(nq), one(nq)],
        out_specs=one(nq),
        out_shape=jax.ShapeDtypeStruct((nb, nq, D_HEADS * D_V), BF16),
        compiler_params=_cparams("arbitrary", "arbitrary"),
        name="mla_sample",
    )(q, kc, vc, kn, vn)


def _even_layer(xp, xs, nb, seq, ndb, dseq, norm_g, w_in, w_out, a_qn, a_kn, a_sinks, b_qn, b_kn, b_rel,
                t5_table, ck_a, cv_a, ck_b, cv_b, ffn2):
    perm = _pair_perm()
    w_in_p = jnp.concatenate([w_in[:, :A_Q][:, perm], w_in[:, A_Q:]], axis=1).astype(BF16)
    woa = w_out[:A_Q][perm].astype(BF16)
    wob = w_out[A_Q:].astype(BF16)
    ones = lambda n: jnp.ones((n,), F32)
    scale = HEAD_DIM ** -0.5
    gain_row = jnp.concatenate([jnp.tile(a_qn, A_HEADS) * scale, jnp.tile(a_kn, A_KV_HEADS), ones(A_KV),
                                jnp.tile(b_qn, B_HEADS) * scale, jnp.tile(b_kn, B_HEADS), ones(B_QKV)]
                               ).reshape(1, EVEN_IN).astype(F32)
    bias_a = _bias_a(t5_table)
    bias_b = _bias_b(b_rel)
    sink = jnp.broadcast_to(jnp.repeat(a_sinks.astype(F32), PAIR_ROWS)[:, None], (A_HEADS * PAIR_ROWS, LANES))
    la, lb = min(WINDOW, seq), min(B_REACH, seq)

    aq, ak, av, bq, bk, bv, cak, cav, cbk, cbv = _even_inproj(xp, norm_g, w_in_p, gain_row, nb, la, lb)
    r3 = lambda a: a.reshape(nb, seq, a.shape[-1])
    ya = _attn_a(r3(aq), r3(ak), r3(av), bias_a, sink, WINDOW, True)
    yb = _attn_b(r3(bq), r3(bk), r3(bv), bias_b, B_REACH, True)
    xp = _ffn(xp, *ffn2, out_proj=(ya.reshape(nb * seq, A_Q), yb.reshape(nb * seq, B_QKV), woa, wob))
    st_p = (cak.reshape(nb, la, A_KV_HEADS, HEAD_DIM), cav.reshape(nb, la, A_KV_HEADS, HEAD_DIM),
            cbk.reshape(nb, lb, B_HEADS, HEAD_DIM), cbv.reshape(nb, lb, B_HEADS, HEAD_DIM))

    ts = ndb * dseq
    aq, ak, av, bq, bk, bv, nak, nav, nbk, nbv = _even_inproj(xs, norm_g, w_in_p, gain_row, 1, ts, ts)
    pad_q = lambda a: jnp.pad(a.reshape(ndb, dseq, a.shape[-1]), ((0, 0), (0, PAIR_ROWS - dseq), (0, 0)))

    def window(cache, new):
        w = cache.shape[-2] * cache.shape[-1]
        full = jnp.concatenate([cache.reshape(ndb, -1, w), new.reshape(ndb, dseq, w)], axis=1)
        buf = jnp.pad(full, ((0, 0), (0, PAIR_ROWS - dseq), (0, 0))).astype(BF16)
        return full[:, dseq:].reshape(cache.shape), buf

    st_ak, kbuf_a = window(ck_a, nak)
    st_av, vbuf_a = window(cv_a, nav)
    st_bk, kbuf_b = window(ck_b, nbk)
    st_bv, vbuf_b = window(cv_b, nbv)
    ya = _attn_a(pad_q(aq), kbuf_a, vbuf_a, bias_a, sink, WINDOW, False)[:, :dseq]
    yb = _attn_b(pad_q(bq), kbuf_b, vbuf_b, bias_b, B_REACH, False)[:, :dseq]
    xs = _ffn(xs, *ffn2, out_proj=(ya.reshape(ts, A_Q), yb.reshape(ts, B_QKV), woa, wob))
    return xp, xs, st_p, (st_ak, st_av, st_bk, st_bv)


def _odd_layer(xp, xs, nb, seq, ndb, dseq, past, norm_g, ow, w_out, conv_prev, c_ckv, c_kpe, ffn2):
    woc = w_out[:C_WIDTH].astype(BF16)
    wod = w_out[C_WIDTH:].astype(BF16)

    zero_init = jnp.zeros((nb, 8, C_WIDTH), F32)
    yc, q, k, v, ckv, kpe, cs = _odd_inproj(xp, norm_g, ow, zero_init, nb, 0)
    r3 = lambda a: a.reshape(nb, seq, a.shape[-1])
    yd = _mla_prompt(r3(q), r3(k), r3(v))
    xp = _ffn(xp, *ffn2, out_proj=(yc, yd.reshape(nb * seq, D_HEADS * D_V), woc, wod))
    st_p = (cs, ckv.reshape(nb, seq, D_KV_LORA), kpe.reshape(nb, seq, D_ROPE))

    ts = ndb * dseq
    init = jnp.pad(conv_prev.astype(F32), ((0, 0), (8 - (CONV_W - 1), 0), (0, 0)))
    yc, q, kn, vn, ckv, kpe, cs = _odd_inproj(xs, norm_g, ow, init, ndb, past)
    kpe_blk = jnp.pad(c_kpe.reshape(ndb * past, D_ROPE), ((0, 0), (ROPE_LO, LANES - ROPE_HI)))
    kc, vc = _kv_expand(c_ckv.reshape(ndb * past, D_KV_LORA), kpe_blk, ow)
    s3 = lambda a: a.reshape(ndb, dseq, a.shape[-1])
    yd = _mla_sample(s3(q), kc.reshape(ndb, past, -1), vc.reshape(ndb, past, -1), s3(kn), s3(vn))
    xs = _ffn(xs, *ffn2, out_proj=(yc, yd.reshape(ts, D_HEADS * D_V), woc, wod))
    st_s = (cs, ckv.reshape(ndb, dseq, D_KV_LORA), kpe.reshape(ndb, dseq, D_ROPE))
    return xp, xs, st_p, st_s


def kernel(x_prompt, x_sample, cache_a_k, cache_a_v, cache_b_k, cache_b_v, state_c_conv, cache_d_ckv, cache_d_kpe, ff1_norm, ff1_w_gu, ff1_w_down, mix_norm, ff2_norm, ff2_w_gu, ff2_w_down, t5_bias_table, ev_w_in, ev_w_out, a_q_norm, a_k_norm, a_sinks, b_q_norm, b_k_norm, b_rel_bias, od_w_in, od_w_out, c_conv_w, d_q_a_norm, d_w_q_b, d_kv_a_norm, d_w_kv_b, d_q_nope_norm, d_q_rope_norm, d_k_nope_norm, d_k_rope_norm):
    nb, seq, _ = x_prompt.shape
    ndb, dseq, _ = x_sample.shape
    past = cache_d_ckv.shape[2]
    depth = ff1_norm.shape[0]
    assert seq % TOKEN_TILE == 0 and dseq == CHUNK and past % CHUNK == 0
    xp = x_prompt.reshape(nb * seq, D_MODEL)
    xs = x_sample.reshape(ndb * dseq, D_MODEL)
    even_p, even_s, odd_p, odd_s = [], [], [], []
    for l in range(depth):
        i = l // 2
        ff1 = _ffn_weights(ff1_w_gu[l], ff1_w_down[l])
        ffn2 = (ff2_norm[l],) + _ffn_weights(ff2_w_gu[l], ff2_w_down[l])
        xp = _ffn(xp, ff1_norm[l], *ff1)
        xs = _ffn(xs, ff1_norm[l], *ff1)
        if l % 2 == 0:
            xp, xs, sp, ss = _even_layer(
                xp, xs, nb, seq, ndb, dseq, mix_norm[l], ev_w_in[i], ev_w_out[i], a_q_norm[i], a_k_norm[i],
                a_sinks[i], b_q_norm[i], b_k_norm[i], b_rel_bias[i], t5_bias_table,
                cache_a_k[i], cache_a_v[i], cache_b_k[i], cache_b_v[i], ffn2)
            even_p.append(sp)
            even_s.append(ss)
        else:
            ow = _odd_weights(od_w_in[i], c_conv_w[i], d_q_a_norm[i], d_w_q_b[i], d_kv_a_norm[i], d_w_kv_b[i],
                              d_q_nope_norm[i], d_q_rope_norm[i], d_k_nope_norm[i], d_k_rope_norm[i])
            xp, xs, sp, ss = _odd_layer(xp, xs, nb, seq, ndb, dseq, past, mix_norm[l], ow, od_w_out[i],
                                        state_c_conv[i], cache_d_ckv[i], cache_d_kpe[i], ffn2)
            odd_p.append(sp)
            odd_s.append(ss)
    stack = lambda group, j: jnp.stack([g[j] for g in group])
    return (xp.reshape(nb, seq, D_MODEL), xs.reshape(ndb, dseq, D_MODEL),
            stack(even_p, 0), stack(even_p, 1), stack(even_p, 2), stack(even_p, 3),
            stack(odd_p, 0), stack(odd_p, 1), stack(odd_p, 2),
            stack(even_s, 0), stack(even_s, 1), stack(even_s, 2), stack(even_s, 3),
            stack(odd_s, 0), stack(odd_s, 1), stack(odd_s, 2))
```

```python
import functools
import math

import jax
import jax.numpy as jnp
from jax import lax
from jax.experimental import pallas as pl
from jax.experimental.pallas import tpu as pltpu

F32 = jnp.float32
BF16 = jnp.bfloat16

D_MODEL = 1024
CHUNK = 64
HEAD_DIM = 64
EPS = 1e-6
A_HEADS = 8
A_KV_HEADS = 2
WINDOW = 128
T5_BUCKETS = 32
T5_MAX_DIST = 128
B_HEADS = 8
B_REACH = 512
B_MAX_REL = 128
C_WIDTH = 512
CONV_W = 3
D_HEADS = 8
D_Q_LORA = 256
D_KV_LORA = 128
D_NOPE = 64
D_ROPE = 32
D_V = 64
ROPE_THETA = 10000.0
FFN_DIM = 2816
A_Q = A_HEADS * HEAD_DIM
A_KV = A_KV_HEADS * HEAD_DIM
B_QKV = B_HEADS * HEAD_DIM
EVEN_IN = A_Q + 2 * A_KV + 3 * B_QKV

LANES = 128
TOKEN_TILE = 512
MXU_COLS = 256
FFN_CHUNK = MXU_COLS
PAIR_ROWS = 2 * CHUNK
MLA_QBLOCK = 256
VMEM_LIMIT_BYTES = 56 * 1024 * 1024
MLA_QSCALE = (D_NOPE + D_ROPE) ** -0.5 * math.log2(math.e)
NEG_INF = float("-inf")

_NT = (((1,), (1,)), ((), ()))


def _cparams(*sem):
    return pltpu.CompilerParams(dimension_semantics=sem, vmem_limit_bytes=VMEM_LIMIT_BYTES)


def _resident(shape):
    zeros = (0,) * len(shape)
    return pl.BlockSpec(shape, lambda *_: zeros, pipeline_mode=pl.Buffered(1))


def _rms(x, g):
    ms = jnp.mean(x * x, axis=-1, keepdims=True)
    return (x * lax.rsqrt(ms + EPS)) * g


def _dot(a, b):
    return jnp.dot(a, b, preferred_element_type=F32)


def _ffn_kernel(*refs, n_chunks, fused_out):
    if fused_out:
        (x_ref, ya_ref, yb_ref, woa_ref, wob_ref, g_ref, wg_ref, wu_ref, wd_ref,
         o_ref, h_scr, a_scr) = refs
        y = _dot(ya_ref[...], woa_ref[...]) + _dot(yb_ref[...], wob_ref[...])
        o_ref[...] = x_ref[...] + y
        res_ref = o_ref
    else:
        x_ref, g_ref, wg_ref, wu_ref, wd_ref, o_ref, h_scr, a_scr = refs
        res_ref = x_ref
    h_scr[...] = _rms(res_ref[...], g_ref[...]).astype(BF16)
    for j in range(n_chunks):
        h = h_scr[...]
        g = _dot(h, wg_ref[j])
        u = _dot(h, wu_ref[j])
        a_scr[:, j * FFN_CHUNK:(j + 1) * FFN_CHUNK] = ((g * jax.nn.sigmoid(g)) * u).astype(BF16)
    o_ref[...] = res_ref[...] + 0.5 * _dot(a_scr[...], wd_ref[...])


def _ffn(x, norm_g, wg3, wu3, wd3, out_proj=None):
    t = x.shape[0]
    tm = min(TOKEN_TILE, t)
    n_chunks = wg3.shape[0]
    row = lambda w: pl.BlockSpec((tm, w), lambda i: (i, 0))
    in_specs = [row(D_MODEL)]
    args = [x]
    if out_proj is not None:
        ya, yb, woa, wob = out_proj
        in_specs += [row(ya.shape[1]), row(yb.shape[1]), _resident(woa.shape), _resident(wob.shape)]
        args += [ya, yb, woa, wob]
    in_specs += [_resident((1, D_MODEL)), _resident(wg3.shape), _resident(wu3.shape), _resident(wd3.shape)]
    args += [norm_g.reshape(1, D_MODEL), wg3, wu3, wd3]
    return pl.pallas_call(
        functools.partial(_ffn_kernel, n_chunks=n_chunks, fused_out=out_proj is not None),
        grid=(t // tm,),
        in_specs=in_specs,
        out_specs=row(D_MODEL),
        out_shape=jax.ShapeDtypeStruct((t, D_MODEL), F32),
        scratch_shapes=[pltpu.VMEM((tm, D_MODEL), BF16), pltpu.VMEM((tm, FFN_DIM), BF16)],
        compiler_params=_cparams("arbitrary"),
        name="ffn_out" if out_proj is not None else "ffn",
    )(*args)


def _ffn_weights(w_gu, w_down):
    n = FFN_DIM // FFN_CHUNK
    wg = w_gu[:, :FFN_DIM].astype(BF16).reshape(D_MODEL, n, FFN_CHUNK).transpose(1, 0, 2)
    wu = w_gu[:, FFN_DIM:].astype(BF16).reshape(D_MODEL, n, FFN_CHUNK).transpose(1, 0, 2)
    return wg, wu, w_down.astype(BF16)


def _norm_halves(blk, gain):
    sq = blk * blk
    lo = lax.broadcasted_iota(jnp.int32, blk.shape, 1) < HEAD_DIM
    s_lo = jnp.sum(jnp.where(lo, sq, 0.0), axis=-1, keepdims=True)
    s_hi = jnp.sum(jnp.where(lo, 0.0, sq), axis=-1, keepdims=True)
    inv = jnp.where(lo, lax.rsqrt(s_lo * (1.0 / HEAD_DIM) + EPS), lax.rsqrt(s_hi * (1.0 / HEAD_DIM) + EPS))
    return (blk * inv) * gain


def _even_inproj_kernel(x_ref, g_ref, w_ref, gain_ref,
                        aq_ref, ak_ref, av_ref, bq_ref, bk_ref, bv_ref,
                        cak_ref, cav_ref, cbk_ref, cbv_ref, h_scr, z_scr, *, tm, tpb, rows_a, rows_b):
    i = pl.program_id(0)
    last = (i % tpb) == (tpb - 1)
    h_scr[...] = _rms(x_ref[...], g_ref[...]).astype(BF16)

    def lanes(c):
        return slice(c * LANES, (c + 1) * LANES)

    def emit(c):
        blk = z_scr[:, lanes(c)]
        gain = gain_ref[:, lanes(c)]
        if c < 4:
            aq_ref[:, lanes(c)] = _norm_halves(blk, gain).astype(BF16)
        elif c == 4:
            kn = _norm_halves(blk, gain)
            ak_ref[...] = kn.astype(BF16)
            z_scr[:, lanes(c)] = kn
        elif c == 5:
            av_ref[...] = blk.astype(BF16)
        elif c < 10:
            bq_ref[:, lanes(c - 6)] = _norm_halves(blk, gain).astype(BF16)
        elif c < 14:
            kn = _norm_halves(blk, gain)
            bk_ref[:, lanes(c - 10)] = kn.astype(BF16)
            z_scr[:, lanes(c)] = kn
        else:
            bv_ref[:, lanes(c - 14)] = blk.astype(BF16)

    for grp in range(EVEN_IN // MXU_COLS):
        cols = slice(grp * MXU_COLS, (grp + 1) * MXU_COLS)
        z_scr[:, cols] = _dot(h_scr[...], w_ref[:, cols])
        emit(2 * grp)
        emit(2 * grp + 1)

    @pl.when(last)
    def _():
        cak_ref[0] = z_scr[tm - rows_a:, lanes(4)]
        cav_ref[0] = z_scr[tm - rows_a:, lanes(5)]
        cbk_ref[0] = z_scr[tm - rows_b:, 10 * LANES:14 * LANES]
        cbv_ref[0] = z_scr[tm - rows_b:, 14 * LANES:18 * LANES]


def _even_inproj(x, norm_g, w_in, gain_row, n_batch, rows_a, rows_b):
    t = x.shape[0]
    tm = min(TOKEN_TILE, t)
    tpb = t // n_batch // tm
    assert tpb * tm * n_batch == t and rows_a <= tm and rows_b <= tm
    row = lambda w: pl.BlockSpec((tm, w), lambda i: (i, 0))
    cache = lambda r, w: pl.BlockSpec((1, r, w), lambda i: (i // tpb, 0, 0))
    bshape = lambda w: jax.ShapeDtypeStruct((t, w), BF16)
    cshape = lambda r, w: jax.ShapeDtypeStruct((n_batch, r, w), F32)
    return pl.pallas_call(
        functools.partial(_even_inproj_kernel, tm=tm, tpb=tpb, rows_a=rows_a, rows_b=rows_b),
        grid=(t // tm,),
        in_specs=[row(D_MODEL), _resident((1, D_MODEL)), _resident(w_in.shape), _resident(gain_row.shape)],
        out_specs=[row(A_Q), row(A_KV), row(A_KV), row(B_QKV), row(B_QKV), row(B_QKV),
                   cache(rows_a, A_KV), cache(rows_a, A_KV), cache(rows_b, B_QKV), cache(rows_b, B_QKV)],
        out_shape=[bshape(A_Q), bshape(A_KV), bshape(A_KV), bshape(B_QKV), bshape(B_QKV), bshape(B_QKV),
                   cshape(rows_a, A_KV), cshape(rows_a, A_KV), cshape(rows_b, B_QKV), cshape(rows_b, B_QKV)],
        scratch_shapes=[pltpu.VMEM((tm, D_MODEL), BF16), pltpu.VMEM((tm, EVEN_IN), F32)],
        compiler_params=_cparams("arbitrary"),
        name="even_inproj",
    )(x, norm_g.reshape(1, D_MODEL), w_in, gain_row)


def _pair_perm():
    idx = []
    for p in range(4):
        for half in range(2):
            head = p + 4 * half
            idx.extend(range(head * HEAD_DIM, (head + 1) * HEAD_DIM))
    return jnp.asarray(idx, dtype=jnp.int32)


def _fill_padded(buf, src_ref, ctx):
    buf[0:ctx, :] = jnp.zeros((ctx, buf.shape[1]), buf.dtype)
    buf[ctx:, :] = src_ref[0]


def _windows(k_ref, v_ref, pad_scr, ctx, pad_front):
    win = ctx + PAIR_ROWS
    if pad_front:
        kbuf, vbuf = pad_scr
        _fill_padded(kbuf, k_ref, ctx)
        _fill_padded(vbuf, v_ref, ctx)
        return (lambda r0: kbuf[pl.ds(r0, win), :]), (lambda r0: vbuf[pl.ds(r0, win), :])
    return (lambda r0: k_ref[0, pl.ds(r0, win), :]), (lambda r0: v_ref[0, pl.ds(r0, win), :])


def _fold_lanes(x, op):
    acc = x[:, :LANES]
    for c in range(1, x.shape[1] // LANES):
        acc = op(acc, x[:, c * LANES:(c + 1) * LANES])
    return acc


def _row_stat(reduce_fn, x):
    return jnp.broadcast_to(reduce_fn(x, axis=-1, keepdims=True), x.shape)


PAIR_SLOTS = 4


def _run_pairs(pair, n_steps, n_front):
    for t in range(n_steps):
        pair(t * PAIR_ROWS, t % PAIR_SLOTS, t < n_front)


def _attn_a_kernel(q_ref, k_ref, v_ref, bias_ref, sink_ref, o_ref, s_scr, e_scr, *pad_scr, nq, ctx, pad_front):
    win = ctx + PAIR_ROWS
    kwin, vwin = _windows(k_ref, v_ref, pad_scr, ctx, pad_front)
    lo = lax.broadcasted_iota(jnp.int32, (PAIR_ROWS, LANES), 1) < HEAD_DIM

    def pair(r0, slot, masked):
        q_lo, q_hi = [], []
        for p in range(4):
            qp = q_ref[0, pl.ds(r0, PAIR_ROWS), p * LANES:(p + 1) * LANES]
            zero = jnp.zeros_like(qp)
            q_lo.append(jnp.where(lo, qp, zero))
            q_hi.append(jnp.where(lo, zero, qp))
        q2 = jnp.concatenate(q_lo + q_hi, axis=0)
        s_scr[slot] = lax.dot_general(q2, kwin(r0), _NT, preferred_element_type=F32)
        for head in range(A_HEADS):
            rows = slice(head * PAIR_ROWS, (head + 1) * PAIR_ROWS)
            s = s_scr[slot, rows, :] + bias_ref[rows, :]
            if masked:
                col = lax.broadcasted_iota(jnp.int32, (PAIR_ROWS, win), 1)
                s = jnp.where(col + r0 >= ctx, s, NEG_INF)
            sk = sink_ref[rows, :]
            m = jnp.maximum(_row_stat(jnp.max, _fold_lanes(s, jnp.maximum)), sk)
            e = [jnp.exp(s[:, c * LANES:(c + 1) * LANES] - m) for c in range(win // LANES)]
            den = _row_stat(jnp.sum, functools.reduce(jnp.add, e)) + jnp.exp(sk - m)
            inv = 1.0 / den
            for c in range(win // LANES):
                e_scr[slot, rows, c * LANES:(c + 1) * LANES] = (e[c] * inv).astype(BF16)
        o2 = _dot(e_scr[slot], vwin(r0))
        half = 4 * PAIR_ROWS
        for p in range(4):
            o = jnp.where(lo, o2[p * PAIR_ROWS:(p + 1) * PAIR_ROWS],
                          o2[half + p * PAIR_ROWS:half + (p + 1) * PAIR_ROWS])
            o_ref[0, pl.ds(r0, PAIR_ROWS), p * LANES:(p + 1) * LANES] = o.astype(BF16)

    _run_pairs(pair, nq // PAIR_ROWS, ctx // PAIR_ROWS if pad_front else 0)


def _attn_a(q, k, v, bias, sink, ctx, pad_front):
    nb, nq, _ = q.shape
    nk = k.shape[1]
    assert nq % PAIR_ROWS == 0 and nk == (nq if pad_front else ctx + nq)
    per_b = lambda r, w: pl.BlockSpec((1, r, w), lambda b: (b, 0, 0))
    rows, win = A_HEADS * PAIR_ROWS, ctx + PAIR_ROWS
    scratch = [pltpu.VMEM((PAIR_SLOTS, rows, win), F32), pltpu.VMEM((PAIR_SLOTS, rows, win), BF16)]
    if pad_front:
        scratch += [pltpu.VMEM((ctx + nq, A_KV), BF16)] * 2
    return pl.pallas_call(
        functools.partial(_attn_a_kernel, nq=nq, ctx=ctx, pad_front=pad_front),
        grid=(nb,),
        in_specs=[per_b(nq, A_Q), per_b(nk, A_KV), per_b(nk, A_KV), _resident(bias.shape), _resident(sink.shape)],
        out_specs=per_b(nq, A_Q),
        out_shape=jax.ShapeDtypeStruct((nb, nq, A_Q), BF16),
        scratch_shapes=scratch,
        compiler_params=_cparams("arbitrary"),
        name="attn_a",
    )(q, k, v, bias, sink)


def _attn_b_kernel(q_ref, k_ref, v_ref, bias_ref, o_ref, s_scr, e_scr, r_scr, *pad_scr, nq, ctx, pad_front):
    win = ctx + PAIR_ROWS
    kwin, vwin = _windows(k_ref, v_ref, pad_scr, ctx, pad_front)
    lo = lax.broadcasted_iota(jnp.int32, (PAIR_ROWS, LANES), 1) < HEAD_DIM

    def pair(r0, slot, masked):
        qp = q_ref[0, pl.ds(r0, PAIR_ROWS), :]
        zero = jnp.zeros_like(qp)
        q2 = jnp.concatenate([jnp.where(lo, qp, zero), jnp.where(lo, zero, qp)], axis=0)
        s_scr[slot] = lax.dot_general(q2, kwin(r0), _NT, preferred_element_type=F32)
        for rb in range(2 * PAIR_ROWS // CHUNK):
            rows = slice(rb * CHUNK, (rb + 1) * CHUNK)
            s = s_scr[slot, rows, :] + bias_ref[0, rows, :]
            if masked:
                col = lax.broadcasted_iota(jnp.int32, (CHUNK, win), 1)
                s = jnp.where(col + r0 >= ctx, s, NEG_INF)
            m = _row_stat(jnp.max, _fold_lanes(s, jnp.maximum))
            e = [jnp.exp(s[:, c * LANES:(c + 1) * LANES] - m) for c in range(win // LANES)]
            for c in range(win // LANES):
                e_scr[slot, rows, c * LANES:(c + 1) * LANES] = e[c].astype(BF16)
            r_scr[slot, rows, :] = 1.0 / _row_stat(jnp.sum, functools.reduce(jnp.add, e))
        o2 = _dot(e_scr[slot], vwin(r0)) * r_scr[slot]
        o = jnp.where(lo, o2[:PAIR_ROWS], o2[PAIR_ROWS:])
        o_ref[0, pl.ds(r0, PAIR_ROWS), :] = o.astype(BF16)

    _run_pairs(pair, nq // PAIR_ROWS, ctx // PAIR_ROWS if pad_front else 0)


def _attn_b(q, k, v, bias, ctx, pad_front):
    nb, nq, _ = q.shape
    nk = k.shape[1]
    assert nq % PAIR_ROWS == 0 and nk == (nq if pad_front else ctx + nq)
    blk = lambda r: pl.BlockSpec((1, r, LANES), lambda b, p: (b, 0, p))
    rows, win = 2 * PAIR_ROWS, ctx + PAIR_ROWS
    scratch = [pltpu.VMEM((PAIR_SLOTS, rows, win), F32), pltpu.VMEM((PAIR_SLOTS, rows, win), BF16),
               pltpu.VMEM((PAIR_SLOTS, rows, LANES), F32)]
    if pad_front:
        scratch += [pltpu.VMEM((ctx + nq, LANES), BF16)] * 2
    return pl.pallas_call(
        functools.partial(_attn_b_kernel, nq=nq, ctx=ctx, pad_front=pad_front),
        grid=(nb, B_HEADS // 2),
        in_specs=[blk(nq), blk(nk), blk(nk),
                  pl.BlockSpec((1,) + bias.shape[1:], lambda b, p: (p, 0, 0))],
        out_specs=blk(nq),
        out_shape=jax.ShapeDtypeStruct((nb, nq, B_QKV), BF16),
        scratch_shapes=scratch,
        compiler_params=_cparams("arbitrary", "arbitrary"),
        name="attn_b",
    )(q, k, v, bias)


def _t5_bucket(rel):
    nb = T5_BUCKETS // 2
    max_exact = nb // 2
    n = -rel
    ret = jnp.where(n < 0, nb, 0)
    n = jnp.abs(n)
    nf = jnp.maximum(n, 1).astype(F32)
    large = max_exact + (jnp.log(nf / max_exact) / math.log(T5_MAX_DIST / max_exact)
                         * (nb - max_exact)).astype(jnp.int32)
    large = jnp.minimum(large, nb - 1)
    return ret + jnp.where(n < max_exact, n, large)


def _band_tables(ctx, value_of_rel):
    n_cols = ctx + PAIR_ROWS
    period = n_cols + PAIR_ROWS - 1
    d = jnp.concatenate([jnp.arange(0, n_cols), jnp.arange(-(PAIR_ROWS - 1), 0)])
    ext = value_of_rel(d - ctx)
    heads = ext.shape[0]
    flat = jnp.tile(ext, (1, PAIR_ROWS))[:, :PAIR_ROWS * (period - 1)]
    tb = flat.reshape(heads, PAIR_ROWS, period - 1)[:, :, :n_cols]
    i = jnp.arange(PAIR_ROWS)[:, None]
    j = jnp.arange(n_cols)[None, :]
    visible = jnp.where(i < CHUNK, j < ctx + CHUNK, j >= CHUNK)
    return jnp.where(visible[None], tb, NEG_INF)


def _bias_a(t5_table):
    tb = _band_tables(WINDOW, lambda rel: t5_table.astype(F32)[_t5_bucket(rel)].T)
    return tb.reshape(A_HEADS * PAIR_ROWS, WINDOW + PAIR_ROWS)


def _bias_b(b_rel):
    tb = _band_tables(B_REACH, lambda rel: b_rel.astype(F32)[:, jnp.clip(rel, -B_MAX_REL, B_MAX_REL) + B_MAX_REL])
    return tb.reshape(B_HEADS // 2, 2 * PAIR_ROWS, B_REACH + PAIR_ROWS)


ODD_IN_PAD = 2048
ROPE_LO = D_NOPE
ROPE_HI = D_NOPE + D_ROPE
ROPE_HALF = D_ROPE // 2


def _lane_masks(shape):
    lane = lax.broadcasted_iota(jnp.int32, shape, 1)
    return lane < ROPE_LO, (lane >= ROPE_LO) & (lane < ROPE_HI), lane < ROPE_LO + ROPE_HALF


def _rope(y, cosf, sinf, first_half):
    rot = jnp.where(first_half, pltpu.roll(y, LANES - ROPE_HALF, 1), pltpu.roll(y, ROPE_HALF, 1))
    return y * cosf + rot * sinf


ROW_BLOCK = 128


def _row_blocks(tm):
    rb = min(ROW_BLOCK, tm)
    return [slice(r, r + rb) for r in range(0, tm, rb)]


def _expand_kv(ckvb_ref, kpe_blk_ref, wkv_ref, kgain_ref, kv_scr, k_ref, v_ref):
    k_cols = D_HEADS * LANES
    tm = kv_scr.shape[0]
    for grp in range(kv_scr.shape[1] // MXU_COLS):
        cols = slice(grp * MXU_COLS, (grp + 1) * MXU_COLS)
        kv_scr[:, cols] = _dot(ckvb_ref[...], wkv_ref[:, cols])
        for rows in _row_blocks(tm):
            if cols.start < k_cols:
                for h in (2 * grp, 2 * grp + 1):
                    kb = kv_scr[rows, h * LANES:(h + 1) * LANES]
                    ms = jnp.sum(kb * kb, axis=-1, keepdims=True) * (1.0 / D_NOPE)
                    kn = (kb * lax.rsqrt(ms + EPS)) * kgain_ref[...]
                    k_ref[rows, h * LANES:(h + 1) * LANES] = (kn + kpe_blk_ref[rows, :]).astype(BF16)
            else:
                v_ref[rows, cols.start - k_cols:cols.stop - k_cols] = kv_scr[rows, cols].astype(BF16)


def _odd_inproj_kernel(x_ref, g_ref, w_ref, cinit_ref, convw_ref, qan_ref, wqb_ref, qgain_ref,
                       kvan_ref, krgain_ref, invf_ref, sgn_ref, wkv_ref, kgain_ref,
                       yc_ref, q_ref, k_ref, v_ref, ckv_ref, kpe_ref, cs_ref,
                       h_scr, z_scr, uext_scr, qn_scr, q_scr, kv_scr, rot_scr, ckvb_scr, kpe_scr,
                       *, tm, tpb, pos0):
    i = pl.program_id(0)
    tile = i % tpb
    off = pl.multiple_of(tile * tm, tm)
    blocks = _row_blocks(tm)
    rb = blocks[0].stop

    @pl.when(tile == 0)
    def _():
        uext_scr[0:8, :] = cinit_ref[0]

    @pl.when(i < tpb)
    def _():
        row = lax.broadcasted_iota(jnp.int32, (tm, LANES), 0)
        ang = (row + (tile * tm + pos0)).astype(F32) * invf_ref[...]
        rot_scr[0, pl.ds(off, tm), :] = jnp.cos(ang)
        rot_scr[1, pl.ds(off, tm), :] = jnp.sin(ang) * sgn_ref[...]

    h_scr[...] = _rms(x_ref[...], g_ref[...]).astype(BF16)
    nope, ropem, first_half = _lane_masks((rb, LANES))

    def proj(c0, c1):
        z_scr[:, c0:c1] = _dot(h_scr[...], w_ref[:, c0:c1])

    def rotary(y, rows):
        cosf = rot_scr[0, pl.ds(off + rows.start, rb), :]
        sinf = rot_scr[1, pl.ds(off + rows.start, rb), :]
        return _rope(y, cosf, sinf, first_half)

    proj(1536, 1792)
    for rows in blocks:
        qn_scr[rows, :] = _rms(z_scr[rows, 1536:1792], qan_ref[...]).astype(BF16)
    for grp in range(D_HEADS * LANES // MXU_COLS):
        cols = slice(grp * MXU_COLS, (grp + 1) * MXU_COLS)
        q_scr[:, cols] = _dot(qn_scr[...], wqb_ref[:, cols])
        for h in (2 * grp, 2 * grp + 1):
            for rows in blocks:
                blk = q_scr[rows, h * LANES:(h + 1) * LANES]
                sq = blk * blk
                s_n = jnp.sum(jnp.where(nope, sq, 0.0), axis=-1, keepdims=True)
                s_r = jnp.sum(jnp.where(ropem, sq, 0.0), axis=-1, keepdims=True)
                inv = jnp.where(nope, lax.rsqrt(s_n * (1.0 / D_NOPE) + EPS),
                                lax.rsqrt(s_r * (1.0 / D_ROPE) + EPS))
                y = rotary((blk * inv) * qgain_ref[...], rows)
                q_ref[rows, h * LANES:(h + 1) * LANES] = y.astype(BF16)

    proj(1792, 2048)
    for rows in blocks:
        kb = z_scr[rows, 1920:2048]
        ms = jnp.sum(kb * kb, axis=-1, keepdims=True) * (1.0 / D_ROPE)
        kpe_blk = rotary((kb * lax.rsqrt(ms + EPS)) * krgain_ref[...], rows)
        kpe_scr[rows, :] = kpe_blk
        kpe_ref[rows, :] = pltpu.roll(kpe_blk, LANES - ROPE_LO, 1)[:, 0:D_ROPE]
        ckv = _rms(z_scr[rows, 1792:1920], kvan_ref[...])
        ckv_ref[rows, :] = ckv
        ckvb_scr[rows, :] = ckv.astype(BF16)
    _expand_kv(ckvb_scr, kpe_scr, wkv_ref, kgain_ref, kv_scr, k_ref, v_ref)

    for grp in range(C_WIDTH // MXU_COLS):
        c0 = grp * MXU_COLS
        cols = slice(c0, c0 + MXU_COLS)
        for base in (0, C_WIDTH, 2 * C_WIDTH):
            proj(base + c0, base + c0 + MXU_COLS)
        for rows in blocks:
            up = slice(rows.start + 8, rows.stop + 8)
            uext_scr[up, cols] = (z_scr[rows, C_WIDTH + c0:C_WIDTH + c0 + MXU_COLS]
                                  * z_scr[rows, 2 * C_WIDTH + c0:2 * C_WIDTH + c0 + MXU_COLS])
        for rows in blocks:
            yconv = convw_ref[0:1, cols] * uext_scr[rows.start + 6:rows.stop + 6, cols]
            yconv = yconv + convw_ref[1:2, cols] * uext_scr[rows.start + 7:rows.stop + 7, cols]
            yconv = yconv + convw_ref[2:3, cols] * uext_scr[rows.start + 8:rows.stop + 8, cols]
            yc_ref[rows, cols] = (z_scr[rows, cols] * yconv).astype(BF16)

    cs_ref[0] = uext_scr[tm + 6:tm + 8, :]
    uext_scr[0:8, :] = uext_scr[tm:tm + 8, :]


def _odd_inproj(x, norm_g, ow, conv_init, n_batch, pos0):
    t = x.shape[0]
    tm = min(TOKEN_TILE, t // n_batch)
    tpb = t // n_batch // tm
    assert tpb * tm * n_batch == t
    row = lambda w: pl.BlockSpec((tm, w), lambda i: (i, 0))
    per_b = lambda r, w: pl.BlockSpec((1, r, w), lambda i: (i // tpb, 0, 0))
    small = [ow["conv_w"], ow["qan"], ow["wqb"], ow["qgain"], ow["kvan"], ow["krgain"],
             ow["invf"], ow["sgn"], ow["wkv"], ow["kgain"]]
    kv_w = D_HEADS * LANES
    v_w = D_HEADS * D_V
    return pl.pallas_call(
        functools.partial(_odd_inproj_kernel, tm=tm, tpb=tpb, pos0=pos0),
        grid=(t // tm,),
        in_specs=[row(D_MODEL), _resident((1, D_MODEL)), _resident(ow["w_in"].shape), per_b(8, C_WIDTH)]
                 + [_resident(a.shape) for a in small],
        out_specs=[row(C_WIDTH), row(kv_w), row(kv_w), row(v_w), row(D_KV_LORA), row(D_ROPE),
                   per_b(CONV_W - 1, C_WIDTH)],
        out_shape=[jax.ShapeDtypeStruct((t, C_WIDTH), BF16), jax.ShapeDtypeStruct((t, kv_w), BF16),
                   jax.ShapeDtypeStruct((t, kv_w), BF16), jax.ShapeDtypeStruct((t, v_w), BF16),
                   jax.ShapeDtypeStruct((t, D_KV_LORA), F32), jax.ShapeDtypeStruct((t, D_ROPE), F32),
                   jax.ShapeDtypeStruct((n_batch, CONV_W - 1, C_WIDTH), F32)],
        scratch_shapes=[pltpu.VMEM((tm, D_MODEL), BF16), pltpu.VMEM((tm, ODD_IN_PAD), F32),
                        pltpu.VMEM((tm + 8, C_WIDTH), F32), pltpu.VMEM((tm, D_Q_LORA), BF16),
                        pltpu.VMEM((tm, kv_w), F32), pltpu.VMEM((tm, kv_w + v_w), F32),
                        pltpu.VMEM((2, tpb * tm, LANES), F32), pltpu.VMEM((tm, D_KV_LORA), BF16),
                        pltpu.VMEM((tm, LANES), F32)],
        compiler_params=_cparams("arbitrary"),
        name="odd_inproj",
    )(x, norm_g.reshape(1, D_MODEL), ow["w_in"], conv_init, *small)


def _kv_expand_kernel(ckv_ref, kpe_ref, wkv_ref, kgain_ref, k_ref, v_ref, kv_scr, ckvb_scr):
    ckvb_scr[...] = ckv_ref[...].astype(BF16)
    _expand_kv(ckvb_scr, kpe_ref, wkv_ref, kgain_ref, kv_scr, k_ref, v_ref)


def _kv_expand(ckv, kpe_blk, ow):
    t = ckv.shape[0]
    tm = min(TOKEN_TILE, t)
    row = lambda w: pl.BlockSpec((tm, w), lambda i: (i, 0))
    kv_w = D_HEADS * LANES
    v_w = D_HEADS * D_V
    return pl.pallas_call(
        _kv_expand_kernel,
        grid=(t // tm,),
        in_specs=[row(D_KV_LORA), row(LANES), _resident(ow["wkv"].shape), _resident(ow["kgain"].shape)],
        out_specs=[row(kv_w), row(v_w)],
        out_shape=[jax.ShapeDtypeStruct((t, kv_w), BF16), jax.ShapeDtypeStruct((t, v_w), BF16)],
        scratch_shapes=[pltpu.VMEM((tm, kv_w + v_w), F32), pltpu.VMEM((tm, D_KV_LORA), BF16)],
        compiler_params=_cparams("arbitrary"),
        name="kv_expand",
    )(ckv, kpe_blk, ow["wkv"], ow["kgain"])


def _odd_weights(w_in, conv_w, q_a_norm, w_q_b, kv_a_norm, w_kv_b, qn_nope, qn_rope, kn_nope, kn_rope):
    z = lambda n: jnp.zeros((n,), F32)
    w_pad = jnp.zeros((D_MODEL, ODD_IN_PAD), F32)
    w_pad = w_pad.at[:, :1920].set(w_in[:, :1920])
    w_pad = w_pad.at[:, 1920 + ROPE_LO:1920 + ROPE_HI].set(w_in[:, 1920:])
    qk = D_NOPE + D_ROPE
    wqb = jnp.zeros((D_Q_LORA, D_HEADS, LANES), F32)
    wqb = wqb.at[:, :, :qk].set(w_q_b.reshape(D_Q_LORA, D_HEADS, qk)).reshape(D_Q_LORA, D_HEADS * LANES)
    kvb = w_kv_b.reshape(D_KV_LORA, D_HEADS, D_NOPE + D_V)
    wk = jnp.zeros((D_KV_LORA, D_HEADS, LANES), F32).at[:, :, :D_NOPE].set(kvb[:, :, :D_NOPE])
    wkv = jnp.concatenate([wk.reshape(D_KV_LORA, D_HEADS * LANES),
                           kvb[:, :, D_NOPE:].reshape(D_KV_LORA, D_HEADS * D_V)], axis=1)
    inv = 1.0 / (ROPE_THETA ** (jnp.arange(ROPE_HALF, dtype=F32) / ROPE_HALF))
    ones = jnp.ones((ROPE_HALF,), F32)
    return {
        "w_in": w_pad.astype(BF16),
        "conv_w": conv_w.astype(F32),
        "qan": q_a_norm.reshape(1, D_Q_LORA),
        "wqb": wqb.astype(BF16),
        "qgain": jnp.concatenate([qn_nope, qn_rope, z(LANES - qk)]).reshape(1, LANES) * MLA_QSCALE,
        "kvan": kv_a_norm.reshape(1, D_KV_LORA),
        "krgain": jnp.concatenate([z(ROPE_LO), kn_rope, z(LANES - ROPE_HI)]).reshape(1, LANES),
        "invf": jnp.concatenate([z(ROPE_LO), inv, inv, z(LANES - ROPE_HI)]).reshape(1, LANES),
        "sgn": jnp.concatenate([z(ROPE_LO), -ones, ones, z(LANES - ROPE_HI)]).reshape(1, LANES),
        "wkv": wkv.astype(BF16),
        "kgain": jnp.concatenate([kn_nope, z(LANES - D_NOPE)]).reshape(1, LANES),
    }


def _mla_prompt_kernel(q_ref, k_ref, v_ref, o_ref, s_scr, e_scr, r_scr, *, seq):
    qb = min(MLA_QBLOCK, seq)
    lo = lax.broadcasted_iota(jnp.int32, (qb, LANES), 1) < D_V
    first_half = lax.broadcasted_iota(jnp.int32, (CHUNK, LANES), 1) < CHUNK
    for i in range(seq // qb):
        tk = qb * (i + 1)
        outs = []
        for h in range(2):
            q = q_ref[0, i * qb:(i + 1) * qb, h * LANES:(h + 1) * LANES]
            k = k_ref[0, 0:tk, h * LANES:(h + 1) * LANES]
            s_scr[h, :, 0:tk] = lax.dot_general(q, k, _NT, preferred_element_type=F32)
            for rb in range(qb // CHUNK):
                rows = slice(rb * CHUNK, (rb + 1) * CHUNK)
                visible = tk - qb + CHUNK * (rb + 1)
                n_full, ragged = visible // LANES, visible % LANES != 0
                blk = lambda c: s_scr[h, rows, c * LANES:(c + 1) * LANES]
                cols = [blk(c) for c in range(n_full)]
                if ragged:
                    cols.append(jnp.where(first_half, blk(n_full), NEG_INF))
                mm = cols[0]
                for c in cols[1:]:
                    mm = jnp.maximum(mm, c)
                m = _row_stat(jnp.max, mm)
                acc = None
                for c in range(tk // LANES):
                    if c < len(cols):
                        e = jnp.exp2(blk(c) - m)
                        if c >= n_full:
                            e = jnp.where(first_half, e, 0.0)
                        acc = e if acc is None else acc + e
                    else:
                        e = jnp.zeros((CHUNK, LANES), F32)
                    e_scr[h, rows, c * LANES:(c + 1) * LANES] = e.astype(BF16)
                r_scr[h, rows, :] = 1.0 / _row_stat(jnp.sum, acc)
            outs.append(_dot(e_scr[h, :, 0:tk], v_ref[0, 0:tk, :]) * r_scr[h])
        o_ref[0, i * qb:(i + 1) * qb, :] = jnp.where(lo, outs[0], outs[1]).astype(BF16)


def _mla_prompt(q, k, v):
    nb, seq, _ = q.shape
    qb = min(MLA_QBLOCK, seq)
    return pl.pallas_call(
        functools.partial(_mla_prompt_kernel, seq=seq),
        grid=(nb, D_HEADS // 2),
        in_specs=[pl.BlockSpec((1, seq, 2 * LANES), lambda b, p: (b, 0, p)),
                  pl.BlockSpec((1, seq, 2 * LANES), lambda b, p: (b, 0, p)),
                  pl.BlockSpec((1, seq, LANES), lambda b, p: (b, 0, p))],
        out_specs=pl.BlockSpec((1, seq, LANES), lambda b, p: (b, 0, p)),
        out_shape=jax.ShapeDtypeStruct((nb, seq, D_HEADS * D_V), BF16),
        scratch_shapes=[pltpu.VMEM((2, qb, seq), F32), pltpu.VMEM((2, qb, seq), BF16),
                        pltpu.VMEM((2, qb, LANES), F32)],
        compiler_params=_cparams("arbitrary", "arbitrary"),
        name="mla_prompt",
    )(q, k, v)


def _mla_sample_kernel(q_ref, kc_ref, vc_ref, kn_ref, vn_ref, o_ref):
    nq = q_ref.shape[1]
    lo = lax.broadcasted_iota(jnp.int32, (nq, LANES), 1) < D_V
    outs = []
    for h in range(2):
        cols = slice(h * LANES, (h + 1) * LANES)
        q = q_ref[0, :, cols]
        s1 = lax.dot_general(q, kc_ref[0, :, cols], _NT, preferred_element_type=F32)
        s2 = lax.dot_general(q, kn_ref[0, :, cols], _NT, preferred_element_type=F32)
        m = jnp.maximum(jnp.max(s1, axis=-1, keepdims=True), jnp.max(s2, axis=-1, keepdims=True))
        e1 = jnp.exp2(s1 - m)
        e2 = jnp.exp2(s2 - m)
        den = jnp.sum(e1, axis=-1, keepdims=True) + jnp.sum(e2, axis=-1, keepdims=True)
        outs.append((_dot(e1.astype(BF16), vc_ref[0]) + _dot(e2.astype(BF16), vn_ref[0])) / den)
    o_ref[0] = jnp.where(lo, outs[0], outs[1]).astype(BF16)


def _mla_sample(q, kc, vc, kn, vn):
    nb, nq, _ = q.shape
    past = kc.shape[1]
    two = lambda r: pl.BlockSpec((1, r, 2 * LANES), lambda b, p: (b, 0, p))
    one = lambda r: pl.BlockSpec((1, r, LANES), lambda b, p: (b, 0, p))
    return pl.pallas_call(
        _mla_sample_kernel,
        grid=(nb, D_HEADS // 2),
        in_specs=[two(nq), two(past), one(past), two(nq), one(nq)],
        out_specs=one(nq),
        out_shape=jax.ShapeDtypeStruct((nb, nq, D_HEADS * D_V), BF16),
        compiler_params=_cparams("arbitrary", "arbitrary"),
        name="mla_sample",
    )(q, kc, vc, kn, vn)


def _even_layer(xp, xs, nb, seq, ndb, dseq, norm_g, w_in, w_out, a_qn, a_kn, a_sinks, b_qn, b_kn, b_rel,
                t5_table, ck_a, cv_a, ck_b, cv_b, ffn2):
    perm = _pair_perm()
    w_in_p = jnp.concatenate([w_in[:, :A_Q][:, perm], w_in[:, A_Q:]], axis=1).astype(BF16)
    woa = w_out[:A_Q][perm].astype(BF16)
    wob = w_out[A_Q:].astype(BF16)
    ones = lambda n: jnp.ones((n,), F32)
    scale = HEAD_DIM ** -0.5
    gain_row = jnp.concatenate([jnp.tile(a_qn, A_HEADS) * scale, jnp.tile(a_kn, A_KV_HEADS), ones(A_KV),
                                jnp.tile(b_qn, B_HEADS) * scale, jnp.tile(b_kn, B_HEADS), ones(B_QKV)]
                               ).reshape(1, EVEN_IN).astype(F32)
    bias_a = _bias_a(t5_table)
    bias_b = _bias_b(b_rel)
    sink = jnp.broadcast_to(jnp.repeat(a_sinks.astype(F32), PAIR_ROWS)[:, None], (A_HEADS * PAIR_ROWS, LANES))
    la, lb = min(WINDOW, seq), min(B_REACH, seq)

    aq, ak, av, bq, bk, bv, cak, cav, cbk, cbv = _even_inproj(xp, norm_g, w_in_p, gain_row, nb, la, lb)
    r3 = lambda a: a.reshape(nb, seq, a.shape[-1])
    ya = _attn_a(r3(aq), r3(ak), r3(av), bias_a, sink, WINDOW, True)
    yb = _attn_b(r3(bq), r3(bk), r3(bv), bias_b, B_REACH, True)
    xp = _ffn(xp, *ffn2, out_proj=(ya.reshape(nb * seq, A_Q), yb.reshape(nb * seq, B_QKV), woa, wob))
    st_p = (cak.reshape(nb, la, A_KV_HEADS, HEAD_DIM), cav.reshape(nb, la, A_KV_HEADS, HEAD_DIM),
            cbk.reshape(nb, lb, B_HEADS, HEAD_DIM), cbv.reshape(nb, lb, B_HEADS, HEAD_DIM))

    ts = ndb * dseq
    aq, ak, av, bq, bk, bv, nak, nav, nbk, nbv = _even_inproj(xs, norm_g, w_in_p, gain_row, 1, ts, ts)
    pad_q = lambda a: jnp.pad(a.reshape(ndb, dseq, a.shape[-1]), ((0, 0), (0, PAIR_ROWS - dseq), (0, 0)))

    def window(cache, new):
        w = cache.shape[-2] * cache.shape[-1]
        full = jnp.concatenate([cache.reshape(ndb, -1, w), new.reshape(ndb, dseq, w)], axis=1)
        buf = jnp.pad(full, ((0, 0), (0, PAIR_ROWS - dseq), (0, 0))).astype(BF16)
        return full[:, dseq:].reshape(cache.shape), buf

    st_ak, kbuf_a = window(ck_a, nak)
    st_av, vbuf_a = window(cv_a, nav)
    st_bk, kbuf_b = window(ck_b, nbk)
    st_bv, vbuf_b = window(cv_b, nbv)
    ya = _attn_a(pad_q(aq), kbuf_a, vbuf_a, bias_a, sink, WINDOW, False)[:, :dseq]
    yb = _attn_b(pad_q(bq), kbuf_b, vbuf_b, bias_b, B_REACH, False)[:, :dseq]
    xs = _ffn(xs, *ffn2, out_proj=(ya.reshape(ts, A_Q), yb.reshape(ts, B_QKV), woa, wob))
    return xp, xs, st_p, (st_ak, st_av, st_bk, st_bv)


def _odd_layer(xp, xs, nb, seq, ndb, dseq, past, norm_g, ow, w_out, conv_prev, c_ckv, c_kpe, ffn2):
    woc = w_out[:C_WIDTH].astype(BF16)
    wod = w_out[C_WIDTH:].astype(BF16)

    zero_init = jnp.zeros((nb, 8, C_WIDTH), F32)
    yc, q, k, v, ckv, kpe, cs = _odd_inproj(xp, norm_g, ow, zero_init, nb, 0)
    r3 = lambda a: a.reshape(nb, seq, a.shape[-1])
    yd = _mla_prompt(r3(q), r3(k), r3(v))
    xp = _ffn(xp, *ffn2, out_proj=(yc, yd.reshape(nb * seq, D_HEADS * D_V), woc, wod))
    st_p = (cs, ckv.reshape(nb, seq, D_KV_LORA), kpe.reshape(nb, seq, D_ROPE))

    ts = ndb * dseq
    init = jnp.pad(conv_prev.astype(F32), ((0, 0), (8 - (CONV_W - 1), 0), (0, 0)))
    yc, q, kn, vn, ckv, kpe, cs = _odd_inproj(xs, norm_g, ow, init, ndb, past)
    kpe_blk = jnp.pad(c_kpe.reshape(ndb * past, D_ROPE), ((0, 0), (ROPE_LO, LANES - ROPE_HI)))
    kc, vc = _kv_expand(c_ckv.reshape(ndb * past, D_KV_LORA), kpe_blk, ow)
    s3 = lambda a: a.reshape(ndb, dseq, a.shape[-1])
    yd = _mla_sample(s3(q), kc.reshape(ndb, past, -1), vc.reshape(ndb, past, -1), s3(kn), s3(vn))
    xs = _ffn(xs, *ffn2, out_proj=(yc, yd.reshape(ts, D_HEADS * D_V), woc, wod))
    st_s = (cs, ckv.reshape(ndb, dseq, D_KV_LORA), kpe.reshape(ndb, dseq, D_ROPE))
    return xp, xs, st_p, st_s


def kernel(x_prompt, x_sample, cache_a_k, cache_a_v, cache_b_k, cache_b_v, state_c_conv, cache_d_ckv, cache_d_kpe, ff1_norm, ff1_w_gu, ff1_w_down, mix_norm, ff2_norm, ff2_w_gu, ff2_w_down, t5_bias_table, ev_w_in, ev_w_out, a_q_norm, a_k_norm, a_sinks, b_q_norm, b_k_norm, b_rel_bias, od_w_in, od_w_out, c_conv_w, d_q_a_norm, d_w_q_b, d_kv_a_norm, d_w_kv_b, d_q_nope_norm, d_q_rope_norm, d_k_nope_norm, d_k_rope_norm):
    nb, seq, _ = x_prompt.shape
    ndb, dseq, _ = x_sample.shape
    past = cache_d_ckv.shape[2]
    depth = ff1_norm.shape[0]
    assert seq % TOKEN_TILE == 0 and dseq == CHUNK and past % CHUNK == 0
    xp = x_prompt.reshape(nb * seq, D_MODEL)
    xs = x_sample.reshape(ndb * dseq, D_MODEL)
    even_p, even_s, odd_p, odd_s = [], [], [], []
    for l in range(depth):
        i = l // 2
        ff1 = _ffn_weights(ff1_w_gu[l], ff1_w_down[l])
        ffn2 = (ff2_norm[l],) + _ffn_weights(ff2_w_gu[l], ff2_w_down[l])
        xp = _ffn(xp, ff1_norm[l], *ff1)
        xs = _ffn(xs, ff1_norm[l], *ff1)
        if l % 2 == 0:
            xp, xs, sp, ss = _even_layer(
                xp, xs, nb, seq, ndb, dseq, mix_norm[l], ev_w_in[i], ev_w_out[i], a_q_norm[i], a_k_norm[i],
                a_sinks[i], b_q_norm[i], b_k_norm[i], b_rel_bias[i], t5_bias_table,
                cache_a_k[i], cache_a_v[i], cache_b_k[i], cache_b_v[i], ffn2)
            even_p.append(sp)
            even_s.append(ss)
        else:
            ow = _odd_weights(od_w_in[i], c_conv_w[i], d_q_a_norm[i], d_w_q_b[i], d_kv_a_norm[i], d_w_kv_b[i],
                              d_q_nope_norm[i], d_q_rope_norm[i], d_k_nope_norm[i], d_k_rope_norm[i])
            xp, xs, sp, ss = _odd_layer(xp, xs, nb, seq, ndb, dseq, past, mix_norm[l], ow, od_w_out[i],
                                        state_c_conv[i], cache_d_ckv[i], cache_d_kpe[i], ffn2)
            odd_p.append(sp)
            odd_s.append(ss)
    stack = lambda group, j: jnp.stack([g[j] for g in group])
    return (xp.reshape(nb, seq, D_MODEL), xs.reshape(ndb, dseq, D_MODEL),
            stack(even_p, 0), stack(even_p, 1), stack(even_p, 2), stack(even_p, 3),
            stack(odd_p, 0), stack(odd_p, 1), stack(odd_p, 2),
            stack(even_s, 0), stack(even_s, 1), stack(even_s, 2), stack(even_s, 3),
            stack(odd_s, 0), stack(odd_s, 1), stack(odd_s, 2))
```

```python
import functools
import math

import jax
import jax.numpy as jnp
from jax import lax
from jax.experimental import pallas as pl
from jax.experimental.pallas import tpu as pltpu

F32 = jnp.float32
BF16 = jnp.bfloat16

D_MODEL = 1024
CHUNK = 64
HEAD_DIM = 64
EPS = 1e-6
A_HEADS = 8
A_KV_HEADS = 2
WINDOW = 128
T5_BUCKETS = 32
T5_MAX_DIST = 128
B_HEADS = 8
B_REACH = 512
B_MAX_REL = 128
C_WIDTH = 512
CONV_W = 3
D_HEADS = 8
D_Q_LORA = 256
D_KV_LORA = 128
D_NOPE = 64
D_ROPE = 32
D_V = 64
ROPE_THETA = 10000.0
FFN_DIM = 2816
A_Q = A_HEADS * HEAD_DIM
A_KV = A_KV_HEADS * HEAD_DIM
B_QKV = B_HEADS * HEAD_DIM
EVEN_IN = A_Q + 2 * A_KV + 3 * B_QKV

LANES = 128
TOKEN_TILE = 512
MXU_COLS = 256
FFN_CHUNK = MXU_COLS
PAIR_ROWS = 2 * CHUNK
MLA_QBLOCK = 256
VMEM_LIMIT_BYTES = 56 * 1024 * 1024
MLA_QSCALE = (D_NOPE + D_ROPE) ** -0.5 * math.log2(math.e)
NEG_INF = float("-inf")

_NT = (((1,), (1,)), ((), ()))


def _cparams(*sem):
    return pltpu.CompilerParams(dimension_semantics=sem, vmem_limit_bytes=VMEM_LIMIT_BYTES)


def _resident(shape):
    zeros = (0,) * len(shape)
    return pl.BlockSpec(shape, lambda *_: zeros, pipeline_mode=pl.Buffered(1))


def _rms(x, g):
    ms = jnp.mean(x * x, axis=-1, keepdims=True)
    return (x * lax.rsqrt(ms + EPS)) * g


def _dot(a, b):
    return jnp.dot(a, b, preferred_element_type=F32)


def _ffn_kernel(*refs, n_chunks, fused_out):
    if fused_out:
        (x_ref, ya_ref, yb_ref, woa_ref, wob_ref, g_ref, wgu_ref, wd_ref,
         o_ref, h_scr, a_scr) = refs
        y = _dot(ya_ref[...], woa_ref[...]) + _dot(yb_ref[...], wob_ref[...])
        o_ref[...] = x_ref[...] + y
        res_ref = o_ref
    else:
        x_ref, g_ref, wgu_ref, wd_ref, o_ref, h_scr, a_scr = refs
        res_ref = x_ref
    h_scr[...] = _rms(res_ref[...], g_ref[...]).astype(BF16)
    for j in range(n_chunks):
        h = h_scr[...]
        g = _dot(h, wgu_ref[:, j * FFN_CHUNK:(j + 1) * FFN_CHUNK])
        u = _dot(h, wgu_ref[:, FFN_DIM + j * FFN_CHUNK:FFN_DIM + (j + 1) * FFN_CHUNK])
        a_scr[:, j * FFN_CHUNK:(j + 1) * FFN_CHUNK] = ((g * jax.nn.sigmoid(g)) * u).astype(BF16)
    o_ref[...] = res_ref[...] + 0.5 * _dot(a_scr[...], wd_ref[...])


def _ffn(x, norm_g, wgu, wd, out_proj=None):
    t = x.shape[0]
    tm = min(TOKEN_TILE, t)
    n_chunks = FFN_DIM // FFN_CHUNK
    row = lambda w: pl.BlockSpec((tm, w), lambda i: (i, 0))
    in_specs = [row(D_MODEL)]
    args = [x]
    if out_proj is not None:
        ya, yb, woa, wob = out_proj
        in_specs += [row(ya.shape[1]), row(yb.shape[1]), _resident(woa.shape), _resident(wob.shape)]
        args += [ya, yb, woa, wob]
    in_specs += [_resident((1, D_MODEL)), _resident(wgu.shape), _resident(wd.shape)]
    args += [norm_g.reshape(1, D_MODEL), wgu, wd]
    return pl.pallas_call(
        functools.partial(_ffn_kernel, n_chunks=n_chunks, fused_out=out_proj is not None),
        grid=(t // tm,),
        in_specs=in_specs,
        out_specs=row(D_MODEL),
        out_shape=jax.ShapeDtypeStruct((t, D_MODEL), F32),
        scratch_shapes=[pltpu.VMEM((tm, D_MODEL), BF16), pltpu.VMEM((tm, FFN_DIM), BF16)],
        compiler_params=_cparams("arbitrary"),
        name="ffn_out" if out_proj is not None else "ffn",
    )(*args)


def _ffn_weights(w_gu, w_down):
    return w_gu.astype(BF16), w_down.astype(BF16)


def _norm_halves(blk, gain):
    sq = blk * blk
    lo = lax.broadcasted_iota(jnp.int32, blk.shape, 1) < HEAD_DIM
    s_lo = jnp.sum(jnp.where(lo, sq, 0.0), axis=-1, keepdims=True)
    s_hi = jnp.sum(jnp.where(lo, 0.0, sq), axis=-1, keepdims=True)
    inv = jnp.where(lo, lax.rsqrt(s_lo * (1.0 / HEAD_DIM) + EPS), lax.rsqrt(s_hi * (1.0 / HEAD_DIM) + EPS))
    return (blk * inv) * gain


def _even_inproj_kernel(x_ref, g_ref, w_ref, gain_ref,
                        aq_ref, ak_ref, av_ref, bq_ref, bk_ref, bv_ref,
                        cak_ref, cav_ref, cbk_ref, cbv_ref, h_scr, z_scr, *, tm, tpb, rows_a, rows_b):
    i = pl.program_id(0)
    last = (i % tpb) == (tpb - 1)
    h_scr[...] = _rms(x_ref[...], g_ref[...]).astype(BF16)

    def lanes(c):
        return slice(c * LANES, (c + 1) * LANES)

    def emit(c):
        blk = z_scr[:, lanes(c)]
        gain = gain_ref[:, lanes(c)]
        if c < 4:
            aq_ref[:, lanes(c)] = _norm_halves(blk, gain).astype(BF16)
        elif c == 4:
            kn = _norm_halves(blk, gain)
            ak_ref[...] = kn.astype(BF16)
            z_scr[:, lanes(c)] = kn
        elif c == 5:
            av_ref[...] = blk.astype(BF16)
        elif c < 10:
            bq_ref[:, lanes(c - 6)] = _norm_halves(blk, gain).astype(BF16)
        elif c < 14:
            kn = _norm_halves(blk, gain)
            bk_ref[:, lanes(c - 10)] = kn.astype(BF16)
            z_scr[:, lanes(c)] = kn
        else:
            bv_ref[:, lanes(c - 14)] = blk.astype(BF16)

    for grp in range(EVEN_IN // MXU_COLS):
        cols = slice(grp * MXU_COLS, (grp + 1) * MXU_COLS)
        z_scr[:, cols] = _dot(h_scr[...], w_ref[:, cols])
        emit(2 * grp)
        emit(2 * grp + 1)

    @pl.when(last)
    def _():
        cak_ref[0] = z_scr[tm - rows_a:, lanes(4)]
        cav_ref[0] = z_scr[tm - rows_a:, lanes(5)]
        cbk_ref[0] = z_scr[tm - rows_b:, 10 * LANES:14 * LANES]
        cbv_ref[0] = z_scr[tm - rows_b:, 14 * LANES:18 * LANES]


def _even_inproj(x, norm_g, w_in, gain_row, n_batch, rows_a, rows_b):
    t = x.shape[0]
    tm = min(TOKEN_TILE, t)
    tpb = t // n_batch // tm
    assert tpb * tm * n_batch == t and rows_a <= tm and rows_b <= tm
    row = lambda w: pl.BlockSpec((tm, w), lambda i: (i, 0))
    cache = lambda r, w: pl.BlockSpec((1, r, w), lambda i: (i // tpb, 0, 0))
    bshape = lambda w: jax.ShapeDtypeStruct((t, w), BF16)
    cshape = lambda r, w: jax.ShapeDtypeStruct((n_batch, r, w), F32)
    return pl.pallas_call(
        functools.partial(_even_inproj_kernel, tm=tm, tpb=tpb, rows_a=rows_a, rows_b=rows_b),
        grid=(t // tm,),
        in_specs=[row(D_MODEL), _resident((1, D_MODEL)), _resident(w_in.shape), _resident(gain_row.shape)],
        out_specs=[row(A_Q), row(A_KV), row(A_KV), row(B_QKV), row(B_QKV), row(B_QKV),
                   cache(rows_a, A_KV), cache(rows_a, A_KV), cache(rows_b, B_QKV), cache(rows_b, B_QKV)],
        out_shape=[bshape(A_Q), bshape(A_KV), bshape(A_KV), bshape(B_QKV), bshape(B_QKV), bshape(B_QKV),
                   cshape(rows_a, A_KV), cshape(rows_a, A_KV), cshape(rows_b, B_QKV), cshape(rows_b, B_QKV)],
        scratch_shapes=[pltpu.VMEM((tm, D_MODEL), BF16), pltpu.VMEM((tm, EVEN_IN), F32)],
        compiler_params=_cparams("arbitrary"),
        name="even_inproj",
    )(x, norm_g.reshape(1, D_MODEL), w_in, gain_row)


def _pair_perm():
    idx = []
    for p in range(4):
        for half in range(2):
            head = p + 4 * half
            idx.extend(range(head * HEAD_DIM, (head + 1) * HEAD_DIM))
    return jnp.asarray(idx, dtype=jnp.int32)


def _fill_padded(buf, src_ref, ctx):
    buf[0:ctx, :] = jnp.zeros((ctx, buf.shape[1]), buf.dtype)
    buf[ctx:, :] = src_ref[0]


def _windows(k_ref, v_ref, pad_scr, ctx, pad_front):
    win = ctx + PAIR_ROWS
    if pad_front:
        kbuf, vbuf = pad_scr
        _fill_padded(kbuf, k_ref, ctx)
        _fill_padded(vbuf, v_ref, ctx)
        return (lambda r0: kbuf[pl.ds(r0, win), :]), (lambda r0: vbuf[pl.ds(r0, win), :])
    return (lambda r0: k_ref[0, pl.ds(r0, win), :]), (lambda r0: v_ref[0, pl.ds(r0, win), :])


def _fold_lanes(x, op):
    acc = x[:, :LANES]
    for c in range(1, x.shape[1] // LANES):
        acc = op(acc, x[:, c * LANES:(c + 1) * LANES])
    return acc


def _row_stat(reduce_fn, x):
    return jnp.broadcast_to(reduce_fn(x, axis=-1, keepdims=True), x.shape)


PAIR_SLOTS = 4


def _run_pairs(pair, n_steps, n_front):
    for t in range(n_steps):
        pair(t * PAIR_ROWS, t % PAIR_SLOTS, t < n_front)


def _attn_a_kernel(q_ref, k_ref, v_ref, bias_ref, sink_ref, o_ref, s_scr, e_scr, *pad_scr, nq, ctx, pad_front):
    win = ctx + PAIR_ROWS
    kwin, vwin = _windows(k_ref, v_ref, pad_scr, ctx, pad_front)
    lo = lax.broadcasted_iota(jnp.int32, (PAIR_ROWS, LANES), 1) < HEAD_DIM

    def pair(r0, slot, masked):
        q_lo, q_hi = [], []
        for p in range(4):
            qp = q_ref[0, pl.ds(r0, PAIR_ROWS), p * LANES:(p + 1) * LANES]
            zero = jnp.zeros_like(qp)
            q_lo.append(jnp.where(lo, qp, zero))
            q_hi.append(jnp.where(lo, zero, qp))
        q2 = jnp.concatenate(q_lo + q_hi, axis=0)
        s_scr[slot] = lax.dot_general(q2, kwin(r0), _NT, preferred_element_type=F32)
        for head in range(A_HEADS):
            rows = slice(head * PAIR_ROWS, (head + 1) * PAIR_ROWS)
            s = s_scr[slot, rows, :] + bias_ref[rows, :]
            if masked:
                col = lax.broadcasted_iota(jnp.int32, (PAIR_ROWS, win), 1)
                s = jnp.where(col + r0 >= ctx, s, NEG_INF)
            sk = sink_ref[rows, :]
            m = jnp.maximum(_row_stat(jnp.max, _fold_lanes(s, jnp.maximum)), sk)
            e = [jnp.exp(s[:, c * LANES:(c + 1) * LANES] - m) for c in range(win // LANES)]
            den = _row_stat(jnp.sum, functools.reduce(jnp.add, e)) + jnp.exp(sk - m)
            inv = 1.0 / den
            for c in range(win // LANES):
                e_scr[slot, rows, c * LANES:(c + 1) * LANES] = (e[c] * inv).astype(BF16)
        o2 = _dot(e_scr[slot], vwin(r0))
        half = 4 * PAIR_ROWS
        for p in range(4):
            o = jnp.where(lo, o2[p * PAIR_ROWS:(p + 1) * PAIR_ROWS],
                          o2[half + p * PAIR_ROWS:half + (p + 1) * PAIR_ROWS])
            o_ref[0, pl.ds(r0, PAIR_ROWS), p * LANES:(p + 1) * LANES] = o.astype(BF16)

    _run_pairs(pair, nq // PAIR_ROWS, ctx // PAIR_ROWS if pad_front else 0)


def _attn_a(q, k, v, bias, sink, ctx, pad_front):
    nb, nq, _ = q.shape
    nk = k.shape[1]
    assert nq % PAIR_ROWS == 0 and nk == (nq if pad_front else ctx + nq)
    per_b = lambda r, w: pl.BlockSpec((1, r, w), lambda b: (b, 0, 0))
    rows, win = A_HEADS * PAIR_ROWS, ctx + PAIR_ROWS
    scratch = [pltpu.VMEM((PAIR_SLOTS, rows, win), F32), pltpu.VMEM((PAIR_SLOTS, rows, win), BF16)]
    if pad_front:
        scratch += [pltpu.VMEM((ctx + nq, A_KV), BF16)] * 2
    return pl.pallas_call(
        functools.partial(_attn_a_kernel, nq=nq, ctx=ctx, pad_front=pad_front),
        grid=(nb,),
        in_specs=[per_b(nq, A_Q), per_b(nk, A_KV), per_b(nk, A_KV), _resident(bias.shape), _resident(sink.shape)],
        out_specs=per_b(nq, A_Q),
        out_shape=jax.ShapeDtypeStruct((nb, nq, A_Q), BF16),
        scratch_shapes=scratch,
        compiler_params=_cparams("arbitrary"),
        name="attn_a",
    )(q, k, v, bias, sink)


def _attn_b_kernel(q_ref, k_ref, v_ref, bias_ref, o_ref, s_scr, e_scr, r_scr, *pad_scr, nq, ctx, pad_front):
    win = ctx + PAIR_ROWS
    kwin, vwin = _windows(k_ref, v_ref, pad_scr, ctx, pad_front)
    lo = lax.broadcasted_iota(jnp.int32, (PAIR_ROWS, LANES), 1) < HEAD_DIM

    def pair(r0, slot, masked):
        qp = q_ref[0, pl.ds(r0, PAIR_ROWS), :]
        zero = jnp.zeros_like(qp)
        q2 = jnp.concatenate([jnp.where(lo, qp, zero), jnp.where(lo, zero, qp)], axis=0)
        s_scr[slot] = lax.dot_general(q2, kwin(r0), _NT, preferred_element_type=F32)
        for rb in range(2 * PAIR_ROWS // CHUNK):
            rows = slice(rb * CHUNK, (rb + 1) * CHUNK)
            s = s_scr[slot, rows, :] + bias_ref[0, rows, :]
            if masked:
                col = lax.broadcasted_iota(jnp.int32, (CHUNK, win), 1)
                s = jnp.where(col + r0 >= ctx, s, NEG_INF)
            m = _row_stat(jnp.max, _fold_lanes(s, jnp.maximum))
            e = [jnp.exp(s[:, c * LANES:(c + 1) * LANES] - m) for c in range(win // LANES)]
            for c in range(win // LANES):
                e_scr[slot, rows, c * LANES:(c + 1) * LANES] = e[c].astype(BF16)
            r_scr[slot, rows, :] = 1.0 / _row_stat(jnp.sum, functools.reduce(jnp.add, e))
        o2 = _dot(e_scr[slot], vwin(r0)) * r_scr[slot]
        o = jnp.where(lo, o2[:PAIR_ROWS], o2[PAIR_ROWS:])
        o_ref[0, pl.ds(r0, PAIR_ROWS), :] = o.astype(BF16)

    _run_pairs(pair, nq // PAIR_ROWS, ctx // PAIR_ROWS if pad_front else 0)


def _attn_b(q, k, v, bias, ctx, pad_front):
    nb, nq, _ = q.shape
    nk = k.shape[1]
    assert nq % PAIR_ROWS == 0 and nk == (nq if pad_front else ctx + nq)
    blk = lambda r: pl.BlockSpec((1, r, LANES), lambda b, p: (b, 0, p))
    rows, win = 2 * PAIR_ROWS, ctx + PAIR_ROWS
    scratch = [pltpu.VMEM((PAIR_SLOTS, rows, win), F32), pltpu.VMEM((PAIR_SLOTS, rows, win), BF16),
               pltpu.VMEM((PAIR_SLOTS, rows, LANES), F32)]
    if pad_front:
        scratch += [pltpu.VMEM((ctx + nq, LANES), BF16)] * 2
    return pl.pallas_call(
        functools.partial(_attn_b_kernel, nq=nq, ctx=ctx, pad_front=pad_front),
        grid=(nb, B_HEADS // 2),
        in_specs=[blk(nq), blk(nk), blk(nk),
                  pl.BlockSpec((1,) + bias.shape[1:], lambda b, p: (p, 0, 0))],
        out_specs=blk(nq),
        out_shape=jax.ShapeDtypeStruct((nb, nq, B_QKV), BF16),
        scratch_shapes=scratch,
        compiler_params=_cparams("arbitrary", "arbitrary"),
        name="attn_b",
    )(q, k, v, bias)


def _t5_bucket(rel):
    nb = T5_BUCKETS // 2
    max_exact = nb // 2
    n = -rel
    ret = jnp.where(n < 0, nb, 0)
    n = jnp.abs(n)
    nf = jnp.maximum(n, 1).astype(F32)
    large = max_exact + (jnp.log(nf / max_exact) / math.log(T5_MAX_DIST / max_exact)
                         * (nb - max_exact)).astype(jnp.int32)
    large = jnp.minimum(large, nb - 1)
    return ret + jnp.where(n < max_exact, n, large)


def _band_tables(ctx, value_of_rel):
    n_cols = ctx + PAIR_ROWS
    period = n_cols + PAIR_ROWS - 1
    d = jnp.concatenate([jnp.arange(0, n_cols), jnp.arange(-(PAIR_ROWS - 1), 0)])
    ext = value_of_rel(d - ctx)
    heads = ext.shape[0]
    flat = jnp.tile(ext, (1, PAIR_ROWS))[:, :PAIR_ROWS * (period - 1)]
    tb = flat.reshape(heads, PAIR_ROWS, period - 1)[:, :, :n_cols]
    i = jnp.arange(PAIR_ROWS)[:, None]
    j = jnp.arange(n_cols)[None, :]
    visible = jnp.where(i < CHUNK, j < ctx + CHUNK, j >= CHUNK)
    return jnp.where(visible[None], tb, NEG_INF)


def _bias_a(t5_table):
    tb = _band_tables(WINDOW, lambda rel: t5_table.astype(F32)[_t5_bucket(rel)].T)
    return tb.reshape(A_HEADS * PAIR_ROWS, WINDOW + PAIR_ROWS)


def _bias_b(b_rel):
    tb = _band_tables(B_REACH, lambda rel: b_rel.astype(F32)[:, jnp.clip(rel, -B_MAX_REL, B_MAX_REL) + B_MAX_REL])
    return tb.reshape(B_HEADS // 2, 2 * PAIR_ROWS, B_REACH + PAIR_ROWS)


ODD_IN_PAD = 2048
ROPE_HALF = D_ROPE // 2
X1_LO = LANES // 2
NOPE_SPLIT = X1_LO - ROPE_HALF


def _head_lane_source():
    zero = D_NOPE + D_ROPE
    src = []
    for lane in range(LANES):
        if lane < ROPE_HALF:
            src.append(D_NOPE + ROPE_HALF + lane)
        elif lane < X1_LO:
            src.append(lane - ROPE_HALF)
        elif lane < X1_LO + ROPE_HALF:
            src.append(D_NOPE + lane - X1_LO)
        elif lane < X1_LO + ROPE_HALF + D_NOPE - NOPE_SPLIT:
            src.append(NOPE_SPLIT + lane - X1_LO - ROPE_HALF)
        else:
            src.append(zero)
    return src


def _to_head_lanes(a):
    padded = jnp.concatenate([a, jnp.zeros(a.shape[:-1] + (1,), a.dtype)], axis=-1)
    return padded[..., jnp.asarray(_head_lane_source(), dtype=jnp.int32)]


def _lane_masks(shape):
    lane = lax.broadcasted_iota(jnp.int32, shape, 1)
    rope = (lane < ROPE_HALF) | ((lane >= X1_LO) & (lane < X1_LO + ROPE_HALF))
    used = lane < X1_LO + ROPE_HALF + D_NOPE - NOPE_SPLIT
    return used & ~rope, rope


def _rope(y, cosf, sinf):
    return y * cosf + pltpu.roll(y, LANES // 2, 1) * sinf


ROW_BLOCK = 128


def _row_blocks(tm):
    rb = min(ROW_BLOCK, tm)
    return [slice(r, r + rb) for r in range(0, tm, rb)]


def _segment_mean_squares(src_scr, cols, seg_ref, sq_scr, ms_scr, slot):
    for rows in _row_blocks(src_scr.shape[0]):
        v = src_scr[rows, cols]
        sq_scr[slot, rows, :] = (v * v).astype(BF16)
    ms_scr[slot] = _dot(sq_scr[slot], seg_ref[...])


def _expand_kv(ckvb_ref, kpe_blk_ref, wkv_ref, kgain_ref, kseg_ref, kv_scr, sq_scr, ms_scr, k_ref, v_ref,
               between=None):
    k_cols = D_HEADS * LANES
    tm = kv_scr.shape[0]
    for grp in range(kv_scr.shape[1] // MXU_COLS):
        if between and grp in between:
            between[grp]()
        cols = slice(grp * MXU_COLS, (grp + 1) * MXU_COLS)
        kv_scr[:, cols] = _dot(ckvb_ref[...], wkv_ref[:, cols])
        if cols.start < k_cols:
            slot = grp % 2
            _segment_mean_squares(kv_scr, cols, kseg_ref, sq_scr, ms_scr, slot)
        for rows in _row_blocks(tm):
            if cols.start < k_cols:
                for half in range(2):
                    h = 2 * grp + half
                    kb = kv_scr[rows, h * LANES:(h + 1) * LANES]
                    ms = ms_scr[slot, rows, half * LANES:(half + 1) * LANES]
                    kn = (kb * lax.rsqrt(ms + EPS)) * kgain_ref[...]
                    k_ref[rows, h * LANES:(h + 1) * LANES] = (kn + kpe_blk_ref[rows, :]).astype(BF16)
            else:
                v_ref[rows, cols.start - k_cols:cols.stop - k_cols] = kv_scr[rows, cols].astype(BF16)


def _odd_inproj_kernel(x_ref, g_ref, w_ref, cinit_ref, convw_ref, qan_ref, wqb_ref, qgain_ref,
                       kvan_ref, krgain_ref, invf_ref, sgn_ref, wkv_ref, kgain_ref, qseg_ref, kseg_ref,
                       yc_ref, q_ref, k_ref, v_ref, ckv_ref, kpe_ref, cs_ref,
                       h_scr, z_scr, uext_scr, qn_scr, q_scr, kv_scr, rot_scr, ckvb_scr, kpe_scr, sq_scr, ms_scr,
                       *, tm, tpb, pos0):
    i = pl.program_id(0)
    tile = i % tpb
    off = pl.multiple_of(tile * tm, tm)
    blocks = _row_blocks(tm)
    rb = blocks[0].stop

    @pl.when(tile == 0)
    def _():
        uext_scr[0:8, :] = cinit_ref[0]

    @pl.when(i < tpb)
    def _():
        row = lax.broadcasted_iota(jnp.int32, (tm, LANES), 0)
        ang = (row + (tile * tm + pos0)).astype(F32) * invf_ref[...]
        rot_scr[0, pl.ds(off, tm), :] = jnp.cos(ang)
        rot_scr[1, pl.ds(off, tm), :] = jnp.sin(ang) * sgn_ref[...]

    h_scr[...] = _rms(x_ref[...], g_ref[...]).astype(BF16)
    low_lanes = lax.broadcasted_iota(jnp.int32, (rb, LANES), 1) < ROPE_HALF

    def proj(c0, c1):
        z_scr[:, c0:c1] = _dot(h_scr[...], w_ref[:, c0:c1])

    def rotary(y, rows):
        cosf = rot_scr[0, pl.ds(off + rows.start, rb), :]
        sinf = rot_scr[1, pl.ds(off + rows.start, rb), :]
        return _rope(y, cosf, sinf)

    def conv_group(grp):
        c0 = grp * MXU_COLS
        cols = slice(c0, c0 + MXU_COLS)
        for base in (0, C_WIDTH, 2 * C_WIDTH):
            proj(base + c0, base + c0 + MXU_COLS)
        for rows in blocks:
            up = slice(rows.start + 8, rows.stop + 8)
            uext_scr[up, cols] = (z_scr[rows, C_WIDTH + c0:C_WIDTH + c0 + MXU_COLS]
                                  * z_scr[rows, 2 * C_WIDTH + c0:2 * C_WIDTH + c0 + MXU_COLS])
        for rows in blocks:
            yconv = convw_ref[0:1, cols] * uext_scr[rows.start + 6:rows.stop + 6, cols]
            yconv = yconv + convw_ref[1:2, cols] * uext_scr[rows.start + 7:rows.stop + 7, cols]
            yconv = yconv + convw_ref[2:3, cols] * uext_scr[rows.start + 8:rows.stop + 8, cols]
            yc_ref[rows, cols] = (z_scr[rows, cols] * yconv).astype(BF16)

    proj(1536, 1792)
    proj(1792, 2048)

    for rows in blocks:
        qn_scr[rows, :] = _rms(z_scr[rows, 1536:1792], qan_ref[...]).astype(BF16)
        ckv = _rms(z_scr[rows, 1792:1920], kvan_ref[...])
        ckv_ref[rows, :] = ckv
        ckvb_scr[rows, :] = ckv.astype(BF16)

    for grp in range(D_HEADS * LANES // MXU_COLS):
        cols = slice(grp * MXU_COLS, (grp + 1) * MXU_COLS)
        q_scr[:, cols] = _dot(qn_scr[...], wqb_ref[:, cols])
        slot = grp % 2
        _segment_mean_squares(q_scr, cols, qseg_ref, sq_scr, ms_scr, slot)
        for half in range(2):
            h = 2 * grp + half
            for rows in blocks:
                blk = q_scr[rows, h * LANES:(h + 1) * LANES]
                inv = lax.rsqrt(ms_scr[slot, rows, half * LANES:(half + 1) * LANES] + EPS)
                y = rotary((blk * inv) * qgain_ref[...], rows)
                q_ref[rows, h * LANES:(h + 1) * LANES] = y.astype(BF16)

    for rows in blocks:
        kb = z_scr[rows, 1920:2048]
        ms = jnp.sum(kb * kb, axis=-1, keepdims=True) * (1.0 / D_ROPE)
        kpe_blk = rotary((kb * lax.rsqrt(ms + EPS)) * krgain_ref[...], rows)
        kpe_scr[rows, :] = kpe_blk
        x1_then_x2 = jnp.where(low_lanes, pltpu.roll(kpe_blk, LANES - X1_LO, 1), pltpu.roll(kpe_blk, ROPE_HALF, 1))
        kpe_ref[rows, :] = x1_then_x2[:, 0:D_ROPE]
    _expand_kv(ckvb_scr, kpe_scr, wkv_ref, kgain_ref, kseg_ref, kv_scr, sq_scr, ms_scr, k_ref, v_ref,
               between={0: lambda: conv_group(0), 4: lambda: conv_group(1)})

    cs_ref[0] = uext_scr[tm + 6:tm + 8, :]
    uext_scr[0:8, :] = uext_scr[tm:tm + 8, :]


def _odd_inproj(x, norm_g, ow, conv_init, n_batch, pos0):
    t = x.shape[0]
    tm = min(TOKEN_TILE, t // n_batch)
    tpb = t // n_batch // tm
    assert tpb * tm * n_batch == t
    row = lambda w: pl.BlockSpec((tm, w), lambda i: (i, 0))
    per_b = lambda r, w: pl.BlockSpec((1, r, w), lambda i: (i // tpb, 0, 0))
    small = [ow["conv_w"], ow["qan"], ow["wqb"], ow["qgain"], ow["kvan"], ow["krgain"],
             ow["invf"], ow["sgn"], ow["wkv"], ow["kgain"], ow["qseg"], ow["kseg"]]
    kv_w = D_HEADS * LANES
    v_w = D_HEADS * D_V
    return pl.pallas_call(
        functools.partial(_odd_inproj_kernel, tm=tm, tpb=tpb, pos0=pos0),
        grid=(t // tm,),
        in_specs=[row(D_MODEL), _resident((1, D_MODEL)), _resident(ow["w_in"].shape), per_b(8, C_WIDTH)]
                 + [_resident(a.shape) for a in small],
        out_specs=[row(C_WIDTH), row(kv_w), row(kv_w), row(v_w), row(D_KV_LORA), row(D_ROPE),
                   per_b(CONV_W - 1, C_WIDTH)],
        out_shape=[jax.ShapeDtypeStruct((t, C_WIDTH), BF16), jax.ShapeDtypeStruct((t, kv_w), BF16),
                   jax.ShapeDtypeStruct((t, kv_w), BF16), jax.ShapeDtypeStruct((t, v_w), BF16),
                   jax.ShapeDtypeStruct((t, D_KV_LORA), F32), jax.ShapeDtypeStruct((t, D_ROPE), F32),
                   jax.ShapeDtypeStruct((n_batch, CONV_W - 1, C_WIDTH), F32)],
        scratch_shapes=[pltpu.VMEM((tm, D_MODEL), BF16), pltpu.VMEM((tm, ODD_IN_PAD), F32),
                        pltpu.VMEM((tm + 8, C_WIDTH), F32), pltpu.VMEM((tm, D_Q_LORA), BF16),
                        pltpu.VMEM((tm, kv_w), F32), pltpu.VMEM((tm, kv_w + v_w), F32),
                        pltpu.VMEM((2, tpb * tm, LANES), F32), pltpu.VMEM((tm, D_KV_LORA), BF16),
                        pltpu.VMEM((tm, LANES), F32), pltpu.VMEM((2, tm, MXU_COLS), BF16),
                        pltpu.VMEM((2, tm, MXU_COLS), F32)],
        compiler_params=_cparams("arbitrary"),
        name="odd_inproj",
    )(x, norm_g.reshape(1, D_MODEL), ow["w_in"], conv_init, *small)


def _kv_expand_kernel(ckv_ref, kpe_ref, place_ref, wkv_ref, kgain_ref, kseg_ref, k_ref, v_ref,
                      kv_scr, ckvb_scr, kpe_scr, sq_scr, ms_scr):
    ckvb_scr[...] = ckv_ref[...].astype(BF16)
    kpe_scr[...] = _dot(kpe_ref[...].astype(BF16), place_ref[...])
    _expand_kv(ckvb_scr, kpe_scr, wkv_ref, kgain_ref, kseg_ref, kv_scr, sq_scr, ms_scr, k_ref, v_ref)


def _kv_expand(ckv, kpe, ow):
    t = ckv.shape[0]
    tm = min(TOKEN_TILE, t)
    row = lambda w: pl.BlockSpec((tm, w), lambda i: (i, 0))
    kv_w = D_HEADS * LANES
    v_w = D_HEADS * D_V
    return pl.pallas_call(
        _kv_expand_kernel,
        grid=(t // tm,),
        in_specs=[row(D_KV_LORA), row(D_ROPE), _resident(ow["place"].shape), _resident(ow["wkv"].shape),
                  _resident(ow["kgain"].shape), _resident(ow["kseg"].shape)],
        out_specs=[row(kv_w), row(v_w)],
        out_shape=[jax.ShapeDtypeStruct((t, kv_w), BF16), jax.ShapeDtypeStruct((t, v_w), BF16)],
        scratch_shapes=[pltpu.VMEM((tm, kv_w + v_w), F32), pltpu.VMEM((tm, D_KV_LORA), BF16),
                        pltpu.VMEM((tm, LANES), F32), pltpu.VMEM((2, tm, MXU_COLS), BF16),
                        pltpu.VMEM((2, tm, MXU_COLS), F32)],
        compiler_params=_cparams("arbitrary"),
        name="kv_expand",
    )(ckv, kpe, ow["place"], ow["wkv"], ow["kgain"], ow["kseg"])


def _odd_weights(w_in, conv_w, q_a_norm, w_q_b, kv_a_norm, w_kv_b, qn_nope, qn_rope, kn_nope, kn_rope):
    z = lambda n: jnp.zeros((n,), F32)
    qk = D_NOPE + D_ROPE
    w_in = w_in.astype(BF16)
    kr_blk = _to_head_lanes(jnp.concatenate([jnp.zeros((D_MODEL, D_NOPE), BF16), w_in[:, 1920:]], axis=1))
    w_pad = jnp.concatenate([w_in[:, :1920], kr_blk], axis=1)
    wqb = _to_head_lanes(w_q_b.astype(BF16).reshape(D_Q_LORA, D_HEADS, qk)).reshape(D_Q_LORA, D_HEADS * LANES)
    kvb = w_kv_b.astype(BF16).reshape(D_KV_LORA, D_HEADS, D_NOPE + D_V)
    wk = _to_head_lanes(jnp.concatenate([kvb[:, :, :D_NOPE], jnp.zeros((D_KV_LORA, D_HEADS, D_ROPE), BF16)], axis=-1))
    wkv = jnp.concatenate([wk.reshape(D_KV_LORA, D_HEADS * LANES),
                           kvb[:, :, D_NOPE:].reshape(D_KV_LORA, D_HEADS * D_V)], axis=1)
    inv = 1.0 / (ROPE_THETA ** (jnp.arange(ROPE_HALF, dtype=F32) / ROPE_HALF))
    ones = jnp.ones((ROPE_HALF,), F32)
    lanes = lambda v: _to_head_lanes(v).reshape(1, LANES)
    in_nope = lanes(jnp.concatenate([jnp.ones((D_NOPE,), F32), z(D_ROPE)]))
    in_rope = lanes(jnp.concatenate([z(D_NOPE), jnp.ones((D_ROPE,), F32)]))
    seg_nope = in_nope.T * in_nope * (1.0 / D_NOPE)
    seg_rope = in_rope.T * in_rope * (1.0 / D_ROPE)
    two_heads = lambda m: jnp.kron(jnp.eye(MXU_COLS // LANES, dtype=F32), m).astype(BF16)
    return {
        "qseg": two_heads(seg_nope + seg_rope),
        "kseg": two_heads(seg_nope),
        "w_in": w_pad,
        "conv_w": conv_w.astype(F32),
        "qan": q_a_norm.reshape(1, D_Q_LORA),
        "wqb": wqb,
        "qgain": lanes(jnp.concatenate([qn_nope, qn_rope])) * MLA_QSCALE,
        "kvan": kv_a_norm.reshape(1, D_KV_LORA),
        "krgain": lanes(jnp.concatenate([z(D_NOPE), kn_rope])),
        "invf": lanes(jnp.concatenate([z(D_NOPE), inv, inv])),
        "sgn": lanes(jnp.concatenate([z(D_NOPE), -ones, ones])),
        "wkv": wkv,
        "kgain": lanes(jnp.concatenate([kn_nope, z(D_ROPE)])),
        "place": _to_head_lanes(jnp.concatenate([jnp.zeros((D_ROPE, D_NOPE), BF16), jnp.eye(D_ROPE, dtype=BF16)], axis=1)),
    }


def _mla_prompt_kernel(q_ref, k_ref, v_ref, o_ref, s_scr, e_scr, r_scr, *, seq):
    qb = min(MLA_QBLOCK, seq)
    lo = lax.broadcasted_iota(jnp.int32, (qb, LANES), 1) < D_V
    first_half = lax.broadcasted_iota(jnp.int32, (CHUNK, LANES), 1) < CHUNK
    for i in range(seq // qb):
        tk = qb * (i + 1)
        outs = []
        for h in range(2):
            q = q_ref[0, i * qb:(i + 1) * qb, h * LANES:(h + 1) * LANES]
            k = k_ref[0, 0:tk, h * LANES:(h + 1) * LANES]
            s_scr[h, :, 0:tk] = lax.dot_general(q, k, _NT, preferred_element_type=F32)
            for rb in range(qb // CHUNK):
                rows = slice(rb * CHUNK, (rb + 1) * CHUNK)
                visible = tk - qb + CHUNK * (rb + 1)
                n_full, ragged = visible // LANES, visible % LANES != 0
                blk = lambda c: s_scr[h, rows, c * LANES:(c + 1) * LANES]
                cols = [blk(c) for c in range(n_full)]
                if ragged:
                    cols.append(jnp.where(first_half, blk(n_full), NEG_INF))
                mm = cols[0]
                for c in cols[1:]:
                    mm = jnp.maximum(mm, c)
                m = _row_stat(jnp.max, mm)
                acc = None
                for c in range(tk // LANES):
                    if c < len(cols):
                        e = jnp.exp2(blk(c) - m)
                        if c >= n_full:
                            e = jnp.where(first_half, e, 0.0)
                        acc = e if acc is None else acc + e
                    else:
                        e = jnp.zeros((CHUNK, LANES), F32)
                    e_scr[h, rows, c * LANES:(c + 1) * LANES] = e.astype(BF16)
                r_scr[h, rows, :] = 1.0 / _row_stat(jnp.sum, acc)
            outs.append(_dot(e_scr[h, :, 0:tk], v_ref[0, 0:tk, :]) * r_scr[h])
        o_ref[0, i * qb:(i + 1) * qb, :] = jnp.where(lo, outs[0], outs[1]).astype(BF16)


def _mla_prompt(q, k, v):
    nb, seq, _ = q.shape
    qb = min(MLA_QBLOCK, seq)
    return pl.pallas_call(
        functools.partial(_mla_prompt_kernel, seq=seq),
        grid=(nb, D_HEADS // 2),
        in_specs=[pl.BlockSpec((1, seq, 2 * LANES), lambda b, p: (b, 0, p)),
                  pl.BlockSpec((1, seq, 2 * LANES), lambda b, p: (b, 0, p)),
                  pl.BlockSpec((1, seq, LANES), lambda b, p: (b, 0, p))],
        out_specs=pl.BlockSpec((1, seq, LANES), lambda b, p: (b, 0, p)),
        out_shape=jax.ShapeDtypeStruct((nb, seq, D_HEADS * D_V), BF16),
        scratch_shapes=[pltpu.VMEM((2, qb, seq), F32), pltpu.VMEM((2, qb, seq), BF16),
                        pltpu.VMEM((2, qb, LANES), F32)],
        compiler_params=_cparams("arbitrary", "arbitrary"),
        name="mla_prompt",
    )(q, k, v)


def _mla_sample_kernel(q_ref, kc_ref, vc_ref, kn_ref, vn_ref, o_ref):
    nq = q_ref.shape[1]
    lo = lax.broadcasted_iota(jnp.int32, (nq, LANES), 1) < D_V
    outs = []
    for h in range(2):
        cols = slice(h * LANES, (h + 1) * LANES)
        q = q_ref[0, :, cols]
        s1 = lax.dot_general(q, kc_ref[0, :, cols], _NT, preferred_element_type=F32)
        s2 = lax.dot_general(q, kn_ref[0, :, cols], _NT, preferred_element_type=F32)
        m = jnp.maximum(jnp.max(s1, axis=-1, keepdims=True), jnp.max(s2, axis=-1, keepdims=True))
        e1 = jnp.exp2(s1 - m)
        e2 = jnp.exp2(s2 - m)
        den = jnp.sum(e1, axis=-1, keepdims=True) + jnp.sum(e2, axis=-1, keepdims=True)
        outs.append((_dot(e1.astype(BF16), vc_ref[0]) + _dot(e2.astype(BF16), vn_ref[0])) / den)
    o_ref[0] = jnp.where(lo, outs[0], outs[1]).astype(BF16)


def _mla_sample(q, kc, vc, kn, vn):
    nb, nq, _ = q.shape
    past = kc.shape[1]
    two = lambda r: pl.BlockSpec((1, r, 2 * LANES), lambda b, p: (b, 0, p))
    one = lambda r: pl.BlockSpec((1, r, LANES), lambda b, p: (b, 0, p))
    return pl.pallas_call(
        _mla_sample_kernel,
        grid=(nb, D_HEADS // 2),
        in_specs=[two(nq), two(past), one(past), two(nq), one(nq)],
        out_specs=one(nq),
        out_shape=jax.ShapeDtypeStruct((nb, nq, D_HEADS * D_V), BF16),
        compiler_params=_cparams("arbitrary", "arbitrary"),
        name="mla_sample",
    )(q, kc, vc, kn, vn)


def _even_layer(xp, xs, nb, seq, ndb, dseq, norm_g, w_in, w_out, a_qn, a_kn, a_sinks, b_qn, b_kn, b_rel,
                t5_table, ck_a, cv_a, ck_b, cv_b, ffn2):
    perm = _pair_perm()
    w_in = w_in.astype(BF16)
    w_out = w_out.astype(BF16)
    w_in_p = jnp.concatenate([w_in[:, :A_Q][:, perm], w_in[:, A_Q:]], axis=1)
    woa = w_out[:A_Q][perm]
    wob = w_out[A_Q:]
    ones = lambda n: jnp.ones((n,), F32)
    scale = HEAD_DIM ** -0.5
    gain_row = jnp.concatenate([jnp.tile(a_qn, A_HEADS) * scale, jnp.tile(a_kn, A_KV_HEADS), ones(A_KV),
                                jnp.tile(b_qn, B_HEADS) * scale, jnp.tile(b_kn, B_HEADS), ones(B_QKV)]
                               ).reshape(1, EVEN_IN).astype(F32)
    bias_a = _bias_a(t5_table)
    bias_b = _bias_b(b_rel)
    sink = jnp.broadcast_to(jnp.repeat(a_sinks.astype(F32), PAIR_ROWS)[:, None], (A_HEADS * PAIR_ROWS, LANES))
    la, lb = min(WINDOW, seq), min(B_REACH, seq)

    aq, ak, av, bq, bk, bv, cak, cav, cbk, cbv = _even_inproj(xp, norm_g, w_in_p, gain_row, nb, la, lb)
    r3 = lambda a: a.reshape(nb, seq, a.shape[-1])
    ya = _attn_a(r3(aq), r3(ak), r3(av), bias_a, sink, WINDOW, True)
    yb = _attn_b(r3(bq), r3(bk), r3(bv), bias_b, B_REACH, True)
    xp = _ffn(xp, *ffn2, out_proj=(ya.reshape(nb * seq, A_Q), yb.reshape(nb * seq, B_QKV), woa, wob))
    st_p = (cak.reshape(nb, la, A_KV_HEADS, HEAD_DIM), cav.reshape(nb, la, A_KV_HEADS, HEAD_DIM),
            cbk.reshape(nb, lb, B_HEADS, HEAD_DIM), cbv.reshape(nb, lb, B_HEADS, HEAD_DIM))

    ts = ndb * dseq
    aq, ak, av, bq, bk, bv, nak, nav, nbk, nbv = _even_inproj(xs, norm_g, w_in_p, gain_row, 1, ts, ts)
    pad_q = lambda a: jnp.pad(a.reshape(ndb, dseq, a.shape[-1]), ((0, 0), (0, PAIR_ROWS - dseq), (0, 0)))

    def window(cache, new):
        w = cache.shape[-2] * cache.shape[-1]
        full = jnp.concatenate([cache.reshape(ndb, -1, w), new.reshape(ndb, dseq, w)], axis=1)
        buf = jnp.pad(full, ((0, 0), (0, PAIR_ROWS - dseq), (0, 0))).astype(BF16)
        return full[:, dseq:].reshape(cache.shape), buf

    st_ak, kbuf_a = window(ck_a, nak)
    st_av, vbuf_a = window(cv_a, nav)
    st_bk, kbuf_b = window(ck_b, nbk)
    st_bv, vbuf_b = window(cv_b, nbv)
    ya = _attn_a(pad_q(aq), kbuf_a, vbuf_a, bias_a, sink, WINDOW, False)[:, :dseq]
    yb = _attn_b(pad_q(bq), kbuf_b, vbuf_b, bias_b, B_REACH, False)[:, :dseq]
    xs = _ffn(xs, *ffn2, out_proj=(ya.reshape(ts, A_Q), yb.reshape(ts, B_QKV), woa, wob))
    return xp, xs, st_p, (st_ak, st_av, st_bk, st_bv)


def _odd_layer(xp, xs, nb, seq, ndb, dseq, past, norm_g, ow, w_out, conv_prev, c_ckv, c_kpe, ffn2):
    woc = w_out[:C_WIDTH].astype(BF16)
    wod = w_out[C_WIDTH:].astype(BF16)

    zero_init = jnp.zeros((nb, 8, C_WIDTH), F32)
    yc, q, k, v, ckv, kpe, cs = _odd_inproj(xp, norm_g, ow, zero_init, nb, 0)
    r3 = lambda a: a.reshape(nb, seq, a.shape[-1])
    yd = _mla_prompt(r3(q), r3(k), r3(v))
    xp = _ffn(xp, *ffn2, out_proj=(yc, yd.reshape(nb * seq, D_HEADS * D_V), woc, wod))
    st_p = (cs, ckv.reshape(nb, seq, D_KV_LORA), kpe.reshape(nb, seq, D_ROPE))

    ts = ndb * dseq
    init = jnp.pad(conv_prev.astype(F32), ((0, 0), (8 - (CONV_W - 1), 0), (0, 0)))
    yc, q, kn, vn, ckv, kpe, cs = _odd_inproj(xs, norm_g, ow, init, ndb, past)
    kc, vc = _kv_expand(c_ckv.reshape(ndb * past, D_KV_LORA), c_kpe.reshape(ndb * past, D_ROPE), ow)
    s3 = lambda a: a.reshape(ndb, dseq, a.shape[-1])
    yd = _mla_sample(s3(q), kc.reshape(ndb, past, -1), vc.reshape(ndb, past, -1), s3(kn), s3(vn))
    xs = _ffn(xs, *ffn2, out_proj=(yc, yd.reshape(ts, D_HEADS * D_V), woc, wod))
    st_s = (cs, ckv.reshape(ndb, dseq, D_KV_LORA), kpe.reshape(ndb, dseq, D_ROPE))
    return xp, xs, st_p, st_s


def kernel(x_prompt, x_sample, cache_a_k, cache_a_v, cache_b_k, cache_b_v, state_c_conv, cache_d_ckv, cache_d_kpe, ff1_norm, ff1_w_gu, ff1_w_down, mix_norm, ff2_norm, ff2_w_gu, ff2_w_down, t5_bias_table, ev_w_in, ev_w_out, a_q_norm, a_k_norm, a_sinks, b_q_norm, b_k_norm, b_rel_bias, od_w_in, od_w_out, c_conv_w, d_q_a_norm, d_w_q_b, d_kv_a_norm, d_w_kv_b, d_q_nope_norm, d_q_rope_norm, d_k_nope_norm, d_k_rope_norm):
    nb, seq, _ = x_prompt.shape
    ndb, dseq, _ = x_sample.shape
    past = cache_d_ckv.shape[2]
    depth = ff1_norm.shape[0]
    assert seq % TOKEN_TILE == 0 and dseq == CHUNK and past % CHUNK == 0
    xp = x_prompt.reshape(nb * seq, D_MODEL)
    xs = x_sample.reshape(ndb * dseq, D_MODEL)
    even_p, even_s, odd_p, odd_s = [], [], [], []
    for l in range(depth):
        i = l // 2
        ff1 = _ffn_weights(ff1_w_gu[l], ff1_w_down[l])
        ffn2 = (ff2_norm[l],) + _ffn_weights(ff2_w_gu[l], ff2_w_down[l])
        xp = _ffn(xp, ff1_norm[l], *ff1)
        xs = _ffn(xs, ff1_norm[l], *ff1)
        if l % 2 == 0:
            xp, xs, sp, ss = _even_layer(
                xp, xs, nb, seq, ndb, dseq, mix_norm[l], ev_w_in[i], ev_w_out[i], a_q_norm[i], a_k_norm[i],
                a_sinks[i], b_q_norm[i], b_k_norm[i], b_rel_bias[i], t5_bias_table,
                cache_a_k[i], cache_a_v[i], cache_b_k[i], cache_b_v[i], ffn2)
            even_p.append(sp)
            even_s.append(ss)
        else:
            ow = _odd_weights(od_w_in[i], c_conv_w[i], d_q_a_norm[i], d_w_q_b[i], d_kv_a_norm[i], d_w_kv_b[i],
                              d_q_nope_norm[i], d_q_rope_norm[i], d_k_nope_norm[i], d_k_rope_norm[i])
            xp, xs, sp, ss = _odd_layer(xp, xs, nb, seq, ndb, dseq, past, mix_norm[l], ow, od_w_out[i],
                                        state_c_conv[i], cache_d_ckv[i], cache_d_kpe[i], ffn2)
            odd_p.append(sp)
            odd_s.append(ss)
    stack = lambda group, j: jnp.stack([g[j] for g in group])
    return (xp.reshape(nb, seq, D_MODEL), xs.reshape(ndb, dseq, D_MODEL),
            stack(even_p, 0), stack(even_p, 1), stack(even_p, 2), stack(even_p, 3),
            stack(odd_p, 0), stack(odd_p, 1), stack(odd_p, 2),
            stack(even_s, 0), stack(even_s, 1), stack(even_s, 2), stack(even_s, 3),
            stack(odd_s, 0), stack(odd_s, 1), stack(odd_s, 2))
```

```python
import functools
import math

import jax
import jax.numpy as jnp
from jax import lax
from jax.experimental import pallas as pl
from jax.experimental.pallas import tpu as pltpu

F32 = jnp.float32
BF16 = jnp.bfloat16

D_MODEL = 1024
CHUNK = 64
HEAD_DIM = 64
EPS = 1e-6
A_HEADS = 8
A_KV_HEADS = 2
WINDOW = 128
T5_BUCKETS = 32
T5_MAX_DIST = 128
B_HEADS = 8
B_REACH = 512
B_MAX_REL = 128
C_WIDTH = 512
CONV_W = 3
D_HEADS = 8
D_Q_LORA = 256
D_KV_LORA = 128
D_NOPE = 64
D_ROPE = 32
D_V = 64
ROPE_THETA = 10000.0
FFN_DIM = 2816
A_Q = A_HEADS * HEAD_DIM
A_KV = A_KV_HEADS * HEAD_DIM
B_QKV = B_HEADS * HEAD_DIM
EVEN_IN = A_Q + 2 * A_KV + 3 * B_QKV

LANES = 128
TOKEN_TILE = 512
CAST_ROWS = 256
MXU_COLS = 256
FFN_CHUNK = MXU_COLS
PAIR_ROWS = 2 * CHUNK
MLA_QBLOCK = 512
VMEM_LIMIT_BYTES = 56 * 1024 * 1024
MLA_QSCALE = (D_NOPE + D_ROPE) ** -0.5 * math.log2(math.e)
NEG_INF = float("-inf")

_NT = (((1,), (1,)), ((), ()))


def _cparams(*sem):
    return pltpu.CompilerParams(dimension_semantics=sem, vmem_limit_bytes=VMEM_LIMIT_BYTES)


def _resident(shape):
    zeros = (0,) * len(shape)
    return pl.BlockSpec(shape, lambda *_: zeros, pipeline_mode=pl.Buffered(1))


def _rms(x, g):
    ms = jnp.mean(x * x, axis=-1, keepdims=True)
    return (x * lax.rsqrt(ms + EPS)) * g


def _dot(a, b):
    return jnp.dot(a, b, preferred_element_type=F32)


def _ffn_kernel(*refs, n_chunks, fused_out):
    if fused_out:
        (x_ref, ya_ref, yb_ref, woa_ref, wob_ref, g_ref, wgu_ref, wd_ref,
         o_ref, h_scr, a_scr) = refs
        y = _dot(ya_ref[...], woa_ref[...]) + _dot(yb_ref[...], wob_ref[...])
        o_ref[...] = x_ref[...] + y
        res_ref = o_ref
    else:
        x_ref, g_ref, wgu_ref, wd_ref, o_ref, h_scr, a_scr = refs
        res_ref = x_ref
    h_scr[...] = _rms(res_ref[...], g_ref[...]).astype(BF16)
    for j in range(n_chunks):
        h = h_scr[...]
        g = _dot(h, wgu_ref[:, j * FFN_CHUNK:(j + 1) * FFN_CHUNK])
        u = _dot(h, wgu_ref[:, FFN_DIM + j * FFN_CHUNK:FFN_DIM + (j + 1) * FFN_CHUNK])
        a_scr[:, j * FFN_CHUNK:(j + 1) * FFN_CHUNK] = ((g * jax.nn.sigmoid(g)) * u).astype(BF16)
    o_ref[...] = res_ref[...] + 0.5 * _dot(a_scr[...], wd_ref[...])


def _ffn(x, norm_g, wgu, wd, out_proj=None):
    t = x.shape[0]
    tm = min(TOKEN_TILE, t)
    n_chunks = FFN_DIM // FFN_CHUNK
    row = lambda w: pl.BlockSpec((tm, w), lambda i: (i, 0))
    in_specs = [row(D_MODEL)]
    args = [x]
    if out_proj is not None:
        ya, yb, woa, wob = out_proj
        in_specs += [row(ya.shape[1]), row(yb.shape[1]), _resident(woa.shape), _resident(wob.shape)]
        args += [ya, yb, woa, wob]
    in_specs += [_resident((1, D_MODEL)), _resident(wgu.shape), _resident(wd.shape)]
    args += [norm_g.reshape(1, D_MODEL), wgu, wd]
    return pl.pallas_call(
        functools.partial(_ffn_kernel, n_chunks=n_chunks, fused_out=out_proj is not None),
        grid=(t // tm,),
        in_specs=in_specs,
        out_specs=row(D_MODEL),
        out_shape=jax.ShapeDtypeStruct((t, D_MODEL), F32),
        scratch_shapes=[pltpu.VMEM((tm, D_MODEL), BF16), pltpu.VMEM((tm, FFN_DIM), BF16)],
        compiler_params=_cparams("arbitrary"),
        name="ffn_out" if out_proj is not None else "ffn",
    )(*args)


def _cast_kernel(x_ref, o_ref):
    o_ref[...] = x_ref[...].astype(BF16)


def _to_bf16(w):
    r, c = w.shape
    tr = CAST_ROWS if r % CAST_ROWS == 0 else r
    spec = pl.BlockSpec((tr, c), lambda i: (i, 0))
    return pl.pallas_call(
        _cast_kernel, grid=(r // tr,), in_specs=[spec], out_specs=spec,
        out_shape=jax.ShapeDtypeStruct((r, c), BF16),
        compiler_params=_cparams("arbitrary"), name="cast_bf16",
    )(w)


def _ffn_weights(w_gu, w_down):
    return _to_bf16(w_gu), _to_bf16(w_down)


def _norm_halves(blk, gain):
    sq = blk * blk
    lo = lax.broadcasted_iota(jnp.int32, blk.shape, 1) < HEAD_DIM
    s_lo = jnp.sum(jnp.where(lo, sq, 0.0), axis=-1, keepdims=True)
    s_hi = jnp.sum(jnp.where(lo, 0.0, sq), axis=-1, keepdims=True)
    inv = jnp.where(lo, lax.rsqrt(s_lo * (1.0 / HEAD_DIM) + EPS), lax.rsqrt(s_hi * (1.0 / HEAD_DIM) + EPS))
    return (blk * inv) * gain


def _even_inproj_kernel(x_ref, g_ref, w_ref, gain_ref,
                        aq_ref, ak_ref, av_ref, bq_ref, bk_ref, bv_ref,
                        cak_ref, cav_ref, cbk_ref, cbv_ref, h_scr, z_scr, *, tm, tpb, rows_a, rows_b):
    i = pl.program_id(0)
    last = (i % tpb) == (tpb - 1)
    h_scr[...] = _rms(x_ref[...], g_ref[...]).astype(BF16)

    def lanes(c):
        return slice(c * LANES, (c + 1) * LANES)

    def emit(c):
        blk = z_scr[:, lanes(c)]
        gain = gain_ref[:, lanes(c)]
        if c < 4:
            aq_ref[:, lanes(c)] = _norm_halves(blk, gain).astype(BF16)
        elif c == 4:
            kn = _norm_halves(blk, gain)
            ak_ref[...] = kn.astype(BF16)
            z_scr[:, lanes(c)] = kn
        elif c == 5:
            av_ref[...] = blk.astype(BF16)
        elif c < 10:
            bq_ref[:, lanes(c - 6)] = _norm_halves(blk, gain).astype(BF16)
        elif c < 14:
            kn = _norm_halves(blk, gain)
            bk_ref[:, lanes(c - 10)] = kn.astype(BF16)
            z_scr[:, lanes(c)] = kn
        else:
            bv_ref[:, lanes(c - 14)] = blk.astype(BF16)

    for grp in range(EVEN_IN // MXU_COLS):
        cols = slice(grp * MXU_COLS, (grp + 1) * MXU_COLS)
        z_scr[:, cols] = _dot(h_scr[...], w_ref[:, cols])
        emit(2 * grp)
        emit(2 * grp + 1)

    @pl.when(last)
    def _():
        cak_ref[0] = z_scr[tm - rows_a:, lanes(4)]
        cav_ref[0] = z_scr[tm - rows_a:, lanes(5)]
        cbk_ref[0] = z_scr[tm - rows_b:, 10 * LANES:14 * LANES]
        cbv_ref[0] = z_scr[tm - rows_b:, 14 * LANES:18 * LANES]


def _even_inproj(x, norm_g, w_in, gain_row, n_batch, rows_a, rows_b):
    t = x.shape[0]
    tm = min(TOKEN_TILE, t)
    tpb = t // n_batch // tm
    assert tpb * tm * n_batch == t and rows_a <= tm and rows_b <= tm
    row = lambda w: pl.BlockSpec((tm, w), lambda i: (i, 0))
    cache = lambda r, w: pl.BlockSpec((1, r, w), lambda i: (i // tpb, 0, 0))
    bshape = lambda w: jax.ShapeDtypeStruct((t, w), BF16)
    cshape = lambda r, w: jax.ShapeDtypeStruct((n_batch, r, w), F32)
    return pl.pallas_call(
        functools.partial(_even_inproj_kernel, tm=tm, tpb=tpb, rows_a=rows_a, rows_b=rows_b),
        grid=(t // tm,),
        in_specs=[row(D_MODEL), _resident((1, D_MODEL)), _resident(w_in.shape), _resident(gain_row.shape)],
        out_specs=[row(A_Q), row(A_KV), row(A_KV), row(B_QKV), row(B_QKV), row(B_QKV),
                   cache(rows_a, A_KV), cache(rows_a, A_KV), cache(rows_b, B_QKV), cache(rows_b, B_QKV)],
        out_shape=[bshape(A_Q), bshape(A_KV), bshape(A_KV), bshape(B_QKV), bshape(B_QKV), bshape(B_QKV),
                   cshape(rows_a, A_KV), cshape(rows_a, A_KV), cshape(rows_b, B_QKV), cshape(rows_b, B_QKV)],
        scratch_shapes=[pltpu.VMEM((tm, D_MODEL), BF16), pltpu.VMEM((tm, EVEN_IN), F32)],
        compiler_params=_cparams("arbitrary"),
        name="even_inproj",
    )(x, norm_g.reshape(1, D_MODEL), w_in, gain_row)


def _pair_perm():
    idx = []
    for p in range(4):
        for half in range(2):
            head = p + 4 * half
            idx.extend(range(head * HEAD_DIM, (head + 1) * HEAD_DIM))
    return jnp.asarray(idx, dtype=jnp.int32)


def _fill_padded(buf, src_ref, ctx):
    buf[0:ctx, :] = jnp.zeros((ctx, buf.shape[1]), buf.dtype)
    buf[ctx:, :] = src_ref[0]


def _windows(k_ref, v_ref, pad_scr, ctx, pad_front):
    win = ctx + PAIR_ROWS
    if pad_front:
        kbuf, vbuf = pad_scr
        _fill_padded(kbuf, k_ref, ctx)
        _fill_padded(vbuf, v_ref, ctx)
        return (lambda r0: kbuf[pl.ds(r0, win), :]), (lambda r0: vbuf[pl.ds(r0, win), :])
    return (lambda r0: k_ref[0, pl.ds(r0, win), :]), (lambda r0: v_ref[0, pl.ds(r0, win), :])


def _fold_lanes(x, op):
    acc = x[:, :LANES]
    for c in range(1, x.shape[1] // LANES):
        acc = op(acc, x[:, c * LANES:(c + 1) * LANES])
    return acc


def _row_stat(reduce_fn, x):
    return jnp.broadcast_to(reduce_fn(x, axis=-1, keepdims=True), x.shape)


PAIR_SLOTS = 4


def _run_pairs(pair, n_steps, n_front):
    for t in range(n_steps):
        pair(t * PAIR_ROWS, t % PAIR_SLOTS, t < n_front)


def _attn_a_kernel(q_ref, k_ref, v_ref, bias_ref, sink_ref, o_ref, s_scr, e_scr, *pad_scr, nq, ctx, pad_front):
    win = ctx + PAIR_ROWS
    kwin, vwin = _windows(k_ref, v_ref, pad_scr, ctx, pad_front)
    lo = lax.broadcasted_iota(jnp.int32, (PAIR_ROWS, LANES), 1) < HEAD_DIM

    def pair(r0, slot, masked):
        q_lo, q_hi = [], []
        for p in range(4):
            qp = q_ref[0, pl.ds(r0, PAIR_ROWS), p * LANES:(p + 1) * LANES]
            zero = jnp.zeros_like(qp)
            q_lo.append(jnp.where(lo, qp, zero))
            q_hi.append(jnp.where(lo, zero, qp))
        q2 = jnp.concatenate(q_lo + q_hi, axis=0)
        s_scr[slot] = lax.dot_general(q2, kwin(r0), _NT, preferred_element_type=F32)
        for head in range(A_HEADS):
            rows = slice(head * PAIR_ROWS, (head + 1) * PAIR_ROWS)
            s = s_scr[slot, rows, :] + bias_ref[rows, :]
            if masked:
                col = lax.broadcasted_iota(jnp.int32, (PAIR_ROWS, win), 1)
                s = jnp.where(col + r0 >= ctx, s, NEG_INF)
            sk = sink_ref[rows, :]
            m = jnp.maximum(_row_stat(jnp.max, _fold_lanes(s, jnp.maximum)), sk)
            e = [jnp.exp(s[:, c * LANES:(c + 1) * LANES] - m) for c in range(win // LANES)]
            den = _row_stat(jnp.sum, functools.reduce(jnp.add, e)) + jnp.exp(sk - m)
            inv = 1.0 / den
            for c in range(win // LANES):
                e_scr[slot, rows, c * LANES:(c + 1) * LANES] = (e[c] * inv).astype(BF16)
        o2 = _dot(e_scr[slot], vwin(r0))
        half = 4 * PAIR_ROWS
        for p in range(4):
            o = jnp.where(lo, o2[p * PAIR_ROWS:(p + 1) * PAIR_ROWS],
                          o2[half + p * PAIR_ROWS:half + (p + 1) * PAIR_ROWS])
            o_ref[0, pl.ds(r0, PAIR_ROWS), p * LANES:(p + 1) * LANES] = o.astype(BF16)

    _run_pairs(pair, nq // PAIR_ROWS, ctx // PAIR_ROWS if pad_front else 0)


def _attn_a(q, k, v, bias, sink, ctx, pad_front):
    nb, nq, _ = q.shape
    nk = k.shape[1]
    assert nq % PAIR_ROWS == 0 and nk == (nq if pad_front else ctx + nq)
    per_b = lambda r, w: pl.BlockSpec((1, r, w), lambda b: (b, 0, 0))
    rows, win = A_HEADS * PAIR_ROWS, ctx + PAIR_ROWS
    scratch = [pltpu.VMEM((PAIR_SLOTS, rows, win), F32), pltpu.VMEM((PAIR_SLOTS, rows, win), BF16)]
    if pad_front:
        scratch += [pltpu.VMEM((ctx + nq, A_KV), BF16)] * 2
    return pl.pallas_call(
        functools.partial(_attn_a_kernel, nq=nq, ctx=ctx, pad_front=pad_front),
        grid=(nb,),
        in_specs=[per_b(nq, A_Q), per_b(nk, A_KV), per_b(nk, A_KV), _resident(bias.shape), _resident(sink.shape)],
        out_specs=per_b(nq, A_Q),
        out_shape=jax.ShapeDtypeStruct((nb, nq, A_Q), BF16),
        scratch_shapes=scratch,
        compiler_params=_cparams("arbitrary"),
        name="attn_a",
    )(q, k, v, bias, sink)


def _attn_b_kernel(q_ref, k_ref, v_ref, bias_ref, o_ref, s_scr, e_scr, r_scr, *pad_scr, nq, ctx, pad_front):
    win = ctx + PAIR_ROWS
    kwin, vwin = _windows(k_ref, v_ref, pad_scr, ctx, pad_front)
    lo = lax.broadcasted_iota(jnp.int32, (PAIR_ROWS, LANES), 1) < HEAD_DIM

    def pair(r0, slot, masked):
        qp = q_ref[0, pl.ds(r0, PAIR_ROWS), :]
        zero = jnp.zeros_like(qp)
        q2 = jnp.concatenate([jnp.where(lo, qp, zero), jnp.where(lo, zero, qp)], axis=0)
        s_scr[slot] = lax.dot_general(q2, kwin(r0), _NT, preferred_element_type=F32)
        for rb in range(2 * PAIR_ROWS // CHUNK):
            rows = slice(rb * CHUNK, (rb + 1) * CHUNK)
            s = s_scr[slot, rows, :] + bias_ref[0, rows, :]
            if masked:
                col = lax.broadcasted_iota(jnp.int32, (CHUNK, win), 1)
                s = jnp.where(col + r0 >= ctx, s, NEG_INF)
            m = _row_stat(jnp.max, _fold_lanes(s, jnp.maximum))
            e = [jnp.exp(s[:, c * LANES:(c + 1) * LANES] - m) for c in range(win // LANES)]
            for c in range(win // LANES):
                e_scr[slot, rows, c * LANES:(c + 1) * LANES] = e[c].astype(BF16)
            r_scr[slot, rows, :] = 1.0 / _row_stat(jnp.sum, functools.reduce(jnp.add, e))
        o2 = _dot(e_scr[slot], vwin(r0)) * r_scr[slot]
        o = jnp.where(lo, o2[:PAIR_ROWS], o2[PAIR_ROWS:])
        o_ref[0, pl.ds(r0, PAIR_ROWS), :] = o.astype(BF16)

    _run_pairs(pair, nq // PAIR_ROWS, ctx // PAIR_ROWS if pad_front else 0)


def _attn_b(q, k, v, bias, ctx, pad_front):
    nb, nq, _ = q.shape
    nk = k.shape[1]
    assert nq % PAIR_ROWS == 0 and nk == (nq if pad_front else ctx + nq)
    blk = lambda r: pl.BlockSpec((1, r, LANES), lambda b, p: (b, 0, p))
    rows, win = 2 * PAIR_ROWS, ctx + PAIR_ROWS
    scratch = [pltpu.VMEM((PAIR_SLOTS, rows, win), F32), pltpu.VMEM((PAIR_SLOTS, rows, win), BF16),
               pltpu.VMEM((PAIR_SLOTS, rows, LANES), F32)]
    if pad_front:
        scratch += [pltpu.VMEM((ctx + nq, LANES), BF16)] * 2
    return pl.pallas_call(
        functools.partial(_attn_b_kernel, nq=nq, ctx=ctx, pad_front=pad_front),
        grid=(nb, B_HEADS // 2),
        in_specs=[blk(nq), blk(nk), blk(nk),
                  pl.BlockSpec((1,) + bias.shape[1:], lambda b, p: (p, 0, 0))],
        out_specs=blk(nq),
        out_shape=jax.ShapeDtypeStruct((nb, nq, B_QKV), BF16),
        scratch_shapes=scratch,
        compiler_params=_cparams("arbitrary", "arbitrary"),
        name="attn_b",
    )(q, k, v, bias)


def _t5_bucket(rel):
    nb = T5_BUCKETS // 2
    max_exact = nb // 2
    n = -rel
    ret = jnp.where(n < 0, nb, 0)
    n = jnp.abs(n)
    nf = jnp.maximum(n, 1).astype(F32)
    large = max_exact + (jnp.log(nf / max_exact) / math.log(T5_MAX_DIST / max_exact)
                         * (nb - max_exact)).astype(jnp.int32)
    large = jnp.minimum(large, nb - 1)
    return ret + jnp.where(n < max_exact, n, large)


def _band_tables(ctx, value_of_rel):
    n_cols = ctx + PAIR_ROWS
    period = n_cols + PAIR_ROWS - 1
    d = jnp.concatenate([jnp.arange(0, n_cols), jnp.arange(-(PAIR_ROWS - 1), 0)])
    ext = value_of_rel(d - ctx)
    heads = ext.shape[0]
    flat = jnp.tile(ext, (1, PAIR_ROWS))[:, :PAIR_ROWS * (period - 1)]
    tb = flat.reshape(heads, PAIR_ROWS, period - 1)[:, :, :n_cols]
    i = jnp.arange(PAIR_ROWS)[:, None]
    j = jnp.arange(n_cols)[None, :]
    visible = jnp.where(i < CHUNK, j < ctx + CHUNK, j >= CHUNK)
    return jnp.where(visible[None], tb, NEG_INF)


def _bias_a(t5_table):
    tb = _band_tables(WINDOW, lambda rel: t5_table.astype(F32)[_t5_bucket(rel)].T)
    return tb.reshape(A_HEADS * PAIR_ROWS, WINDOW + PAIR_ROWS)


def _bias_b(b_rel):
    tb = _band_tables(B_REACH, lambda rel: b_rel.astype(F32)[:, jnp.clip(rel, -B_MAX_REL, B_MAX_REL) + B_MAX_REL])
    return tb.reshape(B_HEADS // 2, 2 * PAIR_ROWS, B_REACH + PAIR_ROWS)


ODD_IN_PAD = 2048
ROPE_HALF = D_ROPE // 2
X1_LO = LANES // 2
NOPE_SPLIT = X1_LO - ROPE_HALF


def _head_lane_source():
    zero = D_NOPE + D_ROPE
    src = []
    for lane in range(LANES):
        if lane < ROPE_HALF:
            src.append(D_NOPE + ROPE_HALF + lane)
        elif lane < X1_LO:
            src.append(lane - ROPE_HALF)
        elif lane < X1_LO + ROPE_HALF:
            src.append(D_NOPE + lane - X1_LO)
        elif lane < X1_LO + ROPE_HALF + D_NOPE - NOPE_SPLIT:
            src.append(NOPE_SPLIT + lane - X1_LO - ROPE_HALF)
        else:
            src.append(zero)
    return src


def _to_head_lanes(a):
    padded = jnp.concatenate([a, jnp.zeros(a.shape[:-1] + (1,), a.dtype)], axis=-1)
    return padded[..., jnp.asarray(_head_lane_source(), dtype=jnp.int32)]


def _lane_masks(shape):
    lane = lax.broadcasted_iota(jnp.int32, shape, 1)
    rope = (lane < ROPE_HALF) | ((lane >= X1_LO) & (lane < X1_LO + ROPE_HALF))
    used = lane < X1_LO + ROPE_HALF + D_NOPE - NOPE_SPLIT
    return used & ~rope, rope


def _rope(y, cosf, sinf):
    return y * cosf + pltpu.roll(y, LANES // 2, 1) * sinf


ROW_BLOCK = 128


def _row_blocks(tm):
    rb = min(ROW_BLOCK, tm)
    return [slice(r, r + rb) for r in range(0, tm, rb)]


def _segment_mean_squares(src_scr, cols, seg_ref, sq_scr, ms_scr, slot):
    for rows in _row_blocks(src_scr.shape[0]):
        v = src_scr[rows, cols]
        sq_scr[slot, rows, :] = (v * v).astype(BF16)
    ms_scr[slot] = _dot(sq_scr[slot], seg_ref[...])


def _expand_kv(ckvb_ref, kpe_blk_ref, wkv_ref, kgain_ref, kseg_ref, kv_scr, sq_scr, ms_scr, k_ref, v_ref,
               between=None):
    k_cols = D_HEADS * LANES
    tm = kv_scr.shape[0]
    for grp in range(kv_scr.shape[1] // MXU_COLS):
        if between and grp in between:
            between[grp]()
        cols = slice(grp * MXU_COLS, (grp + 1) * MXU_COLS)
        kv_scr[:, cols] = _dot(ckvb_ref[...], wkv_ref[:, cols])
        if cols.start < k_cols:
            slot = grp % 2
            _segment_mean_squares(kv_scr, cols, kseg_ref, sq_scr, ms_scr, slot)
        for rows in _row_blocks(tm):
            if cols.start < k_cols:
                for half in range(2):
                    h = 2 * grp + half
                    kb = kv_scr[rows, h * LANES:(h + 1) * LANES]
                    ms = ms_scr[slot, rows, half * LANES:(half + 1) * LANES]
                    kn = (kb * lax.rsqrt(ms + EPS)) * kgain_ref[...]
                    k_ref[rows, h * LANES:(h + 1) * LANES] = (kn + kpe_blk_ref[rows, :]).astype(BF16)
            else:
                v_ref[rows, cols.start - k_cols:cols.stop - k_cols] = kv_scr[rows, cols].astype(BF16)


def _odd_inproj_kernel(x_ref, g_ref, w_ref, cinit_ref, convw_ref, qan_ref, wqb_ref, qgain_ref,
                       kvan_ref, krgain_ref, invf_ref, sgn_ref, wkv_ref, kgain_ref, qseg_ref, kseg_ref,
                       yc_ref, q_ref, k_ref, v_ref, ckv_ref, kpe_ref, cs_ref,
                       h_scr, z_scr, uext_scr, qn_scr, q_scr, kv_scr, rot_scr, ckvb_scr, kpe_scr, sq_scr, ms_scr,
                       *, tm, tpb, pos0):
    i = pl.program_id(0)
    tile = i % tpb
    off = pl.multiple_of(tile * tm, tm)
    blocks = _row_blocks(tm)
    rb = blocks[0].stop

    @pl.when(tile == 0)
    def _():
        uext_scr[0:8, :] = cinit_ref[0]

    @pl.when(i < tpb)
    def _():
        row = lax.broadcasted_iota(jnp.int32, (tm, LANES), 0)
        ang = (row + (tile * tm + pos0)).astype(F32) * invf_ref[...]
        rot_scr[0, pl.ds(off, tm), :] = jnp.cos(ang)
        rot_scr[1, pl.ds(off, tm), :] = jnp.sin(ang) * sgn_ref[...]

    h_scr[...] = _rms(x_ref[...], g_ref[...]).astype(BF16)
    low_lanes = lax.broadcasted_iota(jnp.int32, (rb, LANES), 1) < ROPE_HALF

    def proj(c0, c1):
        z_scr[:, c0:c1] = _dot(h_scr[...], w_ref[:, c0:c1])

    def rotary(y, rows):
        cosf = rot_scr[0, pl.ds(off + rows.start, rb), :]
        sinf = rot_scr[1, pl.ds(off + rows.start, rb), :]
        return _rope(y, cosf, sinf)

    def conv_group(grp):
        c0 = grp * MXU_COLS
        cols = slice(c0, c0 + MXU_COLS)
        for base in (0, C_WIDTH, 2 * C_WIDTH):
            proj(base + c0, base + c0 + MXU_COLS)
        for rows in blocks:
            up = slice(rows.start + 8, rows.stop + 8)
            uext_scr[up, cols] = (z_scr[rows, C_WIDTH + c0:C_WIDTH + c0 + MXU_COLS]
                                  * z_scr[rows, 2 * C_WIDTH + c0:2 * C_WIDTH + c0 + MXU_COLS])
        for rows in blocks:
            yconv = convw_ref[0:1, cols] * uext_scr[rows.start + 6:rows.stop + 6, cols]
            yconv = yconv + convw_ref[1:2, cols] * uext_scr[rows.start + 7:rows.stop + 7, cols]
            yconv = yconv + convw_ref[2:3, cols] * uext_scr[rows.start + 8:rows.stop + 8, cols]
            yc_ref[rows, cols] = (z_scr[rows, cols] * yconv).astype(BF16)

    proj(1536, 1792)
    proj(1792, 2048)

    for rows in blocks:
        qn_scr[rows, :] = _rms(z_scr[rows, 1536:1792], qan_ref[...]).astype(BF16)
        ckv = _rms(z_scr[rows, 1792:1920], kvan_ref[...])
        ckv_ref[rows, :] = ckv
        ckvb_scr[rows, :] = ckv.astype(BF16)

    for grp in range(D_HEADS * LANES // MXU_COLS):
        cols = slice(grp * MXU_COLS, (grp + 1) * MXU_COLS)
        q_scr[:, cols] = _dot(qn_scr[...], wqb_ref[:, cols])
        slot = grp % 2
        _segment_mean_squares(q_scr, cols, qseg_ref, sq_scr, ms_scr, slot)
        for half in range(2):
            h = 2 * grp + half
            for rows in blocks:
                blk = q_scr[rows, h * LANES:(h + 1) * LANES]
                inv = lax.rsqrt(ms_scr[slot, rows, half * LANES:(half + 1) * LANES] + EPS)
                y = rotary((blk * inv) * qgain_ref[...], rows)
                q_ref[rows, h * LANES:(h + 1) * LANES] = y.astype(BF16)

    for rows in blocks:
        kb = z_scr[rows, 1920:2048]
        ms = jnp.sum(kb * kb, axis=-1, keepdims=True) * (1.0 / D_ROPE)
        kpe_blk = rotary((kb * lax.rsqrt(ms + EPS)) * krgain_ref[...], rows)
        kpe_scr[rows, :] = kpe_blk
        x1_then_x2 = jnp.where(low_lanes, pltpu.roll(kpe_blk, LANES - X1_LO, 1), pltpu.roll(kpe_blk, ROPE_HALF, 1))
        kpe_ref[rows, :] = x1_then_x2[:, 0:D_ROPE]
    _expand_kv(ckvb_scr, kpe_scr, wkv_ref, kgain_ref, kseg_ref, kv_scr, sq_scr, ms_scr, k_ref, v_ref,
               between={0: lambda: conv_group(0), 4: lambda: conv_group(1)})

    cs_ref[0] = uext_scr[tm + 6:tm + 8, :]
    uext_scr[0:8, :] = uext_scr[tm:tm + 8, :]


def _odd_inproj(x, norm_g, ow, conv_init, n_batch, pos0):
    t = x.shape[0]
    tm = min(TOKEN_TILE, t // n_batch)
    tpb = t // n_batch // tm
    assert tpb * tm * n_batch == t
    row = lambda w: pl.BlockSpec((tm, w), lambda i: (i, 0))
    per_b = lambda r, w: pl.BlockSpec((1, r, w), lambda i: (i // tpb, 0, 0))
    small = [ow["conv_w"], ow["qan"], ow["wqb"], ow["qgain"], ow["kvan"], ow["krgain"],
             ow["invf"], ow["sgn"], ow["wkv"], ow["kgain"], ow["qseg"], ow["kseg"]]
    kv_w = D_HEADS * LANES
    v_w = D_HEADS * D_V
    return pl.pallas_call(
        functools.partial(_odd_inproj_kernel, tm=tm, tpb=tpb, pos0=pos0),
        grid=(t // tm,),
        in_specs=[row(D_MODEL), _resident((1, D_MODEL)), _resident(ow["w_in"].shape), per_b(8, C_WIDTH)]
                 + [_resident(a.shape) for a in small],
        out_specs=[row(C_WIDTH), row(kv_w), row(kv_w), row(v_w), row(D_KV_LORA), row(D_ROPE),
                   per_b(CONV_W - 1, C_WIDTH)],
        out_shape=[jax.ShapeDtypeStruct((t, C_WIDTH), BF16), jax.ShapeDtypeStruct((t, kv_w), BF16),
                   jax.ShapeDtypeStruct((t, kv_w), BF16), jax.ShapeDtypeStruct((t, v_w), BF16),
                   jax.ShapeDtypeStruct((t, D_KV_LORA), F32), jax.ShapeDtypeStruct((t, D_ROPE), F32),
                   jax.ShapeDtypeStruct((n_batch, CONV_W - 1, C_WIDTH), F32)],
        scratch_shapes=[pltpu.VMEM((tm, D_MODEL), BF16), pltpu.VMEM((tm, ODD_IN_PAD), F32),
                        pltpu.VMEM((tm + 8, C_WIDTH), F32), pltpu.VMEM((tm, D_Q_LORA), BF16),
                        pltpu.VMEM((tm, kv_w), F32), pltpu.VMEM((tm, kv_w + v_w), F32),
                        pltpu.VMEM((2, tpb * tm, LANES), F32), pltpu.VMEM((tm, D_KV_LORA), BF16),
                        pltpu.VMEM((tm, LANES), F32), pltpu.VMEM((2, tm, MXU_COLS), BF16),
                        pltpu.VMEM((2, tm, MXU_COLS), F32)],
        compiler_params=_cparams("arbitrary"),
        name="odd_inproj",
    )(x, norm_g.reshape(1, D_MODEL), ow["w_in"], conv_init, *small)


def _kv_expand_kernel(ckv_ref, kpe_ref, place_ref, wkv_ref, kgain_ref, kseg_ref, k_ref, v_ref,
                      kv_scr, ckvb_scr, kpe_scr, sq_scr, ms_scr):
    ckvb_scr[...] = ckv_ref[...].astype(BF16)
    kpe_scr[...] = _dot(kpe_ref[...].astype(BF16), place_ref[...])
    _expand_kv(ckvb_scr, kpe_scr, wkv_ref, kgain_ref, kseg_ref, kv_scr, sq_scr, ms_scr, k_ref, v_ref)


def _kv_expand(ckv, kpe, ow):
    t = ckv.shape[0]
    tm = min(TOKEN_TILE, t)
    row = lambda w: pl.BlockSpec((tm, w), lambda i: (i, 0))
    kv_w = D_HEADS * LANES
    v_w = D_HEADS * D_V
    return pl.pallas_call(
        _kv_expand_kernel,
        grid=(t // tm,),
        in_specs=[row(D_KV_LORA), row(D_ROPE), _resident(ow["place"].shape), _resident(ow["wkv"].shape),
                  _resident(ow["kgain"].shape), _resident(ow["kseg"].shape)],
        out_specs=[row(kv_w), row(v_w)],
        out_shape=[jax.ShapeDtypeStruct((t, kv_w), BF16), jax.ShapeDtypeStruct((t, v_w), BF16)],
        scratch_shapes=[pltpu.VMEM((tm, kv_w + v_w), F32), pltpu.VMEM((tm, D_KV_LORA), BF16),
                        pltpu.VMEM((tm, LANES), F32), pltpu.VMEM((2, tm, MXU_COLS), BF16),
                        pltpu.VMEM((2, tm, MXU_COLS), F32)],
        compiler_params=_cparams("arbitrary"),
        name="kv_expand",
    )(ckv, kpe, ow["place"], ow["wkv"], ow["kgain"], ow["kseg"])


def _odd_weights(w_in, conv_w, q_a_norm, w_q_b, kv_a_norm, w_kv_b, qn_nope, qn_rope, kn_nope, kn_rope):
    z = lambda n: jnp.zeros((n,), F32)
    qk = D_NOPE + D_ROPE
    w_in = _to_bf16(w_in)
    kr_blk = _to_head_lanes(jnp.concatenate([jnp.zeros((D_MODEL, D_NOPE), BF16), w_in[:, 1920:]], axis=1))
    w_pad = jnp.concatenate([w_in[:, :1920], kr_blk], axis=1)
    wqb = _to_head_lanes(w_q_b.astype(BF16).reshape(D_Q_LORA, D_HEADS, qk)).reshape(D_Q_LORA, D_HEADS * LANES)
    kvb = w_kv_b.astype(BF16).reshape(D_KV_LORA, D_HEADS, D_NOPE + D_V)
    wk = _to_head_lanes(jnp.concatenate([kvb[:, :, :D_NOPE], jnp.zeros((D_KV_LORA, D_HEADS, D_ROPE), BF16)], axis=-1))
    wkv = jnp.concatenate([wk.reshape(D_KV_LORA, D_HEADS * LANES),
                           kvb[:, :, D_NOPE:].reshape(D_KV_LORA, D_HEADS * D_V)], axis=1)
    inv = 1.0 / (ROPE_THETA ** (jnp.arange(ROPE_HALF, dtype=F32) / ROPE_HALF))
    ones = jnp.ones((ROPE_HALF,), F32)
    lanes = lambda v: _to_head_lanes(v).reshape(1, LANES)
    in_nope = lanes(jnp.concatenate([jnp.ones((D_NOPE,), F32), z(D_ROPE)]))
    in_rope = lanes(jnp.concatenate([z(D_NOPE), jnp.ones((D_ROPE,), F32)]))
    seg_nope = in_nope.T * in_nope * (1.0 / D_NOPE)
    seg_rope = in_rope.T * in_rope * (1.0 / D_ROPE)
    two_heads = lambda m: jnp.kron(jnp.eye(MXU_COLS // LANES, dtype=F32), m).astype(BF16)
    return {
        "qseg": two_heads(seg_nope + seg_rope),
        "kseg": two_heads(seg_nope),
        "w_in": w_pad,
        "conv_w": conv_w.astype(F32),
        "qan": q_a_norm.reshape(1, D_Q_LORA),
        "wqb": wqb,
        "qgain": lanes(jnp.concatenate([qn_nope, qn_rope])) * MLA_QSCALE,
        "kvan": kv_a_norm.reshape(1, D_KV_LORA),
        "krgain": lanes(jnp.concatenate([z(D_NOPE), kn_rope])),
        "invf": lanes(jnp.concatenate([z(D_NOPE), inv, inv])),
        "sgn": lanes(jnp.concatenate([z(D_NOPE), -ones, ones])),
        "wkv": wkv,
        "kgain": lanes(jnp.concatenate([kn_nope, z(D_ROPE)])),
        "place": _to_head_lanes(jnp.concatenate([jnp.zeros((D_ROPE, D_NOPE), BF16), jnp.eye(D_ROPE, dtype=BF16)], axis=1)),
    }


def _mla_prompt_kernel(q_ref, k_ref, v_ref, o_ref, s_scr, e_scr, r_scr, *, seq):
    qb = min(MLA_QBLOCK, seq)
    hb = qb // 2
    lo = lax.broadcasted_iota(jnp.int32, (qb, LANES), 1) < D_V
    first_half = lax.broadcasted_iota(jnp.int32, (CHUNK, LANES), 1) < CHUNK
    scores = lambda q, k: lax.dot_general(q, k, _NT, preferred_element_type=F32)
    for i in range(seq // qb):
        tk = qb * (i + 1)
        t0, t1 = tk - qb, tk - hb
        top, bot = slice(0, hb), slice(hb, qb)
        outs = []
        for h in range(2):
            hl = slice(h * LANES, (h + 1) * LANES)
            q0 = i * qb
            if t0 > 0:
                s_scr[h, :, 0:t0] = scores(q_ref[0, q0:q0 + qb, hl], k_ref[0, 0:t0, hl])
            s_scr[h, top, t0:t1] = scores(q_ref[0, q0:q0 + hb, hl], k_ref[0, t0:t1, hl])
            s_scr[h, bot, t0:tk] = scores(q_ref[0, q0 + hb:q0 + qb, hl], k_ref[0, t0:tk, hl])
            for rb in range(qb // CHUNK):
                rows = slice(rb * CHUNK, (rb + 1) * CHUNK)
                visible = t0 + CHUNK * (rb + 1)
                width = t1 if rb < hb // CHUNK else tk
                n_full, ragged = visible // LANES, visible % LANES != 0
                blk = lambda c: s_scr[h, rows, c * LANES:(c + 1) * LANES]
                cols = [blk(c) for c in range(n_full)]
                if ragged:
                    cols.append(jnp.where(first_half, blk(n_full), NEG_INF))
                mm = cols[0]
                for c in cols[1:]:
                    mm = jnp.maximum(mm, c)
                m = _row_stat(jnp.max, mm)
                acc = None
                for c in range(width // LANES):
                    if c < len(cols):
                        e = jnp.exp2(blk(c) - m)
                        if c >= n_full:
                            e = jnp.where(first_half, e, 0.0)
                        acc = e if acc is None else acc + e
                    else:
                        e = jnp.zeros((CHUNK, LANES), F32)
                    e_scr[h, rows, c * LANES:(c + 1) * LANES] = e.astype(BF16)
                r_scr[h, rows, :] = 1.0 / _row_stat(jnp.sum, acc)
            pv = jnp.concatenate([_dot(e_scr[h, top, t0:t1], v_ref[0, t0:t1, :]),
                                  _dot(e_scr[h, bot, t0:tk], v_ref[0, t0:tk, :])], axis=0)
            if t0 > 0:
                pv = pv + _dot(e_scr[h, :, 0:t0], v_ref[0, 0:t0, :])
            outs.append(pv * r_scr[h])
        o_ref[0, i * qb:(i + 1) * qb, :] = jnp.where(lo, outs[0], outs[1]).astype(BF16)


def _mla_prompt(q, k, v):
    nb, seq, _ = q.shape
    qb = min(MLA_QBLOCK, seq)
    return pl.pallas_call(
        functools.partial(_mla_prompt_kernel, seq=seq),
        grid=(nb, D_HEADS // 2),
        in_specs=[pl.BlockSpec((1, seq, 2 * LANES), lambda b, p: (b, 0, p)),
                  pl.BlockSpec((1, seq, 2 * LANES), lambda b, p: (b, 0, p)),
                  pl.BlockSpec((1, seq, LANES), lambda b, p: (b, 0, p))],
        out_specs=pl.BlockSpec((1, seq, LANES), lambda b, p: (b, 0, p)),
        out_shape=jax.ShapeDtypeStruct((nb, seq, D_HEADS * D_V), BF16),
        scratch_shapes=[pltpu.VMEM((2, qb, seq), F32), pltpu.VMEM((2, qb, seq), BF16),
                        pltpu.VMEM((2, qb, LANES), F32)],
        compiler_params=_cparams("arbitrary", "arbitrary"),
        name="mla_prompt",
    )(q, k, v)


def _mla_sample_kernel(q_ref, kc_ref, vc_ref, kn_ref, vn_ref, o_ref):
    nq = q_ref.shape[1]
    lo = lax.broadcasted_iota(jnp.int32, (nq, LANES), 1) < D_V
    outs = []
    for h in range(2):
        cols = slice(h * LANES, (h + 1) * LANES)
        q = q_ref[0, :, cols]
        s1 = lax.dot_general(q, kc_ref[0, :, cols], _NT, preferred_element_type=F32)
        s2 = lax.dot_general(q, kn_ref[0, :, cols], _NT, preferred_element_type=F32)
        m = jnp.maximum(jnp.max(s1, axis=-1, keepdims=True), jnp.max(s2, axis=-1, keepdims=True))
        e1 = jnp.exp2(s1 - m)
        e2 = jnp.exp2(s2 - m)
        den = jnp.sum(e1, axis=-1, keepdims=True) + jnp.sum(e2, axis=-1, keepdims=True)
        outs.append((_dot(e1.astype(BF16), vc_ref[0]) + _dot(e2.astype(BF16), vn_ref[0])) / den)
    o_ref[0] = jnp.where(lo, outs[0], outs[1]).astype(BF16)


def _mla_sample(q, kc, vc, kn, vn):
    nb, nq, _ = q.shape
    past = kc.shape[1]
    two = lambda r: pl.BlockSpec((1, r, 2 * LANES), lambda b, p: (b, 0, p))
    one = lambda r: pl.BlockSpec((1, r, LANES), lambda b, p: (b, 0, p))
    return pl.pallas_call(
        _mla_sample_kernel,
        grid=(nb, D_HEADS // 2),
        in_specs=[two(nq), two(past), one(past), two(nq), one(nq)],
        out_specs=one(nq),
        out_shape=jax.ShapeDtypeStruct((nb, nq, D_HEADS * D_V), BF16),
        compiler_params=_cparams("arbitrary", "arbitrary"),
        name="mla_sample",
    )(q, kc, vc, kn, vn)


def _even_layer(xp, xs, nb, seq, ndb, dseq, norm_g, w_in, w_out, a_qn, a_kn, a_sinks, b_qn, b_kn, b_rel,
                t5_table, ck_a, cv_a, ck_b, cv_b, ffn2):
    perm = _pair_perm()
    w_in = _to_bf16(w_in)
    w_out = _to_bf16(w_out)
    w_in_p = jnp.concatenate([w_in[:, :A_Q][:, perm], w_in[:, A_Q:]], axis=1)
    woa = w_out[:A_Q][perm]
    wob = w_out[A_Q:]
    ones = lambda n: jnp.ones((n,), F32)
    scale = HEAD_DIM ** -0.5
    gain_row = jnp.concatenate([jnp.tile(a_qn, A_HEADS) * scale, jnp.tile(a_kn, A_KV_HEADS), ones(A_KV),
                                jnp.tile(b_qn, B_HEADS) * scale, jnp.tile(b_kn, B_HEADS), ones(B_QKV)]
                               ).reshape(1, EVEN_IN).astype(F32)
    bias_a = _bias_a(t5_table)
    bias_b = _bias_b(b_rel)
    sink = jnp.broadcast_to(jnp.repeat(a_sinks.astype(F32), PAIR_ROWS)[:, None], (A_HEADS * PAIR_ROWS, LANES))
    la, lb = min(WINDOW, seq), min(B_REACH, seq)

    aq, ak, av, bq, bk, bv, cak, cav, cbk, cbv = _even_inproj(xp, norm_g, w_in_p, gain_row, nb, la, lb)
    r3 = lambda a: a.reshape(nb, seq, a.shape[-1])
    ya = _attn_a(r3(aq), r3(ak), r3(av), bias_a, sink, WINDOW, True)
    yb = _attn_b(r3(bq), r3(bk), r3(bv), bias_b, B_REACH, True)
    xp = _ffn(xp, *ffn2, out_proj=(ya.reshape(nb * seq, A_Q), yb.reshape(nb * seq, B_QKV), woa, wob))
    st_p = (cak.reshape(nb, la, A_KV_HEADS, HEAD_DIM), cav.reshape(nb, la, A_KV_HEADS, HEAD_DIM),
            cbk.reshape(nb, lb, B_HEADS, HEAD_DIM), cbv.reshape(nb, lb, B_HEADS, HEAD_DIM))

    ts = ndb * dseq
    aq, ak, av, bq, bk, bv, nak, nav, nbk, nbv = _even_inproj(xs, norm_g, w_in_p, gain_row, 1, ts, ts)
    pad_q = lambda a: jnp.pad(a.reshape(ndb, dseq, a.shape[-1]), ((0, 0), (0, PAIR_ROWS - dseq), (0, 0)))

    def window(cache, new):
        w = cache.shape[-2] * cache.shape[-1]
        full = jnp.concatenate([cache.reshape(ndb, -1, w), new.reshape(ndb, dseq, w)], axis=1)
        buf = jnp.pad(full, ((0, 0), (0, PAIR_ROWS - dseq), (0, 0))).astype(BF16)
        return full[:, dseq:].reshape(cache.shape), buf

    st_ak, kbuf_a = window(ck_a, nak)
    st_av, vbuf_a = window(cv_a, nav)
    st_bk, kbuf_b = window(ck_b, nbk)
    st_bv, vbuf_b = window(cv_b, nbv)
    ya = _attn_a(pad_q(aq), kbuf_a, vbuf_a, bias_a, sink, WINDOW, False)[:, :dseq]
    yb = _attn_b(pad_q(bq), kbuf_b, vbuf_b, bias_b, B_REACH, False)[:, :dseq]
    xs = _ffn(xs, *ffn2, out_proj=(ya.reshape(ts, A_Q), yb.reshape(ts, B_QKV), woa, wob))
    return xp, xs, st_p, (st_ak, st_av, st_bk, st_bv)


def _odd_layer(xp, xs, nb, seq, ndb, dseq, past, norm_g, ow, w_out, conv_prev, c_ckv, c_kpe, ffn2):
    w_out = _to_bf16(w_out)
    woc = w_out[:C_WIDTH]
    wod = w_out[C_WIDTH:]

    zero_init = jnp.zeros((nb, 8, C_WIDTH), F32)
    yc, q, k, v, ckv, kpe, cs = _odd_inproj(xp, norm_g, ow, zero_init, nb, 0)
    r3 = lambda a: a.reshape(nb, seq, a.shape[-1])
    yd = _mla_prompt(r3(q), r3(k), r3(v))
    xp = _ffn(xp, *ffn2, out_proj=(yc, yd.reshape(nb * seq, D_HEADS * D_V), woc, wod))
    st_p = (cs, ckv.reshape(nb, seq, D_KV_LORA), kpe.reshape(nb, seq, D_ROPE))

    ts = ndb * dseq
    init = jnp.pad(conv_prev.astype(F32), ((0, 0), (8 - (CONV_W - 1), 0), (0, 0)))
    yc, q, kn, vn, ckv, kpe, cs = _odd_inproj(xs, norm_g, ow, init, ndb, past)
    kc, vc = _kv_expand(c_ckv.reshape(ndb * past, D_KV_LORA), c_kpe.reshape(ndb * past, D_ROPE), ow)
    s3 = lambda a: a.reshape(ndb, dseq, a.shape[-1])
    yd = _mla_sample(s3(q), kc.reshape(ndb, past, -1), vc.reshape(ndb, past, -1), s3(kn), s3(vn))
    xs = _ffn(xs, *ffn2, out_proj=(yc, yd.reshape(ts, D_HEADS * D_V), woc, wod))
    st_s = (cs, ckv.reshape(ndb, dseq, D_KV_LORA), kpe.reshape(ndb, dseq, D_ROPE))
    return xp, xs, st_p, st_s


def kernel(x_prompt, x_sample, cache_a_k, cache_a_v, cache_b_k, cache_b_v, state_c_conv, cache_d_ckv, cache_d_kpe, ff1_norm, ff1_w_gu, ff1_w_down, mix_norm, ff2_norm, ff2_w_gu, ff2_w_down, t5_bias_table, ev_w_in, ev_w_out, a_q_norm, a_k_norm, a_sinks, b_q_norm, b_k_norm, b_rel_bias, od_w_in, od_w_out, c_conv_w, d_q_a_norm, d_w_q_b, d_kv_a_norm, d_w_kv_b, d_q_nope_norm, d_q_rope_norm, d_k_nope_norm, d_k_rope_norm):
    nb, seq, _ = x_prompt.shape
    ndb, dseq, _ = x_sample.shape
    past = cache_d_ckv.shape[2]
    depth = ff1_norm.shape[0]
    assert seq % TOKEN_TILE == 0 and dseq == CHUNK and past % CHUNK == 0
    xp = x_prompt.reshape(nb * seq, D_MODEL)
    xs = x_sample.reshape(ndb * dseq, D_MODEL)
    even_p, even_s, odd_p, odd_s = [], [], [], []
    for l in range(depth):
        i = l // 2
        ff1 = _ffn_weights(ff1_w_gu[l], ff1_w_down[l])
        ffn2 = (ff2_norm[l],) + _ffn_weights(ff2_w_gu[l], ff2_w_down[l])
        xp = _ffn(xp, ff1_norm[l], *ff1)
        xs = _ffn(xs, ff1_norm[l], *ff1)
        if l % 2 == 0:
            xp, xs, sp, ss = _even_layer(
                xp, xs, nb, seq, ndb, dseq, mix_norm[l], ev_w_in[i], ev_w_out[i], a_q_norm[i], a_k_norm[i],
                a_sinks[i], b_q_norm[i], b_k_norm[i], b_rel_bias[i], t5_bias_table,
                cache_a_k[i], cache_a_v[i], cache_b_k[i], cache_b_v[i], ffn2)
            even_p.append(sp)
            even_s.append(ss)
        else:
            ow = _odd_weights(od_w_in[i], c_conv_w[i], d_q_a_norm[i], d_w_q_b[i], d_kv_a_norm[i], d_w_kv_b[i],
                              d_q_nope_norm[i], d_q_rope_norm[i], d_k_nope_norm[i], d_k_rope_norm[i])
            xp, xs, sp, ss = _odd_layer(xp, xs, nb, seq, ndb, dseq, past, mix_norm[l], ow, od_w_out[i],
                                        state_c_conv[i], cache_d_ckv[i], cache_d_kpe[i], ffn2)
            odd_p.append(sp)
            odd_s.append(ss)
    stack = lambda group, j: jnp.stack([g[j] for g in group])
    return (xp.reshape(nb, seq, D_MODEL), xs.reshape(ndb, dseq, D_MODEL),
            stack(even_p, 0), stack(even_p, 1), stack(even_p, 2), stack(even_p, 3),
            stack(odd_p, 0), stack(odd_p, 1), stack(odd_p, 2),
            stack(even_s, 0), stack(even_s, 1), stack(even_s, 2), stack(even_s, 3),
            stack(odd_s, 0), stack(odd_s, 1), stack(odd_s, 2))
```

```python
import functools
import math

import jax
import jax.numpy as jnp
from jax import lax
from jax.experimental import pallas as pl
from jax.experimental.pallas import tpu as pltpu

F32 = jnp.float32
BF16 = jnp.bfloat16

D_MODEL = 1024
CHUNK = 64
HEAD_DIM = 64
EPS = 1e-6
A_HEADS = 8
A_KV_HEADS = 2
WINDOW = 128
T5_BUCKETS = 32
T5_MAX_DIST = 128
B_HEADS = 8
B_REACH = 512
B_MAX_REL = 128
C_WIDTH = 512
CONV_W = 3
D_HEADS = 8
D_Q_LORA = 256
D_KV_LORA = 128
D_NOPE = 64
D_ROPE = 32
D_V = 64
ROPE_THETA = 10000.0
FFN_DIM = 2816
A_Q = A_HEADS * HEAD_DIM
A_KV = A_KV_HEADS * HEAD_DIM
B_QKV = B_HEADS * HEAD_DIM
EVEN_IN = A_Q + 2 * A_KV + 3 * B_QKV

LANES = 128
TOKEN_TILE = 512
CAST_ROWS = 256
MXU_COLS = 256
FFN_CHUNK = MXU_COLS
PAIR_ROWS = 2 * CHUNK
MLA_QBLOCK = 512
VMEM_LIMIT_BYTES = 56 * 1024 * 1024
MLA_QSCALE = (D_NOPE + D_ROPE) ** -0.5 * math.log2(math.e)
NEG_INF = float("-inf")

_NT = (((1,), (1,)), ((), ()))


def _cparams(*sem):
    return pltpu.CompilerParams(dimension_semantics=sem, vmem_limit_bytes=VMEM_LIMIT_BYTES)


def _resident(shape):
    zeros = (0,) * len(shape)
    return pl.BlockSpec(shape, lambda *_: zeros, pipeline_mode=pl.Buffered(1))


def _rms(x, g):
    ms = jnp.mean(x * x, axis=-1, keepdims=True)
    return (x * lax.rsqrt(ms + EPS)) * g


def _dot(a, b):
    return jnp.dot(a, b, preferred_element_type=F32)


def _ffn_kernel(*refs, n_chunks, fused_out):
    if fused_out:
        (x_ref, ya_ref, yb_ref, woa_ref, wob_ref, g_ref, wgu_ref, wd_ref,
         o_ref, h_scr, a_scr) = refs
        y = _dot(ya_ref[...], woa_ref[...]) + _dot(yb_ref[...], wob_ref[...])
        o_ref[...] = x_ref[...] + y
        res_ref = o_ref
    else:
        x_ref, g_ref, wgu_ref, wd_ref, o_ref, h_scr, a_scr = refs
        res_ref = x_ref
    h_scr[...] = _rms(res_ref[...], g_ref[...]).astype(BF16)
    for j in range(n_chunks):
        h = h_scr[...]
        g = _dot(h, wgu_ref[:, j * FFN_CHUNK:(j + 1) * FFN_CHUNK])
        u = _dot(h, wgu_ref[:, FFN_DIM + j * FFN_CHUNK:FFN_DIM + (j + 1) * FFN_CHUNK])
        a_scr[:, j * FFN_CHUNK:(j + 1) * FFN_CHUNK] = ((g * jax.nn.sigmoid(g)) * u).astype(BF16)
    o_ref[...] = res_ref[...] + 0.5 * _dot(a_scr[...], wd_ref[...])


def _ffn(x, norm_g, wgu, wd, out_proj=None):
    t = x.shape[0]
    tm = min(TOKEN_TILE, t)
    n_chunks = FFN_DIM // FFN_CHUNK
    row = lambda w: pl.BlockSpec((tm, w), lambda i: (i, 0))
    in_specs = [row(D_MODEL)]
    args = [x]
    if out_proj is not None:
        ya, yb, woa, wob = out_proj
        in_specs += [row(ya.shape[1]), row(yb.shape[1]), _resident(woa.shape), _resident(wob.shape)]
        args += [ya, yb, woa, wob]
    in_specs += [_resident((1, D_MODEL)), _resident(wgu.shape), _resident(wd.shape)]
    args += [norm_g.reshape(1, D_MODEL), wgu, wd]
    return pl.pallas_call(
        functools.partial(_ffn_kernel, n_chunks=n_chunks, fused_out=out_proj is not None),
        grid=(t // tm,),
        in_specs=in_specs,
        out_specs=row(D_MODEL),
        out_shape=jax.ShapeDtypeStruct((t, D_MODEL), F32),
        scratch_shapes=[pltpu.VMEM((tm, D_MODEL), BF16), pltpu.VMEM((tm, FFN_DIM), BF16)],
        compiler_params=_cparams("arbitrary"),
        name="ffn_out" if out_proj is not None else "ffn",
    )(*args)


def _cast_kernel(x_ref, o_ref):
    o_ref[...] = x_ref[...].astype(BF16)


def _to_bf16(w, layer=None):
    r, c = w.shape[-2:]
    tr = CAST_ROWS if r % CAST_ROWS == 0 else r
    out_spec = pl.BlockSpec((tr, c), lambda i: (i, 0))
    in_spec = out_spec if layer is None else pl.BlockSpec((None, tr, c), lambda i: (layer, i, 0))
    return pl.pallas_call(
        _cast_kernel, grid=(r // tr,), in_specs=[in_spec], out_specs=out_spec,
        out_shape=jax.ShapeDtypeStruct((r, c), BF16),
        compiler_params=_cparams("arbitrary"), name="cast_bf16",
    )(w)


def _ffn_weights(w_gu, w_down, layer):
    return _to_bf16(w_gu, layer), _to_bf16(w_down, layer)


def _norm_halves(blk, gain):
    sq = blk * blk
    lo = lax.broadcasted_iota(jnp.int32, blk.shape, 1) < HEAD_DIM
    s_lo = jnp.sum(jnp.where(lo, sq, 0.0), axis=-1, keepdims=True)
    s_hi = jnp.sum(jnp.where(lo, 0.0, sq), axis=-1, keepdims=True)
    inv = jnp.where(lo, lax.rsqrt(s_lo * (1.0 / HEAD_DIM) + EPS), lax.rsqrt(s_hi * (1.0 / HEAD_DIM) + EPS))
    return (blk * inv) * gain


def _even_inproj_kernel(x_ref, g_ref, w_ref, gain_ref,
                        aq_ref, ak_ref, av_ref, bq_ref, bk_ref, bv_ref,
                        cak_ref, cav_ref, cbk_ref, cbv_ref, h_scr, z_scr, *, tm, tpb, rows_a, rows_b):
    i = pl.program_id(0)
    last = (i % tpb) == (tpb - 1)
    h_scr[...] = _rms(x_ref[...], g_ref[...]).astype(BF16)

    def lanes(c):
        return slice(c * LANES, (c + 1) * LANES)

    def emit(c):
        blk = z_scr[:, lanes(c)]
        gain = gain_ref[:, lanes(c)]
        if c < 4:
            aq_ref[:, lanes(c)] = _norm_halves(blk, gain).astype(BF16)
        elif c == 4:
            kn = _norm_halves(blk, gain)
            ak_ref[...] = kn.astype(BF16)
            z_scr[:, lanes(c)] = kn
        elif c == 5:
            av_ref[...] = blk.astype(BF16)
        elif c < 10:
            bq_ref[:, lanes(c - 6)] = _norm_halves(blk, gain).astype(BF16)
        elif c < 14:
            kn = _norm_halves(blk, gain)
            bk_ref[:, lanes(c - 10)] = kn.astype(BF16)
            z_scr[:, lanes(c)] = kn
        else:
            bv_ref[:, lanes(c - 14)] = blk.astype(BF16)

    for grp in range(EVEN_IN // MXU_COLS):
        cols = slice(grp * MXU_COLS, (grp + 1) * MXU_COLS)
        z_scr[:, cols] = _dot(h_scr[...], w_ref[:, cols])
        emit(2 * grp)
        emit(2 * grp + 1)

    @pl.when(last)
    def _():
        cak_ref[0] = z_scr[tm - rows_a:, lanes(4)]
        cav_ref[0] = z_scr[tm - rows_a:, lanes(5)]
        cbk_ref[0] = z_scr[tm - rows_b:, 10 * LANES:14 * LANES]
        cbv_ref[0] = z_scr[tm - rows_b:, 14 * LANES:18 * LANES]


def _even_inproj(x, norm_g, w_in, gain_row, n_batch, rows_a, rows_b):
    t = x.shape[0]
    tm = min(TOKEN_TILE, t)
    tpb = t // n_batch // tm
    assert tpb * tm * n_batch == t and rows_a <= tm and rows_b <= tm
    row = lambda w: pl.BlockSpec((tm, w), lambda i: (i, 0))
    cache = lambda r, w: pl.BlockSpec((1, r, w), lambda i: (i // tpb, 0, 0))
    bshape = lambda w: jax.ShapeDtypeStruct((t, w), BF16)
    cshape = lambda r, w: jax.ShapeDtypeStruct((n_batch, r, w), F32)
    return pl.pallas_call(
        functools.partial(_even_inproj_kernel, tm=tm, tpb=tpb, rows_a=rows_a, rows_b=rows_b),
        grid=(t // tm,),
        in_specs=[row(D_MODEL), _resident((1, D_MODEL)), _resident(w_in.shape), _resident(gain_row.shape)],
        out_specs=[row(A_Q), row(A_KV), row(A_KV), row(B_QKV), row(B_QKV), row(B_QKV),
                   cache(rows_a, A_KV), cache(rows_a, A_KV), cache(rows_b, B_QKV), cache(rows_b, B_QKV)],
        out_shape=[bshape(A_Q), bshape(A_KV), bshape(A_KV), bshape(B_QKV), bshape(B_QKV), bshape(B_QKV),
                   cshape(rows_a, A_KV), cshape(rows_a, A_KV), cshape(rows_b, B_QKV), cshape(rows_b, B_QKV)],
        scratch_shapes=[pltpu.VMEM((tm, D_MODEL), BF16), pltpu.VMEM((tm, EVEN_IN), F32)],
        compiler_params=_cparams("arbitrary"),
        name="even_inproj",
    )(x, norm_g.reshape(1, D_MODEL), w_in, gain_row)


def _pair_perm():
    idx = []
    for p in range(4):
        for half in range(2):
            head = p + 4 * half
            idx.extend(range(head * HEAD_DIM, (head + 1) * HEAD_DIM))
    return jnp.asarray(idx, dtype=jnp.int32)


def _fill_padded(buf, src_ref, ctx):
    buf[0:ctx, :] = jnp.zeros((ctx, buf.shape[1]), buf.dtype)
    buf[ctx:, :] = src_ref[0]


def _windows(k_ref, v_ref, pad_scr, ctx, pad_front, step_rows):
    win = ctx + step_rows
    if pad_front:
        kbuf, vbuf = pad_scr
        _fill_padded(kbuf, k_ref, ctx)
        _fill_padded(vbuf, v_ref, ctx)
        return (lambda r0: kbuf[pl.ds(r0, win), :]), (lambda r0: vbuf[pl.ds(r0, win), :])
    return (lambda r0: k_ref[0, pl.ds(r0, win), :]), (lambda r0: v_ref[0, pl.ds(r0, win), :])


def _fold_lanes(x, op):
    acc = x[:, :LANES]
    for c in range(1, x.shape[1] // LANES):
        acc = op(acc, x[:, c * LANES:(c + 1) * LANES])
    return acc


def _row_stat(reduce_fn, x):
    return jnp.broadcast_to(reduce_fn(x, axis=-1, keepdims=True), x.shape)


PAIR_SLOTS = 4


def _run_steps(step, n_steps, n_front, step_rows):
    for t in range(n_steps):
        step(t * step_rows, t % PAIR_SLOTS, t < n_front)


def _attn_a_kernel(q_ref, k_ref, v_ref, bias_ref, sink_ref, o_ref, s_scr, e_scr, *pad_scr, nq, ctx, pad_front):
    win = ctx + PAIR_ROWS
    kwin, vwin = _windows(k_ref, v_ref, pad_scr, ctx, pad_front, PAIR_ROWS)
    lo = lax.broadcasted_iota(jnp.int32, (PAIR_ROWS, LANES), 1) < HEAD_DIM

    def pair(r0, slot, masked):
        q_lo, q_hi = [], []
        for p in range(4):
            qp = q_ref[0, pl.ds(r0, PAIR_ROWS), p * LANES:(p + 1) * LANES]
            zero = jnp.zeros_like(qp)
            q_lo.append(jnp.where(lo, qp, zero))
            q_hi.append(jnp.where(lo, zero, qp))
        q2 = jnp.concatenate(q_lo + q_hi, axis=0)
        s_scr[slot] = lax.dot_general(q2, kwin(r0), _NT, preferred_element_type=F32)
        for head in range(A_HEADS):
            rows = slice(head * PAIR_ROWS, (head + 1) * PAIR_ROWS)
            s = s_scr[slot, rows, :] + bias_ref[rows, :]
            if masked:
                col = lax.broadcasted_iota(jnp.int32, (PAIR_ROWS, win), 1)
                s = jnp.where(col + r0 >= ctx, s, NEG_INF)
            sk = sink_ref[rows, :]
            m = jnp.maximum(_row_stat(jnp.max, _fold_lanes(s, jnp.maximum)), sk)
            e = [jnp.exp(s[:, c * LANES:(c + 1) * LANES] - m) for c in range(win // LANES)]
            den = _row_stat(jnp.sum, functools.reduce(jnp.add, e)) + jnp.exp(sk - m)
            inv = 1.0 / den
            for c in range(win // LANES):
                e_scr[slot, rows, c * LANES:(c + 1) * LANES] = (e[c] * inv).astype(BF16)
        o2 = _dot(e_scr[slot], vwin(r0))
        half = 4 * PAIR_ROWS
        for p in range(4):
            o = jnp.where(lo, o2[p * PAIR_ROWS:(p + 1) * PAIR_ROWS],
                          o2[half + p * PAIR_ROWS:half + (p + 1) * PAIR_ROWS])
            o_ref[0, pl.ds(r0, PAIR_ROWS), p * LANES:(p + 1) * LANES] = o.astype(BF16)

    _run_steps(pair, nq // PAIR_ROWS, ctx // PAIR_ROWS if pad_front else 0, PAIR_ROWS)


def _attn_a(q, k, v, bias, sink, ctx, pad_front):
    nb, nq, _ = q.shape
    nk = k.shape[1]
    assert nq % PAIR_ROWS == 0 and nk == (nq if pad_front else ctx + nq)
    per_b = lambda r, w: pl.BlockSpec((1, r, w), lambda b: (b, 0, 0))
    rows, win = A_HEADS * PAIR_ROWS, ctx + PAIR_ROWS
    scratch = [pltpu.VMEM((PAIR_SLOTS, rows, win), F32), pltpu.VMEM((PAIR_SLOTS, rows, win), BF16)]
    if pad_front:
        scratch += [pltpu.VMEM((ctx + nq, A_KV), BF16)] * 2
    return pl.pallas_call(
        functools.partial(_attn_a_kernel, nq=nq, ctx=ctx, pad_front=pad_front),
        grid=(nb,),
        in_specs=[per_b(nq, A_Q), per_b(nk, A_KV), per_b(nk, A_KV), _resident(bias.shape), _resident(sink.shape)],
        out_specs=per_b(nq, A_Q),
        out_shape=jax.ShapeDtypeStruct((nb, nq, A_Q), BF16),
        scratch_shapes=scratch,
        compiler_params=_cparams("arbitrary"),
        name="attn_a",
    )(q, k, v, bias, sink)


B_STEP_ROWS = 4 * CHUNK


def _attn_b_kernel(q_ref, k_ref, v_ref, bias_ref, o_ref, s_scr, e_scr, r_scr, *pad_scr, nq, ctx, pad_front):
    win = ctx + B_STEP_ROWS
    kwin, vwin = _windows(k_ref, v_ref, pad_scr, ctx, pad_front, B_STEP_ROWS)
    lo = lax.broadcasted_iota(jnp.int32, (B_STEP_ROWS, LANES), 1) < HEAD_DIM
    chunks = B_STEP_ROWS // CHUNK

    def step(r0, slot, masked):
        qp = q_ref[0, pl.ds(r0, B_STEP_ROWS), :]
        zero = jnp.zeros_like(qp)
        q2 = jnp.concatenate([jnp.where(lo, qp, zero), jnp.where(lo, zero, qp)], axis=0)
        s_scr[slot] = lax.dot_general(q2, kwin(r0), _NT, preferred_element_type=F32)
        for rb in range(2 * chunks):
            rows = slice(rb * CHUNK, (rb + 1) * CHUNK)
            ci = rb % chunks
            c_lo, c_hi = CHUNK * ci // LANES, -(-(ctx + CHUNK * (ci + 1)) // LANES)
            band = slice(c_lo * LANES, c_hi * LANES)
            s = s_scr[slot, rows, band] + bias_ref[0, rows, band]
            if masked:
                col = lax.broadcasted_iota(jnp.int32, s.shape, 1) + c_lo * LANES
                s = jnp.where(col + r0 >= ctx, s, NEG_INF)
            m = _row_stat(jnp.max, _fold_lanes(s, jnp.maximum))
            e = [jnp.exp(s[:, c * LANES:(c + 1) * LANES] - m) for c in range(c_hi - c_lo)]
            for c in range(win // LANES):
                blk = e[c - c_lo].astype(BF16) if c_lo <= c < c_hi else jnp.zeros((CHUNK, LANES), BF16)
                e_scr[slot, rows, c * LANES:(c + 1) * LANES] = blk
            r_scr[slot, rows, :] = 1.0 / _row_stat(jnp.sum, functools.reduce(jnp.add, e))
        o2 = _dot(e_scr[slot], vwin(r0)) * r_scr[slot]
        o = jnp.where(lo, o2[:B_STEP_ROWS], o2[B_STEP_ROWS:])
        o_ref[0, pl.ds(r0, B_STEP_ROWS), :] = o.astype(BF16)

    _run_steps(step, nq // B_STEP_ROWS, -(-ctx // B_STEP_ROWS) if pad_front else 0, B_STEP_ROWS)


def _attn_b(q, k, v, bias, ctx, pad_front):
    nb, nq, _ = q.shape
    nk = k.shape[1]
    assert nq % B_STEP_ROWS == 0 and nk == (nq if pad_front else ctx + nq)
    blk = lambda r: pl.BlockSpec((1, r, LANES), lambda b, p: (b, 0, p))
    rows, win = 2 * B_STEP_ROWS, ctx + B_STEP_ROWS
    scratch = [pltpu.VMEM((PAIR_SLOTS, rows, win), F32), pltpu.VMEM((PAIR_SLOTS, rows, win), BF16),
               pltpu.VMEM((PAIR_SLOTS, rows, LANES), F32)]
    if pad_front:
        scratch += [pltpu.VMEM((ctx + nq, LANES), BF16)] * 2
    return pl.pallas_call(
        functools.partial(_attn_b_kernel, nq=nq, ctx=ctx, pad_front=pad_front),
        grid=(nb, B_HEADS // 2),
        in_specs=[blk(nq), blk(nk), blk(nk),
                  pl.BlockSpec((1,) + bias.shape[1:], lambda b, p: (p, 0, 0))],
        out_specs=blk(nq),
        out_shape=jax.ShapeDtypeStruct((nb, nq, B_QKV), BF16),
        scratch_shapes=scratch,
        compiler_params=_cparams("arbitrary", "arbitrary"),
        name="attn_b",
    )(q, k, v, bias)


def _t5_bucket(rel):
    nb = T5_BUCKETS // 2
    max_exact = nb // 2
    n = -rel
    ret = jnp.where(n < 0, nb, 0)
    n = jnp.abs(n)
    nf = jnp.maximum(n, 1).astype(F32)
    large = max_exact + (jnp.log(nf / max_exact) / math.log(T5_MAX_DIST / max_exact)
                         * (nb - max_exact)).astype(jnp.int32)
    large = jnp.minimum(large, nb - 1)
    return ret + jnp.where(n < max_exact, n, large)


def _band_tables(ctx, n_rows, value_of_rel):
    n_cols = ctx + n_rows
    period = n_cols + n_rows - 1
    d = jnp.concatenate([jnp.arange(0, n_cols), jnp.arange(-(n_rows - 1), 0)])
    ext = value_of_rel(d - ctx)
    heads = ext.shape[0]
    flat = jnp.tile(ext, (1, n_rows))[:, :n_rows * (period - 1)]
    tb = flat.reshape(heads, n_rows, period - 1)[:, :, :n_cols]
    chunk = jnp.arange(n_rows)[:, None] // CHUNK
    j = jnp.arange(n_cols)[None, :]
    visible = (j >= CHUNK * chunk) & (j < ctx + CHUNK * (chunk + 1))
    return jnp.where(visible[None], tb, NEG_INF)


def _bias_a(t5_table):
    tb = _band_tables(WINDOW, PAIR_ROWS, lambda rel: t5_table.astype(F32)[_t5_bucket(rel)].T)
    return tb.reshape(A_HEADS * PAIR_ROWS, WINDOW + PAIR_ROWS)


def _bias_b(b_rel):
    tb = _band_tables(B_REACH, B_STEP_ROWS,
                      lambda rel: b_rel.astype(F32)[:, jnp.clip(rel, -B_MAX_REL, B_MAX_REL) + B_MAX_REL])
    return tb.reshape(B_HEADS // 2, 2 * B_STEP_ROWS, B_REACH + B_STEP_ROWS)


ODD_IN_PAD = 2048
ROPE_HALF = D_ROPE // 2
X1_LO = LANES // 2
NOPE_SPLIT = X1_LO - ROPE_HALF


def _head_lane_source():
    zero = D_NOPE + D_ROPE
    src = []
    for lane in range(LANES):
        if lane < ROPE_HALF:
            src.append(D_NOPE + ROPE_HALF + lane)
        elif lane < X1_LO:
            src.append(lane - ROPE_HALF)
        elif lane < X1_LO + ROPE_HALF:
            src.append(D_NOPE + lane - X1_LO)
        elif lane < X1_LO + ROPE_HALF + D_NOPE - NOPE_SPLIT:
            src.append(NOPE_SPLIT + lane - X1_LO - ROPE_HALF)
        else:
            src.append(zero)
    return src


def _to_head_lanes(a):
    padded = jnp.concatenate([a, jnp.zeros(a.shape[:-1] + (1,), a.dtype)], axis=-1)
    return padded[..., jnp.asarray(_head_lane_source(), dtype=jnp.int32)]


def _lane_masks(shape):
    lane = lax.broadcasted_iota(jnp.int32, shape, 1)
    rope = (lane < ROPE_HALF) | ((lane >= X1_LO) & (lane < X1_LO + ROPE_HALF))
    used = lane < X1_LO + ROPE_HALF + D_NOPE - NOPE_SPLIT
    return used & ~rope, rope


def _rope(y, cosf, sinf):
    return y * cosf + pltpu.roll(y, LANES // 2, 1) * sinf


ROW_BLOCK = 128


def _row_blocks(tm):
    rb = min(ROW_BLOCK, tm)
    return [slice(r, r + rb) for r in range(0, tm, rb)]


def _segment_mean_squares(src_scr, cols, seg_ref, sq_scr, ms_scr, slot):
    for rows in _row_blocks(src_scr.shape[0]):
        v = src_scr[rows, cols]
        sq_scr[slot, rows, :] = (v * v).astype(BF16)
    ms_scr[slot] = _dot(sq_scr[slot], seg_ref[...])


def _expand_kv(ckvb_ref, kpe_blk_ref, wkv_ref, kgain_ref, kseg_ref, kv_scr, sq_scr, ms_scr, k_ref, v_ref,
               between=None):
    k_cols = D_HEADS * LANES
    tm = kv_scr.shape[0]
    for grp in range(kv_scr.shape[1] // MXU_COLS):
        if between and grp in between:
            between[grp]()
        cols = slice(grp * MXU_COLS, (grp + 1) * MXU_COLS)
        kv_scr[:, cols] = _dot(ckvb_ref[...], wkv_ref[:, cols])
        if cols.start < k_cols:
            slot = grp % 2
            _segment_mean_squares(kv_scr, cols, kseg_ref, sq_scr, ms_scr, slot)
        for rows in _row_blocks(tm):
            if cols.start < k_cols:
                for half in range(2):
                    h = 2 * grp + half
                    kb = kv_scr[rows, h * LANES:(h + 1) * LANES]
                    ms = ms_scr[slot, rows, half * LANES:(half + 1) * LANES]
                    kn = (kb * lax.rsqrt(ms + EPS)) * kgain_ref[...]
                    k_ref[rows, h * LANES:(h + 1) * LANES] = (kn + kpe_blk_ref[rows, :]).astype(BF16)
            else:
                v_ref[rows, cols.start - k_cols:cols.stop - k_cols] = kv_scr[rows, cols].astype(BF16)


def _odd_inproj_kernel(x_ref, g_ref, w_ref, cinit_ref, convw_ref, qan_ref, wqb_ref, qgain_ref,
                       kvan_ref, krgain_ref, invf_ref, sgn_ref, wkv_ref, kgain_ref, qseg_ref, kseg_ref,
                       yc_ref, q_ref, k_ref, v_ref, ckv_ref, kpe_ref, cs_ref,
                       h_scr, z_scr, uext_scr, qn_scr, q_scr, kv_scr, rot_scr, ckvb_scr, kpe_scr, sq_scr, ms_scr,
                       *, tm, tpb, pos0):
    i = pl.program_id(0)
    tile = i % tpb
    off = pl.multiple_of(tile * tm, tm)
    blocks = _row_blocks(tm)
    rb = blocks[0].stop

    @pl.when(tile == 0)
    def _():
        uext_scr[0:8, :] = cinit_ref[0]

    @pl.when(i < tpb)
    def _():
        row = lax.broadcasted_iota(jnp.int32, (tm, LANES), 0)
        ang = (row + (tile * tm + pos0)).astype(F32) * invf_ref[...]
        rot_scr[0, pl.ds(off, tm), :] = jnp.cos(ang)
        rot_scr[1, pl.ds(off, tm), :] = jnp.sin(ang) * sgn_ref[...]

    h_scr[...] = _rms(x_ref[...], g_ref[...]).astype(BF16)
    low_lanes = lax.broadcasted_iota(jnp.int32, (rb, LANES), 1) < ROPE_HALF

    def proj(c0, c1):
        z_scr[:, c0:c1] = _dot(h_scr[...], w_ref[:, c0:c1])

    def rotary(y, rows):
        cosf = rot_scr[0, pl.ds(off + rows.start, rb), :]
        sinf = rot_scr[1, pl.ds(off + rows.start, rb), :]
        return _rope(y, cosf, sinf)

    def conv_group(grp):
        c0 = grp * MXU_COLS
        cols = slice(c0, c0 + MXU_COLS)
        for base in (0, C_WIDTH, 2 * C_WIDTH):
            proj(base + c0, base + c0 + MXU_COLS)
        for rows in blocks:
            up = slice(rows.start + 8, rows.stop + 8)
            uext_scr[up, cols] = (z_scr[rows, C_WIDTH + c0:C_WIDTH + c0 + MXU_COLS]
                                  * z_scr[rows, 2 * C_WIDTH + c0:2 * C_WIDTH + c0 + MXU_COLS])
        for rows in blocks:
            yconv = convw_ref[0:1, cols] * uext_scr[rows.start + 6:rows.stop + 6, cols]
            yconv = yconv + convw_ref[1:2, cols] * uext_scr[rows.start + 7:rows.stop + 7, cols]
            yconv = yconv + convw_ref[2:3, cols] * uext_scr[rows.start + 8:rows.stop + 8, cols]
            yc_ref[rows, cols] = (z_scr[rows, cols] * yconv).astype(BF16)

    proj(1536, 1792)
    proj(1792, 2048)

    for rows in blocks:
        qn_scr[rows, :] = _rms(z_scr[rows, 1536:1792], qan_ref[...]).astype(BF16)
        ckv = _rms(z_scr[rows, 1792:1920], kvan_ref[...])
        ckv_ref[rows, :] = ckv
        ckvb_scr[rows, :] = ckv.astype(BF16)

    for grp in range(D_HEADS * LANES // MXU_COLS):
        cols = slice(grp * MXU_COLS, (grp + 1) * MXU_COLS)
        q_scr[:, cols] = _dot(qn_scr[...], wqb_ref[:, cols])
        slot = grp % 2
        _segment_mean_squares(q_scr, cols, qseg_ref, sq_scr, ms_scr, slot)
        for half in range(2):
            h = 2 * grp + half
            for rows in blocks:
                blk = q_scr[rows, h * LANES:(h + 1) * LANES]
                inv = lax.rsqrt(ms_scr[slot, rows, half * LANES:(half + 1) * LANES] + EPS)
                y = rotary((blk * inv) * qgain_ref[...], rows)
                q_ref[rows, h * LANES:(h + 1) * LANES] = y.astype(BF16)

    for rows in blocks:
        kb = z_scr[rows, 1920:2048]
        ms = jnp.sum(kb * kb, axis=-1, keepdims=True) * (1.0 / D_ROPE)
        kpe_blk = rotary((kb * lax.rsqrt(ms + EPS)) * krgain_ref[...], rows)
        kpe_scr[rows, :] = kpe_blk
        x1_then_x2 = jnp.where(low_lanes, pltpu.roll(kpe_blk, LANES - X1_LO, 1), pltpu.roll(kpe_blk, ROPE_HALF, 1))
        kpe_ref[rows, :] = x1_then_x2[:, 0:D_ROPE]
    _expand_kv(ckvb_scr, kpe_scr, wkv_ref, kgain_ref, kseg_ref, kv_scr, sq_scr, ms_scr, k_ref, v_ref,
               between={0: lambda: conv_group(0), 4: lambda: conv_group(1)})

    cs_ref[0] = uext_scr[tm + 6:tm + 8, :]
    uext_scr[0:8, :] = uext_scr[tm:tm + 8, :]


def _odd_inproj(x, norm_g, ow, conv_init, n_batch, pos0):
    t = x.shape[0]
    tm = min(TOKEN_TILE, t // n_batch)
    tpb = t // n_batch // tm
    assert tpb * tm * n_batch == t
    row = lambda w: pl.BlockSpec((tm, w), lambda i: (i, 0))
    per_b = lambda r, w: pl.BlockSpec((1, r, w), lambda i: (i // tpb, 0, 0))
    small = [ow["conv_w"], ow["qan"], ow["wqb"], ow["qgain"], ow["kvan"], ow["krgain"],
             ow["invf"], ow["sgn"], ow["wkv"], ow["kgain"], ow["qseg"], ow["kseg"]]
    kv_w = D_HEADS * LANES
    v_w = D_HEADS * D_V
    return pl.pallas_call(
        functools.partial(_odd_inproj_kernel, tm=tm, tpb=tpb, pos0=pos0),
        grid=(t // tm,),
        in_specs=[row(D_MODEL), _resident((1, D_MODEL)), _resident(ow["w_in"].shape), per_b(8, C_WIDTH)]
                 + [_resident(a.shape) for a in small],
        out_specs=[row(C_WIDTH), row(kv_w), row(kv_w), row(v_w), row(D_KV_LORA), row(D_ROPE),
                   per_b(CONV_W - 1, C_WIDTH)],
        out_shape=[jax.ShapeDtypeStruct((t, C_WIDTH), BF16), jax.ShapeDtypeStruct((t, kv_w), BF16),
                   jax.ShapeDtypeStruct((t, kv_w), BF16), jax.ShapeDtypeStruct((t, v_w), BF16),
                   jax.ShapeDtypeStruct((t, D_KV_LORA), F32), jax.ShapeDtypeStruct((t, D_ROPE), F32),
                   jax.ShapeDtypeStruct((n_batch, CONV_W - 1, C_WIDTH), F32)],
        scratch_shapes=[pltpu.VMEM((tm, D_MODEL), BF16), pltpu.VMEM((tm, ODD_IN_PAD), F32),
                        pltpu.VMEM((tm + 8, C_WIDTH), F32), pltpu.VMEM((tm, D_Q_LORA), BF16),
                        pltpu.VMEM((tm, kv_w), F32), pltpu.VMEM((tm, kv_w + v_w), F32),
                        pltpu.VMEM((2, tpb * tm, LANES), F32), pltpu.VMEM((tm, D_KV_LORA), BF16),
                        pltpu.VMEM((tm, LANES), F32), pltpu.VMEM((2, tm, MXU_COLS), BF16),
                        pltpu.VMEM((2, tm, MXU_COLS), F32)],
        compiler_params=_cparams("arbitrary"),
        name="odd_inproj",
    )(x, norm_g.reshape(1, D_MODEL), ow["w_in"], conv_init, *small)


def _kv_expand_kernel(ckv_ref, kpe_ref, place_ref, wkv_ref, kgain_ref, kseg_ref, k_ref, v_ref,
                      kv_scr, ckvb_scr, kpe_scr, sq_scr, ms_scr):
    ckvb_scr[...] = ckv_ref[...].astype(BF16)
    kpe_scr[...] = _dot(kpe_ref[...].astype(BF16), place_ref[...])
    _expand_kv(ckvb_scr, kpe_scr, wkv_ref, kgain_ref, kseg_ref, kv_scr, sq_scr, ms_scr, k_ref, v_ref)


def _kv_expand(ckv, kpe, ow):
    t = ckv.shape[0]
    tm = min(TOKEN_TILE, t)
    row = lambda w: pl.BlockSpec((tm, w), lambda i: (i, 0))
    kv_w = D_HEADS * LANES
    v_w = D_HEADS * D_V
    return pl.pallas_call(
        _kv_expand_kernel,
        grid=(t // tm,),
        in_specs=[row(D_KV_LORA), row(D_ROPE), _resident(ow["place"].shape), _resident(ow["wkv"].shape),
                  _resident(ow["kgain"].shape), _resident(ow["kseg"].shape)],
        out_specs=[row(kv_w), row(v_w)],
        out_shape=[jax.ShapeDtypeStruct((t, kv_w), BF16), jax.ShapeDtypeStruct((t, v_w), BF16)],
        scratch_shapes=[pltpu.VMEM((tm, kv_w + v_w), F32), pltpu.VMEM((tm, D_KV_LORA), BF16),
                        pltpu.VMEM((tm, LANES), F32), pltpu.VMEM((2, tm, MXU_COLS), BF16),
                        pltpu.VMEM((2, tm, MXU_COLS), F32)],
        compiler_params=_cparams("arbitrary"),
        name="kv_expand",
    )(ckv, kpe, ow["place"], ow["wkv"], ow["kgain"], ow["kseg"])


def _odd_weights(w_in, conv_w, q_a_norm, w_q_b, kv_a_norm, w_kv_b, qn_nope, qn_rope, kn_nope, kn_rope):
    z = lambda n: jnp.zeros((n,), F32)
    qk = D_NOPE + D_ROPE
    w_in = _to_bf16(w_in)
    kr_blk = _to_head_lanes(jnp.concatenate([jnp.zeros((D_MODEL, D_NOPE), BF16), w_in[:, 1920:]], axis=1))
    w_pad = jnp.concatenate([w_in[:, :1920], kr_blk], axis=1)
    wqb = _to_head_lanes(w_q_b.astype(BF16).reshape(D_Q_LORA, D_HEADS, qk)).reshape(D_Q_LORA, D_HEADS * LANES)
    kvb = w_kv_b.astype(BF16).reshape(D_KV_LORA, D_HEADS, D_NOPE + D_V)
    wk = _to_head_lanes(jnp.concatenate([kvb[:, :, :D_NOPE], jnp.zeros((D_KV_LORA, D_HEADS, D_ROPE), BF16)], axis=-1))
    wkv = jnp.concatenate([wk.reshape(D_KV_LORA, D_HEADS * LANES),
                           kvb[:, :, D_NOPE:].reshape(D_KV_LORA, D_HEADS * D_V)], axis=1)
    inv = 1.0 / (ROPE_THETA ** (jnp.arange(ROPE_HALF, dtype=F32) / ROPE_HALF))
    ones = jnp.ones((ROPE_HALF,), F32)
    lanes = lambda v: _to_head_lanes(v).reshape(1, LANES)
    in_nope = lanes(jnp.concatenate([jnp.ones((D_NOPE,), F32), z(D_ROPE)]))
    in_rope = lanes(jnp.concatenate([z(D_NOPE), jnp.ones((D_ROPE,), F32)]))
    seg_nope = in_nope.T * in_nope * (1.0 / D_NOPE)
    seg_rope = in_rope.T * in_rope * (1.0 / D_ROPE)
    two_heads = lambda m: jnp.kron(jnp.eye(MXU_COLS // LANES, dtype=F32), m).astype(BF16)
    return {
        "qseg": two_heads(seg_nope + seg_rope),
        "kseg": two_heads(seg_nope),
        "w_in": w_pad,
        "conv_w": conv_w.astype(F32),
        "qan": q_a_norm.reshape(1, D_Q_LORA),
        "wqb": wqb,
        "qgain": lanes(jnp.concatenate([qn_nope, qn_rope])) * MLA_QSCALE,
        "kvan": kv_a_norm.reshape(1, D_KV_LORA),
        "krgain": lanes(jnp.concatenate([z(D_NOPE), kn_rope])),
        "invf": lanes(jnp.concatenate([z(D_NOPE), inv, inv])),
        "sgn": lanes(jnp.concatenate([z(D_NOPE), -ones, ones])),
        "wkv": wkv,
        "kgain": lanes(jnp.concatenate([kn_nope, z(D_ROPE)])),
        "place": _to_head_lanes(jnp.concatenate([jnp.zeros((D_ROPE, D_NOPE), BF16), jnp.eye(D_ROPE, dtype=BF16)], axis=1)),
    }


def _mla_prompt_kernel(q_ref, k_ref, v_ref, o_ref, s_scr, e_scr, r_scr, *, seq):
    qb = min(MLA_QBLOCK, seq)
    hb = qb // 2
    lo = lax.broadcasted_iota(jnp.int32, (qb, LANES), 1) < D_V
    first_half = lax.broadcasted_iota(jnp.int32, (CHUNK, LANES), 1) < CHUNK
    scores = lambda q, k: lax.dot_general(q, k, _NT, preferred_element_type=F32)
    for i in range(seq // qb):
        tk = qb * (i + 1)
        t0, t1 = tk - qb, tk - hb
        top, bot = slice(0, hb), slice(hb, qb)
        outs = []
        for h in range(2):
            hl = slice(h * LANES, (h + 1) * LANES)
            q0 = i * qb
            if t0 > 0:
                s_scr[h, :, 0:t0] = scores(q_ref[0, q0:q0 + qb, hl], k_ref[0, 0:t0, hl])
            s_scr[h, top, t0:t1] = scores(q_ref[0, q0:q0 + hb, hl], k_ref[0, t0:t1, hl])
            s_scr[h, bot, t0:tk] = scores(q_ref[0, q0 + hb:q0 + qb, hl], k_ref[0, t0:tk, hl])
            for rb in range(qb // CHUNK):
                rows = slice(rb * CHUNK, (rb + 1) * CHUNK)
                visible = t0 + CHUNK * (rb + 1)
                width = t1 if rb < hb // CHUNK else tk
                n_full, ragged = visible // LANES, visible % LANES != 0
                blk = lambda c: s_scr[h, rows, c * LANES:(c + 1) * LANES]
                cols = [blk(c) for c in range(n_full)]
                if ragged:
                    cols.append(jnp.where(first_half, blk(n_full), NEG_INF))
                mm = cols[0]
                for c in cols[1:]:
                    mm = jnp.maximum(mm, c)
                m = _row_stat(jnp.max, mm)
                acc = None
                for c in range(width // LANES):
                    if c < len(cols):
                        e = jnp.exp2(blk(c) - m)
                        if c >= n_full:
                            e = jnp.where(first_half, e, 0.0)
                        acc = e if acc is None else acc + e
                    else:
                        e = jnp.zeros((CHUNK, LANES), F32)
                    e_scr[h, rows, c * LANES:(c + 1) * LANES] = e.astype(BF16)
                r_scr[h, rows, :] = 1.0 / _row_stat(jnp.sum, acc)
            pv = jnp.concatenate([_dot(e_scr[h, top, t0:t1], v_ref[0, t0:t1, :]),
                                  _dot(e_scr[h, bot, t0:tk], v_ref[0, t0:tk, :])], axis=0)
            if t0 > 0:
                pv = pv + _dot(e_scr[h, :, 0:t0], v_ref[0, 0:t0, :])
            outs.append(pv * r_scr[h])
        o_ref[0, i * qb:(i + 1) * qb, :] = jnp.where(lo, outs[0], outs[1]).astype(BF16)


def _mla_prompt(q, k, v):
    nb, seq, _ = q.shape
    qb = min(MLA_QBLOCK, seq)
    return pl.pallas_call(
        functools.partial(_mla_prompt_kernel, seq=seq),
        grid=(nb, D_HEADS // 2),
        in_specs=[pl.BlockSpec((1, seq, 2 * LANES), lambda b, p: (b, 0, p)),
                  pl.BlockSpec((1, seq, 2 * LANES), lambda b, p: (b, 0, p)),
                  pl.BlockSpec((1, seq, LANES), lambda b, p: (b, 0, p))],
        out_specs=pl.BlockSpec((1, seq, LANES), lambda b, p: (b, 0, p)),
        out_shape=jax.ShapeDtypeStruct((nb, seq, D_HEADS * D_V), BF16),
        scratch_shapes=[pltpu.VMEM((2, qb, seq), F32), pltpu.VMEM((2, qb, seq), BF16),
                        pltpu.VMEM((2, qb, LANES), F32)],
        compiler_params=_cparams("arbitrary", "arbitrary"),
        name="mla_prompt",
    )(q, k, v)


def _mla_sample_kernel(q_ref, kc_ref, vc_ref, kn_ref, vn_ref, o_ref):
    nq = q_ref.shape[1]
    lo = lax.broadcasted_iota(jnp.int32, (nq, LANES), 1) < D_V
    outs = []
    for h in range(2):
        cols = slice(h * LANES, (h + 1) * LANES)
        q = q_ref[0, :, cols]
        s1 = lax.dot_general(q, kc_ref[0, :, cols], _NT, preferred_element_type=F32)
        s2 = lax.dot_general(q, kn_ref[0, :, cols], _NT, preferred_element_type=F32)
        m = jnp.maximum(jnp.max(s1, axis=-1, keepdims=True), jnp.max(s2, axis=-1, keepdims=True))
        e1 = jnp.exp2(s1 - m)
        e2 = jnp.exp2(s2 - m)
        den = jnp.sum(e1, axis=-1, keepdims=True) + jnp.sum(e2, axis=-1, keepdims=True)
        outs.append((_dot(e1.astype(BF16), vc_ref[0]) + _dot(e2.astype(BF16), vn_ref[0])) / den)
    o_ref[0] = jnp.where(lo, outs[0], outs[1]).astype(BF16)


def _mla_sample(q, kc, vc, kn, vn):
    nb, nq, _ = q.shape
    past = kc.shape[1]
    two = lambda r: pl.BlockSpec((1, r, 2 * LANES), lambda b, p: (b, 0, p))
    one = lambda r: pl.BlockSpec((1, r, LANES), lambda b, p: (b, 0, p))
    return pl.pallas_call(
        _mla_sample_kernel,
        grid=(nb, D_HEADS // 2),
        in_specs=[two(nq), two(past), one(past), two(nq), one(nq)],
        out_specs=one(nq),
        out_shape=jax.ShapeDtypeStruct((nb, nq, D_HEADS * D_V), BF16),
        compiler_params=_cparams("arbitrary", "arbitrary"),
        name="mla_sample",
    )(q, kc, vc, kn, vn)


def _even_layer(xp, xs, nb, seq, ndb, dseq, norm_g, w_in, w_out, a_qn, a_kn, a_sinks, b_qn, b_kn, b_rel,
                t5_table, ck_a, cv_a, ck_b, cv_b, ffn2):
    perm = _pair_perm()
    w_in = _to_bf16(w_in)
    w_out = _to_bf16(w_out)
    w_in_p = jnp.concatenate([w_in[:, :A_Q][:, perm], w_in[:, A_Q:]], axis=1)
    woa = w_out[:A_Q][perm]
    wob = w_out[A_Q:]
    ones = lambda n: jnp.ones((n,), F32)
    scale = HEAD_DIM ** -0.5
    gain_row = jnp.concatenate([jnp.tile(a_qn, A_HEADS) * scale, jnp.tile(a_kn, A_KV_HEADS), ones(A_KV),
                                jnp.tile(b_qn, B_HEADS) * scale, jnp.tile(b_kn, B_HEADS), ones(B_QKV)]
                               ).reshape(1, EVEN_IN).astype(F32)
    bias_a = _bias_a(t5_table)
    bias_b = _bias_b(b_rel)
    sink = jnp.broadcast_to(jnp.repeat(a_sinks.astype(F32), PAIR_ROWS)[:, None], (A_HEADS * PAIR_ROWS, LANES))
    la, lb = min(WINDOW, seq), min(B_REACH, seq)

    aq, ak, av, bq, bk, bv, cak, cav, cbk, cbv = _even_inproj(xp, norm_g, w_in_p, gain_row, nb, la, lb)
    r3 = lambda a: a.reshape(nb, seq, a.shape[-1])
    ya = _attn_a(r3(aq), r3(ak), r3(av), bias_a, sink, WINDOW, True)
    yb = _attn_b(r3(bq), r3(bk), r3(bv), bias_b, B_REACH, True)
    xp = _ffn(xp, *ffn2, out_proj=(ya.reshape(nb * seq, A_Q), yb.reshape(nb * seq, B_QKV), woa, wob))
    st_p = (cak.reshape(nb, la, A_KV_HEADS, HEAD_DIM), cav.reshape(nb, la, A_KV_HEADS, HEAD_DIM),
            cbk.reshape(nb, lb, B_HEADS, HEAD_DIM), cbv.reshape(nb, lb, B_HEADS, HEAD_DIM))

    ts = ndb * dseq
    aq, ak, av, bq, bk, bv, nak, nav, nbk, nbv = _even_inproj(xs, norm_g, w_in_p, gain_row, 1, ts, ts)
    pad_q = lambda a, rows: jnp.pad(a.reshape(ndb, dseq, a.shape[-1]), ((0, 0), (0, rows - dseq), (0, 0)))

    def window(cache, new, rows):
        w = cache.shape[-2] * cache.shape[-1]
        full = jnp.concatenate([cache.reshape(ndb, -1, w), new.reshape(ndb, dseq, w)], axis=1)
        buf = jnp.pad(full, ((0, 0), (0, rows - dseq), (0, 0))).astype(BF16)
        return full[:, dseq:].reshape(cache.shape), buf

    st_ak, kbuf_a = window(ck_a, nak, PAIR_ROWS)
    st_av, vbuf_a = window(cv_a, nav, PAIR_ROWS)
    st_bk, kbuf_b = window(ck_b, nbk, B_STEP_ROWS)
    st_bv, vbuf_b = window(cv_b, nbv, B_STEP_ROWS)
    ya = _attn_a(pad_q(aq, PAIR_ROWS), kbuf_a, vbuf_a, bias_a, sink, WINDOW, False)[:, :dseq]
    yb = _attn_b(pad_q(bq, B_STEP_ROWS), kbuf_b, vbuf_b, bias_b, B_REACH, False)[:, :dseq]
    xs = _ffn(xs, *ffn2, out_proj=(ya.reshape(ts, A_Q), yb.reshape(ts, B_QKV), woa, wob))
    return xp, xs, st_p, (st_ak, st_av, st_bk, st_bv)


def _odd_layer(xp, xs, nb, seq, ndb, dseq, past, norm_g, ow, w_out, conv_prev, c_ckv, c_kpe, ffn2):
    w_out = _to_bf16(w_out)
    woc = w_out[:C_WIDTH]
    wod = w_out[C_WIDTH:]

    zero_init = jnp.zeros((nb, 8, C_WIDTH), F32)
    yc, q, k, v, ckv, kpe, cs = _odd_inproj(xp, norm_g, ow, zero_init, nb, 0)
    r3 = lambda a: a.reshape(nb, seq, a.shape[-1])
    yd = _mla_prompt(r3(q), r3(k), r3(v))
    xp = _ffn(xp, *ffn2, out_proj=(yc, yd.reshape(nb * seq, D_HEADS * D_V), woc, wod))
    st_p = (cs, ckv.reshape(nb, seq, D_KV_LORA), kpe.reshape(nb, seq, D_ROPE))

    ts = ndb * dseq
    init = jnp.pad(conv_prev.astype(F32), ((0, 0), (8 - (CONV_W - 1), 0), (0, 0)))
    yc, q, kn, vn, ckv, kpe, cs = _odd_inproj(xs, norm_g, ow, init, ndb, past)
    kc, vc = _kv_expand(c_ckv.reshape(ndb * past, D_KV_LORA), c_kpe.reshape(ndb * past, D_ROPE), ow)
    s3 = lambda a: a.reshape(ndb, dseq, a.shape[-1])
    yd = _mla_sample(s3(q), kc.reshape(ndb, past, -1), vc.reshape(ndb, past, -1), s3(kn), s3(vn))
    xs = _ffn(xs, *ffn2, out_proj=(yc, yd.reshape(ts, D_HEADS * D_V), woc, wod))
    st_s = (cs, ckv.reshape(ndb, dseq, D_KV_LORA), kpe.reshape(ndb, dseq, D_ROPE))
    return xp, xs, st_p, st_s


def kernel(x_prompt, x_sample, cache_a_k, cache_a_v, cache_b_k, cache_b_v, state_c_conv, cache_d_ckv, cache_d_kpe, ff1_norm, ff1_w_gu, ff1_w_down, mix_norm, ff2_norm, ff2_w_gu, ff2_w_down, t5_bias_table, ev_w_in, ev_w_out, a_q_norm, a_k_norm, a_sinks, b_q_norm, b_k_norm, b_rel_bias, od_w_in, od_w_out, c_conv_w, d_q_a_norm, d_w_q_b, d_kv_a_norm, d_w_kv_b, d_q_nope_norm, d_q_rope_norm, d_k_nope_norm, d_k_rope_norm):
    nb, seq, _ = x_prompt.shape
    ndb, dseq, _ = x_sample.shape
    past = cache_d_ckv.shape[2]
    depth = ff1_norm.shape[0]
    assert seq % TOKEN_TILE == 0 and dseq == CHUNK and past % CHUNK == 0
    xp = x_prompt.reshape(nb * seq, D_MODEL)
    xs = x_sample.reshape(ndb * dseq, D_MODEL)
    even_p, even_s, odd_p, odd_s = [], [], [], []
    for l in range(depth):
        i = l // 2
        ff1 = _ffn_weights(ff1_w_gu, ff1_w_down, l)
        ffn2 = (ff2_norm[l],) + _ffn_weights(ff2_w_gu, ff2_w_down, l)
        xp = _ffn(xp, ff1_norm[l], *ff1)
        xs = _ffn(xs, ff1_norm[l], *ff1)
        if l % 2 == 0:
            xp, xs, sp, ss = _even_layer(
                xp, xs, nb, seq, ndb, dseq, mix_norm[l], ev_w_in[i], ev_w_out[i], a_q_norm[i], a_k_norm[i],
                a_sinks[i], b_q_norm[i], b_k_norm[i], b_rel_bias[i], t5_bias_table,
                cache_a_k[i], cache_a_v[i], cache_b_k[i], cache_b_v[i], ffn2)
            even_p.append(sp)
            even_s.append(ss)
        else:
            ow = _odd_weights(od_w_in[i], c_conv_w[i], d_q_a_norm[i], d_w_q_b[i], d_kv_a_norm[i], d_w_kv_b[i],
                              d_q_nope_norm[i], d_q_rope_norm[i], d_k_nope_norm[i], d_k_rope_norm[i])
            xp, xs, sp, ss = _odd_layer(xp, xs, nb, seq, ndb, dseq, past, mix_norm[l], ow, od_w_out[i],
                                        state_c_conv[i], cache_d_ckv[i], cache_d_kpe[i], ffn2)
            odd_p.append(sp)
            odd_s.append(ss)
    stack = lambda group, j: jnp.stack([g[j] for g in group])
    return (xp.reshape(nb, seq, D_MODEL), xs.reshape(ndb, dseq, D_MODEL),
            stack(even_p, 0), stack(even_p, 1), stack(even_p, 2), stack(even_p, 3),
            stack(odd_p, 0), stack(odd_p, 1), stack(odd_p, 2),
            stack(even_s, 0), stack(even_s, 1), stack(even_s, 2), stack(even_s, 3),
            stack(odd_s, 0), stack(odd_s, 1), stack(odd_s, 2))
```

```python
import functools
import math

import jax
import jax.numpy as jnp
from jax import lax
from jax.experimental import pallas as pl
from jax.experimental.pallas import tpu as pltpu

F32 = jnp.float32
BF16 = jnp.bfloat16

D_MODEL = 1024
CHUNK = 64
HEAD_DIM = 64
EPS = 1e-6
A_HEADS = 8
A_KV_HEADS = 2
WINDOW = 128
T5_BUCKETS = 32
T5_MAX_DIST = 128
B_HEADS = 8
B_REACH = 512
B_MAX_REL = 128
C_WIDTH = 512
CONV_W = 3
D_HEADS = 8
D_Q_LORA = 256
D_KV_LORA = 128
D_NOPE = 64
D_ROPE = 32
D_V = 64
ROPE_THETA = 10000.0
FFN_DIM = 2816
A_Q = A_HEADS * HEAD_DIM
A_KV = A_KV_HEADS * HEAD_DIM
B_QKV = B_HEADS * HEAD_DIM
EVEN_IN = A_Q + 2 * A_KV + 3 * B_QKV

LANES = 128
TOKEN_TILE = 512
CAST_ROWS = 256
MXU_COLS = 256
FFN_CHUNK = MXU_COLS
PAIR_ROWS = 2 * CHUNK
MLA_QBLOCK = 512
VMEM_LIMIT_BYTES = 56 * 1024 * 1024
MLA_QSCALE = (D_NOPE + D_ROPE) ** -0.5 * math.log2(math.e)
NEG_INF = float("-inf")

_NT = (((1,), (1,)), ((), ()))


def _cparams(*sem):
    return pltpu.CompilerParams(dimension_semantics=sem, vmem_limit_bytes=VMEM_LIMIT_BYTES)


def _resident(shape):
    zeros = (0,) * len(shape)
    return pl.BlockSpec(shape, lambda *_: zeros, pipeline_mode=pl.Buffered(1))


def _rms(x, g):
    ms = jnp.mean(x * x, axis=-1, keepdims=True)
    return (x * lax.rsqrt(ms + EPS)) * g


def _dot(a, b):
    return jnp.dot(a, b, preferred_element_type=F32)


def _ffn_kernel(*refs, n_chunks, fused_out):
    if fused_out:
        (x_ref, ya_ref, yb_ref, woa_ref, wob_ref, g_ref, wgu_ref, wd_ref,
         o_ref, h_scr, a_scr) = refs
        y = _dot(ya_ref[...], woa_ref[...]) + _dot(yb_ref[...], wob_ref[...])
        o_ref[...] = x_ref[...] + y
        res_ref = o_ref
    else:
        x_ref, g_ref, wgu_ref, wd_ref, o_ref, h_scr, a_scr = refs
        res_ref = x_ref
    h_scr[...] = _rms(res_ref[...], g_ref[...]).astype(BF16)
    for j in range(n_chunks):
        h = h_scr[...]
        g = _dot(h, wgu_ref[:, j * FFN_CHUNK:(j + 1) * FFN_CHUNK])
        u = _dot(h, wgu_ref[:, FFN_DIM + j * FFN_CHUNK:FFN_DIM + (j + 1) * FFN_CHUNK])
        a_scr[:, j * FFN_CHUNK:(j + 1) * FFN_CHUNK] = ((g * jax.nn.sigmoid(g)) * u).astype(BF16)
    o_ref[...] = res_ref[...] + 0.5 * _dot(a_scr[...], wd_ref[...])


def _ffn(x, norm_g, wgu, wd, out_proj=None):
    t = x.shape[0]
    tm = min(TOKEN_TILE, t)
    n_chunks = FFN_DIM // FFN_CHUNK
    row = lambda w: pl.BlockSpec((tm, w), lambda i: (i, 0))
    in_specs = [row(D_MODEL)]
    args = [x]
    if out_proj is not None:
        ya, yb, woa, wob = out_proj
        in_specs += [row(ya.shape[1]), row(yb.shape[1]), _resident(woa.shape), _resident(wob.shape)]
        args += [ya, yb, woa, wob]
    in_specs += [_resident((1, D_MODEL)), _resident(wgu.shape), _resident(wd.shape)]
    args += [norm_g.reshape(1, D_MODEL), wgu, wd]
    return pl.pallas_call(
        functools.partial(_ffn_kernel, n_chunks=n_chunks, fused_out=out_proj is not None),
        grid=(t // tm,),
        in_specs=in_specs,
        out_specs=row(D_MODEL),
        out_shape=jax.ShapeDtypeStruct((t, D_MODEL), F32),
        scratch_shapes=[pltpu.VMEM((tm, D_MODEL), BF16), pltpu.VMEM((tm, FFN_DIM), BF16)],
        compiler_params=_cparams("arbitrary"),
        name="ffn_out" if out_proj is not None else "ffn",
    )(*args)


def _cast_kernel(x_ref, o_ref):
    o_ref[...] = x_ref[...].astype(BF16)


def _to_bf16(w, layer=None):
    r, c = w.shape[-2:]
    tr = CAST_ROWS if r % CAST_ROWS == 0 else r
    out_spec = pl.BlockSpec((tr, c), lambda i: (i, 0))
    in_spec = out_spec if layer is None else pl.BlockSpec((None, tr, c), lambda i: (layer, i, 0))
    return pl.pallas_call(
        _cast_kernel, grid=(r // tr,), in_specs=[in_spec], out_specs=out_spec,
        out_shape=jax.ShapeDtypeStruct((r, c), BF16),
        compiler_params=_cparams("arbitrary"), name="cast_bf16",
    )(w)


def _ffn_weights(w_gu, w_down, layer):
    return _to_bf16(w_gu, layer), _to_bf16(w_down, layer)


def _norm_halves(blk, gain):
    sq = blk * blk
    lo = lax.broadcasted_iota(jnp.int32, blk.shape, 1) < HEAD_DIM
    s_lo = jnp.sum(jnp.where(lo, sq, 0.0), axis=-1, keepdims=True)
    s_hi = jnp.sum(jnp.where(lo, 0.0, sq), axis=-1, keepdims=True)
    inv = jnp.where(lo, lax.rsqrt(s_lo * (1.0 / HEAD_DIM) + EPS), lax.rsqrt(s_hi * (1.0 / HEAD_DIM) + EPS))
    return (blk * inv) * gain


def _even_inproj_kernel(x_ref, g_ref, w_ref, gain_ref,
                        aq_ref, ak_ref, av_ref, bq_ref, bk_ref, bv_ref,
                        cak_ref, cav_ref, cbk_ref, cbv_ref, h_scr, z_scr, *, tm, tpb, rows_a, rows_b):
    i = pl.program_id(0)
    last = (i % tpb) == (tpb - 1)
    h_scr[...] = _rms(x_ref[...], g_ref[...]).astype(BF16)

    def lanes(c):
        return slice(c * LANES, (c + 1) * LANES)

    def emit(c):
        blk = z_scr[:, lanes(c)]
        gain = gain_ref[:, lanes(c)]
        if c < 4:
            aq_ref[:, lanes(c)] = _norm_halves(blk, gain).astype(BF16)
        elif c == 4:
            kn = _norm_halves(blk, gain)
            ak_ref[...] = kn.astype(BF16)
            z_scr[:, lanes(c)] = kn
        elif c == 5:
            av_ref[...] = blk.astype(BF16)
        elif c < 10:
            bq_ref[:, lanes(c - 6)] = _norm_halves(blk, gain).astype(BF16)
        elif c < 14:
            kn = _norm_halves(blk, gain)
            bk_ref[:, lanes(c - 10)] = kn.astype(BF16)
            z_scr[:, lanes(c)] = kn
        else:
            bv_ref[:, lanes(c - 14)] = blk.astype(BF16)

    for grp in range(EVEN_IN // MXU_COLS):
        cols = slice(grp * MXU_COLS, (grp + 1) * MXU_COLS)
        z_scr[:, cols] = _dot(h_scr[...], w_ref[:, cols])
        emit(2 * grp)
        emit(2 * grp + 1)

    @pl.when(last)
    def _():
        cak_ref[0] = z_scr[tm - rows_a:, lanes(4)]
        cav_ref[0] = z_scr[tm - rows_a:, lanes(5)]
        cbk_ref[0] = z_scr[tm - rows_b:, 10 * LANES:14 * LANES]
        cbv_ref[0] = z_scr[tm - rows_b:, 14 * LANES:18 * LANES]


def _even_inproj(x, norm_g, w_in, gain_row, n_batch, rows_a, rows_b):
    t = x.shape[0]
    tm = min(TOKEN_TILE, t)
    tpb = t // n_batch // tm
    assert tpb * tm * n_batch == t and rows_a <= tm and rows_b <= tm
    row = lambda w: pl.BlockSpec((tm, w), lambda i: (i, 0))
    cache = lambda r, w: pl.BlockSpec((1, r, w), lambda i: (i // tpb, 0, 0))
    bshape = lambda w: jax.ShapeDtypeStruct((t, w), BF16)
    cshape = lambda r, w: jax.ShapeDtypeStruct((n_batch, r, w), F32)
    return pl.pallas_call(
        functools.partial(_even_inproj_kernel, tm=tm, tpb=tpb, rows_a=rows_a, rows_b=rows_b),
        grid=(t // tm,),
        in_specs=[row(D_MODEL), _resident((1, D_MODEL)), _resident(w_in.shape), _resident(gain_row.shape)],
        out_specs=[row(A_Q), row(A_KV), row(A_KV), row(B_QKV), row(B_QKV), row(B_QKV),
                   cache(rows_a, A_KV), cache(rows_a, A_KV), cache(rows_b, B_QKV), cache(rows_b, B_QKV)],
        out_shape=[bshape(A_Q), bshape(A_KV), bshape(A_KV), bshape(B_QKV), bshape(B_QKV), bshape(B_QKV),
                   cshape(rows_a, A_KV), cshape(rows_a, A_KV), cshape(rows_b, B_QKV), cshape(rows_b, B_QKV)],
        scratch_shapes=[pltpu.VMEM((tm, D_MODEL), BF16), pltpu.VMEM((tm, EVEN_IN), F32)],
        compiler_params=_cparams("arbitrary"),
        name="even_inproj",
    )(x, norm_g.reshape(1, D_MODEL), w_in, gain_row)


def _pair_heads(a, axis):
    shape = a.shape
    split = shape[:axis] + (2, A_HEADS // 2, HEAD_DIM) + shape[axis + 1:]
    return a.reshape(split).swapaxes(axis, axis + 1).reshape(shape)


def _fill_padded(buf, src_ref, ctx):
    buf[0:ctx, :] = jnp.zeros((ctx, buf.shape[1]), buf.dtype)
    buf[ctx:, :] = src_ref[0]


def _windows(k_ref, v_ref, pad_scr, ctx, pad_front, step_rows, stream):
    win = ctx + step_rows
    if pad_front:
        kbuf, vbuf = pad_scr
        _fill_padded(kbuf, k_ref, ctx)
        _fill_padded(vbuf, v_ref, ctx)
        return (lambda r0: kbuf[pl.ds(r0, win), :]), (lambda r0: vbuf[pl.ds(r0, win), :])
    return (lambda r0: k_ref[stream, pl.ds(r0, win), :]), (lambda r0: v_ref[stream, pl.ds(r0, win), :])


def _fold_lanes(x, op):
    acc = x[:, :LANES]
    for c in range(1, x.shape[1] // LANES):
        acc = op(acc, x[:, c * LANES:(c + 1) * LANES])
    return acc


def _row_stat(reduce_fn, x):
    return jnp.broadcast_to(reduce_fn(x, axis=-1, keepdims=True), x.shape)


PAIR_SLOTS = 4


def _run_steps(make_step, n_streams, n_steps, n_front, step_rows):
    for stream in range(n_streams):
        step = make_step(stream)
        for t in range(n_steps):
            step(t * step_rows, (stream * n_steps + t) % PAIR_SLOTS, t < n_front)


def _band_bias(ext_row, n_rows, ctx):
    win = ctx + n_rows
    period = ext_row.shape[1]
    t = pltpu.roll(jnp.broadcast_to(ext_row, (n_rows, period)), 0, 1, stride=1, stride_axis=0)[:, :win]
    chunk = lax.broadcasted_iota(jnp.int32, (n_rows, win), 0) // CHUNK
    col = lax.broadcasted_iota(jnp.int32, (n_rows, win), 1)
    visible = (col >= CHUNK * chunk) & (col < ctx + CHUNK * (chunk + 1))
    return jnp.where(visible, t, NEG_INF)


def _attn_a_kernel(q_ref, k_ref, v_ref, ext_ref, sink_ref, o_ref, bias_ref, s_scr, e_scr, *pad_scr,
                   nq, ctx, pad_front):
    win = ctx + PAIR_ROWS
    lo = lax.broadcasted_iota(jnp.int32, (PAIR_ROWS, LANES), 1) < HEAD_DIM

    @pl.when(pl.program_id(0) == 0)
    def _():
        for head in range(A_HEADS):
            bias_ref[head * PAIR_ROWS:(head + 1) * PAIR_ROWS, :] = _band_bias(
                ext_ref[head:head + 1, :], PAIR_ROWS, ctx)

    def make_step(stream):
        kwin, vwin = _windows(k_ref, v_ref, pad_scr, ctx, pad_front, PAIR_ROWS, stream)
        return functools.partial(pair, stream, kwin, vwin)

    def pair(stream, kwin, vwin, r0, slot, masked):
        q_lo, q_hi = [], []
        for p in range(4):
            qp = q_ref[stream, pl.ds(r0, PAIR_ROWS), p * LANES:(p + 1) * LANES]
            zero = jnp.zeros_like(qp)
            q_lo.append(jnp.where(lo, qp, zero))
            q_hi.append(jnp.where(lo, zero, qp))
        q2 = jnp.concatenate(q_lo + q_hi, axis=0)
        s_scr[slot] = lax.dot_general(q2, kwin(r0), _NT, preferred_element_type=F32)
        for head in range(A_HEADS):
            rows = slice(head * PAIR_ROWS, (head + 1) * PAIR_ROWS)
            s = s_scr[slot, rows, :] + bias_ref[rows, :]
            if masked:
                col = lax.broadcasted_iota(jnp.int32, (PAIR_ROWS, win), 1)
                s = jnp.where(col + r0 >= ctx, s, NEG_INF)
            sk = sink_ref[rows, :]
            m = jnp.maximum(_row_stat(jnp.max, _fold_lanes(s, jnp.maximum)), sk)
            e = [jnp.exp(s[:, c * LANES:(c + 1) * LANES] - m) for c in range(win // LANES)]
            den = _row_stat(jnp.sum, functools.reduce(jnp.add, e)) + jnp.exp(sk - m)
            inv = 1.0 / den
            for c in range(win // LANES):
                e_scr[slot, rows, c * LANES:(c + 1) * LANES] = (e[c] * inv).astype(BF16)
        o2 = _dot(e_scr[slot], vwin(r0))
        half = 4 * PAIR_ROWS
        for p in range(4):
            o = jnp.where(lo, o2[p * PAIR_ROWS:(p + 1) * PAIR_ROWS],
                          o2[half + p * PAIR_ROWS:half + (p + 1) * PAIR_ROWS])
            o_ref[stream, pl.ds(r0, PAIR_ROWS), p * LANES:(p + 1) * LANES] = o.astype(BF16)

    _run_steps(make_step, q_ref.shape[0], nq // PAIR_ROWS, ctx // PAIR_ROWS if pad_front else 0, PAIR_ROWS)


def _streams_per_block(nb, pad_front):
    return 1 if pad_front else nb


def _attn_a(q, k, v, ext, sink, ctx, pad_front):
    nb, nq, _ = q.shape
    nk = k.shape[1]
    assert nq % PAIR_ROWS == 0 and nk == (nq if pad_front else ctx + nq)
    bb = _streams_per_block(nb, pad_front)
    per_b = lambda r, w: pl.BlockSpec((bb, r, w), lambda b: (b, 0, 0))
    rows, win = A_HEADS * PAIR_ROWS, ctx + PAIR_ROWS
    scratch = [pltpu.VMEM((rows, win), F32),
               pltpu.VMEM((PAIR_SLOTS, rows, win), F32), pltpu.VMEM((PAIR_SLOTS, rows, win), BF16)]
    if pad_front:
        scratch += [pltpu.VMEM((ctx + nq, A_KV), BF16)] * 2
    return pl.pallas_call(
        functools.partial(_attn_a_kernel, nq=nq, ctx=ctx, pad_front=pad_front),
        grid=(nb // bb,),
        in_specs=[per_b(nq, A_Q), per_b(nk, A_KV), per_b(nk, A_KV), _resident(ext.shape), _resident(sink.shape)],
        out_specs=per_b(nq, A_Q),
        out_shape=jax.ShapeDtypeStruct((nb, nq, A_Q), BF16),
        scratch_shapes=scratch,
        compiler_params=_cparams("arbitrary"),
        name="attn_a",
    )(q, k, v, ext, sink)


B_STEP_ROWS = 4 * CHUNK


def _attn_b_kernel(q_ref, k_ref, v_ref, ext_ref, o_ref, bias_scr, s_scr, e_scr, r_scr, *pad_scr,
                   nq, ctx, pad_front):
    win = ctx + B_STEP_ROWS
    lo = lax.broadcasted_iota(jnp.int32, (B_STEP_ROWS, LANES), 1) < HEAD_DIM
    chunks = B_STEP_ROWS // CHUNK
    pair_idx = pl.program_id(1)

    @pl.when(pl.program_id(0) == 0)
    def _():
        for half in range(2):
            bias_scr[pair_idx, half * B_STEP_ROWS:(half + 1) * B_STEP_ROWS, :] = _band_bias(
                ext_ref[0, half:half + 1, :], B_STEP_ROWS, ctx)

    bias_ref = bias_scr.at[pair_idx]

    def make_step(stream):
        kwin, vwin = _windows(k_ref, v_ref, pad_scr, ctx, pad_front, B_STEP_ROWS, stream)
        return functools.partial(step, stream, kwin, vwin)

    def step(stream, kwin, vwin, r0, slot, masked):
        qp = q_ref[stream, pl.ds(r0, B_STEP_ROWS), :]
        zero = jnp.zeros_like(qp)
        q2 = jnp.concatenate([jnp.where(lo, qp, zero), jnp.where(lo, zero, qp)], axis=0)
        s_scr[slot] = lax.dot_general(q2, kwin(r0), _NT, preferred_element_type=F32)
        for rb in range(2 * chunks):
            rows = slice(rb * CHUNK, (rb + 1) * CHUNK)
            ci = rb % chunks
            c_lo, c_hi = CHUNK * ci // LANES, -(-(ctx + CHUNK * (ci + 1)) // LANES)
            band = slice(c_lo * LANES, c_hi * LANES)
            s = s_scr[slot, rows, band] + bias_ref[rows, band]
            if masked:
                col = lax.broadcasted_iota(jnp.int32, s.shape, 1) + c_lo * LANES
                s = jnp.where(col + r0 >= ctx, s, NEG_INF)
            m = _row_stat(jnp.max, _fold_lanes(s, jnp.maximum))
            e = [jnp.exp(s[:, c * LANES:(c + 1) * LANES] - m) for c in range(c_hi - c_lo)]
            for c in range(win // LANES):
                blk = e[c - c_lo].astype(BF16) if c_lo <= c < c_hi else jnp.zeros((CHUNK, LANES), BF16)
                e_scr[slot, rows, c * LANES:(c + 1) * LANES] = blk
            r_scr[slot, rows, :] = 1.0 / _row_stat(jnp.sum, functools.reduce(jnp.add, e))
        o2 = _dot(e_scr[slot], vwin(r0)) * r_scr[slot]
        o = jnp.where(lo, o2[:B_STEP_ROWS], o2[B_STEP_ROWS:])
        o_ref[stream, pl.ds(r0, B_STEP_ROWS), :] = o.astype(BF16)

    _run_steps(make_step, q_ref.shape[0], nq // B_STEP_ROWS,
               -(-ctx // B_STEP_ROWS) if pad_front else 0, B_STEP_ROWS)


def _attn_b(q, k, v, ext, ctx, pad_front):
    nb, nq, _ = q.shape
    nk = k.shape[1]
    assert nq % B_STEP_ROWS == 0 and nk == (nq if pad_front else ctx + nq)
    bb = _streams_per_block(nb, pad_front)
    blk = lambda r: pl.BlockSpec((bb, r, LANES), lambda b, p: (b, 0, p))
    rows, win = 2 * B_STEP_ROWS, ctx + B_STEP_ROWS
    scratch = [pltpu.VMEM((B_HEADS // 2, rows, win), F32),
               pltpu.VMEM((PAIR_SLOTS, rows, win), F32), pltpu.VMEM((PAIR_SLOTS, rows, win), BF16),
               pltpu.VMEM((PAIR_SLOTS, rows, LANES), F32)]
    if pad_front:
        scratch += [pltpu.VMEM((ctx + nq, LANES), BF16)] * 2
    return pl.pallas_call(
        functools.partial(_attn_b_kernel, nq=nq, ctx=ctx, pad_front=pad_front),
        grid=(nb // bb, B_HEADS // 2),
        in_specs=[blk(nq), blk(nk), blk(nk),
                  pl.BlockSpec((1,) + ext.shape[1:], lambda b, p: (p, 0, 0))],
        out_specs=blk(nq),
        out_shape=jax.ShapeDtypeStruct((nb, nq, B_QKV), BF16),
        scratch_shapes=scratch,
        compiler_params=_cparams("arbitrary", "arbitrary"),
        name="attn_b",
    )(q, k, v, ext)


def _t5_bucket(rel):
    nb = T5_BUCKETS // 2
    max_exact = nb // 2
    n = -rel
    ret = jnp.where(n < 0, nb, 0)
    n = jnp.abs(n)
    nf = jnp.maximum(n, 1).astype(F32)
    large = max_exact + (jnp.log(nf / max_exact) / math.log(T5_MAX_DIST / max_exact)
                         * (nb - max_exact)).astype(jnp.int32)
    large = jnp.minimum(large, nb - 1)
    return ret + jnp.where(n < max_exact, n, large)


def _band_values(ctx, n_rows, value_of_rel):
    n_cols = ctx + n_rows
    period = -(-(n_cols + n_rows - 1) // LANES) * LANES
    d = jnp.concatenate([jnp.arange(0, period - (n_rows - 1)), jnp.arange(-(n_rows - 1), 0)])
    return value_of_rel(d - ctx)


def _bias_a(t5_table):
    return _band_values(WINDOW, PAIR_ROWS, lambda rel: t5_table.astype(F32)[_t5_bucket(rel)].T)


def _bias_b(b_rel):
    ext = _band_values(B_REACH, B_STEP_ROWS,
                       lambda rel: b_rel.astype(F32)[:, jnp.clip(rel, -B_MAX_REL, B_MAX_REL) + B_MAX_REL])
    return ext.reshape(B_HEADS // 2, 2, ext.shape[-1])


ODD_IN_PAD = 2048
ROPE_HALF = D_ROPE // 2
X1_LO = LANES // 2
NOPE_SPLIT = X1_LO - ROPE_HALF


def _head_lane_source():
    zero = D_NOPE + D_ROPE
    src = []
    for lane in range(LANES):
        if lane < ROPE_HALF:
            src.append(D_NOPE + ROPE_HALF + lane)
        elif lane < X1_LO:
            src.append(lane - ROPE_HALF)
        elif lane < X1_LO + ROPE_HALF:
            src.append(D_NOPE + lane - X1_LO)
        elif lane < X1_LO + ROPE_HALF + D_NOPE - NOPE_SPLIT:
            src.append(NOPE_SPLIT + lane - X1_LO - ROPE_HALF)
        else:
            src.append(zero)
    return src


def _to_head_lanes(a):
    padded = jnp.concatenate([a, jnp.zeros(a.shape[:-1] + (1,), a.dtype)], axis=-1)
    return padded[..., jnp.asarray(_head_lane_source(), dtype=jnp.int32)]


def _lane_masks(shape):
    lane = lax.broadcasted_iota(jnp.int32, shape, 1)
    rope = (lane < ROPE_HALF) | ((lane >= X1_LO) & (lane < X1_LO + ROPE_HALF))
    used = lane < X1_LO + ROPE_HALF + D_NOPE - NOPE_SPLIT
    return used & ~rope, rope


def _rope(y, cosf, sinf):
    return y * cosf + pltpu.roll(y, LANES // 2, 1) * sinf


ROW_BLOCK = 128


def _row_blocks(tm):
    rb = min(ROW_BLOCK, tm)
    return [slice(r, r + rb) for r in range(0, tm, rb)]


def _segment_mean_squares(src_scr, cols, seg_ref, sq_scr, ms_scr, slot):
    for rows in _row_blocks(src_scr.shape[0]):
        v = src_scr[rows, cols]
        sq_scr[slot, rows, :] = (v * v).astype(BF16)
    ms_scr[slot] = _dot(sq_scr[slot], seg_ref[...])


def _expand_kv(ckvb_ref, kpe_blk_ref, wkv_ref, kgain_ref, kseg_ref, kv_scr, sq_scr, ms_scr, k_ref, v_ref,
               between=None):
    k_cols = D_HEADS * LANES
    tm = kv_scr.shape[0]
    for grp in range(kv_scr.shape[1] // MXU_COLS):
        if between and grp in between:
            between[grp]()
        cols = slice(grp * MXU_COLS, (grp + 1) * MXU_COLS)
        kv_scr[:, cols] = _dot(ckvb_ref[...], wkv_ref[:, cols])
        if cols.start < k_cols:
            slot = grp % 2
            _segment_mean_squares(kv_scr, cols, kseg_ref, sq_scr, ms_scr, slot)
        for rows in _row_blocks(tm):
            if cols.start < k_cols:
                for half in range(2):
                    h = 2 * grp + half
                    kb = kv_scr[rows, h * LANES:(h + 1) * LANES]
                    ms = ms_scr[slot, rows, half * LANES:(half + 1) * LANES]
                    kn = (kb * lax.rsqrt(ms + EPS)) * kgain_ref[...]
                    k_ref[rows, h * LANES:(h + 1) * LANES] = (kn + kpe_blk_ref[rows, :]).astype(BF16)
            else:
                v_ref[rows, cols.start - k_cols:cols.stop - k_cols] = kv_scr[rows, cols].astype(BF16)


def _odd_inproj_kernel(x_ref, g_ref, w_ref, cinit_ref, convw_ref, qan_ref, wqb_ref, qgain_ref,
                       kvan_ref, krgain_ref, invf_ref, sgn_ref, wkv_ref, kgain_ref, qseg_ref, kseg_ref,
                       yc_ref, q_ref, k_ref, v_ref, ckv_ref, kpe_ref, cs_ref,
                       h_scr, z_scr, uext_scr, qn_scr, q_scr, kv_scr, rot_scr, ckvb_scr, kpe_scr, sq_scr, ms_scr,
                       *, tm, tpb, pos0):
    i = pl.program_id(0)
    tile = i % tpb
    off = pl.multiple_of(tile * tm, tm)
    blocks = _row_blocks(tm)
    rb = blocks[0].stop

    @pl.when(tile == 0)
    def _():
        uext_scr[0:8, :] = cinit_ref[0]

    @pl.when(i < tpb)
    def _():
        row = lax.broadcasted_iota(jnp.int32, (tm, LANES), 0)
        ang = (row + (tile * tm + pos0)).astype(F32) * invf_ref[...]
        rot_scr[0, pl.ds(off, tm), :] = jnp.cos(ang)
        rot_scr[1, pl.ds(off, tm), :] = jnp.sin(ang) * sgn_ref[...]

    h_scr[...] = _rms(x_ref[...], g_ref[...]).astype(BF16)
    low_lanes = lax.broadcasted_iota(jnp.int32, (rb, LANES), 1) < ROPE_HALF

    def proj(c0, c1):
        z_scr[:, c0:c1] = _dot(h_scr[...], w_ref[:, c0:c1])

    def rotary(y, rows):
        cosf = rot_scr[0, pl.ds(off + rows.start, rb), :]
        sinf = rot_scr[1, pl.ds(off + rows.start, rb), :]
        return _rope(y, cosf, sinf)

    def conv_group(grp):
        c0 = grp * MXU_COLS
        cols = slice(c0, c0 + MXU_COLS)
        for base in (0, C_WIDTH, 2 * C_WIDTH):
            proj(base + c0, base + c0 + MXU_COLS)
        for rows in blocks:
            up = slice(rows.start + 8, rows.stop + 8)
            uext_scr[up, cols] = (z_scr[rows, C_WIDTH + c0:C_WIDTH + c0 + MXU_COLS]
                                  * z_scr[rows, 2 * C_WIDTH + c0:2 * C_WIDTH + c0 + MXU_COLS])
        for rows in blocks:
            yconv = convw_ref[0:1, cols] * uext_scr[rows.start + 6:rows.stop + 6, cols]
            yconv = yconv + convw_ref[1:2, cols] * uext_scr[rows.start + 7:rows.stop + 7, cols]
            yconv = yconv + convw_ref[2:3, cols] * uext_scr[rows.start + 8:rows.stop + 8, cols]
            yc_ref[rows, cols] = (z_scr[rows, cols] * yconv).astype(BF16)

    proj(1536, 1792)
    proj(1792, 2048)

    for rows in blocks:
        qn_scr[rows, :] = _rms(z_scr[rows, 1536:1792], qan_ref[...]).astype(BF16)
        ckv = _rms(z_scr[rows, 1792:1920], kvan_ref[...])
        ckv_ref[rows, :] = ckv
        ckvb_scr[rows, :] = ckv.astype(BF16)

    for grp in range(D_HEADS * LANES // MXU_COLS):
        cols = slice(grp * MXU_COLS, (grp + 1) * MXU_COLS)
        q_scr[:, cols] = _dot(qn_scr[...], wqb_ref[:, cols])
        slot = grp % 2
        _segment_mean_squares(q_scr, cols, qseg_ref, sq_scr, ms_scr, slot)
        for half in range(2):
            h = 2 * grp + half
            for rows in blocks:
                blk = q_scr[rows, h * LANES:(h + 1) * LANES]
                inv = lax.rsqrt(ms_scr[slot, rows, half * LANES:(half + 1) * LANES] + EPS)
                y = rotary((blk * inv) * qgain_ref[...], rows)
                q_ref[rows, h * LANES:(h + 1) * LANES] = y.astype(BF16)

    for rows in blocks:
        kb = z_scr[rows, 1920:2048]
        ms = jnp.sum(kb * kb, axis=-1, keepdims=True) * (1.0 / D_ROPE)
        kpe_blk = rotary((kb * lax.rsqrt(ms + EPS)) * krgain_ref[...], rows)
        kpe_scr[rows, :] = kpe_blk
        x1_then_x2 = jnp.where(low_lanes, pltpu.roll(kpe_blk, LANES - X1_LO, 1), pltpu.roll(kpe_blk, ROPE_HALF, 1))
        kpe_ref[rows, :] = x1_then_x2[:, 0:D_ROPE]
    _expand_kv(ckvb_scr, kpe_scr, wkv_ref, kgain_ref, kseg_ref, kv_scr, sq_scr, ms_scr, k_ref, v_ref,
               between={0: lambda: conv_group(0), 4: lambda: conv_group(1)})

    cs_ref[0] = uext_scr[tm + 6:tm + 8, :]
    uext_scr[0:8, :] = uext_scr[tm:tm + 8, :]


def _odd_inproj(x, norm_g, ow, conv_init, n_batch, pos0):
    t = x.shape[0]
    tm = min(TOKEN_TILE, t // n_batch)
    tpb = t // n_batch // tm
    assert tpb * tm * n_batch == t
    row = lambda w: pl.BlockSpec((tm, w), lambda i: (i, 0))
    per_b = lambda r, w: pl.BlockSpec((1, r, w), lambda i: (i // tpb, 0, 0))
    small = [ow["conv_w"], ow["qan"], ow["wqb"], ow["qgain"], ow["kvan"], ow["krgain"],
             ow["invf"], ow["sgn"], ow["wkv"], ow["kgain"], ow["qseg"], ow["kseg"]]
    kv_w = D_HEADS * LANES
    v_w = D_HEADS * D_V
    return pl.pallas_call(
        functools.partial(_odd_inproj_kernel, tm=tm, tpb=tpb, pos0=pos0),
        grid=(t // tm,),
        in_specs=[row(D_MODEL), _resident((1, D_MODEL)), _resident(ow["w_in"].shape), per_b(8, C_WIDTH)]
                 + [_resident(a.shape) for a in small],
        out_specs=[row(C_WIDTH), row(kv_w), row(kv_w), row(v_w), row(D_KV_LORA), row(D_ROPE),
                   per_b(CONV_W - 1, C_WIDTH)],
        out_shape=[jax.ShapeDtypeStruct((t, C_WIDTH), BF16), jax.ShapeDtypeStruct((t, kv_w), BF16),
                   jax.ShapeDtypeStruct((t, kv_w), BF16), jax.ShapeDtypeStruct((t, v_w), BF16),
                   jax.ShapeDtypeStruct((t, D_KV_LORA), F32), jax.ShapeDtypeStruct((t, D_ROPE), F32),
                   jax.ShapeDtypeStruct((n_batch, CONV_W - 1, C_WIDTH), F32)],
        scratch_shapes=[pltpu.VMEM((tm, D_MODEL), BF16), pltpu.VMEM((tm, ODD_IN_PAD), F32),
                        pltpu.VMEM((tm + 8, C_WIDTH), F32), pltpu.VMEM((tm, D_Q_LORA), BF16),
                        pltpu.VMEM((tm, kv_w), F32), pltpu.VMEM((tm, kv_w + v_w), F32),
                        pltpu.VMEM((2, tpb * tm, LANES), F32), pltpu.VMEM((tm, D_KV_LORA), BF16),
                        pltpu.VMEM((tm, LANES), F32), pltpu.VMEM((2, tm, MXU_COLS), BF16),
                        pltpu.VMEM((2, tm, MXU_COLS), F32)],
        compiler_params=_cparams("arbitrary"),
        name="odd_inproj",
    )(x, norm_g.reshape(1, D_MODEL), ow["w_in"], conv_init, *small)


def _kv_expand_kernel(ckv_ref, kpe_ref, place_ref, wkv_ref, kgain_ref, kseg_ref, k_ref, v_ref,
                      kv_scr, ckvb_scr, kpe_scr, sq_scr, ms_scr):
    ckvb_scr[...] = ckv_ref[...].astype(BF16)
    kpe_scr[...] = _dot(kpe_ref[...].astype(BF16), place_ref[...])
    _expand_kv(ckvb_scr, kpe_scr, wkv_ref, kgain_ref, kseg_ref, kv_scr, sq_scr, ms_scr, k_ref, v_ref)


def _kv_expand(ckv, kpe, ow):
    t = ckv.shape[0]
    tm = min(TOKEN_TILE, t)
    row = lambda w: pl.BlockSpec((tm, w), lambda i: (i, 0))
    kv_w = D_HEADS * LANES
    v_w = D_HEADS * D_V
    return pl.pallas_call(
        _kv_expand_kernel,
        grid=(t // tm,),
        in_specs=[row(D_KV_LORA), row(D_ROPE), _resident(ow["place"].shape), _resident(ow["wkv"].shape),
                  _resident(ow["kgain"].shape), _resident(ow["kseg"].shape)],
        out_specs=[row(kv_w), row(v_w)],
        out_shape=[jax.ShapeDtypeStruct((t, kv_w), BF16), jax.ShapeDtypeStruct((t, v_w), BF16)],
        scratch_shapes=[pltpu.VMEM((tm, kv_w + v_w), F32), pltpu.VMEM((tm, D_KV_LORA), BF16),
                        pltpu.VMEM((tm, LANES), F32), pltpu.VMEM((2, tm, MXU_COLS), BF16),
                        pltpu.VMEM((2, tm, MXU_COLS), F32)],
        compiler_params=_cparams("arbitrary"),
        name="kv_expand",
    )(ckv, kpe, ow["place"], ow["wkv"], ow["kgain"], ow["kseg"])


def _odd_weights(w_in, conv_w, q_a_norm, w_q_b, kv_a_norm, w_kv_b, qn_nope, qn_rope, kn_nope, kn_rope):
    z = lambda n: jnp.zeros((n,), F32)
    qk = D_NOPE + D_ROPE
    w_in = _to_bf16(w_in)
    kr_blk = _to_head_lanes(jnp.concatenate([jnp.zeros((D_MODEL, D_NOPE), BF16), w_in[:, 1920:]], axis=1))
    w_pad = jnp.concatenate([w_in[:, :1920], kr_blk], axis=1)
    wqb = _to_head_lanes(w_q_b.astype(BF16).reshape(D_Q_LORA, D_HEADS, qk)).reshape(D_Q_LORA, D_HEADS * LANES)
    kvb = w_kv_b.astype(BF16).reshape(D_KV_LORA, D_HEADS, D_NOPE + D_V)
    wk = _to_head_lanes(jnp.concatenate([kvb[:, :, :D_NOPE], jnp.zeros((D_KV_LORA, D_HEADS, D_ROPE), BF16)], axis=-1))
    wkv = jnp.concatenate([wk.reshape(D_KV_LORA, D_HEADS * LANES),
                           kvb[:, :, D_NOPE:].reshape(D_KV_LORA, D_HEADS * D_V)], axis=1)
    inv = 1.0 / (ROPE_THETA ** (jnp.arange(ROPE_HALF, dtype=F32) / ROPE_HALF))
    ones = jnp.ones((ROPE_HALF,), F32)
    lanes = lambda v: _to_head_lanes(v).reshape(1, LANES)
    in_nope = lanes(jnp.concatenate([jnp.ones((D_NOPE,), F32), z(D_ROPE)]))
    in_rope = lanes(jnp.concatenate([z(D_NOPE), jnp.ones((D_ROPE,), F32)]))
    seg_nope = in_nope.T * in_nope * (1.0 / D_NOPE)
    seg_rope = in_rope.T * in_rope * (1.0 / D_ROPE)
    two_heads = lambda m: jnp.kron(jnp.eye(MXU_COLS // LANES, dtype=F32), m).astype(BF16)
    return {
        "qseg": two_heads(seg_nope + seg_rope),
        "kseg": two_heads(seg_nope),
        "w_in": w_pad,
        "conv_w": conv_w.astype(F32),
        "qan": q_a_norm.reshape(1, D_Q_LORA),
        "wqb": wqb,
        "qgain": lanes(jnp.concatenate([qn_nope, qn_rope])) * MLA_QSCALE,
        "kvan": kv_a_norm.reshape(1, D_KV_LORA),
        "krgain": lanes(jnp.concatenate([z(D_NOPE), kn_rope])),
        "invf": lanes(jnp.concatenate([z(D_NOPE), inv, inv])),
        "sgn": lanes(jnp.concatenate([z(D_NOPE), -ones, ones])),
        "wkv": wkv,
        "kgain": lanes(jnp.concatenate([kn_nope, z(D_ROPE)])),
        "place": _to_head_lanes(jnp.concatenate([jnp.zeros((D_ROPE, D_NOPE), BF16), jnp.eye(D_ROPE, dtype=BF16)], axis=1)),
    }


def _mla_prompt_kernel(q_ref, k_ref, v_ref, o_ref, s_scr, e_scr, r_scr, *, seq):
    qb = min(MLA_QBLOCK, seq)
    hb = qb // 2
    lo = lax.broadcasted_iota(jnp.int32, (qb, LANES), 1) < D_V
    first_half = lax.broadcasted_iota(jnp.int32, (CHUNK, LANES), 1) < CHUNK
    scores = lambda q, k: lax.dot_general(q, k, _NT, preferred_element_type=F32)
    for i in range(seq // qb):
        tk = qb * (i + 1)
        t0, t1 = tk - qb, tk - hb
        top, bot = slice(0, hb), slice(hb, qb)
        outs = []
        for h in range(2):
            hl = slice(h * LANES, (h + 1) * LANES)
            q0 = i * qb
            if t0 > 0:
                s_scr[h, :, 0:t0] = scores(q_ref[0, q0:q0 + qb, hl], k_ref[0, 0:t0, hl])
            s_scr[h, top, t0:t1] = scores(q_ref[0, q0:q0 + hb, hl], k_ref[0, t0:t1, hl])
            s_scr[h, bot, t0:tk] = scores(q_ref[0, q0 + hb:q0 + qb, hl], k_ref[0, t0:tk, hl])
            for rb in range(qb // CHUNK):
                rows = slice(rb * CHUNK, (rb + 1) * CHUNK)
                visible = t0 + CHUNK * (rb + 1)
                width = t1 if rb < hb // CHUNK else tk
                n_full, ragged = visible // LANES, visible % LANES != 0
                blk = lambda c: s_scr[h, rows, c * LANES:(c + 1) * LANES]
                cols = [blk(c) for c in range(n_full)]
                if ragged:
                    cols.append(jnp.where(first_half, blk(n_full), NEG_INF))
                mm = cols[0]
                for c in cols[1:]:
                    mm = jnp.maximum(mm, c)
                m = _row_stat(jnp.max, mm)
                acc = None
                for c in range(width // LANES):
                    if c < len(cols):
                        e = jnp.exp2(blk(c) - m)
                        if c >= n_full:
                            e = jnp.where(first_half, e, 0.0)
                        acc = e if acc is None else acc + e
                    else:
                        e = jnp.zeros((CHUNK, LANES), F32)
                    e_scr[h, rows, c * LANES:(c + 1) * LANES] = e.astype(BF16)
                r_scr[h, rows, :] = 1.0 / _row_stat(jnp.sum, acc)
            pv = jnp.concatenate([_dot(e_scr[h, top, t0:t1], v_ref[0, t0:t1, :]),
                                  _dot(e_scr[h, bot, t0:tk], v_ref[0, t0:tk, :])], axis=0)
            if t0 > 0:
                pv = pv + _dot(e_scr[h, :, 0:t0], v_ref[0, 0:t0, :])
            outs.append(pv * r_scr[h])
        o_ref[0, i * qb:(i + 1) * qb, :] = jnp.where(lo, outs[0], outs[1]).astype(BF16)


def _mla_prompt(q, k, v):
    nb, seq, _ = q.shape
    qb = min(MLA_QBLOCK, seq)
    return pl.pallas_call(
        functools.partial(_mla_prompt_kernel, seq=seq),
        grid=(nb, D_HEADS // 2),
        in_specs=[pl.BlockSpec((1, seq, 2 * LANES), lambda b, p: (b, 0, p)),
                  pl.BlockSpec((1, seq, 2 * LANES), lambda b, p: (b, 0, p)),
                  pl.BlockSpec((1, seq, LANES), lambda b, p: (b, 0, p))],
        out_specs=pl.BlockSpec((1, seq, LANES), lambda b, p: (b, 0, p)),
        out_shape=jax.ShapeDtypeStruct((nb, seq, D_HEADS * D_V), BF16),
        scratch_shapes=[pltpu.VMEM((2, qb, seq), F32), pltpu.VMEM((2, qb, seq), BF16),
                        pltpu.VMEM((2, qb, LANES), F32)],
        compiler_params=_cparams("arbitrary", "arbitrary"),
        name="mla_prompt",
    )(q, k, v)


def _mla_sample_kernel(q_ref, kc_ref, vc_ref, kn_ref, vn_ref, o_ref):
    nq = q_ref.shape[1]
    lo = lax.broadcasted_iota(jnp.int32, (nq, LANES), 1) < D_V
    outs = []
    for h in range(2):
        cols = slice(h * LANES, (h + 1) * LANES)
        q = q_ref[0, :, cols]
        s1 = lax.dot_general(q, kc_ref[0, :, cols], _NT, preferred_element_type=F32)
        s2 = lax.dot_general(q, kn_ref[0, :, cols], _NT, preferred_element_type=F32)
        m = jnp.maximum(jnp.max(s1, axis=-1, keepdims=True), jnp.max(s2, axis=-1, keepdims=True))
        e1 = jnp.exp2(s1 - m)
        e2 = jnp.exp2(s2 - m)
        den = jnp.sum(e1, axis=-1, keepdims=True) + jnp.sum(e2, axis=-1, keepdims=True)
        outs.append((_dot(e1.astype(BF16), vc_ref[0]) + _dot(e2.astype(BF16), vn_ref[0])) / den)
    o_ref[0] = jnp.where(lo, outs[0], outs[1]).astype(BF16)


def _mla_sample(q, kc, vc, kn, vn):
    nb, nq, _ = q.shape
    past = kc.shape[1]
    two = lambda r: pl.BlockSpec((1, r, 2 * LANES), lambda b, p: (b, 0, p))
    one = lambda r: pl.BlockSpec((1, r, LANES), lambda b, p: (b, 0, p))
    return pl.pallas_call(
        _mla_sample_kernel,
        grid=(nb, D_HEADS // 2),
        in_specs=[two(nq), two(past), one(past), two(nq), one(nq)],
        out_specs=one(nq),
        out_shape=jax.ShapeDtypeStruct((nb, nq, D_HEADS * D_V), BF16),
        compiler_params=_cparams("arbitrary", "arbitrary"),
        name="mla_sample",
    )(q, kc, vc, kn, vn)


def _even_layer(xp, xs, nb, seq, ndb, dseq, norm_g, w_in, w_out, a_qn, a_kn, a_sinks, b_qn, b_kn, b_rel,
                t5_table, ck_a, cv_a, ck_b, cv_b, ffn2):
    w_in = _to_bf16(w_in)
    w_out = _to_bf16(w_out)
    w_in_p = jnp.concatenate([_pair_heads(w_in[:, :A_Q], 1), w_in[:, A_Q:]], axis=1)
    woa = _pair_heads(w_out[:A_Q], 0)
    wob = w_out[A_Q:]
    ones = lambda n: jnp.ones((n,), F32)
    scale = HEAD_DIM ** -0.5
    gain_row = jnp.concatenate([jnp.tile(a_qn, A_HEADS) * scale, jnp.tile(a_kn, A_KV_HEADS), ones(A_KV),
                                jnp.tile(b_qn, B_HEADS) * scale, jnp.tile(b_kn, B_HEADS), ones(B_QKV)]
                               ).reshape(1, EVEN_IN).astype(F32)
    bias_a = _bias_a(t5_table)
    bias_b = _bias_b(b_rel)
    sink = jnp.broadcast_to(jnp.repeat(a_sinks.astype(F32), PAIR_ROWS)[:, None], (A_HEADS * PAIR_ROWS, LANES))
    la, lb = min(WINDOW, seq), min(B_REACH, seq)

    aq, ak, av, bq, bk, bv, cak, cav, cbk, cbv = _even_inproj(xp, norm_g, w_in_p, gain_row, nb, la, lb)
    r3 = lambda a: a.reshape(nb, seq, a.shape[-1])
    ya = _attn_a(r3(aq), r3(ak), r3(av), bias_a, sink, WINDOW, True)
    yb = _attn_b(r3(bq), r3(bk), r3(bv), bias_b, B_REACH, True)
    xp = _ffn(xp, *ffn2, out_proj=(ya.reshape(nb * seq, A_Q), yb.reshape(nb * seq, B_QKV), woa, wob))
    st_p = (cak.reshape(nb, la, A_KV_HEADS, HEAD_DIM), cav.reshape(nb, la, A_KV_HEADS, HEAD_DIM),
            cbk.reshape(nb, lb, B_HEADS, HEAD_DIM), cbv.reshape(nb, lb, B_HEADS, HEAD_DIM))

    ts = ndb * dseq
    aq, ak, av, bq, bk, bv, nak, nav, nbk, nbv = _even_inproj(xs, norm_g, w_in_p, gain_row, 1, ts, ts)
    pad_q = lambda a, rows: jnp.pad(a.reshape(ndb, dseq, a.shape[-1]), ((0, 0), (0, rows - dseq), (0, 0)))

    def window(cache, new, rows):
        w = cache.shape[-2] * cache.shape[-1]
        full = jnp.concatenate([cache.reshape(ndb, -1, w), new.reshape(ndb, dseq, w)], axis=1)
        buf = jnp.pad(full, ((0, 0), (0, rows - dseq), (0, 0))).astype(BF16)
        return full[:, dseq:].reshape(cache.shape), buf

    st_ak, kbuf_a = window(ck_a, nak, PAIR_ROWS)
    st_av, vbuf_a = window(cv_a, nav, PAIR_ROWS)
    st_bk, kbuf_b = window(ck_b, nbk, B_STEP_ROWS)
    st_bv, vbuf_b = window(cv_b, nbv, B_STEP_ROWS)
    ya = _attn_a(pad_q(aq, PAIR_ROWS), kbuf_a, vbuf_a, bias_a, sink, WINDOW, False)[:, :dseq]
    yb = _attn_b(pad_q(bq, B_STEP_ROWS), kbuf_b, vbuf_b, bias_b, B_REACH, False)[:, :dseq]
    xs = _ffn(xs, *ffn2, out_proj=(ya.reshape(ts, A_Q), yb.reshape(ts, B_QKV), woa, wob))
    return xp, xs, st_p, (st_ak, st_av, st_bk, st_bv)


def _odd_layer(xp, xs, nb, seq, ndb, dseq, past, norm_g, ow, w_out, conv_prev, c_ckv, c_kpe, ffn2):
    w_out = _to_bf16(w_out)
    woc = w_out[:C_WIDTH]
    wod = w_out[C_WIDTH:]

    zero_init = jnp.zeros((nb, 8, C_WIDTH), F32)
    yc, q, k, v, ckv, kpe, cs = _odd_inproj(xp, norm_g, ow, zero_init, nb, 0)
    r3 = lambda a: a.reshape(nb, seq, a.shape[-1])
    yd = _mla_prompt(r3(q), r3(k), r3(v))
    xp = _ffn(xp, *ffn2, out_proj=(yc, yd.reshape(nb * seq, D_HEADS * D_V), woc, wod))
    st_p = (cs, ckv.reshape(nb, seq, D_KV_LORA), kpe.reshape(nb, seq, D_ROPE))

    ts = ndb * dseq
    init = jnp.pad(conv_prev.astype(F32), ((0, 0), (8 - (CONV_W - 1), 0), (0, 0)))
    yc, q, kn, vn, ckv, kpe, cs = _odd_inproj(xs, norm_g, ow, init, ndb, past)
    kc, vc = _kv_expand(c_ckv.reshape(ndb * past, D_KV_LORA), c_kpe.reshape(ndb * past, D_ROPE), ow)
    s3 = lambda a: a.reshape(ndb, dseq, a.shape[-1])
    yd = _mla_sample(s3(q), kc.reshape(ndb, past, -1), vc.reshape(ndb, past, -1), s3(kn), s3(vn))
    xs = _ffn(xs, *ffn2, out_proj=(yc, yd.reshape(ts, D_HEADS * D_V), woc, wod))
    st_s = (cs, ckv.reshape(ndb, dseq, D_KV_LORA), kpe.reshape(ndb, dseq, D_ROPE))
    return xp, xs, st_p, st_s


def kernel(x_prompt, x_sample, cache_a_k, cache_a_v, cache_b_k, cache_b_v, state_c_conv, cache_d_ckv, cache_d_kpe, ff1_norm, ff1_w_gu, ff1_w_down, mix_norm, ff2_norm, ff2_w_gu, ff2_w_down, t5_bias_table, ev_w_in, ev_w_out, a_q_norm, a_k_norm, a_sinks, b_q_norm, b_k_norm, b_rel_bias, od_w_in, od_w_out, c_conv_w, d_q_a_norm, d_w_q_b, d_kv_a_norm, d_w_kv_b, d_q_nope_norm, d_q_rope_norm, d_k_nope_norm, d_k_rope_norm):
    nb, seq, _ = x_prompt.shape
    ndb, dseq, _ = x_sample.shape
    past = cache_d_ckv.shape[2]
    depth = ff1_norm.shape[0]
    assert seq % TOKEN_TILE == 0 and dseq == CHUNK and past % CHUNK == 0
    xp = x_prompt.reshape(nb * seq, D_MODEL)
    xs = x_sample.reshape(ndb * dseq, D_MODEL)
    even_p, even_s, odd_p, odd_s = [], [], [], []
    for l in range(depth):
        i = l // 2
        ff1 = _ffn_weights(ff1_w_gu, ff1_w_down, l)
        ffn2 = (ff2_norm[l],) + _ffn_weights(ff2_w_gu, ff2_w_down, l)
        xp = _ffn(xp, ff1_norm[l], *ff1)
        xs = _ffn(xs, ff1_norm[l], *ff1)
        if l % 2 == 0:
            xp, xs, sp, ss = _even_layer(
                xp, xs, nb, seq, ndb, dseq, mix_norm[l], ev_w_in[i], ev_w_out[i], a_q_norm[i], a_k_norm[i],
                a_sinks[i], b_q_norm[i], b_k_norm[i], b_rel_bias[i], t5_bias_table,
                cache_a_k[i], cache_a_v[i], cache_b_k[i], cache_b_v[i], ffn2)
            even_p.append(sp)
            even_s.append(ss)
        else:
            ow = _odd_weights(od_w_in[i], c_conv_w[i], d_q_a_norm[i], d_w_q_b[i], d_kv_a_norm[i], d_w_kv_b[i],
                              d_q_nope_norm[i], d_q_rope_norm[i], d_k_nope_norm[i], d_k_rope_norm[i])
            xp, xs, sp, ss = _odd_layer(xp, xs, nb, seq, ndb, dseq, past, mix_norm[l], ow, od_w_out[i],
                                        state_c_conv[i], cache_d_ckv[i], cache_d_kpe[i], ffn2)
            odd_p.append(sp)
            odd_s.append(ss)
    stack = lambda group, j: jnp.stack([g[j] for g in group])
    return (xp.reshape(nb, seq, D_MODEL), xs.reshape(ndb, dseq, D_MODEL),
            stack(even_p, 0), stack(even_p, 1), stack(even_p, 2), stack(even_p, 3),
            stack(odd_p, 0), stack(odd_p, 1), stack(odd_p, 2),
            stack(even_s, 0), stack(even_s, 1), stack(even_s, 2), stack(even_s, 3),
            stack(odd_s, 0), stack(odd_s, 1), stack(odd_s, 2))
```

```python
import functools
import math

import jax
import jax.numpy as jnp
from jax import lax
from jax.experimental import pallas as pl
from jax.experimental.pallas import tpu as pltpu

F32 = jnp.float32
BF16 = jnp.bfloat16

D_MODEL = 1024
CHUNK = 64
HEAD_DIM = 64
EPS = 1e-6
A_HEADS = 8
A_KV_HEADS = 2
WINDOW = 128
T5_BUCKETS = 32
T5_MAX_DIST = 128
B_HEADS = 8
B_REACH = 512
B_MAX_REL = 128
C_WIDTH = 512
CONV_W = 3
D_HEADS = 8
D_Q_LORA = 256
D_KV_LORA = 128
D_NOPE = 64
D_ROPE = 32
D_V = 64
ROPE_THETA = 10000.0
FFN_DIM = 2816
A_Q = A_HEADS * HEAD_DIM
A_KV = A_KV_HEADS * HEAD_DIM
B_QKV = B_HEADS * HEAD_DIM
EVEN_IN = A_Q + 2 * A_KV + 3 * B_QKV

LANES = 128
TOKEN_TILE = 512
CAST_ROWS = 128
MXU_COLS = 256
FFN_CHUNK = MXU_COLS
PAIR_ROWS = 2 * CHUNK
MLA_QBLOCK = 512
VMEM_LIMIT_BYTES = 56 * 1024 * 1024
LOG2E = math.log2(math.e)
MLA_QSCALE = (D_NOPE + D_ROPE) ** -0.5 * LOG2E
NEG_INF = float("-inf")

_NT = (((1,), (1,)), ((), ()))


def _cparams(*sem):
    return pltpu.CompilerParams(dimension_semantics=sem, vmem_limit_bytes=VMEM_LIMIT_BYTES)


def _resident(shape):
    zeros = (0,) * len(shape)
    return pl.BlockSpec(shape, lambda *_: zeros, pipeline_mode=pl.Buffered(1))


def _rms(x, g):
    ms = jnp.mean(x * x, axis=-1, keepdims=True)
    return (x * lax.rsqrt(ms + EPS)) * g


def _dot(a, b):
    return jnp.dot(a, b, preferred_element_type=F32)


def _ffn_kernel(*refs, n_chunks, fused_out):
    if fused_out:
        (x_ref, ya_ref, yb_ref, woa_ref, wob_ref, g_ref, wgu_ref, wd_ref,
         o_ref, h_scr, a_scr) = refs
        y = _dot(ya_ref[...], woa_ref[...]) + _dot(yb_ref[...], wob_ref[...])
        o_ref[...] = x_ref[...] + y
        res_ref = o_ref
    else:
        x_ref, g_ref, wgu_ref, wd_ref, o_ref, h_scr, a_scr = refs
        res_ref = x_ref
    h_scr[...] = _rms(res_ref[...], g_ref[...]).astype(BF16)
    for j in range(n_chunks):
        h = h_scr[...]
        g = _dot(h, wgu_ref[:, j * FFN_CHUNK:(j + 1) * FFN_CHUNK])
        u = _dot(h, wgu_ref[:, FFN_DIM + j * FFN_CHUNK:FFN_DIM + (j + 1) * FFN_CHUNK])
        a_scr[:, j * FFN_CHUNK:(j + 1) * FFN_CHUNK] = ((g * jax.nn.sigmoid(g)) * u).astype(BF16)
    o_ref[...] = res_ref[...] + 0.5 * _dot(a_scr[...], wd_ref[...])


def _ffn(x, norm_g, wgu, wd, out_proj=None):
    t = x.shape[0]
    tm = min(TOKEN_TILE, t)
    n_chunks = FFN_DIM // FFN_CHUNK
    row = lambda w: pl.BlockSpec((tm, w), lambda i: (i, 0))
    in_specs = [row(D_MODEL)]
    args = [x]
    if out_proj is not None:
        ya, yb, woa, wob = out_proj
        in_specs += [row(ya.shape[1]), row(yb.shape[1]), _resident(woa.shape), _resident(wob.shape)]
        args += [ya, yb, woa, wob]
    in_specs += [_resident((1, D_MODEL)), _resident(wgu.shape), _resident(wd.shape)]
    args += [norm_g.reshape(1, D_MODEL), wgu, wd]
    return pl.pallas_call(
        functools.partial(_ffn_kernel, n_chunks=n_chunks, fused_out=out_proj is not None),
        grid=(t // tm,),
        in_specs=in_specs,
        out_specs=row(D_MODEL),
        out_shape=jax.ShapeDtypeStruct((t, D_MODEL), F32),
        scratch_shapes=[pltpu.VMEM((tm, D_MODEL), BF16), pltpu.VMEM((tm, FFN_DIM), BF16)],
        compiler_params=_cparams("arbitrary"),
        name="ffn_out" if out_proj is not None else "ffn",
    )(*args)


def _cast_kernel(x_ref, o_ref):
    o_ref[...] = x_ref[...].astype(BF16)


def _to_bf16(w, layer=None):
    r, c = w.shape[-2:]
    tr = CAST_ROWS if r % CAST_ROWS == 0 else r
    out_spec = pl.BlockSpec((tr, c), lambda i: (i, 0))
    in_spec = out_spec if layer is None else pl.BlockSpec((None, tr, c), lambda i: (layer, i, 0))
    return pl.pallas_call(
        _cast_kernel, grid=(r // tr,), in_specs=[in_spec], out_specs=out_spec,
        out_shape=jax.ShapeDtypeStruct((r, c), BF16),
        compiler_params=_cparams("arbitrary"), name="cast_bf16",
    )(w)


def _ffn_weights(w_gu, w_down, layer):
    return _to_bf16(w_gu, layer), _to_bf16(w_down, layer)


def _norm_halves(blk, gain):
    sq = blk * blk
    lo = lax.broadcasted_iota(jnp.int32, blk.shape, 1) < HEAD_DIM
    s_lo = jnp.sum(jnp.where(lo, sq, 0.0), axis=-1, keepdims=True)
    s_hi = jnp.sum(jnp.where(lo, 0.0, sq), axis=-1, keepdims=True)
    inv = jnp.where(lo, lax.rsqrt(s_lo * (1.0 / HEAD_DIM) + EPS), lax.rsqrt(s_hi * (1.0 / HEAD_DIM) + EPS))
    return (blk * inv) * gain


def _even_inproj_kernel(x_ref, g_ref, w_ref, gain_ref,
                        aq_ref, ak_ref, av_ref, bq_ref, bk_ref, bv_ref,
                        cak_ref, cav_ref, cbk_ref, cbv_ref, h_scr, z_scr, *, tm, tpb, rows_a, rows_b):
    i = pl.program_id(0)
    last = (i % tpb) == (tpb - 1)
    h_scr[...] = _rms(x_ref[...], g_ref[...]).astype(BF16)

    def lanes(c):
        return slice(c * LANES, (c + 1) * LANES)

    def emit(c):
        blk = z_scr[:, lanes(c)]
        gain = gain_ref[:, lanes(c)]
        if c < 4:
            aq_ref[:, lanes(c)] = _norm_halves(blk, gain).astype(BF16)
        elif c == 4:
            kn = _norm_halves(blk, gain)
            ak_ref[...] = kn.astype(BF16)
            z_scr[:, lanes(c)] = kn
        elif c == 5:
            av_ref[...] = blk.astype(BF16)
        elif c < 10:
            bq_ref[:, lanes(c - 6)] = _norm_halves(blk, gain).astype(BF16)
        elif c < 14:
            kn = _norm_halves(blk, gain)
            bk_ref[:, lanes(c - 10)] = kn.astype(BF16)
            z_scr[:, lanes(c)] = kn
        else:
            bv_ref[:, lanes(c - 14)] = blk.astype(BF16)

    for grp in range(EVEN_IN // MXU_COLS):
        cols = slice(grp * MXU_COLS, (grp + 1) * MXU_COLS)
        z_scr[:, cols] = _dot(h_scr[...], w_ref[:, cols])
        emit(2 * grp)
        emit(2 * grp + 1)

    @pl.when(last)
    def _():
        cak_ref[0] = z_scr[tm - rows_a:, lanes(4)]
        cav_ref[0] = z_scr[tm - rows_a:, lanes(5)]
        cbk_ref[0] = z_scr[tm - rows_b:, 10 * LANES:14 * LANES]
        cbv_ref[0] = z_scr[tm - rows_b:, 14 * LANES:18 * LANES]


def _even_inproj(x, norm_g, w_in, gain_row, n_batch, rows_a, rows_b):
    t = x.shape[0]
    tm = min(TOKEN_TILE, t)
    tpb = t // n_batch // tm
    assert tpb * tm * n_batch == t and rows_a <= tm and rows_b <= tm
    row = lambda w: pl.BlockSpec((tm, w), lambda i: (i, 0))
    cache = lambda r, w: pl.BlockSpec((1, r, w), lambda i: (i // tpb, 0, 0))
    bshape = lambda w: jax.ShapeDtypeStruct((t, w), BF16)
    cshape = lambda r, w: jax.ShapeDtypeStruct((n_batch, r, w), F32)
    return pl.pallas_call(
        functools.partial(_even_inproj_kernel, tm=tm, tpb=tpb, rows_a=rows_a, rows_b=rows_b),
        grid=(t // tm,),
        in_specs=[row(D_MODEL), _resident((1, D_MODEL)), _resident(w_in.shape), _resident(gain_row.shape)],
        out_specs=[row(A_Q), row(A_KV), row(A_KV), row(B_QKV), row(B_QKV), row(B_QKV),
                   cache(rows_a, A_KV), cache(rows_a, A_KV), cache(rows_b, B_QKV), cache(rows_b, B_QKV)],
        out_shape=[bshape(A_Q), bshape(A_KV), bshape(A_KV), bshape(B_QKV), bshape(B_QKV), bshape(B_QKV),
                   cshape(rows_a, A_KV), cshape(rows_a, A_KV), cshape(rows_b, B_QKV), cshape(rows_b, B_QKV)],
        scratch_shapes=[pltpu.VMEM((tm, D_MODEL), BF16), pltpu.VMEM((tm, EVEN_IN), F32)],
        compiler_params=_cparams("arbitrary"),
        name="even_inproj",
    )(x, norm_g.reshape(1, D_MODEL), w_in, gain_row)


def _pair_heads(a, axis):
    shape = a.shape
    split = shape[:axis] + (2, A_HEADS // 2, HEAD_DIM) + shape[axis + 1:]
    return a.reshape(split).swapaxes(axis, axis + 1).reshape(shape)


def _fill_padded(buf, src_ref, ctx):
    buf[0:ctx, :] = jnp.zeros((ctx, buf.shape[1]), buf.dtype)
    buf[ctx:, :] = src_ref[0]


def _windows(k_ref, v_ref, pad_scr, ctx, pad_front, step_rows, stream):
    win = ctx + step_rows
    if pad_front:
        kbuf, vbuf = pad_scr
        _fill_padded(kbuf, k_ref, ctx)
        _fill_padded(vbuf, v_ref, ctx)
        return (lambda r0: kbuf[pl.ds(r0, win), :]), (lambda r0: vbuf[pl.ds(r0, win), :])
    return (lambda r0: k_ref[stream, pl.ds(r0, win), :]), (lambda r0: v_ref[stream, pl.ds(r0, win), :])


def _fold_lanes(x, op):
    acc = x[:, :LANES]
    for c in range(1, x.shape[1] // LANES):
        acc = op(acc, x[:, c * LANES:(c + 1) * LANES])
    return acc


def _row_stat(reduce_fn, x):
    return jnp.broadcast_to(reduce_fn(x, axis=-1, keepdims=True), x.shape)


PAIR_SLOTS = 4


def _run_steps(make_step, n_streams, n_steps, n_front, step_rows):
    for stream in range(n_streams):
        step = make_step(stream)
        for t in range(n_steps):
            step(t * step_rows, (stream * n_steps + t) % PAIR_SLOTS, t < n_front)


def _band_bias(ext_row, n_rows, ctx):
    win = ctx + n_rows
    period = ext_row.shape[1]
    t = pltpu.roll(jnp.broadcast_to(ext_row, (n_rows, period)), 0, 1, stride=1, stride_axis=0)[:, :win]
    chunk = lax.broadcasted_iota(jnp.int32, (n_rows, win), 0) // CHUNK
    col = lax.broadcasted_iota(jnp.int32, (n_rows, win), 1)
    visible = (col >= CHUNK * chunk) & (col < ctx + CHUNK * (chunk + 1))
    return jnp.where(visible, t, NEG_INF)


def _attn_a_kernel(q_ref, k_ref, v_ref, ext_ref, sink_ref, o_ref, bias_ref, s_scr, e_scr, *pad_scr,
                   nq, ctx, pad_front):
    win = ctx + PAIR_ROWS
    lo = lax.broadcasted_iota(jnp.int32, (PAIR_ROWS, LANES), 1) < HEAD_DIM

    @pl.when(pl.program_id(0) == 0)
    def _():
        for head in range(A_HEADS):
            bias_ref[head * PAIR_ROWS:(head + 1) * PAIR_ROWS, :] = _band_bias(
                ext_ref[head:head + 1, :], PAIR_ROWS, ctx)

    def make_step(stream):
        kwin, vwin = _windows(k_ref, v_ref, pad_scr, ctx, pad_front, PAIR_ROWS, stream)
        return functools.partial(pair, stream, kwin, vwin)

    def pair(stream, kwin, vwin, r0, slot, masked):
        q_lo, q_hi = [], []
        for p in range(4):
            qp = q_ref[stream, pl.ds(r0, PAIR_ROWS), p * LANES:(p + 1) * LANES]
            zero = jnp.zeros_like(qp)
            q_lo.append(jnp.where(lo, qp, zero))
            q_hi.append(jnp.where(lo, zero, qp))
        q2 = jnp.concatenate(q_lo + q_hi, axis=0)
        s_scr[slot] = lax.dot_general(q2, kwin(r0), _NT, preferred_element_type=F32)
        for head in range(A_HEADS):
            rows = slice(head * PAIR_ROWS, (head + 1) * PAIR_ROWS)
            s = s_scr[slot, rows, :] + bias_ref[rows, :]
            if masked:
                col = lax.broadcasted_iota(jnp.int32, (PAIR_ROWS, win), 1)
                s = jnp.where(col + r0 >= ctx, s, NEG_INF)
            sk = sink_ref[rows, :]
            m = jnp.maximum(_row_stat(jnp.max, _fold_lanes(s, jnp.maximum)), sk)
            e = [jnp.exp2(s[:, c * LANES:(c + 1) * LANES] - m) for c in range(win // LANES)]
            den = _row_stat(jnp.sum, functools.reduce(jnp.add, e)) + jnp.exp2(sk - m)
            inv = 1.0 / den
            for c in range(win // LANES):
                e_scr[slot, rows, c * LANES:(c + 1) * LANES] = (e[c] * inv).astype(BF16)
        o2 = _dot(e_scr[slot], vwin(r0))
        half = 4 * PAIR_ROWS
        for p in range(4):
            o = jnp.where(lo, o2[p * PAIR_ROWS:(p + 1) * PAIR_ROWS],
                          o2[half + p * PAIR_ROWS:half + (p + 1) * PAIR_ROWS])
            o_ref[stream, pl.ds(r0, PAIR_ROWS), p * LANES:(p + 1) * LANES] = o.astype(BF16)

    _run_steps(make_step, q_ref.shape[0], nq // PAIR_ROWS, ctx // PAIR_ROWS if pad_front else 0, PAIR_ROWS)


def _streams_per_block(nb, pad_front):
    return 1 if pad_front else nb


def _attn_a(q, k, v, ext, sink, ctx, pad_front):
    nb, nq, _ = q.shape
    nk = k.shape[1]
    assert nq % PAIR_ROWS == 0 and nk == (nq if pad_front else ctx + nq)
    bb = _streams_per_block(nb, pad_front)
    per_b = lambda r, w: pl.BlockSpec((bb, r, w), lambda b: (b, 0, 0))
    rows, win = A_HEADS * PAIR_ROWS, ctx + PAIR_ROWS
    scratch = [pltpu.VMEM((rows, win), F32),
               pltpu.VMEM((PAIR_SLOTS, rows, win), F32), pltpu.VMEM((PAIR_SLOTS, rows, win), BF16)]
    if pad_front:
        scratch += [pltpu.VMEM((ctx + nq, A_KV), BF16)] * 2
    return pl.pallas_call(
        functools.partial(_attn_a_kernel, nq=nq, ctx=ctx, pad_front=pad_front),
        grid=(nb // bb,),
        in_specs=[per_b(nq, A_Q), per_b(nk, A_KV), per_b(nk, A_KV), _resident(ext.shape), _resident(sink.shape)],
        out_specs=per_b(nq, A_Q),
        out_shape=jax.ShapeDtypeStruct((nb, nq, A_Q), BF16),
        scratch_shapes=scratch,
        compiler_params=_cparams("arbitrary"),
        name="attn_a",
    )(q, k, v, ext, sink)


B_STEP_ROWS = 4 * CHUNK


def _attn_b_kernel(q_ref, k_ref, v_ref, ext_ref, o_ref, bias_scr, s_scr, e_scr, r_scr, *pad_scr,
                   nq, ctx, pad_front):
    win = ctx + B_STEP_ROWS
    lo = lax.broadcasted_iota(jnp.int32, (B_STEP_ROWS, LANES), 1) < HEAD_DIM
    chunks = B_STEP_ROWS // CHUNK
    pair_idx = pl.program_id(1)

    @pl.when(pl.program_id(0) == 0)
    def _():
        for half in range(2):
            bias_scr[pair_idx, half * B_STEP_ROWS:(half + 1) * B_STEP_ROWS, :] = _band_bias(
                ext_ref[0, half:half + 1, :], B_STEP_ROWS, ctx)

    bias_ref = bias_scr.at[pair_idx]

    def make_step(stream):
        kwin, vwin = _windows(k_ref, v_ref, pad_scr, ctx, pad_front, B_STEP_ROWS, stream)
        return functools.partial(step, stream, kwin, vwin)

    def step(stream, kwin, vwin, r0, slot, masked):
        qp = q_ref[stream, pl.ds(r0, B_STEP_ROWS), :]
        zero = jnp.zeros_like(qp)
        q2 = jnp.concatenate([jnp.where(lo, qp, zero), jnp.where(lo, zero, qp)], axis=0)
        s_scr[slot] = lax.dot_general(q2, kwin(r0), _NT, preferred_element_type=F32)
        for rb in range(2 * chunks):
            rows = slice(rb * CHUNK, (rb + 1) * CHUNK)
            ci = rb % chunks
            c_lo, c_hi = CHUNK * ci // LANES, -(-(ctx + CHUNK * (ci + 1)) // LANES)
            band = slice(c_lo * LANES, c_hi * LANES)
            s = s_scr[slot, rows, band] + bias_ref[rows, band]
            if masked:
                col = lax.broadcasted_iota(jnp.int32, s.shape, 1) + c_lo * LANES
                s = jnp.where(col + r0 >= ctx, s, NEG_INF)
            m = _row_stat(jnp.max, _fold_lanes(s, jnp.maximum))
            e = [jnp.exp2(s[:, c * LANES:(c + 1) * LANES] - m) for c in range(c_hi - c_lo)]
            for c in range(win // LANES):
                blk = e[c - c_lo].astype(BF16) if c_lo <= c < c_hi else jnp.zeros((CHUNK, LANES), BF16)
                e_scr[slot, rows, c * LANES:(c + 1) * LANES] = blk
            r_scr[slot, rows, :] = 1.0 / _row_stat(jnp.sum, functools.reduce(jnp.add, e))
        o2 = _dot(e_scr[slot], vwin(r0)) * r_scr[slot]
        o = jnp.where(lo, o2[:B_STEP_ROWS], o2[B_STEP_ROWS:])
        o_ref[stream, pl.ds(r0, B_STEP_ROWS), :] = o.astype(BF16)

    _run_steps(make_step, q_ref.shape[0], nq // B_STEP_ROWS,
               -(-ctx // B_STEP_ROWS) if pad_front else 0, B_STEP_ROWS)


def _attn_b(q, k, v, ext, ctx, pad_front):
    nb, nq, _ = q.shape
    nk = k.shape[1]
    assert nq % B_STEP_ROWS == 0 and nk == (nq if pad_front else ctx + nq)
    bb = _streams_per_block(nb, pad_front)
    blk = lambda r: pl.BlockSpec((bb, r, LANES), lambda b, p: (b, 0, p))
    rows, win = 2 * B_STEP_ROWS, ctx + B_STEP_ROWS
    scratch = [pltpu.VMEM((B_HEADS // 2, rows, win), F32),
               pltpu.VMEM((PAIR_SLOTS, rows, win), F32), pltpu.VMEM((PAIR_SLOTS, rows, win), BF16),
               pltpu.VMEM((PAIR_SLOTS, rows, LANES), F32)]
    if pad_front:
        scratch += [pltpu.VMEM((ctx + nq, LANES), BF16)] * 2
    return pl.pallas_call(
        functools.partial(_attn_b_kernel, nq=nq, ctx=ctx, pad_front=pad_front),
        grid=(nb // bb, B_HEADS // 2),
        in_specs=[blk(nq), blk(nk), blk(nk),
                  pl.BlockSpec((1,) + ext.shape[1:], lambda b, p: (p, 0, 0))],
        out_specs=blk(nq),
        out_shape=jax.ShapeDtypeStruct((nb, nq, B_QKV), BF16),
        scratch_shapes=scratch,
        compiler_params=_cparams("arbitrary", "arbitrary"),
        name="attn_b",
    )(q, k, v, ext)


def _t5_bucket(rel):
    nb = T5_BUCKETS // 2
    max_exact = nb // 2
    n = -rel
    ret = jnp.where(n < 0, nb, 0)
    n = jnp.abs(n)
    nf = jnp.maximum(n, 1).astype(F32)
    large = max_exact + (jnp.log(nf / max_exact) / math.log(T5_MAX_DIST / max_exact)
                         * (nb - max_exact)).astype(jnp.int32)
    large = jnp.minimum(large, nb - 1)
    return ret + jnp.where(n < max_exact, n, large)


def _band_values(ctx, n_rows, value_of_rel):
    n_cols = ctx + n_rows
    period = -(-(n_cols + n_rows - 1) // LANES) * LANES
    d = jnp.concatenate([jnp.arange(0, period - (n_rows - 1)), jnp.arange(-(n_rows - 1), 0)])
    return value_of_rel(d - ctx)


def _bias_a(t5_table):
    return _band_values(WINDOW, PAIR_ROWS, lambda rel: t5_table.astype(F32)[_t5_bucket(rel)].T)


def _bias_b(b_rel):
    ext = _band_values(B_REACH, B_STEP_ROWS,
                       lambda rel: b_rel.astype(F32)[:, jnp.clip(rel, -B_MAX_REL, B_MAX_REL) + B_MAX_REL])
    return ext.reshape(B_HEADS // 2, 2, ext.shape[-1])


ODD_IN_PAD = 2048
ROPE_HALF = D_ROPE // 2
X1_LO = LANES // 2
NOPE_SPLIT = X1_LO - ROPE_HALF


def _head_lane_source():
    zero = D_NOPE + D_ROPE
    src = []
    for lane in range(LANES):
        if lane < ROPE_HALF:
            src.append(D_NOPE + ROPE_HALF + lane)
        elif lane < X1_LO:
            src.append(lane - ROPE_HALF)
        elif lane < X1_LO + ROPE_HALF:
            src.append(D_NOPE + lane - X1_LO)
        elif lane < X1_LO + ROPE_HALF + D_NOPE - NOPE_SPLIT:
            src.append(NOPE_SPLIT + lane - X1_LO - ROPE_HALF)
        else:
            src.append(zero)
    return src


def _to_head_lanes(a):
    padded = jnp.concatenate([a, jnp.zeros(a.shape[:-1] + (1,), a.dtype)], axis=-1)
    return padded[..., jnp.asarray(_head_lane_source(), dtype=jnp.int32)]


def _lane_masks(shape):
    lane = lax.broadcasted_iota(jnp.int32, shape, 1)
    rope = (lane < ROPE_HALF) | ((lane >= X1_LO) & (lane < X1_LO + ROPE_HALF))
    used = lane < X1_LO + ROPE_HALF + D_NOPE - NOPE_SPLIT
    return used & ~rope, rope


def _rope(y, cosf, sinf):
    return y * cosf + pltpu.roll(y, LANES // 2, 1) * sinf


ROW_BLOCK = 128


def _row_blocks(tm):
    rb = min(ROW_BLOCK, tm)
    return [slice(r, r + rb) for r in range(0, tm, rb)]


def _segment_mean_squares(src_scr, cols, seg_ref, sq_scr, ms_scr, slot):
    for rows in _row_blocks(src_scr.shape[0]):
        v = src_scr[rows, cols]
        sq_scr[slot, rows, :] = (v * v).astype(BF16)
    ms_scr[slot] = _dot(sq_scr[slot], seg_ref[...])


def _expand_kv(ckvb_ref, kpe_blk_ref, wkv_ref, kgain_ref, kseg_ref, kv_scr, sq_scr, ms_scr, k_ref, v_ref,
               between=None):
    k_cols = D_HEADS * LANES
    tm = kv_scr.shape[0]
    for grp in range(kv_scr.shape[1] // MXU_COLS):
        if between and grp in between:
            between[grp]()
        cols = slice(grp * MXU_COLS, (grp + 1) * MXU_COLS)
        kv_scr[:, cols] = _dot(ckvb_ref[...], wkv_ref[:, cols])
        if cols.start < k_cols:
            slot = grp % 2
            _segment_mean_squares(kv_scr, cols, kseg_ref, sq_scr, ms_scr, slot)
        for rows in _row_blocks(tm):
            if cols.start < k_cols:
                for half in range(2):
                    h = 2 * grp + half
                    kb = kv_scr[rows, h * LANES:(h + 1) * LANES]
                    ms = ms_scr[slot, rows, half * LANES:(half + 1) * LANES]
                    kn = (kb * lax.rsqrt(ms + EPS)) * kgain_ref[...]
                    k_ref[rows, h * LANES:(h + 1) * LANES] = (kn + kpe_blk_ref[rows, :]).astype(BF16)
            else:
                v_ref[rows, cols.start - k_cols:cols.stop - k_cols] = kv_scr[rows, cols].astype(BF16)


def _odd_inproj_kernel(x_ref, g_ref, w_ref, cinit_ref, convw_ref, qan_ref, wqb_ref, qgain_ref,
                       kvan_ref, krgain_ref, invf_ref, sgn_ref, wkv_ref, kgain_ref, qseg_ref, kseg_ref,
                       yc_ref, q_ref, k_ref, v_ref, ckv_ref, kpe_ref, cs_ref,
                       h_scr, z_scr, uext_scr, qn_scr, q_scr, kv_scr, rot_scr, ckvb_scr, kpe_scr, sq_scr, ms_scr,
                       *, tm, tpb, pos0):
    i = pl.program_id(0)
    tile = i % tpb
    off = pl.multiple_of(tile * tm, tm)
    blocks = _row_blocks(tm)
    rb = blocks[0].stop

    @pl.when(tile == 0)
    def _():
        uext_scr[0:8, :] = cinit_ref[0]

    @pl.when(i < tpb)
    def _():
        row = lax.broadcasted_iota(jnp.int32, (tm, LANES), 0)
        ang = (row + (tile * tm + pos0)).astype(F32) * invf_ref[...]
        rot_scr[0, pl.ds(off, tm), :] = jnp.cos(ang)
        rot_scr[1, pl.ds(off, tm), :] = jnp.sin(ang) * sgn_ref[...]

    h_scr[...] = _rms(x_ref[...], g_ref[...]).astype(BF16)
    low_lanes = lax.broadcasted_iota(jnp.int32, (rb, LANES), 1) < ROPE_HALF

    def proj(c0, c1):
        z_scr[:, c0:c1] = _dot(h_scr[...], w_ref[:, c0:c1])

    def rotary(y, rows):
        cosf = rot_scr[0, pl.ds(off + rows.start, rb), :]
        sinf = rot_scr[1, pl.ds(off + rows.start, rb), :]
        return _rope(y, cosf, sinf)

    def conv_group(grp):
        c0 = grp * MXU_COLS
        cols = slice(c0, c0 + MXU_COLS)
        for base in (0, C_WIDTH, 2 * C_WIDTH):
            proj(base + c0, base + c0 + MXU_COLS)
        for rows in blocks:
            up = slice(rows.start + 8, rows.stop + 8)
            uext_scr[up, cols] = (z_scr[rows, C_WIDTH + c0:C_WIDTH + c0 + MXU_COLS]
                                  * z_scr[rows, 2 * C_WIDTH + c0:2 * C_WIDTH + c0 + MXU_COLS])
        for rows in blocks:
            yconv = convw_ref[0:1, cols] * uext_scr[rows.start + 6:rows.stop + 6, cols]
            yconv = yconv + convw_ref[1:2, cols] * uext_scr[rows.start + 7:rows.stop + 7, cols]
            yconv = yconv + convw_ref[2:3, cols] * uext_scr[rows.start + 8:rows.stop + 8, cols]
            yc_ref[rows, cols] = (z_scr[rows, cols] * yconv).astype(BF16)

    proj(1536, 1792)
    proj(1792, 2048)

    for rows in blocks:
        qn_scr[rows, :] = _rms(z_scr[rows, 1536:1792], qan_ref[...]).astype(BF16)
        ckv = _rms(z_scr[rows, 1792:1920], kvan_ref[...])
        ckv_ref[rows, :] = ckv
        ckvb_scr[rows, :] = ckv.astype(BF16)

    for grp in range(D_HEADS * LANES // MXU_COLS):
        cols = slice(grp * MXU_COLS, (grp + 1) * MXU_COLS)
        q_scr[:, cols] = _dot(qn_scr[...], wqb_ref[:, cols])
        slot = grp % 2
        _segment_mean_squares(q_scr, cols, qseg_ref, sq_scr, ms_scr, slot)
        for half in range(2):
            h = 2 * grp + half
            for rows in blocks:
                blk = q_scr[rows, h * LANES:(h + 1) * LANES]
                inv = lax.rsqrt(ms_scr[slot, rows, half * LANES:(half + 1) * LANES] + EPS)
                y = rotary((blk * inv) * qgain_ref[...], rows)
                q_ref[rows, h * LANES:(h + 1) * LANES] = y.astype(BF16)

    for rows in blocks:
        kb = z_scr[rows, 1920:2048]
        ms = jnp.sum(kb * kb, axis=-1, keepdims=True) * (1.0 / D_ROPE)
        kpe_blk = rotary((kb * lax.rsqrt(ms + EPS)) * krgain_ref[...], rows)
        kpe_scr[rows, :] = kpe_blk
        x1_then_x2 = jnp.where(low_lanes, pltpu.roll(kpe_blk, LANES - X1_LO, 1), pltpu.roll(kpe_blk, ROPE_HALF, 1))
        kpe_ref[rows, :] = x1_then_x2[:, 0:D_ROPE]
    _expand_kv(ckvb_scr, kpe_scr, wkv_ref, kgain_ref, kseg_ref, kv_scr, sq_scr, ms_scr, k_ref, v_ref,
               between={0: lambda: conv_group(0), 4: lambda: conv_group(1)})

    cs_ref[0] = uext_scr[tm + 6:tm + 8, :]
    uext_scr[0:8, :] = uext_scr[tm:tm + 8, :]


def _odd_inproj(x, norm_g, ow, conv_init, n_batch, pos0):
    t = x.shape[0]
    tm = min(TOKEN_TILE, t // n_batch)
    tpb = t // n_batch // tm
    assert tpb * tm * n_batch == t
    row = lambda w: pl.BlockSpec((tm, w), lambda i: (i, 0))
    per_b = lambda r, w: pl.BlockSpec((1, r, w), lambda i: (i // tpb, 0, 0))
    small = [ow["conv_w"], ow["qan"], ow["wqb"], ow["qgain"], ow["kvan"], ow["krgain"],
             ow["invf"], ow["sgn"], ow["wkv"], ow["kgain"], ow["qseg"], ow["kseg"]]
    kv_w = D_HEADS * LANES
    v_w = D_HEADS * D_V
    return pl.pallas_call(
        functools.partial(_odd_inproj_kernel, tm=tm, tpb=tpb, pos0=pos0),
        grid=(t // tm,),
        in_specs=[row(D_MODEL), _resident((1, D_MODEL)), _resident(ow["w_in"].shape), per_b(8, C_WIDTH)]
                 + [_resident(a.shape) for a in small],
        out_specs=[row(C_WIDTH), row(kv_w), row(kv_w), row(v_w), row(D_KV_LORA), row(D_ROPE),
                   per_b(CONV_W - 1, C_WIDTH)],
        out_shape=[jax.ShapeDtypeStruct((t, C_WIDTH), BF16), jax.ShapeDtypeStruct((t, kv_w), BF16),
                   jax.ShapeDtypeStruct((t, kv_w), BF16), jax.ShapeDtypeStruct((t, v_w), BF16),
                   jax.ShapeDtypeStruct((t, D_KV_LORA), F32), jax.ShapeDtypeStruct((t, D_ROPE), F32),
                   jax.ShapeDtypeStruct((n_batch, CONV_W - 1, C_WIDTH), F32)],
        scratch_shapes=[pltpu.VMEM((tm, D_MODEL), BF16), pltpu.VMEM((tm, ODD_IN_PAD), F32),
                        pltpu.VMEM((tm + 8, C_WIDTH), F32), pltpu.VMEM((tm, D_Q_LORA), BF16),
                        pltpu.VMEM((tm, kv_w), F32), pltpu.VMEM((tm, kv_w + v_w), F32),
                        pltpu.VMEM((2, tpb * tm, LANES), F32), pltpu.VMEM((tm, D_KV_LORA), BF16),
                        pltpu.VMEM((tm, LANES), F32), pltpu.VMEM((2, tm, MXU_COLS), BF16),
                        pltpu.VMEM((2, tm, MXU_COLS), F32)],
        compiler_params=_cparams("arbitrary"),
        name="odd_inproj",
    )(x, norm_g.reshape(1, D_MODEL), ow["w_in"], conv_init, *small)


def _kv_expand_kernel(ckv_ref, kpe_ref, place_ref, wkv_ref, kgain_ref, kseg_ref, k_ref, v_ref,
                      kv_scr, ckvb_scr, kpe_scr, sq_scr, ms_scr):
    ckvb_scr[...] = ckv_ref[...].astype(BF16)
    kpe_scr[...] = _dot(kpe_ref[...].astype(BF16), place_ref[...])
    _expand_kv(ckvb_scr, kpe_scr, wkv_ref, kgain_ref, kseg_ref, kv_scr, sq_scr, ms_scr, k_ref, v_ref)


def _kv_expand(ckv, kpe, ow):
    t = ckv.shape[0]
    tm = min(TOKEN_TILE, t)
    row = lambda w: pl.BlockSpec((tm, w), lambda i: (i, 0))
    kv_w = D_HEADS * LANES
    v_w = D_HEADS * D_V
    return pl.pallas_call(
        _kv_expand_kernel,
        grid=(t // tm,),
        in_specs=[row(D_KV_LORA), row(D_ROPE), _resident(ow["place"].shape), _resident(ow["wkv"].shape),
                  _resident(ow["kgain"].shape), _resident(ow["kseg"].shape)],
        out_specs=[row(kv_w), row(v_w)],
        out_shape=[jax.ShapeDtypeStruct((t, kv_w), BF16), jax.ShapeDtypeStruct((t, v_w), BF16)],
        scratch_shapes=[pltpu.VMEM((tm, kv_w + v_w), F32), pltpu.VMEM((tm, D_KV_LORA), BF16),
                        pltpu.VMEM((tm, LANES), F32), pltpu.VMEM((2, tm, MXU_COLS), BF16),
                        pltpu.VMEM((2, tm, MXU_COLS), F32)],
        compiler_params=_cparams("arbitrary"),
        name="kv_expand",
    )(ckv, kpe, ow["place"], ow["wkv"], ow["kgain"], ow["kseg"])


def _odd_weights(w_in, conv_w, q_a_norm, w_q_b, kv_a_norm, w_kv_b, qn_nope, qn_rope, kn_nope, kn_rope):
    z = lambda n: jnp.zeros((n,), F32)
    qk = D_NOPE + D_ROPE
    w_in = _to_bf16(w_in)
    kr_blk = _to_head_lanes(jnp.concatenate([jnp.zeros((D_MODEL, D_NOPE), BF16), w_in[:, 1920:]], axis=1))
    w_pad = jnp.concatenate([w_in[:, :1920], kr_blk], axis=1)
    wqb = _to_head_lanes(w_q_b.astype(BF16).reshape(D_Q_LORA, D_HEADS, qk)).reshape(D_Q_LORA, D_HEADS * LANES)
    kvb = w_kv_b.astype(BF16).reshape(D_KV_LORA, D_HEADS, D_NOPE + D_V)
    wk = _to_head_lanes(jnp.concatenate([kvb[:, :, :D_NOPE], jnp.zeros((D_KV_LORA, D_HEADS, D_ROPE), BF16)], axis=-1))
    wkv = jnp.concatenate([wk.reshape(D_KV_LORA, D_HEADS * LANES),
                           kvb[:, :, D_NOPE:].reshape(D_KV_LORA, D_HEADS * D_V)], axis=1)
    inv = 1.0 / (ROPE_THETA ** (jnp.arange(ROPE_HALF, dtype=F32) / ROPE_HALF))
    ones = jnp.ones((ROPE_HALF,), F32)
    lanes = lambda v: _to_head_lanes(v).reshape(1, LANES)
    in_nope = lanes(jnp.concatenate([jnp.ones((D_NOPE,), F32), z(D_ROPE)]))
    in_rope = lanes(jnp.concatenate([z(D_NOPE), jnp.ones((D_ROPE,), F32)]))
    seg_nope = in_nope.T * in_nope * (1.0 / D_NOPE)
    seg_rope = in_rope.T * in_rope * (1.0 / D_ROPE)
    two_heads = lambda m: jnp.kron(jnp.eye(MXU_COLS // LANES, dtype=F32), m).astype(BF16)
    return {
        "qseg": two_heads(seg_nope + seg_rope),
        "kseg": two_heads(seg_nope),
        "w_in": w_pad,
        "conv_w": conv_w.astype(F32),
        "qan": q_a_norm.reshape(1, D_Q_LORA),
        "wqb": wqb,
        "qgain": lanes(jnp.concatenate([qn_nope, qn_rope])) * MLA_QSCALE,
        "kvan": kv_a_norm.reshape(1, D_KV_LORA),
        "krgain": lanes(jnp.concatenate([z(D_NOPE), kn_rope])),
        "invf": lanes(jnp.concatenate([z(D_NOPE), inv, inv])),
        "sgn": lanes(jnp.concatenate([z(D_NOPE), -ones, ones])),
        "wkv": wkv,
        "kgain": lanes(jnp.concatenate([kn_nope, z(D_ROPE)])),
        "place": _to_head_lanes(jnp.concatenate([jnp.zeros((D_ROPE, D_NOPE), BF16), jnp.eye(D_ROPE, dtype=BF16)], axis=1)),
    }


def _mla_prompt_kernel(q_ref, k_ref, v_ref, o_ref, s_scr, e_scr, r_scr, *, seq):
    qb = min(MLA_QBLOCK, seq)
    hb = qb // 2
    lo = lax.broadcasted_iota(jnp.int32, (qb, LANES), 1) < D_V
    first_half = lax.broadcasted_iota(jnp.int32, (CHUNK, LANES), 1) < CHUNK
    scores = lambda q, k: lax.dot_general(q, k, _NT, preferred_element_type=F32)
    for i in range(seq // qb):
        tk = qb * (i + 1)
        t0, t1 = tk - qb, tk - hb
        top, bot = slice(0, hb), slice(hb, qb)
        outs = []
        for h in range(2):
            hl = slice(h * LANES, (h + 1) * LANES)
            q0 = i * qb
            if t0 > 0:
                s_scr[h, :, 0:t0] = scores(q_ref[0, q0:q0 + qb, hl], k_ref[0, 0:t0, hl])
            s_scr[h, top, t0:t1] = scores(q_ref[0, q0:q0 + hb, hl], k_ref[0, t0:t1, hl])
            s_scr[h, bot, t0:tk] = scores(q_ref[0, q0 + hb:q0 + qb, hl], k_ref[0, t0:tk, hl])
            for rb in range(qb // CHUNK):
                rows = slice(rb * CHUNK, (rb + 1) * CHUNK)
                visible = t0 + CHUNK * (rb + 1)
                width = t1 if rb < hb // CHUNK else tk
                n_full, ragged = visible // LANES, visible % LANES != 0
                blk = lambda c: s_scr[h, rows, c * LANES:(c + 1) * LANES]
                cols = [blk(c) for c in range(n_full)]
                if ragged:
                    cols.append(jnp.where(first_half, blk(n_full), NEG_INF))
                mm = cols[0]
                for c in cols[1:]:
                    mm = jnp.maximum(mm, c)
                m = _row_stat(jnp.max, mm)
                acc = None
                for c in range(width // LANES):
                    if c < len(cols):
                        e = jnp.exp2(blk(c) - m)
                        if c >= n_full:
                            e = jnp.where(first_half, e, 0.0)
                        acc = e if acc is None else acc + e
                    else:
                        e = jnp.zeros((CHUNK, LANES), F32)
                    e_scr[h, rows, c * LANES:(c + 1) * LANES] = e.astype(BF16)
                r_scr[h, rows, :] = 1.0 / _row_stat(jnp.sum, acc)
            pv = jnp.concatenate([_dot(e_scr[h, top, t0:t1], v_ref[0, t0:t1, :]),
                                  _dot(e_scr[h, bot, t0:tk], v_ref[0, t0:tk, :])], axis=0)
            if t0 > 0:
                pv = pv + _dot(e_scr[h, :, 0:t0], v_ref[0, 0:t0, :])
            outs.append(pv * r_scr[h])
        o_ref[0, i * qb:(i + 1) * qb, :] = jnp.where(lo, outs[0], outs[1]).astype(BF16)


def _mla_prompt(q, k, v):
    nb, seq, _ = q.shape
    qb = min(MLA_QBLOCK, seq)
    return pl.pallas_call(
        functools.partial(_mla_prompt_kernel, seq=seq),
        grid=(nb, D_HEADS // 2),
        in_specs=[pl.BlockSpec((1, seq, 2 * LANES), lambda b, p: (b, 0, p)),
                  pl.BlockSpec((1, seq, 2 * LANES), lambda b, p: (b, 0, p)),
                  pl.BlockSpec((1, seq, LANES), lambda b, p: (b, 0, p))],
        out_specs=pl.BlockSpec((1, seq, LANES), lambda b, p: (b, 0, p)),
        out_shape=jax.ShapeDtypeStruct((nb, seq, D_HEADS * D_V), BF16),
        scratch_shapes=[pltpu.VMEM((2, qb, seq), F32), pltpu.VMEM((2, qb, seq), BF16),
                        pltpu.VMEM((2, qb, LANES), F32)],
        compiler_params=_cparams("arbitrary", "arbitrary"),
        name="mla_prompt",
    )(q, k, v)


def _mla_sample_kernel(q_ref, kc_ref, vc_ref, kn_ref, vn_ref, o_ref):
    nq = q_ref.shape[1]
    lo = lax.broadcasted_iota(jnp.int32, (nq, LANES), 1) < D_V
    outs = []
    for h in range(2):
        cols = slice(h * LANES, (h + 1) * LANES)
        q = q_ref[0, :, cols]
        s1 = lax.dot_general(q, kc_ref[0, :, cols], _NT, preferred_element_type=F32)
        s2 = lax.dot_general(q, kn_ref[0, :, cols], _NT, preferred_element_type=F32)
        m = jnp.maximum(jnp.max(s1, axis=-1, keepdims=True), jnp.max(s2, axis=-1, keepdims=True))
        e1 = jnp.exp2(s1 - m)
        e2 = jnp.exp2(s2 - m)
        den = jnp.sum(e1, axis=-1, keepdims=True) + jnp.sum(e2, axis=-1, keepdims=True)
        outs.append((_dot(e1.astype(BF16), vc_ref[0]) + _dot(e2.astype(BF16), vn_ref[0])) / den)
    o_ref[0] = jnp.where(lo, outs[0], outs[1]).astype(BF16)


def _mla_sample(q, kc, vc, kn, vn):
    nb, nq, _ = q.shape
    past = kc.shape[1]
    two = lambda r: pl.BlockSpec((1, r, 2 * LANES), lambda b, p: (b, 0, p))
    one = lambda r: pl.BlockSpec((1, r, LANES), lambda b, p: (b, 0, p))
    return pl.pallas_call(
        _mla_sample_kernel,
        grid=(nb, D_HEADS // 2),
        in_specs=[two(nq), two(past), one(past), two(nq), one(nq)],
        out_specs=one(nq),
        out_shape=jax.ShapeDtypeStruct((nb, nq, D_HEADS * D_V), BF16),
        compiler_params=_cparams("arbitrary", "arbitrary"),
        name="mla_sample",
    )(q, kc, vc, kn, vn)


def _even_layer(xp, xs, nb, seq, ndb, dseq, norm_g, w_in, w_out, a_qn, a_kn, a_sinks, b_qn, b_kn, b_rel,
                t5_table, ck_a, cv_a, ck_b, cv_b, ffn2):
    w_in = _to_bf16(w_in)
    w_out = _to_bf16(w_out)
    w_in_p = jnp.concatenate([_pair_heads(w_in[:, :A_Q], 1), w_in[:, A_Q:]], axis=1)
    woa = _pair_heads(w_out[:A_Q], 0)
    wob = w_out[A_Q:]
    ones = lambda n: jnp.ones((n,), F32)
    scale = HEAD_DIM ** -0.5 * LOG2E
    gain_row = jnp.concatenate([jnp.tile(a_qn, A_HEADS) * scale, jnp.tile(a_kn, A_KV_HEADS), ones(A_KV),
                                jnp.tile(b_qn, B_HEADS) * scale, jnp.tile(b_kn, B_HEADS), ones(B_QKV)]
                               ).reshape(1, EVEN_IN).astype(F32)
    bias_a = _bias_a(t5_table) * LOG2E
    bias_b = _bias_b(b_rel) * LOG2E
    sink = jnp.broadcast_to(jnp.repeat(a_sinks.astype(F32) * LOG2E, PAIR_ROWS)[:, None],
                            (A_HEADS * PAIR_ROWS, LANES))
    la, lb = min(WINDOW, seq), min(B_REACH, seq)

    aq, ak, av, bq, bk, bv, cak, cav, cbk, cbv = _even_inproj(xp, norm_g, w_in_p, gain_row, nb, la, lb)
    r3 = lambda a: a.reshape(nb, seq, a.shape[-1])
    ya = _attn_a(r3(aq), r3(ak), r3(av), bias_a, sink, WINDOW, True)
    yb = _attn_b(r3(bq), r3(bk), r3(bv), bias_b, B_REACH, True)
    xp = _ffn(xp, *ffn2, out_proj=(ya.reshape(nb * seq, A_Q), yb.reshape(nb * seq, B_QKV), woa, wob))
    st_p = (cak.reshape(nb, la, A_KV_HEADS, HEAD_DIM), cav.reshape(nb, la, A_KV_HEADS, HEAD_DIM),
            cbk.reshape(nb, lb, B_HEADS, HEAD_DIM), cbv.reshape(nb, lb, B_HEADS, HEAD_DIM))

    ts = ndb * dseq
    aq, ak, av, bq, bk, bv, nak, nav, nbk, nbv = _even_inproj(xs, norm_g, w_in_p, gain_row, 1, ts, ts)
    pad_q = lambda a, rows: jnp.pad(a.reshape(ndb, dseq, a.shape[-1]), ((0, 0), (0, rows - dseq), (0, 0)))

    def window(cache, new, rows):
        w = cache.shape[-2] * cache.shape[-1]
        full = jnp.concatenate([cache.reshape(ndb, -1, w), new.reshape(ndb, dseq, w)], axis=1)
        buf = jnp.pad(full, ((0, 0), (0, rows - dseq), (0, 0))).astype(BF16)
        return full[:, dseq:].reshape(cache.shape), buf

    st_ak, kbuf_a = window(ck_a, nak, PAIR_ROWS)
    st_av, vbuf_a = window(cv_a, nav, PAIR_ROWS)
    st_bk, kbuf_b = window(ck_b, nbk, B_STEP_ROWS)
    st_bv, vbuf_b = window(cv_b, nbv, B_STEP_ROWS)
    ya = _attn_a(pad_q(aq, PAIR_ROWS), kbuf_a, vbuf_a, bias_a, sink, WINDOW, False)[:, :dseq]
    yb = _attn_b(pad_q(bq, B_STEP_ROWS), kbuf_b, vbuf_b, bias_b, B_REACH, False)[:, :dseq]
    xs = _ffn(xs, *ffn2, out_proj=(ya.reshape(ts, A_Q), yb.reshape(ts, B_QKV), woa, wob))
    return xp, xs, st_p, (st_ak, st_av, st_bk, st_bv)


def _odd_layer(xp, xs, nb, seq, ndb, dseq, past, norm_g, ow, w_out, conv_prev, c_ckv, c_kpe, ffn2):
    w_out = _to_bf16(w_out)
    woc = w_out[:C_WIDTH]
    wod = w_out[C_WIDTH:]

    zero_init = jnp.zeros((nb, 8, C_WIDTH), F32)
    yc, q, k, v, ckv, kpe, cs = _odd_inproj(xp, norm_g, ow, zero_init, nb, 0)
    r3 = lambda a: a.reshape(nb, seq, a.shape[-1])
    yd = _mla_prompt(r3(q), r3(k), r3(v))
    xp = _ffn(xp, *ffn2, out_proj=(yc, yd.reshape(nb * seq, D_HEADS * D_V), woc, wod))
    st_p = (cs, ckv.reshape(nb, seq, D_KV_LORA), kpe.reshape(nb, seq, D_ROPE))

    ts = ndb * dseq
    init = jnp.pad(conv_prev.astype(F32), ((0, 0), (8 - (CONV_W - 1), 0), (0, 0)))
    yc, q, kn, vn, ckv, kpe, cs = _odd_inproj(xs, norm_g, ow, init, ndb, past)
    kc, vc = _kv_expand(c_ckv.reshape(ndb * past, D_KV_LORA), c_kpe.reshape(ndb * past, D_ROPE), ow)
    s3 = lambda a: a.reshape(ndb, dseq, a.shape[-1])
    yd = _mla_sample(s3(q), kc.reshape(ndb, past, -1), vc.reshape(ndb, past, -1), s3(kn), s3(vn))
    xs = _ffn(xs, *ffn2, out_proj=(yc, yd.reshape(ts, D_HEADS * D_V), woc, wod))
    st_s = (cs, ckv.reshape(ndb, dseq, D_KV_LORA), kpe.reshape(ndb, dseq, D_ROPE))
    return xp, xs, st_p, st_s


def kernel(x_prompt, x_sample, cache_a_k, cache_a_v, cache_b_k, cache_b_v, state_c_conv, cache_d_ckv, cache_d_kpe, ff1_norm, ff1_w_gu, ff1_w_down, mix_norm, ff2_norm, ff2_w_gu, ff2_w_down, t5_bias_table, ev_w_in, ev_w_out, a_q_norm, a_k_norm, a_sinks, b_q_norm, b_k_norm, b_rel_bias, od_w_in, od_w_out, c_conv_w, d_q_a_norm, d_w_q_b, d_kv_a_norm, d_w_kv_b, d_q_nope_norm, d_q_rope_norm, d_k_nope_norm, d_k_rope_norm):
    nb, seq, _ = x_prompt.shape
    ndb, dseq, _ = x_sample.shape
    past = cache_d_ckv.shape[2]
    depth = ff1_norm.shape[0]
    assert seq % TOKEN_TILE == 0 and dseq == CHUNK and past % CHUNK == 0
    xp = x_prompt.reshape(nb * seq, D_MODEL)
    xs = x_sample.reshape(ndb * dseq, D_MODEL)
    even_p, even_s, odd_p, odd_s = [], [], [], []
    for l in range(depth):
        i = l // 2
        ff1 = _ffn_weights(ff1_w_gu, ff1_w_down, l)
        ffn2 = (ff2_norm[l],) + _ffn_weights(ff2_w_gu, ff2_w_down, l)
        xp = _ffn(xp, ff1_norm[l], *ff1)
        xs = _ffn(xs, ff1_norm[l], *ff1)
        if l % 2 == 0:
            xp, xs, sp, ss = _even_layer(
                xp, xs, nb, seq, ndb, dseq, mix_norm[l], ev_w_in[i], ev_w_out[i], a_q_norm[i], a_k_norm[i],
                a_sinks[i], b_q_norm[i], b_k_norm[i], b_rel_bias[i], t5_bias_table,
                cache_a_k[i], cache_a_v[i], cache_b_k[i], cache_b_v[i], ffn2)
            even_p.append(sp)
            even_s.append(ss)
        else:
            ow = _odd_weights(od_w_in[i], c_conv_w[i], d_q_a_norm[i], d_w_q_b[i], d_kv_a_norm[i], d_w_kv_b[i],
                              d_q_nope_norm[i], d_q_rope_norm[i], d_k_nope_norm[i], d_k_rope_norm[i])
            xp, xs, sp, ss = _odd_layer(xp, xs, nb, seq, ndb, dseq, past, mix_norm[l], ow, od_w_out[i],
                                        state_c_conv[i], cache_d_ckv[i], cache_d_kpe[i], ffn2)
            odd_p.append(sp)
            odd_s.append(ss)
    stack = lambda group, j: group[0][j][None] if len(group) == 1 else jnp.stack([g[j] for g in group])
    return (xp.reshape(nb, seq, D_MODEL), xs.reshape(ndb, dseq, D_MODEL),
            stack(even_p, 0), stack(even_p, 1), stack(even_p, 2), stack(even_p, 3),
            stack(odd_p, 0), stack(odd_p, 1), stack(odd_p, 2),
            stack(even_s, 0), stack(even_s, 1), stack(even_s, 2), stack(even_s, 3),
            stack(odd_s, 0), stack(odd_s, 1), stack(odd_s, 2))
```

```python
import functools
import math

import jax
import jax.numpy as jnp
from jax import lax
from jax.experimental import pallas as pl
from jax.experimental.pallas import tpu as pltpu

F32 = jnp.float32
BF16 = jnp.bfloat16

D_MODEL = 1024
CHUNK = 64
HEAD_DIM = 64
EPS = 1e-6
A_HEADS = 8
A_KV_HEADS = 2
WINDOW = 128
T5_BUCKETS = 32
T5_MAX_DIST = 128
B_HEADS = 8
B_REACH = 512
B_MAX_REL = 128
C_WIDTH = 512
CONV_W = 3
D_HEADS = 8
D_Q_LORA = 256
D_KV_LORA = 128
D_NOPE = 64
D_ROPE = 32
D_V = 64
ROPE_THETA = 10000.0
FFN_DIM = 2816
A_Q = A_HEADS * HEAD_DIM
A_KV = A_KV_HEADS * HEAD_DIM
B_QKV = B_HEADS * HEAD_DIM
EVEN_IN = A_Q + 2 * A_KV + 3 * B_QKV

LANES = 128
TOKEN_TILE = 512
CAST_BLOCK_BYTES = 6 * 1024 * 1024
MXU_COLS = 256
FFN_CHUNK = MXU_COLS
PAIR_ROWS = 2 * CHUNK
MLA_QBLOCK = 512
VMEM_LIMIT_BYTES = 56 * 1024 * 1024
LOG2E = math.log2(math.e)
MLA_QSCALE = (D_NOPE + D_ROPE) ** -0.5 * LOG2E
NEG_INF = float("-inf")

_NT = (((1,), (1,)), ((), ()))


def _cparams(*sem):
    return pltpu.CompilerParams(dimension_semantics=sem, vmem_limit_bytes=VMEM_LIMIT_BYTES)


def _resident(shape):
    zeros = (0,) * len(shape)
    return pl.BlockSpec(shape, lambda *_: zeros, pipeline_mode=pl.Buffered(1))


def _rms(x, g):
    ms = jnp.mean(x * x, axis=-1, keepdims=True)
    return (x * lax.rsqrt(ms + EPS)) * g


def _dot(a, b):
    return jnp.dot(a, b, preferred_element_type=F32)


def _ffn_kernel(*refs, n_chunks, fused_out):
    if fused_out:
        (x_ref, ya_ref, yb_ref, woa_ref, wob_ref, g_ref, wgu_ref, wd_ref,
         o_ref, h_scr, a_scr) = refs
        y = _dot(ya_ref[...], woa_ref[...]) + _dot(yb_ref[...], wob_ref[...])
        o_ref[...] = x_ref[...] + y
        res_ref = o_ref
    else:
        x_ref, g_ref, wgu_ref, wd_ref, o_ref, h_scr, a_scr = refs
        res_ref = x_ref
    h_scr[...] = _rms(res_ref[...], g_ref[...]).astype(BF16)
    for j in range(n_chunks):
        h = h_scr[...]
        g = _dot(h, wgu_ref[:, j * FFN_CHUNK:(j + 1) * FFN_CHUNK])
        u = _dot(h, wgu_ref[:, FFN_DIM + j * FFN_CHUNK:FFN_DIM + (j + 1) * FFN_CHUNK])
        a_scr[:, j * FFN_CHUNK:(j + 1) * FFN_CHUNK] = ((g * jax.nn.sigmoid(g)) * u).astype(BF16)
    o_ref[...] = res_ref[...] + 0.5 * _dot(a_scr[...], wd_ref[...])


def _ffn(x, norm_g, wgu, wd, out_proj=None):
    t = x.shape[0]
    tm = min(TOKEN_TILE, t)
    n_chunks = FFN_DIM // FFN_CHUNK
    row = lambda w: pl.BlockSpec((tm, w), lambda i: (i, 0))
    in_specs = [row(D_MODEL)]
    args = [x]
    if out_proj is not None:
        ya, yb, woa, wob = out_proj
        in_specs += [row(ya.shape[1]), row(yb.shape[1]), _resident(woa.shape), _resident(wob.shape)]
        args += [ya, yb, woa, wob]
    in_specs += [_resident((1, D_MODEL)), _resident(wgu.shape), _resident(wd.shape)]
    args += [norm_g.reshape(1, D_MODEL), wgu, wd]
    return pl.pallas_call(
        functools.partial(_ffn_kernel, n_chunks=n_chunks, fused_out=out_proj is not None),
        grid=(t // tm,),
        in_specs=in_specs,
        out_specs=row(D_MODEL),
        out_shape=jax.ShapeDtypeStruct((t, D_MODEL), F32),
        scratch_shapes=[pltpu.VMEM((tm, D_MODEL), BF16), pltpu.VMEM((tm, FFN_DIM), BF16)],
        compiler_params=_cparams("arbitrary"),
        name="ffn_out" if out_proj is not None else "ffn",
    )(*args)


def _cast_kernel(x_ref, o_ref):
    o_ref[...] = x_ref[...].astype(BF16)


def _to_bf16(w, layer=None):
    r, c = w.shape[-2:]
    fits = [d for d in range(16, r + 1, 16) if r % d == 0 and d * c * 4 <= CAST_BLOCK_BYTES]
    tr = max(fits) if fits else r
    out_spec = pl.BlockSpec((tr, c), lambda i: (i, 0))
    in_spec = out_spec if layer is None else pl.BlockSpec((None, tr, c), lambda i: (layer, i, 0))
    return pl.pallas_call(
        _cast_kernel, grid=(r // tr,), in_specs=[in_spec], out_specs=out_spec,
        out_shape=jax.ShapeDtypeStruct((r, c), BF16),
        compiler_params=_cparams("arbitrary"), name="cast_bf16",
    )(w)


def _ffn_weights(w_gu, w_down, layer):
    return _to_bf16(w_gu, layer), _to_bf16(w_down, layer)


def _norm_halves(blk, gain):
    sq = blk * blk
    lo = lax.broadcasted_iota(jnp.int32, blk.shape, 1) < HEAD_DIM
    s_lo = jnp.sum(jnp.where(lo, sq, 0.0), axis=-1, keepdims=True)
    s_hi = jnp.sum(jnp.where(lo, 0.0, sq), axis=-1, keepdims=True)
    inv = jnp.where(lo, lax.rsqrt(s_lo * (1.0 / HEAD_DIM) + EPS), lax.rsqrt(s_hi * (1.0 / HEAD_DIM) + EPS))
    return (blk * inv) * gain


def _even_inproj_kernel(x_ref, g_ref, w_ref, gain_ref,
                        aq_ref, ak_ref, av_ref, bq_ref, bk_ref, bv_ref,
                        cak_ref, cav_ref, cbk_ref, cbv_ref, h_scr, z_scr, *, tm, tpb, rows_a, rows_b):
    i = pl.program_id(0)
    last = (i % tpb) == (tpb - 1)
    h_scr[...] = _rms(x_ref[...], g_ref[...]).astype(BF16)

    def lanes(c):
        return slice(c * LANES, (c + 1) * LANES)

    def emit(c):
        blk = z_scr[:, lanes(c)]
        gain = gain_ref[:, lanes(c)]
        if c < 4:
            aq_ref[:, lanes(c)] = _norm_halves(blk, gain).astype(BF16)
        elif c == 4:
            kn = _norm_halves(blk, gain)
            ak_ref[...] = kn.astype(BF16)
            z_scr[:, lanes(c)] = kn
        elif c == 5:
            av_ref[...] = blk.astype(BF16)
        elif c < 10:
            bq_ref[:, lanes(c - 6)] = _norm_halves(blk, gain).astype(BF16)
        elif c < 14:
            kn = _norm_halves(blk, gain)
            bk_ref[:, lanes(c - 10)] = kn.astype(BF16)
            z_scr[:, lanes(c)] = kn
        else:
            bv_ref[:, lanes(c - 14)] = blk.astype(BF16)

    for grp in range(EVEN_IN // MXU_COLS):
        cols = slice(grp * MXU_COLS, (grp + 1) * MXU_COLS)
        z_scr[:, cols] = _dot(h_scr[...], w_ref[:, cols])
        emit(2 * grp)
        emit(2 * grp + 1)

    @pl.when(last)
    def _():
        cak_ref[0] = z_scr[tm - rows_a:, lanes(4)]
        cav_ref[0] = z_scr[tm - rows_a:, lanes(5)]
        cbk_ref[0] = z_scr[tm - rows_b:, 10 * LANES:14 * LANES]
        cbv_ref[0] = z_scr[tm - rows_b:, 14 * LANES:18 * LANES]


def _even_inproj(x, norm_g, w_in, gain_row, n_batch, rows_a, rows_b):
    t = x.shape[0]
    tm = min(TOKEN_TILE, t)
    tpb = t // n_batch // tm
    assert tpb * tm * n_batch == t and rows_a <= tm and rows_b <= tm
    row = lambda w: pl.BlockSpec((tm, w), lambda i: (i, 0))
    cache = lambda r, w: pl.BlockSpec((1, r, w), lambda i: (i // tpb, 0, 0))
    bshape = lambda w: jax.ShapeDtypeStruct((t, w), BF16)
    cshape = lambda r, w: jax.ShapeDtypeStruct((n_batch, r, w), F32)
    return pl.pallas_call(
        functools.partial(_even_inproj_kernel, tm=tm, tpb=tpb, rows_a=rows_a, rows_b=rows_b),
        grid=(t // tm,),
        in_specs=[row(D_MODEL), _resident((1, D_MODEL)), _resident(w_in.shape), _resident(gain_row.shape)],
        out_specs=[row(A_Q), row(A_KV), row(A_KV), row(B_QKV), row(B_QKV), row(B_QKV),
                   cache(rows_a, A_KV), cache(rows_a, A_KV), cache(rows_b, B_QKV), cache(rows_b, B_QKV)],
        out_shape=[bshape(A_Q), bshape(A_KV), bshape(A_KV), bshape(B_QKV), bshape(B_QKV), bshape(B_QKV),
                   cshape(rows_a, A_KV), cshape(rows_a, A_KV), cshape(rows_b, B_QKV), cshape(rows_b, B_QKV)],
        scratch_shapes=[pltpu.VMEM((tm, D_MODEL), BF16), pltpu.VMEM((tm, EVEN_IN), F32)],
        compiler_params=_cparams("arbitrary"),
        name="even_inproj",
    )(x, norm_g.reshape(1, D_MODEL), w_in, gain_row)


def _pair_heads(a, axis):
    shape = a.shape
    split = shape[:axis] + (2, A_HEADS // 2, HEAD_DIM) + shape[axis + 1:]
    return a.reshape(split).swapaxes(axis, axis + 1).reshape(shape)


def _fill_padded(buf, src_ref, ctx):
    buf[0:ctx, :] = jnp.zeros((ctx, buf.shape[1]), buf.dtype)
    buf[ctx:, :] = src_ref[0]


def _windows(k_ref, v_ref, pad_scr, ctx, pad_front, step_rows, stream):
    win = ctx + step_rows
    if pad_front:
        kbuf, vbuf = pad_scr
        _fill_padded(kbuf, k_ref, ctx)
        _fill_padded(vbuf, v_ref, ctx)
        return (lambda r0: kbuf[pl.ds(r0, win), :]), (lambda r0: vbuf[pl.ds(r0, win), :])
    return (lambda r0: k_ref[stream, pl.ds(r0, win), :]), (lambda r0: v_ref[stream, pl.ds(r0, win), :])


def _fold_lanes(x, op):
    acc = x[:, :LANES]
    for c in range(1, x.shape[1] // LANES):
        acc = op(acc, x[:, c * LANES:(c + 1) * LANES])
    return acc


def _row_stat(reduce_fn, x):
    return jnp.broadcast_to(reduce_fn(x, axis=-1, keepdims=True), x.shape)


PAIR_SLOTS = 4


def _run_steps(make_step, n_streams, n_steps, n_front, step_rows):
    for stream in range(n_streams):
        step = make_step(stream)
        for t in range(n_steps):
            step(t * step_rows, (stream * n_steps + t) % PAIR_SLOTS, t < n_front)


def _band_bias(ext_row, n_rows, ctx):
    win = ctx + n_rows
    period = ext_row.shape[1]
    t = pltpu.roll(jnp.broadcast_to(ext_row, (n_rows, period)), 0, 1, stride=1, stride_axis=0)[:, :win]
    chunk = lax.broadcasted_iota(jnp.int32, (n_rows, win), 0) // CHUNK
    col = lax.broadcasted_iota(jnp.int32, (n_rows, win), 1)
    visible = (col >= CHUNK * chunk) & (col < ctx + CHUNK * (chunk + 1))
    return jnp.where(visible, t, NEG_INF)


def _attn_a_kernel(q_ref, k_ref, v_ref, ext_ref, sink_ref, o_ref, bias_ref, s_scr, e_scr, *pad_scr,
                   nq, ctx, pad_front):
    win = ctx + PAIR_ROWS
    lo = lax.broadcasted_iota(jnp.int32, (PAIR_ROWS, LANES), 1) < HEAD_DIM

    @pl.when(pl.program_id(0) == 0)
    def _():
        for head in range(A_HEADS):
            bias_ref[head * PAIR_ROWS:(head + 1) * PAIR_ROWS, :] = _band_bias(
                ext_ref[head:head + 1, :], PAIR_ROWS, ctx)

    def make_step(stream):
        kwin, vwin = _windows(k_ref, v_ref, pad_scr, ctx, pad_front, PAIR_ROWS, stream)
        return functools.partial(pair, stream, kwin, vwin)

    def pair(stream, kwin, vwin, r0, slot, masked):
        q_lo, q_hi = [], []
        for p in range(4):
            qp = q_ref[stream, pl.ds(r0, PAIR_ROWS), p * LANES:(p + 1) * LANES]
            zero = jnp.zeros_like(qp)
            q_lo.append(jnp.where(lo, qp, zero))
            q_hi.append(jnp.where(lo, zero, qp))
        q2 = jnp.concatenate(q_lo + q_hi, axis=0)
        s_scr[slot] = lax.dot_general(q2, kwin(r0), _NT, preferred_element_type=F32)
        for head in range(A_HEADS):
            rows = slice(head * PAIR_ROWS, (head + 1) * PAIR_ROWS)
            s = s_scr[slot, rows, :] + bias_ref[rows, :]
            if masked:
                col = lax.broadcasted_iota(jnp.int32, (PAIR_ROWS, win), 1)
                s = jnp.where(col + r0 >= ctx, s, NEG_INF)
            sk = sink_ref[rows, :]
            m = jnp.maximum(_row_stat(jnp.max, _fold_lanes(s, jnp.maximum)), sk)
            e = [jnp.exp2(s[:, c * LANES:(c + 1) * LANES] - m) for c in range(win // LANES)]
            den = _row_stat(jnp.sum, functools.reduce(jnp.add, e)) + jnp.exp2(sk - m)
            inv = 1.0 / den
            for c in range(win // LANES):
                e_scr[slot, rows, c * LANES:(c + 1) * LANES] = (e[c] * inv).astype(BF16)
        o2 = _dot(e_scr[slot], vwin(r0))
        half = 4 * PAIR_ROWS
        for p in range(4):
            o = jnp.where(lo, o2[p * PAIR_ROWS:(p + 1) * PAIR_ROWS],
                          o2[half + p * PAIR_ROWS:half + (p + 1) * PAIR_ROWS])
            o_ref[stream, pl.ds(r0, PAIR_ROWS), p * LANES:(p + 1) * LANES] = o.astype(BF16)

    _run_steps(make_step, q_ref.shape[0], nq // PAIR_ROWS, ctx // PAIR_ROWS if pad_front else 0, PAIR_ROWS)


def _streams_per_block(nb, pad_front):
    return 1 if pad_front else nb


def _attn_a(q, k, v, ext, sink, ctx, pad_front):
    nb, nq, _ = q.shape
    nk = k.shape[1]
    assert nq % PAIR_ROWS == 0 and nk == (nq if pad_front else ctx + nq)
    bb = _streams_per_block(nb, pad_front)
    per_b = lambda r, w: pl.BlockSpec((bb, r, w), lambda b: (b, 0, 0))
    rows, win = A_HEADS * PAIR_ROWS, ctx + PAIR_ROWS
    scratch = [pltpu.VMEM((rows, win), F32),
               pltpu.VMEM((PAIR_SLOTS, rows, win), F32), pltpu.VMEM((PAIR_SLOTS, rows, win), BF16)]
    if pad_front:
        scratch += [pltpu.VMEM((ctx + nq, A_KV), BF16)] * 2
    return pl.pallas_call(
        functools.partial(_attn_a_kernel, nq=nq, ctx=ctx, pad_front=pad_front),
        grid=(nb // bb,),
        in_specs=[per_b(nq, A_Q), per_b(nk, A_KV), per_b(nk, A_KV), _resident(ext.shape), _resident(sink.shape)],
        out_specs=per_b(nq, A_Q),
        out_shape=jax.ShapeDtypeStruct((nb, nq, A_Q), BF16),
        scratch_shapes=scratch,
        compiler_params=_cparams("arbitrary"),
        name="attn_a",
    )(q, k, v, ext, sink)


B_STEP_ROWS = 4 * CHUNK


def _attn_b_kernel(q_ref, k_ref, v_ref, ext_ref, o_ref, bias_scr, s_scr, e_scr, r_scr, *pad_scr,
                   nq, ctx, pad_front):
    win = ctx + B_STEP_ROWS
    lo = lax.broadcasted_iota(jnp.int32, (B_STEP_ROWS, LANES), 1) < HEAD_DIM
    chunks = B_STEP_ROWS // CHUNK
    pair_idx = pl.program_id(1)

    @pl.when(pl.program_id(0) == 0)
    def _():
        for half in range(2):
            bias_scr[pair_idx, half * B_STEP_ROWS:(half + 1) * B_STEP_ROWS, :] = _band_bias(
                ext_ref[0, half:half + 1, :], B_STEP_ROWS, ctx)

    bias_ref = bias_scr.at[pair_idx]

    def make_step(stream):
        kwin, vwin = _windows(k_ref, v_ref, pad_scr, ctx, pad_front, B_STEP_ROWS, stream)
        return functools.partial(step, stream, kwin, vwin)

    def step(stream, kwin, vwin, r0, slot, masked):
        qp = q_ref[stream, pl.ds(r0, B_STEP_ROWS), :]
        zero = jnp.zeros_like(qp)
        q2 = jnp.concatenate([jnp.where(lo, qp, zero), jnp.where(lo, zero, qp)], axis=0)
        s_scr[slot] = lax.dot_general(q2, kwin(r0), _NT, preferred_element_type=F32)
        for rb in range(2 * chunks):
            rows = slice(rb * CHUNK, (rb + 1) * CHUNK)
            ci = rb % chunks
            c_lo, c_hi = CHUNK * ci // LANES, -(-(ctx + CHUNK * (ci + 1)) // LANES)
            band = slice(c_lo * LANES, c_hi * LANES)
            s = s_scr[slot, rows, band] + bias_ref[rows, band]
            if masked:
                col = lax.broadcasted_iota(jnp.int32, s.shape, 1) + c_lo * LANES
                s = jnp.where(col + r0 >= ctx, s, NEG_INF)
            m = _row_stat(jnp.max, _fold_lanes(s, jnp.maximum))
            e = [jnp.exp2(s[:, c * LANES:(c + 1) * LANES] - m) for c in range(c_hi - c_lo)]
            for c in range(win // LANES):
                blk = e[c - c_lo].astype(BF16) if c_lo <= c < c_hi else jnp.zeros((CHUNK, LANES), BF16)
                e_scr[slot, rows, c * LANES:(c + 1) * LANES] = blk
            r_scr[slot, rows, :] = 1.0 / _row_stat(jnp.sum, functools.reduce(jnp.add, e))
        o2 = _dot(e_scr[slot], vwin(r0)) * r_scr[slot]
        o = jnp.where(lo, o2[:B_STEP_ROWS], o2[B_STEP_ROWS:])
        o_ref[stream, pl.ds(r0, B_STEP_ROWS), :] = o.astype(BF16)

    _run_steps(make_step, q_ref.shape[0], nq // B_STEP_ROWS,
               -(-ctx // B_STEP_ROWS) if pad_front else 0, B_STEP_ROWS)


def _attn_b(q, k, v, ext, ctx, pad_front):
    nb, nq, _ = q.shape
    nk = k.shape[1]
    assert nq % B_STEP_ROWS == 0 and nk == (nq if pad_front else ctx + nq)
    bb = _streams_per_block(nb, pad_front)
    blk = lambda r: pl.BlockSpec((bb, r, LANES), lambda b, p: (b, 0, p))
    rows, win = 2 * B_STEP_ROWS, ctx + B_STEP_ROWS
    scratch = [pltpu.VMEM((B_HEADS // 2, rows, win), F32),
               pltpu.VMEM((PAIR_SLOTS, rows, win), F32), pltpu.VMEM((PAIR_SLOTS, rows, win), BF16),
               pltpu.VMEM((PAIR_SLOTS, rows, LANES), F32)]
    if pad_front:
        scratch += [pltpu.VMEM((ctx + nq, LANES), BF16)] * 2
    return pl.pallas_call(
        functools.partial(_attn_b_kernel, nq=nq, ctx=ctx, pad_front=pad_front),
        grid=(nb // bb, B_HEADS // 2),
        in_specs=[blk(nq), blk(nk), blk(nk),
                  pl.BlockSpec((1,) + ext.shape[1:], lambda b, p: (p, 0, 0))],
        out_specs=blk(nq),
        out_shape=jax.ShapeDtypeStruct((nb, nq, B_QKV), BF16),
        scratch_shapes=scratch,
        compiler_params=_cparams("arbitrary", "arbitrary"),
        name="attn_b",
    )(q, k, v, ext)


def _t5_bucket(rel):
    nb = T5_BUCKETS // 2
    max_exact = nb // 2
    n = -rel
    ret = jnp.where(n < 0, nb, 0)
    n = jnp.abs(n)
    nf = jnp.maximum(n, 1).astype(F32)
    large = max_exact + (jnp.log(nf / max_exact) / math.log(T5_MAX_DIST / max_exact)
                         * (nb - max_exact)).astype(jnp.int32)
    large = jnp.minimum(large, nb - 1)
    return ret + jnp.where(n < max_exact, n, large)


def _band_values(ctx, n_rows, value_of_rel):
    n_cols = ctx + n_rows
    period = -(-(n_cols + n_rows - 1) // LANES) * LANES
    d = jnp.concatenate([jnp.arange(0, period - (n_rows - 1)), jnp.arange(-(n_rows - 1), 0)])
    return value_of_rel(d - ctx)


def _bias_a(t5_table):
    return _band_values(WINDOW, PAIR_ROWS, lambda rel: t5_table.astype(F32)[_t5_bucket(rel)].T)


def _bias_b(b_rel):
    ext = _band_values(B_REACH, B_STEP_ROWS,
                       lambda rel: b_rel.astype(F32)[:, jnp.clip(rel, -B_MAX_REL, B_MAX_REL) + B_MAX_REL])
    return ext.reshape(B_HEADS // 2, 2, ext.shape[-1])


ODD_IN_PAD = 2048
ROPE_HALF = D_ROPE // 2
X1_LO = LANES // 2
NOPE_SPLIT = X1_LO - ROPE_HALF


def _head_lane_source():
    zero = D_NOPE + D_ROPE
    src = []
    for lane in range(LANES):
        if lane < ROPE_HALF:
            src.append(D_NOPE + ROPE_HALF + lane)
        elif lane < X1_LO:
            src.append(lane - ROPE_HALF)
        elif lane < X1_LO + ROPE_HALF:
            src.append(D_NOPE + lane - X1_LO)
        elif lane < X1_LO + ROPE_HALF + D_NOPE - NOPE_SPLIT:
            src.append(NOPE_SPLIT + lane - X1_LO - ROPE_HALF)
        else:
            src.append(zero)
    return src


def _to_head_lanes(a):
    padded = jnp.concatenate([a, jnp.zeros(a.shape[:-1] + (1,), a.dtype)], axis=-1)
    return padded[..., jnp.asarray(_head_lane_source(), dtype=jnp.int32)]


def _lane_masks(shape):
    lane = lax.broadcasted_iota(jnp.int32, shape, 1)
    rope = (lane < ROPE_HALF) | ((lane >= X1_LO) & (lane < X1_LO + ROPE_HALF))
    used = lane < X1_LO + ROPE_HALF + D_NOPE - NOPE_SPLIT
    return used & ~rope, rope


def _rope(y, cosf, sinf):
    return y * cosf + pltpu.roll(y, LANES // 2, 1) * sinf


ROW_BLOCK = 128


def _row_blocks(tm):
    rb = min(ROW_BLOCK, tm)
    return [slice(r, r + rb) for r in range(0, tm, rb)]


def _segment_mean_squares(src_scr, cols, seg_ref, sq_scr, ms_scr, slot):
    for rows in _row_blocks(src_scr.shape[0]):
        v = src_scr[rows, cols]
        sq_scr[slot, rows, :] = (v * v).astype(BF16)
    ms_scr[slot] = _dot(sq_scr[slot], seg_ref[...])


def _expand_kv(ckvb_ref, kpe_blk_ref, wkv_ref, kgain_ref, kseg_ref, kv_scr, sq_scr, ms_scr, k_ref, v_ref,
               between=None):
    k_cols = D_HEADS * LANES
    tm = kv_scr.shape[0]
    for grp in range(kv_scr.shape[1] // MXU_COLS):
        if between and grp in between:
            between[grp]()
        cols = slice(grp * MXU_COLS, (grp + 1) * MXU_COLS)
        kv_scr[:, cols] = _dot(ckvb_ref[...], wkv_ref[:, cols])
        if cols.start < k_cols:
            slot = grp % 2
            _segment_mean_squares(kv_scr, cols, kseg_ref, sq_scr, ms_scr, slot)
        for rows in _row_blocks(tm):
            if cols.start < k_cols:
                for half in range(2):
                    h = 2 * grp + half
                    kb = kv_scr[rows, h * LANES:(h + 1) * LANES]
                    ms = ms_scr[slot, rows, half * LANES:(half + 1) * LANES]
                    kn = (kb * lax.rsqrt(ms + EPS)) * kgain_ref[...]
                    k_ref[rows, h * LANES:(h + 1) * LANES] = (kn + kpe_blk_ref[rows, :]).astype(BF16)
            else:
                v_ref[rows, cols.start - k_cols:cols.stop - k_cols] = kv_scr[rows, cols].astype(BF16)


def _odd_inproj_kernel(x_ref, g_ref, w_ref, cinit_ref, convw_ref, qan_ref, wqb_ref, qgain_ref,
                       kvan_ref, krgain_ref, invf_ref, sgn_ref, wkv_ref, kgain_ref, qseg_ref, kseg_ref,
                       yc_ref, q_ref, k_ref, v_ref, ckv_ref, kpe_ref, cs_ref,
                       h_scr, z_scr, uext_scr, qn_scr, q_scr, kv_scr, rot_scr, ckvb_scr, kpe_scr, sq_scr, ms_scr,
                       *, tm, tpb, pos0):
    i = pl.program_id(0)
    tile = i % tpb
    off = pl.multiple_of(tile * tm, tm)
    blocks = _row_blocks(tm)
    rb = blocks[0].stop

    @pl.when(tile == 0)
    def _():
        uext_scr[0:8, :] = cinit_ref[0]

    @pl.when(i < tpb)
    def _():
        row = lax.broadcasted_iota(jnp.int32, (tm, LANES), 0)
        ang = (row + (tile * tm + pos0)).astype(F32) * invf_ref[...]
        rot_scr[0, pl.ds(off, tm), :] = jnp.cos(ang)
        rot_scr[1, pl.ds(off, tm), :] = jnp.sin(ang) * sgn_ref[...]

    h_scr[...] = _rms(x_ref[...], g_ref[...]).astype(BF16)
    low_lanes = lax.broadcasted_iota(jnp.int32, (rb, LANES), 1) < ROPE_HALF

    def proj(c0, c1):
        z_scr[:, c0:c1] = _dot(h_scr[...], w_ref[:, c0:c1])

    def rotary(y, rows):
        cosf = rot_scr[0, pl.ds(off + rows.start, rb), :]
        sinf = rot_scr[1, pl.ds(off + rows.start, rb), :]
        return _rope(y, cosf, sinf)

    def conv_group(grp):
        c0 = grp * MXU_COLS
        cols = slice(c0, c0 + MXU_COLS)
        for base in (0, C_WIDTH, 2 * C_WIDTH):
            proj(base + c0, base + c0 + MXU_COLS)
        for rows in blocks:
            up = slice(rows.start + 8, rows.stop + 8)
            uext_scr[up, cols] = (z_scr[rows, C_WIDTH + c0:C_WIDTH + c0 + MXU_COLS]
                                  * z_scr[rows, 2 * C_WIDTH + c0:2 * C_WIDTH + c0 + MXU_COLS])
        for rows in blocks:
            yconv = convw_ref[0:1, cols] * uext_scr[rows.start + 6:rows.stop + 6, cols]
            yconv = yconv + convw_ref[1:2, cols] * uext_scr[rows.start + 7:rows.stop + 7, cols]
            yconv = yconv + convw_ref[2:3, cols] * uext_scr[rows.start + 8:rows.stop + 8, cols]
            yc_ref[rows, cols] = (z_scr[rows, cols] * yconv).astype(BF16)

    proj(1536, 1792)
    proj(1792, 2048)

    for rows in blocks:
        qn_scr[rows, :] = _rms(z_scr[rows, 1536:1792], qan_ref[...]).astype(BF16)
        ckv = _rms(z_scr[rows, 1792:1920], kvan_ref[...])
        ckv_ref[rows, :] = ckv
        ckvb_scr[rows, :] = ckv.astype(BF16)

    for grp in range(D_HEADS * LANES // MXU_COLS):
        cols = slice(grp * MXU_COLS, (grp + 1) * MXU_COLS)
        q_scr[:, cols] = _dot(qn_scr[...], wqb_ref[:, cols])
        slot = grp % 2
        _segment_mean_squares(q_scr, cols, qseg_ref, sq_scr, ms_scr, slot)
        for half in range(2):
            h = 2 * grp + half
            for rows in blocks:
                blk = q_scr[rows, h * LANES:(h + 1) * LANES]
                inv = lax.rsqrt(ms_scr[slot, rows, half * LANES:(half + 1) * LANES] + EPS)
                y = rotary((blk * inv) * qgain_ref[...], rows)
                q_ref[rows, h * LANES:(h + 1) * LANES] = y.astype(BF16)

    for rows in blocks:
        kb = z_scr[rows, 1920:2048]
        ms = jnp.sum(kb * kb, axis=-1, keepdims=True) * (1.0 / D_ROPE)
        kpe_blk = rotary((kb * lax.rsqrt(ms + EPS)) * krgain_ref[...], rows)
        kpe_scr[rows, :] = kpe_blk
        x1_then_x2 = jnp.where(low_lanes, pltpu.roll(kpe_blk, LANES - X1_LO, 1), pltpu.roll(kpe_blk, ROPE_HALF, 1))
        kpe_ref[rows, :] = x1_then_x2[:, 0:D_ROPE]
    _expand_kv(ckvb_scr, kpe_scr, wkv_ref, kgain_ref, kseg_ref, kv_scr, sq_scr, ms_scr, k_ref, v_ref,
               between={0: lambda: conv_group(0), 4: lambda: conv_group(1)})

    cs_ref[0] = uext_scr[tm + 6:tm + 8, :]
    uext_scr[0:8, :] = uext_scr[tm:tm + 8, :]


def _odd_inproj(x, norm_g, ow, conv_init, n_batch, pos0):
    t = x.shape[0]
    tm = min(TOKEN_TILE, t // n_batch)
    tpb = t // n_batch // tm
    assert tpb * tm * n_batch == t
    row = lambda w: pl.BlockSpec((tm, w), lambda i: (i, 0))
    per_b = lambda r, w: pl.BlockSpec((1, r, w), lambda i: (i // tpb, 0, 0))
    small = [ow["conv_w"], ow["qan"], ow["wqb"], ow["qgain"], ow["kvan"], ow["krgain"],
             ow["invf"], ow["sgn"], ow["wkv"], ow["kgain"], ow["qseg"], ow["kseg"]]
    kv_w = D_HEADS * LANES
    v_w = D_HEADS * D_V
    return pl.pallas_call(
        functools.partial(_odd_inproj_kernel, tm=tm, tpb=tpb, pos0=pos0),
        grid=(t // tm,),
        in_specs=[row(D_MODEL), _resident((1, D_MODEL)), _resident(ow["w_in"].shape), per_b(8, C_WIDTH)]
                 + [_resident(a.shape) for a in small],
        out_specs=[row(C_WIDTH), row(kv_w), row(kv_w), row(v_w), row(D_KV_LORA), row(D_ROPE),
                   per_b(CONV_W - 1, C_WIDTH)],
        out_shape=[jax.ShapeDtypeStruct((t, C_WIDTH), BF16), jax.ShapeDtypeStruct((t, kv_w), BF16),
                   jax.ShapeDtypeStruct((t, kv_w), BF16), jax.ShapeDtypeStruct((t, v_w), BF16),
                   jax.ShapeDtypeStruct((t, D_KV_LORA), F32), jax.ShapeDtypeStruct((t, D_ROPE), F32),
                   jax.ShapeDtypeStruct((n_batch, CONV_W - 1, C_WIDTH), F32)],
        scratch_shapes=[pltpu.VMEM((tm, D_MODEL), BF16), pltpu.VMEM((tm, ODD_IN_PAD), F32),
                        pltpu.VMEM((tm + 8, C_WIDTH), F32), pltpu.VMEM((tm, D_Q_LORA), BF16),
                        pltpu.VMEM((tm, kv_w), F32), pltpu.VMEM((tm, kv_w + v_w), F32),
                        pltpu.VMEM((2, tpb * tm, LANES), F32), pltpu.VMEM((tm, D_KV_LORA), BF16),
                        pltpu.VMEM((tm, LANES), F32), pltpu.VMEM((2, tm, MXU_COLS), BF16),
                        pltpu.VMEM((2, tm, MXU_COLS), F32)],
        compiler_params=_cparams("arbitrary"),
        name="odd_inproj",
    )(x, norm_g.reshape(1, D_MODEL), ow["w_in"], conv_init, *small)


def _odd_weights(w_in, conv_w, q_a_norm, w_q_b, kv_a_norm, w_kv_b, qn_nope, qn_rope, kn_nope, kn_rope):
    z = lambda n: jnp.zeros((n,), F32)
    qk = D_NOPE + D_ROPE
    w_in = _to_bf16(w_in)
    kr_blk = _to_head_lanes(jnp.concatenate([jnp.zeros((D_MODEL, D_NOPE), BF16), w_in[:, 1920:]], axis=1))
    w_pad = jnp.concatenate([w_in[:, :1920], kr_blk], axis=1)
    wqb = _to_head_lanes(w_q_b.astype(BF16).reshape(D_Q_LORA, D_HEADS, qk)).reshape(D_Q_LORA, D_HEADS * LANES)
    kvb = w_kv_b.astype(BF16).reshape(D_KV_LORA, D_HEADS, D_NOPE + D_V)
    wk = _to_head_lanes(jnp.concatenate([kvb[:, :, :D_NOPE], jnp.zeros((D_KV_LORA, D_HEADS, D_ROPE), BF16)], axis=-1))
    wkv = jnp.concatenate([wk.reshape(D_KV_LORA, D_HEADS * LANES),
                           kvb[:, :, D_NOPE:].reshape(D_KV_LORA, D_HEADS * D_V)], axis=1)
    inv = 1.0 / (ROPE_THETA ** (jnp.arange(ROPE_HALF, dtype=F32) / ROPE_HALF))
    ones = jnp.ones((ROPE_HALF,), F32)
    lanes = lambda v: _to_head_lanes(v).reshape(1, LANES)
    in_nope = lanes(jnp.concatenate([jnp.ones((D_NOPE,), F32), z(D_ROPE)]))
    in_rope = lanes(jnp.concatenate([z(D_NOPE), jnp.ones((D_ROPE,), F32)]))
    seg_nope = in_nope.T * in_nope * (1.0 / D_NOPE)
    seg_rope = in_rope.T * in_rope * (1.0 / D_ROPE)
    two_heads = lambda m: jnp.kron(jnp.eye(MXU_COLS // LANES, dtype=F32), m).astype(BF16)
    return {
        "qseg": two_heads(seg_nope + seg_rope),
        "kseg": two_heads(seg_nope),
        "wk_dense": kvb[:, :, :D_NOPE].reshape(D_KV_LORA, D_HEADS * D_NOPE),
        "kseg_t": jnp.pad(jnp.kron(jnp.eye(D_HEADS, dtype=F32), jnp.full((1, D_NOPE), 1.0 / D_NOPE, F32)),
                          ((0, 16 - D_HEADS), (0, 0))).astype(BF16),
        "q_absorb": _to_head_lanes(jnp.concatenate(
            [w_kv_b.reshape(D_KV_LORA, D_HEADS, D_NOPE + D_V)[:, :, :D_NOPE] * kn_nope,
             jnp.zeros((D_KV_LORA, D_HEADS, D_ROPE), F32)], axis=-1)).transpose(1, 2, 0).astype(BF16),
        "wv_lanes": jnp.stack([jnp.pad(kvb[:, h, D_NOPE:], ((0, 0), ((h % 2) * D_V, LANES - D_V - (h % 2) * D_V)))
                               for h in range(D_HEADS)]),
        "w_in": w_pad,
        "conv_w": conv_w.astype(F32),
        "qan": q_a_norm.reshape(1, D_Q_LORA),
        "wqb": wqb,
        "qgain": lanes(jnp.concatenate([qn_nope, qn_rope])) * MLA_QSCALE,
        "kvan": kv_a_norm.reshape(1, D_KV_LORA),
        "krgain": lanes(jnp.concatenate([z(D_NOPE), kn_rope])),
        "invf": lanes(jnp.concatenate([z(D_NOPE), inv, inv])),
        "sgn": lanes(jnp.concatenate([z(D_NOPE), -ones, ones])),
        "wkv": wkv,
        "kgain": lanes(jnp.concatenate([kn_nope, z(D_ROPE)])),
        "place": _to_head_lanes(jnp.concatenate([jnp.zeros((D_ROPE, D_NOPE), BF16), jnp.eye(D_ROPE, dtype=BF16)], axis=1)),
    }


def _mla_prompt_kernel(q_ref, k_ref, v_ref, o_ref, s_scr, e_scr, r_scr, *, seq):
    qb = min(MLA_QBLOCK, seq)
    hb = qb // 2
    lo = lax.broadcasted_iota(jnp.int32, (qb, LANES), 1) < D_V
    first_half = lax.broadcasted_iota(jnp.int32, (CHUNK, LANES), 1) < CHUNK
    scores = lambda q, k: lax.dot_general(q, k, _NT, preferred_element_type=F32)
    for i in range(seq // qb):
        tk = qb * (i + 1)
        t0, t1 = tk - qb, tk - hb
        top, bot = slice(0, hb), slice(hb, qb)
        outs = []
        for h in range(2):
            hl = slice(h * LANES, (h + 1) * LANES)
            q0 = i * qb
            if t0 > 0:
                s_scr[h, :, 0:t0] = scores(q_ref[0, q0:q0 + qb, hl], k_ref[0, 0:t0, hl])
            s_scr[h, top, t0:t1] = scores(q_ref[0, q0:q0 + hb, hl], k_ref[0, t0:t1, hl])
            s_scr[h, bot, t0:tk] = scores(q_ref[0, q0 + hb:q0 + qb, hl], k_ref[0, t0:tk, hl])
            for rb in range(qb // CHUNK):
                rows = slice(rb * CHUNK, (rb + 1) * CHUNK)
                visible = t0 + CHUNK * (rb + 1)
                width = t1 if rb < hb // CHUNK else tk
                n_full, ragged = visible // LANES, visible % LANES != 0
                blk = lambda c: s_scr[h, rows, c * LANES:(c + 1) * LANES]
                cols = [blk(c) for c in range(n_full)]
                if ragged:
                    cols.append(jnp.where(first_half, blk(n_full), NEG_INF))
                mm = cols[0]
                for c in cols[1:]:
                    mm = jnp.maximum(mm, c)
                m = _row_stat(jnp.max, mm)
                acc = None
                for c in range(width // LANES):
                    if c < len(cols):
                        e = jnp.exp2(blk(c) - m)
                        if c >= n_full:
                            e = jnp.where(first_half, e, 0.0)
                        acc = e if acc is None else acc + e
                    else:
                        e = jnp.zeros((CHUNK, LANES), F32)
                    e_scr[h, rows, c * LANES:(c + 1) * LANES] = e.astype(BF16)
                r_scr[h, rows, :] = 1.0 / _row_stat(jnp.sum, acc)
            pv = jnp.concatenate([_dot(e_scr[h, top, t0:t1], v_ref[0, t0:t1, :]),
                                  _dot(e_scr[h, bot, t0:tk], v_ref[0, t0:tk, :])], axis=0)
            if t0 > 0:
                pv = pv + _dot(e_scr[h, :, 0:t0], v_ref[0, 0:t0, :])
            outs.append(pv * r_scr[h])
        o_ref[0, i * qb:(i + 1) * qb, :] = jnp.where(lo, outs[0], outs[1]).astype(BF16)


def _mla_prompt(q, k, v):
    nb, seq, _ = q.shape
    qb = min(MLA_QBLOCK, seq)
    return pl.pallas_call(
        functools.partial(_mla_prompt_kernel, seq=seq),
        grid=(nb, D_HEADS // 2),
        in_specs=[pl.BlockSpec((1, seq, 2 * LANES), lambda b, p: (b, 0, p)),
                  pl.BlockSpec((1, seq, 2 * LANES), lambda b, p: (b, 0, p)),
                  pl.BlockSpec((1, seq, LANES), lambda b, p: (b, 0, p))],
        out_specs=pl.BlockSpec((1, seq, LANES), lambda b, p: (b, 0, p)),
        out_shape=jax.ShapeDtypeStruct((nb, seq, D_HEADS * D_V), BF16),
        scratch_shapes=[pltpu.VMEM((2, qb, seq), F32), pltpu.VMEM((2, qb, seq), BF16),
                        pltpu.VMEM((2, qb, LANES), F32)],
        compiler_params=_cparams("arbitrary", "arbitrary"),
        name="mla_prompt",
    )(q, k, v)


MLA_CACHE_ROWS = 512


def _mla_sample_kernel(q_ref, ckv_ref, kpe_ref, kn_ref, vn_ref, place_ref, wk_ref, kseg_ref, qabs_ref, wv_ref,
                       o_ref, ckvb_scr, kpeb_scr, sq_scr, s1_scr, s2_scr, e_scr):
    nq, past = q_ref.shape[1], ckv_ref.shape[1]
    lane_blocks = past // LANES
    lo = lax.broadcasted_iota(jnp.int32, (nq, LANES), 1) < D_V
    head = lambda h: slice(h * LANES, (h + 1) * LANES)

    for r0 in range(0, past, MLA_CACHE_ROWS):
        rows = slice(r0, r0 + MLA_CACHE_ROWS)
        c = ckv_ref[0, rows, :].astype(BF16)
        ckvb_scr[rows, :] = c
        kpeb_scr[rows, :] = _dot(kpe_ref[0, rows, :].astype(BF16), place_ref[...]).astype(BF16)
        kraw = _dot(c, wk_ref[...])
        sq_scr[rows, :] = (kraw * kraw).astype(BF16)
    r_t = lax.rsqrt(lax.dot_general(kseg_ref[...], sq_scr[...], _NT, preferred_element_type=F32) + EPS)

    q_abs = jnp.concatenate([_dot(q_ref[0, :, head(h)], qabs_ref[h]).astype(BF16) for h in range(D_HEADS)], axis=0)
    q_all = jnp.concatenate([q_ref[0, :, head(h)] for h in range(D_HEADS)], axis=0)
    s1_scr[...] = lax.dot_general(q_abs, ckvb_scr[...], _NT, preferred_element_type=F32)
    s2_scr[...] = lax.dot_general(q_all, kpeb_scr[...], _NT, preferred_element_type=F32)

    inv_den, e_new = [], []
    for h in range(D_HEADS):
        rows = slice(h * nq, (h + 1) * nq)
        s_new = lax.dot_general(q_ref[0, :, head(h)], kn_ref[0, :, head(h)], _NT, preferred_element_type=F32)
        mm = None
        for c in range(lane_blocks):
            cl = slice(c * LANES, (c + 1) * LANES)
            blk = s1_scr[rows, cl] * r_t[h:h + 1, cl] + s2_scr[rows, cl]
            s1_scr[rows, cl] = blk
            mm = blk if mm is None else jnp.maximum(mm, blk)
        m = jnp.maximum(jnp.max(mm, axis=-1, keepdims=True), jnp.max(s_new, axis=-1, keepdims=True))
        acc = None
        for c in range(lane_blocks):
            cl = slice(c * LANES, (c + 1) * LANES)
            e = jnp.exp2(s1_scr[rows, cl] - m)
            acc = e if acc is None else acc + e
            e_scr[rows, cl] = e.astype(BF16)
        en = jnp.exp2(s_new - m)
        inv_den.append(1.0 / (jnp.sum(acc, axis=-1, keepdims=True) + jnp.sum(en, axis=-1, keepdims=True)))
        e_new.append(en.astype(BF16))

    latent = _dot(e_scr[...], ckvb_scr[...])
    for p in range(D_HEADS // 2):
        halves = []
        for half in range(2):
            h = 2 * p + half
            rows = slice(h * nq, (h + 1) * nq)
            o = _dot(latent[rows].astype(BF16), wv_ref[h]) + _dot(e_new[h], vn_ref[0, :, head(p)])
            halves.append(o * inv_den[h])
        o_ref[0, :, head(p)] = jnp.where(lo, halves[0], halves[1]).astype(BF16)


def _mla_sample(q, ckv, kpe, kn, vn, ow):
    nb, nq, _ = q.shape
    past = ckv.shape[1]
    assert past % MLA_CACHE_ROWS == 0
    per_b = lambda r, w: pl.BlockSpec((1, r, w), lambda b: (b, 0, 0))
    consts = [ow["place"], ow["wk_dense"], ow["kseg_t"], ow["q_absorb"], ow["wv_lanes"]]
    rows = D_HEADS * nq
    return pl.pallas_call(
        _mla_sample_kernel,
        grid=(nb,),
        in_specs=[per_b(nq, D_HEADS * LANES), per_b(past, D_KV_LORA), per_b(past, D_ROPE),
                  per_b(nq, D_HEADS * LANES), per_b(nq, D_HEADS * D_V)] + [_resident(a.shape) for a in consts],
        out_specs=per_b(nq, D_HEADS * D_V),
        out_shape=jax.ShapeDtypeStruct((nb, nq, D_HEADS * D_V), BF16),
        scratch_shapes=[pltpu.VMEM((past, D_KV_LORA), BF16), pltpu.VMEM((past, LANES), BF16),
                        pltpu.VMEM((past, D_HEADS * D_NOPE), BF16), pltpu.VMEM((rows, past), F32),
                        pltpu.VMEM((rows, past), F32), pltpu.VMEM((rows, past), BF16)],
        compiler_params=_cparams("arbitrary"),
        name="mla_sample",
    )(q, ckv, kpe, kn, vn, *consts)


def _even_layer(xp, xs, nb, seq, ndb, dseq, norm_g, w_in, w_out, a_qn, a_kn, a_sinks, b_qn, b_kn, b_rel,
                t5_table, ck_a, cv_a, ck_b, cv_b, ffn2):
    w_in = _to_bf16(w_in)
    w_out = _to_bf16(w_out)
    w_in_p = jnp.concatenate([_pair_heads(w_in[:, :A_Q], 1), w_in[:, A_Q:]], axis=1)
    woa = _pair_heads(w_out[:A_Q], 0)
    wob = w_out[A_Q:]
    ones = lambda n: jnp.ones((n,), F32)
    scale = HEAD_DIM ** -0.5 * LOG2E
    gain_row = jnp.concatenate([jnp.tile(a_qn, A_HEADS) * scale, jnp.tile(a_kn, A_KV_HEADS), ones(A_KV),
                                jnp.tile(b_qn, B_HEADS) * scale, jnp.tile(b_kn, B_HEADS), ones(B_QKV)]
                               ).reshape(1, EVEN_IN).astype(F32)
    bias_a = _bias_a(t5_table) * LOG2E
    bias_b = _bias_b(b_rel) * LOG2E
    sink = jnp.broadcast_to(jnp.repeat(a_sinks.astype(F32) * LOG2E, PAIR_ROWS)[:, None],
                            (A_HEADS * PAIR_ROWS, LANES))
    la, lb = min(WINDOW, seq), min(B_REACH, seq)

    aq, ak, av, bq, bk, bv, cak, cav, cbk, cbv = _even_inproj(xp, norm_g, w_in_p, gain_row, nb, la, lb)
    r3 = lambda a: a.reshape(nb, seq, a.shape[-1])
    ya = _attn_a(r3(aq), r3(ak), r3(av), bias_a, sink, WINDOW, True)
    yb = _attn_b(r3(bq), r3(bk), r3(bv), bias_b, B_REACH, True)
    xp = _ffn(xp, *ffn2, out_proj=(ya.reshape(nb * seq, A_Q), yb.reshape(nb * seq, B_QKV), woa, wob))
    st_p = (cak.reshape(nb, la, A_KV_HEADS, HEAD_DIM), cav.reshape(nb, la, A_KV_HEADS, HEAD_DIM),
            cbk.reshape(nb, lb, B_HEADS, HEAD_DIM), cbv.reshape(nb, lb, B_HEADS, HEAD_DIM))

    ts = ndb * dseq
    aq, ak, av, bq, bk, bv, nak, nav, nbk, nbv = _even_inproj(xs, norm_g, w_in_p, gain_row, 1, ts, ts)
    pad_q = lambda a, rows: jnp.pad(a.reshape(ndb, dseq, a.shape[-1]), ((0, 0), (0, rows - dseq), (0, 0)))

    def window(cache, new, rows):
        w = cache.shape[-2] * cache.shape[-1]
        full = jnp.concatenate([cache.reshape(ndb, -1, w), new.reshape(ndb, dseq, w)], axis=1)
        buf = jnp.pad(full, ((0, 0), (0, rows - dseq), (0, 0))).astype(BF16)
        return full[:, dseq:].reshape(cache.shape), buf

    st_ak, kbuf_a = window(ck_a, nak, PAIR_ROWS)
    st_av, vbuf_a = window(cv_a, nav, PAIR_ROWS)
    st_bk, kbuf_b = window(ck_b, nbk, B_STEP_ROWS)
    st_bv, vbuf_b = window(cv_b, nbv, B_STEP_ROWS)
    ya = _attn_a(pad_q(aq, PAIR_ROWS), kbuf_a, vbuf_a, bias_a, sink, WINDOW, False)[:, :dseq]
    yb = _attn_b(pad_q(bq, B_STEP_ROWS), kbuf_b, vbuf_b, bias_b, B_REACH, False)[:, :dseq]
    xs = _ffn(xs, *ffn2, out_proj=(ya.reshape(ts, A_Q), yb.reshape(ts, B_QKV), woa, wob))
    return xp, xs, st_p, (st_ak, st_av, st_bk, st_bv)


def _odd_layer(xp, xs, nb, seq, ndb, dseq, past, norm_g, ow, w_out, conv_prev, c_ckv, c_kpe, ffn2):
    w_out = _to_bf16(w_out)
    woc = w_out[:C_WIDTH]
    wod = w_out[C_WIDTH:]

    zero_init = jnp.zeros((nb, 8, C_WIDTH), F32)
    yc, q, k, v, ckv, kpe, cs = _odd_inproj(xp, norm_g, ow, zero_init, nb, 0)
    r3 = lambda a: a.reshape(nb, seq, a.shape[-1])
    yd = _mla_prompt(r3(q), r3(k), r3(v))
    xp = _ffn(xp, *ffn2, out_proj=(yc, yd.reshape(nb * seq, D_HEADS * D_V), woc, wod))
    st_p = (cs, ckv.reshape(nb, seq, D_KV_LORA), kpe.reshape(nb, seq, D_ROPE))

    ts = ndb * dseq
    init = jnp.pad(conv_prev.astype(F32), ((0, 0), (8 - (CONV_W - 1), 0), (0, 0)))
    yc, q, kn, vn, ckv, kpe, cs = _odd_inproj(xs, norm_g, ow, init, ndb, past)
    s3 = lambda a: a.reshape(ndb, dseq, a.shape[-1])
    yd = _mla_sample(s3(q), c_ckv, c_kpe, s3(kn), s3(vn), ow)
    xs = _ffn(xs, *ffn2, out_proj=(yc, yd.reshape(ts, D_HEADS * D_V), woc, wod))
    st_s = (cs, ckv.reshape(ndb, dseq, D_KV_LORA), kpe.reshape(ndb, dseq, D_ROPE))
    return xp, xs, st_p, st_s


def kernel(x_prompt, x_sample, cache_a_k, cache_a_v, cache_b_k, cache_b_v, state_c_conv, cache_d_ckv, cache_d_kpe, ff1_norm, ff1_w_gu, ff1_w_down, mix_norm, ff2_norm, ff2_w_gu, ff2_w_down, t5_bias_table, ev_w_in, ev_w_out, a_q_norm, a_k_norm, a_sinks, b_q_norm, b_k_norm, b_rel_bias, od_w_in, od_w_out, c_conv_w, d_q_a_norm, d_w_q_b, d_kv_a_norm, d_w_kv_b, d_q_nope_norm, d_q_rope_norm, d_k_nope_norm, d_k_rope_norm):
    nb, seq, _ = x_prompt.shape
    ndb, dseq, _ = x_sample.shape
    past = cache_d_ckv.shape[2]
    depth = ff1_norm.shape[0]
    assert seq % TOKEN_TILE == 0 and dseq == CHUNK and past % CHUNK == 0
    xp = x_prompt.reshape(nb * seq, D_MODEL)
    xs = x_sample.reshape(ndb * dseq, D_MODEL)
    even_p, even_s, odd_p, odd_s = [], [], [], []
    for l in range(depth):
        i = l // 2
        ff1 = _ffn_weights(ff1_w_gu, ff1_w_down, l)
        ffn2 = (ff2_norm[l],) + _ffn_weights(ff2_w_gu, ff2_w_down, l)
        xp = _ffn(xp, ff1_norm[l], *ff1)
        xs = _ffn(xs, ff1_norm[l], *ff1)
        if l % 2 == 0:
            xp, xs, sp, ss = _even_layer(
                xp, xs, nb, seq, ndb, dseq, mix_norm[l], ev_w_in[i], ev_w_out[i], a_q_norm[i], a_k_norm[i],
                a_sinks[i], b_q_norm[i], b_k_norm[i], b_rel_bias[i], t5_bias_table,
                cache_a_k[i], cache_a_v[i], cache_b_k[i], cache_b_v[i], ffn2)
            even_p.append(sp)
            even_s.append(ss)
        else:
            ow = _odd_weights(od_w_in[i], c_conv_w[i], d_q_a_norm[i], d_w_q_b[i], d_kv_a_norm[i], d_w_kv_b[i],
                              d_q_nope_norm[i], d_q_rope_norm[i], d_k_nope_norm[i], d_k_rope_norm[i])
            xp, xs, sp, ss = _odd_layer(xp, xs, nb, seq, ndb, dseq, past, mix_norm[l], ow, od_w_out[i],
                                        state_c_conv[i], cache_d_ckv[i], cache_d_kpe[i], ffn2)
            odd_p.append(sp)
            odd_s.append(ss)
    stack = lambda group, j: group[0][j][None] if len(group) == 1 else jnp.stack([g[j] for g in group])
    return (xp.reshape(nb, seq, D_MODEL), xs.reshape(ndb, dseq, D_MODEL),
            stack(even_p, 0), stack(even_p, 1), stack(even_p, 2), stack(even_p, 3),
            stack(odd_p, 0), stack(odd_p, 1), stack(odd_p, 2),
            stack(even_s, 0), stack(even_s, 1), stack(even_s, 2), stack(even_s, 3),
            stack(odd_s, 0), stack(odd_s, 1), stack(odd_s, 2))
```

```python
import functools
import math

import jax
import jax.numpy as jnp
from jax import lax
from jax.experimental import pallas as pl
from jax.experimental.pallas import tpu as pltpu

F32 = jnp.float32
BF16 = jnp.bfloat16

D_MODEL = 1024
CHUNK = 64
HEAD_DIM = 64
EPS = 1e-6
A_HEADS = 8
A_KV_HEADS = 2
WINDOW = 128
T5_BUCKETS = 32
T5_MAX_DIST = 128
B_HEADS = 8
B_REACH = 512
B_MAX_REL = 128
C_WIDTH = 512
CONV_W = 3
D_HEADS = 8
D_Q_LORA = 256
D_KV_LORA = 128
D_NOPE = 64
D_ROPE = 32
D_V = 64
ROPE_THETA = 10000.0
FFN_DIM = 2816
A_Q = A_HEADS * HEAD_DIM
A_KV = A_KV_HEADS * HEAD_DIM
B_QKV = B_HEADS * HEAD_DIM
EVEN_IN = A_Q + 2 * A_KV + 3 * B_QKV

LANES = 128
TOKEN_TILE = 512
CAST_BLOCK_BYTES = 6 * 1024 * 1024
MXU_COLS = 256
FFN_CHUNK = MXU_COLS
PAIR_ROWS = 2 * CHUNK
MLA_QBLOCK = 512
VMEM_LIMIT_BYTES = 56 * 1024 * 1024
LOG2E = math.log2(math.e)
MLA_QSCALE = (D_NOPE + D_ROPE) ** -0.5 * LOG2E
NEG_INF = float("-inf")

_NT = (((1,), (1,)), ((), ()))


def _cparams(*sem):
    return pltpu.CompilerParams(dimension_semantics=sem, vmem_limit_bytes=VMEM_LIMIT_BYTES)


def _resident(shape):
    zeros = (0,) * len(shape)
    return pl.BlockSpec(shape, lambda *_: zeros, pipeline_mode=pl.Buffered(1))


def _rms(x, g):
    ms = jnp.mean(x * x, axis=-1, keepdims=True)
    return (x * lax.rsqrt(ms + EPS)) * g


def _dot(a, b):
    return jnp.dot(a, b, preferred_element_type=F32)


def _ffn_kernel(*refs, n_chunks, fused_out):
    if fused_out:
        (x_ref, ya_ref, yb_ref, woa_ref, wob_ref, g_ref, wgu_ref, wd_ref,
         o_ref, h_scr, a_scr) = refs
        y = _dot(ya_ref[...], woa_ref[...]) + _dot(yb_ref[...], wob_ref[...])
        o_ref[...] = x_ref[...] + y
        res_ref = o_ref
    else:
        x_ref, g_ref, wgu_ref, wd_ref, o_ref, h_scr, a_scr = refs
        res_ref = x_ref
    h_scr[...] = _rms(res_ref[...], g_ref[...]).astype(BF16)
    for j in range(n_chunks):
        h = h_scr[...]
        g = _dot(h, wgu_ref[:, j * FFN_CHUNK:(j + 1) * FFN_CHUNK])
        u = _dot(h, wgu_ref[:, FFN_DIM + j * FFN_CHUNK:FFN_DIM + (j + 1) * FFN_CHUNK])
        a_scr[:, j * FFN_CHUNK:(j + 1) * FFN_CHUNK] = ((g * jax.nn.sigmoid(g)) * u).astype(BF16)
    o_ref[...] = res_ref[...] + 0.5 * _dot(a_scr[...], wd_ref[...])


def _ffn(x, norm_g, wgu, wd, out_proj=None):
    t = x.shape[0]
    tm = min(TOKEN_TILE, t)
    n_chunks = FFN_DIM // FFN_CHUNK
    row = lambda w: pl.BlockSpec((tm, w), lambda i: (i, 0))
    in_specs = [row(D_MODEL)]
    args = [x]
    if out_proj is not None:
        ya, yb, woa, wob = out_proj
        in_specs += [row(ya.shape[1]), row(yb.shape[1]), _resident(woa.shape), _resident(wob.shape)]
        args += [ya, yb, woa, wob]
    in_specs += [_resident((1, D_MODEL)), _resident(wgu.shape), _resident(wd.shape)]
    args += [norm_g.reshape(1, D_MODEL), wgu, wd]
    return pl.pallas_call(
        functools.partial(_ffn_kernel, n_chunks=n_chunks, fused_out=out_proj is not None),
        grid=(t // tm,),
        in_specs=in_specs,
        out_specs=row(D_MODEL),
        out_shape=jax.ShapeDtypeStruct((t, D_MODEL), F32),
        scratch_shapes=[pltpu.VMEM((tm, D_MODEL), BF16), pltpu.VMEM((tm, FFN_DIM), BF16)],
        compiler_params=_cparams("arbitrary"),
        name="ffn_out" if out_proj is not None else "ffn",
    )(*args)


def _cast_kernel(x_ref, o_ref):
    o_ref[...] = x_ref[...].astype(BF16)


def _to_bf16(w, layer=None):
    r, c = w.shape[-2:]
    fits = [d for d in range(16, r + 1, 16) if r % d == 0 and d * c * 4 <= CAST_BLOCK_BYTES]
    tr = max(fits) if fits else r
    out_spec = pl.BlockSpec((tr, c), lambda i: (i, 0))
    in_spec = out_spec if layer is None else pl.BlockSpec((None, tr, c), lambda i: (layer, i, 0))
    return pl.pallas_call(
        _cast_kernel, grid=(r // tr,), in_specs=[in_spec], out_specs=out_spec,
        out_shape=jax.ShapeDtypeStruct((r, c), BF16),
        compiler_params=_cparams("arbitrary"), name="cast_bf16",
    )(w)


def _ffn_weights(w_gu, w_down, layer):
    return _to_bf16(w_gu, layer), _to_bf16(w_down, layer)


def _norm_halves(blk, gain):
    sq = blk * blk
    lo = lax.broadcasted_iota(jnp.int32, blk.shape, 1) < HEAD_DIM
    s_lo = jnp.sum(jnp.where(lo, sq, 0.0), axis=-1, keepdims=True)
    s_hi = jnp.sum(jnp.where(lo, 0.0, sq), axis=-1, keepdims=True)
    inv = jnp.where(lo, lax.rsqrt(s_lo * (1.0 / HEAD_DIM) + EPS), lax.rsqrt(s_hi * (1.0 / HEAD_DIM) + EPS))
    return (blk * inv) * gain


def _even_inproj_kernel(x_ref, g_ref, w_ref, gain_ref,
                        aq_ref, ak_ref, av_ref, bq_ref, bk_ref, bv_ref,
                        cak_ref, cav_ref, cbk_ref, cbv_ref, h_scr, z_scr, *, tm, tpb, rows_a, rows_b):
    i = pl.program_id(0)
    last = (i % tpb) == (tpb - 1)
    h_scr[...] = _rms(x_ref[...], g_ref[...]).astype(BF16)

    def lanes(c):
        return slice(c * LANES, (c + 1) * LANES)

    def emit(c):
        blk = z_scr[:, lanes(c)]
        gain = gain_ref[:, lanes(c)]
        if c < 4:
            aq_ref[:, lanes(c)] = _norm_halves(blk, gain).astype(BF16)
        elif c == 4:
            kn = _norm_halves(blk, gain)
            ak_ref[...] = kn.astype(BF16)
            z_scr[:, lanes(c)] = kn
        elif c == 5:
            av_ref[...] = blk.astype(BF16)
        elif c < 10:
            bq_ref[:, lanes(c - 6)] = _norm_halves(blk, gain).astype(BF16)
        elif c < 14:
            kn = _norm_halves(blk, gain)
            bk_ref[:, lanes(c - 10)] = kn.astype(BF16)
            z_scr[:, lanes(c)] = kn
        else:
            bv_ref[:, lanes(c - 14)] = blk.astype(BF16)

    for grp in range(EVEN_IN // MXU_COLS):
        cols = slice(grp * MXU_COLS, (grp + 1) * MXU_COLS)
        z_scr[:, cols] = _dot(h_scr[...], w_ref[:, cols])
        emit(2 * grp)
        emit(2 * grp + 1)

    @pl.when(last)
    def _():
        cak_ref[0] = z_scr[tm - rows_a:, lanes(4)]
        cav_ref[0] = z_scr[tm - rows_a:, lanes(5)]
        cbk_ref[0] = z_scr[tm - rows_b:, 10 * LANES:14 * LANES]
        cbv_ref[0] = z_scr[tm - rows_b:, 14 * LANES:18 * LANES]


def _even_inproj(x, norm_g, w_in, gain_row, n_batch, rows_a, rows_b):
    t = x.shape[0]
    tm = min(TOKEN_TILE, t)
    tpb = t // n_batch // tm
    assert tpb * tm * n_batch == t and rows_a <= tm and rows_b <= tm
    row = lambda w: pl.BlockSpec((tm, w), lambda i: (i, 0))
    cache = lambda r, w: pl.BlockSpec((1, r, w), lambda i: (i // tpb, 0, 0))
    bshape = lambda w: jax.ShapeDtypeStruct((t, w), BF16)
    cshape = lambda r, w: jax.ShapeDtypeStruct((n_batch, r, w), F32)
    return pl.pallas_call(
        functools.partial(_even_inproj_kernel, tm=tm, tpb=tpb, rows_a=rows_a, rows_b=rows_b),
        grid=(t // tm,),
        in_specs=[row(D_MODEL), _resident((1, D_MODEL)), _resident(w_in.shape), _resident(gain_row.shape)],
        out_specs=[row(A_Q), row(A_KV), row(A_KV), row(B_QKV), row(B_QKV), row(B_QKV),
                   cache(rows_a, A_KV), cache(rows_a, A_KV), cache(rows_b, B_QKV), cache(rows_b, B_QKV)],
        out_shape=[bshape(A_Q), bshape(A_KV), bshape(A_KV), bshape(B_QKV), bshape(B_QKV), bshape(B_QKV),
                   cshape(rows_a, A_KV), cshape(rows_a, A_KV), cshape(rows_b, B_QKV), cshape(rows_b, B_QKV)],
        scratch_shapes=[pltpu.VMEM((tm, D_MODEL), BF16), pltpu.VMEM((tm, EVEN_IN), F32)],
        compiler_params=_cparams("arbitrary"),
        name="even_inproj",
    )(x, norm_g.reshape(1, D_MODEL), w_in, gain_row)


def _pair_heads(a, axis):
    shape = a.shape
    split = shape[:axis] + (2, A_HEADS // 2, HEAD_DIM) + shape[axis + 1:]
    return a.reshape(split).swapaxes(axis, axis + 1).reshape(shape)


def _fill_padded(buf, src_ref, ctx):
    buf[0:ctx, :] = jnp.zeros((ctx, buf.shape[1]), buf.dtype)
    buf[ctx:, :] = src_ref[0]


def _windows(k_ref, v_ref, pad_scr, ctx, pad_front, step_rows, stream):
    win = ctx + step_rows
    if pad_front:
        kbuf, vbuf = pad_scr
        _fill_padded(kbuf, k_ref, ctx)
        _fill_padded(vbuf, v_ref, ctx)
        return (lambda r0: kbuf[pl.ds(r0, win), :]), (lambda r0: vbuf[pl.ds(r0, win), :])
    return (lambda r0: k_ref[stream, pl.ds(r0, win), :]), (lambda r0: v_ref[stream, pl.ds(r0, win), :])


def _fold_lanes(x, op):
    acc = x[:, :LANES]
    for c in range(1, x.shape[1] // LANES):
        acc = op(acc, x[:, c * LANES:(c + 1) * LANES])
    return acc


def _row_stat(reduce_fn, x):
    return jnp.broadcast_to(reduce_fn(x, axis=-1, keepdims=True), x.shape)


PAIR_SLOTS = 4


def _run_steps(make_stages, n_streams, n_steps, n_front, step_rows, lookahead):
    stages = [make_stages(stream) for stream in range(n_streams)]
    tasks = [(stream, t) for stream in range(n_streams) for t in range(n_steps)]

    def run(stage, k):
        stream, t = tasks[k]
        stages[stream][stage](t * step_rows, k % PAIR_SLOTS, t < n_front)

    if not lookahead:
        for k in range(len(tasks)):
            run(0, k)
            run(1, k)
        return
    run(0, 0)
    for k in range(len(tasks)):
        if k + 1 < len(tasks):
            run(0, k + 1)
        run(1, k)


def _band_bias(ext_row, n_rows, ctx):
    win = ctx + n_rows
    period = ext_row.shape[1]
    t = pltpu.roll(jnp.broadcast_to(ext_row, (n_rows, period)), 0, 1, stride=1, stride_axis=0)[:, :win]
    chunk = lax.broadcasted_iota(jnp.int32, (n_rows, win), 0) // CHUNK
    col = lax.broadcasted_iota(jnp.int32, (n_rows, win), 1)
    visible = (col >= CHUNK * chunk) & (col < ctx + CHUNK * (chunk + 1))
    return jnp.where(visible, t, NEG_INF)


def _attn_a_kernel(q_ref, k_ref, v_ref, ext_ref, sink_ref, o_ref, bias_ref, s_scr, e_scr, *pad_scr,
                   nq, ctx, pad_front):
    win = ctx + PAIR_ROWS
    lo = lax.broadcasted_iota(jnp.int32, (PAIR_ROWS, LANES), 1) < HEAD_DIM

    @pl.when(pl.program_id(0) == 0)
    def _():
        for head in range(A_HEADS):
            bias_ref[head * PAIR_ROWS:(head + 1) * PAIR_ROWS, :] = _band_bias(
                ext_ref[head:head + 1, :], PAIR_ROWS, ctx)

    def make_stages(stream):
        kwin, vwin = _windows(k_ref, v_ref, pad_scr, ctx, pad_front, PAIR_ROWS, stream)
        return functools.partial(scores, stream, kwin), functools.partial(finish, stream, vwin)

    def scores(stream, kwin, r0, slot, masked):
        q_lo, q_hi = [], []
        for p in range(4):
            qp = q_ref[stream, pl.ds(r0, PAIR_ROWS), p * LANES:(p + 1) * LANES]
            zero = jnp.zeros_like(qp)
            q_lo.append(jnp.where(lo, qp, zero))
            q_hi.append(jnp.where(lo, zero, qp))
        q2 = jnp.concatenate(q_lo + q_hi, axis=0)
        s_scr[slot] = lax.dot_general(q2, kwin(r0), _NT, preferred_element_type=F32)

    def finish(stream, vwin, r0, slot, masked):
        for head in range(A_HEADS):
            rows = slice(head * PAIR_ROWS, (head + 1) * PAIR_ROWS)
            s = s_scr[slot, rows, :] + bias_ref[rows, :]
            if masked:
                col = lax.broadcasted_iota(jnp.int32, (PAIR_ROWS, win), 1)
                s = jnp.where(col + r0 >= ctx, s, NEG_INF)
            sk = sink_ref[rows, :]
            m = jnp.maximum(_row_stat(jnp.max, _fold_lanes(s, jnp.maximum)), sk)
            e = [jnp.exp2(s[:, c * LANES:(c + 1) * LANES] - m) for c in range(win // LANES)]
            den = _row_stat(jnp.sum, functools.reduce(jnp.add, e)) + jnp.exp2(sk - m)
            inv = 1.0 / den
            for c in range(win // LANES):
                e_scr[slot, rows, c * LANES:(c + 1) * LANES] = (e[c] * inv).astype(BF16)
        o2 = _dot(e_scr[slot], vwin(r0))
        half = 4 * PAIR_ROWS
        for p in range(4):
            o = jnp.where(lo, o2[p * PAIR_ROWS:(p + 1) * PAIR_ROWS],
                          o2[half + p * PAIR_ROWS:half + (p + 1) * PAIR_ROWS])
            o_ref[stream, pl.ds(r0, PAIR_ROWS), p * LANES:(p + 1) * LANES] = o.astype(BF16)

    _run_steps(make_stages, q_ref.shape[0], nq // PAIR_ROWS, ctx // PAIR_ROWS if pad_front else 0, PAIR_ROWS,
               lookahead=False)


def _streams_per_block(nb, pad_front):
    return 1 if pad_front else nb


def _attn_a(q, k, v, ext, sink, ctx, pad_front):
    nb, nq, _ = q.shape
    nk = k.shape[1]
    assert nq % PAIR_ROWS == 0 and nk == (nq if pad_front else ctx + nq)
    bb = _streams_per_block(nb, pad_front)
    per_b = lambda r, w: pl.BlockSpec((bb, r, w), lambda b: (b, 0, 0))
    rows, win = A_HEADS * PAIR_ROWS, ctx + PAIR_ROWS
    scratch = [pltpu.VMEM((rows, win), F32),
               pltpu.VMEM((PAIR_SLOTS, rows, win), F32), pltpu.VMEM((PAIR_SLOTS, rows, win), BF16)]
    if pad_front:
        scratch += [pltpu.VMEM((ctx + nq, A_KV), BF16)] * 2
    return pl.pallas_call(
        functools.partial(_attn_a_kernel, nq=nq, ctx=ctx, pad_front=pad_front),
        grid=(nb // bb,),
        in_specs=[per_b(nq, A_Q), per_b(nk, A_KV), per_b(nk, A_KV), _resident(ext.shape), _resident(sink.shape)],
        out_specs=per_b(nq, A_Q),
        out_shape=jax.ShapeDtypeStruct((nb, nq, A_Q), BF16),
        scratch_shapes=scratch,
        compiler_params=_cparams("arbitrary"),
        name="attn_a",
    )(q, k, v, ext, sink)


B_STEP_ROWS = 4 * CHUNK


def _attn_b_kernel(q_ref, k_ref, v_ref, ext_ref, o_ref, bias_scr, s_scr, e_scr, r_scr, *pad_scr,
                   nq, ctx, pad_front):
    win = ctx + B_STEP_ROWS
    lo = lax.broadcasted_iota(jnp.int32, (B_STEP_ROWS, LANES), 1) < HEAD_DIM
    chunks = B_STEP_ROWS // CHUNK
    pair_idx = pl.program_id(1)

    @pl.when(pl.program_id(0) == 0)
    def _():
        for half in range(2):
            bias_scr[pair_idx, half * B_STEP_ROWS:(half + 1) * B_STEP_ROWS, :] = _band_bias(
                ext_ref[0, half:half + 1, :], B_STEP_ROWS, ctx)

    bias_ref = bias_scr.at[pair_idx]

    def make_stages(stream):
        kwin, vwin = _windows(k_ref, v_ref, pad_scr, ctx, pad_front, B_STEP_ROWS, stream)
        return functools.partial(scores, stream, kwin), functools.partial(finish, stream, vwin)

    def scores(stream, kwin, r0, slot, masked):
        qp = q_ref[stream, pl.ds(r0, B_STEP_ROWS), :]
        zero = jnp.zeros_like(qp)
        q2 = jnp.concatenate([jnp.where(lo, qp, zero), jnp.where(lo, zero, qp)], axis=0)
        s_scr[slot] = lax.dot_general(q2, kwin(r0), _NT, preferred_element_type=F32)

    def finish(stream, vwin, r0, slot, masked):
        for rb in range(2 * chunks):
            rows = slice(rb * CHUNK, (rb + 1) * CHUNK)
            ci = rb % chunks
            c_lo, c_hi = CHUNK * ci // LANES, -(-(ctx + CHUNK * (ci + 1)) // LANES)
            band = slice(c_lo * LANES, c_hi * LANES)
            s = s_scr[slot, rows, band] + bias_ref[rows, band]
            if masked:
                col = lax.broadcasted_iota(jnp.int32, s.shape, 1) + c_lo * LANES
                s = jnp.where(col + r0 >= ctx, s, NEG_INF)
            m = _row_stat(jnp.max, _fold_lanes(s, jnp.maximum))
            e = [jnp.exp2(s[:, c * LANES:(c + 1) * LANES] - m) for c in range(c_hi - c_lo)]
            for c in range(win // LANES):
                blk = e[c - c_lo].astype(BF16) if c_lo <= c < c_hi else jnp.zeros((CHUNK, LANES), BF16)
                e_scr[slot, rows, c * LANES:(c + 1) * LANES] = blk
            r_scr[slot, rows, :] = 1.0 / _row_stat(jnp.sum, functools.reduce(jnp.add, e))
        o2 = _dot(e_scr[slot], vwin(r0)) * r_scr[slot]
        o = jnp.where(lo, o2[:B_STEP_ROWS], o2[B_STEP_ROWS:])
        o_ref[stream, pl.ds(r0, B_STEP_ROWS), :] = o.astype(BF16)

    _run_steps(make_stages, q_ref.shape[0], nq // B_STEP_ROWS,
               -(-ctx // B_STEP_ROWS) if pad_front else 0, B_STEP_ROWS, lookahead=True)


def _attn_b(q, k, v, ext, ctx, pad_front):
    nb, nq, _ = q.shape
    nk = k.shape[1]
    assert nq % B_STEP_ROWS == 0 and nk == (nq if pad_front else ctx + nq)
    bb = _streams_per_block(nb, pad_front)
    blk = lambda r: pl.BlockSpec((bb, r, LANES), lambda b, p: (b, 0, p))
    rows, win = 2 * B_STEP_ROWS, ctx + B_STEP_ROWS
    scratch = [pltpu.VMEM((B_HEADS // 2, rows, win), F32),
               pltpu.VMEM((PAIR_SLOTS, rows, win), F32), pltpu.VMEM((PAIR_SLOTS, rows, win), BF16),
               pltpu.VMEM((PAIR_SLOTS, rows, LANES), F32)]
    if pad_front:
        scratch += [pltpu.VMEM((ctx + nq, LANES), BF16)] * 2
    return pl.pallas_call(
        functools.partial(_attn_b_kernel, nq=nq, ctx=ctx, pad_front=pad_front),
        grid=(nb // bb, B_HEADS // 2),
        in_specs=[blk(nq), blk(nk), blk(nk),
                  pl.BlockSpec((1,) + ext.shape[1:], lambda b, p: (p, 0, 0))],
        out_specs=blk(nq),
        out_shape=jax.ShapeDtypeStruct((nb, nq, B_QKV), BF16),
        scratch_shapes=scratch,
        compiler_params=_cparams("arbitrary", "arbitrary"),
        name="attn_b",
    )(q, k, v, ext)


def _t5_bucket(rel):
    nb = T5_BUCKETS // 2
    max_exact = nb // 2
    n = -rel
    ret = jnp.where(n < 0, nb, 0)
    n = jnp.abs(n)
    nf = jnp.maximum(n, 1).astype(F32)
    large = max_exact + (jnp.log(nf / max_exact) / math.log(T5_MAX_DIST / max_exact)
                         * (nb - max_exact)).astype(jnp.int32)
    large = jnp.minimum(large, nb - 1)
    return ret + jnp.where(n < max_exact, n, large)


def _band_values(ctx, n_rows, value_of_rel):
    n_cols = ctx + n_rows
    period = -(-(n_cols + n_rows - 1) // LANES) * LANES
    d = jnp.concatenate([jnp.arange(0, period - (n_rows - 1)), jnp.arange(-(n_rows - 1), 0)])
    return value_of_rel(d - ctx)


def _bias_a(t5_table):
    return _band_values(WINDOW, PAIR_ROWS, lambda rel: t5_table.astype(F32)[_t5_bucket(rel)].T)


def _bias_b(b_rel):
    ext = _band_values(B_REACH, B_STEP_ROWS,
                       lambda rel: b_rel.astype(F32)[:, jnp.clip(rel, -B_MAX_REL, B_MAX_REL) + B_MAX_REL])
    return ext.reshape(B_HEADS // 2, 2, ext.shape[-1])


ODD_IN_PAD = 2048
ROPE_HALF = D_ROPE // 2
X1_LO = LANES // 2
NOPE_SPLIT = X1_LO - ROPE_HALF


def _head_lane_source():
    zero = D_NOPE + D_ROPE
    src = []
    for lane in range(LANES):
        if lane < ROPE_HALF:
            src.append(D_NOPE + ROPE_HALF + lane)
        elif lane < X1_LO:
            src.append(lane - ROPE_HALF)
        elif lane < X1_LO + ROPE_HALF:
            src.append(D_NOPE + lane - X1_LO)
        elif lane < X1_LO + ROPE_HALF + D_NOPE - NOPE_SPLIT:
            src.append(NOPE_SPLIT + lane - X1_LO - ROPE_HALF)
        else:
            src.append(zero)
    return src


def _to_head_lanes(a):
    padded = jnp.concatenate([a, jnp.zeros(a.shape[:-1] + (1,), a.dtype)], axis=-1)
    return padded[..., jnp.asarray(_head_lane_source(), dtype=jnp.int32)]


def _lane_masks(shape):
    lane = lax.broadcasted_iota(jnp.int32, shape, 1)
    rope = (lane < ROPE_HALF) | ((lane >= X1_LO) & (lane < X1_LO + ROPE_HALF))
    used = lane < X1_LO + ROPE_HALF + D_NOPE - NOPE_SPLIT
    return used & ~rope, rope


def _rope(y, cosf, sinf):
    return y * cosf + pltpu.roll(y, LANES // 2, 1) * sinf


ROW_BLOCK = 128


def _row_blocks(tm):
    rb = min(ROW_BLOCK, tm)
    return [slice(r, r + rb) for r in range(0, tm, rb)]


def _segment_mean_squares(src_scr, cols, seg_ref, sq_scr, ms_scr, slot):
    for rows in _row_blocks(src_scr.shape[0]):
        v = src_scr[rows, cols]
        sq_scr[slot, rows, :] = (v * v).astype(BF16)
    ms_scr[slot] = _dot(sq_scr[slot], seg_ref[...])


def _expand_kv(ckvb_ref, kpe_blk_ref, wkv_ref, kgain_ref, kseg_ref, kv_scr, sq_scr, ms_scr, k_ref, v_ref,
               between=None):
    k_cols = D_HEADS * LANES
    tm = kv_scr.shape[0]
    for grp in range(kv_scr.shape[1] // MXU_COLS):
        if between and grp in between:
            between[grp]()
        cols = slice(grp * MXU_COLS, (grp + 1) * MXU_COLS)
        kv_scr[:, cols] = _dot(ckvb_ref[...], wkv_ref[:, cols])
        if cols.start < k_cols:
            slot = grp % 2
            _segment_mean_squares(kv_scr, cols, kseg_ref, sq_scr, ms_scr, slot)
        for rows in _row_blocks(tm):
            if cols.start < k_cols:
                for half in range(2):
                    h = 2 * grp + half
                    kb = kv_scr[rows, h * LANES:(h + 1) * LANES]
                    ms = ms_scr[slot, rows, half * LANES:(half + 1) * LANES]
                    kn = (kb * lax.rsqrt(ms + EPS)) * kgain_ref[...]
                    k_ref[rows, h * LANES:(h + 1) * LANES] = (kn + kpe_blk_ref[rows, :]).astype(BF16)
            else:
                v_ref[rows, cols.start - k_cols:cols.stop - k_cols] = kv_scr[rows, cols].astype(BF16)


def _odd_inproj_kernel(x_ref, g_ref, w_ref, cinit_ref, convw_ref, qan_ref, wqb_ref, qgain_ref,
                       kvan_ref, krgain_ref, invf_ref, sgn_ref, wkv_ref, kgain_ref, qseg_ref, kseg_ref,
                       yc_ref, q_ref, k_ref, v_ref, ckv_ref, kpe_ref, cs_ref,
                       h_scr, z_scr, uext_scr, qn_scr, q_scr, kv_scr, rot_scr, ckvb_scr, kpe_scr, sq_scr, ms_scr,
                       *, tm, tpb, pos0):
    i = pl.program_id(0)
    tile = i % tpb
    off = pl.multiple_of(tile * tm, tm)
    blocks = _row_blocks(tm)
    rb = blocks[0].stop

    @pl.when(tile == 0)
    def _():
        uext_scr[0:8, :] = cinit_ref[0]

    @pl.when(i < tpb)
    def _():
        row = lax.broadcasted_iota(jnp.int32, (tm, LANES), 0)
        ang = (row + (tile * tm + pos0)).astype(F32) * invf_ref[...]
        rot_scr[0, pl.ds(off, tm), :] = jnp.cos(ang)
        rot_scr[1, pl.ds(off, tm), :] = jnp.sin(ang) * sgn_ref[...]

    h_scr[...] = _rms(x_ref[...], g_ref[...]).astype(BF16)
    low_lanes = lax.broadcasted_iota(jnp.int32, (rb, LANES), 1) < ROPE_HALF

    def proj(c0, c1):
        z_scr[:, c0:c1] = _dot(h_scr[...], w_ref[:, c0:c1])

    def rotary(y, rows):
        cosf = rot_scr[0, pl.ds(off + rows.start, rb), :]
        sinf = rot_scr[1, pl.ds(off + rows.start, rb), :]
        return _rope(y, cosf, sinf)

    def conv_group(grp):
        c0 = grp * MXU_COLS
        cols = slice(c0, c0 + MXU_COLS)
        for base in (0, C_WIDTH, 2 * C_WIDTH):
            proj(base + c0, base + c0 + MXU_COLS)
        for rows in blocks:
            up = slice(rows.start + 8, rows.stop + 8)
            uext_scr[up, cols] = (z_scr[rows, C_WIDTH + c0:C_WIDTH + c0 + MXU_COLS]
                                  * z_scr[rows, 2 * C_WIDTH + c0:2 * C_WIDTH + c0 + MXU_COLS])
        for rows in blocks:
            yconv = convw_ref[0:1, cols] * uext_scr[rows.start + 6:rows.stop + 6, cols]
            yconv = yconv + convw_ref[1:2, cols] * uext_scr[rows.start + 7:rows.stop + 7, cols]
            yconv = yconv + convw_ref[2:3, cols] * uext_scr[rows.start + 8:rows.stop + 8, cols]
            yc_ref[rows, cols] = (z_scr[rows, cols] * yconv).astype(BF16)

    proj(1536, 1792)
    proj(1792, 2048)

    for rows in blocks:
        qn_scr[rows, :] = _rms(z_scr[rows, 1536:1792], qan_ref[...]).astype(BF16)
        ckv = _rms(z_scr[rows, 1792:1920], kvan_ref[...])
        ckv_ref[rows, :] = ckv
        ckvb_scr[rows, :] = ckv.astype(BF16)

    for grp in range(D_HEADS * LANES // MXU_COLS):
        cols = slice(grp * MXU_COLS, (grp + 1) * MXU_COLS)
        q_scr[:, cols] = _dot(qn_scr[...], wqb_ref[:, cols])
        slot = grp % 2
        _segment_mean_squares(q_scr, cols, qseg_ref, sq_scr, ms_scr, slot)
        for half in range(2):
            h = 2 * grp + half
            for rows in blocks:
                blk = q_scr[rows, h * LANES:(h + 1) * LANES]
                inv = lax.rsqrt(ms_scr[slot, rows, half * LANES:(half + 1) * LANES] + EPS)
                y = rotary((blk * inv) * qgain_ref[...], rows)
                q_ref[rows, h * LANES:(h + 1) * LANES] = y.astype(BF16)

    for rows in blocks:
        kb = z_scr[rows, 1920:2048]
        ms = jnp.sum(kb * kb, axis=-1, keepdims=True) * (1.0 / D_ROPE)
        kpe_blk = rotary((kb * lax.rsqrt(ms + EPS)) * krgain_ref[...], rows)
        kpe_scr[rows, :] = kpe_blk
        x1_then_x2 = jnp.where(low_lanes, pltpu.roll(kpe_blk, LANES - X1_LO, 1), pltpu.roll(kpe_blk, ROPE_HALF, 1))
        kpe_ref[rows, :] = x1_then_x2[:, 0:D_ROPE]
    _expand_kv(ckvb_scr, kpe_scr, wkv_ref, kgain_ref, kseg_ref, kv_scr, sq_scr, ms_scr, k_ref, v_ref,
               between={0: lambda: conv_group(0), 4: lambda: conv_group(1)})

    cs_ref[0] = uext_scr[tm + 6:tm + 8, :]
    uext_scr[0:8, :] = uext_scr[tm:tm + 8, :]


def _odd_inproj(x, norm_g, ow, conv_init, n_batch, pos0):
    t = x.shape[0]
    tm = min(TOKEN_TILE, t // n_batch)
    tpb = t // n_batch // tm
    assert tpb * tm * n_batch == t
    row = lambda w: pl.BlockSpec((tm, w), lambda i: (i, 0))
    per_b = lambda r, w: pl.BlockSpec((1, r, w), lambda i: (i // tpb, 0, 0))
    small = [ow["conv_w"], ow["qan"], ow["wqb"], ow["qgain"], ow["kvan"], ow["krgain"],
             ow["invf"], ow["sgn"], ow["wkv"], ow["kgain"], ow["qseg"], ow["kseg"]]
    kv_w = D_HEADS * LANES
    v_w = D_HEADS * D_V
    return pl.pallas_call(
        functools.partial(_odd_inproj_kernel, tm=tm, tpb=tpb, pos0=pos0),
        grid=(t // tm,),
        in_specs=[row(D_MODEL), _resident((1, D_MODEL)), _resident(ow["w_in"].shape), per_b(8, C_WIDTH)]
                 + [_resident(a.shape) for a in small],
        out_specs=[row(C_WIDTH), row(kv_w), row(kv_w), row(v_w), row(D_KV_LORA), row(D_ROPE),
                   per_b(CONV_W - 1, C_WIDTH)],
        out_shape=[jax.ShapeDtypeStruct((t, C_WIDTH), BF16), jax.ShapeDtypeStruct((t, kv_w), BF16),
                   jax.ShapeDtypeStruct((t, kv_w), BF16), jax.ShapeDtypeStruct((t, v_w), BF16),
                   jax.ShapeDtypeStruct((t, D_KV_LORA), F32), jax.ShapeDtypeStruct((t, D_ROPE), F32),
                   jax.ShapeDtypeStruct((n_batch, CONV_W - 1, C_WIDTH), F32)],
        scratch_shapes=[pltpu.VMEM((tm, D_MODEL), BF16), pltpu.VMEM((tm, ODD_IN_PAD), F32),
                        pltpu.VMEM((tm + 8, C_WIDTH), F32), pltpu.VMEM((tm, D_Q_LORA), BF16),
                        pltpu.VMEM((tm, kv_w), F32), pltpu.VMEM((tm, kv_w + v_w), F32),
                        pltpu.VMEM((2, tpb * tm, LANES), F32), pltpu.VMEM((tm, D_KV_LORA), BF16),
                        pltpu.VMEM((tm, LANES), F32), pltpu.VMEM((2, tm, MXU_COLS), BF16),
                        pltpu.VMEM((2, tm, MXU_COLS), F32)],
        compiler_params=_cparams("arbitrary"),
        name="odd_inproj",
    )(x, norm_g.reshape(1, D_MODEL), ow["w_in"], conv_init, *small)


def _odd_weights(w_in, conv_w, q_a_norm, w_q_b, kv_a_norm, w_kv_b, qn_nope, qn_rope, kn_nope, kn_rope):
    z = lambda n: jnp.zeros((n,), F32)
    qk = D_NOPE + D_ROPE
    w_in = _to_bf16(w_in)
    kr_blk = _to_head_lanes(jnp.concatenate([jnp.zeros((D_MODEL, D_NOPE), BF16), w_in[:, 1920:]], axis=1))
    w_pad = jnp.concatenate([w_in[:, :1920], kr_blk], axis=1)
    wqb = _to_head_lanes(w_q_b.astype(BF16).reshape(D_Q_LORA, D_HEADS, qk)).reshape(D_Q_LORA, D_HEADS * LANES)
    kvb = w_kv_b.astype(BF16).reshape(D_KV_LORA, D_HEADS, D_NOPE + D_V)
    wk = _to_head_lanes(jnp.concatenate([kvb[:, :, :D_NOPE], jnp.zeros((D_KV_LORA, D_HEADS, D_ROPE), BF16)], axis=-1))
    wkv = jnp.concatenate([wk.reshape(D_KV_LORA, D_HEADS * LANES),
                           kvb[:, :, D_NOPE:].reshape(D_KV_LORA, D_HEADS * D_V)], axis=1)
    inv = 1.0 / (ROPE_THETA ** (jnp.arange(ROPE_HALF, dtype=F32) / ROPE_HALF))
    ones = jnp.ones((ROPE_HALF,), F32)
    lanes = lambda v: _to_head_lanes(v).reshape(1, LANES)
    in_nope = lanes(jnp.concatenate([jnp.ones((D_NOPE,), F32), z(D_ROPE)]))
    in_rope = lanes(jnp.concatenate([z(D_NOPE), jnp.ones((D_ROPE,), F32)]))
    seg_nope = in_nope.T * in_nope * (1.0 / D_NOPE)
    seg_rope = in_rope.T * in_rope * (1.0 / D_ROPE)
    two_heads = lambda m: jnp.kron(jnp.eye(MXU_COLS // LANES, dtype=F32), m).astype(BF16)
    return {
        "qseg": two_heads(seg_nope + seg_rope),
        "kseg": two_heads(seg_nope),
        "wk_dense": kvb[:, :, :D_NOPE].reshape(D_KV_LORA, D_HEADS * D_NOPE),
        "kseg_t": jnp.pad(jnp.kron(jnp.eye(D_HEADS, dtype=F32), jnp.full((1, D_NOPE), 1.0 / D_NOPE, F32)),
                          ((0, 16 - D_HEADS), (0, 0))).astype(BF16),
        "q_absorb": _to_head_lanes(jnp.concatenate(
            [w_kv_b.reshape(D_KV_LORA, D_HEADS, D_NOPE + D_V)[:, :, :D_NOPE] * kn_nope,
             jnp.zeros((D_KV_LORA, D_HEADS, D_ROPE), F32)], axis=-1)).transpose(1, 2, 0).astype(BF16),
        "wv_lanes": jnp.stack([jnp.pad(kvb[:, h, D_NOPE:], ((0, 0), ((h % 2) * D_V, LANES - D_V - (h % 2) * D_V)))
                               for h in range(D_HEADS)]),
        "w_in": w_pad,
        "conv_w": conv_w.astype(F32),
        "qan": q_a_norm.reshape(1, D_Q_LORA),
        "wqb": wqb,
        "qgain": lanes(jnp.concatenate([qn_nope, qn_rope])) * MLA_QSCALE,
        "kvan": kv_a_norm.reshape(1, D_KV_LORA),
        "krgain": lanes(jnp.concatenate([z(D_NOPE), kn_rope])),
        "invf": lanes(jnp.concatenate([z(D_NOPE), inv, inv])),
        "sgn": lanes(jnp.concatenate([z(D_NOPE), -ones, ones])),
        "wkv": wkv,
        "kgain": lanes(jnp.concatenate([kn_nope, z(D_ROPE)])),
        "place": _to_head_lanes(jnp.concatenate([jnp.zeros((D_ROPE, D_NOPE), BF16), jnp.eye(D_ROPE, dtype=BF16)], axis=1)),
    }


def _mla_prompt_kernel(q_ref, k_ref, v_ref, o_ref, s_scr, e_scr, r_scr, *, seq):
    qb = min(MLA_QBLOCK, seq)
    hb = qb // 2
    lo = lax.broadcasted_iota(jnp.int32, (qb, LANES), 1) < D_V
    first_half = lax.broadcasted_iota(jnp.int32, (CHUNK, LANES), 1) < CHUNK
    scores = lambda q, k: lax.dot_general(q, k, _NT, preferred_element_type=F32)
    top, bot = slice(0, hb), slice(hb, qb)

    def spans(i):
        tk = qb * (i + 1)
        return tk - qb, tk - hb, tk

    def qk(i, h):
        t0, t1, tk = spans(i)
        hl = slice(h * LANES, (h + 1) * LANES)
        q0 = i * qb
        if t0 > 0:
            s_scr[h, :, 0:t0] = scores(q_ref[0, q0:q0 + qb, hl], k_ref[0, 0:t0, hl])
        s_scr[h, top, t0:t1] = scores(q_ref[0, q0:q0 + hb, hl], k_ref[0, t0:t1, hl])
        s_scr[h, bot, t0:tk] = scores(q_ref[0, q0 + hb:q0 + qb, hl], k_ref[0, t0:tk, hl])

    def softmax(i, h):
        t0, t1, tk = spans(i)
        for rb in range(qb // CHUNK):
            rows = slice(rb * CHUNK, (rb + 1) * CHUNK)
            visible = t0 + CHUNK * (rb + 1)
            width = t1 if rb < hb // CHUNK else tk
            n_full, ragged = visible // LANES, visible % LANES != 0
            blk = lambda c: s_scr[h, rows, c * LANES:(c + 1) * LANES]
            cols = [blk(c) for c in range(n_full)]
            if ragged:
                cols.append(jnp.where(first_half, blk(n_full), NEG_INF))
            mm = cols[0]
            for c in cols[1:]:
                mm = jnp.maximum(mm, c)
            m = _row_stat(jnp.max, mm)
            acc = None
            for c in range(width // LANES):
                if c < len(cols):
                    e = jnp.exp2(blk(c) - m)
                    if c >= n_full:
                        e = jnp.where(first_half, e, 0.0)
                    acc = e if acc is None else acc + e
                else:
                    e = jnp.zeros((CHUNK, LANES), F32)
                e_scr[h, rows, c * LANES:(c + 1) * LANES] = e.astype(BF16)
            r_scr[h, rows, :] = 1.0 / _row_stat(jnp.sum, acc)

    def pv(i, h):
        t0, t1, tk = spans(i)
        out = jnp.concatenate([_dot(e_scr[h, top, t0:t1], v_ref[0, t0:t1, :]),
                               _dot(e_scr[h, bot, t0:tk], v_ref[0, t0:tk, :])], axis=0)
        if t0 > 0:
            out = out + _dot(e_scr[h, :, 0:t0], v_ref[0, 0:t0, :])
        return out * r_scr[h]

    tasks = [(i, h) for i in range(seq // qb) for h in range(2)]
    outs = {}
    qk(*tasks[0])
    for k, (i, h) in enumerate(tasks):
        if k + 1 < len(tasks):
            qk(*tasks[k + 1])
        softmax(i, h)
        outs[h] = pv(i, h)
        if h == 1:
            o_ref[0, i * qb:(i + 1) * qb, :] = jnp.where(lo, outs[0], outs[1]).astype(BF16)


def _mla_prompt(q, k, v):
    nb, seq, _ = q.shape
    qb = min(MLA_QBLOCK, seq)
    return pl.pallas_call(
        functools.partial(_mla_prompt_kernel, seq=seq),
        grid=(nb, D_HEADS // 2),
        in_specs=[pl.BlockSpec((1, seq, 2 * LANES), lambda b, p: (b, 0, p)),
                  pl.BlockSpec((1, seq, 2 * LANES), lambda b, p: (b, 0, p)),
                  pl.BlockSpec((1, seq, LANES), lambda b, p: (b, 0, p))],
        out_specs=pl.BlockSpec((1, seq, LANES), lambda b, p: (b, 0, p)),
        out_shape=jax.ShapeDtypeStruct((nb, seq, D_HEADS * D_V), BF16),
        scratch_shapes=[pltpu.VMEM((2, qb, seq), F32), pltpu.VMEM((2, qb, seq), BF16),
                        pltpu.VMEM((2, qb, LANES), F32)],
        compiler_params=_cparams("arbitrary", "arbitrary"),
        name="mla_prompt",
    )(q, k, v)


MLA_CACHE_ROWS = 512


def _mla_sample_kernel(q_ref, ckv_ref, kpe_ref, kn_ref, vn_ref, place_ref, wk_ref, kseg_ref, qabs_ref, wv_ref,
                       o_ref, ckvb_scr, kpeb_scr, sq_scr, s1_scr, s2_scr, e_scr):
    nq, past = q_ref.shape[1], ckv_ref.shape[1]
    lane_blocks = past // LANES
    lo = lax.broadcasted_iota(jnp.int32, (nq, LANES), 1) < D_V
    head = lambda h: slice(h * LANES, (h + 1) * LANES)

    for r0 in range(0, past, MLA_CACHE_ROWS):
        rows = slice(r0, r0 + MLA_CACHE_ROWS)
        c = ckv_ref[0, rows, :].astype(BF16)
        ckvb_scr[rows, :] = c
        kpeb_scr[rows, :] = _dot(kpe_ref[0, rows, :].astype(BF16), place_ref[...]).astype(BF16)
        kraw = _dot(c, wk_ref[...])
        sq_scr[rows, :] = (kraw * kraw).astype(BF16)
    r_t = lax.rsqrt(lax.dot_general(kseg_ref[...], sq_scr[...], _NT, preferred_element_type=F32) + EPS)

    q_abs = jnp.concatenate([_dot(q_ref[0, :, head(h)], qabs_ref[h]).astype(BF16) for h in range(D_HEADS)], axis=0)
    q_all = jnp.concatenate([q_ref[0, :, head(h)] for h in range(D_HEADS)], axis=0)
    s1_scr[...] = lax.dot_general(q_abs, ckvb_scr[...], _NT, preferred_element_type=F32)
    s2_scr[...] = lax.dot_general(q_all, kpeb_scr[...], _NT, preferred_element_type=F32)

    inv_den, e_new = [], []
    for h in range(D_HEADS):
        rows = slice(h * nq, (h + 1) * nq)
        s_new = lax.dot_general(q_ref[0, :, head(h)], kn_ref[0, :, head(h)], _NT, preferred_element_type=F32)
        mm = None
        for c in range(lane_blocks):
            cl = slice(c * LANES, (c + 1) * LANES)
            blk = s1_scr[rows, cl] * r_t[h:h + 1, cl] + s2_scr[rows, cl]
            s1_scr[rows, cl] = blk
            mm = blk if mm is None else jnp.maximum(mm, blk)
        m = jnp.maximum(jnp.max(mm, axis=-1, keepdims=True), jnp.max(s_new, axis=-1, keepdims=True))
        acc = None
        for c in range(lane_blocks):
            cl = slice(c * LANES, (c + 1) * LANES)
            e = jnp.exp2(s1_scr[rows, cl] - m)
            acc = e if acc is None else acc + e
            e_scr[rows, cl] = e.astype(BF16)
        en = jnp.exp2(s_new - m)
        inv_den.append(1.0 / (jnp.sum(acc, axis=-1, keepdims=True) + jnp.sum(en, axis=-1, keepdims=True)))
        e_new.append(en.astype(BF16))

    latent = _dot(e_scr[...], ckvb_scr[...])
    for p in range(D_HEADS // 2):
        halves = []
        for half in range(2):
            h = 2 * p + half
            rows = slice(h * nq, (h + 1) * nq)
            o = _dot(latent[rows].astype(BF16), wv_ref[h]) + _dot(e_new[h], vn_ref[0, :, head(p)])
            halves.append(o * inv_den[h])
        o_ref[0, :, head(p)] = jnp.where(lo, halves[0], halves[1]).astype(BF16)


def _mla_sample(q, ckv, kpe, kn, vn, ow):
    nb, nq, _ = q.shape
    past = ckv.shape[1]
    assert past % MLA_CACHE_ROWS == 0
    per_b = lambda r, w: pl.BlockSpec((1, r, w), lambda b: (b, 0, 0))
    consts = [ow["place"], ow["wk_dense"], ow["kseg_t"], ow["q_absorb"], ow["wv_lanes"]]
    rows = D_HEADS * nq
    return pl.pallas_call(
        _mla_sample_kernel,
        grid=(nb,),
        in_specs=[per_b(nq, D_HEADS * LANES), per_b(past, D_KV_LORA), per_b(past, D_ROPE),
                  per_b(nq, D_HEADS * LANES), per_b(nq, D_HEADS * D_V)] + [_resident(a.shape) for a in consts],
        out_specs=per_b(nq, D_HEADS * D_V),
        out_shape=jax.ShapeDtypeStruct((nb, nq, D_HEADS * D_V), BF16),
        scratch_shapes=[pltpu.VMEM((past, D_KV_LORA), BF16), pltpu.VMEM((past, LANES), BF16),
                        pltpu.VMEM((past, D_HEADS * D_NOPE), BF16), pltpu.VMEM((rows, past), F32),
                        pltpu.VMEM((rows, past), F32), pltpu.VMEM((rows, past), BF16)],
        compiler_params=_cparams("arbitrary"),
        name="mla_sample",
    )(q, ckv, kpe, kn, vn, *consts)


def _even_layer(xp, xs, nb, seq, ndb, dseq, norm_g, w_in, w_out, a_qn, a_kn, a_sinks, b_qn, b_kn, b_rel,
                t5_table, ck_a, cv_a, ck_b, cv_b, ffn2):
    w_in = _to_bf16(w_in)
    w_out = _to_bf16(w_out)
    w_in_p = jnp.concatenate([_pair_heads(w_in[:, :A_Q], 1), w_in[:, A_Q:]], axis=1)
    woa = _pair_heads(w_out[:A_Q], 0)
    wob = w_out[A_Q:]
    ones = lambda n: jnp.ones((n,), F32)
    scale = HEAD_DIM ** -0.5 * LOG2E
    gain_row = jnp.concatenate([jnp.tile(a_qn, A_HEADS) * scale, jnp.tile(a_kn, A_KV_HEADS), ones(A_KV),
                                jnp.tile(b_qn, B_HEADS) * scale, jnp.tile(b_kn, B_HEADS), ones(B_QKV)]
                               ).reshape(1, EVEN_IN).astype(F32)
    bias_a = _bias_a(t5_table) * LOG2E
    bias_b = _bias_b(b_rel) * LOG2E
    sink = jnp.broadcast_to(jnp.repeat(a_sinks.astype(F32) * LOG2E, PAIR_ROWS)[:, None],
                            (A_HEADS * PAIR_ROWS, LANES))
    la, lb = min(WINDOW, seq), min(B_REACH, seq)

    aq, ak, av, bq, bk, bv, cak, cav, cbk, cbv = _even_inproj(xp, norm_g, w_in_p, gain_row, nb, la, lb)
    r3 = lambda a: a.reshape(nb, seq, a.shape[-1])
    ya = _attn_a(r3(aq), r3(ak), r3(av), bias_a, sink, WINDOW, True)
    yb = _attn_b(r3(bq), r3(bk), r3(bv), bias_b, B_REACH, True)
    xp = _ffn(xp, *ffn2, out_proj=(ya.reshape(nb * seq, A_Q), yb.reshape(nb * seq, B_QKV), woa, wob))
    st_p = (cak.reshape(nb, la, A_KV_HEADS, HEAD_DIM), cav.reshape(nb, la, A_KV_HEADS, HEAD_DIM),
            cbk.reshape(nb, lb, B_HEADS, HEAD_DIM), cbv.reshape(nb, lb, B_HEADS, HEAD_DIM))

    ts = ndb * dseq
    aq, ak, av, bq, bk, bv, nak, nav, nbk, nbv = _even_inproj(xs, norm_g, w_in_p, gain_row, 1, ts, ts)
    pad_q = lambda a, rows: jnp.pad(a.reshape(ndb, dseq, a.shape[-1]), ((0, 0), (0, rows - dseq), (0, 0)))

    def window(cache, new, rows):
        w = cache.shape[-2] * cache.shape[-1]
        full = jnp.concatenate([cache.reshape(ndb, -1, w), new.reshape(ndb, dseq, w)], axis=1)
        buf = jnp.pad(full, ((0, 0), (0, rows - dseq), (0, 0))).astype(BF16)
        return full[:, dseq:].reshape(cache.shape), buf

    st_ak, kbuf_a = window(ck_a, nak, PAIR_ROWS)
    st_av, vbuf_a = window(cv_a, nav, PAIR_ROWS)
    st_bk, kbuf_b = window(ck_b, nbk, B_STEP_ROWS)
    st_bv, vbuf_b = window(cv_b, nbv, B_STEP_ROWS)
    ya = _attn_a(pad_q(aq, PAIR_ROWS), kbuf_a, vbuf_a, bias_a, sink, WINDOW, False)[:, :dseq]
    yb = _attn_b(pad_q(bq, B_STEP_ROWS), kbuf_b, vbuf_b, bias_b, B_REACH, False)[:, :dseq]
    xs = _ffn(xs, *ffn2, out_proj=(ya.reshape(ts, A_Q), yb.reshape(ts, B_QKV), woa, wob))
    return xp, xs, st_p, (st_ak, st_av, st_bk, st_bv)


def _odd_layer(xp, xs, nb, seq, ndb, dseq, past, norm_g, ow, w_out, conv_prev, c_ckv, c_kpe, ffn2):
    w_out = _to_bf16(w_out)
    woc = w_out[:C_WIDTH]
    wod = w_out[C_WIDTH:]

    zero_init = jnp.zeros((nb, 8, C_WIDTH), F32)
    yc, q, k, v, ckv, kpe, cs = _odd_inproj(xp, norm_g, ow, zero_init, nb, 0)
    r3 = lambda a: a.reshape(nb, seq, a.shape[-1])
    yd = _mla_prompt(r3(q), r3(k), r3(v))
    xp = _ffn(xp, *ffn2, out_proj=(yc, yd.reshape(nb * seq, D_HEADS * D_V), woc, wod))
    st_p = (cs, ckv.reshape(nb, seq, D_KV_LORA), kpe.reshape(nb, seq, D_ROPE))

    ts = ndb * dseq
    init = jnp.pad(conv_prev.astype(F32), ((0, 0), (8 - (CONV_W - 1), 0), (0, 0)))
    yc, q, kn, vn, ckv, kpe, cs = _odd_inproj(xs, norm_g, ow, init, ndb, past)
    s3 = lambda a: a.reshape(ndb, dseq, a.shape[-1])
    yd = _mla_sample(s3(q), c_ckv, c_kpe, s3(kn), s3(vn), ow)
    xs = _ffn(xs, *ffn2, out_proj=(yc, yd.reshape(ts, D_HEADS * D_V), woc, wod))
    st_s = (cs, ckv.reshape(ndb, dseq, D_KV_LORA), kpe.reshape(ndb, dseq, D_ROPE))
    return xp, xs, st_p, st_s


def kernel(x_prompt, x_sample, cache_a_k, cache_a_v, cache_b_k, cache_b_v, state_c_conv, cache_d_ckv, cache_d_kpe, ff1_norm, ff1_w_gu, ff1_w_down, mix_norm, ff2_norm, ff2_w_gu, ff2_w_down, t5_bias_table, ev_w_in, ev_w_out, a_q_norm, a_k_norm, a_sinks, b_q_norm, b_k_norm, b_rel_bias, od_w_in, od_w_out, c_conv_w, d_q_a_norm, d_w_q_b, d_kv_a_norm, d_w_kv_b, d_q_nope_norm, d_q_rope_norm, d_k_nope_norm, d_k_rope_norm):
    nb, seq, _ = x_prompt.shape
    ndb, dseq, _ = x_sample.shape
    past = cache_d_ckv.shape[2]
    depth = ff1_norm.shape[0]
    assert seq % TOKEN_TILE == 0 and dseq == CHUNK and past % CHUNK == 0
    xp = x_prompt.reshape(nb * seq, D_MODEL)
    xs = x_sample.reshape(ndb * dseq, D_MODEL)
    even_p, even_s, odd_p, odd_s = [], [], [], []
    for l in range(depth):
        i = l // 2
        ff1 = _ffn_weights(ff1_w_gu, ff1_w_down, l)
        ffn2 = (ff2_norm[l],) + _ffn_weights(ff2_w_gu, ff2_w_down, l)
        xp = _ffn(xp, ff1_norm[l], *ff1)
        xs = _ffn(xs, ff1_norm[l], *ff1)
        if l % 2 == 0:
            xp, xs, sp, ss = _even_layer(
                xp, xs, nb, seq, ndb, dseq, mix_norm[l], ev_w_in[i], ev_w_out[i], a_q_norm[i], a_k_norm[i],
                a_sinks[i], b_q_norm[i], b_k_norm[i], b_rel_bias[i], t5_bias_table,
                cache_a_k[i], cache_a_v[i], cache_b_k[i], cache_b_v[i], ffn2)
            even_p.append(sp)
            even_s.append(ss)
        else:
            ow = _odd_weights(od_w_in[i], c_conv_w[i], d_q_a_norm[i], d_w_q_b[i], d_kv_a_norm[i], d_w_kv_b[i],
                              d_q_nope_norm[i], d_q_rope_norm[i], d_k_nope_norm[i], d_k_rope_norm[i])
            xp, xs, sp, ss = _odd_layer(xp, xs, nb, seq, ndb, dseq, past, mix_norm[l], ow, od_w_out[i],
                                        state_c_conv[i], cache_d_ckv[i], cache_d_kpe[i], ffn2)
            odd_p.append(sp)
            odd_s.append(ss)
    stack = lambda group, j: group[0][j][None] if len(group) == 1 else jnp.stack([g[j] for g in group])
    return (xp.reshape(nb, seq, D_MODEL), xs.reshape(ndb, dseq, D_MODEL),
            stack(even_p, 0), stack(even_p, 1), stack(even_p, 2), stack(even_p, 3),
            stack(odd_p, 0), stack(odd_p, 1), stack(odd_p, 2),
            stack(even_s, 0), stack(even_s, 1), stack(even_s, 2), stack(even_s, 3),
            stack(odd_s, 0), stack(odd_s, 1), stack(odd_s, 2))
```

```python
import functools
import math

import jax
import jax.numpy as jnp
from jax import lax
from jax.experimental import pallas as pl
from jax.experimental.pallas import tpu as pltpu

F32 = jnp.float32
BF16 = jnp.bfloat16

D_MODEL = 1024
CHUNK = 64
HEAD_DIM = 64
EPS = 1e-6
A_HEADS = 8
A_KV_HEADS = 2
WINDOW = 128
T5_BUCKETS = 32
T5_MAX_DIST = 128
B_HEADS = 8
B_REACH = 512
B_MAX_REL = 128
C_WIDTH = 512
CONV_W = 3
D_HEADS = 8
D_Q_LORA = 256
D_KV_LORA = 128
D_NOPE = 64
D_ROPE = 32
D_V = 64
ROPE_THETA = 10000.0
FFN_DIM = 2816
A_Q = A_HEADS * HEAD_DIM
A_KV = A_KV_HEADS * HEAD_DIM
B_QKV = B_HEADS * HEAD_DIM
EVEN_IN = A_Q + 2 * A_KV + 3 * B_QKV

LANES = 128
TOKEN_TILE = 512
CAST_BLOCK_BYTES = 6 * 1024 * 1024
MXU_COLS = 256
FFN_CHUNK = MXU_COLS
PAIR_ROWS = 2 * CHUNK
MLA_QBLOCK = 512
VMEM_LIMIT_BYTES = 56 * 1024 * 1024
LOG2E = math.log2(math.e)
MLA_QSCALE = (D_NOPE + D_ROPE) ** -0.5 * LOG2E
NEG_INF = float("-inf")

_NT = (((1,), (1,)), ((), ()))


def _cparams(*sem):
    return pltpu.CompilerParams(dimension_semantics=sem, vmem_limit_bytes=VMEM_LIMIT_BYTES)


def _resident(shape):
    zeros = (0,) * len(shape)
    return pl.BlockSpec(shape, lambda *_: zeros, pipeline_mode=pl.Buffered(1))


def _rms(x, g):
    ms = jnp.mean(x * x, axis=-1, keepdims=True)
    return (x * lax.rsqrt(ms + EPS)) * g


def _dot(a, b):
    return jnp.dot(a, b, preferred_element_type=F32)


def _ffn_kernel(*refs, n_chunks, fused_out):
    if fused_out:
        (x_ref, ya_ref, yb_ref, woa_ref, wob_ref, g_ref, wgu_ref, wd_ref,
         o_ref, h_scr, a_scr) = refs
        y = _dot(ya_ref[...], woa_ref[...]) + _dot(yb_ref[...], wob_ref[...])
        o_ref[...] = x_ref[...] + y
        res_ref = o_ref
    else:
        x_ref, g_ref, wgu_ref, wd_ref, o_ref, h_scr, a_scr = refs
        res_ref = x_ref
    h_scr[...] = _rms(res_ref[...], g_ref[...]).astype(BF16)
    for j in range(n_chunks):
        h = h_scr[...]
        g = _dot(h, wgu_ref[:, j * FFN_CHUNK:(j + 1) * FFN_CHUNK])
        u = _dot(h, wgu_ref[:, FFN_DIM + j * FFN_CHUNK:FFN_DIM + (j + 1) * FFN_CHUNK])
        a_scr[:, j * FFN_CHUNK:(j + 1) * FFN_CHUNK] = ((g * jax.nn.sigmoid(g)) * u).astype(BF16)
    o_ref[...] = res_ref[...] + 0.5 * _dot(a_scr[...], wd_ref[...])


def _ffn(x, norm_g, wgu, wd, out_proj=None):
    t = x.shape[0]
    tm = min(TOKEN_TILE, t)
    n_chunks = FFN_DIM // FFN_CHUNK
    row = lambda w: pl.BlockSpec((tm, w), lambda i: (i, 0))
    in_specs = [row(D_MODEL)]
    args = [x]
    if out_proj is not None:
        ya, yb, woa, wob = out_proj
        in_specs += [row(ya.shape[1]), row(yb.shape[1]), _resident(woa.shape), _resident(wob.shape)]
        args += [ya, yb, woa, wob]
    in_specs += [_resident((1, D_MODEL)), _resident(wgu.shape), _resident(wd.shape)]
    args += [norm_g.reshape(1, D_MODEL), wgu, wd]
    return pl.pallas_call(
        functools.partial(_ffn_kernel, n_chunks=n_chunks, fused_out=out_proj is not None),
        grid=(t // tm,),
        in_specs=in_specs,
        out_specs=row(D_MODEL),
        out_shape=jax.ShapeDtypeStruct((t, D_MODEL), F32),
        scratch_shapes=[pltpu.VMEM((tm, D_MODEL), BF16), pltpu.VMEM((tm, FFN_DIM), BF16)],
        compiler_params=_cparams("arbitrary"),
        name="ffn_out" if out_proj is not None else "ffn",
    )(*args)


def _cast_kernel(x_ref, o_ref):
    o_ref[...] = x_ref[...].astype(BF16)


def _to_bf16(w, layer=None):
    r, c = w.shape[-2:]
    fits = [d for d in range(16, r + 1, 16) if r % d == 0 and d * c * 4 <= CAST_BLOCK_BYTES]
    tr = max(fits) if fits else r
    out_spec = pl.BlockSpec((tr, c), lambda i: (i, 0))
    in_spec = out_spec if layer is None else pl.BlockSpec((None, tr, c), lambda i: (layer, i, 0))
    return pl.pallas_call(
        _cast_kernel, grid=(r // tr,), in_specs=[in_spec], out_specs=out_spec,
        out_shape=jax.ShapeDtypeStruct((r, c), BF16),
        compiler_params=_cparams("arbitrary"), name="cast_bf16",
    )(w)


def _ffn_weights(w_gu, w_down, layer):
    return _to_bf16(w_gu, layer), _to_bf16(w_down, layer)


def _norm_halves(blk, gain):
    sq = blk * blk
    lo = lax.broadcasted_iota(jnp.int32, blk.shape, 1) < HEAD_DIM
    s_lo = jnp.sum(jnp.where(lo, sq, 0.0), axis=-1, keepdims=True)
    s_hi = jnp.sum(jnp.where(lo, 0.0, sq), axis=-1, keepdims=True)
    inv = jnp.where(lo, lax.rsqrt(s_lo * (1.0 / HEAD_DIM) + EPS), lax.rsqrt(s_hi * (1.0 / HEAD_DIM) + EPS))
    return (blk * inv) * gain


def _even_inproj_kernel(x_ref, g_ref, w_ref, gain_ref,
                        aq_ref, ak_ref, av_ref, bq_ref, bk_ref, bv_ref,
                        cak_ref, cav_ref, cbk_ref, cbv_ref, h_scr, z_scr, *, tm, tpb, rows_a, rows_b):
    i = pl.program_id(0)
    last = (i % tpb) == (tpb - 1)
    h_scr[...] = _rms(x_ref[...], g_ref[...]).astype(BF16)

    def lanes(c):
        return slice(c * LANES, (c + 1) * LANES)

    def emit(c):
        blk = z_scr[:, lanes(c)]
        gain = gain_ref[:, lanes(c)]
        if c < 4:
            aq_ref[:, lanes(c)] = _norm_halves(blk, gain).astype(BF16)
        elif c == 4:
            kn = _norm_halves(blk, gain)
            ak_ref[...] = kn.astype(BF16)
            z_scr[:, lanes(c)] = kn
        elif c == 5:
            av_ref[...] = blk.astype(BF16)
        elif c < 10:
            bq_ref[:, lanes(c - 6)] = _norm_halves(blk, gain).astype(BF16)
        elif c < 14:
            kn = _norm_halves(blk, gain)
            bk_ref[:, lanes(c - 10)] = kn.astype(BF16)
            z_scr[:, lanes(c)] = kn
        else:
            bv_ref[:, lanes(c - 14)] = blk.astype(BF16)

    for grp in range(EVEN_IN // MXU_COLS):
        cols = slice(grp * MXU_COLS, (grp + 1) * MXU_COLS)
        z_scr[:, cols] = _dot(h_scr[...], w_ref[:, cols])
        emit(2 * grp)
        emit(2 * grp + 1)

    @pl.when(last)
    def _():
        cak_ref[0] = z_scr[tm - rows_a:, lanes(4)]
        cav_ref[0] = z_scr[tm - rows_a:, lanes(5)]
        cbk_ref[0] = z_scr[tm - rows_b:, 10 * LANES:14 * LANES]
        cbv_ref[0] = z_scr[tm - rows_b:, 14 * LANES:18 * LANES]


def _even_inproj(x, norm_g, w_in, gain_row, n_batch, rows_a, rows_b):
    t = x.shape[0]
    tm = min(TOKEN_TILE, t)
    tpb = t // n_batch // tm
    assert tpb * tm * n_batch == t and rows_a <= tm and rows_b <= tm
    row = lambda w: pl.BlockSpec((tm, w), lambda i: (i, 0))
    cache = lambda r, w: pl.BlockSpec((1, r, w), lambda i: (i // tpb, 0, 0))
    bshape = lambda w: jax.ShapeDtypeStruct((t, w), BF16)
    cshape = lambda r, w: jax.ShapeDtypeStruct((n_batch, r, w), F32)
    return pl.pallas_call(
        functools.partial(_even_inproj_kernel, tm=tm, tpb=tpb, rows_a=rows_a, rows_b=rows_b),
        grid=(t // tm,),
        in_specs=[row(D_MODEL), _resident((1, D_MODEL)), _resident(w_in.shape), _resident(gain_row.shape)],
        out_specs=[row(A_Q), row(A_KV), row(A_KV), row(B_QKV), row(B_QKV), row(B_QKV),
                   cache(rows_a, A_KV), cache(rows_a, A_KV), cache(rows_b, B_QKV), cache(rows_b, B_QKV)],
        out_shape=[bshape(A_Q), bshape(A_KV), bshape(A_KV), bshape(B_QKV), bshape(B_QKV), bshape(B_QKV),
                   cshape(rows_a, A_KV), cshape(rows_a, A_KV), cshape(rows_b, B_QKV), cshape(rows_b, B_QKV)],
        scratch_shapes=[pltpu.VMEM((tm, D_MODEL), BF16), pltpu.VMEM((tm, EVEN_IN), F32)],
        compiler_params=_cparams("arbitrary"),
        name="even_inproj",
    )(x, norm_g.reshape(1, D_MODEL), w_in, gain_row)


def _pair_heads(a, axis):
    shape = a.shape
    split = shape[:axis] + (2, A_HEADS // 2, HEAD_DIM) + shape[axis + 1:]
    return a.reshape(split).swapaxes(axis, axis + 1).reshape(shape)


def _fill_padded(buf, src_ref, ctx):
    buf[0:ctx, :] = jnp.zeros((ctx, buf.shape[1]), buf.dtype)
    buf[ctx:, :] = src_ref[0]


def _windows(k_ref, v_ref, pad_scr, ctx, pad_front, step_rows, stream):
    win = ctx + step_rows
    if pad_front:
        kbuf, vbuf = pad_scr
        _fill_padded(kbuf, k_ref, ctx)
        _fill_padded(vbuf, v_ref, ctx)
        return (lambda r0: kbuf[pl.ds(r0, win), :]), (lambda r0: vbuf[pl.ds(r0, win), :])
    return (lambda r0: k_ref[stream, pl.ds(r0, win), :]), (lambda r0: v_ref[stream, pl.ds(r0, win), :])


def _fold_lanes(x, op):
    acc = x[:, :LANES]
    for c in range(1, x.shape[1] // LANES):
        acc = op(acc, x[:, c * LANES:(c + 1) * LANES])
    return acc


def _row_stat(reduce_fn, x):
    return jnp.broadcast_to(reduce_fn(x, axis=-1, keepdims=True), x.shape)


PAIR_SLOTS = 4


def _run_steps(make_stages, n_streams, n_steps, n_front, step_rows, lookahead):
    stages = [make_stages(stream) for stream in range(n_streams)]
    tasks = [(stream, t) for stream in range(n_streams) for t in range(n_steps)]

    def run(stage, k):
        stream, t = tasks[k]
        stages[stream][stage](t * step_rows, k % PAIR_SLOTS, t < n_front)

    n = len(tasks)
    if not lookahead:
        for k in range(n):
            for stage in range(3):
                run(stage, k)
        return
    run(0, 0)
    if n > 1:
        run(0, 1)
    run(1, 0)
    for k in range(n):
        if k + 2 < n:
            run(0, k + 2)
        if k + 1 < n:
            run(1, k + 1)
        run(2, k)


def _band_bias(ext_row, n_rows, ctx):
    win = ctx + n_rows
    period = ext_row.shape[1]
    t = pltpu.roll(jnp.broadcast_to(ext_row, (n_rows, period)), 0, 1, stride=1, stride_axis=0)[:, :win]
    chunk = lax.broadcasted_iota(jnp.int32, (n_rows, win), 0) // CHUNK
    col = lax.broadcasted_iota(jnp.int32, (n_rows, win), 1)
    visible = (col >= CHUNK * chunk) & (col < ctx + CHUNK * (chunk + 1))
    return jnp.where(visible, t, NEG_INF)


def _attn_a_kernel(q_ref, k_ref, v_ref, ext_ref, sink_ref, o_ref, bias_ref, s_scr, e_scr, *pad_scr,
                   nq, ctx, pad_front):
    win = ctx + PAIR_ROWS
    lo = lax.broadcasted_iota(jnp.int32, (PAIR_ROWS, LANES), 1) < HEAD_DIM

    @pl.when(pl.program_id(0) == 0)
    def _():
        for head in range(A_HEADS):
            bias_ref[head * PAIR_ROWS:(head + 1) * PAIR_ROWS, :] = _band_bias(
                ext_ref[head:head + 1, :], PAIR_ROWS, ctx)

    def make_stages(stream):
        kwin, vwin = _windows(k_ref, v_ref, pad_scr, ctx, pad_front, PAIR_ROWS, stream)
        return (functools.partial(scores, stream, kwin), functools.partial(softmax, stream),
                functools.partial(values, stream, vwin))

    def scores(stream, kwin, r0, slot, masked):
        q_lo, q_hi = [], []
        for p in range(4):
            qp = q_ref[stream, pl.ds(r0, PAIR_ROWS), p * LANES:(p + 1) * LANES]
            zero = jnp.zeros_like(qp)
            q_lo.append(jnp.where(lo, qp, zero))
            q_hi.append(jnp.where(lo, zero, qp))
        q2 = jnp.concatenate(q_lo + q_hi, axis=0)
        s_scr[slot] = lax.dot_general(q2, kwin(r0), _NT, preferred_element_type=F32)

    def softmax(stream, r0, slot, masked):
        for head in range(A_HEADS):
            rows = slice(head * PAIR_ROWS, (head + 1) * PAIR_ROWS)
            s = s_scr[slot, rows, :] + bias_ref[rows, :]
            if masked:
                col = lax.broadcasted_iota(jnp.int32, (PAIR_ROWS, win), 1)
                s = jnp.where(col + r0 >= ctx, s, NEG_INF)
            sk = sink_ref[rows, :]
            m = jnp.maximum(_row_stat(jnp.max, _fold_lanes(s, jnp.maximum)), sk)
            e = [jnp.exp2(s[:, c * LANES:(c + 1) * LANES] - m) for c in range(win // LANES)]
            den = _row_stat(jnp.sum, functools.reduce(jnp.add, e)) + jnp.exp2(sk - m)
            inv = 1.0 / den
            for c in range(win // LANES):
                e_scr[slot, rows, c * LANES:(c + 1) * LANES] = (e[c] * inv).astype(BF16)

    def values(stream, vwin, r0, slot, masked):
        o2 = _dot(e_scr[slot], vwin(r0))
        half = 4 * PAIR_ROWS
        for p in range(4):
            o = jnp.where(lo, o2[p * PAIR_ROWS:(p + 1) * PAIR_ROWS],
                          o2[half + p * PAIR_ROWS:half + (p + 1) * PAIR_ROWS])
            o_ref[stream, pl.ds(r0, PAIR_ROWS), p * LANES:(p + 1) * LANES] = o.astype(BF16)

    _run_steps(make_stages, q_ref.shape[0], nq // PAIR_ROWS, ctx // PAIR_ROWS if pad_front else 0, PAIR_ROWS,
               lookahead=False)


def _streams_per_block(nb, pad_front):
    return 1 if pad_front else nb


def _attn_a(q, k, v, ext, sink, ctx, pad_front):
    nb, nq, _ = q.shape
    nk = k.shape[1]
    assert nq % PAIR_ROWS == 0 and nk == (nq if pad_front else ctx + nq)
    bb = _streams_per_block(nb, pad_front)
    per_b = lambda r, w: pl.BlockSpec((bb, r, w), lambda b: (b, 0, 0))
    rows, win = A_HEADS * PAIR_ROWS, ctx + PAIR_ROWS
    scratch = [pltpu.VMEM((rows, win), F32),
               pltpu.VMEM((PAIR_SLOTS, rows, win), F32), pltpu.VMEM((PAIR_SLOTS, rows, win), BF16)]
    if pad_front:
        scratch += [pltpu.VMEM((ctx + nq, A_KV), BF16)] * 2
    return pl.pallas_call(
        functools.partial(_attn_a_kernel, nq=nq, ctx=ctx, pad_front=pad_front),
        grid=(nb // bb,),
        in_specs=[per_b(nq, A_Q), per_b(nk, A_KV), per_b(nk, A_KV), _resident(ext.shape), _resident(sink.shape)],
        out_specs=per_b(nq, A_Q),
        out_shape=jax.ShapeDtypeStruct((nb, nq, A_Q), BF16),
        scratch_shapes=scratch,
        compiler_params=_cparams("arbitrary"),
        name="attn_a",
    )(q, k, v, ext, sink)


B_STEP_ROWS = 4 * CHUNK


def _attn_b_kernel(q_ref, k_ref, v_ref, ext_ref, o_ref, bias_scr, s_scr, e_scr, r_scr, *pad_scr,
                   nq, ctx, pad_front):
    win = ctx + B_STEP_ROWS
    lo = lax.broadcasted_iota(jnp.int32, (B_STEP_ROWS, LANES), 1) < HEAD_DIM
    chunks = B_STEP_ROWS // CHUNK
    pair_idx = pl.program_id(1)

    @pl.when(pl.program_id(0) == 0)
    def _():
        for half in range(2):
            bias_scr[pair_idx, half * B_STEP_ROWS:(half + 1) * B_STEP_ROWS, :] = _band_bias(
                ext_ref[0, half:half + 1, :], B_STEP_ROWS, ctx)

    bias_ref = bias_scr.at[pair_idx]

    def make_stages(stream):
        kwin, vwin = _windows(k_ref, v_ref, pad_scr, ctx, pad_front, B_STEP_ROWS, stream)
        return (functools.partial(scores, stream, kwin), functools.partial(softmax, stream),
                functools.partial(values, stream, vwin))

    def scores(stream, kwin, r0, slot, masked):
        qp = q_ref[stream, pl.ds(r0, B_STEP_ROWS), :]
        zero = jnp.zeros_like(qp)
        q2 = jnp.concatenate([jnp.where(lo, qp, zero), jnp.where(lo, zero, qp)], axis=0)
        s_scr[slot] = lax.dot_general(q2, kwin(r0), _NT, preferred_element_type=F32)

    def softmax(stream, r0, slot, masked):
        for rb in range(2 * chunks):
            rows = slice(rb * CHUNK, (rb + 1) * CHUNK)
            ci = rb % chunks
            c_lo, c_hi = CHUNK * ci // LANES, -(-(ctx + CHUNK * (ci + 1)) // LANES)
            band = slice(c_lo * LANES, c_hi * LANES)
            s = s_scr[slot, rows, band] + bias_ref[rows, band]
            if masked:
                col = lax.broadcasted_iota(jnp.int32, s.shape, 1) + c_lo * LANES
                s = jnp.where(col + r0 >= ctx, s, NEG_INF)
            m = _row_stat(jnp.max, _fold_lanes(s, jnp.maximum))
            e = [jnp.exp2(s[:, c * LANES:(c + 1) * LANES] - m) for c in range(c_hi - c_lo)]
            for c in range(win // LANES):
                blk = e[c - c_lo].astype(BF16) if c_lo <= c < c_hi else jnp.zeros((CHUNK, LANES), BF16)
                e_scr[slot, rows, c * LANES:(c + 1) * LANES] = blk
            r_scr[slot, rows, :] = 1.0 / _row_stat(jnp.sum, functools.reduce(jnp.add, e))

    def values(stream, vwin, r0, slot, masked):
        o2 = _dot(e_scr[slot], vwin(r0)) * r_scr[slot]
        o = jnp.where(lo, o2[:B_STEP_ROWS], o2[B_STEP_ROWS:])
        o_ref[stream, pl.ds(r0, B_STEP_ROWS), :] = o.astype(BF16)

    _run_steps(make_stages, q_ref.shape[0], nq // B_STEP_ROWS,
               -(-ctx // B_STEP_ROWS) if pad_front else 0, B_STEP_ROWS, lookahead=True)


def _attn_b(q, k, v, ext, ctx, pad_front):
    nb, nq, _ = q.shape
    nk = k.shape[1]
    assert nq % B_STEP_ROWS == 0 and nk == (nq if pad_front else ctx + nq)
    bb = _streams_per_block(nb, pad_front)
    blk = lambda r: pl.BlockSpec((bb, r, LANES), lambda b, p: (b, 0, p))
    rows, win = 2 * B_STEP_ROWS, ctx + B_STEP_ROWS
    scratch = [pltpu.VMEM((B_HEADS // 2, rows, win), F32),
               pltpu.VMEM((PAIR_SLOTS, rows, win), F32), pltpu.VMEM((PAIR_SLOTS, rows, win), BF16),
               pltpu.VMEM((PAIR_SLOTS, rows, LANES), F32)]
    if pad_front:
        scratch += [pltpu.VMEM((ctx + nq, LANES), BF16)] * 2
    return pl.pallas_call(
        functools.partial(_attn_b_kernel, nq=nq, ctx=ctx, pad_front=pad_front),
        grid=(nb // bb, B_HEADS // 2),
        in_specs=[blk(nq), blk(nk), blk(nk),
                  pl.BlockSpec((1,) + ext.shape[1:], lambda b, p: (p, 0, 0))],
        out_specs=blk(nq),
        out_shape=jax.ShapeDtypeStruct((nb, nq, B_QKV), BF16),
        scratch_shapes=scratch,
        compiler_params=_cparams("arbitrary", "arbitrary"),
        name="attn_b",
    )(q, k, v, ext)


def _t5_bucket(rel):
    nb = T5_BUCKETS // 2
    max_exact = nb // 2
    n = -rel
    ret = jnp.where(n < 0, nb, 0)
    n = jnp.abs(n)
    nf = jnp.maximum(n, 1).astype(F32)
    large = max_exact + (jnp.log(nf / max_exact) / math.log(T5_MAX_DIST / max_exact)
                         * (nb - max_exact)).astype(jnp.int32)
    large = jnp.minimum(large, nb - 1)
    return ret + jnp.where(n < max_exact, n, large)


def _band_values(ctx, n_rows, value_of_rel):
    n_cols = ctx + n_rows
    period = -(-(n_cols + n_rows - 1) // LANES) * LANES
    d = jnp.concatenate([jnp.arange(0, period - (n_rows - 1)), jnp.arange(-(n_rows - 1), 0)])
    return value_of_rel(d - ctx)


def _bias_a(t5_table):
    return _band_values(WINDOW, PAIR_ROWS, lambda rel: t5_table.astype(F32)[_t5_bucket(rel)].T)


def _bias_b(b_rel):
    ext = _band_values(B_REACH, B_STEP_ROWS,
                       lambda rel: b_rel.astype(F32)[:, jnp.clip(rel, -B_MAX_REL, B_MAX_REL) + B_MAX_REL])
    return ext.reshape(B_HEADS // 2, 2, ext.shape[-1])


ODD_IN_PAD = 2048
ROPE_HALF = D_ROPE // 2
X1_LO = LANES // 2
NOPE_SPLIT = X1_LO - ROPE_HALF


def _head_lane_source():
    zero = D_NOPE + D_ROPE
    src = []
    for lane in range(LANES):
        if lane < ROPE_HALF:
            src.append(D_NOPE + ROPE_HALF + lane)
        elif lane < X1_LO:
            src.append(lane - ROPE_HALF)
        elif lane < X1_LO + ROPE_HALF:
            src.append(D_NOPE + lane - X1_LO)
        elif lane < X1_LO + ROPE_HALF + D_NOPE - NOPE_SPLIT:
            src.append(NOPE_SPLIT + lane - X1_LO - ROPE_HALF)
        else:
            src.append(zero)
    return src


def _to_head_lanes(a):
    padded = jnp.concatenate([a, jnp.zeros(a.shape[:-1] + (1,), a.dtype)], axis=-1)
    return padded[..., jnp.asarray(_head_lane_source(), dtype=jnp.int32)]


def _lane_masks(shape):
    lane = lax.broadcasted_iota(jnp.int32, shape, 1)
    rope = (lane < ROPE_HALF) | ((lane >= X1_LO) & (lane < X1_LO + ROPE_HALF))
    used = lane < X1_LO + ROPE_HALF + D_NOPE - NOPE_SPLIT
    return used & ~rope, rope


def _rope(y, cosf, sinf):
    return y * cosf + pltpu.roll(y, LANES // 2, 1) * sinf


ROW_BLOCK = 128


def _row_blocks(tm):
    rb = min(ROW_BLOCK, tm)
    return [slice(r, r + rb) for r in range(0, tm, rb)]


def _segment_mean_squares(src_scr, cols, seg_ref, sq_scr, ms_scr, slot):
    for rows in _row_blocks(src_scr.shape[0]):
        v = src_scr[rows, cols]
        sq_scr[slot, rows, :] = (v * v).astype(BF16)
    ms_scr[slot] = _dot(sq_scr[slot], seg_ref[...])


def _expand_kv(ckvb_ref, kpe_blk_ref, wkv_ref, kgain_ref, kseg_ref, kv_scr, sq_scr, ms_scr, k_ref, v_ref,
               between=None):
    k_cols = D_HEADS * LANES
    tm = kv_scr.shape[0]
    for grp in range(kv_scr.shape[1] // MXU_COLS):
        if between and grp in between:
            between[grp]()
        cols = slice(grp * MXU_COLS, (grp + 1) * MXU_COLS)
        kv_scr[:, cols] = _dot(ckvb_ref[...], wkv_ref[:, cols])
        if cols.start < k_cols:
            slot = grp % 2
            _segment_mean_squares(kv_scr, cols, kseg_ref, sq_scr, ms_scr, slot)
        for rows in _row_blocks(tm):
            if cols.start < k_cols:
                for half in range(2):
                    h = 2 * grp + half
                    kb = kv_scr[rows, h * LANES:(h + 1) * LANES]
                    ms = ms_scr[slot, rows, half * LANES:(half + 1) * LANES]
                    kn = (kb * lax.rsqrt(ms + EPS)) * kgain_ref[...]
                    k_ref[rows, h * LANES:(h + 1) * LANES] = (kn + kpe_blk_ref[rows, :]).astype(BF16)
            else:
                v_ref[rows, cols.start - k_cols:cols.stop - k_cols] = kv_scr[rows, cols].astype(BF16)


def _odd_inproj_kernel(x_ref, g_ref, w_ref, cinit_ref, convw_ref, qan_ref, wqb_ref, qgain_ref,
                       kvan_ref, krgain_ref, invf_ref, sgn_ref, wkv_ref, kgain_ref, qseg_ref, kseg_ref,
                       yc_ref, q_ref, k_ref, v_ref, ckv_ref, kpe_ref, cs_ref,
                       h_scr, z_scr, uext_scr, qn_scr, q_scr, kv_scr, rot_scr, ckvb_scr, kpe_scr, sq_scr, ms_scr,
                       *, tm, tpb, pos0):
    i = pl.program_id(0)
    tile = i % tpb
    off = pl.multiple_of(tile * tm, tm)
    blocks = _row_blocks(tm)
    rb = blocks[0].stop

    @pl.when(tile == 0)
    def _():
        uext_scr[0:8, :] = cinit_ref[0]

    @pl.when(i < tpb)
    def _():
        row = lax.broadcasted_iota(jnp.int32, (tm, LANES), 0)
        ang = (row + (tile * tm + pos0)).astype(F32) * invf_ref[...]
        rot_scr[0, pl.ds(off, tm), :] = jnp.cos(ang)
        rot_scr[1, pl.ds(off, tm), :] = jnp.sin(ang) * sgn_ref[...]

    h_scr[...] = _rms(x_ref[...], g_ref[...]).astype(BF16)
    low_lanes = lax.broadcasted_iota(jnp.int32, (rb, LANES), 1) < ROPE_HALF

    def proj(c0, c1):
        z_scr[:, c0:c1] = _dot(h_scr[...], w_ref[:, c0:c1])

    def rotary(y, rows):
        cosf = rot_scr[0, pl.ds(off + rows.start, rb), :]
        sinf = rot_scr[1, pl.ds(off + rows.start, rb), :]
        return _rope(y, cosf, sinf)

    def conv_group(grp):
        c0 = grp * MXU_COLS
        cols = slice(c0, c0 + MXU_COLS)
        for base in (0, C_WIDTH, 2 * C_WIDTH):
            proj(base + c0, base + c0 + MXU_COLS)
        for rows in blocks:
            up = slice(rows.start + 8, rows.stop + 8)
            uext_scr[up, cols] = (z_scr[rows, C_WIDTH + c0:C_WIDTH + c0 + MXU_COLS]
                                  * z_scr[rows, 2 * C_WIDTH + c0:2 * C_WIDTH + c0 + MXU_COLS])
        for rows in blocks:
            yconv = convw_ref[0:1, cols] * uext_scr[rows.start + 6:rows.stop + 6, cols]
            yconv = yconv + convw_ref[1:2, cols] * uext_scr[rows.start + 7:rows.stop + 7, cols]
            yconv = yconv + convw_ref[2:3, cols] * uext_scr[rows.start + 8:rows.stop + 8, cols]
            yc_ref[rows, cols] = (z_scr[rows, cols] * yconv).astype(BF16)

    proj(1536, 1792)
    proj(1792, 2048)

    for rows in blocks:
        qn_scr[rows, :] = _rms(z_scr[rows, 1536:1792], qan_ref[...]).astype(BF16)
        ckv = _rms(z_scr[rows, 1792:1920], kvan_ref[...])
        ckv_ref[rows, :] = ckv
        ckvb_scr[rows, :] = ckv.astype(BF16)

    for grp in range(D_HEADS * LANES // MXU_COLS):
        cols = slice(grp * MXU_COLS, (grp + 1) * MXU_COLS)
        q_scr[:, cols] = _dot(qn_scr[...], wqb_ref[:, cols])
        slot = grp % 2
        _segment_mean_squares(q_scr, cols, qseg_ref, sq_scr, ms_scr, slot)
        for half in range(2):
            h = 2 * grp + half
            for rows in blocks:
                blk = q_scr[rows, h * LANES:(h + 1) * LANES]
                inv = lax.rsqrt(ms_scr[slot, rows, half * LANES:(half + 1) * LANES] + EPS)
                y = rotary((blk * inv) * qgain_ref[...], rows)
                q_ref[rows, h * LANES:(h + 1) * LANES] = y.astype(BF16)

    for rows in blocks:
        kb = z_scr[rows, 1920:2048]
        ms = jnp.sum(kb * kb, axis=-1, keepdims=True) * (1.0 / D_ROPE)
        kpe_blk = rotary((kb * lax.rsqrt(ms + EPS)) * krgain_ref[...], rows)
        kpe_scr[rows, :] = kpe_blk
        x1_then_x2 = jnp.where(low_lanes, pltpu.roll(kpe_blk, LANES - X1_LO, 1), pltpu.roll(kpe_blk, ROPE_HALF, 1))
        kpe_ref[rows, :] = x1_then_x2[:, 0:D_ROPE]
    _expand_kv(ckvb_scr, kpe_scr, wkv_ref, kgain_ref, kseg_ref, kv_scr, sq_scr, ms_scr, k_ref, v_ref,
               between={0: lambda: conv_group(0), 4: lambda: conv_group(1)})

    cs_ref[0] = uext_scr[tm + 6:tm + 8, :]
    uext_scr[0:8, :] = uext_scr[tm:tm + 8, :]


def _odd_inproj(x, norm_g, ow, conv_init, n_batch, pos0):
    t = x.shape[0]
    tm = min(TOKEN_TILE, t // n_batch)
    tpb = t // n_batch // tm
    assert tpb * tm * n_batch == t
    row = lambda w: pl.BlockSpec((tm, w), lambda i: (i, 0))
    per_b = lambda r, w: pl.BlockSpec((1, r, w), lambda i: (i // tpb, 0, 0))
    small = [ow["conv_w"], ow["qan"], ow["wqb"], ow["qgain"], ow["kvan"], ow["krgain"],
             ow["invf"], ow["sgn"], ow["wkv"], ow["kgain"], ow["qseg"], ow["kseg"]]
    kv_w = D_HEADS * LANES
    v_w = D_HEADS * D_V
    return pl.pallas_call(
        functools.partial(_odd_inproj_kernel, tm=tm, tpb=tpb, pos0=pos0),
        grid=(t // tm,),
        in_specs=[row(D_MODEL), _resident((1, D_MODEL)), _resident(ow["w_in"].shape), per_b(8, C_WIDTH)]
                 + [_resident(a.shape) for a in small],
        out_specs=[row(C_WIDTH), row(kv_w), row(kv_w), row(v_w), row(D_KV_LORA), row(D_ROPE),
                   per_b(CONV_W - 1, C_WIDTH)],
        out_shape=[jax.ShapeDtypeStruct((t, C_WIDTH), BF16), jax.ShapeDtypeStruct((t, kv_w), BF16),
                   jax.ShapeDtypeStruct((t, kv_w), BF16), jax.ShapeDtypeStruct((t, v_w), BF16),
                   jax.ShapeDtypeStruct((t, D_KV_LORA), F32), jax.ShapeDtypeStruct((t, D_ROPE), F32),
                   jax.ShapeDtypeStruct((n_batch, CONV_W - 1, C_WIDTH), F32)],
        scratch_shapes=[pltpu.VMEM((tm, D_MODEL), BF16), pltpu.VMEM((tm, ODD_IN_PAD), F32),
                        pltpu.VMEM((tm + 8, C_WIDTH), F32), pltpu.VMEM((tm, D_Q_LORA), BF16),
                        pltpu.VMEM((tm, kv_w), F32), pltpu.VMEM((tm, kv_w + v_w), F32),
                        pltpu.VMEM((2, tpb * tm, LANES), F32), pltpu.VMEM((tm, D_KV_LORA), BF16),
                        pltpu.VMEM((tm, LANES), F32), pltpu.VMEM((2, tm, MXU_COLS), BF16),
                        pltpu.VMEM((2, tm, MXU_COLS), F32)],
        compiler_params=_cparams("arbitrary"),
        name="odd_inproj",
    )(x, norm_g.reshape(1, D_MODEL), ow["w_in"], conv_init, *small)


def _odd_weights(w_in, conv_w, q_a_norm, w_q_b, kv_a_norm, w_kv_b, qn_nope, qn_rope, kn_nope, kn_rope):
    z = lambda n: jnp.zeros((n,), F32)
    qk = D_NOPE + D_ROPE
    w_in = _to_bf16(w_in)
    kr_blk = _to_head_lanes(jnp.concatenate([jnp.zeros((D_MODEL, D_NOPE), BF16), w_in[:, 1920:]], axis=1))
    w_pad = jnp.concatenate([w_in[:, :1920], kr_blk], axis=1)
    wqb = _to_head_lanes(w_q_b.astype(BF16).reshape(D_Q_LORA, D_HEADS, qk)).reshape(D_Q_LORA, D_HEADS * LANES)
    kvb = w_kv_b.astype(BF16).reshape(D_KV_LORA, D_HEADS, D_NOPE + D_V)
    wk = _to_head_lanes(jnp.concatenate([kvb[:, :, :D_NOPE], jnp.zeros((D_KV_LORA, D_HEADS, D_ROPE), BF16)], axis=-1))
    wkv = jnp.concatenate([wk.reshape(D_KV_LORA, D_HEADS * LANES),
                           kvb[:, :, D_NOPE:].reshape(D_KV_LORA, D_HEADS * D_V)], axis=1)
    inv = 1.0 / (ROPE_THETA ** (jnp.arange(ROPE_HALF, dtype=F32) / ROPE_HALF))
    ones = jnp.ones((ROPE_HALF,), F32)
    lanes = lambda v: _to_head_lanes(v).reshape(1, LANES)
    in_nope = lanes(jnp.concatenate([jnp.ones((D_NOPE,), F32), z(D_ROPE)]))
    in_rope = lanes(jnp.concatenate([z(D_NOPE), jnp.ones((D_ROPE,), F32)]))
    seg_nope = in_nope.T * in_nope * (1.0 / D_NOPE)
    seg_rope = in_rope.T * in_rope * (1.0 / D_ROPE)
    two_heads = lambda m: jnp.kron(jnp.eye(MXU_COLS // LANES, dtype=F32), m).astype(BF16)
    return {
        "qseg": two_heads(seg_nope + seg_rope),
        "kseg": two_heads(seg_nope),
        "wk_dense": kvb[:, :, :D_NOPE].reshape(D_KV_LORA, D_HEADS * D_NOPE),
        "kseg_t": jnp.pad(jnp.kron(jnp.eye(D_HEADS, dtype=F32), jnp.full((1, D_NOPE), 1.0 / D_NOPE, F32)),
                          ((0, 16 - D_HEADS), (0, 0))).astype(BF16),
        "q_absorb": _to_head_lanes(jnp.concatenate(
            [w_kv_b.reshape(D_KV_LORA, D_HEADS, D_NOPE + D_V)[:, :, :D_NOPE] * kn_nope,
             jnp.zeros((D_KV_LORA, D_HEADS, D_ROPE), F32)], axis=-1)).transpose(1, 2, 0).astype(BF16),
        "wv_lanes": jnp.stack([jnp.pad(kvb[:, h, D_NOPE:], ((0, 0), ((h % 2) * D_V, LANES - D_V - (h % 2) * D_V)))
                               for h in range(D_HEADS)]),
        "w_in": w_pad,
        "conv_w": conv_w.astype(F32),
        "qan": q_a_norm.reshape(1, D_Q_LORA),
        "wqb": wqb,
        "qgain": lanes(jnp.concatenate([qn_nope, qn_rope])) * MLA_QSCALE,
        "kvan": kv_a_norm.reshape(1, D_KV_LORA),
        "krgain": lanes(jnp.concatenate([z(D_NOPE), kn_rope])),
        "invf": lanes(jnp.concatenate([z(D_NOPE), inv, inv])),
        "sgn": lanes(jnp.concatenate([z(D_NOPE), -ones, ones])),
        "wkv": wkv,
        "kgain": lanes(jnp.concatenate([kn_nope, z(D_ROPE)])),
        "place": _to_head_lanes(jnp.concatenate([jnp.zeros((D_ROPE, D_NOPE), BF16), jnp.eye(D_ROPE, dtype=BF16)], axis=1)),
    }


def _mla_prompt_kernel(q_ref, k_ref, v_ref, o_ref, s_scr, e_scr, r_scr, *, seq):
    qb = min(MLA_QBLOCK, seq)
    hb = qb // 2
    lo = lax.broadcasted_iota(jnp.int32, (qb, LANES), 1) < D_V
    first_half = lax.broadcasted_iota(jnp.int32, (CHUNK, LANES), 1) < CHUNK
    scores = lambda q, k: lax.dot_general(q, k, _NT, preferred_element_type=F32)
    top, bot = slice(0, hb), slice(hb, qb)

    def spans(i):
        tk = qb * (i + 1)
        return tk - qb, tk - hb, tk

    def qk(i, h):
        t0, t1, tk = spans(i)
        hl = slice(h * LANES, (h + 1) * LANES)
        q0 = i * qb
        if t0 > 0:
            s_scr[h, :, 0:t0] = scores(q_ref[0, q0:q0 + qb, hl], k_ref[0, 0:t0, hl])
        s_scr[h, top, t0:t1] = scores(q_ref[0, q0:q0 + hb, hl], k_ref[0, t0:t1, hl])
        s_scr[h, bot, t0:tk] = scores(q_ref[0, q0 + hb:q0 + qb, hl], k_ref[0, t0:tk, hl])

    def softmax(i, h):
        t0, t1, tk = spans(i)
        for rb in range(qb // CHUNK):
            rows = slice(rb * CHUNK, (rb + 1) * CHUNK)
            visible = t0 + CHUNK * (rb + 1)
            width = t1 if rb < hb // CHUNK else tk
            n_full, ragged = visible // LANES, visible % LANES != 0
            blk = lambda c: s_scr[h, rows, c * LANES:(c + 1) * LANES]
            cols = [blk(c) for c in range(n_full)]
            if ragged:
                cols.append(jnp.where(first_half, blk(n_full), NEG_INF))
            mm = cols[0]
            for c in cols[1:]:
                mm = jnp.maximum(mm, c)
            m = _row_stat(jnp.max, mm)
            acc = None
            for c in range(width // LANES):
                if c < len(cols):
                    e = jnp.exp2(blk(c) - m)
                    if c >= n_full:
                        e = jnp.where(first_half, e, 0.0)
                    acc = e if acc is None else acc + e
                else:
                    e = jnp.zeros((CHUNK, LANES), F32)
                e_scr[h, rows, c * LANES:(c + 1) * LANES] = e.astype(BF16)
            r_scr[h, rows, :] = 1.0 / _row_stat(jnp.sum, acc)

    def pv(i, h):
        t0, t1, tk = spans(i)
        out = jnp.concatenate([_dot(e_scr[h, top, t0:t1], v_ref[0, t0:t1, :]),
                               _dot(e_scr[h, bot, t0:tk], v_ref[0, t0:tk, :])], axis=0)
        if t0 > 0:
            out = out + _dot(e_scr[h, :, 0:t0], v_ref[0, 0:t0, :])
        return out * r_scr[h]

    tasks = [(i, h) for i in range(seq // qb) for h in range(2)]
    n = len(tasks)
    outs = {}
    qk(*tasks[0])
    if n > 1:
        qk(*tasks[1])
    softmax(*tasks[0])
    for k, (i, h) in enumerate(tasks):
        if k + 2 < n:
            qk(*tasks[k + 2])
        if k + 1 < n:
            softmax(*tasks[k + 1])
        outs[h] = pv(i, h)
        if h == 1:
            o_ref[0, i * qb:(i + 1) * qb, :] = jnp.where(lo, outs[0], outs[1]).astype(BF16)


def _mla_prompt(q, k, v):
    nb, seq, _ = q.shape
    qb = min(MLA_QBLOCK, seq)
    return pl.pallas_call(
        functools.partial(_mla_prompt_kernel, seq=seq),
        grid=(nb, D_HEADS // 2),
        in_specs=[pl.BlockSpec((1, seq, 2 * LANES), lambda b, p: (b, 0, p)),
                  pl.BlockSpec((1, seq, 2 * LANES), lambda b, p: (b, 0, p)),
                  pl.BlockSpec((1, seq, LANES), lambda b, p: (b, 0, p))],
        out_specs=pl.BlockSpec((1, seq, LANES), lambda b, p: (b, 0, p)),
        out_shape=jax.ShapeDtypeStruct((nb, seq, D_HEADS * D_V), BF16),
        scratch_shapes=[pltpu.VMEM((2, qb, seq), F32), pltpu.VMEM((2, qb, seq), BF16),
                        pltpu.VMEM((2, qb, LANES), F32)],
        compiler_params=_cparams("arbitrary", "arbitrary"),
        name="mla_prompt",
    )(q, k, v)


MLA_CACHE_ROWS = 512


def _mla_sample_kernel(q_ref, ckv_ref, kpe_ref, kn_ref, vn_ref, place_ref, wk_ref, kseg_ref, qabs_ref, wv_ref,
                       o_ref, ckvb_scr, kpeb_scr, sq_scr, s1_scr, s2_scr, e_scr):
    nq, past = q_ref.shape[1], ckv_ref.shape[1]
    lane_blocks = past // LANES
    lo = lax.broadcasted_iota(jnp.int32, (nq, LANES), 1) < D_V
    head = lambda h: slice(h * LANES, (h + 1) * LANES)

    for r0 in range(0, past, MLA_CACHE_ROWS):
        rows = slice(r0, r0 + MLA_CACHE_ROWS)
        c = ckv_ref[0, rows, :].astype(BF16)
        ckvb_scr[rows, :] = c
        kpeb_scr[rows, :] = _dot(kpe_ref[0, rows, :].astype(BF16), place_ref[...]).astype(BF16)
        kraw = _dot(c, wk_ref[...])
        sq_scr[rows, :] = (kraw * kraw).astype(BF16)
    r_t = lax.rsqrt(lax.dot_general(kseg_ref[...], sq_scr[...], _NT, preferred_element_type=F32) + EPS)

    q_abs = jnp.concatenate([_dot(q_ref[0, :, head(h)], qabs_ref[h]).astype(BF16) for h in range(D_HEADS)], axis=0)
    q_all = jnp.concatenate([q_ref[0, :, head(h)] for h in range(D_HEADS)], axis=0)
    s1_scr[...] = lax.dot_general(q_abs, ckvb_scr[...], _NT, preferred_element_type=F32)
    s2_scr[...] = lax.dot_general(q_all, kpeb_scr[...], _NT, preferred_element_type=F32)

    inv_den, e_new = [], []
    for h in range(D_HEADS):
        rows = slice(h * nq, (h + 1) * nq)
        s_new = lax.dot_general(q_ref[0, :, head(h)], kn_ref[0, :, head(h)], _NT, preferred_element_type=F32)
        mm = None
        for c in range(lane_blocks):
            cl = slice(c * LANES, (c + 1) * LANES)
            blk = s1_scr[rows, cl] * r_t[h:h + 1, cl] + s2_scr[rows, cl]
            s1_scr[rows, cl] = blk
            mm = blk if mm is None else jnp.maximum(mm, blk)
        m = jnp.maximum(jnp.max(mm, axis=-1, keepdims=True), jnp.max(s_new, axis=-1, keepdims=True))
        acc = None
        for c in range(lane_blocks):
            cl = slice(c * LANES, (c + 1) * LANES)
            e = jnp.exp2(s1_scr[rows, cl] - m)
            acc = e if acc is None else acc + e
            e_scr[rows, cl] = e.astype(BF16)
        en = jnp.exp2(s_new - m)
        inv_den.append(1.0 / (jnp.sum(acc, axis=-1, keepdims=True) + jnp.sum(en, axis=-1, keepdims=True)))
        e_new.append(en.astype(BF16))

    latent = _dot(e_scr[...], ckvb_scr[...])
    for p in range(D_HEADS // 2):
        halves = []
        for half in range(2):
            h = 2 * p + half
            rows = slice(h * nq, (h + 1) * nq)
            o = _dot(latent[rows].astype(BF16), wv_ref[h]) + _dot(e_new[h], vn_ref[0, :, head(p)])
            halves.append(o * inv_den[h])
        o_ref[0, :, head(p)] = jnp.where(lo, halves[0], halves[1]).astype(BF16)


def _mla_sample(q, ckv, kpe, kn, vn, ow):
    nb, nq, _ = q.shape
    past = ckv.shape[1]
    assert past % MLA_CACHE_ROWS == 0
    per_b = lambda r, w: pl.BlockSpec((1, r, w), lambda b: (b, 0, 0))
    consts = [ow["place"], ow["wk_dense"], ow["kseg_t"], ow["q_absorb"], ow["wv_lanes"]]
    rows = D_HEADS * nq
    return pl.pallas_call(
        _mla_sample_kernel,
        grid=(nb,),
        in_specs=[per_b(nq, D_HEADS * LANES), per_b(past, D_KV_LORA), per_b(past, D_ROPE),
                  per_b(nq, D_HEADS * LANES), per_b(nq, D_HEADS * D_V)] + [_resident(a.shape) for a in consts],
        out_specs=per_b(nq, D_HEADS * D_V),
        out_shape=jax.ShapeDtypeStruct((nb, nq, D_HEADS * D_V), BF16),
        scratch_shapes=[pltpu.VMEM((past, D_KV_LORA), BF16), pltpu.VMEM((past, LANES), BF16),
                        pltpu.VMEM((past, D_HEADS * D_NOPE), BF16), pltpu.VMEM((rows, past), F32),
                        pltpu.VMEM((rows, past), F32), pltpu.VMEM((rows, past), BF16)],
        compiler_params=_cparams("arbitrary"),
        name="mla_sample",
    )(q, ckv, kpe, kn, vn, *consts)


def _even_layer(xp, xs, nb, seq, ndb, dseq, norm_g, w_in, w_out, a_qn, a_kn, a_sinks, b_qn, b_kn, b_rel,
                t5_table, ck_a, cv_a, ck_b, cv_b, ffn2):
    w_in = _to_bf16(w_in)
    w_out = _to_bf16(w_out)
    w_in_p = jnp.concatenate([_pair_heads(w_in[:, :A_Q], 1), w_in[:, A_Q:]], axis=1)
    woa = _pair_heads(w_out[:A_Q], 0)
    wob = w_out[A_Q:]
    ones = lambda n: jnp.ones((n,), F32)
    scale = HEAD_DIM ** -0.5 * LOG2E
    gain_row = jnp.concatenate([jnp.tile(a_qn, A_HEADS) * scale, jnp.tile(a_kn, A_KV_HEADS), ones(A_KV),
                                jnp.tile(b_qn, B_HEADS) * scale, jnp.tile(b_kn, B_HEADS), ones(B_QKV)]
                               ).reshape(1, EVEN_IN).astype(F32)
    bias_a = _bias_a(t5_table) * LOG2E
    bias_b = _bias_b(b_rel) * LOG2E
    sink = jnp.broadcast_to(jnp.repeat(a_sinks.astype(F32) * LOG2E, PAIR_ROWS)[:, None],
                            (A_HEADS * PAIR_ROWS, LANES))
    la, lb = min(WINDOW, seq), min(B_REACH, seq)

    aq, ak, av, bq, bk, bv, cak, cav, cbk, cbv = _even_inproj(xp, norm_g, w_in_p, gain_row, nb, la, lb)
    r3 = lambda a: a.reshape(nb, seq, a.shape[-1])
    ya = _attn_a(r3(aq), r3(ak), r3(av), bias_a, sink, WINDOW, True)
    yb = _attn_b(r3(bq), r3(bk), r3(bv), bias_b, B_REACH, True)
    xp = _ffn(xp, *ffn2, out_proj=(ya.reshape(nb * seq, A_Q), yb.reshape(nb * seq, B_QKV), woa, wob))
    st_p = (cak.reshape(nb, la, A_KV_HEADS, HEAD_DIM), cav.reshape(nb, la, A_KV_HEADS, HEAD_DIM),
            cbk.reshape(nb, lb, B_HEADS, HEAD_DIM), cbv.reshape(nb, lb, B_HEADS, HEAD_DIM))

    ts = ndb * dseq
    aq, ak, av, bq, bk, bv, nak, nav, nbk, nbv = _even_inproj(xs, norm_g, w_in_p, gain_row, 1, ts, ts)
    pad_q = lambda a, rows: jnp.pad(a.reshape(ndb, dseq, a.shape[-1]), ((0, 0), (0, rows - dseq), (0, 0)))

    def window(cache, new, rows):
        w = cache.shape[-2] * cache.shape[-1]
        full = jnp.concatenate([cache.reshape(ndb, -1, w), new.reshape(ndb, dseq, w)], axis=1)
        buf = jnp.pad(full, ((0, 0), (0, rows - dseq), (0, 0))).astype(BF16)
        return full[:, dseq:].reshape(cache.shape), buf

    st_ak, kbuf_a = window(ck_a, nak, PAIR_ROWS)
    st_av, vbuf_a = window(cv_a, nav, PAIR_ROWS)
    st_bk, kbuf_b = window(ck_b, nbk, B_STEP_ROWS)
    st_bv, vbuf_b = window(cv_b, nbv, B_STEP_ROWS)
    ya = _attn_a(pad_q(aq, PAIR_ROWS), kbuf_a, vbuf_a, bias_a, sink, WINDOW, False)[:, :dseq]
    yb = _attn_b(pad_q(bq, B_STEP_ROWS), kbuf_b, vbuf_b, bias_b, B_REACH, False)[:, :dseq]
    xs = _ffn(xs, *ffn2, out_proj=(ya.reshape(ts, A_Q), yb.reshape(ts, B_QKV), woa, wob))
    return xp, xs, st_p, (st_ak, st_av, st_bk, st_bv)


def _odd_layer(xp, xs, nb, seq, ndb, dseq, past, norm_g, ow, w_out, conv_prev, c_ckv, c_kpe, ffn2):
    w_out = _to_bf16(w_out)
    woc = w_out[:C_WIDTH]
    wod = w_out[C_WIDTH:]

    zero_init = jnp.zeros((nb, 8, C_WIDTH), F32)
    yc, q, k, v, ckv, kpe, cs = _odd_inproj(xp, norm_g, ow, zero_init, nb, 0)
    r3 = lambda a: a.reshape(nb, seq, a.shape[-1])
    yd = _mla_prompt(r3(q), r3(k), r3(v))
    xp = _ffn(xp, *ffn2, out_proj=(yc, yd.reshape(nb * seq, D_HEADS * D_V), woc, wod))
    st_p = (cs, ckv.reshape(nb, seq, D_KV_LORA), kpe.reshape(nb, seq, D_ROPE))

    ts = ndb * dseq
    init = jnp.pad(conv_prev.astype(F32), ((0, 0), (8 - (CONV_W - 1), 0), (0, 0)))
    yc, q, kn, vn, ckv, kpe, cs = _odd_inproj(xs, norm_g, ow, init, ndb, past)
    s3 = lambda a: a.reshape(ndb, dseq, a.shape[-1])
    yd = _mla_sample(s3(q), c_ckv, c_kpe, s3(kn), s3(vn), ow)
    xs = _ffn(xs, *ffn2, out_proj=(yc, yd.reshape(ts, D_HEADS * D_V), woc, wod))
    st_s = (cs, ckv.reshape(ndb, dseq, D_KV_LORA), kpe.reshape(ndb, dseq, D_ROPE))
    return xp, xs, st_p, st_s


def kernel(x_prompt, x_sample, cache_a_k, cache_a_v, cache_b_k, cache_b_v, state_c_conv, cache_d_ckv, cache_d_kpe, ff1_norm, ff1_w_gu, ff1_w_down, mix_norm, ff2_norm, ff2_w_gu, ff2_w_down, t5_bias_table, ev_w_in, ev_w_out, a_q_norm, a_k_norm, a_sinks, b_q_norm, b_k_norm, b_rel_bias, od_w_in, od_w_out, c_conv_w, d_q_a_norm, d_w_q_b, d_kv_a_norm, d_w_kv_b, d_q_nope_norm, d_q_rope_norm, d_k_nope_norm, d_k_rope_norm):
    nb, seq, _ = x_prompt.shape
    ndb, dseq, _ = x_sample.shape
    past = cache_d_ckv.shape[2]
    depth = ff1_norm.shape[0]
    assert seq % TOKEN_TILE == 0 and dseq == CHUNK and past % CHUNK == 0
    xp = x_prompt.reshape(nb * seq, D_MODEL)
    xs = x_sample.reshape(ndb * dseq, D_MODEL)
    even_p, even_s, odd_p, odd_s = [], [], [], []
    for l in range(depth):
        i = l // 2
        ff1 = _ffn_weights(ff1_w_gu, ff1_w_down, l)
        ffn2 = (ff2_norm[l],) + _ffn_weights(ff2_w_gu, ff2_w_down, l)
        xp = _ffn(xp, ff1_norm[l], *ff1)
        xs = _ffn(xs, ff1_norm[l], *ff1)
        if l % 2 == 0:
            xp, xs, sp, ss = _even_layer(
                xp, xs, nb, seq, ndb, dseq, mix_norm[l], ev_w_in[i], ev_w_out[i], a_q_norm[i], a_k_norm[i],
                a_sinks[i], b_q_norm[i], b_k_norm[i], b_rel_bias[i], t5_bias_table,
                cache_a_k[i], cache_a_v[i], cache_b_k[i], cache_b_v[i], ffn2)
            even_p.append(sp)
            even_s.append(ss)
        else:
            ow = _odd_weights(od_w_in[i], c_conv_w[i], d_q_a_norm[i], d_w_q_b[i], d_kv_a_norm[i], d_w_kv_b[i],
                              d_q_nope_norm[i], d_q_rope_norm[i], d_k_nope_norm[i], d_k_rope_norm[i])
            xp, xs, sp, ss = _odd_layer(xp, xs, nb, seq, ndb, dseq, past, mix_norm[l], ow, od_w_out[i],
                                        state_c_conv[i], cache_d_ckv[i], cache_d_kpe[i], ffn2)
            odd_p.append(sp)
            odd_s.append(ss)
    stack = lambda group, j: group[0][j][None] if len(group) == 1 else jnp.stack([g[j] for g in group])
    return (xp.reshape(nb, seq, D_MODEL), xs.reshape(ndb, dseq, D_MODEL),
            stack(even_p, 0), stack(even_p, 1), stack(even_p, 2), stack(even_p, 3),
            stack(odd_p, 0), stack(odd_p, 1), stack(odd_p, 2),
            stack(even_s, 0), stack(even_s, 1), stack(even_s, 2), stack(even_s, 3),
            stack(odd_s, 0), stack(odd_s, 1), stack(odd_s, 2))
```

```python
import functools
import math

import jax
import jax.numpy as jnp
from jax import lax
from jax.experimental import pallas as pl
from jax.experimental.pallas import tpu as pltpu

F32 = jnp.float32
BF16 = jnp.bfloat16

D_MODEL = 1024
CHUNK = 64
HEAD_DIM = 64
EPS = 1e-6
A_HEADS = 8
A_KV_HEADS = 2
WINDOW = 128
T5_BUCKETS = 32
T5_MAX_DIST = 128
B_HEADS = 8
B_REACH = 512
B_MAX_REL = 128
C_WIDTH = 512
CONV_W = 3
D_HEADS = 8
D_Q_LORA = 256
D_KV_LORA = 128
D_NOPE = 64
D_ROPE = 32
D_V = 64
ROPE_THETA = 10000.0
FFN_DIM = 2816
A_Q = A_HEADS * HEAD_DIM
A_KV = A_KV_HEADS * HEAD_DIM
B_QKV = B_HEADS * HEAD_DIM
EVEN_IN = A_Q + 2 * A_KV + 3 * B_QKV

LANES = 128
TOKEN_TILE = 512
FFN_TILE = 1024
CAST_BLOCK_BYTES = 6 * 1024 * 1024
MXU_COLS = 256
FFN_CHUNK = MXU_COLS
PAIR_ROWS = 2 * CHUNK
MLA_QBLOCK = 512
VMEM_LIMIT_BYTES = 56 * 1024 * 1024
LOG2E = math.log2(math.e)
MLA_QSCALE = (D_NOPE + D_ROPE) ** -0.5 * LOG2E
NEG_INF = float("-inf")

_NT = (((1,), (1,)), ((), ()))


def _cparams(*sem):
    return pltpu.CompilerParams(dimension_semantics=sem, vmem_limit_bytes=VMEM_LIMIT_BYTES)


def _resident(shape):
    zeros = (0,) * len(shape)
    return pl.BlockSpec(shape, lambda *_: zeros, pipeline_mode=pl.Buffered(1))


def _rms(x, g):
    ms = jnp.mean(x * x, axis=-1, keepdims=True)
    return (x * lax.rsqrt(ms + EPS)) * g


def _dot(a, b):
    return jnp.dot(a, b, preferred_element_type=F32)


def _ffn_kernel(*refs, n_chunks, fused_out):
    if fused_out:
        (x_ref, ya_ref, yb_ref, woa_ref, wob_ref, g_ref, wgu_ref, wd_ref,
         o_ref, h_scr, a_scr) = refs
        y = _dot(ya_ref[...], woa_ref[...]) + _dot(yb_ref[...], wob_ref[...])
        o_ref[...] = x_ref[...] + y
        res_ref = o_ref
    else:
        x_ref, g_ref, wgu_ref, wd_ref, o_ref, h_scr, a_scr = refs
        res_ref = x_ref
    h_scr[...] = _rms(res_ref[...], g_ref[...]).astype(BF16)
    for j in range(n_chunks):
        h = h_scr[...]
        g = _dot(h, wgu_ref[:, j * FFN_CHUNK:(j + 1) * FFN_CHUNK])
        u = _dot(h, wgu_ref[:, FFN_DIM + j * FFN_CHUNK:FFN_DIM + (j + 1) * FFN_CHUNK])
        a_scr[:, j * FFN_CHUNK:(j + 1) * FFN_CHUNK] = ((g * jax.nn.sigmoid(g)) * u).astype(BF16)
    o_ref[...] = res_ref[...] + 0.5 * _dot(a_scr[...], wd_ref[...])


def _ffn(x, norm_g, wgu, wd, out_proj=None):
    t = x.shape[0]
    tm = FFN_TILE if t % FFN_TILE == 0 else min(TOKEN_TILE, t)
    n_chunks = FFN_DIM // FFN_CHUNK
    row = lambda w: pl.BlockSpec((tm, w), lambda i: (i, 0))
    in_specs = [row(D_MODEL)]
    args = [x]
    if out_proj is not None:
        ya, yb, woa, wob = out_proj
        in_specs += [row(ya.shape[1]), row(yb.shape[1]), _resident(woa.shape), _resident(wob.shape)]
        args += [ya, yb, woa, wob]
    in_specs += [_resident((1, D_MODEL)), _resident(wgu.shape), _resident(wd.shape)]
    args += [norm_g.reshape(1, D_MODEL), wgu, wd]
    return pl.pallas_call(
        functools.partial(_ffn_kernel, n_chunks=n_chunks, fused_out=out_proj is not None),
        grid=(t // tm,),
        in_specs=in_specs,
        out_specs=row(D_MODEL),
        out_shape=jax.ShapeDtypeStruct((t, D_MODEL), F32),
        scratch_shapes=[pltpu.VMEM((tm, D_MODEL), BF16), pltpu.VMEM((tm, FFN_DIM), BF16)],
        compiler_params=_cparams("arbitrary"),
        name="ffn_out" if out_proj is not None else "ffn",
    )(*args)


def _cast_kernel(x_ref, o_ref):
    o_ref[...] = x_ref[...].astype(BF16)


def _to_bf16(w, layer=None):
    r, c = w.shape[-2:]
    fits = [d for d in range(16, r + 1, 16) if r % d == 0 and d * c * 4 <= CAST_BLOCK_BYTES]
    tr = max(fits) if fits else r
    out_spec = pl.BlockSpec((tr, c), lambda i: (i, 0))
    in_spec = out_spec if layer is None else pl.BlockSpec((None, tr, c), lambda i: (layer, i, 0))
    return pl.pallas_call(
        _cast_kernel, grid=(r // tr,), in_specs=[in_spec], out_specs=out_spec,
        out_shape=jax.ShapeDtypeStruct((r, c), BF16),
        compiler_params=_cparams("arbitrary"), name="cast_bf16",
    )(w)


def _ffn_weights(w_gu, w_down, layer):
    return _to_bf16(w_gu, layer), _to_bf16(w_down, layer)


def _norm_halves(blk, gain):
    sq = blk * blk
    lo = lax.broadcasted_iota(jnp.int32, blk.shape, 1) < HEAD_DIM
    s_lo = jnp.sum(jnp.where(lo, sq, 0.0), axis=-1, keepdims=True)
    s_hi = jnp.sum(jnp.where(lo, 0.0, sq), axis=-1, keepdims=True)
    inv = jnp.where(lo, lax.rsqrt(s_lo * (1.0 / HEAD_DIM) + EPS), lax.rsqrt(s_hi * (1.0 / HEAD_DIM) + EPS))
    return (blk * inv) * gain


def _even_inproj_kernel(x_ref, g_ref, w_ref, gain_ref,
                        aq_ref, ak_ref, av_ref, bq_ref, bk_ref, bv_ref,
                        cak_ref, cav_ref, cbk_ref, cbv_ref, h_scr, z_scr, *, tm, tpb, rows_a, rows_b):
    i = pl.program_id(0)
    last = (i % tpb) == (tpb - 1)
    h_scr[...] = _rms(x_ref[...], g_ref[...]).astype(BF16)

    def lanes(c):
        return slice(c * LANES, (c + 1) * LANES)

    def emit(c):
        blk = z_scr[:, lanes(c)]
        gain = gain_ref[:, lanes(c)]
        if c < 4:
            aq_ref[:, lanes(c)] = _norm_halves(blk, gain).astype(BF16)
        elif c == 4:
            kn = _norm_halves(blk, gain)
            ak_ref[...] = kn.astype(BF16)
            z_scr[:, lanes(c)] = kn
        elif c == 5:
            av_ref[...] = blk.astype(BF16)
        elif c < 10:
            bq_ref[:, lanes(c - 6)] = _norm_halves(blk, gain).astype(BF16)
        elif c < 14:
            kn = _norm_halves(blk, gain)
            bk_ref[:, lanes(c - 10)] = kn.astype(BF16)
            z_scr[:, lanes(c)] = kn
        else:
            bv_ref[:, lanes(c - 14)] = blk.astype(BF16)

    for grp in range(EVEN_IN // MXU_COLS):
        cols = slice(grp * MXU_COLS, (grp + 1) * MXU_COLS)
        z_scr[:, cols] = _dot(h_scr[...], w_ref[:, cols])
        emit(2 * grp)
        emit(2 * grp + 1)

    @pl.when(last)
    def _():
        cak_ref[0] = z_scr[tm - rows_a:, lanes(4)]
        cav_ref[0] = z_scr[tm - rows_a:, lanes(5)]
        cbk_ref[0] = z_scr[tm - rows_b:, 10 * LANES:14 * LANES]
        cbv_ref[0] = z_scr[tm - rows_b:, 14 * LANES:18 * LANES]


def _even_inproj(x, norm_g, w_in, gain_row, n_batch, rows_a, rows_b):
    t = x.shape[0]
    tm = min(TOKEN_TILE, t)
    tpb = t // n_batch // tm
    assert tpb * tm * n_batch == t and rows_a <= tm and rows_b <= tm
    row = lambda w: pl.BlockSpec((tm, w), lambda i: (i, 0))
    cache = lambda r, w: pl.BlockSpec((1, r, w), lambda i: (i // tpb, 0, 0))
    bshape = lambda w: jax.ShapeDtypeStruct((t, w), BF16)
    cshape = lambda r, w: jax.ShapeDtypeStruct((n_batch, r, w), F32)
    return pl.pallas_call(
        functools.partial(_even_inproj_kernel, tm=tm, tpb=tpb, rows_a=rows_a, rows_b=rows_b),
        grid=(t // tm,),
        in_specs=[row(D_MODEL), _resident((1, D_MODEL)), _resident(w_in.shape), _resident(gain_row.shape)],
        out_specs=[row(A_Q), row(A_KV), row(A_KV), row(B_QKV), row(B_QKV), row(B_QKV),
                   cache(rows_a, A_KV), cache(rows_a, A_KV), cache(rows_b, B_QKV), cache(rows_b, B_QKV)],
        out_shape=[bshape(A_Q), bshape(A_KV), bshape(A_KV), bshape(B_QKV), bshape(B_QKV), bshape(B_QKV),
                   cshape(rows_a, A_KV), cshape(rows_a, A_KV), cshape(rows_b, B_QKV), cshape(rows_b, B_QKV)],
        scratch_shapes=[pltpu.VMEM((tm, D_MODEL), BF16), pltpu.VMEM((tm, EVEN_IN), F32)],
        compiler_params=_cparams("arbitrary"),
        name="even_inproj",
    )(x, norm_g.reshape(1, D_MODEL), w_in, gain_row)


def _pair_heads(a, axis):
    shape = a.shape
    split = shape[:axis] + (2, A_HEADS // 2, HEAD_DIM) + shape[axis + 1:]
    return a.reshape(split).swapaxes(axis, axis + 1).reshape(shape)


def _fill_padded(buf, src_ref, ctx):
    buf[0:ctx, :] = jnp.zeros((ctx, buf.shape[1]), buf.dtype)
    buf[ctx:, :] = src_ref[0]


def _windows(k_ref, v_ref, pad_scr, ctx, pad_front, step_rows, stream):
    win = ctx + step_rows
    if pad_front:
        kbuf, vbuf = pad_scr
        _fill_padded(kbuf, k_ref, ctx)
        _fill_padded(vbuf, v_ref, ctx)
        return (lambda r0: kbuf[pl.ds(r0, win), :]), (lambda r0: vbuf[pl.ds(r0, win), :])
    return (lambda r0: k_ref[stream, pl.ds(r0, win), :]), (lambda r0: v_ref[stream, pl.ds(r0, win), :])


def _fold_lanes(x, op):
    acc = x[:, :LANES]
    for c in range(1, x.shape[1] // LANES):
        acc = op(acc, x[:, c * LANES:(c + 1) * LANES])
    return acc


def _row_stat(reduce_fn, x):
    return jnp.broadcast_to(reduce_fn(x, axis=-1, keepdims=True), x.shape)


PAIR_SLOTS = 4


def _run_steps(make_stages, n_streams, n_steps, n_front, step_rows, lookahead):
    stages = [make_stages(stream) for stream in range(n_streams)]
    tasks = [(stream, t) for stream in range(n_streams) for t in range(n_steps)]

    def run(stage, k):
        stream, t = tasks[k]
        stages[stream][stage](t * step_rows, k % PAIR_SLOTS, t < n_front)

    n = len(tasks)
    if not lookahead:
        for k in range(n):
            for stage in range(3):
                run(stage, k)
        return
    run(0, 0)
    if n > 1:
        run(0, 1)
    run(1, 0)
    for k in range(n):
        if k + 2 < n:
            run(0, k + 2)
        if k + 1 < n:
            run(1, k + 1)
        run(2, k)


def _band_bias(ext_row, n_rows, ctx):
    win = ctx + n_rows
    period = ext_row.shape[1]
    t = pltpu.roll(jnp.broadcast_to(ext_row, (n_rows, period)), 0, 1, stride=1, stride_axis=0)[:, :win]
    chunk = lax.broadcasted_iota(jnp.int32, (n_rows, win), 0) // CHUNK
    col = lax.broadcasted_iota(jnp.int32, (n_rows, win), 1)
    visible = (col >= CHUNK * chunk) & (col < ctx + CHUNK * (chunk + 1))
    return jnp.where(visible, t, NEG_INF)


def _attn_a_kernel(q_ref, k_ref, v_ref, ext_ref, sink_ref, o_ref, bias_ref, s_scr, e_scr, *pad_scr,
                   nq, ctx, pad_front):
    win = ctx + PAIR_ROWS
    lo = lax.broadcasted_iota(jnp.int32, (PAIR_ROWS, LANES), 1) < HEAD_DIM

    @pl.when(pl.program_id(0) == 0)
    def _():
        for head in range(A_HEADS):
            bias_ref[head * PAIR_ROWS:(head + 1) * PAIR_ROWS, :] = _band_bias(
                ext_ref[head:head + 1, :], PAIR_ROWS, ctx)

    def make_stages(stream):
        kwin, vwin = _windows(k_ref, v_ref, pad_scr, ctx, pad_front, PAIR_ROWS, stream)
        return (functools.partial(scores, stream, kwin), functools.partial(softmax, stream),
                functools.partial(values, stream, vwin))

    def scores(stream, kwin, r0, slot, masked):
        q_lo, q_hi = [], []
        for p in range(4):
            qp = q_ref[stream, pl.ds(r0, PAIR_ROWS), p * LANES:(p + 1) * LANES]
            zero = jnp.zeros_like(qp)
            q_lo.append(jnp.where(lo, qp, zero))
            q_hi.append(jnp.where(lo, zero, qp))
        q2 = jnp.concatenate(q_lo + q_hi, axis=0)
        s_scr[slot] = lax.dot_general(q2, kwin(r0), _NT, preferred_element_type=F32)

    def softmax(stream, r0, slot, masked):
        for head in range(A_HEADS):
            rows = slice(head * PAIR_ROWS, (head + 1) * PAIR_ROWS)
            s = s_scr[slot, rows, :] + bias_ref[rows, :]
            if masked:
                col = lax.broadcasted_iota(jnp.int32, (PAIR_ROWS, win), 1)
                s = jnp.where(col + r0 >= ctx, s, NEG_INF)
            sk = sink_ref[rows, :]
            m = jnp.maximum(_row_stat(jnp.max, _fold_lanes(s, jnp.maximum)), sk)
            e = [jnp.exp2(s[:, c * LANES:(c + 1) * LANES] - m) for c in range(win // LANES)]
            den = _row_stat(jnp.sum, functools.reduce(jnp.add, e)) + jnp.exp2(sk - m)
            inv = 1.0 / den
            for c in range(win // LANES):
                e_scr[slot, rows, c * LANES:(c + 1) * LANES] = (e[c] * inv).astype(BF16)

    def values(stream, vwin, r0, slot, masked):
        o2 = _dot(e_scr[slot], vwin(r0))
        half = 4 * PAIR_ROWS
        for p in range(4):
            o = jnp.where(lo, o2[p * PAIR_ROWS:(p + 1) * PAIR_ROWS],
                          o2[half + p * PAIR_ROWS:half + (p + 1) * PAIR_ROWS])
            o_ref[stream, pl.ds(r0, PAIR_ROWS), p * LANES:(p + 1) * LANES] = o.astype(BF16)

    _run_steps(make_stages, q_ref.shape[0], nq // PAIR_ROWS, ctx // PAIR_ROWS if pad_front else 0, PAIR_ROWS,
               lookahead=False)


def _streams_per_block(nb, pad_front):
    return 1 if pad_front else nb


def _attn_a(q, k, v, ext, sink, ctx, pad_front):
    nb, nq, _ = q.shape
    nk = k.shape[1]
    assert nq % PAIR_ROWS == 0 and nk == (nq if pad_front else ctx + nq)
    bb = _streams_per_block(nb, pad_front)
    per_b = lambda r, w: pl.BlockSpec((bb, r, w), lambda b: (b, 0, 0))
    rows, win = A_HEADS * PAIR_ROWS, ctx + PAIR_ROWS
    scratch = [pltpu.VMEM((rows, win), F32),
               pltpu.VMEM((PAIR_SLOTS, rows, win), F32), pltpu.VMEM((PAIR_SLOTS, rows, win), BF16)]
    if pad_front:
        scratch += [pltpu.VMEM((ctx + nq, A_KV), BF16)] * 2
    return pl.pallas_call(
        functools.partial(_attn_a_kernel, nq=nq, ctx=ctx, pad_front=pad_front),
        grid=(nb // bb,),
        in_specs=[per_b(nq, A_Q), per_b(nk, A_KV), per_b(nk, A_KV), _resident(ext.shape), _resident(sink.shape)],
        out_specs=per_b(nq, A_Q),
        out_shape=jax.ShapeDtypeStruct((nb, nq, A_Q), BF16),
        scratch_shapes=scratch,
        compiler_params=_cparams("arbitrary"),
        name="attn_a",
    )(q, k, v, ext, sink)


B_STEP_ROWS = 4 * CHUNK


def _attn_b_kernel(q_ref, k_ref, v_ref, ext_ref, o_ref, bias_scr, s_scr, e_scr, r_scr, *pad_scr,
                   nq, ctx, pad_front):
    win = ctx + B_STEP_ROWS
    lo = lax.broadcasted_iota(jnp.int32, (B_STEP_ROWS, LANES), 1) < HEAD_DIM
    chunks = B_STEP_ROWS // CHUNK
    pair_idx = pl.program_id(1)

    @pl.when(pl.program_id(0) == 0)
    def _():
        for half in range(2):
            bias_scr[pair_idx, half * B_STEP_ROWS:(half + 1) * B_STEP_ROWS, :] = _band_bias(
                ext_ref[0, half:half + 1, :], B_STEP_ROWS, ctx)

    bias_ref = bias_scr.at[pair_idx]

    def make_stages(stream):
        kwin, vwin = _windows(k_ref, v_ref, pad_scr, ctx, pad_front, B_STEP_ROWS, stream)
        return (functools.partial(scores, stream, kwin), functools.partial(softmax, stream),
                functools.partial(values, stream, vwin))

    def scores(stream, kwin, r0, slot, masked):
        qp = q_ref[stream, pl.ds(r0, B_STEP_ROWS), :]
        zero = jnp.zeros_like(qp)
        q2 = jnp.concatenate([jnp.where(lo, qp, zero), jnp.where(lo, zero, qp)], axis=0)
        s_scr[slot] = lax.dot_general(q2, kwin(r0), _NT, preferred_element_type=F32)

    def softmax(stream, r0, slot, masked):
        for rb in range(2 * chunks):
            rows = slice(rb * CHUNK, (rb + 1) * CHUNK)
            ci = rb % chunks
            c_lo, c_hi = CHUNK * ci // LANES, -(-(ctx + CHUNK * (ci + 1)) // LANES)
            band = slice(c_lo * LANES, c_hi * LANES)
            s = s_scr[slot, rows, band] + bias_ref[rows, band]
            if masked:
                col = lax.broadcasted_iota(jnp.int32, s.shape, 1) + c_lo * LANES
                s = jnp.where(col + r0 >= ctx, s, NEG_INF)
            m = _row_stat(jnp.max, _fold_lanes(s, jnp.maximum))
            e = [jnp.exp2(s[:, c * LANES:(c + 1) * LANES] - m) for c in range(c_hi - c_lo)]
            for c in range(win // LANES):
                blk = e[c - c_lo].astype(BF16) if c_lo <= c < c_hi else jnp.zeros((CHUNK, LANES), BF16)
                e_scr[slot, rows, c * LANES:(c + 1) * LANES] = blk
            r_scr[slot, rows, :] = 1.0 / _row_stat(jnp.sum, functools.reduce(jnp.add, e))

    def values(stream, vwin, r0, slot, masked):
        o2 = _dot(e_scr[slot], vwin(r0)) * r_scr[slot]
        o = jnp.where(lo, o2[:B_STEP_ROWS], o2[B_STEP_ROWS:])
        o_ref[stream, pl.ds(r0, B_STEP_ROWS), :] = o.astype(BF16)

    _run_steps(make_stages, q_ref.shape[0], nq // B_STEP_ROWS,
               -(-ctx // B_STEP_ROWS) if pad_front else 0, B_STEP_ROWS, lookahead=True)


def _attn_b(q, k, v, ext, ctx, pad_front):
    nb, nq, _ = q.shape
    nk = k.shape[1]
    assert nq % B_STEP_ROWS == 0 and nk == (nq if pad_front else ctx + nq)
    bb = _streams_per_block(nb, pad_front)
    blk = lambda r: pl.BlockSpec((bb, r, LANES), lambda b, p: (b, 0, p))
    rows, win = 2 * B_STEP_ROWS, ctx + B_STEP_ROWS
    scratch = [pltpu.VMEM((B_HEADS // 2, rows, win), F32),
               pltpu.VMEM((PAIR_SLOTS, rows, win), F32), pltpu.VMEM((PAIR_SLOTS, rows, win), BF16),
               pltpu.VMEM((PAIR_SLOTS, rows, LANES), F32)]
    if pad_front:
        scratch += [pltpu.VMEM((ctx + nq, LANES), BF16)] * 2
    return pl.pallas_call(
        functools.partial(_attn_b_kernel, nq=nq, ctx=ctx, pad_front=pad_front),
        grid=(nb // bb, B_HEADS // 2),
        in_specs=[blk(nq), blk(nk), blk(nk),
                  pl.BlockSpec((1,) + ext.shape[1:], lambda b, p: (p, 0, 0))],
        out_specs=blk(nq),
        out_shape=jax.ShapeDtypeStruct((nb, nq, B_QKV), BF16),
        scratch_shapes=scratch,
        compiler_params=_cparams("arbitrary", "arbitrary"),
        name="attn_b",
    )(q, k, v, ext)


def _t5_bucket(rel):
    nb = T5_BUCKETS // 2
    max_exact = nb // 2
    n = -rel
    ret = jnp.where(n < 0, nb, 0)
    n = jnp.abs(n)
    nf = jnp.maximum(n, 1).astype(F32)
    large = max_exact + (jnp.log(nf / max_exact) / math.log(T5_MAX_DIST / max_exact)
                         * (nb - max_exact)).astype(jnp.int32)
    large = jnp.minimum(large, nb - 1)
    return ret + jnp.where(n < max_exact, n, large)


def _band_values(ctx, n_rows, value_of_rel):
    n_cols = ctx + n_rows
    period = -(-(n_cols + n_rows - 1) // LANES) * LANES
    d = jnp.concatenate([jnp.arange(0, period - (n_rows - 1)), jnp.arange(-(n_rows - 1), 0)])
    return value_of_rel(d - ctx)


def _bias_a(t5_table):
    return _band_values(WINDOW, PAIR_ROWS, lambda rel: t5_table.astype(F32)[_t5_bucket(rel)].T)


def _bias_b(b_rel):
    ext = _band_values(B_REACH, B_STEP_ROWS,
                       lambda rel: b_rel.astype(F32)[:, jnp.clip(rel, -B_MAX_REL, B_MAX_REL) + B_MAX_REL])
    return ext.reshape(B_HEADS // 2, 2, ext.shape[-1])


ODD_IN_PAD = 2048
ROPE_HALF = D_ROPE // 2
X1_LO = LANES // 2
NOPE_SPLIT = X1_LO - ROPE_HALF


def _head_lane_source():
    zero = D_NOPE + D_ROPE
    src = []
    for lane in range(LANES):
        if lane < ROPE_HALF:
            src.append(D_NOPE + ROPE_HALF + lane)
        elif lane < X1_LO:
            src.append(lane - ROPE_HALF)
        elif lane < X1_LO + ROPE_HALF:
            src.append(D_NOPE + lane - X1_LO)
        elif lane < X1_LO + ROPE_HALF + D_NOPE - NOPE_SPLIT:
            src.append(NOPE_SPLIT + lane - X1_LO - ROPE_HALF)
        else:
            src.append(zero)
    return src


def _to_head_lanes(a):
    padded = jnp.concatenate([a, jnp.zeros(a.shape[:-1] + (1,), a.dtype)], axis=-1)
    return padded[..., jnp.asarray(_head_lane_source(), dtype=jnp.int32)]


def _lane_masks(shape):
    lane = lax.broadcasted_iota(jnp.int32, shape, 1)
    rope = (lane < ROPE_HALF) | ((lane >= X1_LO) & (lane < X1_LO + ROPE_HALF))
    used = lane < X1_LO + ROPE_HALF + D_NOPE - NOPE_SPLIT
    return used & ~rope, rope


def _rope(y, cosf, sinf):
    return y * cosf + pltpu.roll(y, LANES // 2, 1) * sinf


ROW_BLOCK = 128


def _row_blocks(tm):
    rb = min(ROW_BLOCK, tm)
    return [slice(r, r + rb) for r in range(0, tm, rb)]


def _segment_mean_squares(src_scr, cols, seg_ref, sq_scr, ms_scr, slot):
    for rows in _row_blocks(src_scr.shape[0]):
        v = src_scr[rows, cols]
        sq_scr[slot, rows, :] = (v * v).astype(BF16)
    ms_scr[slot] = _dot(sq_scr[slot], seg_ref[...])


def _expand_kv(ckvb_ref, kpe_blk_ref, wkv_ref, kgain_ref, kseg_ref, kv_scr, sq_scr, ms_scr, k_ref, v_ref,
               between=None):
    k_cols = D_HEADS * LANES
    tm = kv_scr.shape[0]
    for grp in range(kv_scr.shape[1] // MXU_COLS):
        if between and grp in between:
            between[grp]()
        cols = slice(grp * MXU_COLS, (grp + 1) * MXU_COLS)
        kv_scr[:, cols] = _dot(ckvb_ref[...], wkv_ref[:, cols])
        if cols.start < k_cols:
            slot = grp % 2
            _segment_mean_squares(kv_scr, cols, kseg_ref, sq_scr, ms_scr, slot)
        for rows in _row_blocks(tm):
            if cols.start < k_cols:
                for half in range(2):
                    h = 2 * grp + half
                    kb = kv_scr[rows, h * LANES:(h + 1) * LANES]
                    ms = ms_scr[slot, rows, half * LANES:(half + 1) * LANES]
                    kn = (kb * lax.rsqrt(ms + EPS)) * kgain_ref[...]
                    k_ref[rows, h * LANES:(h + 1) * LANES] = (kn + kpe_blk_ref[rows, :]).astype(BF16)
            else:
                v_ref[rows, cols.start - k_cols:cols.stop - k_cols] = kv_scr[rows, cols].astype(BF16)


def _odd_inproj_kernel(x_ref, g_ref, w_ref, cinit_ref, convw_ref, qan_ref, wqb_ref, qgain_ref,
                       kvan_ref, krgain_ref, invf_ref, sgn_ref, wkv_ref, kgain_ref, qseg_ref, kseg_ref,
                       yc_ref, q_ref, k_ref, v_ref, ckv_ref, kpe_ref, cs_ref,
                       h_scr, z_scr, uext_scr, qn_scr, q_scr, kv_scr, rot_scr, ckvb_scr, kpe_scr, sq_scr, ms_scr,
                       *, tm, tpb, pos0):
    i = pl.program_id(0)
    tile = i % tpb
    off = pl.multiple_of(tile * tm, tm)
    blocks = _row_blocks(tm)
    rb = blocks[0].stop

    @pl.when(tile == 0)
    def _():
        uext_scr[0:8, :] = cinit_ref[0]

    @pl.when(i < tpb)
    def _():
        row = lax.broadcasted_iota(jnp.int32, (tm, LANES), 0)
        ang = (row + (tile * tm + pos0)).astype(F32) * invf_ref[...]
        rot_scr[0, pl.ds(off, tm), :] = jnp.cos(ang)
        rot_scr[1, pl.ds(off, tm), :] = jnp.sin(ang) * sgn_ref[...]

    h_scr[...] = _rms(x_ref[...], g_ref[...]).astype(BF16)
    low_lanes = lax.broadcasted_iota(jnp.int32, (rb, LANES), 1) < ROPE_HALF

    def proj(c0, c1):
        z_scr[:, c0:c1] = _dot(h_scr[...], w_ref[:, c0:c1])

    def rotary(y, rows):
        cosf = rot_scr[0, pl.ds(off + rows.start, rb), :]
        sinf = rot_scr[1, pl.ds(off + rows.start, rb), :]
        return _rope(y, cosf, sinf)

    def conv_group(grp):
        c0 = grp * MXU_COLS
        cols = slice(c0, c0 + MXU_COLS)
        for base in (0, C_WIDTH, 2 * C_WIDTH):
            proj(base + c0, base + c0 + MXU_COLS)
        for rows in blocks:
            up = slice(rows.start + 8, rows.stop + 8)
            uext_scr[up, cols] = (z_scr[rows, C_WIDTH + c0:C_WIDTH + c0 + MXU_COLS]
                                  * z_scr[rows, 2 * C_WIDTH + c0:2 * C_WIDTH + c0 + MXU_COLS])
        for rows in blocks:
            yconv = convw_ref[0:1, cols] * uext_scr[rows.start + 6:rows.stop + 6, cols]
            yconv = yconv + convw_ref[1:2, cols] * uext_scr[rows.start + 7:rows.stop + 7, cols]
            yconv = yconv + convw_ref[2:3, cols] * uext_scr[rows.start + 8:rows.stop + 8, cols]
            yc_ref[rows, cols] = (z_scr[rows, cols] * yconv).astype(BF16)

    proj(1536, 1792)
    proj(1792, 2048)

    for rows in blocks:
        qn_scr[rows, :] = _rms(z_scr[rows, 1536:1792], qan_ref[...]).astype(BF16)
        ckv = _rms(z_scr[rows, 1792:1920], kvan_ref[...])
        ckv_ref[rows, :] = ckv
        ckvb_scr[rows, :] = ckv.astype(BF16)

    for grp in range(D_HEADS * LANES // MXU_COLS):
        cols = slice(grp * MXU_COLS, (grp + 1) * MXU_COLS)
        q_scr[:, cols] = _dot(qn_scr[...], wqb_ref[:, cols])
        slot = grp % 2
        _segment_mean_squares(q_scr, cols, qseg_ref, sq_scr, ms_scr, slot)
        for half in range(2):
            h = 2 * grp + half
            for rows in blocks:
                blk = q_scr[rows, h * LANES:(h + 1) * LANES]
                inv = lax.rsqrt(ms_scr[slot, rows, half * LANES:(half + 1) * LANES] + EPS)
                y = rotary((blk * inv) * qgain_ref[...], rows)
                q_ref[rows, h * LANES:(h + 1) * LANES] = y.astype(BF16)

    for rows in blocks:
        kb = z_scr[rows, 1920:2048]
        ms = jnp.sum(kb * kb, axis=-1, keepdims=True) * (1.0 / D_ROPE)
        kpe_blk = rotary((kb * lax.rsqrt(ms + EPS)) * krgain_ref[...], rows)
        kpe_scr[rows, :] = kpe_blk
        x1_then_x2 = jnp.where(low_lanes, pltpu.roll(kpe_blk, LANES - X1_LO, 1), pltpu.roll(kpe_blk, ROPE_HALF, 1))
        kpe_ref[rows, :] = x1_then_x2[:, 0:D_ROPE]
    _expand_kv(ckvb_scr, kpe_scr, wkv_ref, kgain_ref, kseg_ref, kv_scr, sq_scr, ms_scr, k_ref, v_ref,
               between={0: lambda: conv_group(0), 4: lambda: conv_group(1)})

    cs_ref[0] = uext_scr[tm + 6:tm + 8, :]
    uext_scr[0:8, :] = uext_scr[tm:tm + 8, :]


def _odd_inproj(x, norm_g, ow, conv_init, n_batch, pos0):
    t = x.shape[0]
    tm = min(TOKEN_TILE, t // n_batch)
    tpb = t // n_batch // tm
    assert tpb * tm * n_batch == t
    row = lambda w: pl.BlockSpec((tm, w), lambda i: (i, 0))
    per_b = lambda r, w: pl.BlockSpec((1, r, w), lambda i: (i // tpb, 0, 0))
    small = [ow["conv_w"], ow["qan"], ow["wqb"], ow["qgain"], ow["kvan"], ow["krgain"],
             ow["invf"], ow["sgn"], ow["wkv"], ow["kgain"], ow["qseg"], ow["kseg"]]
    kv_w = D_HEADS * LANES
    v_w = D_HEADS * D_V
    return pl.pallas_call(
        functools.partial(_odd_inproj_kernel, tm=tm, tpb=tpb, pos0=pos0),
        grid=(t // tm,),
        in_specs=[row(D_MODEL), _resident((1, D_MODEL)), _resident(ow["w_in"].shape), per_b(8, C_WIDTH)]
                 + [_resident(a.shape) for a in small],
        out_specs=[row(C_WIDTH), row(kv_w), row(kv_w), row(v_w), row(D_KV_LORA), row(D_ROPE),
                   per_b(CONV_W - 1, C_WIDTH)],
        out_shape=[jax.ShapeDtypeStruct((t, C_WIDTH), BF16), jax.ShapeDtypeStruct((t, kv_w), BF16),
                   jax.ShapeDtypeStruct((t, kv_w), BF16), jax.ShapeDtypeStruct((t, v_w), BF16),
                   jax.ShapeDtypeStruct((t, D_KV_LORA), F32), jax.ShapeDtypeStruct((t, D_ROPE), F32),
                   jax.ShapeDtypeStruct((n_batch, CONV_W - 1, C_WIDTH), F32)],
        scratch_shapes=[pltpu.VMEM((tm, D_MODEL), BF16), pltpu.VMEM((tm, ODD_IN_PAD), F32),
                        pltpu.VMEM((tm + 8, C_WIDTH), F32), pltpu.VMEM((tm, D_Q_LORA), BF16),
                        pltpu.VMEM((tm, kv_w), F32), pltpu.VMEM((tm, kv_w + v_w), F32),
                        pltpu.VMEM((2, tpb * tm, LANES), F32), pltpu.VMEM((tm, D_KV_LORA), BF16),
                        pltpu.VMEM((tm, LANES), F32), pltpu.VMEM((2, tm, MXU_COLS), BF16),
                        pltpu.VMEM((2, tm, MXU_COLS), F32)],
        compiler_params=_cparams("arbitrary"),
        name="odd_inproj",
    )(x, norm_g.reshape(1, D_MODEL), ow["w_in"], conv_init, *small)


def _odd_weights(w_in, conv_w, q_a_norm, w_q_b, kv_a_norm, w_kv_b, qn_nope, qn_rope, kn_nope, kn_rope):
    z = lambda n: jnp.zeros((n,), F32)
    qk = D_NOPE + D_ROPE
    w_in = _to_bf16(w_in)
    kr_blk = _to_head_lanes(jnp.concatenate([jnp.zeros((D_MODEL, D_NOPE), BF16), w_in[:, 1920:]], axis=1))
    w_pad = jnp.concatenate([w_in[:, :1920], kr_blk], axis=1)
    wqb = _to_head_lanes(w_q_b.astype(BF16).reshape(D_Q_LORA, D_HEADS, qk)).reshape(D_Q_LORA, D_HEADS * LANES)
    kvb = w_kv_b.astype(BF16).reshape(D_KV_LORA, D_HEADS, D_NOPE + D_V)
    wk = _to_head_lanes(jnp.concatenate([kvb[:, :, :D_NOPE], jnp.zeros((D_KV_LORA, D_HEADS, D_ROPE), BF16)], axis=-1))
    wkv = jnp.concatenate([wk.reshape(D_KV_LORA, D_HEADS * LANES),
                           kvb[:, :, D_NOPE:].reshape(D_KV_LORA, D_HEADS * D_V)], axis=1)
    inv = 1.0 / (ROPE_THETA ** (jnp.arange(ROPE_HALF, dtype=F32) / ROPE_HALF))
    ones = jnp.ones((ROPE_HALF,), F32)
    lanes = lambda v: _to_head_lanes(v).reshape(1, LANES)
    in_nope = lanes(jnp.concatenate([jnp.ones((D_NOPE,), F32), z(D_ROPE)]))
    in_rope = lanes(jnp.concatenate([z(D_NOPE), jnp.ones((D_ROPE,), F32)]))
    seg_nope = in_nope.T * in_nope * (1.0 / D_NOPE)
    seg_rope = in_rope.T * in_rope * (1.0 / D_ROPE)
    two_heads = lambda m: jnp.kron(jnp.eye(MXU_COLS // LANES, dtype=F32), m).astype(BF16)
    return {
        "qseg": two_heads(seg_nope + seg_rope),
        "kseg": two_heads(seg_nope),
        "wk_dense": kvb[:, :, :D_NOPE].reshape(D_KV_LORA, D_HEADS * D_NOPE),
        "kseg_t": jnp.pad(jnp.kron(jnp.eye(D_HEADS, dtype=F32), jnp.full((1, D_NOPE), 1.0 / D_NOPE, F32)),
                          ((0, 16 - D_HEADS), (0, 0))).astype(BF16),
        "q_absorb": _to_head_lanes(jnp.concatenate(
            [w_kv_b.reshape(D_KV_LORA, D_HEADS, D_NOPE + D_V)[:, :, :D_NOPE] * kn_nope,
             jnp.zeros((D_KV_LORA, D_HEADS, D_ROPE), F32)], axis=-1)).transpose(1, 2, 0).astype(BF16),
        "wv_lanes": jnp.stack([jnp.pad(kvb[:, h, D_NOPE:], ((0, 0), ((h % 2) * D_V, LANES - D_V - (h % 2) * D_V)))
                               for h in range(D_HEADS)]),
        "w_in": w_pad,
        "conv_w": conv_w.astype(F32),
        "qan": q_a_norm.reshape(1, D_Q_LORA),
        "wqb": wqb,
        "qgain": lanes(jnp.concatenate([qn_nope, qn_rope])) * MLA_QSCALE,
        "kvan": kv_a_norm.reshape(1, D_KV_LORA),
        "krgain": lanes(jnp.concatenate([z(D_NOPE), kn_rope])),
        "invf": lanes(jnp.concatenate([z(D_NOPE), inv, inv])),
        "sgn": lanes(jnp.concatenate([z(D_NOPE), -ones, ones])),
        "wkv": wkv,
        "kgain": lanes(jnp.concatenate([kn_nope, z(D_ROPE)])),
        "place": _to_head_lanes(jnp.concatenate([jnp.zeros((D_ROPE, D_NOPE), BF16), jnp.eye(D_ROPE, dtype=BF16)], axis=1)),
    }


def _mla_prompt_kernel(q_ref, k_ref, v_ref, o_ref, s_scr, e_scr, r_scr, *, seq):
    qb = min(MLA_QBLOCK, seq)
    hb = qb // 2
    lo = lax.broadcasted_iota(jnp.int32, (qb, LANES), 1) < D_V
    first_half = lax.broadcasted_iota(jnp.int32, (CHUNK, LANES), 1) < CHUNK
    scores = lambda q, k: lax.dot_general(q, k, _NT, preferred_element_type=F32)
    top, bot = slice(0, hb), slice(hb, qb)

    def spans(i):
        tk = qb * (i + 1)
        return tk - qb, tk - hb, tk

    def qk(i, h):
        t0, t1, tk = spans(i)
        hl = slice(h * LANES, (h + 1) * LANES)
        q0 = i * qb
        if t0 > 0:
            s_scr[h, :, 0:t0] = scores(q_ref[0, q0:q0 + qb, hl], k_ref[0, 0:t0, hl])
        s_scr[h, top, t0:t1] = scores(q_ref[0, q0:q0 + hb, hl], k_ref[0, t0:t1, hl])
        s_scr[h, bot, t0:tk] = scores(q_ref[0, q0 + hb:q0 + qb, hl], k_ref[0, t0:tk, hl])

    def softmax(i, h):
        t0, t1, tk = spans(i)
        for rb in range(qb // CHUNK):
            rows = slice(rb * CHUNK, (rb + 1) * CHUNK)
            visible = t0 + CHUNK * (rb + 1)
            width = t1 if rb < hb // CHUNK else tk
            n_full, ragged = visible // LANES, visible % LANES != 0
            blk = lambda c: s_scr[h, rows, c * LANES:(c + 1) * LANES]
            cols = [blk(c) for c in range(n_full)]
            if ragged:
                cols.append(jnp.where(first_half, blk(n_full), NEG_INF))
            mm = cols[0]
            for c in cols[1:]:
                mm = jnp.maximum(mm, c)
            m = _row_stat(jnp.max, mm)
            acc = None
            for c in range(width // LANES):
                if c < len(cols):
                    e = jnp.exp2(blk(c) - m)
                    if c >= n_full:
                        e = jnp.where(first_half, e, 0.0)
                    acc = e if acc is None else acc + e
                else:
                    e = jnp.zeros((CHUNK, LANES), F32)
                e_scr[h, rows, c * LANES:(c + 1) * LANES] = e.astype(BF16)
            r_scr[h, rows, :] = 1.0 / _row_stat(jnp.sum, acc)

    def pv(i, h):
        t0, t1, tk = spans(i)
        out = jnp.concatenate([_dot(e_scr[h, top, t0:t1], v_ref[0, t0:t1, :]),
                               _dot(e_scr[h, bot, t0:tk], v_ref[0, t0:tk, :])], axis=0)
        if t0 > 0:
            out = out + _dot(e_scr[h, :, 0:t0], v_ref[0, 0:t0, :])
        return out * r_scr[h]

    tasks = [(i, h) for i in range(seq // qb) for h in range(2)]
    n = len(tasks)
    outs = {}
    qk(*tasks[0])
    if n > 1:
        qk(*tasks[1])
    softmax(*tasks[0])
    for k, (i, h) in enumerate(tasks):
        if k + 2 < n:
            qk(*tasks[k + 2])
        if k + 1 < n:
            softmax(*tasks[k + 1])
        outs[h] = pv(i, h)
        if h == 1:
            o_ref[0, i * qb:(i + 1) * qb, :] = jnp.where(lo, outs[0], outs[1]).astype(BF16)


def _mla_prompt(q, k, v):
    nb, seq, _ = q.shape
    qb = min(MLA_QBLOCK, seq)
    return pl.pallas_call(
        functools.partial(_mla_prompt_kernel, seq=seq),
        grid=(nb, D_HEADS // 2),
        in_specs=[pl.BlockSpec((1, seq, 2 * LANES), lambda b, p: (b, 0, p)),
                  pl.BlockSpec((1, seq, 2 * LANES), lambda b, p: (b, 0, p)),
                  pl.BlockSpec((1, seq, LANES), lambda b, p: (b, 0, p))],
        out_specs=pl.BlockSpec((1, seq, LANES), lambda b, p: (b, 0, p)),
        out_shape=jax.ShapeDtypeStruct((nb, seq, D_HEADS * D_V), BF16),
        scratch_shapes=[pltpu.VMEM((2, qb, seq), F32), pltpu.VMEM((2, qb, seq), BF16),
                        pltpu.VMEM((2, qb, LANES), F32)],
        compiler_params=_cparams("arbitrary", "arbitrary"),
        name="mla_prompt",
    )(q, k, v)


MLA_CACHE_ROWS = 512


def _mla_sample_kernel(q_ref, ckv_ref, kpe_ref, kn_ref, vn_ref, place_ref, wk_ref, kseg_ref, qabs_ref, wv_ref,
                       o_ref, ckvb_scr, kpeb_scr, sq_scr, s1_scr, s2_scr, e_scr):
    nq, past = q_ref.shape[1], ckv_ref.shape[1]
    lane_blocks = past // LANES
    lo = lax.broadcasted_iota(jnp.int32, (nq, LANES), 1) < D_V
    head = lambda h: slice(h * LANES, (h + 1) * LANES)

    for r0 in range(0, past, MLA_CACHE_ROWS):
        rows = slice(r0, r0 + MLA_CACHE_ROWS)
        c = ckv_ref[0, rows, :].astype(BF16)
        ckvb_scr[rows, :] = c
        kpeb_scr[rows, :] = _dot(kpe_ref[0, rows, :].astype(BF16), place_ref[...]).astype(BF16)
        kraw = _dot(c, wk_ref[...])
        sq_scr[rows, :] = (kraw * kraw).astype(BF16)
    r_t = lax.rsqrt(lax.dot_general(kseg_ref[...], sq_scr[...], _NT, preferred_element_type=F32) + EPS)

    q_abs = jnp.concatenate([_dot(q_ref[0, :, head(h)], qabs_ref[h]).astype(BF16) for h in range(D_HEADS)], axis=0)
    q_all = jnp.concatenate([q_ref[0, :, head(h)] for h in range(D_HEADS)], axis=0)
    s1_scr[...] = lax.dot_general(q_abs, ckvb_scr[...], _NT, preferred_element_type=F32)
    s2_scr[...] = lax.dot_general(q_all, kpeb_scr[...], _NT, preferred_element_type=F32)

    inv_den, e_new = [], []
    for h in range(D_HEADS):
        rows = slice(h * nq, (h + 1) * nq)
        s_new = lax.dot_general(q_ref[0, :, head(h)], kn_ref[0, :, head(h)], _NT, preferred_element_type=F32)
        mm = None
        for c in range(lane_blocks):
            cl = slice(c * LANES, (c + 1) * LANES)
            blk = s1_scr[rows, cl] * r_t[h:h + 1, cl] + s2_scr[rows, cl]
            s1_scr[rows, cl] = blk
            mm = blk if mm is None else jnp.maximum(mm, blk)
        m = jnp.maximum(jnp.max(mm, axis=-1, keepdims=True), jnp.max(s_new, axis=-1, keepdims=True))
        acc = None
        for c in range(lane_blocks):
            cl = slice(c * LANES, (c + 1) * LANES)
            e = jnp.exp2(s1_scr[rows, cl] - m)
            acc = e if acc is None else acc + e
            e_scr[rows, cl] = e.astype(BF16)
        en = jnp.exp2(s_new - m)
        inv_den.append(1.0 / (jnp.sum(acc, axis=-1, keepdims=True) + jnp.sum(en, axis=-1, keepdims=True)))
        e_new.append(en.astype(BF16))

    latent = _dot(e_scr[...], ckvb_scr[...])
    for p in range(D_HEADS // 2):
        halves = []
        for half in range(2):
            h = 2 * p + half
            rows = slice(h * nq, (h + 1) * nq)
            o = _dot(latent[rows].astype(BF16), wv_ref[h]) + _dot(e_new[h], vn_ref[0, :, head(p)])
            halves.append(o * inv_den[h])
        o_ref[0, :, head(p)] = jnp.where(lo, halves[0], halves[1]).astype(BF16)


def _mla_sample(q, ckv, kpe, kn, vn, ow):
    nb, nq, _ = q.shape
    past = ckv.shape[1]
    assert past % MLA_CACHE_ROWS == 0
    per_b = lambda r, w: pl.BlockSpec((1, r, w), lambda b: (b, 0, 0))
    consts = [ow["place"], ow["wk_dense"], ow["kseg_t"], ow["q_absorb"], ow["wv_lanes"]]
    rows = D_HEADS * nq
    return pl.pallas_call(
        _mla_sample_kernel,
        grid=(nb,),
        in_specs=[per_b(nq, D_HEADS * LANES), per_b(past, D_KV_LORA), per_b(past, D_ROPE),
                  per_b(nq, D_HEADS * LANES), per_b(nq, D_HEADS * D_V)] + [_resident(a.shape) for a in consts],
        out_specs=per_b(nq, D_HEADS * D_V),
        out_shape=jax.ShapeDtypeStruct((nb, nq, D_HEADS * D_V), BF16),
        scratch_shapes=[pltpu.VMEM((past, D_KV_LORA), BF16), pltpu.VMEM((past, LANES), BF16),
                        pltpu.VMEM((past, D_HEADS * D_NOPE), BF16), pltpu.VMEM((rows, past), F32),
                        pltpu.VMEM((rows, past), F32), pltpu.VMEM((rows, past), BF16)],
        compiler_params=_cparams("arbitrary"),
        name="mla_sample",
    )(q, ckv, kpe, kn, vn, *consts)


def _even_layer(xp, xs, nb, seq, ndb, dseq, norm_g, w_in, w_out, a_qn, a_kn, a_sinks, b_qn, b_kn, b_rel,
                t5_table, ck_a, cv_a, ck_b, cv_b, ffn2):
    w_in = _to_bf16(w_in)
    w_out = _to_bf16(w_out)
    w_in_p = jnp.concatenate([_pair_heads(w_in[:, :A_Q], 1), w_in[:, A_Q:]], axis=1)
    woa = _pair_heads(w_out[:A_Q], 0)
    wob = w_out[A_Q:]
    ones = lambda n: jnp.ones((n,), F32)
    scale = HEAD_DIM ** -0.5 * LOG2E
    gain_row = jnp.concatenate([jnp.tile(a_qn, A_HEADS) * scale, jnp.tile(a_kn, A_KV_HEADS), ones(A_KV),
                                jnp.tile(b_qn, B_HEADS) * scale, jnp.tile(b_kn, B_HEADS), ones(B_QKV)]
                               ).reshape(1, EVEN_IN).astype(F32)
    bias_a = _bias_a(t5_table) * LOG2E
    bias_b = _bias_b(b_rel) * LOG2E
    sink = jnp.broadcast_to(jnp.repeat(a_sinks.astype(F32) * LOG2E, PAIR_ROWS)[:, None],
                            (A_HEADS * PAIR_ROWS, LANES))
    la, lb = min(WINDOW, seq), min(B_REACH, seq)

    aq, ak, av, bq, bk, bv, cak, cav, cbk, cbv = _even_inproj(xp, norm_g, w_in_p, gain_row, nb, la, lb)
    r3 = lambda a: a.reshape(nb, seq, a.shape[-1])
    ya = _attn_a(r3(aq), r3(ak), r3(av), bias_a, sink, WINDOW, True)
    yb = _attn_b(r3(bq), r3(bk), r3(bv), bias_b, B_REACH, True)
    xp = _ffn(xp, *ffn2, out_proj=(ya.reshape(nb * seq, A_Q), yb.reshape(nb * seq, B_QKV), woa, wob))
    st_p = (cak.reshape(nb, la, A_KV_HEADS, HEAD_DIM), cav.reshape(nb, la, A_KV_HEADS, HEAD_DIM),
            cbk.reshape(nb, lb, B_HEADS, HEAD_DIM), cbv.reshape(nb, lb, B_HEADS, HEAD_DIM))

    ts = ndb * dseq
    aq, ak, av, bq, bk, bv, nak, nav, nbk, nbv = _even_inproj(xs, norm_g, w_in_p, gain_row, 1, ts, ts)
    pad_q = lambda a, rows: jnp.pad(a.reshape(ndb, dseq, a.shape[-1]), ((0, 0), (0, rows - dseq), (0, 0)))

    def window(cache, new, rows):
        w = cache.shape[-2] * cache.shape[-1]
        full = jnp.concatenate([cache.reshape(ndb, -1, w), new.reshape(ndb, dseq, w)], axis=1)
        buf = jnp.pad(full, ((0, 0), (0, rows - dseq), (0, 0))).astype(BF16)
        return full[:, dseq:].reshape(cache.shape), buf

    st_ak, kbuf_a = window(ck_a, nak, PAIR_ROWS)
    st_av, vbuf_a = window(cv_a, nav, PAIR_ROWS)
    st_bk, kbuf_b = window(ck_b, nbk, B_STEP_ROWS)
    st_bv, vbuf_b = window(cv_b, nbv, B_STEP_ROWS)
    ya = _attn_a(pad_q(aq, PAIR_ROWS), kbuf_a, vbuf_a, bias_a, sink, WINDOW, False)[:, :dseq]
    yb = _attn_b(pad_q(bq, B_STEP_ROWS), kbuf_b, vbuf_b, bias_b, B_REACH, False)[:, :dseq]
    xs = _ffn(xs, *ffn2, out_proj=(ya.reshape(ts, A_Q), yb.reshape(ts, B_QKV), woa, wob))
    return xp, xs, st_p, (st_ak, st_av, st_bk, st_bv)


def _odd_layer(xp, xs, nb, seq, ndb, dseq, past, norm_g, ow, w_out, conv_prev, c_ckv, c_kpe, ffn2):
    w_out = _to_bf16(w_out)
    woc = w_out[:C_WIDTH]
    wod = w_out[C_WIDTH:]

    zero_init = jnp.zeros((nb, 8, C_WIDTH), F32)
    yc, q, k, v, ckv, kpe, cs = _odd_inproj(xp, norm_g, ow, zero_init, nb, 0)
    r3 = lambda a: a.reshape(nb, seq, a.shape[-1])
    yd = _mla_prompt(r3(q), r3(k), r3(v))
    xp = _ffn(xp, *ffn2, out_proj=(yc, yd.reshape(nb * seq, D_HEADS * D_V), woc, wod))
    st_p = (cs, ckv.reshape(nb, seq, D_KV_LORA), kpe.reshape(nb, seq, D_ROPE))

    ts = ndb * dseq
    init = jnp.pad(conv_prev.astype(F32), ((0, 0), (8 - (CONV_W - 1), 0), (0, 0)))
    yc, q, kn, vn, ckv, kpe, cs = _odd_inproj(xs, norm_g, ow, init, ndb, past)
    s3 = lambda a: a.reshape(ndb, dseq, a.shape[-1])
    yd = _mla_sample(s3(q), c_ckv, c_kpe, s3(kn), s3(vn), ow)
    xs = _ffn(xs, *ffn2, out_proj=(yc, yd.reshape(ts, D_HEADS * D_V), woc, wod))
    st_s = (cs, ckv.reshape(ndb, dseq, D_KV_LORA), kpe.reshape(ndb, dseq, D_ROPE))
    return xp, xs, st_p, st_s


def kernel(x_prompt, x_sample, cache_a_k, cache_a_v, cache_b_k, cache_b_v, state_c_conv, cache_d_ckv, cache_d_kpe, ff1_norm, ff1_w_gu, ff1_w_down, mix_norm, ff2_norm, ff2_w_gu, ff2_w_down, t5_bias_table, ev_w_in, ev_w_out, a_q_norm, a_k_norm, a_sinks, b_q_norm, b_k_norm, b_rel_bias, od_w_in, od_w_out, c_conv_w, d_q_a_norm, d_w_q_b, d_kv_a_norm, d_w_kv_b, d_q_nope_norm, d_q_rope_norm, d_k_nope_norm, d_k_rope_norm):
    nb, seq, _ = x_prompt.shape
    ndb, dseq, _ = x_sample.shape
    past = cache_d_ckv.shape[2]
    depth = ff1_norm.shape[0]
    assert seq % TOKEN_TILE == 0 and dseq == CHUNK and past % CHUNK == 0
    xp = x_prompt.reshape(nb * seq, D_MODEL)
    xs = x_sample.reshape(ndb * dseq, D_MODEL)
    even_p, even_s, odd_p, odd_s = [], [], [], []
    for l in range(depth):
        i = l // 2
        ff1 = _ffn_weights(ff1_w_gu, ff1_w_down, l)
        ffn2 = (ff2_norm[l],) + _ffn_weights(ff2_w_gu, ff2_w_down, l)
        xp = _ffn(xp, ff1_norm[l], *ff1)
        xs = _ffn(xs, ff1_norm[l], *ff1)
        if l % 2 == 0:
            xp, xs, sp, ss = _even_layer(
                xp, xs, nb, seq, ndb, dseq, mix_norm[l], ev_w_in[i], ev_w_out[i], a_q_norm[i], a_k_norm[i],
                a_sinks[i], b_q_norm[i], b_k_norm[i], b_rel_bias[i], t5_bias_table,
                cache_a_k[i], cache_a_v[i], cache_b_k[i], cache_b_v[i], ffn2)
            even_p.append(sp)
            even_s.append(ss)
        else:
            ow = _odd_weights(od_w_in[i], c_conv_w[i], d_q_a_norm[i], d_w_q_b[i], d_kv_a_norm[i], d_w_kv_b[i],
                              d_q_nope_norm[i], d_q_rope_norm[i], d_k_nope_norm[i], d_k_rope_norm[i])
            xp, xs, sp, ss = _odd_layer(xp, xs, nb, seq, ndb, dseq, past, mix_norm[l], ow, od_w_out[i],
                                        state_c_conv[i], cache_d_ckv[i], cache_d_kpe[i], ffn2)
            odd_p.append(sp)
            odd_s.append(ss)
    stack = lambda group, j: group[0][j][None] if len(group) == 1 else jnp.stack([g[j] for g in group])
    return (xp.reshape(nb, seq, D_MODEL), xs.reshape(ndb, dseq, D_MODEL),
            stack(even_p, 0), stack(even_p, 1), stack(even_p, 2), stack(even_p, 3),
            stack(odd_p, 0), stack(odd_p, 1), stack(odd_p, 2),
            stack(even_s, 0), stack(even_s, 1), stack(even_s, 2), stack(even_s, 3),
            stack(odd_s, 0), stack(odd_s, 1), stack(odd_s, 2))
```

```python
import functools
import math

import jax
import jax.numpy as jnp
from jax import lax
from jax.experimental import pallas as pl
from jax.experimental.pallas import tpu as pltpu

F32 = jnp.float32
BF16 = jnp.bfloat16

D_MODEL = 1024
CHUNK = 64
HEAD_DIM = 64
EPS = 1e-6
A_HEADS = 8
A_KV_HEADS = 2
WINDOW = 128
T5_BUCKETS = 32
T5_MAX_DIST = 128
B_HEADS = 8
B_REACH = 512
B_MAX_REL = 128
C_WIDTH = 512
CONV_W = 3
D_HEADS = 8
D_Q_LORA = 256
D_KV_LORA = 128
D_NOPE = 64
D_ROPE = 32
D_V = 64
ROPE_THETA = 10000.0
FFN_DIM = 2816
A_Q = A_HEADS * HEAD_DIM
A_KV = A_KV_HEADS * HEAD_DIM
B_QKV = B_HEADS * HEAD_DIM
EVEN_IN = A_Q + 2 * A_KV + 3 * B_QKV

LANES = 128
TOKEN_TILE = 512
FFN_TILE = 1024
CAST_BLOCK_BYTES = 6 * 1024 * 1024
MXU_COLS = 256
FFN_CHUNK = MXU_COLS
PAIR_ROWS = 2 * CHUNK
MLA_QBLOCK = 512
VMEM_LIMIT_BYTES = 56 * 1024 * 1024
LOG2E = math.log2(math.e)
MLA_QSCALE = (D_NOPE + D_ROPE) ** -0.5 * LOG2E
NEG_INF = float("-inf")

_NT = (((1,), (1,)), ((), ()))


def _cparams(*sem):
    return pltpu.CompilerParams(dimension_semantics=sem, vmem_limit_bytes=VMEM_LIMIT_BYTES)


def _resident(shape):
    zeros = (0,) * len(shape)
    return pl.BlockSpec(shape, lambda *_: zeros, pipeline_mode=pl.Buffered(1))


def _rms(x, g):
    ms = jnp.mean(x * x, axis=-1, keepdims=True)
    return (x * lax.rsqrt(ms + EPS)) * g


def _dot(a, b):
    return jnp.dot(a, b, preferred_element_type=F32)


GU_STAGE_ROWS = 64
DOWN_STAGE_ROWS = 352


def _stream_cast(src_hbm, dst_scr, stage, sem, rows):
    n = src_hbm.shape[0] // rows
    copy = lambda c: pltpu.make_async_copy(src_hbm.at[pl.ds(c * rows, rows), :], stage.at[c % 2], sem.at[c % 2])
    copy(0).start()
    for c in range(n):
        if c + 1 < n:
            copy(c + 1).start()
        copy(c).wait()
        dst_scr[c * rows:(c + 1) * rows, :] = stage[c % 2].astype(BF16)


def _ffn_kernel(*refs, n_chunks, fused_out, layer):
    if fused_out:
        (x_ref, ya_ref, yb_ref, woa_ref, wob_ref, g_ref, wgu_hbm, wd_hbm,
         o_ref, h_scr, a_scr, wgu_ref, wd_ref, gu_stage, d_stage, sem) = refs
    else:
        (x_ref, g_ref, wgu_hbm, wd_hbm, o_ref, h_scr, a_scr, wgu_ref, wd_ref, gu_stage, d_stage, sem) = refs

    @pl.when(pl.program_id(0) == 0)
    def _():
        _stream_cast(wgu_hbm.at[layer], wgu_ref, gu_stage, sem, GU_STAGE_ROWS)
        _stream_cast(wd_hbm.at[layer], wd_ref, d_stage, sem, DOWN_STAGE_ROWS)

    if fused_out:
        y = _dot(ya_ref[...], woa_ref[...]) + _dot(yb_ref[...], wob_ref[...])
        o_ref[...] = x_ref[...] + y
        res_ref = o_ref
    else:
        res_ref = x_ref
    h_scr[...] = _rms(res_ref[...], g_ref[...]).astype(BF16)
    for j in range(n_chunks):
        h = h_scr[...]
        g = _dot(h, wgu_ref[:, j * FFN_CHUNK:(j + 1) * FFN_CHUNK])
        u = _dot(h, wgu_ref[:, FFN_DIM + j * FFN_CHUNK:FFN_DIM + (j + 1) * FFN_CHUNK])
        a_scr[:, j * FFN_CHUNK:(j + 1) * FFN_CHUNK] = ((g * jax.nn.sigmoid(g)) * u).astype(BF16)
    o_ref[...] = res_ref[...] + 0.5 * _dot(a_scr[...], wd_ref[...])


def _ffn(x, norm_g, w_gu, w_down, layer, out_proj=None):
    t = x.shape[0]
    tm = FFN_TILE if t % FFN_TILE == 0 else min(TOKEN_TILE, t)
    n_chunks = FFN_DIM // FFN_CHUNK
    assert D_MODEL % GU_STAGE_ROWS == 0 and FFN_DIM % DOWN_STAGE_ROWS == 0
    row = lambda w: pl.BlockSpec((tm, w), lambda i: (i, 0))
    in_hbm = pl.BlockSpec(memory_space=pl.ANY)
    in_specs = [row(D_MODEL)]
    args = [x]
    if out_proj is not None:
        ya, yb, woa, wob = out_proj
        in_specs += [row(ya.shape[1]), row(yb.shape[1]), _resident(woa.shape), _resident(wob.shape)]
        args += [ya, yb, woa, wob]
    in_specs += [_resident((1, D_MODEL)), in_hbm, in_hbm]
    args += [norm_g.reshape(1, D_MODEL), w_gu, w_down]
    return pl.pallas_call(
        functools.partial(_ffn_kernel, n_chunks=n_chunks, fused_out=out_proj is not None, layer=layer),
        grid=(t // tm,),
        in_specs=in_specs,
        out_specs=row(D_MODEL),
        out_shape=jax.ShapeDtypeStruct((t, D_MODEL), F32),
        scratch_shapes=[pltpu.VMEM((tm, D_MODEL), BF16), pltpu.VMEM((tm, FFN_DIM), BF16),
                        pltpu.VMEM((D_MODEL, 2 * FFN_DIM), BF16), pltpu.VMEM((FFN_DIM, D_MODEL), BF16),
                        pltpu.VMEM((2, GU_STAGE_ROWS, 2 * FFN_DIM), F32),
                        pltpu.VMEM((2, DOWN_STAGE_ROWS, D_MODEL), F32),
                        pltpu.SemaphoreType.DMA((2,))],
        compiler_params=_cparams("arbitrary"),
        name="ffn_out" if out_proj is not None else "ffn",
    )(*args)


def _cast_kernel(x_ref, o_ref):
    o_ref[...] = x_ref[...].astype(BF16)


def _to_bf16(w, layer=None):
    r, c = w.shape[-2:]
    fits = [d for d in range(16, r + 1, 16) if r % d == 0 and d * c * 4 <= CAST_BLOCK_BYTES]
    tr = max(fits) if fits else r
    out_spec = pl.BlockSpec((tr, c), lambda i: (i, 0))
    in_spec = out_spec if layer is None else pl.BlockSpec((None, tr, c), lambda i: (layer, i, 0))
    return pl.pallas_call(
        _cast_kernel, grid=(r // tr,), in_specs=[in_spec], out_specs=out_spec,
        out_shape=jax.ShapeDtypeStruct((r, c), BF16),
        compiler_params=_cparams("arbitrary"), name="cast_bf16",
    )(w)


def _norm_halves(blk, gain):
    sq = blk * blk
    lo = lax.broadcasted_iota(jnp.int32, blk.shape, 1) < HEAD_DIM
    s_lo = jnp.sum(jnp.where(lo, sq, 0.0), axis=-1, keepdims=True)
    s_hi = jnp.sum(jnp.where(lo, 0.0, sq), axis=-1, keepdims=True)
    inv = jnp.where(lo, lax.rsqrt(s_lo * (1.0 / HEAD_DIM) + EPS), lax.rsqrt(s_hi * (1.0 / HEAD_DIM) + EPS))
    return (blk * inv) * gain


def _even_inproj_kernel(x_ref, g_ref, w_ref, gain_ref,
                        aq_ref, ak_ref, av_ref, bq_ref, bk_ref, bv_ref,
                        cak_ref, cav_ref, cbk_ref, cbv_ref, h_scr, z_scr, *, tm, tpb, rows_a, rows_b):
    i = pl.program_id(0)
    last = (i % tpb) == (tpb - 1)
    h_scr[...] = _rms(x_ref[...], g_ref[...]).astype(BF16)

    def lanes(c):
        return slice(c * LANES, (c + 1) * LANES)

    def emit(c):
        blk = z_scr[:, lanes(c)]
        gain = gain_ref[:, lanes(c)]
        if c < 4:
            aq_ref[:, lanes(c)] = _norm_halves(blk, gain).astype(BF16)
        elif c == 4:
            kn = _norm_halves(blk, gain)
            ak_ref[...] = kn.astype(BF16)
            z_scr[:, lanes(c)] = kn
        elif c == 5:
            av_ref[...] = blk.astype(BF16)
        elif c < 10:
            bq_ref[:, lanes(c - 6)] = _norm_halves(blk, gain).astype(BF16)
        elif c < 14:
            kn = _norm_halves(blk, gain)
            bk_ref[:, lanes(c - 10)] = kn.astype(BF16)
            z_scr[:, lanes(c)] = kn
        else:
            bv_ref[:, lanes(c - 14)] = blk.astype(BF16)

    for grp in range(EVEN_IN // MXU_COLS):
        cols = slice(grp * MXU_COLS, (grp + 1) * MXU_COLS)
        z_scr[:, cols] = _dot(h_scr[...], w_ref[:, cols])
        emit(2 * grp)
        emit(2 * grp + 1)

    @pl.when(last)
    def _():
        cak_ref[0] = z_scr[tm - rows_a:, lanes(4)]
        cav_ref[0] = z_scr[tm - rows_a:, lanes(5)]
        cbk_ref[0] = z_scr[tm - rows_b:, 10 * LANES:14 * LANES]
        cbv_ref[0] = z_scr[tm - rows_b:, 14 * LANES:18 * LANES]


def _even_inproj(x, norm_g, w_in, gain_row, n_batch, rows_a, rows_b):
    t = x.shape[0]
    tm = min(TOKEN_TILE, t)
    tpb = t // n_batch // tm
    assert tpb * tm * n_batch == t and rows_a <= tm and rows_b <= tm
    row = lambda w: pl.BlockSpec((tm, w), lambda i: (i, 0))
    cache = lambda r, w: pl.BlockSpec((1, r, w), lambda i: (i // tpb, 0, 0))
    bshape = lambda w: jax.ShapeDtypeStruct((t, w), BF16)
    cshape = lambda r, w: jax.ShapeDtypeStruct((n_batch, r, w), F32)
    return pl.pallas_call(
        functools.partial(_even_inproj_kernel, tm=tm, tpb=tpb, rows_a=rows_a, rows_b=rows_b),
        grid=(t // tm,),
        in_specs=[row(D_MODEL), _resident((1, D_MODEL)), _resident(w_in.shape), _resident(gain_row.shape)],
        out_specs=[row(A_Q), row(A_KV), row(A_KV), row(B_QKV), row(B_QKV), row(B_QKV),
                   cache(rows_a, A_KV), cache(rows_a, A_KV), cache(rows_b, B_QKV), cache(rows_b, B_QKV)],
        out_shape=[bshape(A_Q), bshape(A_KV), bshape(A_KV), bshape(B_QKV), bshape(B_QKV), bshape(B_QKV),
                   cshape(rows_a, A_KV), cshape(rows_a, A_KV), cshape(rows_b, B_QKV), cshape(rows_b, B_QKV)],
        scratch_shapes=[pltpu.VMEM((tm, D_MODEL), BF16), pltpu.VMEM((tm, EVEN_IN), F32)],
        compiler_params=_cparams("arbitrary"),
        name="even_inproj",
    )(x, norm_g.reshape(1, D_MODEL), w_in, gain_row)


def _pair_heads(a, axis):
    shape = a.shape
    split = shape[:axis] + (2, A_HEADS // 2, HEAD_DIM) + shape[axis + 1:]
    return a.reshape(split).swapaxes(axis, axis + 1).reshape(shape)


def _fill_padded(buf, src_ref, ctx):
    buf[0:ctx, :] = jnp.zeros((ctx, buf.shape[1]), buf.dtype)
    buf[ctx:, :] = src_ref[0]


def _windows(k_ref, v_ref, pad_scr, ctx, pad_front, step_rows, stream):
    win = ctx + step_rows
    if pad_front:
        kbuf, vbuf = pad_scr
        _fill_padded(kbuf, k_ref, ctx)
        _fill_padded(vbuf, v_ref, ctx)
        return (lambda r0: kbuf[pl.ds(r0, win), :]), (lambda r0: vbuf[pl.ds(r0, win), :])
    return (lambda r0: k_ref[stream, pl.ds(r0, win), :]), (lambda r0: v_ref[stream, pl.ds(r0, win), :])


def _fold_lanes(x, op):
    acc = x[:, :LANES]
    for c in range(1, x.shape[1] // LANES):
        acc = op(acc, x[:, c * LANES:(c + 1) * LANES])
    return acc


def _row_stat(reduce_fn, x):
    return jnp.broadcast_to(reduce_fn(x, axis=-1, keepdims=True), x.shape)


PAIR_SLOTS = 4


def _run_steps(make_stages, n_streams, n_steps, n_front, step_rows, lookahead):
    stages = [make_stages(stream) for stream in range(n_streams)]
    tasks = [(stream, t) for stream in range(n_streams) for t in range(n_steps)]

    def run(stage, k):
        stream, t = tasks[k]
        stages[stream][stage](t * step_rows, k % PAIR_SLOTS, t < n_front)

    n = len(tasks)
    if not lookahead:
        for k in range(n):
            for stage in range(3):
                run(stage, k)
        return
    run(0, 0)
    if n > 1:
        run(0, 1)
    run(1, 0)
    for k in range(n):
        if k + 2 < n:
            run(0, k + 2)
        if k + 1 < n:
            run(1, k + 1)
        run(2, k)


def _band_bias(ext_row, n_rows, ctx):
    win = ctx + n_rows
    period = ext_row.shape[1]
    t = pltpu.roll(jnp.broadcast_to(ext_row, (n_rows, period)), 0, 1, stride=1, stride_axis=0)[:, :win]
    chunk = lax.broadcasted_iota(jnp.int32, (n_rows, win), 0) // CHUNK
    col = lax.broadcasted_iota(jnp.int32, (n_rows, win), 1)
    visible = (col >= CHUNK * chunk) & (col < ctx + CHUNK * (chunk + 1))
    return jnp.where(visible, t, NEG_INF)


def _attn_a_kernel(q_ref, k_ref, v_ref, ext_ref, sink_ref, o_ref, bias_ref, s_scr, e_scr, *pad_scr,
                   nq, ctx, pad_front):
    win = ctx + PAIR_ROWS
    lo = lax.broadcasted_iota(jnp.int32, (PAIR_ROWS, LANES), 1) < HEAD_DIM

    @pl.when(pl.program_id(0) == 0)
    def _():
        for head in range(A_HEADS):
            bias_ref[head * PAIR_ROWS:(head + 1) * PAIR_ROWS, :] = _band_bias(
                ext_ref[head:head + 1, :], PAIR_ROWS, ctx)

    def make_stages(stream):
        kwin, vwin = _windows(k_ref, v_ref, pad_scr, ctx, pad_front, PAIR_ROWS, stream)
        return (functools.partial(scores, stream, kwin), functools.partial(softmax, stream),
                functools.partial(values, stream, vwin))

    def scores(stream, kwin, r0, slot, masked):
        q_lo, q_hi = [], []
        for p in range(4):
            qp = q_ref[stream, pl.ds(r0, PAIR_ROWS), p * LANES:(p + 1) * LANES]
            zero = jnp.zeros_like(qp)
            q_lo.append(jnp.where(lo, qp, zero))
            q_hi.append(jnp.where(lo, zero, qp))
        q2 = jnp.concatenate(q_lo + q_hi, axis=0)
        s_scr[slot] = lax.dot_general(q2, kwin(r0), _NT, preferred_element_type=F32)

    def softmax(stream, r0, slot, masked):
        for head in range(A_HEADS):
            rows = slice(head * PAIR_ROWS, (head + 1) * PAIR_ROWS)
            s = s_scr[slot, rows, :] + bias_ref[rows, :]
            if masked:
                col = lax.broadcasted_iota(jnp.int32, (PAIR_ROWS, win), 1)
                s = jnp.where(col + r0 >= ctx, s, NEG_INF)
            sk = sink_ref[rows, :]
            m = jnp.maximum(_row_stat(jnp.max, _fold_lanes(s, jnp.maximum)), sk)
            e = [jnp.exp2(s[:, c * LANES:(c + 1) * LANES] - m) for c in range(win // LANES)]
            den = _row_stat(jnp.sum, functools.reduce(jnp.add, e)) + jnp.exp2(sk - m)
            inv = 1.0 / den
            for c in range(win // LANES):
                e_scr[slot, rows, c * LANES:(c + 1) * LANES] = (e[c] * inv).astype(BF16)

    def values(stream, vwin, r0, slot, masked):
        o2 = _dot(e_scr[slot], vwin(r0))
        half = 4 * PAIR_ROWS
        for p in range(4):
            o = jnp.where(lo, o2[p * PAIR_ROWS:(p + 1) * PAIR_ROWS],
                          o2[half + p * PAIR_ROWS:half + (p + 1) * PAIR_ROWS])
            o_ref[stream, pl.ds(r0, PAIR_ROWS), p * LANES:(p + 1) * LANES] = o.astype(BF16)

    _run_steps(make_stages, q_ref.shape[0], nq // PAIR_ROWS, ctx // PAIR_ROWS if pad_front else 0, PAIR_ROWS,
               lookahead=False)


def _streams_per_block(nb, pad_front):
    return 1 if pad_front else nb


def _attn_a(q, k, v, ext, sink, ctx, pad_front):
    nb, nq, _ = q.shape
    nk = k.shape[1]
    assert nq % PAIR_ROWS == 0 and nk == (nq if pad_front else ctx + nq)
    bb = _streams_per_block(nb, pad_front)
    per_b = lambda r, w: pl.BlockSpec((bb, r, w), lambda b: (b, 0, 0))
    rows, win = A_HEADS * PAIR_ROWS, ctx + PAIR_ROWS
    scratch = [pltpu.VMEM((rows, win), F32),
               pltpu.VMEM((PAIR_SLOTS, rows, win), F32), pltpu.VMEM((PAIR_SLOTS, rows, win), BF16)]
    if pad_front:
        scratch += [pltpu.VMEM((ctx + nq, A_KV), BF16)] * 2
    return pl.pallas_call(
        functools.partial(_attn_a_kernel, nq=nq, ctx=ctx, pad_front=pad_front),
        grid=(nb // bb,),
        in_specs=[per_b(nq, A_Q), per_b(nk, A_KV), per_b(nk, A_KV), _resident(ext.shape), _resident(sink.shape)],
        out_specs=per_b(nq, A_Q),
        out_shape=jax.ShapeDtypeStruct((nb, nq, A_Q), BF16),
        scratch_shapes=scratch,
        compiler_params=_cparams("arbitrary"),
        name="attn_a",
    )(q, k, v, ext, sink)


B_STEP_ROWS = 4 * CHUNK


def _attn_b_kernel(q_ref, k_ref, v_ref, ext_ref, o_ref, bias_scr, s_scr, e_scr, r_scr, *pad_scr,
                   nq, ctx, pad_front):
    win = ctx + B_STEP_ROWS
    lo = lax.broadcasted_iota(jnp.int32, (B_STEP_ROWS, LANES), 1) < HEAD_DIM
    chunks = B_STEP_ROWS // CHUNK
    pair_idx = pl.program_id(1)

    @pl.when(pl.program_id(0) == 0)
    def _():
        for half in range(2):
            bias_scr[pair_idx, half * B_STEP_ROWS:(half + 1) * B_STEP_ROWS, :] = _band_bias(
                ext_ref[0, half:half + 1, :], B_STEP_ROWS, ctx)

    bias_ref = bias_scr.at[pair_idx]

    def make_stages(stream):
        kwin, vwin = _windows(k_ref, v_ref, pad_scr, ctx, pad_front, B_STEP_ROWS, stream)
        return (functools.partial(scores, stream, kwin), functools.partial(softmax, stream),
                functools.partial(values, stream, vwin))

    def scores(stream, kwin, r0, slot, masked):
        qp = q_ref[stream, pl.ds(r0, B_STEP_ROWS), :]
        zero = jnp.zeros_like(qp)
        q2 = jnp.concatenate([jnp.where(lo, qp, zero), jnp.where(lo, zero, qp)], axis=0)
        s_scr[slot] = lax.dot_general(q2, kwin(r0), _NT, preferred_element_type=F32)

    def softmax(stream, r0, slot, masked):
        for rb in range(2 * chunks):
            rows = slice(rb * CHUNK, (rb + 1) * CHUNK)
            ci = rb % chunks
            c_lo, c_hi = CHUNK * ci // LANES, -(-(ctx + CHUNK * (ci + 1)) // LANES)
            band = slice(c_lo * LANES, c_hi * LANES)
            s = s_scr[slot, rows, band] + bias_ref[rows, band]
            if masked:
                col = lax.broadcasted_iota(jnp.int32, s.shape, 1) + c_lo * LANES
                s = jnp.where(col + r0 >= ctx, s, NEG_INF)
            m = _row_stat(jnp.max, _fold_lanes(s, jnp.maximum))
            e = [jnp.exp2(s[:, c * LANES:(c + 1) * LANES] - m) for c in range(c_hi - c_lo)]
            for c in range(win // LANES):
                blk = e[c - c_lo].astype(BF16) if c_lo <= c < c_hi else jnp.zeros((CHUNK, LANES), BF16)
                e_scr[slot, rows, c * LANES:(c + 1) * LANES] = blk
            r_scr[slot, rows, :] = 1.0 / _row_stat(jnp.sum, functools.reduce(jnp.add, e))

    def values(stream, vwin, r0, slot, masked):
        o2 = _dot(e_scr[slot], vwin(r0)) * r_scr[slot]
        o = jnp.where(lo, o2[:B_STEP_ROWS], o2[B_STEP_ROWS:])
        o_ref[stream, pl.ds(r0, B_STEP_ROWS), :] = o.astype(BF16)

    _run_steps(make_stages, q_ref.shape[0], nq // B_STEP_ROWS,
               -(-ctx // B_STEP_ROWS) if pad_front else 0, B_STEP_ROWS, lookahead=True)


def _attn_b(q, k, v, ext, ctx, pad_front):
    nb, nq, _ = q.shape
    nk = k.shape[1]
    assert nq % B_STEP_ROWS == 0 and nk == (nq if pad_front else ctx + nq)
    bb = _streams_per_block(nb, pad_front)
    blk = lambda r: pl.BlockSpec((bb, r, LANES), lambda b, p: (b, 0, p))
    rows, win = 2 * B_STEP_ROWS, ctx + B_STEP_ROWS
    scratch = [pltpu.VMEM((B_HEADS // 2, rows, win), F32),
               pltpu.VMEM((PAIR_SLOTS, rows, win), F32), pltpu.VMEM((PAIR_SLOTS, rows, win), BF16),
               pltpu.VMEM((PAIR_SLOTS, rows, LANES), F32)]
    if pad_front:
        scratch += [pltpu.VMEM((ctx + nq, LANES), BF16)] * 2
    return pl.pallas_call(
        functools.partial(_attn_b_kernel, nq=nq, ctx=ctx, pad_front=pad_front),
        grid=(nb // bb, B_HEADS // 2),
        in_specs=[blk(nq), blk(nk), blk(nk),
                  pl.BlockSpec((1,) + ext.shape[1:], lambda b, p: (p, 0, 0))],
        out_specs=blk(nq),
        out_shape=jax.ShapeDtypeStruct((nb, nq, B_QKV), BF16),
        scratch_shapes=scratch,
        compiler_params=_cparams("arbitrary", "arbitrary"),
        name="attn_b",
    )(q, k, v, ext)


def _t5_bucket(rel):
    nb = T5_BUCKETS // 2
    max_exact = nb // 2
    n = -rel
    ret = jnp.where(n < 0, nb, 0)
    n = jnp.abs(n)
    nf = jnp.maximum(n, 1).astype(F32)
    large = max_exact + (jnp.log(nf / max_exact) / math.log(T5_MAX_DIST / max_exact)
                         * (nb - max_exact)).astype(jnp.int32)
    large = jnp.minimum(large, nb - 1)
    return ret + jnp.where(n < max_exact, n, large)


def _band_values(ctx, n_rows, value_of_rel):
    n_cols = ctx + n_rows
    period = -(-(n_cols + n_rows - 1) // LANES) * LANES
    d = jnp.concatenate([jnp.arange(0, period - (n_rows - 1)), jnp.arange(-(n_rows - 1), 0)])
    return value_of_rel(d - ctx)


def _bias_a(t5_table):
    return _band_values(WINDOW, PAIR_ROWS, lambda rel: t5_table.astype(F32)[_t5_bucket(rel)].T)


def _bias_b(b_rel):
    ext = _band_values(B_REACH, B_STEP_ROWS,
                       lambda rel: b_rel.astype(F32)[:, jnp.clip(rel, -B_MAX_REL, B_MAX_REL) + B_MAX_REL])
    return ext.reshape(B_HEADS // 2, 2, ext.shape[-1])


ODD_IN_PAD = 2048
ROPE_HALF = D_ROPE // 2
X1_LO = LANES // 2
NOPE_SPLIT = X1_LO - ROPE_HALF


def _head_lane_source():
    zero = D_NOPE + D_ROPE
    src = []
    for lane in range(LANES):
        if lane < ROPE_HALF:
            src.append(D_NOPE + ROPE_HALF + lane)
        elif lane < X1_LO:
            src.append(lane - ROPE_HALF)
        elif lane < X1_LO + ROPE_HALF:
            src.append(D_NOPE + lane - X1_LO)
        elif lane < X1_LO + ROPE_HALF + D_NOPE - NOPE_SPLIT:
            src.append(NOPE_SPLIT + lane - X1_LO - ROPE_HALF)
        else:
            src.append(zero)
    return src


def _to_head_lanes(a):
    padded = jnp.concatenate([a, jnp.zeros(a.shape[:-1] + (1,), a.dtype)], axis=-1)
    return padded[..., jnp.asarray(_head_lane_source(), dtype=jnp.int32)]


def _lane_masks(shape):
    lane = lax.broadcasted_iota(jnp.int32, shape, 1)
    rope = (lane < ROPE_HALF) | ((lane >= X1_LO) & (lane < X1_LO + ROPE_HALF))
    used = lane < X1_LO + ROPE_HALF + D_NOPE - NOPE_SPLIT
    return used & ~rope, rope


def _rope(y, cosf, sinf):
    return y * cosf + pltpu.roll(y, LANES // 2, 1) * sinf


ROW_BLOCK = 128


def _row_blocks(tm):
    rb = min(ROW_BLOCK, tm)
    return [slice(r, r + rb) for r in range(0, tm, rb)]


def _segment_mean_squares(src_scr, cols, seg_ref, sq_scr, ms_scr, slot):
    for rows in _row_blocks(src_scr.shape[0]):
        v = src_scr[rows, cols]
        sq_scr[slot, rows, :] = (v * v).astype(BF16)
    ms_scr[slot] = _dot(sq_scr[slot], seg_ref[...])


def _expand_kv(ckvb_ref, kpe_blk_ref, wkv_ref, kgain_ref, kseg_ref, kv_scr, sq_scr, ms_scr, k_ref, v_ref,
               between=None):
    k_cols = D_HEADS * LANES
    tm = kv_scr.shape[0]
    for grp in range(kv_scr.shape[1] // MXU_COLS):
        if between and grp in between:
            between[grp]()
        cols = slice(grp * MXU_COLS, (grp + 1) * MXU_COLS)
        kv_scr[:, cols] = _dot(ckvb_ref[...], wkv_ref[:, cols])
        if cols.start < k_cols:
            slot = grp % 2
            _segment_mean_squares(kv_scr, cols, kseg_ref, sq_scr, ms_scr, slot)
        for rows in _row_blocks(tm):
            if cols.start < k_cols:
                for half in range(2):
                    h = 2 * grp + half
                    kb = kv_scr[rows, h * LANES:(h + 1) * LANES]
                    ms = ms_scr[slot, rows, half * LANES:(half + 1) * LANES]
                    kn = (kb * lax.rsqrt(ms + EPS)) * kgain_ref[...]
                    k_ref[rows, h * LANES:(h + 1) * LANES] = (kn + kpe_blk_ref[rows, :]).astype(BF16)
            else:
                v_ref[rows, cols.start - k_cols:cols.stop - k_cols] = kv_scr[rows, cols].astype(BF16)


def _odd_inproj_kernel(x_ref, g_ref, w_ref, cinit_ref, convw_ref, qan_ref, wqb_ref, qgain_ref,
                       kvan_ref, krgain_ref, invf_ref, sgn_ref, wkv_ref, kgain_ref, qseg_ref, kseg_ref,
                       yc_ref, q_ref, k_ref, v_ref, ckv_ref, kpe_ref, cs_ref,
                       h_scr, z_scr, uext_scr, qn_scr, q_scr, kv_scr, rot_scr, ckvb_scr, kpe_scr, sq_scr, ms_scr,
                       *, tm, tpb, pos0):
    i = pl.program_id(0)
    tile = i % tpb
    off = pl.multiple_of(tile * tm, tm)
    blocks = _row_blocks(tm)
    rb = blocks[0].stop

    @pl.when(tile == 0)
    def _():
        uext_scr[0:8, :] = cinit_ref[0]

    @pl.when(i < tpb)
    def _():
        row = lax.broadcasted_iota(jnp.int32, (tm, LANES), 0)
        ang = (row + (tile * tm + pos0)).astype(F32) * invf_ref[...]
        rot_scr[0, pl.ds(off, tm), :] = jnp.cos(ang)
        rot_scr[1, pl.ds(off, tm), :] = jnp.sin(ang) * sgn_ref[...]

    h_scr[...] = _rms(x_ref[...], g_ref[...]).astype(BF16)
    low_lanes = lax.broadcasted_iota(jnp.int32, (rb, LANES), 1) < ROPE_HALF

    def proj(c0, c1):
        z_scr[:, c0:c1] = _dot(h_scr[...], w_ref[:, c0:c1])

    def rotary(y, rows):
        cosf = rot_scr[0, pl.ds(off + rows.start, rb), :]
        sinf = rot_scr[1, pl.ds(off + rows.start, rb), :]
        return _rope(y, cosf, sinf)

    def conv_group(grp):
        c0 = grp * MXU_COLS
        cols = slice(c0, c0 + MXU_COLS)
        for base in (0, C_WIDTH, 2 * C_WIDTH):
            proj(base + c0, base + c0 + MXU_COLS)
        for rows in blocks:
            up = slice(rows.start + 8, rows.stop + 8)
            uext_scr[up, cols] = (z_scr[rows, C_WIDTH + c0:C_WIDTH + c0 + MXU_COLS]
                                  * z_scr[rows, 2 * C_WIDTH + c0:2 * C_WIDTH + c0 + MXU_COLS])
        for rows in blocks:
            yconv = convw_ref[0:1, cols] * uext_scr[rows.start + 6:rows.stop + 6, cols]
            yconv = yconv + convw_ref[1:2, cols] * uext_scr[rows.start + 7:rows.stop + 7, cols]
            yconv = yconv + convw_ref[2:3, cols] * uext_scr[rows.start + 8:rows.stop + 8, cols]
            yc_ref[rows, cols] = (z_scr[rows, cols] * yconv).astype(BF16)

    proj(1536, 1792)
    proj(1792, 2048)

    for rows in blocks:
        qn_scr[rows, :] = _rms(z_scr[rows, 1536:1792], qan_ref[...]).astype(BF16)
        ckv = _rms(z_scr[rows, 1792:1920], kvan_ref[...])
        ckv_ref[rows, :] = ckv
        ckvb_scr[rows, :] = ckv.astype(BF16)

    for grp in range(D_HEADS * LANES // MXU_COLS):
        cols = slice(grp * MXU_COLS, (grp + 1) * MXU_COLS)
        q_scr[:, cols] = _dot(qn_scr[...], wqb_ref[:, cols])
        slot = grp % 2
        _segment_mean_squares(q_scr, cols, qseg_ref, sq_scr, ms_scr, slot)
        for half in range(2):
            h = 2 * grp + half
            for rows in blocks:
                blk = q_scr[rows, h * LANES:(h + 1) * LANES]
                inv = lax.rsqrt(ms_scr[slot, rows, half * LANES:(half + 1) * LANES] + EPS)
                y = rotary((blk * inv) * qgain_ref[...], rows)
                q_ref[rows, h * LANES:(h + 1) * LANES] = y.astype(BF16)

    for rows in blocks:
        kb = z_scr[rows, 1920:2048]
        ms = jnp.sum(kb * kb, axis=-1, keepdims=True) * (1.0 / D_ROPE)
        kpe_blk = rotary((kb * lax.rsqrt(ms + EPS)) * krgain_ref[...], rows)
        kpe_scr[rows, :] = kpe_blk
        x1_then_x2 = jnp.where(low_lanes, pltpu.roll(kpe_blk, LANES - X1_LO, 1), pltpu.roll(kpe_blk, ROPE_HALF, 1))
        kpe_ref[rows, :] = x1_then_x2[:, 0:D_ROPE]
    _expand_kv(ckvb_scr, kpe_scr, wkv_ref, kgain_ref, kseg_ref, kv_scr, sq_scr, ms_scr, k_ref, v_ref,
               between={0: lambda: conv_group(0), 4: lambda: conv_group(1)})

    cs_ref[0] = uext_scr[tm + 6:tm + 8, :]
    uext_scr[0:8, :] = uext_scr[tm:tm + 8, :]


def _odd_inproj(x, norm_g, ow, conv_init, n_batch, pos0):
    t = x.shape[0]
    tm = min(TOKEN_TILE, t // n_batch)
    tpb = t // n_batch // tm
    assert tpb * tm * n_batch == t
    row = lambda w: pl.BlockSpec((tm, w), lambda i: (i, 0))
    per_b = lambda r, w: pl.BlockSpec((1, r, w), lambda i: (i // tpb, 0, 0))
    small = [ow["conv_w"], ow["qan"], ow["wqb"], ow["qgain"], ow["kvan"], ow["krgain"],
             ow["invf"], ow["sgn"], ow["wkv"], ow["kgain"], ow["qseg"], ow["kseg"]]
    kv_w = D_HEADS * LANES
    v_w = D_HEADS * D_V
    return pl.pallas_call(
        functools.partial(_odd_inproj_kernel, tm=tm, tpb=tpb, pos0=pos0),
        grid=(t // tm,),
        in_specs=[row(D_MODEL), _resident((1, D_MODEL)), _resident(ow["w_in"].shape), per_b(8, C_WIDTH)]
                 + [_resident(a.shape) for a in small],
        out_specs=[row(C_WIDTH), row(kv_w), row(kv_w), row(v_w), row(D_KV_LORA), row(D_ROPE),
                   per_b(CONV_W - 1, C_WIDTH)],
        out_shape=[jax.ShapeDtypeStruct((t, C_WIDTH), BF16), jax.ShapeDtypeStruct((t, kv_w), BF16),
                   jax.ShapeDtypeStruct((t, kv_w), BF16), jax.ShapeDtypeStruct((t, v_w), BF16),
                   jax.ShapeDtypeStruct((t, D_KV_LORA), F32), jax.ShapeDtypeStruct((t, D_ROPE), F32),
                   jax.ShapeDtypeStruct((n_batch, CONV_W - 1, C_WIDTH), F32)],
        scratch_shapes=[pltpu.VMEM((tm, D_MODEL), BF16), pltpu.VMEM((tm, ODD_IN_PAD), F32),
                        pltpu.VMEM((tm + 8, C_WIDTH), F32), pltpu.VMEM((tm, D_Q_LORA), BF16),
                        pltpu.VMEM((tm, kv_w), F32), pltpu.VMEM((tm, kv_w + v_w), F32),
                        pltpu.VMEM((2, tpb * tm, LANES), F32), pltpu.VMEM((tm, D_KV_LORA), BF16),
                        pltpu.VMEM((tm, LANES), F32), pltpu.VMEM((2, tm, MXU_COLS), BF16),
                        pltpu.VMEM((2, tm, MXU_COLS), F32)],
        compiler_params=_cparams("arbitrary"),
        name="odd_inproj",
    )(x, norm_g.reshape(1, D_MODEL), ow["w_in"], conv_init, *small)


def _odd_weights(w_in, conv_w, q_a_norm, w_q_b, kv_a_norm, w_kv_b, qn_nope, qn_rope, kn_nope, kn_rope):
    z = lambda n: jnp.zeros((n,), F32)
    qk = D_NOPE + D_ROPE
    w_in = _to_bf16(w_in)
    kr_blk = _to_head_lanes(jnp.concatenate([jnp.zeros((D_MODEL, D_NOPE), BF16), w_in[:, 1920:]], axis=1))
    w_pad = jnp.concatenate([w_in[:, :1920], kr_blk], axis=1)
    wqb = _to_head_lanes(w_q_b.astype(BF16).reshape(D_Q_LORA, D_HEADS, qk)).reshape(D_Q_LORA, D_HEADS * LANES)
    kvb = w_kv_b.astype(BF16).reshape(D_KV_LORA, D_HEADS, D_NOPE + D_V)
    wk = _to_head_lanes(jnp.concatenate([kvb[:, :, :D_NOPE], jnp.zeros((D_KV_LORA, D_HEADS, D_ROPE), BF16)], axis=-1))
    wkv = jnp.concatenate([wk.reshape(D_KV_LORA, D_HEADS * LANES),
                           kvb[:, :, D_NOPE:].reshape(D_KV_LORA, D_HEADS * D_V)], axis=1)
    inv = 1.0 / (ROPE_THETA ** (jnp.arange(ROPE_HALF, dtype=F32) / ROPE_HALF))
    ones = jnp.ones((ROPE_HALF,), F32)
    lanes = lambda v: _to_head_lanes(v).reshape(1, LANES)
    in_nope = lanes(jnp.concatenate([jnp.ones((D_NOPE,), F32), z(D_ROPE)]))
    in_rope = lanes(jnp.concatenate([z(D_NOPE), jnp.ones((D_ROPE,), F32)]))
    seg_nope = in_nope.T * in_nope * (1.0 / D_NOPE)
    seg_rope = in_rope.T * in_rope * (1.0 / D_ROPE)
    two_heads = lambda m: jnp.kron(jnp.eye(MXU_COLS // LANES, dtype=F32), m).astype(BF16)
    return {
        "qseg": two_heads(seg_nope + seg_rope),
        "kseg": two_heads(seg_nope),
        "wk_dense": kvb[:, :, :D_NOPE].reshape(D_KV_LORA, D_HEADS * D_NOPE),
        "kseg_t": jnp.pad(jnp.kron(jnp.eye(D_HEADS, dtype=F32), jnp.full((1, D_NOPE), 1.0 / D_NOPE, F32)),
                          ((0, 16 - D_HEADS), (0, 0))).astype(BF16),
        "q_absorb": _to_head_lanes(jnp.concatenate(
            [w_kv_b.reshape(D_KV_LORA, D_HEADS, D_NOPE + D_V)[:, :, :D_NOPE] * kn_nope,
             jnp.zeros((D_KV_LORA, D_HEADS, D_ROPE), F32)], axis=-1)).transpose(1, 2, 0).astype(BF16),
        "wv_lanes": jnp.stack([jnp.pad(kvb[:, h, D_NOPE:], ((0, 0), ((h % 2) * D_V, LANES - D_V - (h % 2) * D_V)))
                               for h in range(D_HEADS)]),
        "w_in": w_pad,
        "conv_w": conv_w.astype(F32),
        "qan": q_a_norm.reshape(1, D_Q_LORA),
        "wqb": wqb,
        "qgain": lanes(jnp.concatenate([qn_nope, qn_rope])) * MLA_QSCALE,
        "kvan": kv_a_norm.reshape(1, D_KV_LORA),
        "krgain": lanes(jnp.concatenate([z(D_NOPE), kn_rope])),
        "invf": lanes(jnp.concatenate([z(D_NOPE), inv, inv])),
        "sgn": lanes(jnp.concatenate([z(D_NOPE), -ones, ones])),
        "wkv": wkv,
        "kgain": lanes(jnp.concatenate([kn_nope, z(D_ROPE)])),
        "place": _to_head_lanes(jnp.concatenate([jnp.zeros((D_ROPE, D_NOPE), BF16), jnp.eye(D_ROPE, dtype=BF16)], axis=1)),
    }


def _mla_prompt_kernel(q_ref, k_ref, v_ref, o_ref, s_scr, e_scr, r_scr, *, seq):
    qb = min(MLA_QBLOCK, seq)
    hb = qb // 2
    lo = lax.broadcasted_iota(jnp.int32, (qb, LANES), 1) < D_V
    first_half = lax.broadcasted_iota(jnp.int32, (CHUNK, LANES), 1) < CHUNK
    scores = lambda q, k: lax.dot_general(q, k, _NT, preferred_element_type=F32)
    top, bot = slice(0, hb), slice(hb, qb)

    def spans(i):
        tk = qb * (i + 1)
        return tk - qb, tk - hb, tk

    def qk(i, h):
        t0, t1, tk = spans(i)
        hl = slice(h * LANES, (h + 1) * LANES)
        q0 = i * qb
        if t0 > 0:
            s_scr[h, :, 0:t0] = scores(q_ref[0, q0:q0 + qb, hl], k_ref[0, 0:t0, hl])
        s_scr[h, top, t0:t1] = scores(q_ref[0, q0:q0 + hb, hl], k_ref[0, t0:t1, hl])
        s_scr[h, bot, t0:tk] = scores(q_ref[0, q0 + hb:q0 + qb, hl], k_ref[0, t0:tk, hl])

    def softmax(i, h):
        t0, t1, tk = spans(i)
        for rb in range(qb // CHUNK):
            rows = slice(rb * CHUNK, (rb + 1) * CHUNK)
            visible = t0 + CHUNK * (rb + 1)
            width = t1 if rb < hb // CHUNK else tk
            n_full, ragged = visible // LANES, visible % LANES != 0
            blk = lambda c: s_scr[h, rows, c * LANES:(c + 1) * LANES]
            cols = [blk(c) for c in range(n_full)]
            if ragged:
                cols.append(jnp.where(first_half, blk(n_full), NEG_INF))
            mm = cols[0]
            for c in cols[1:]:
                mm = jnp.maximum(mm, c)
            m = _row_stat(jnp.max, mm)
            acc = None
            for c in range(width // LANES):
                if c < len(cols):
                    e = jnp.exp2(blk(c) - m)
                    if c >= n_full:
                        e = jnp.where(first_half, e, 0.0)
                    acc = e if acc is None else acc + e
                else:
                    e = jnp.zeros((CHUNK, LANES), F32)
                e_scr[h, rows, c * LANES:(c + 1) * LANES] = e.astype(BF16)
            r_scr[h, rows, :] = 1.0 / _row_stat(jnp.sum, acc)

    def pv(i, h):
        t0, t1, tk = spans(i)
        out = jnp.concatenate([_dot(e_scr[h, top, t0:t1], v_ref[0, t0:t1, :]),
                               _dot(e_scr[h, bot, t0:tk], v_ref[0, t0:tk, :])], axis=0)
        if t0 > 0:
            out = out + _dot(e_scr[h, :, 0:t0], v_ref[0, 0:t0, :])
        return out * r_scr[h]

    tasks = [(i, h) for i in range(seq // qb) for h in range(2)]
    n = len(tasks)
    outs = {}
    qk(*tasks[0])
    if n > 1:
        qk(*tasks[1])
    softmax(*tasks[0])
    for k, (i, h) in enumerate(tasks):
        if k + 2 < n:
            qk(*tasks[k + 2])
        if k + 1 < n:
            softmax(*tasks[k + 1])
        outs[h] = pv(i, h)
        if h == 1:
            o_ref[0, i * qb:(i + 1) * qb, :] = jnp.where(lo, outs[0], outs[1]).astype(BF16)


def _mla_prompt(q, k, v):
    nb, seq, _ = q.shape
    qb = min(MLA_QBLOCK, seq)
    return pl.pallas_call(
        functools.partial(_mla_prompt_kernel, seq=seq),
        grid=(nb, D_HEADS // 2),
        in_specs=[pl.BlockSpec((1, seq, 2 * LANES), lambda b, p: (b, 0, p)),
                  pl.BlockSpec((1, seq, 2 * LANES), lambda b, p: (b, 0, p)),
                  pl.BlockSpec((1, seq, LANES), lambda b, p: (b, 0, p))],
        out_specs=pl.BlockSpec((1, seq, LANES), lambda b, p: (b, 0, p)),
        out_shape=jax.ShapeDtypeStruct((nb, seq, D_HEADS * D_V), BF16),
        scratch_shapes=[pltpu.VMEM((2, qb, seq), F32), pltpu.VMEM((2, qb, seq), BF16),
                        pltpu.VMEM((2, qb, LANES), F32)],
        compiler_params=_cparams("arbitrary", "arbitrary"),
        name="mla_prompt",
    )(q, k, v)


MLA_CACHE_ROWS = 512


def _mla_sample_kernel(q_ref, ckv_ref, kpe_ref, kn_ref, vn_ref, place_ref, wk_ref, kseg_ref, qabs_ref, wv_ref,
                       o_ref, ckvb_scr, kpeb_scr, sq_scr, s1_scr, s2_scr, e_scr):
    nq, past = q_ref.shape[1], ckv_ref.shape[1]
    lane_blocks = past // LANES
    lo = lax.broadcasted_iota(jnp.int32, (nq, LANES), 1) < D_V
    head = lambda h: slice(h * LANES, (h + 1) * LANES)

    for r0 in range(0, past, MLA_CACHE_ROWS):
        rows = slice(r0, r0 + MLA_CACHE_ROWS)
        c = ckv_ref[0, rows, :].astype(BF16)
        ckvb_scr[rows, :] = c
        kpeb_scr[rows, :] = _dot(kpe_ref[0, rows, :].astype(BF16), place_ref[...]).astype(BF16)
        kraw = _dot(c, wk_ref[...])
        sq_scr[rows, :] = (kraw * kraw).astype(BF16)
    r_t = lax.rsqrt(lax.dot_general(kseg_ref[...], sq_scr[...], _NT, preferred_element_type=F32) + EPS)

    q_abs = jnp.concatenate([_dot(q_ref[0, :, head(h)], qabs_ref[h]).astype(BF16) for h in range(D_HEADS)], axis=0)
    q_all = jnp.concatenate([q_ref[0, :, head(h)] for h in range(D_HEADS)], axis=0)
    s1_scr[...] = lax.dot_general(q_abs, ckvb_scr[...], _NT, preferred_element_type=F32)
    s2_scr[...] = lax.dot_general(q_all, kpeb_scr[...], _NT, preferred_element_type=F32)

    inv_den, e_new = [], []
    for h in range(D_HEADS):
        rows = slice(h * nq, (h + 1) * nq)
        s_new = lax.dot_general(q_ref[0, :, head(h)], kn_ref[0, :, head(h)], _NT, preferred_element_type=F32)
        mm = None
        for c in range(lane_blocks):
            cl = slice(c * LANES, (c + 1) * LANES)
            blk = s1_scr[rows, cl] * r_t[h:h + 1, cl] + s2_scr[rows, cl]
            s1_scr[rows, cl] = blk
            mm = blk if mm is None else jnp.maximum(mm, blk)
        m = jnp.maximum(jnp.max(mm, axis=-1, keepdims=True), jnp.max(s_new, axis=-1, keepdims=True))
        acc = None
        for c in range(lane_blocks):
            cl = slice(c * LANES, (c + 1) * LANES)
            e = jnp.exp2(s1_scr[rows, cl] - m)
            acc = e if acc is None else acc + e
            e_scr[rows, cl] = e.astype(BF16)
        en = jnp.exp2(s_new - m)
        inv_den.append(1.0 / (jnp.sum(acc, axis=-1, keepdims=True) + jnp.sum(en, axis=-1, keepdims=True)))
        e_new.append(en.astype(BF16))

    latent = _dot(e_scr[...], ckvb_scr[...])
    for p in range(D_HEADS // 2):
        halves = []
        for half in range(2):
            h = 2 * p + half
            rows = slice(h * nq, (h + 1) * nq)
            o = _dot(latent[rows].astype(BF16), wv_ref[h]) + _dot(e_new[h], vn_ref[0, :, head(p)])
            halves.append(o * inv_den[h])
        o_ref[0, :, head(p)] = jnp.where(lo, halves[0], halves[1]).astype(BF16)


def _mla_sample(q, ckv, kpe, kn, vn, ow):
    nb, nq, _ = q.shape
    past = ckv.shape[1]
    assert past % MLA_CACHE_ROWS == 0
    per_b = lambda r, w: pl.BlockSpec((1, r, w), lambda b: (b, 0, 0))
    consts = [ow["place"], ow["wk_dense"], ow["kseg_t"], ow["q_absorb"], ow["wv_lanes"]]
    rows = D_HEADS * nq
    return pl.pallas_call(
        _mla_sample_kernel,
        grid=(nb,),
        in_specs=[per_b(nq, D_HEADS * LANES), per_b(past, D_KV_LORA), per_b(past, D_ROPE),
                  per_b(nq, D_HEADS * LANES), per_b(nq, D_HEADS * D_V)] + [_resident(a.shape) for a in consts],
        out_specs=per_b(nq, D_HEADS * D_V),
        out_shape=jax.ShapeDtypeStruct((nb, nq, D_HEADS * D_V), BF16),
        scratch_shapes=[pltpu.VMEM((past, D_KV_LORA), BF16), pltpu.VMEM((past, LANES), BF16),
                        pltpu.VMEM((past, D_HEADS * D_NOPE), BF16), pltpu.VMEM((rows, past), F32),
                        pltpu.VMEM((rows, past), F32), pltpu.VMEM((rows, past), BF16)],
        compiler_params=_cparams("arbitrary"),
        name="mla_sample",
    )(q, ckv, kpe, kn, vn, *consts)


def _even_layer(xp, xs, nb, seq, ndb, dseq, norm_g, w_in, w_out, a_qn, a_kn, a_sinks, b_qn, b_kn, b_rel,
                t5_table, ck_a, cv_a, ck_b, cv_b, ffn2):
    w_in = _to_bf16(w_in)
    w_out = _to_bf16(w_out)
    w_in_p = jnp.concatenate([_pair_heads(w_in[:, :A_Q], 1), w_in[:, A_Q:]], axis=1)
    woa = _pair_heads(w_out[:A_Q], 0)
    wob = w_out[A_Q:]
    ones = lambda n: jnp.ones((n,), F32)
    scale = HEAD_DIM ** -0.5 * LOG2E
    gain_row = jnp.concatenate([jnp.tile(a_qn, A_HEADS) * scale, jnp.tile(a_kn, A_KV_HEADS), ones(A_KV),
                                jnp.tile(b_qn, B_HEADS) * scale, jnp.tile(b_kn, B_HEADS), ones(B_QKV)]
                               ).reshape(1, EVEN_IN).astype(F32)
    bias_a = _bias_a(t5_table) * LOG2E
    bias_b = _bias_b(b_rel) * LOG2E
    sink = jnp.broadcast_to(jnp.repeat(a_sinks.astype(F32) * LOG2E, PAIR_ROWS)[:, None],
                            (A_HEADS * PAIR_ROWS, LANES))
    la, lb = min(WINDOW, seq), min(B_REACH, seq)

    aq, ak, av, bq, bk, bv, cak, cav, cbk, cbv = _even_inproj(xp, norm_g, w_in_p, gain_row, nb, la, lb)
    r3 = lambda a: a.reshape(nb, seq, a.shape[-1])
    ya = _attn_a(r3(aq), r3(ak), r3(av), bias_a, sink, WINDOW, True)
    yb = _attn_b(r3(bq), r3(bk), r3(bv), bias_b, B_REACH, True)
    xp = _ffn(xp, *ffn2, out_proj=(ya.reshape(nb * seq, A_Q), yb.reshape(nb * seq, B_QKV), woa, wob))
    st_p = (cak.reshape(nb, la, A_KV_HEADS, HEAD_DIM), cav.reshape(nb, la, A_KV_HEADS, HEAD_DIM),
            cbk.reshape(nb, lb, B_HEADS, HEAD_DIM), cbv.reshape(nb, lb, B_HEADS, HEAD_DIM))

    ts = ndb * dseq
    aq, ak, av, bq, bk, bv, nak, nav, nbk, nbv = _even_inproj(xs, norm_g, w_in_p, gain_row, 1, ts, ts)
    pad_q = lambda a, rows: jnp.pad(a.reshape(ndb, dseq, a.shape[-1]), ((0, 0), (0, rows - dseq), (0, 0)))

    def window(cache, new, rows):
        w = cache.shape[-2] * cache.shape[-1]
        full = jnp.concatenate([cache.reshape(ndb, -1, w), new.reshape(ndb, dseq, w)], axis=1)
        buf = jnp.pad(full, ((0, 0), (0, rows - dseq), (0, 0))).astype(BF16)
        return full[:, dseq:].reshape(cache.shape), buf

    st_ak, kbuf_a = window(ck_a, nak, PAIR_ROWS)
    st_av, vbuf_a = window(cv_a, nav, PAIR_ROWS)
    st_bk, kbuf_b = window(ck_b, nbk, B_STEP_ROWS)
    st_bv, vbuf_b = window(cv_b, nbv, B_STEP_ROWS)
    ya = _attn_a(pad_q(aq, PAIR_ROWS), kbuf_a, vbuf_a, bias_a, sink, WINDOW, False)[:, :dseq]
    yb = _attn_b(pad_q(bq, B_STEP_ROWS), kbuf_b, vbuf_b, bias_b, B_REACH, False)[:, :dseq]
    xs = _ffn(xs, *ffn2, out_proj=(ya.reshape(ts, A_Q), yb.reshape(ts, B_QKV), woa, wob))
    return xp, xs, st_p, (st_ak, st_av, st_bk, st_bv)


def _odd_layer(xp, xs, nb, seq, ndb, dseq, past, norm_g, ow, w_out, conv_prev, c_ckv, c_kpe, ffn2):
    w_out = _to_bf16(w_out)
    woc = w_out[:C_WIDTH]
    wod = w_out[C_WIDTH:]

    zero_init = jnp.zeros((nb, 8, C_WIDTH), F32)
    yc, q, k, v, ckv, kpe, cs = _odd_inproj(xp, norm_g, ow, zero_init, nb, 0)
    r3 = lambda a: a.reshape(nb, seq, a.shape[-1])
    yd = _mla_prompt(r3(q), r3(k), r3(v))
    xp = _ffn(xp, *ffn2, out_proj=(yc, yd.reshape(nb * seq, D_HEADS * D_V), woc, wod))
    st_p = (cs, ckv.reshape(nb, seq, D_KV_LORA), kpe.reshape(nb, seq, D_ROPE))

    ts = ndb * dseq
    init = jnp.pad(conv_prev.astype(F32), ((0, 0), (8 - (CONV_W - 1), 0), (0, 0)))
    yc, q, kn, vn, ckv, kpe, cs = _odd_inproj(xs, norm_g, ow, init, ndb, past)
    s3 = lambda a: a.reshape(ndb, dseq, a.shape[-1])
    yd = _mla_sample(s3(q), c_ckv, c_kpe, s3(kn), s3(vn), ow)
    xs = _ffn(xs, *ffn2, out_proj=(yc, yd.reshape(ts, D_HEADS * D_V), woc, wod))
    st_s = (cs, ckv.reshape(ndb, dseq, D_KV_LORA), kpe.reshape(ndb, dseq, D_ROPE))
    return xp, xs, st_p, st_s


def kernel(x_prompt, x_sample, cache_a_k, cache_a_v, cache_b_k, cache_b_v, state_c_conv, cache_d_ckv, cache_d_kpe, ff1_norm, ff1_w_gu, ff1_w_down, mix_norm, ff2_norm, ff2_w_gu, ff2_w_down, t5_bias_table, ev_w_in, ev_w_out, a_q_norm, a_k_norm, a_sinks, b_q_norm, b_k_norm, b_rel_bias, od_w_in, od_w_out, c_conv_w, d_q_a_norm, d_w_q_b, d_kv_a_norm, d_w_kv_b, d_q_nope_norm, d_q_rope_norm, d_k_nope_norm, d_k_rope_norm):
    nb, seq, _ = x_prompt.shape
    ndb, dseq, _ = x_sample.shape
    past = cache_d_ckv.shape[2]
    depth = ff1_norm.shape[0]
    assert seq % TOKEN_TILE == 0 and dseq == CHUNK and past % CHUNK == 0
    xp = x_prompt.reshape(nb * seq, D_MODEL)
    xs = x_sample.reshape(ndb * dseq, D_MODEL)
    even_p, even_s, odd_p, odd_s = [], [], [], []
    for l in range(depth):
        i = l // 2
        ffn2 = (ff2_norm[l], ff2_w_gu, ff2_w_down, l)
        xp = _ffn(xp, ff1_norm[l], ff1_w_gu, ff1_w_down, l)
        xs = _ffn(xs, ff1_norm[l], ff1_w_gu, ff1_w_down, l)
        if l % 2 == 0:
            xp, xs, sp, ss = _even_layer(
                xp, xs, nb, seq, ndb, dseq, mix_norm[l], ev_w_in[i], ev_w_out[i], a_q_norm[i], a_k_norm[i],
                a_sinks[i], b_q_norm[i], b_k_norm[i], b_rel_bias[i], t5_bias_table,
                cache_a_k[i], cache_a_v[i], cache_b_k[i], cache_b_v[i], ffn2)
            even_p.append(sp)
            even_s.append(ss)
        else:
            ow = _odd_weights(od_w_in[i], c_conv_w[i], d_q_a_norm[i], d_w_q_b[i], d_kv_a_norm[i], d_w_kv_b[i],
                              d_q_nope_norm[i], d_q_rope_norm[i], d_k_nope_norm[i], d_k_rope_norm[i])
            xp, xs, sp, ss = _odd_layer(xp, xs, nb, seq, ndb, dseq, past, mix_norm[l], ow, od_w_out[i],
                                        state_c_conv[i], cache_d_ckv[i], cache_d_kpe[i], ffn2)
            odd_p.append(sp)
            odd_s.append(ss)
    stack = lambda group, j: group[0][j][None] if len(group) == 1 else jnp.stack([g[j] for g in group])
    return (xp.reshape(nb, seq, D_MODEL), xs.reshape(ndb, dseq, D_MODEL),
            stack(even_p, 0), stack(even_p, 1), stack(even_p, 2), stack(even_p, 3),
            stack(odd_p, 0), stack(odd_p, 1), stack(odd_p, 2),
            stack(even_s, 0), stack(even_s, 1), stack(even_s, 2), stack(even_s, 3),
            stack(odd_s, 0), stack(odd_s, 1), stack(odd_s, 2))
```

```python
import functools
import math

import jax
import jax.numpy as jnp
from jax import lax
from jax.experimental import pallas as pl
from jax.experimental.pallas import tpu as pltpu

F32 = jnp.float32
BF16 = jnp.bfloat16

D_MODEL = 1024
CHUNK = 64
HEAD_DIM = 64
EPS = 1e-6
A_HEADS = 8
A_KV_HEADS = 2
WINDOW = 128
T5_BUCKETS = 32
T5_MAX_DIST = 128
B_HEADS = 8
B_REACH = 512
B_MAX_REL = 128
C_WIDTH = 512
CONV_W = 3
D_HEADS = 8
D_Q_LORA = 256
D_KV_LORA = 128
D_NOPE = 64
D_ROPE = 32
D_V = 64
ROPE_THETA = 10000.0
FFN_DIM = 2816
A_Q = A_HEADS * HEAD_DIM
A_KV = A_KV_HEADS * HEAD_DIM
B_QKV = B_HEADS * HEAD_DIM
EVEN_IN = A_Q + 2 * A_KV + 3 * B_QKV

LANES = 128
TOKEN_TILE = 512
CAST_BLOCK_BYTES = 6 * 1024 * 1024
MXU_COLS = 256
FFN_CHUNK = MXU_COLS
PAIR_ROWS = 2 * CHUNK
MLA_QBLOCK = 512
VMEM_LIMIT_BYTES = 56 * 1024 * 1024
LOG2E = math.log2(math.e)
MLA_QSCALE = (D_NOPE + D_ROPE) ** -0.5 * LOG2E
NEG_INF = float("-inf")

_NT = (((1,), (1,)), ((), ()))


def _cparams(*sem):
    return pltpu.CompilerParams(dimension_semantics=sem, vmem_limit_bytes=VMEM_LIMIT_BYTES)


def _resident(shape):
    zeros = (0,) * len(shape)
    return pl.BlockSpec(shape, lambda *_: zeros, pipeline_mode=pl.Buffered(1))


def _rms(x, g):
    ms = jnp.mean(x * x, axis=-1, keepdims=True)
    return (x * lax.rsqrt(ms + EPS)) * g


def _dot(a, b):
    return jnp.dot(a, b, preferred_element_type=F32)


GU_STAGE_ROWS = 64
DOWN_STAGE_ROWS = 352


def _stream_cast(src_hbm, dst_scr, stage, sem, rows):
    n = src_hbm.shape[0] // rows
    copy = lambda c: pltpu.make_async_copy(src_hbm.at[pl.ds(c * rows, rows), :], stage.at[c % 2], sem.at[c % 2])
    copy(0).start()
    for c in range(n):
        if c + 1 < n:
            copy(c + 1).start()
        copy(c).wait()
        dst_scr[c * rows:(c + 1) * rows, :] = stage[c % 2].astype(BF16)


def _ffn_tile(x_ref, y_refs, wo_refs, g_ref, wgu_ref, wd_ref, o_ref, h_scr, a_scr, n_chunks):
    rows = x_ref.shape[0]
    if rows < h_scr.shape[0]:
        h_scr, a_scr = h_scr.at[pl.ds(0, rows)], a_scr.at[pl.ds(0, rows)]
    if y_refs:
        (ya_ref, yb_ref), (woa_ref, wob_ref) = y_refs, wo_refs
        y = _dot(ya_ref[...], woa_ref[...]) + _dot(yb_ref[...], wob_ref[...])
        o_ref[...] = x_ref[...] + y
        res_ref = o_ref
    else:
        res_ref = x_ref
    h_scr[...] = _rms(res_ref[...], g_ref[...]).astype(BF16)
    for j in range(n_chunks):
        h = h_scr[...]
        g = _dot(h, wgu_ref[:, j * FFN_CHUNK:(j + 1) * FFN_CHUNK])
        u = _dot(h, wgu_ref[:, FFN_DIM + j * FFN_CHUNK:FFN_DIM + (j + 1) * FFN_CHUNK])
        a_scr[:, j * FFN_CHUNK:(j + 1) * FFN_CHUNK] = ((g * jax.nn.sigmoid(g)) * u).astype(BF16)
    o_ref[...] = res_ref[...] + 0.5 * _dot(a_scr[...], wd_ref[...])


def _ffn_kernel(*refs, n_chunks, fused_out, layer, n_main):
    if fused_out:
        (xp_ref, yap_ref, ybp_ref, xs_ref, yas_ref, ybs_ref, woa_ref, wob_ref, g_ref, wgu_hbm, wd_hbm,
         op_ref, os_ref, h_scr, a_scr, wgu_ref, wd_ref, gu_stage, d_stage, sem) = refs
        yp, ys, wo = (yap_ref, ybp_ref), (yas_ref, ybs_ref), (woa_ref, wob_ref)
    else:
        (xp_ref, xs_ref, g_ref, wgu_hbm, wd_hbm,
         op_ref, os_ref, h_scr, a_scr, wgu_ref, wd_ref, gu_stage, d_stage, sem) = refs
        yp = ys = wo = None
    i = pl.program_id(0)

    @pl.when(i == 0)
    def _():
        _stream_cast(wgu_hbm.at[layer], wgu_ref, gu_stage, sem, GU_STAGE_ROWS)
        _stream_cast(wd_hbm.at[layer], wd_ref, d_stage, sem, DOWN_STAGE_ROWS)

    @pl.when(i < n_main)
    def _():
        _ffn_tile(xp_ref, yp, wo, g_ref, wgu_ref, wd_ref, op_ref, h_scr, a_scr, n_chunks)

    @pl.when(i == n_main)
    def _():
        _ffn_tile(xs_ref, ys, wo, g_ref, wgu_ref, wd_ref, os_ref, h_scr, a_scr, n_chunks)


def _ffn(xp, xs, norm_g, w_gu, w_down, layer, out_proj=None):
    tp, ts = xp.shape[0], xs.shape[0]
    tm = min(TOKEN_TILE, tp)
    n_main = tp // tm
    n_chunks = FFN_DIM // FFN_CHUNK
    assert n_main * tm == tp and ts <= tm and ts % 16 == 0
    assert D_MODEL % GU_STAGE_ROWS == 0 and FFN_DIM % DOWN_STAGE_ROWS == 0
    main = lambda w: pl.BlockSpec((tm, w), lambda i: (jnp.minimum(i, n_main - 1), 0))
    side_in = lambda w: pl.BlockSpec((ts, w), lambda i: (0, 0), pipeline_mode=pl.Buffered(1))
    side_out = lambda w: pl.BlockSpec((ts, w), lambda i: (0, 0))
    in_hbm = pl.BlockSpec(memory_space=pl.ANY)
    if out_proj is not None:
        yap, ybp, yas, ybs, woa, wob = out_proj
        in_specs = [main(D_MODEL), main(yap.shape[1]), main(ybp.shape[1]),
                    side_in(D_MODEL), side_in(yas.shape[1]), side_in(ybs.shape[1]),
                    _resident(woa.shape), _resident(wob.shape)]
        args = [xp, yap, ybp, xs, yas, ybs, woa, wob]
    else:
        in_specs = [main(D_MODEL), side_in(D_MODEL)]
        args = [xp, xs]
    in_specs += [_resident((1, D_MODEL)), in_hbm, in_hbm]
    args += [norm_g.reshape(1, D_MODEL), w_gu, w_down]
    return pl.pallas_call(
        functools.partial(_ffn_kernel, n_chunks=n_chunks, fused_out=out_proj is not None, layer=layer,
                          n_main=n_main),
        grid=(n_main + 1,),
        in_specs=in_specs,
        out_specs=[main(D_MODEL), side_out(D_MODEL)],
        out_shape=[jax.ShapeDtypeStruct((tp, D_MODEL), F32), jax.ShapeDtypeStruct((ts, D_MODEL), F32)],
        scratch_shapes=[pltpu.VMEM((tm, D_MODEL), BF16), pltpu.VMEM((tm, FFN_DIM), BF16),
                        pltpu.VMEM((D_MODEL, 2 * FFN_DIM), BF16), pltpu.VMEM((FFN_DIM, D_MODEL), BF16),
                        pltpu.VMEM((2, GU_STAGE_ROWS, 2 * FFN_DIM), F32),
                        pltpu.VMEM((2, DOWN_STAGE_ROWS, D_MODEL), F32),
                        pltpu.SemaphoreType.DMA((2,))],
        compiler_params=_cparams("arbitrary"),
        name="ffn_out" if out_proj is not None else "ffn",
    )(*args)


def _cast_kernel(x_ref, o_ref):
    o_ref[...] = x_ref[...].astype(BF16)


def _to_bf16(w, layer=None):
    r, c = w.shape[-2:]
    fits = [d for d in range(16, r + 1, 16) if r % d == 0 and d * c * 4 <= CAST_BLOCK_BYTES]
    tr = max(fits) if fits else r
    out_spec = pl.BlockSpec((tr, c), lambda i: (i, 0))
    in_spec = out_spec if layer is None else pl.BlockSpec((None, tr, c), lambda i: (layer, i, 0))
    return pl.pallas_call(
        _cast_kernel, grid=(r // tr,), in_specs=[in_spec], out_specs=out_spec,
        out_shape=jax.ShapeDtypeStruct((r, c), BF16),
        compiler_params=_cparams("arbitrary"), name="cast_bf16",
    )(w)


def _norm_halves(blk, gain):
    sq = blk * blk
    lo = lax.broadcasted_iota(jnp.int32, blk.shape, 1) < HEAD_DIM
    s_lo = jnp.sum(jnp.where(lo, sq, 0.0), axis=-1, keepdims=True)
    s_hi = jnp.sum(jnp.where(lo, 0.0, sq), axis=-1, keepdims=True)
    inv = jnp.where(lo, lax.rsqrt(s_lo * (1.0 / HEAD_DIM) + EPS), lax.rsqrt(s_hi * (1.0 / HEAD_DIM) + EPS))
    return (blk * inv) * gain


def _even_inproj_kernel(x_ref, g_ref, w_ref, gain_ref,
                        aq_ref, ak_ref, av_ref, bq_ref, bk_ref, bv_ref,
                        cak_ref, cav_ref, cbk_ref, cbv_ref, h_scr, z_scr, *, tm, tpb, rows_a, rows_b):
    i = pl.program_id(0)
    last = (i % tpb) == (tpb - 1)
    h_scr[...] = _rms(x_ref[...], g_ref[...]).astype(BF16)

    def lanes(c):
        return slice(c * LANES, (c + 1) * LANES)

    def emit(c):
        blk = z_scr[:, lanes(c)]
        gain = gain_ref[:, lanes(c)]
        if c < 4:
            aq_ref[:, lanes(c)] = _norm_halves(blk, gain).astype(BF16)
        elif c == 4:
            kn = _norm_halves(blk, gain)
            ak_ref[...] = kn.astype(BF16)
            z_scr[:, lanes(c)] = kn
        elif c == 5:
            av_ref[...] = blk.astype(BF16)
        elif c < 10:
            bq_ref[:, lanes(c - 6)] = _norm_halves(blk, gain).astype(BF16)
        elif c < 14:
            kn = _norm_halves(blk, gain)
            bk_ref[:, lanes(c - 10)] = kn.astype(BF16)
            z_scr[:, lanes(c)] = kn
        else:
            bv_ref[:, lanes(c - 14)] = blk.astype(BF16)

    for grp in range(EVEN_IN // MXU_COLS):
        cols = slice(grp * MXU_COLS, (grp + 1) * MXU_COLS)
        z_scr[:, cols] = _dot(h_scr[...], w_ref[:, cols])
        emit(2 * grp)
        emit(2 * grp + 1)

    @pl.when(last)
    def _():
        cak_ref[0] = z_scr[tm - rows_a:, lanes(4)]
        cav_ref[0] = z_scr[tm - rows_a:, lanes(5)]
        cbk_ref[0] = z_scr[tm - rows_b:, 10 * LANES:14 * LANES]
        cbv_ref[0] = z_scr[tm - rows_b:, 14 * LANES:18 * LANES]


def _even_inproj(x, norm_g, w_in, gain_row, n_batch, rows_a, rows_b):
    t = x.shape[0]
    tm = min(TOKEN_TILE, t)
    tpb = t // n_batch // tm
    assert tpb * tm * n_batch == t and rows_a <= tm and rows_b <= tm
    row = lambda w: pl.BlockSpec((tm, w), lambda i: (i, 0))
    cache = lambda r, w: pl.BlockSpec((1, r, w), lambda i: (i // tpb, 0, 0))
    bshape = lambda w: jax.ShapeDtypeStruct((t, w), BF16)
    cshape = lambda r, w: jax.ShapeDtypeStruct((n_batch, r, w), F32)
    return pl.pallas_call(
        functools.partial(_even_inproj_kernel, tm=tm, tpb=tpb, rows_a=rows_a, rows_b=rows_b),
        grid=(t // tm,),
        in_specs=[row(D_MODEL), _resident((1, D_MODEL)), _resident(w_in.shape), _resident(gain_row.shape)],
        out_specs=[row(A_Q), row(A_KV), row(A_KV), row(B_QKV), row(B_QKV), row(B_QKV),
                   cache(rows_a, A_KV), cache(rows_a, A_KV), cache(rows_b, B_QKV), cache(rows_b, B_QKV)],
        out_shape=[bshape(A_Q), bshape(A_KV), bshape(A_KV), bshape(B_QKV), bshape(B_QKV), bshape(B_QKV),
                   cshape(rows_a, A_KV), cshape(rows_a, A_KV), cshape(rows_b, B_QKV), cshape(rows_b, B_QKV)],
        scratch_shapes=[pltpu.VMEM((tm, D_MODEL), BF16), pltpu.VMEM((tm, EVEN_IN), F32)],
        compiler_params=_cparams("arbitrary"),
        name="even_inproj",
    )(x, norm_g.reshape(1, D_MODEL), w_in, gain_row)


def _pair_heads(a, axis):
    shape = a.shape
    split = shape[:axis] + (2, A_HEADS // 2, HEAD_DIM) + shape[axis + 1:]
    return a.reshape(split).swapaxes(axis, axis + 1).reshape(shape)


def _fill_padded(buf, src_ref, ctx):
    buf[0:ctx, :] = jnp.zeros((ctx, buf.shape[1]), buf.dtype)
    buf[ctx:, :] = src_ref[0]


def _windows(k_ref, v_ref, pad_scr, ctx, pad_front, step_rows, stream):
    win = ctx + step_rows
    if pad_front:
        kbuf, vbuf = pad_scr
        _fill_padded(kbuf, k_ref, ctx)
        _fill_padded(vbuf, v_ref, ctx)
        return (lambda r0: kbuf[pl.ds(r0, win), :]), (lambda r0: vbuf[pl.ds(r0, win), :])
    return (lambda r0: k_ref[stream, pl.ds(r0, win), :]), (lambda r0: v_ref[stream, pl.ds(r0, win), :])


def _fold_lanes(x, op):
    acc = x[:, :LANES]
    for c in range(1, x.shape[1] // LANES):
        acc = op(acc, x[:, c * LANES:(c + 1) * LANES])
    return acc


def _row_stat(reduce_fn, x):
    return jnp.broadcast_to(reduce_fn(x, axis=-1, keepdims=True), x.shape)


PAIR_SLOTS = 4


def _run_steps(make_stages, n_streams, n_steps, n_front, step_rows, lookahead):
    stages = [make_stages(stream) for stream in range(n_streams)]
    tasks = [(stream, t) for stream in range(n_streams) for t in range(n_steps)]

    def run(stage, k):
        stream, t = tasks[k]
        stages[stream][stage](t * step_rows, k % PAIR_SLOTS, t < n_front)

    n = len(tasks)
    if not lookahead:
        for k in range(n):
            for stage in range(3):
                run(stage, k)
        return
    run(0, 0)
    if n > 1:
        run(0, 1)
    run(1, 0)
    for k in range(n):
        if k + 2 < n:
            run(0, k + 2)
        if k + 1 < n:
            run(1, k + 1)
        run(2, k)


def _band_bias(ext_row, n_rows, ctx):
    win = ctx + n_rows
    period = ext_row.shape[1]
    t = pltpu.roll(jnp.broadcast_to(ext_row, (n_rows, period)), 0, 1, stride=1, stride_axis=0)[:, :win]
    chunk = lax.broadcasted_iota(jnp.int32, (n_rows, win), 0) // CHUNK
    col = lax.broadcasted_iota(jnp.int32, (n_rows, win), 1)
    visible = (col >= CHUNK * chunk) & (col < ctx + CHUNK * (chunk + 1))
    return jnp.where(visible, t, NEG_INF)


def _attn_a_kernel(q_ref, k_ref, v_ref, ext_ref, sink_ref, o_ref, bias_ref, s_scr, e_scr, *pad_scr,
                   nq, ctx, pad_front):
    win = ctx + PAIR_ROWS
    lo = lax.broadcasted_iota(jnp.int32, (PAIR_ROWS, LANES), 1) < HEAD_DIM

    @pl.when(pl.program_id(0) == 0)
    def _():
        for head in range(A_HEADS):
            bias_ref[head * PAIR_ROWS:(head + 1) * PAIR_ROWS, :] = _band_bias(
                ext_ref[head:head + 1, :], PAIR_ROWS, ctx)

    def make_stages(stream):
        kwin, vwin = _windows(k_ref, v_ref, pad_scr, ctx, pad_front, PAIR_ROWS, stream)
        return (functools.partial(scores, stream, kwin), functools.partial(softmax, stream),
                functools.partial(values, stream, vwin))

    def scores(stream, kwin, r0, slot, masked):
        q_lo, q_hi = [], []
        for p in range(4):
            qp = q_ref[stream, pl.ds(r0, PAIR_ROWS), p * LANES:(p + 1) * LANES]
            zero = jnp.zeros_like(qp)
            q_lo.append(jnp.where(lo, qp, zero))
            q_hi.append(jnp.where(lo, zero, qp))
        q2 = jnp.concatenate(q_lo + q_hi, axis=0)
        s_scr[slot] = lax.dot_general(q2, kwin(r0), _NT, preferred_element_type=F32)

    def softmax(stream, r0, slot, masked):
        for head in range(A_HEADS):
            rows = slice(head * PAIR_ROWS, (head + 1) * PAIR_ROWS)
            s = s_scr[slot, rows, :] + bias_ref[rows, :]
            if masked:
                col = lax.broadcasted_iota(jnp.int32, (PAIR_ROWS, win), 1)
                s = jnp.where(col + r0 >= ctx, s, NEG_INF)
            sk = sink_ref[rows, :]
            m = jnp.maximum(_row_stat(jnp.max, _fold_lanes(s, jnp.maximum)), sk)
            e = [jnp.exp2(s[:, c * LANES:(c + 1) * LANES] - m) for c in range(win // LANES)]
            den = _row_stat(jnp.sum, functools.reduce(jnp.add, e)) + jnp.exp2(sk - m)
            inv = 1.0 / den
            for c in range(win // LANES):
                e_scr[slot, rows, c * LANES:(c + 1) * LANES] = (e[c] * inv).astype(BF16)

    def values(stream, vwin, r0, slot, masked):
        o2 = _dot(e_scr[slot], vwin(r0))
        half = 4 * PAIR_ROWS
        for p in range(4):
            o = jnp.where(lo, o2[p * PAIR_ROWS:(p + 1) * PAIR_ROWS],
                          o2[half + p * PAIR_ROWS:half + (p + 1) * PAIR_ROWS])
            o_ref[stream, pl.ds(r0, PAIR_ROWS), p * LANES:(p + 1) * LANES] = o.astype(BF16)

    _run_steps(make_stages, q_ref.shape[0], nq // PAIR_ROWS, ctx // PAIR_ROWS if pad_front else 0, PAIR_ROWS,
               lookahead=False)


def _streams_per_block(nb, pad_front):
    return 1 if pad_front else nb


def _attn_a(q, k, v, ext, sink, ctx, pad_front):
    nb, nq, _ = q.shape
    nk = k.shape[1]
    assert nq % PAIR_ROWS == 0 and nk == (nq if pad_front else ctx + nq)
    bb = _streams_per_block(nb, pad_front)
    per_b = lambda r, w: pl.BlockSpec((bb, r, w), lambda b: (b, 0, 0))
    rows, win = A_HEADS * PAIR_ROWS, ctx + PAIR_ROWS
    scratch = [pltpu.VMEM((rows, win), F32),
               pltpu.VMEM((PAIR_SLOTS, rows, win), F32), pltpu.VMEM((PAIR_SLOTS, rows, win), BF16)]
    if pad_front:
        scratch += [pltpu.VMEM((ctx + nq, A_KV), BF16)] * 2
    return pl.pallas_call(
        functools.partial(_attn_a_kernel, nq=nq, ctx=ctx, pad_front=pad_front),
        grid=(nb // bb,),
        in_specs=[per_b(nq, A_Q), per_b(nk, A_KV), per_b(nk, A_KV), _resident(ext.shape), _resident(sink.shape)],
        out_specs=per_b(nq, A_Q),
        out_shape=jax.ShapeDtypeStruct((nb, nq, A_Q), BF16),
        scratch_shapes=scratch,
        compiler_params=_cparams("arbitrary"),
        name="attn_a",
    )(q, k, v, ext, sink)


B_STEP_ROWS = 4 * CHUNK


def _attn_b_kernel(q_ref, k_ref, v_ref, ext_ref, o_ref, bias_scr, s_scr, e_scr, r_scr, *pad_scr,
                   nq, ctx, pad_front):
    win = ctx + B_STEP_ROWS
    lo = lax.broadcasted_iota(jnp.int32, (B_STEP_ROWS, LANES), 1) < HEAD_DIM
    chunks = B_STEP_ROWS // CHUNK
    pair_idx = pl.program_id(1)

    @pl.when(pl.program_id(0) == 0)
    def _():
        for half in range(2):
            bias_scr[pair_idx, half * B_STEP_ROWS:(half + 1) * B_STEP_ROWS, :] = _band_bias(
                ext_ref[0, half:half + 1, :], B_STEP_ROWS, ctx)

    bias_ref = bias_scr.at[pair_idx]

    def make_stages(stream):
        kwin, vwin = _windows(k_ref, v_ref, pad_scr, ctx, pad_front, B_STEP_ROWS, stream)
        return (functools.partial(scores, stream, kwin), functools.partial(softmax, stream),
                functools.partial(values, stream, vwin))

    def scores(stream, kwin, r0, slot, masked):
        qp = q_ref[stream, pl.ds(r0, B_STEP_ROWS), :]
        zero = jnp.zeros_like(qp)
        q2 = jnp.concatenate([jnp.where(lo, qp, zero), jnp.where(lo, zero, qp)], axis=0)
        s_scr[slot] = lax.dot_general(q2, kwin(r0), _NT, preferred_element_type=F32)

    def softmax(stream, r0, slot, masked):
        for rb in range(2 * chunks):
            rows = slice(rb * CHUNK, (rb + 1) * CHUNK)
            ci = rb % chunks
            c_lo, c_hi = CHUNK * ci // LANES, -(-(ctx + CHUNK * (ci + 1)) // LANES)
            band = slice(c_lo * LANES, c_hi * LANES)
            s = s_scr[slot, rows, band] + bias_ref[rows, band]
            if masked:
                col = lax.broadcasted_iota(jnp.int32, s.shape, 1) + c_lo * LANES
                s = jnp.where(col + r0 >= ctx, s, NEG_INF)
            m = _row_stat(jnp.max, _fold_lanes(s, jnp.maximum))
            e = [jnp.exp2(s[:, c * LANES:(c + 1) * LANES] - m) for c in range(c_hi - c_lo)]
            for c in range(win // LANES):
                blk = e[c - c_lo].astype(BF16) if c_lo <= c < c_hi else jnp.zeros((CHUNK, LANES), BF16)
                e_scr[slot, rows, c * LANES:(c + 1) * LANES] = blk
            r_scr[slot, rows, :] = 1.0 / _row_stat(jnp.sum, functools.reduce(jnp.add, e))

    def values(stream, vwin, r0, slot, masked):
        o2 = _dot(e_scr[slot], vwin(r0)) * r_scr[slot]
        o = jnp.where(lo, o2[:B_STEP_ROWS], o2[B_STEP_ROWS:])
        o_ref[stream, pl.ds(r0, B_STEP_ROWS), :] = o.astype(BF16)

    _run_steps(make_stages, q_ref.shape[0], nq // B_STEP_ROWS,
               -(-ctx // B_STEP_ROWS) if pad_front else 0, B_STEP_ROWS, lookahead=True)


def _attn_b(q, k, v, ext, ctx, pad_front):
    nb, nq, _ = q.shape
    nk = k.shape[1]
    assert nq % B_STEP_ROWS == 0 and nk == (nq if pad_front else ctx + nq)
    bb = _streams_per_block(nb, pad_front)
    blk = lambda r: pl.BlockSpec((bb, r, LANES), lambda b, p: (b, 0, p))
    rows, win = 2 * B_STEP_ROWS, ctx + B_STEP_ROWS
    scratch = [pltpu.VMEM((B_HEADS // 2, rows, win), F32),
               pltpu.VMEM((PAIR_SLOTS, rows, win), F32), pltpu.VMEM((PAIR_SLOTS, rows, win), BF16),
               pltpu.VMEM((PAIR_SLOTS, rows, LANES), F32)]
    if pad_front:
        scratch += [pltpu.VMEM((ctx + nq, LANES), BF16)] * 2
    return pl.pallas_call(
        functools.partial(_attn_b_kernel, nq=nq, ctx=ctx, pad_front=pad_front),
        grid=(nb // bb, B_HEADS // 2),
        in_specs=[blk(nq), blk(nk), blk(nk),
                  pl.BlockSpec((1,) + ext.shape[1:], lambda b, p: (p, 0, 0))],
        out_specs=blk(nq),
        out_shape=jax.ShapeDtypeStruct((nb, nq, B_QKV), BF16),
        scratch_shapes=scratch,
        compiler_params=_cparams("arbitrary", "arbitrary"),
        name="attn_b",
    )(q, k, v, ext)


def _t5_bucket(rel):
    nb = T5_BUCKETS // 2
    max_exact = nb // 2
    n = -rel
    ret = jnp.where(n < 0, nb, 0)
    n = jnp.abs(n)
    nf = jnp.maximum(n, 1).astype(F32)
    large = max_exact + (jnp.log(nf / max_exact) / math.log(T5_MAX_DIST / max_exact)
                         * (nb - max_exact)).astype(jnp.int32)
    large = jnp.minimum(large, nb - 1)
    return ret + jnp.where(n < max_exact, n, large)


def _band_values(ctx, n_rows, value_of_rel):
    n_cols = ctx + n_rows
    period = -(-(n_cols + n_rows - 1) // LANES) * LANES
    d = jnp.concatenate([jnp.arange(0, period - (n_rows - 1)), jnp.arange(-(n_rows - 1), 0)])
    return value_of_rel(d - ctx)


def _bias_a(t5_table):
    return _band_values(WINDOW, PAIR_ROWS, lambda rel: t5_table.astype(F32)[_t5_bucket(rel)].T)


def _bias_b(b_rel):
    ext = _band_values(B_REACH, B_STEP_ROWS,
                       lambda rel: b_rel.astype(F32)[:, jnp.clip(rel, -B_MAX_REL, B_MAX_REL) + B_MAX_REL])
    return ext.reshape(B_HEADS // 2, 2, ext.shape[-1])


ODD_IN_PAD = 2048
ROPE_HALF = D_ROPE // 2
X1_LO = LANES // 2
NOPE_SPLIT = X1_LO - ROPE_HALF


def _head_lane_source():
    zero = D_NOPE + D_ROPE
    src = []
    for lane in range(LANES):
        if lane < ROPE_HALF:
            src.append(D_NOPE + ROPE_HALF + lane)
        elif lane < X1_LO:
            src.append(lane - ROPE_HALF)
        elif lane < X1_LO + ROPE_HALF:
            src.append(D_NOPE + lane - X1_LO)
        elif lane < X1_LO + ROPE_HALF + D_NOPE - NOPE_SPLIT:
            src.append(NOPE_SPLIT + lane - X1_LO - ROPE_HALF)
        else:
            src.append(zero)
    return src


def _to_head_lanes(a):
    padded = jnp.concatenate([a, jnp.zeros(a.shape[:-1] + (1,), a.dtype)], axis=-1)
    return padded[..., jnp.asarray(_head_lane_source(), dtype=jnp.int32)]


def _lane_masks(shape):
    lane = lax.broadcasted_iota(jnp.int32, shape, 1)
    rope = (lane < ROPE_HALF) | ((lane >= X1_LO) & (lane < X1_LO + ROPE_HALF))
    used = lane < X1_LO + ROPE_HALF + D_NOPE - NOPE_SPLIT
    return used & ~rope, rope


def _rope(y, cosf, sinf):
    return y * cosf + pltpu.roll(y, LANES // 2, 1) * sinf


ROW_BLOCK = 128


def _row_blocks(tm):
    rb = min(ROW_BLOCK, tm)
    return [slice(r, r + rb) for r in range(0, tm, rb)]


def _segment_mean_squares(src_scr, cols, seg_ref, sq_scr, ms_scr, slot):
    for rows in _row_blocks(src_scr.shape[0]):
        v = src_scr[rows, cols]
        sq_scr[slot, rows, :] = (v * v).astype(BF16)
    ms_scr[slot] = _dot(sq_scr[slot], seg_ref[...])


def _expand_kv(ckvb_ref, kpe_blk_ref, wkv_ref, kgain_ref, kseg_ref, kv_scr, sq_scr, ms_scr, k_ref, v_ref,
               between=None):
    k_cols = D_HEADS * LANES
    tm = kv_scr.shape[0]
    for grp in range(kv_scr.shape[1] // MXU_COLS):
        if between and grp in between:
            between[grp]()
        cols = slice(grp * MXU_COLS, (grp + 1) * MXU_COLS)
        kv_scr[:, cols] = _dot(ckvb_ref[...], wkv_ref[:, cols])
        if cols.start < k_cols:
            slot = grp % 2
            _segment_mean_squares(kv_scr, cols, kseg_ref, sq_scr, ms_scr, slot)
        for rows in _row_blocks(tm):
            if cols.start < k_cols:
                for half in range(2):
                    h = 2 * grp + half
                    kb = kv_scr[rows, h * LANES:(h + 1) * LANES]
                    ms = ms_scr[slot, rows, half * LANES:(half + 1) * LANES]
                    kn = (kb * lax.rsqrt(ms + EPS)) * kgain_ref[...]
                    k_ref[rows, h * LANES:(h + 1) * LANES] = (kn + kpe_blk_ref[rows, :]).astype(BF16)
            else:
                v_ref[rows, cols.start - k_cols:cols.stop - k_cols] = kv_scr[rows, cols].astype(BF16)


def _odd_inproj_kernel(x_ref, g_ref, w_ref, cinit_ref, convw_ref, qan_ref, wqb_ref, qgain_ref,
                       kvan_ref, krgain_ref, invf_ref, sgn_ref, wkv_ref, kgain_ref, qseg_ref, kseg_ref,
                       yc_ref, q_ref, k_ref, v_ref, ckv_ref, kpe_ref, cs_ref,
                       h_scr, z_scr, uext_scr, qn_scr, q_scr, kv_scr, rot_scr, ckvb_scr, kpe_scr, sq_scr, ms_scr,
                       *, tm, tpb, pos0):
    i = pl.program_id(0)
    tile = i % tpb
    off = pl.multiple_of(tile * tm, tm)
    blocks = _row_blocks(tm)
    rb = blocks[0].stop

    @pl.when(tile == 0)
    def _():
        uext_scr[0:8, :] = cinit_ref[0]

    @pl.when(i < tpb)
    def _():
        row = lax.broadcasted_iota(jnp.int32, (tm, LANES), 0)
        ang = (row + (tile * tm + pos0)).astype(F32) * invf_ref[...]
        rot_scr[0, pl.ds(off, tm), :] = jnp.cos(ang)
        rot_scr[1, pl.ds(off, tm), :] = jnp.sin(ang) * sgn_ref[...]

    h_scr[...] = _rms(x_ref[...], g_ref[...]).astype(BF16)
    low_lanes = lax.broadcasted_iota(jnp.int32, (rb, LANES), 1) < ROPE_HALF

    def proj(c0, c1):
        z_scr[:, c0:c1] = _dot(h_scr[...], w_ref[:, c0:c1])

    def rotary(y, rows):
        cosf = rot_scr[0, pl.ds(off + rows.start, rb), :]
        sinf = rot_scr[1, pl.ds(off + rows.start, rb), :]
        return _rope(y, cosf, sinf)

    def conv_group(grp):
        c0 = grp * MXU_COLS
        cols = slice(c0, c0 + MXU_COLS)
        for base in (0, C_WIDTH, 2 * C_WIDTH):
            proj(base + c0, base + c0 + MXU_COLS)
        for rows in blocks:
            up = slice(rows.start + 8, rows.stop + 8)
            uext_scr[up, cols] = (z_scr[rows, C_WIDTH + c0:C_WIDTH + c0 + MXU_COLS]
                                  * z_scr[rows, 2 * C_WIDTH + c0:2 * C_WIDTH + c0 + MXU_COLS])
        for rows in blocks:
            yconv = convw_ref[0:1, cols] * uext_scr[rows.start + 6:rows.stop + 6, cols]
            yconv = yconv + convw_ref[1:2, cols] * uext_scr[rows.start + 7:rows.stop + 7, cols]
            yconv = yconv + convw_ref[2:3, cols] * uext_scr[rows.start + 8:rows.stop + 8, cols]
            yc_ref[rows, cols] = (z_scr[rows, cols] * yconv).astype(BF16)

    proj(1536, 1792)
    proj(1792, 2048)

    for rows in blocks:
        qn_scr[rows, :] = _rms(z_scr[rows, 1536:1792], qan_ref[...]).astype(BF16)
        ckv = _rms(z_scr[rows, 1792:1920], kvan_ref[...])
        ckv_ref[rows, :] = ckv
        ckvb_scr[rows, :] = ckv.astype(BF16)

    for grp in range(D_HEADS * LANES // MXU_COLS):
        cols = slice(grp * MXU_COLS, (grp + 1) * MXU_COLS)
        q_scr[:, cols] = _dot(qn_scr[...], wqb_ref[:, cols])
        slot = grp % 2
        _segment_mean_squares(q_scr, cols, qseg_ref, sq_scr, ms_scr, slot)
        for half in range(2):
            h = 2 * grp + half
            for rows in blocks:
                blk = q_scr[rows, h * LANES:(h + 1) * LANES]
                inv = lax.rsqrt(ms_scr[slot, rows, half * LANES:(half + 1) * LANES] + EPS)
                y = rotary((blk * inv) * qgain_ref[...], rows)
                q_ref[rows, h * LANES:(h + 1) * LANES] = y.astype(BF16)

    for rows in blocks:
        kb = z_scr[rows, 1920:2048]
        ms = jnp.sum(kb * kb, axis=-1, keepdims=True) * (1.0 / D_ROPE)
        kpe_blk = rotary((kb * lax.rsqrt(ms + EPS)) * krgain_ref[...], rows)
        kpe_scr[rows, :] = kpe_blk
        x1_then_x2 = jnp.where(low_lanes, pltpu.roll(kpe_blk, LANES - X1_LO, 1), pltpu.roll(kpe_blk, ROPE_HALF, 1))
        kpe_ref[rows, :] = x1_then_x2[:, 0:D_ROPE]
    _expand_kv(ckvb_scr, kpe_scr, wkv_ref, kgain_ref, kseg_ref, kv_scr, sq_scr, ms_scr, k_ref, v_ref,
               between={0: lambda: conv_group(0), 4: lambda: conv_group(1)})

    cs_ref[0] = uext_scr[tm + 6:tm + 8, :]
    uext_scr[0:8, :] = uext_scr[tm:tm + 8, :]


def _odd_inproj(x, norm_g, ow, conv_init, n_batch, pos0):
    t = x.shape[0]
    tm = min(TOKEN_TILE, t // n_batch)
    tpb = t // n_batch // tm
    assert tpb * tm * n_batch == t
    row = lambda w: pl.BlockSpec((tm, w), lambda i: (i, 0))
    per_b = lambda r, w: pl.BlockSpec((1, r, w), lambda i: (i // tpb, 0, 0))
    small = [ow["conv_w"], ow["qan"], ow["wqb"], ow["qgain"], ow["kvan"], ow["krgain"],
             ow["invf"], ow["sgn"], ow["wkv"], ow["kgain"], ow["qseg"], ow["kseg"]]
    kv_w = D_HEADS * LANES
    v_w = D_HEADS * D_V
    return pl.pallas_call(
        functools.partial(_odd_inproj_kernel, tm=tm, tpb=tpb, pos0=pos0),
        grid=(t // tm,),
        in_specs=[row(D_MODEL), _resident((1, D_MODEL)), _resident(ow["w_in"].shape), per_b(8, C_WIDTH)]
                 + [_resident(a.shape) for a in small],
        out_specs=[row(C_WIDTH), row(kv_w), row(kv_w), row(v_w), row(D_KV_LORA), row(D_ROPE),
                   per_b(CONV_W - 1, C_WIDTH)],
        out_shape=[jax.ShapeDtypeStruct((t, C_WIDTH), BF16), jax.ShapeDtypeStruct((t, kv_w), BF16),
                   jax.ShapeDtypeStruct((t, kv_w), BF16), jax.ShapeDtypeStruct((t, v_w), BF16),
                   jax.ShapeDtypeStruct((t, D_KV_LORA), F32), jax.ShapeDtypeStruct((t, D_ROPE), F32),
                   jax.ShapeDtypeStruct((n_batch, CONV_W - 1, C_WIDTH), F32)],
        scratch_shapes=[pltpu.VMEM((tm, D_MODEL), BF16), pltpu.VMEM((tm, ODD_IN_PAD), F32),
                        pltpu.VMEM((tm + 8, C_WIDTH), F32), pltpu.VMEM((tm, D_Q_LORA), BF16),
                        pltpu.VMEM((tm, kv_w), F32), pltpu.VMEM((tm, kv_w + v_w), F32),
                        pltpu.VMEM((2, tpb * tm, LANES), F32), pltpu.VMEM((tm, D_KV_LORA), BF16),
                        pltpu.VMEM((tm, LANES), F32), pltpu.VMEM((2, tm, MXU_COLS), BF16),
                        pltpu.VMEM((2, tm, MXU_COLS), F32)],
        compiler_params=_cparams("arbitrary"),
        name="odd_inproj",
    )(x, norm_g.reshape(1, D_MODEL), ow["w_in"], conv_init, *small)


def _odd_weights(w_in, conv_w, q_a_norm, w_q_b, kv_a_norm, w_kv_b, qn_nope, qn_rope, kn_nope, kn_rope):
    z = lambda n: jnp.zeros((n,), F32)
    qk = D_NOPE + D_ROPE
    w_in = _to_bf16(w_in)
    kr_blk = _to_head_lanes(jnp.concatenate([jnp.zeros((D_MODEL, D_NOPE), BF16), w_in[:, 1920:]], axis=1))
    w_pad = jnp.concatenate([w_in[:, :1920], kr_blk], axis=1)
    wqb = _to_head_lanes(w_q_b.astype(BF16).reshape(D_Q_LORA, D_HEADS, qk)).reshape(D_Q_LORA, D_HEADS * LANES)
    kvb = w_kv_b.astype(BF16).reshape(D_KV_LORA, D_HEADS, D_NOPE + D_V)
    wk = _to_head_lanes(jnp.concatenate([kvb[:, :, :D_NOPE], jnp.zeros((D_KV_LORA, D_HEADS, D_ROPE), BF16)], axis=-1))
    wkv = jnp.concatenate([wk.reshape(D_KV_LORA, D_HEADS * LANES),
                           kvb[:, :, D_NOPE:].reshape(D_KV_LORA, D_HEADS * D_V)], axis=1)
    inv = 1.0 / (ROPE_THETA ** (jnp.arange(ROPE_HALF, dtype=F32) / ROPE_HALF))
    ones = jnp.ones((ROPE_HALF,), F32)
    lanes = lambda v: _to_head_lanes(v).reshape(1, LANES)
    in_nope = lanes(jnp.concatenate([jnp.ones((D_NOPE,), F32), z(D_ROPE)]))
    in_rope = lanes(jnp.concatenate([z(D_NOPE), jnp.ones((D_ROPE,), F32)]))
    seg_nope = in_nope.T * in_nope * (1.0 / D_NOPE)
    seg_rope = in_rope.T * in_rope * (1.0 / D_ROPE)
    two_heads = lambda m: jnp.kron(jnp.eye(MXU_COLS // LANES, dtype=F32), m).astype(BF16)
    return {
        "qseg": two_heads(seg_nope + seg_rope),
        "kseg": two_heads(seg_nope),
        "wk_dense": kvb[:, :, :D_NOPE].reshape(D_KV_LORA, D_HEADS * D_NOPE),
        "kseg_t": jnp.pad(jnp.kron(jnp.eye(D_HEADS, dtype=F32), jnp.full((1, D_NOPE), 1.0 / D_NOPE, F32)),
                          ((0, 16 - D_HEADS), (0, 0))).astype(BF16),
        "q_absorb": _to_head_lanes(jnp.concatenate(
            [w_kv_b.reshape(D_KV_LORA, D_HEADS, D_NOPE + D_V)[:, :, :D_NOPE] * kn_nope,
             jnp.zeros((D_KV_LORA, D_HEADS, D_ROPE), F32)], axis=-1)).transpose(1, 2, 0).astype(BF16),
        "wv_lanes": jnp.stack([jnp.pad(kvb[:, h, D_NOPE:], ((0, 0), ((h % 2) * D_V, LANES - D_V - (h % 2) * D_V)))
                               for h in range(D_HEADS)]),
        "w_in": w_pad,
        "conv_w": conv_w.astype(F32),
        "qan": q_a_norm.reshape(1, D_Q_LORA),
        "wqb": wqb,
        "qgain": lanes(jnp.concatenate([qn_nope, qn_rope])) * MLA_QSCALE,
        "kvan": kv_a_norm.reshape(1, D_KV_LORA),
        "krgain": lanes(jnp.concatenate([z(D_NOPE), kn_rope])),
        "invf": lanes(jnp.concatenate([z(D_NOPE), inv, inv])),
        "sgn": lanes(jnp.concatenate([z(D_NOPE), -ones, ones])),
        "wkv": wkv,
        "kgain": lanes(jnp.concatenate([kn_nope, z(D_ROPE)])),
        "place": _to_head_lanes(jnp.concatenate([jnp.zeros((D_ROPE, D_NOPE), BF16), jnp.eye(D_ROPE, dtype=BF16)], axis=1)),
    }


def _mla_prompt_kernel(q_ref, k_ref, v_ref, o_ref, s_scr, e_scr, r_scr, *, seq):
    qb = min(MLA_QBLOCK, seq)
    hb = qb // 2
    lo = lax.broadcasted_iota(jnp.int32, (qb, LANES), 1) < D_V
    first_half = lax.broadcasted_iota(jnp.int32, (CHUNK, LANES), 1) < CHUNK
    scores = lambda q, k: lax.dot_general(q, k, _NT, preferred_element_type=F32)
    top, bot = slice(0, hb), slice(hb, qb)

    def spans(i):
        tk = qb * (i + 1)
        return tk - qb, tk - hb, tk

    def qk(i, h):
        t0, t1, tk = spans(i)
        hl = slice(h * LANES, (h + 1) * LANES)
        q0 = i * qb
        if t0 > 0:
            s_scr[h, :, 0:t0] = scores(q_ref[0, q0:q0 + qb, hl], k_ref[0, 0:t0, hl])
        s_scr[h, top, t0:t1] = scores(q_ref[0, q0:q0 + hb, hl], k_ref[0, t0:t1, hl])
        s_scr[h, bot, t0:tk] = scores(q_ref[0, q0 + hb:q0 + qb, hl], k_ref[0, t0:tk, hl])

    def softmax(i, h):
        t0, t1, tk = spans(i)
        for rb in range(qb // CHUNK):
            rows = slice(rb * CHUNK, (rb + 1) * CHUNK)
            visible = t0 + CHUNK * (rb + 1)
            width = t1 if rb < hb // CHUNK else tk
            n_full, ragged = visible // LANES, visible % LANES != 0
            blk = lambda c: s_scr[h, rows, c * LANES:(c + 1) * LANES]
            cols = [blk(c) for c in range(n_full)]
            if ragged:
                cols.append(jnp.where(first_half, blk(n_full), NEG_INF))
            mm = cols[0]
            for c in cols[1:]:
                mm = jnp.maximum(mm, c)
            m = _row_stat(jnp.max, mm)
            acc = None
            for c in range(width // LANES):
                if c < len(cols):
                    e = jnp.exp2(blk(c) - m)
                    if c >= n_full:
                        e = jnp.where(first_half, e, 0.0)
                    acc = e if acc is None else acc + e
                else:
                    e = jnp.zeros((CHUNK, LANES), F32)
                e_scr[h, rows, c * LANES:(c + 1) * LANES] = e.astype(BF16)
            r_scr[h, rows, :] = 1.0 / _row_stat(jnp.sum, acc)

    def pv(i, h):
        t0, t1, tk = spans(i)
        out = jnp.concatenate([_dot(e_scr[h, top, t0:t1], v_ref[0, t0:t1, :]),
                               _dot(e_scr[h, bot, t0:tk], v_ref[0, t0:tk, :])], axis=0)
        if t0 > 0:
            out = out + _dot(e_scr[h, :, 0:t0], v_ref[0, 0:t0, :])
        return out * r_scr[h]

    tasks = [(i, h) for i in range(seq // qb) for h in range(2)]
    n = len(tasks)
    outs = {}
    qk(*tasks[0])
    if n > 1:
        qk(*tasks[1])
    softmax(*tasks[0])
    for k, (i, h) in enumerate(tasks):
        if k + 2 < n:
            qk(*tasks[k + 2])
        if k + 1 < n:
            softmax(*tasks[k + 1])
        outs[h] = pv(i, h)
        if h == 1:
            o_ref[0, i * qb:(i + 1) * qb, :] = jnp.where(lo, outs[0], outs[1]).astype(BF16)


def _mla_prompt(q, k, v):
    nb, seq, _ = q.shape
    qb = min(MLA_QBLOCK, seq)
    return pl.pallas_call(
        functools.partial(_mla_prompt_kernel, seq=seq),
        grid=(nb, D_HEADS // 2),
        in_specs=[pl.BlockSpec((1, seq, 2 * LANES), lambda b, p: (b, 0, p)),
                  pl.BlockSpec((1, seq, 2 * LANES), lambda b, p: (b, 0, p)),
                  pl.BlockSpec((1, seq, LANES), lambda b, p: (b, 0, p))],
        out_specs=pl.BlockSpec((1, seq, LANES), lambda b, p: (b, 0, p)),
        out_shape=jax.ShapeDtypeStruct((nb, seq, D_HEADS * D_V), BF16),
        scratch_shapes=[pltpu.VMEM((2, qb, seq), F32), pltpu.VMEM((2, qb, seq), BF16),
                        pltpu.VMEM((2, qb, LANES), F32)],
        compiler_params=_cparams("arbitrary", "arbitrary"),
        name="mla_prompt",
    )(q, k, v)


MLA_CACHE_ROWS = 512


def _mla_sample_kernel(q_ref, ckv_ref, kpe_ref, kn_ref, vn_ref, place_ref, wk_ref, kseg_ref, qabs_ref, wv_ref,
                       o_ref, ckvb_scr, kpeb_scr, sq_scr, s1_scr, s2_scr, e_scr):
    nq, past = q_ref.shape[1], ckv_ref.shape[1]
    lane_blocks = past // LANES
    lo = lax.broadcasted_iota(jnp.int32, (nq, LANES), 1) < D_V
    head = lambda h: slice(h * LANES, (h + 1) * LANES)

    for r0 in range(0, past, MLA_CACHE_ROWS):
        rows = slice(r0, r0 + MLA_CACHE_ROWS)
        c = ckv_ref[0, rows, :].astype(BF16)
        ckvb_scr[rows, :] = c
        kpeb_scr[rows, :] = _dot(kpe_ref[0, rows, :].astype(BF16), place_ref[...]).astype(BF16)
        kraw = _dot(c, wk_ref[...])
        sq_scr[rows, :] = (kraw * kraw).astype(BF16)
    r_t = lax.rsqrt(lax.dot_general(kseg_ref[...], sq_scr[...], _NT, preferred_element_type=F32) + EPS)

    q_abs = jnp.concatenate([_dot(q_ref[0, :, head(h)], qabs_ref[h]).astype(BF16) for h in range(D_HEADS)], axis=0)
    q_all = jnp.concatenate([q_ref[0, :, head(h)] for h in range(D_HEADS)], axis=0)
    s1_scr[...] = lax.dot_general(q_abs, ckvb_scr[...], _NT, preferred_element_type=F32)
    s2_scr[...] = lax.dot_general(q_all, kpeb_scr[...], _NT, preferred_element_type=F32)

    inv_den, e_new = [], []
    for h in range(D_HEADS):
        rows = slice(h * nq, (h + 1) * nq)
        s_new = lax.dot_general(q_ref[0, :, head(h)], kn_ref[0, :, head(h)], _NT, preferred_element_type=F32)
        mm = None
        for c in range(lane_blocks):
            cl = slice(c * LANES, (c + 1) * LANES)
            blk = s1_scr[rows, cl] * r_t[h:h + 1, cl] + s2_scr[rows, cl]
            s1_scr[rows, cl] = blk
            mm = blk if mm is None else jnp.maximum(mm, blk)
        m = jnp.maximum(jnp.max(mm, axis=-1, keepdims=True), jnp.max(s_new, axis=-1, keepdims=True))
        acc = None
        for c in range(lane_blocks):
            cl = slice(c * LANES, (c + 1) * LANES)
            e = jnp.exp2(s1_scr[rows, cl] - m)
            acc = e if acc is None else acc + e
            e_scr[rows, cl] = e.astype(BF16)
        en = jnp.exp2(s_new - m)
        inv_den.append(1.0 / (jnp.sum(acc, axis=-1, keepdims=True) + jnp.sum(en, axis=-1, keepdims=True)))
        e_new.append(en.astype(BF16))

    latent = _dot(e_scr[...], ckvb_scr[...])
    for p in range(D_HEADS // 2):
        halves = []
        for half in range(2):
            h = 2 * p + half
            rows = slice(h * nq, (h + 1) * nq)
            o = _dot(latent[rows].astype(BF16), wv_ref[h]) + _dot(e_new[h], vn_ref[0, :, head(p)])
            halves.append(o * inv_den[h])
        o_ref[0, :, head(p)] = jnp.where(lo, halves[0], halves[1]).astype(BF16)


def _mla_sample(q, ckv, kpe, kn, vn, ow):
    nb, nq, _ = q.shape
    past = ckv.shape[1]
    assert past % MLA_CACHE_ROWS == 0
    per_b = lambda r, w: pl.BlockSpec((1, r, w), lambda b: (b, 0, 0))
    consts = [ow["place"], ow["wk_dense"], ow["kseg_t"], ow["q_absorb"], ow["wv_lanes"]]
    rows = D_HEADS * nq
    return pl.pallas_call(
        _mla_sample_kernel,
        grid=(nb,),
        in_specs=[per_b(nq, D_HEADS * LANES), per_b(past, D_KV_LORA), per_b(past, D_ROPE),
                  per_b(nq, D_HEADS * LANES), per_b(nq, D_HEADS * D_V)] + [_resident(a.shape) for a in consts],
        out_specs=per_b(nq, D_HEADS * D_V),
        out_shape=jax.ShapeDtypeStruct((nb, nq, D_HEADS * D_V), BF16),
        scratch_shapes=[pltpu.VMEM((past, D_KV_LORA), BF16), pltpu.VMEM((past, LANES), BF16),
                        pltpu.VMEM((past, D_HEADS * D_NOPE), BF16), pltpu.VMEM((rows, past), F32),
                        pltpu.VMEM((rows, past), F32), pltpu.VMEM((rows, past), BF16)],
        compiler_params=_cparams("arbitrary"),
        name="mla_sample",
    )(q, ckv, kpe, kn, vn, *consts)


def _even_layer(xp, xs, nb, seq, ndb, dseq, norm_g, w_in, w_out, a_qn, a_kn, a_sinks, b_qn, b_kn, b_rel,
                t5_table, ck_a, cv_a, ck_b, cv_b, ffn2):
    w_in = _to_bf16(w_in)
    w_out = _to_bf16(w_out)
    w_in_p = jnp.concatenate([_pair_heads(w_in[:, :A_Q], 1), w_in[:, A_Q:]], axis=1)
    woa = _pair_heads(w_out[:A_Q], 0)
    wob = w_out[A_Q:]
    ones = lambda n: jnp.ones((n,), F32)
    scale = HEAD_DIM ** -0.5 * LOG2E
    gain_row = jnp.concatenate([jnp.tile(a_qn, A_HEADS) * scale, jnp.tile(a_kn, A_KV_HEADS), ones(A_KV),
                                jnp.tile(b_qn, B_HEADS) * scale, jnp.tile(b_kn, B_HEADS), ones(B_QKV)]
                               ).reshape(1, EVEN_IN).astype(F32)
    bias_a = _bias_a(t5_table) * LOG2E
    bias_b = _bias_b(b_rel) * LOG2E
    sink = jnp.broadcast_to(jnp.repeat(a_sinks.astype(F32) * LOG2E, PAIR_ROWS)[:, None],
                            (A_HEADS * PAIR_ROWS, LANES))
    la, lb = min(WINDOW, seq), min(B_REACH, seq)

    aq, ak, av, bq, bk, bv, cak, cav, cbk, cbv = _even_inproj(xp, norm_g, w_in_p, gain_row, nb, la, lb)
    r3 = lambda a: a.reshape(nb, seq, a.shape[-1])
    ya = _attn_a(r3(aq), r3(ak), r3(av), bias_a, sink, WINDOW, True)
    yb = _attn_b(r3(bq), r3(bk), r3(bv), bias_b, B_REACH, True)
    yap, ybp = ya.reshape(nb * seq, A_Q), yb.reshape(nb * seq, B_QKV)
    st_p = (cak.reshape(nb, la, A_KV_HEADS, HEAD_DIM), cav.reshape(nb, la, A_KV_HEADS, HEAD_DIM),
            cbk.reshape(nb, lb, B_HEADS, HEAD_DIM), cbv.reshape(nb, lb, B_HEADS, HEAD_DIM))

    ts = ndb * dseq
    aq, ak, av, bq, bk, bv, nak, nav, nbk, nbv = _even_inproj(xs, norm_g, w_in_p, gain_row, 1, ts, ts)
    pad_q = lambda a, rows: jnp.pad(a.reshape(ndb, dseq, a.shape[-1]), ((0, 0), (0, rows - dseq), (0, 0)))

    def window(cache, new, rows):
        w = cache.shape[-2] * cache.shape[-1]
        full = jnp.concatenate([cache.reshape(ndb, -1, w), new.reshape(ndb, dseq, w)], axis=1)
        buf = jnp.pad(full, ((0, 0), (0, rows - dseq), (0, 0))).astype(BF16)
        return full[:, dseq:].reshape(cache.shape), buf

    st_ak, kbuf_a = window(ck_a, nak, PAIR_ROWS)
    st_av, vbuf_a = window(cv_a, nav, PAIR_ROWS)
    st_bk, kbuf_b = window(ck_b, nbk, B_STEP_ROWS)
    st_bv, vbuf_b = window(cv_b, nbv, B_STEP_ROWS)
    ya = _attn_a(pad_q(aq, PAIR_ROWS), kbuf_a, vbuf_a, bias_a, sink, WINDOW, False)[:, :dseq]
    yb = _attn_b(pad_q(bq, B_STEP_ROWS), kbuf_b, vbuf_b, bias_b, B_REACH, False)[:, :dseq]
    xp, xs = _ffn(xp, xs, *ffn2, out_proj=(yap, ybp, ya.reshape(ts, A_Q), yb.reshape(ts, B_QKV), woa, wob))
    return xp, xs, st_p, (st_ak, st_av, st_bk, st_bv)


def _odd_layer(xp, xs, nb, seq, ndb, dseq, past, norm_g, ow, w_out, conv_prev, c_ckv, c_kpe, ffn2):
    w_out = _to_bf16(w_out)
    woc = w_out[:C_WIDTH]
    wod = w_out[C_WIDTH:]

    zero_init = jnp.zeros((nb, 8, C_WIDTH), F32)
    yc, q, k, v, ckv, kpe, cs = _odd_inproj(xp, norm_g, ow, zero_init, nb, 0)
    r3 = lambda a: a.reshape(nb, seq, a.shape[-1])
    ycp, ydp = yc, _mla_prompt(r3(q), r3(k), r3(v)).reshape(nb * seq, D_HEADS * D_V)
    st_p = (cs, ckv.reshape(nb, seq, D_KV_LORA), kpe.reshape(nb, seq, D_ROPE))

    ts = ndb * dseq
    init = jnp.pad(conv_prev.astype(F32), ((0, 0), (8 - (CONV_W - 1), 0), (0, 0)))
    yc, q, kn, vn, ckv, kpe, cs = _odd_inproj(xs, norm_g, ow, init, ndb, past)
    s3 = lambda a: a.reshape(ndb, dseq, a.shape[-1])
    yd = _mla_sample(s3(q), c_ckv, c_kpe, s3(kn), s3(vn), ow)
    xp, xs = _ffn(xp, xs, *ffn2, out_proj=(ycp, ydp, yc, yd.reshape(ts, D_HEADS * D_V), woc, wod))
    st_s = (cs, ckv.reshape(ndb, dseq, D_KV_LORA), kpe.reshape(ndb, dseq, D_ROPE))
    return xp, xs, st_p, st_s


def kernel(x_prompt, x_sample, cache_a_k, cache_a_v, cache_b_k, cache_b_v, state_c_conv, cache_d_ckv, cache_d_kpe, ff1_norm, ff1_w_gu, ff1_w_down, mix_norm, ff2_norm, ff2_w_gu, ff2_w_down, t5_bias_table, ev_w_in, ev_w_out, a_q_norm, a_k_norm, a_sinks, b_q_norm, b_k_norm, b_rel_bias, od_w_in, od_w_out, c_conv_w, d_q_a_norm, d_w_q_b, d_kv_a_norm, d_w_kv_b, d_q_nope_norm, d_q_rope_norm, d_k_nope_norm, d_k_rope_norm):
    nb, seq, _ = x_prompt.shape
    ndb, dseq, _ = x_sample.shape
    past = cache_d_ckv.shape[2]
    depth = ff1_norm.shape[0]
    assert seq % TOKEN_TILE == 0 and dseq == CHUNK and past % CHUNK == 0
    xp = x_prompt.reshape(nb * seq, D_MODEL)
    xs = x_sample.reshape(ndb * dseq, D_MODEL)
    even_p, even_s, odd_p, odd_s = [], [], [], []
    for l in range(depth):
        i = l // 2
        ffn2 = (ff2_norm[l], ff2_w_gu, ff2_w_down, l)
        xp, xs = _ffn(xp, xs, ff1_norm[l], ff1_w_gu, ff1_w_down, l)
        if l % 2 == 0:
            xp, xs, sp, ss = _even_layer(
                xp, xs, nb, seq, ndb, dseq, mix_norm[l], ev_w_in[i], ev_w_out[i], a_q_norm[i], a_k_norm[i],
                a_sinks[i], b_q_norm[i], b_k_norm[i], b_rel_bias[i], t5_bias_table,
                cache_a_k[i], cache_a_v[i], cache_b_k[i], cache_b_v[i], ffn2)
            even_p.append(sp)
            even_s.append(ss)
        else:
            ow = _odd_weights(od_w_in[i], c_conv_w[i], d_q_a_norm[i], d_w_q_b[i], d_kv_a_norm[i], d_w_kv_b[i],
                              d_q_nope_norm[i], d_q_rope_norm[i], d_k_nope_norm[i], d_k_rope_norm[i])
            xp, xs, sp, ss = _odd_layer(xp, xs, nb, seq, ndb, dseq, past, mix_norm[l], ow, od_w_out[i],
                                        state_c_conv[i], cache_d_ckv[i], cache_d_kpe[i], ffn2)
            odd_p.append(sp)
            odd_s.append(ss)
    stack = lambda group, j: group[0][j][None] if len(group) == 1 else jnp.stack([g[j] for g in group])
    return (xp.reshape(nb, seq, D_MODEL), xs.reshape(ndb, dseq, D_MODEL),
            stack(even_p, 0), stack(even_p, 1), stack(even_p, 2), stack(even_p, 3),
            stack(odd_p, 0), stack(odd_p, 1), stack(odd_p, 2),
            stack(even_s, 0), stack(even_s, 1), stack(even_s, 2), stack(even_s, 3),
            stack(odd_s, 0), stack(odd_s, 1), stack(odd_s, 2))
```

```python
import functools
import math

import jax
import jax.numpy as jnp
from jax import lax
from jax.experimental import pallas as pl
from jax.experimental.pallas import tpu as pltpu

F32 = jnp.float32
BF16 = jnp.bfloat16

D_MODEL = 1024
CHUNK = 64
HEAD_DIM = 64
EPS = 1e-6
A_HEADS = 8
A_KV_HEADS = 2
WINDOW = 128
T5_BUCKETS = 32
T5_MAX_DIST = 128
B_HEADS = 8
B_REACH = 512
B_MAX_REL = 128
C_WIDTH = 512
CONV_W = 3
D_HEADS = 8
D_Q_LORA = 256
D_KV_LORA = 128
D_NOPE = 64
D_ROPE = 32
D_V = 64
ROPE_THETA = 10000.0
FFN_DIM = 2816
A_Q = A_HEADS * HEAD_DIM
A_KV = A_KV_HEADS * HEAD_DIM
B_QKV = B_HEADS * HEAD_DIM
EVEN_IN = A_Q + 2 * A_KV + 3 * B_QKV

LANES = 128
TOKEN_TILE = 512
CAST_BLOCK_BYTES = 6 * 1024 * 1024
MXU_COLS = 256
FFN_CHUNK = MXU_COLS
PAIR_ROWS = 2 * CHUNK
MLA_QBLOCK = 512
VMEM_LIMIT_BYTES = 56 * 1024 * 1024
LOG2E = math.log2(math.e)
MLA_QSCALE = (D_NOPE + D_ROPE) ** -0.5 * LOG2E
NEG_INF = float("-inf")

_NT = (((1,), (1,)), ((), ()))


def _cparams(*sem):
    return pltpu.CompilerParams(dimension_semantics=sem, vmem_limit_bytes=VMEM_LIMIT_BYTES)


def _resident(shape):
    zeros = (0,) * len(shape)
    return pl.BlockSpec(shape, lambda *_: zeros, pipeline_mode=pl.Buffered(1))


def _rms(x, g):
    ms = jnp.mean(x * x, axis=-1, keepdims=True)
    return (x * lax.rsqrt(ms + EPS)) * g


def _dot(a, b):
    return jnp.dot(a, b, preferred_element_type=F32)


def _weight_stream(wgu_hbm, wd_hbm, wgu_ref, wd_ref, stage, sem):
    n_chunks = FFN_DIM // FFN_CHUNK
    g_stage, u_stage, d_stage = stage

    def copies(j):
        cols = pl.ds(j * FFN_CHUNK, FFN_CHUNK)
        slot = j % 2
        return (pltpu.make_async_copy(wgu_hbm.at[:, cols], g_stage.at[slot], sem.at[0, slot]),
                pltpu.make_async_copy(wgu_hbm.at[:, pl.ds(FFN_DIM + j * FFN_CHUNK, FFN_CHUNK)],
                                      u_stage.at[slot], sem.at[1, slot]),
                pltpu.make_async_copy(wd_hbm.at[cols, :], d_stage.at[slot], sem.at[2, slot]))

    def fetch(j):
        if j == 0:
            for cp in copies(0):
                cp.start()
        if j + 1 < n_chunks:
            for cp in copies(j + 1):
                cp.start()
        for cp in copies(j):
            cp.wait()
        slot = j % 2
        wgu_ref[:, j * FFN_CHUNK:(j + 1) * FFN_CHUNK] = g_stage[slot].astype(BF16)
        wgu_ref[:, FFN_DIM + j * FFN_CHUNK:FFN_DIM + (j + 1) * FFN_CHUNK] = u_stage[slot].astype(BF16)
        wd_ref[j * FFN_CHUNK:(j + 1) * FFN_CHUNK, :] = d_stage[slot].astype(BF16)

    return fetch


def _ffn_tile(x_ref, y_refs, wo_refs, g_ref, wgu_ref, wd_ref, o_ref, h_scr, a_scr, n_chunks, fetch=None):
    rows = x_ref.shape[0]
    if rows < h_scr.shape[0]:
        h_scr, a_scr = h_scr.at[pl.ds(0, rows)], a_scr.at[pl.ds(0, rows)]
    if y_refs:
        (ya_ref, yb_ref), (woa_ref, wob_ref) = y_refs, wo_refs
        y = _dot(ya_ref[...], woa_ref[...]) + _dot(yb_ref[...], wob_ref[...])
        o_ref[...] = x_ref[...] + y
        res_ref = o_ref
    else:
        res_ref = x_ref
    h_scr[...] = _rms(res_ref[...], g_ref[...]).astype(BF16)
    for j in range(n_chunks):
        if fetch is not None:
            fetch(j)
        h = h_scr[...]
        g = _dot(h, wgu_ref[:, j * FFN_CHUNK:(j + 1) * FFN_CHUNK])
        u = _dot(h, wgu_ref[:, FFN_DIM + j * FFN_CHUNK:FFN_DIM + (j + 1) * FFN_CHUNK])
        a_scr[:, j * FFN_CHUNK:(j + 1) * FFN_CHUNK] = ((g * jax.nn.sigmoid(g)) * u).astype(BF16)
    o_ref[...] = res_ref[...] + 0.5 * _dot(a_scr[...], wd_ref[...])


def _ffn_kernel(*refs, n_chunks, fused_out, layer, n_main):
    if fused_out:
        (xp_ref, yap_ref, ybp_ref, xs_ref, yas_ref, ybs_ref, woa_ref, wob_ref, g_ref, wgu_hbm, wd_hbm,
         op_ref, os_ref, h_scr, a_scr, wgu_ref, wd_ref, g_stage, u_stage, d_stage, sem) = refs
        yp, ys, wo = (yap_ref, ybp_ref), (yas_ref, ybs_ref), (woa_ref, wob_ref)
    else:
        (xp_ref, xs_ref, g_ref, wgu_hbm, wd_hbm,
         op_ref, os_ref, h_scr, a_scr, wgu_ref, wd_ref, g_stage, u_stage, d_stage, sem) = refs
        yp = ys = wo = None
    i = pl.program_id(0)
    tile = functools.partial(_ffn_tile, g_ref=g_ref, wgu_ref=wgu_ref, wd_ref=wd_ref, h_scr=h_scr, a_scr=a_scr,
                             n_chunks=n_chunks)

    @pl.when(i == 0)
    def _():
        fetch = _weight_stream(wgu_hbm.at[layer], wd_hbm.at[layer], wgu_ref, wd_ref,
                               (g_stage, u_stage, d_stage), sem)
        tile(xp_ref, yp, wo, o_ref=op_ref, fetch=fetch)

    @pl.when((i > 0) & (i < n_main))
    def _():
        tile(xp_ref, yp, wo, o_ref=op_ref)

    @pl.when(i == n_main)
    def _():
        tile(xs_ref, ys, wo, o_ref=os_ref)


def _ffn(xp, xs, norm_g, w_gu, w_down, layer, out_proj=None):
    tp, ts = xp.shape[0], xs.shape[0]
    tm = min(TOKEN_TILE, tp)
    n_main = tp // tm
    n_chunks = FFN_DIM // FFN_CHUNK
    assert n_main * tm == tp and ts <= tm and ts % 16 == 0
    main = lambda w: pl.BlockSpec((tm, w), lambda i: (jnp.minimum(i, n_main - 1), 0))
    side_in = lambda w: pl.BlockSpec((ts, w), lambda i: (0, 0), pipeline_mode=pl.Buffered(1))
    side_out = lambda w: pl.BlockSpec((ts, w), lambda i: (0, 0))
    in_hbm = pl.BlockSpec(memory_space=pl.ANY)
    if out_proj is not None:
        yap, ybp, yas, ybs, woa, wob = out_proj
        in_specs = [main(D_MODEL), main(yap.shape[1]), main(ybp.shape[1]),
                    side_in(D_MODEL), side_in(yas.shape[1]), side_in(ybs.shape[1]),
                    _resident(woa.shape), _resident(wob.shape)]
        args = [xp, yap, ybp, xs, yas, ybs, woa, wob]
    else:
        in_specs = [main(D_MODEL), side_in(D_MODEL)]
        args = [xp, xs]
    in_specs += [_resident((1, D_MODEL)), in_hbm, in_hbm]
    args += [norm_g.reshape(1, D_MODEL), w_gu, w_down]
    return pl.pallas_call(
        functools.partial(_ffn_kernel, n_chunks=n_chunks, fused_out=out_proj is not None, layer=layer,
                          n_main=n_main),
        grid=(n_main + 1,),
        in_specs=in_specs,
        out_specs=[main(D_MODEL), side_out(D_MODEL)],
        out_shape=[jax.ShapeDtypeStruct((tp, D_MODEL), F32), jax.ShapeDtypeStruct((ts, D_MODEL), F32)],
        scratch_shapes=[pltpu.VMEM((tm, D_MODEL), BF16), pltpu.VMEM((tm, FFN_DIM), BF16),
                        pltpu.VMEM((D_MODEL, 2 * FFN_DIM), BF16), pltpu.VMEM((FFN_DIM, D_MODEL), BF16),
                        pltpu.VMEM((2, D_MODEL, FFN_CHUNK), F32), pltpu.VMEM((2, D_MODEL, FFN_CHUNK), F32),
                        pltpu.VMEM((2, FFN_CHUNK, D_MODEL), F32),
                        pltpu.SemaphoreType.DMA((3, 2))],
        compiler_params=_cparams("arbitrary"),
        name="ffn_out" if out_proj is not None else "ffn",
    )(*args)


def _cast_kernel(x_ref, o_ref):
    o_ref[...] = x_ref[...].astype(BF16)


def _to_bf16(w, layer=None):
    r, c = w.shape[-2:]
    fits = [d for d in range(16, r + 1, 16) if r % d == 0 and d * c * 4 <= CAST_BLOCK_BYTES]
    tr = max(fits) if fits else r
    out_spec = pl.BlockSpec((tr, c), lambda i: (i, 0))
    in_spec = out_spec if layer is None else pl.BlockSpec((None, tr, c), lambda i: (layer, i, 0))
    return pl.pallas_call(
        _cast_kernel, grid=(r // tr,), in_specs=[in_spec], out_specs=out_spec,
        out_shape=jax.ShapeDtypeStruct((r, c), BF16),
        compiler_params=_cparams("arbitrary"), name="cast_bf16",
    )(w)


def _norm_halves(blk, gain):
    sq = blk * blk
    lo = lax.broadcasted_iota(jnp.int32, blk.shape, 1) < HEAD_DIM
    s_lo = jnp.sum(jnp.where(lo, sq, 0.0), axis=-1, keepdims=True)
    s_hi = jnp.sum(jnp.where(lo, 0.0, sq), axis=-1, keepdims=True)
    inv = jnp.where(lo, lax.rsqrt(s_lo * (1.0 / HEAD_DIM) + EPS), lax.rsqrt(s_hi * (1.0 / HEAD_DIM) + EPS))
    return (blk * inv) * gain


def _even_inproj_kernel(x_ref, g_ref, w_ref, gain_ref,
                        aq_ref, ak_ref, av_ref, bq_ref, bk_ref, bv_ref,
                        cak_ref, cav_ref, cbk_ref, cbv_ref, h_scr, z_scr, *, tm, tpb, rows_a, rows_b):
    i = pl.program_id(0)
    last = (i % tpb) == (tpb - 1)
    h_scr[...] = _rms(x_ref[...], g_ref[...]).astype(BF16)

    def lanes(c):
        return slice(c * LANES, (c + 1) * LANES)

    def emit(c):
        blk = z_scr[:, lanes(c)]
        gain = gain_ref[:, lanes(c)]
        if c < 4:
            aq_ref[:, lanes(c)] = _norm_halves(blk, gain).astype(BF16)
        elif c == 4:
            kn = _norm_halves(blk, gain)
            ak_ref[...] = kn.astype(BF16)
            z_scr[:, lanes(c)] = kn
        elif c == 5:
            av_ref[...] = blk.astype(BF16)
        elif c < 10:
            bq_ref[:, lanes(c - 6)] = _norm_halves(blk, gain).astype(BF16)
        elif c < 14:
            kn = _norm_halves(blk, gain)
            bk_ref[:, lanes(c - 10)] = kn.astype(BF16)
            z_scr[:, lanes(c)] = kn
        else:
            bv_ref[:, lanes(c - 14)] = blk.astype(BF16)

    for grp in range(EVEN_IN // MXU_COLS):
        cols = slice(grp * MXU_COLS, (grp + 1) * MXU_COLS)
        z_scr[:, cols] = _dot(h_scr[...], w_ref[:, cols])
        emit(2 * grp)
        emit(2 * grp + 1)

    @pl.when(last)
    def _():
        cak_ref[0] = z_scr[tm - rows_a:, lanes(4)]
        cav_ref[0] = z_scr[tm - rows_a:, lanes(5)]
        cbk_ref[0] = z_scr[tm - rows_b:, 10 * LANES:14 * LANES]
        cbv_ref[0] = z_scr[tm - rows_b:, 14 * LANES:18 * LANES]


def _even_inproj(x, norm_g, w_in, gain_row, n_batch, rows_a, rows_b):
    t = x.shape[0]
    tm = min(TOKEN_TILE, t)
    tpb = t // n_batch // tm
    assert tpb * tm * n_batch == t and rows_a <= tm and rows_b <= tm
    row = lambda w: pl.BlockSpec((tm, w), lambda i: (i, 0))
    cache = lambda r, w: pl.BlockSpec((1, r, w), lambda i: (i // tpb, 0, 0))
    bshape = lambda w: jax.ShapeDtypeStruct((t, w), BF16)
    cshape = lambda r, w: jax.ShapeDtypeStruct((n_batch, r, w), F32)
    return pl.pallas_call(
        functools.partial(_even_inproj_kernel, tm=tm, tpb=tpb, rows_a=rows_a, rows_b=rows_b),
        grid=(t // tm,),
        in_specs=[row(D_MODEL), _resident((1, D_MODEL)), _resident(w_in.shape), _resident(gain_row.shape)],
        out_specs=[row(A_Q), row(A_KV), row(A_KV), row(B_QKV), row(B_QKV), row(B_QKV),
                   cache(rows_a, A_KV), cache(rows_a, A_KV), cache(rows_b, B_QKV), cache(rows_b, B_QKV)],
        out_shape=[bshape(A_Q), bshape(A_KV), bshape(A_KV), bshape(B_QKV), bshape(B_QKV), bshape(B_QKV),
                   cshape(rows_a, A_KV), cshape(rows_a, A_KV), cshape(rows_b, B_QKV), cshape(rows_b, B_QKV)],
        scratch_shapes=[pltpu.VMEM((tm, D_MODEL), BF16), pltpu.VMEM((tm, EVEN_IN), F32)],
        compiler_params=_cparams("arbitrary"),
        name="even_inproj",
    )(x, norm_g.reshape(1, D_MODEL), w_in, gain_row)


def _pair_heads(a, axis):
    shape = a.shape
    split = shape[:axis] + (2, A_HEADS // 2, HEAD_DIM) + shape[axis + 1:]
    return a.reshape(split).swapaxes(axis, axis + 1).reshape(shape)


def _fill_padded(buf, src_ref, ctx):
    buf[0:ctx, :] = jnp.zeros((ctx, buf.shape[1]), buf.dtype)
    buf[ctx:, :] = src_ref[0]


def _windows(k_ref, v_ref, pad_scr, ctx, pad_front, step_rows, stream):
    win = ctx + step_rows
    if pad_front:
        kbuf, vbuf = pad_scr
        _fill_padded(kbuf, k_ref, ctx)
        _fill_padded(vbuf, v_ref, ctx)
        return (lambda r0: kbuf[pl.ds(r0, win), :]), (lambda r0: vbuf[pl.ds(r0, win), :])
    return (lambda r0: k_ref[stream, pl.ds(r0, win), :]), (lambda r0: v_ref[stream, pl.ds(r0, win), :])


def _fold_lanes(x, op):
    acc = x[:, :LANES]
    for c in range(1, x.shape[1] // LANES):
        acc = op(acc, x[:, c * LANES:(c + 1) * LANES])
    return acc


def _row_stat(reduce_fn, x):
    return jnp.broadcast_to(reduce_fn(x, axis=-1, keepdims=True), x.shape)


PAIR_SLOTS = 4


def _run_steps(make_stages, n_streams, n_steps, n_front, step_rows, lookahead):
    stages = [make_stages(stream) for stream in range(n_streams)]
    tasks = [(stream, t) for stream in range(n_streams) for t in range(n_steps)]

    def run(stage, k):
        stream, t = tasks[k]
        stages[stream][stage](t * step_rows, k % PAIR_SLOTS, t < n_front)

    n = len(tasks)
    if not lookahead:
        for k in range(n):
            for stage in range(3):
                run(stage, k)
        return
    run(0, 0)
    if n > 1:
        run(0, 1)
    run(1, 0)
    for k in range(n):
        if k + 2 < n:
            run(0, k + 2)
        if k + 1 < n:
            run(1, k + 1)
        run(2, k)


def _band_bias(ext_row, n_rows, ctx):
    win = ctx + n_rows
    period = ext_row.shape[1]
    t = pltpu.roll(jnp.broadcast_to(ext_row, (n_rows, period)), 0, 1, stride=1, stride_axis=0)[:, :win]
    chunk = lax.broadcasted_iota(jnp.int32, (n_rows, win), 0) // CHUNK
    col = lax.broadcasted_iota(jnp.int32, (n_rows, win), 1)
    visible = (col >= CHUNK * chunk) & (col < ctx + CHUNK * (chunk + 1))
    return jnp.where(visible, t, NEG_INF)


def _attn_a_kernel(q_ref, k_ref, v_ref, ext_ref, sink_ref, o_ref, bias_ref, s_scr, e_scr, *pad_scr,
                   nq, ctx, pad_front):
    win = ctx + PAIR_ROWS
    lo = lax.broadcasted_iota(jnp.int32, (PAIR_ROWS, LANES), 1) < HEAD_DIM

    @pl.when(pl.program_id(0) == 0)
    def _():
        for head in range(A_HEADS):
            bias_ref[head * PAIR_ROWS:(head + 1) * PAIR_ROWS, :] = _band_bias(
                ext_ref[head:head + 1, :], PAIR_ROWS, ctx)

    def make_stages(stream):
        kwin, vwin = _windows(k_ref, v_ref, pad_scr, ctx, pad_front, PAIR_ROWS, stream)
        return (functools.partial(scores, stream, kwin), functools.partial(softmax, stream),
                functools.partial(values, stream, vwin))

    def scores(stream, kwin, r0, slot, masked):
        q_lo, q_hi = [], []
        for p in range(4):
            qp = q_ref[stream, pl.ds(r0, PAIR_ROWS), p * LANES:(p + 1) * LANES]
            zero = jnp.zeros_like(qp)
            q_lo.append(jnp.where(lo, qp, zero))
            q_hi.append(jnp.where(lo, zero, qp))
        q2 = jnp.concatenate(q_lo + q_hi, axis=0)
        s_scr[slot] = lax.dot_general(q2, kwin(r0), _NT, preferred_element_type=F32)

    def softmax(stream, r0, slot, masked):
        for head in range(A_HEADS):
            rows = slice(head * PAIR_ROWS, (head + 1) * PAIR_ROWS)
            s = s_scr[slot, rows, :] + bias_ref[rows, :]
            if masked:
                col = lax.broadcasted_iota(jnp.int32, (PAIR_ROWS, win), 1)
                s = jnp.where(col + r0 >= ctx, s, NEG_INF)
            sk = sink_ref[rows, :]
            m = jnp.maximum(_row_stat(jnp.max, _fold_lanes(s, jnp.maximum)), sk)
            e = [jnp.exp2(s[:, c * LANES:(c + 1) * LANES] - m) for c in range(win // LANES)]
            den = _row_stat(jnp.sum, functools.reduce(jnp.add, e)) + jnp.exp2(sk - m)
            inv = 1.0 / den
            for c in range(win // LANES):
                e_scr[slot, rows, c * LANES:(c + 1) * LANES] = (e[c] * inv).astype(BF16)

    def values(stream, vwin, r0, slot, masked):
        o2 = _dot(e_scr[slot], vwin(r0))
        half = 4 * PAIR_ROWS
        for p in range(4):
            o = jnp.where(lo, o2[p * PAIR_ROWS:(p + 1) * PAIR_ROWS],
                          o2[half + p * PAIR_ROWS:half + (p + 1) * PAIR_ROWS])
            o_ref[stream, pl.ds(r0, PAIR_ROWS), p * LANES:(p + 1) * LANES] = o.astype(BF16)

    _run_steps(make_stages, q_ref.shape[0], nq // PAIR_ROWS, ctx // PAIR_ROWS if pad_front else 0, PAIR_ROWS,
               lookahead=False)


def _streams_per_block(nb, pad_front):
    return 1 if pad_front else nb


def _attn_a(q, k, v, ext, sink, ctx, pad_front):
    nb, nq, _ = q.shape
    nk = k.shape[1]
    assert nq % PAIR_ROWS == 0 and nk == (nq if pad_front else ctx + nq)
    bb = _streams_per_block(nb, pad_front)
    per_b = lambda r, w: pl.BlockSpec((bb, r, w), lambda b: (b, 0, 0))
    rows, win = A_HEADS * PAIR_ROWS, ctx + PAIR_ROWS
    scratch = [pltpu.VMEM((rows, win), F32),
               pltpu.VMEM((PAIR_SLOTS, rows, win), F32), pltpu.VMEM((PAIR_SLOTS, rows, win), BF16)]
    if pad_front:
        scratch += [pltpu.VMEM((ctx + nq, A_KV), BF16)] * 2
    return pl.pallas_call(
        functools.partial(_attn_a_kernel, nq=nq, ctx=ctx, pad_front=pad_front),
        grid=(nb // bb,),
        in_specs=[per_b(nq, A_Q), per_b(nk, A_KV), per_b(nk, A_KV), _resident(ext.shape), _resident(sink.shape)],
        out_specs=per_b(nq, A_Q),
        out_shape=jax.ShapeDtypeStruct((nb, nq, A_Q), BF16),
        scratch_shapes=scratch,
        compiler_params=_cparams("arbitrary"),
        name="attn_a",
    )(q, k, v, ext, sink)


B_STEP_ROWS = 4 * CHUNK


def _attn_b_kernel(q_ref, k_ref, v_ref, ext_ref, o_ref, bias_scr, s_scr, e_scr, r_scr, *pad_scr,
                   nq, ctx, pad_front):
    win = ctx + B_STEP_ROWS
    lo = lax.broadcasted_iota(jnp.int32, (B_STEP_ROWS, LANES), 1) < HEAD_DIM
    chunks = B_STEP_ROWS // CHUNK
    pair_idx = pl.program_id(1)

    @pl.when(pl.program_id(0) == 0)
    def _():
        for half in range(2):
            bias_scr[pair_idx, half * B_STEP_ROWS:(half + 1) * B_STEP_ROWS, :] = _band_bias(
                ext_ref[0, half:half + 1, :], B_STEP_ROWS, ctx)

    bias_ref = bias_scr.at[pair_idx]

    def make_stages(stream):
        kwin, vwin = _windows(k_ref, v_ref, pad_scr, ctx, pad_front, B_STEP_ROWS, stream)
        return (functools.partial(scores, stream, kwin), functools.partial(softmax, stream),
                functools.partial(values, stream, vwin))

    def scores(stream, kwin, r0, slot, masked):
        qp = q_ref[stream, pl.ds(r0, B_STEP_ROWS), :]
        zero = jnp.zeros_like(qp)
        q2 = jnp.concatenate([jnp.where(lo, qp, zero), jnp.where(lo, zero, qp)], axis=0)
        s_scr[slot] = lax.dot_general(q2, kwin(r0), _NT, preferred_element_type=F32)

    def softmax(stream, r0, slot, masked):
        for rb in range(2 * chunks):
            rows = slice(rb * CHUNK, (rb + 1) * CHUNK)
            ci = rb % chunks
            c_lo, c_hi = CHUNK * ci // LANES, -(-(ctx + CHUNK * (ci + 1)) // LANES)
            band = slice(c_lo * LANES, c_hi * LANES)
            s = s_scr[slot, rows, band] + bias_ref[rows, band]
            if masked:
                col = lax.broadcasted_iota(jnp.int32, s.shape, 1) + c_lo * LANES
                s = jnp.where(col + r0 >= ctx, s, NEG_INF)
            m = _row_stat(jnp.max, _fold_lanes(s, jnp.maximum))
            e = [jnp.exp2(s[:, c * LANES:(c + 1) * LANES] - m) for c in range(c_hi - c_lo)]
            for c in range(win // LANES):
                blk = e[c - c_lo].astype(BF16) if c_lo <= c < c_hi else jnp.zeros((CHUNK, LANES), BF16)
                e_scr[slot, rows, c * LANES:(c + 1) * LANES] = blk
            r_scr[slot, rows, :] = 1.0 / _row_stat(jnp.sum, functools.reduce(jnp.add, e))

    def values(stream, vwin, r0, slot, masked):
        o2 = _dot(e_scr[slot], vwin(r0)) * r_scr[slot]
        o = jnp.where(lo, o2[:B_STEP_ROWS], o2[B_STEP_ROWS:])
        o_ref[stream, pl.ds(r0, B_STEP_ROWS), :] = o.astype(BF16)

    _run_steps(make_stages, q_ref.shape[0], nq // B_STEP_ROWS,
               -(-ctx // B_STEP_ROWS) if pad_front else 0, B_STEP_ROWS, lookahead=True)


def _attn_b(q, k, v, ext, ctx, pad_front):
    nb, nq, _ = q.shape
    nk = k.shape[1]
    assert nq % B_STEP_ROWS == 0 and nk == (nq if pad_front else ctx + nq)
    bb = _streams_per_block(nb, pad_front)
    blk = lambda r: pl.BlockSpec((bb, r, LANES), lambda b, p: (b, 0, p))
    rows, win = 2 * B_STEP_ROWS, ctx + B_STEP_ROWS
    scratch = [pltpu.VMEM((B_HEADS // 2, rows, win), F32),
               pltpu.VMEM((PAIR_SLOTS, rows, win), F32), pltpu.VMEM((PAIR_SLOTS, rows, win), BF16),
               pltpu.VMEM((PAIR_SLOTS, rows, LANES), F32)]
    if pad_front:
        scratch += [pltpu.VMEM((ctx + nq, LANES), BF16)] * 2
    return pl.pallas_call(
        functools.partial(_attn_b_kernel, nq=nq, ctx=ctx, pad_front=pad_front),
        grid=(nb // bb, B_HEADS // 2),
        in_specs=[blk(nq), blk(nk), blk(nk),
                  pl.BlockSpec((1,) + ext.shape[1:], lambda b, p: (p, 0, 0))],
        out_specs=blk(nq),
        out_shape=jax.ShapeDtypeStruct((nb, nq, B_QKV), BF16),
        scratch_shapes=scratch,
        compiler_params=_cparams("arbitrary", "arbitrary"),
        name="attn_b",
    )(q, k, v, ext)


def _t5_bucket(rel):
    nb = T5_BUCKETS // 2
    max_exact = nb // 2
    n = -rel
    ret = jnp.where(n < 0, nb, 0)
    n = jnp.abs(n)
    nf = jnp.maximum(n, 1).astype(F32)
    large = max_exact + (jnp.log(nf / max_exact) / math.log(T5_MAX_DIST / max_exact)
                         * (nb - max_exact)).astype(jnp.int32)
    large = jnp.minimum(large, nb - 1)
    return ret + jnp.where(n < max_exact, n, large)


def _band_values(ctx, n_rows, value_of_rel):
    n_cols = ctx + n_rows
    period = -(-(n_cols + n_rows - 1) // LANES) * LANES
    d = jnp.concatenate([jnp.arange(0, period - (n_rows - 1)), jnp.arange(-(n_rows - 1), 0)])
    return value_of_rel(d - ctx)


def _bias_a(t5_table):
    return _band_values(WINDOW, PAIR_ROWS, lambda rel: t5_table.astype(F32)[_t5_bucket(rel)].T)


def _bias_b(b_rel):
    ext = _band_values(B_REACH, B_STEP_ROWS,
                       lambda rel: b_rel.astype(F32)[:, jnp.clip(rel, -B_MAX_REL, B_MAX_REL) + B_MAX_REL])
    return ext.reshape(B_HEADS // 2, 2, ext.shape[-1])


ODD_IN_PAD = 2048
ROPE_HALF = D_ROPE // 2
X1_LO = LANES // 2
NOPE_SPLIT = X1_LO - ROPE_HALF


def _head_lane_source():
    zero = D_NOPE + D_ROPE
    src = []
    for lane in range(LANES):
        if lane < ROPE_HALF:
            src.append(D_NOPE + ROPE_HALF + lane)
        elif lane < X1_LO:
            src.append(lane - ROPE_HALF)
        elif lane < X1_LO + ROPE_HALF:
            src.append(D_NOPE + lane - X1_LO)
        elif lane < X1_LO + ROPE_HALF + D_NOPE - NOPE_SPLIT:
            src.append(NOPE_SPLIT + lane - X1_LO - ROPE_HALF)
        else:
            src.append(zero)
    return src


def _to_head_lanes(a):
    padded = jnp.concatenate([a, jnp.zeros(a.shape[:-1] + (1,), a.dtype)], axis=-1)
    return padded[..., jnp.asarray(_head_lane_source(), dtype=jnp.int32)]


def _lane_masks(shape):
    lane = lax.broadcasted_iota(jnp.int32, shape, 1)
    rope = (lane < ROPE_HALF) | ((lane >= X1_LO) & (lane < X1_LO + ROPE_HALF))
    used = lane < X1_LO + ROPE_HALF + D_NOPE - NOPE_SPLIT
    return used & ~rope, rope


def _rope(y, cosf, sinf):
    return y * cosf + pltpu.roll(y, LANES // 2, 1) * sinf


ROW_BLOCK = 128


def _row_blocks(tm):
    rb = min(ROW_BLOCK, tm)
    return [slice(r, r + rb) for r in range(0, tm, rb)]


def _segment_mean_squares(src_scr, cols, seg_ref, sq_scr, ms_scr, slot):
    for rows in _row_blocks(src_scr.shape[0]):
        v = src_scr[rows, cols]
        sq_scr[slot, rows, :] = (v * v).astype(BF16)
    ms_scr[slot] = _dot(sq_scr[slot], seg_ref[...])


def _expand_kv(ckvb_ref, kpe_blk_ref, wkv_ref, kgain_ref, kseg_ref, kv_scr, sq_scr, ms_scr, k_ref, v_ref,
               between=None):
    k_cols = D_HEADS * LANES
    tm = kv_scr.shape[0]
    for grp in range(kv_scr.shape[1] // MXU_COLS):
        if between and grp in between:
            between[grp]()
        cols = slice(grp * MXU_COLS, (grp + 1) * MXU_COLS)
        kv_scr[:, cols] = _dot(ckvb_ref[...], wkv_ref[:, cols])
        if cols.start < k_cols:
            slot = grp % 2
            _segment_mean_squares(kv_scr, cols, kseg_ref, sq_scr, ms_scr, slot)
        for rows in _row_blocks(tm):
            if cols.start < k_cols:
                for half in range(2):
                    h = 2 * grp + half
                    kb = kv_scr[rows, h * LANES:(h + 1) * LANES]
                    ms = ms_scr[slot, rows, half * LANES:(half + 1) * LANES]
                    kn = (kb * lax.rsqrt(ms + EPS)) * kgain_ref[...]
                    k_ref[rows, h * LANES:(h + 1) * LANES] = (kn + kpe_blk_ref[rows, :]).astype(BF16)
            else:
                v_ref[rows, cols.start - k_cols:cols.stop - k_cols] = kv_scr[rows, cols].astype(BF16)


def _odd_inproj_kernel(x_ref, g_ref, w_ref, cinit_ref, convw_ref, qan_ref, wqb_ref, qgain_ref,
                       kvan_ref, krgain_ref, invf_ref, sgn_ref, wkv_ref, kgain_ref, qseg_ref, kseg_ref,
                       yc_ref, q_ref, k_ref, v_ref, ckv_ref, kpe_ref, cs_ref,
                       h_scr, z_scr, uext_scr, qn_scr, q_scr, kv_scr, rot_scr, ckvb_scr, kpe_scr, sq_scr, ms_scr,
                       *, tm, tpb, pos0):
    i = pl.program_id(0)
    tile = i % tpb
    off = pl.multiple_of(tile * tm, tm)
    blocks = _row_blocks(tm)
    rb = blocks[0].stop

    @pl.when(tile == 0)
    def _():
        uext_scr[0:8, :] = cinit_ref[0]

    @pl.when(i < tpb)
    def _():
        row = lax.broadcasted_iota(jnp.int32, (tm, LANES), 0)
        ang = (row + (tile * tm + pos0)).astype(F32) * invf_ref[...]
        rot_scr[0, pl.ds(off, tm), :] = jnp.cos(ang)
        rot_scr[1, pl.ds(off, tm), :] = jnp.sin(ang) * sgn_ref[...]

    h_scr[...] = _rms(x_ref[...], g_ref[...]).astype(BF16)
    low_lanes = lax.broadcasted_iota(jnp.int32, (rb, LANES), 1) < ROPE_HALF

    def proj(c0, c1):
        z_scr[:, c0:c1] = _dot(h_scr[...], w_ref[:, c0:c1])

    def rotary(y, rows):
        cosf = rot_scr[0, pl.ds(off + rows.start, rb), :]
        sinf = rot_scr[1, pl.ds(off + rows.start, rb), :]
        return _rope(y, cosf, sinf)

    def conv_group(grp):
        c0 = grp * MXU_COLS
        cols = slice(c0, c0 + MXU_COLS)
        for base in (0, C_WIDTH, 2 * C_WIDTH):
            proj(base + c0, base + c0 + MXU_COLS)
        for rows in blocks:
            up = slice(rows.start + 8, rows.stop + 8)
            uext_scr[up, cols] = (z_scr[rows, C_WIDTH + c0:C_WIDTH + c0 + MXU_COLS]
                                  * z_scr[rows, 2 * C_WIDTH + c0:2 * C_WIDTH + c0 + MXU_COLS])
        for rows in blocks:
            yconv = convw_ref[0:1, cols] * uext_scr[rows.start + 6:rows.stop + 6, cols]
            yconv = yconv + convw_ref[1:2, cols] * uext_scr[rows.start + 7:rows.stop + 7, cols]
            yconv = yconv + convw_ref[2:3, cols] * uext_scr[rows.start + 8:rows.stop + 8, cols]
            yc_ref[rows, cols] = (z_scr[rows, cols] * yconv).astype(BF16)

    proj(1536, 1792)
    proj(1792, 2048)

    for rows in blocks:
        qn_scr[rows, :] = _rms(z_scr[rows, 1536:1792], qan_ref[...]).astype(BF16)
        ckv = _rms(z_scr[rows, 1792:1920], kvan_ref[...])
        ckv_ref[rows, :] = ckv
        ckvb_scr[rows, :] = ckv.astype(BF16)

    for grp in range(D_HEADS * LANES // MXU_COLS):
        cols = slice(grp * MXU_COLS, (grp + 1) * MXU_COLS)
        q_scr[:, cols] = _dot(qn_scr[...], wqb_ref[:, cols])
        slot = grp % 2
        _segment_mean_squares(q_scr, cols, qseg_ref, sq_scr, ms_scr, slot)
        for half in range(2):
            h = 2 * grp + half
            for rows in blocks:
                blk = q_scr[rows, h * LANES:(h + 1) * LANES]
                inv = lax.rsqrt(ms_scr[slot, rows, half * LANES:(half + 1) * LANES] + EPS)
                y = rotary((blk * inv) * qgain_ref[...], rows)
                q_ref[rows, h * LANES:(h + 1) * LANES] = y.astype(BF16)

    for rows in blocks:
        kb = z_scr[rows, 1920:2048]
        ms = jnp.sum(kb * kb, axis=-1, keepdims=True) * (1.0 / D_ROPE)
        kpe_blk = rotary((kb * lax.rsqrt(ms + EPS)) * krgain_ref[...], rows)
        kpe_scr[rows, :] = kpe_blk
        x1_then_x2 = jnp.where(low_lanes, pltpu.roll(kpe_blk, LANES - X1_LO, 1), pltpu.roll(kpe_blk, ROPE_HALF, 1))
        kpe_ref[rows, :] = x1_then_x2[:, 0:D_ROPE]
    _expand_kv(ckvb_scr, kpe_scr, wkv_ref, kgain_ref, kseg_ref, kv_scr, sq_scr, ms_scr, k_ref, v_ref,
               between={0: lambda: conv_group(0), 4: lambda: conv_group(1)})

    cs_ref[0] = uext_scr[tm + 6:tm + 8, :]
    uext_scr[0:8, :] = uext_scr[tm:tm + 8, :]


def _odd_inproj(x, norm_g, ow, conv_init, n_batch, pos0):
    t = x.shape[0]
    tm = min(TOKEN_TILE, t // n_batch)
    tpb = t // n_batch // tm
    assert tpb * tm * n_batch == t
    row = lambda w: pl.BlockSpec((tm, w), lambda i: (i, 0))
    per_b = lambda r, w: pl.BlockSpec((1, r, w), lambda i: (i // tpb, 0, 0))
    small = [ow["conv_w"], ow["qan"], ow["wqb"], ow["qgain"], ow["kvan"], ow["krgain"],
             ow["invf"], ow["sgn"], ow["wkv"], ow["kgain"], ow["qseg"], ow["kseg"]]
    kv_w = D_HEADS * LANES
    v_w = D_HEADS * D_V
    return pl.pallas_call(
        functools.partial(_odd_inproj_kernel, tm=tm, tpb=tpb, pos0=pos0),
        grid=(t // tm,),
        in_specs=[row(D_MODEL), _resident((1, D_MODEL)), _resident(ow["w_in"].shape), per_b(8, C_WIDTH)]
                 + [_resident(a.shape) for a in small],
        out_specs=[row(C_WIDTH), row(kv_w), row(kv_w), row(v_w), row(D_KV_LORA), row(D_ROPE),
                   per_b(CONV_W - 1, C_WIDTH)],
        out_shape=[jax.ShapeDtypeStruct((t, C_WIDTH), BF16), jax.ShapeDtypeStruct((t, kv_w), BF16),
                   jax.ShapeDtypeStruct((t, kv_w), BF16), jax.ShapeDtypeStruct((t, v_w), BF16),
                   jax.ShapeDtypeStruct((t, D_KV_LORA), F32), jax.ShapeDtypeStruct((t, D_ROPE), F32),
                   jax.ShapeDtypeStruct((n_batch, CONV_W - 1, C_WIDTH), F32)],
        scratch_shapes=[pltpu.VMEM((tm, D_MODEL), BF16), pltpu.VMEM((tm, ODD_IN_PAD), F32),
                        pltpu.VMEM((tm + 8, C_WIDTH), F32), pltpu.VMEM((tm, D_Q_LORA), BF16),
                        pltpu.VMEM((tm, kv_w), F32), pltpu.VMEM((tm, kv_w + v_w), F32),
                        pltpu.VMEM((2, tpb * tm, LANES), F32), pltpu.VMEM((tm, D_KV_LORA), BF16),
                        pltpu.VMEM((tm, LANES), F32), pltpu.VMEM((2, tm, MXU_COLS), BF16),
                        pltpu.VMEM((2, tm, MXU_COLS), F32)],
        compiler_params=_cparams("arbitrary"),
        name="odd_inproj",
    )(x, norm_g.reshape(1, D_MODEL), ow["w_in"], conv_init, *small)


def _odd_weights(w_in, conv_w, q_a_norm, w_q_b, kv_a_norm, w_kv_b, qn_nope, qn_rope, kn_nope, kn_rope):
    z = lambda n: jnp.zeros((n,), F32)
    qk = D_NOPE + D_ROPE
    w_in = _to_bf16(w_in)
    kr_blk = _to_head_lanes(jnp.concatenate([jnp.zeros((D_MODEL, D_NOPE), BF16), w_in[:, 1920:]], axis=1))
    w_pad = jnp.concatenate([w_in[:, :1920], kr_blk], axis=1)
    wqb = _to_head_lanes(w_q_b.astype(BF16).reshape(D_Q_LORA, D_HEADS, qk)).reshape(D_Q_LORA, D_HEADS * LANES)
    kvb = w_kv_b.astype(BF16).reshape(D_KV_LORA, D_HEADS, D_NOPE + D_V)
    wk = _to_head_lanes(jnp.concatenate([kvb[:, :, :D_NOPE], jnp.zeros((D_KV_LORA, D_HEADS, D_ROPE), BF16)], axis=-1))
    wkv = jnp.concatenate([wk.reshape(D_KV_LORA, D_HEADS * LANES),
                           kvb[:, :, D_NOPE:].reshape(D_KV_LORA, D_HEADS * D_V)], axis=1)
    inv = 1.0 / (ROPE_THETA ** (jnp.arange(ROPE_HALF, dtype=F32) / ROPE_HALF))
    ones = jnp.ones((ROPE_HALF,), F32)
    lanes = lambda v: _to_head_lanes(v).reshape(1, LANES)
    in_nope = lanes(jnp.concatenate([jnp.ones((D_NOPE,), F32), z(D_ROPE)]))
    in_rope = lanes(jnp.concatenate([z(D_NOPE), jnp.ones((D_ROPE,), F32)]))
    seg_nope = in_nope.T * in_nope * (1.0 / D_NOPE)
    seg_rope = in_rope.T * in_rope * (1.0 / D_ROPE)
    two_heads = lambda m: jnp.kron(jnp.eye(MXU_COLS // LANES, dtype=F32), m).astype(BF16)
    return {
        "qseg": two_heads(seg_nope + seg_rope),
        "kseg": two_heads(seg_nope),
        "wk_dense": kvb[:, :, :D_NOPE].reshape(D_KV_LORA, D_HEADS * D_NOPE),
        "kseg_t": jnp.pad(jnp.kron(jnp.eye(D_HEADS, dtype=F32), jnp.full((1, D_NOPE), 1.0 / D_NOPE, F32)),
                          ((0, 16 - D_HEADS), (0, 0))).astype(BF16),
        "q_absorb": _to_head_lanes(jnp.concatenate(
            [w_kv_b.reshape(D_KV_LORA, D_HEADS, D_NOPE + D_V)[:, :, :D_NOPE] * kn_nope,
             jnp.zeros((D_KV_LORA, D_HEADS, D_ROPE), F32)], axis=-1)).transpose(1, 2, 0).astype(BF16),
        "wv_lanes": jnp.stack([jnp.pad(kvb[:, h, D_NOPE:], ((0, 0), ((h % 2) * D_V, LANES - D_V - (h % 2) * D_V)))
                               for h in range(D_HEADS)]),
        "w_in": w_pad,
        "conv_w": conv_w.astype(F32),
        "qan": q_a_norm.reshape(1, D_Q_LORA),
        "wqb": wqb,
        "qgain": lanes(jnp.concatenate([qn_nope, qn_rope])) * MLA_QSCALE,
        "kvan": kv_a_norm.reshape(1, D_KV_LORA),
        "krgain": lanes(jnp.concatenate([z(D_NOPE), kn_rope])),
        "invf": lanes(jnp.concatenate([z(D_NOPE), inv, inv])),
        "sgn": lanes(jnp.concatenate([z(D_NOPE), -ones, ones])),
        "wkv": wkv,
        "kgain": lanes(jnp.concatenate([kn_nope, z(D_ROPE)])),
        "place": _to_head_lanes(jnp.concatenate([jnp.zeros((D_ROPE, D_NOPE), BF16), jnp.eye(D_ROPE, dtype=BF16)], axis=1)),
    }


def _mla_prompt_kernel(q_ref, k_ref, v_ref, o_ref, s_scr, e_scr, r_scr, *, seq):
    qb = min(MLA_QBLOCK, seq)
    hb = qb // 2
    lo = lax.broadcasted_iota(jnp.int32, (qb, LANES), 1) < D_V
    first_half = lax.broadcasted_iota(jnp.int32, (CHUNK, LANES), 1) < CHUNK
    scores = lambda q, k: lax.dot_general(q, k, _NT, preferred_element_type=F32)
    top, bot = slice(0, hb), slice(hb, qb)

    def spans(i):
        tk = qb * (i + 1)
        return tk - qb, tk - hb, tk

    def qk(i, h):
        t0, t1, tk = spans(i)
        hl = slice(h * LANES, (h + 1) * LANES)
        q0 = i * qb
        if t0 > 0:
            s_scr[h, :, 0:t0] = scores(q_ref[0, q0:q0 + qb, hl], k_ref[0, 0:t0, hl])
        s_scr[h, top, t0:t1] = scores(q_ref[0, q0:q0 + hb, hl], k_ref[0, t0:t1, hl])
        s_scr[h, bot, t0:tk] = scores(q_ref[0, q0 + hb:q0 + qb, hl], k_ref[0, t0:tk, hl])

    def softmax(i, h):
        t0, t1, tk = spans(i)
        for rb in range(qb // CHUNK):
            rows = slice(rb * CHUNK, (rb + 1) * CHUNK)
            visible = t0 + CHUNK * (rb + 1)
            width = t1 if rb < hb // CHUNK else tk
            n_full, ragged = visible // LANES, visible % LANES != 0
            blk = lambda c: s_scr[h, rows, c * LANES:(c + 1) * LANES]
            cols = [blk(c) for c in range(n_full)]
            if ragged:
                cols.append(jnp.where(first_half, blk(n_full), NEG_INF))
            mm = cols[0]
            for c in cols[1:]:
                mm = jnp.maximum(mm, c)
            m = _row_stat(jnp.max, mm)
            acc = None
            for c in range(width // LANES):
                if c < len(cols):
                    e = jnp.exp2(blk(c) - m)
                    if c >= n_full:
                        e = jnp.where(first_half, e, 0.0)
                    acc = e if acc is None else acc + e
                else:
                    e = jnp.zeros((CHUNK, LANES), F32)
                e_scr[h, rows, c * LANES:(c + 1) * LANES] = e.astype(BF16)
            r_scr[h, rows, :] = 1.0 / _row_stat(jnp.sum, acc)

    def pv(i, h):
        t0, t1, tk = spans(i)
        out = jnp.concatenate([_dot(e_scr[h, top, t0:t1], v_ref[0, t0:t1, :]),
                               _dot(e_scr[h, bot, t0:tk], v_ref[0, t0:tk, :])], axis=0)
        if t0 > 0:
            out = out + _dot(e_scr[h, :, 0:t0], v_ref[0, 0:t0, :])
        return out * r_scr[h]

    tasks = [(i, h) for i in range(seq // qb) for h in range(2)]
    n = len(tasks)
    outs = {}
    qk(*tasks[0])
    if n > 1:
        qk(*tasks[1])
    softmax(*tasks[0])
    for k, (i, h) in enumerate(tasks):
        if k + 2 < n:
            qk(*tasks[k + 2])
        if k + 1 < n:
            softmax(*tasks[k + 1])
        outs[h] = pv(i, h)
        if h == 1:
            o_ref[0, i * qb:(i + 1) * qb, :] = jnp.where(lo, outs[0], outs[1]).astype(BF16)


def _mla_prompt(q, k, v):
    nb, seq, _ = q.shape
    qb = min(MLA_QBLOCK, seq)
    return pl.pallas_call(
        functools.partial(_mla_prompt_kernel, seq=seq),
        grid=(nb, D_HEADS // 2),
        in_specs=[pl.BlockSpec((1, seq, 2 * LANES), lambda b, p: (b, 0, p)),
                  pl.BlockSpec((1, seq, 2 * LANES), lambda b, p: (b, 0, p)),
                  pl.BlockSpec((1, seq, LANES), lambda b, p: (b, 0, p))],
        out_specs=pl.BlockSpec((1, seq, LANES), lambda b, p: (b, 0, p)),
        out_shape=jax.ShapeDtypeStruct((nb, seq, D_HEADS * D_V), BF16),
        scratch_shapes=[pltpu.VMEM((2, qb, seq), F32), pltpu.VMEM((2, qb, seq), BF16),
                        pltpu.VMEM((2, qb, LANES), F32)],
        compiler_params=_cparams("arbitrary", "arbitrary"),
        name="mla_prompt",
    )(q, k, v)


MLA_CACHE_ROWS = 512


def _mla_sample_kernel(q_ref, ckv_ref, kpe_ref, kn_ref, vn_ref, place_ref, wk_ref, kseg_ref, qabs_ref, wv_ref,
                       o_ref, ckvb_scr, kpeb_scr, sq_scr, s1_scr, s2_scr, e_scr):
    nq, past = q_ref.shape[1], ckv_ref.shape[1]
    lane_blocks = past // LANES
    lo = lax.broadcasted_iota(jnp.int32, (nq, LANES), 1) < D_V
    head = lambda h: slice(h * LANES, (h + 1) * LANES)

    for r0 in range(0, past, MLA_CACHE_ROWS):
        rows = slice(r0, r0 + MLA_CACHE_ROWS)
        c = ckv_ref[0, rows, :].astype(BF16)
        ckvb_scr[rows, :] = c
        kpeb_scr[rows, :] = _dot(kpe_ref[0, rows, :].astype(BF16), place_ref[...]).astype(BF16)
        kraw = _dot(c, wk_ref[...])
        sq_scr[rows, :] = (kraw * kraw).astype(BF16)
    r_t = lax.rsqrt(lax.dot_general(kseg_ref[...], sq_scr[...], _NT, preferred_element_type=F32) + EPS)

    q_abs = jnp.concatenate([_dot(q_ref[0, :, head(h)], qabs_ref[h]).astype(BF16) for h in range(D_HEADS)], axis=0)
    q_all = jnp.concatenate([q_ref[0, :, head(h)] for h in range(D_HEADS)], axis=0)
    s1_scr[...] = lax.dot_general(q_abs, ckvb_scr[...], _NT, preferred_element_type=F32)
    s2_scr[...] = lax.dot_general(q_all, kpeb_scr[...], _NT, preferred_element_type=F32)

    inv_den, e_new = [], []
    for h in range(D_HEADS):
        rows = slice(h * nq, (h + 1) * nq)
        s_new = lax.dot_general(q_ref[0, :, head(h)], kn_ref[0, :, head(h)], _NT, preferred_element_type=F32)
        mm = None
        for c in range(lane_blocks):
            cl = slice(c * LANES, (c + 1) * LANES)
            blk = s1_scr[rows, cl] * r_t[h:h + 1, cl] + s2_scr[rows, cl]
            s1_scr[rows, cl] = blk
            mm = blk if mm is None else jnp.maximum(mm, blk)
        m = jnp.maximum(jnp.max(mm, axis=-1, keepdims=True), jnp.max(s_new, axis=-1, keepdims=True))
        acc = None
        for c in range(lane_blocks):
            cl = slice(c * LANES, (c + 1) * LANES)
            e = jnp.exp2(s1_scr[rows, cl] - m)
            acc = e if acc is None else acc + e
            e_scr[rows, cl] = e.astype(BF16)
        en = jnp.exp2(s_new - m)
        inv_den.append(1.0 / (jnp.sum(acc, axis=-1, keepdims=True) + jnp.sum(en, axis=-1, keepdims=True)))
        e_new.append(en.astype(BF16))

    latent = _dot(e_scr[...], ckvb_scr[...])
    for p in range(D_HEADS // 2):
        halves = []
        for half in range(2):
            h = 2 * p + half
            rows = slice(h * nq, (h + 1) * nq)
            o = _dot(latent[rows].astype(BF16), wv_ref[h]) + _dot(e_new[h], vn_ref[0, :, head(p)])
            halves.append(o * inv_den[h])
        o_ref[0, :, head(p)] = jnp.where(lo, halves[0], halves[1]).astype(BF16)


def _mla_sample(q, ckv, kpe, kn, vn, ow):
    nb, nq, _ = q.shape
    past = ckv.shape[1]
    assert past % MLA_CACHE_ROWS == 0
    per_b = lambda r, w: pl.BlockSpec((1, r, w), lambda b: (b, 0, 0))
    consts = [ow["place"], ow["wk_dense"], ow["kseg_t"], ow["q_absorb"], ow["wv_lanes"]]
    rows = D_HEADS * nq
    return pl.pallas_call(
        _mla_sample_kernel,
        grid=(nb,),
        in_specs=[per_b(nq, D_HEADS * LANES), per_b(past, D_KV_LORA), per_b(past, D_ROPE),
                  per_b(nq, D_HEADS * LANES), per_b(nq, D_HEADS * D_V)] + [_resident(a.shape) for a in consts],
        out_specs=per_b(nq, D_HEADS * D_V),
        out_shape=jax.ShapeDtypeStruct((nb, nq, D_HEADS * D_V), BF16),
        scratch_shapes=[pltpu.VMEM((past, D_KV_LORA), BF16), pltpu.VMEM((past, LANES), BF16),
                        pltpu.VMEM((past, D_HEADS * D_NOPE), BF16), pltpu.VMEM((rows, past), F32),
                        pltpu.VMEM((rows, past), F32), pltpu.VMEM((rows, past), BF16)],
        compiler_params=_cparams("arbitrary"),
        name="mla_sample",
    )(q, ckv, kpe, kn, vn, *consts)


def _even_layer(xp, xs, nb, seq, ndb, dseq, norm_g, w_in, w_out, a_qn, a_kn, a_sinks, b_qn, b_kn, b_rel,
                t5_table, ck_a, cv_a, ck_b, cv_b, ffn2):
    w_in = _to_bf16(w_in)
    w_out = _to_bf16(w_out)
    w_in_p = jnp.concatenate([_pair_heads(w_in[:, :A_Q], 1), w_in[:, A_Q:]], axis=1)
    woa = _pair_heads(w_out[:A_Q], 0)
    wob = w_out[A_Q:]
    ones = lambda n: jnp.ones((n,), F32)
    scale = HEAD_DIM ** -0.5 * LOG2E
    gain_row = jnp.concatenate([jnp.tile(a_qn, A_HEADS) * scale, jnp.tile(a_kn, A_KV_HEADS), ones(A_KV),
                                jnp.tile(b_qn, B_HEADS) * scale, jnp.tile(b_kn, B_HEADS), ones(B_QKV)]
                               ).reshape(1, EVEN_IN).astype(F32)
    bias_a = _bias_a(t5_table) * LOG2E
    bias_b = _bias_b(b_rel) * LOG2E
    sink = jnp.broadcast_to(jnp.repeat(a_sinks.astype(F32) * LOG2E, PAIR_ROWS)[:, None],
                            (A_HEADS * PAIR_ROWS, LANES))
    la, lb = min(WINDOW, seq), min(B_REACH, seq)

    aq, ak, av, bq, bk, bv, cak, cav, cbk, cbv = _even_inproj(xp, norm_g, w_in_p, gain_row, nb, la, lb)
    r3 = lambda a: a.reshape(nb, seq, a.shape[-1])
    ya = _attn_a(r3(aq), r3(ak), r3(av), bias_a, sink, WINDOW, True)
    yb = _attn_b(r3(bq), r3(bk), r3(bv), bias_b, B_REACH, True)
    yap, ybp = ya.reshape(nb * seq, A_Q), yb.reshape(nb * seq, B_QKV)
    st_p = (cak.reshape(nb, la, A_KV_HEADS, HEAD_DIM), cav.reshape(nb, la, A_KV_HEADS, HEAD_DIM),
            cbk.reshape(nb, lb, B_HEADS, HEAD_DIM), cbv.reshape(nb, lb, B_HEADS, HEAD_DIM))

    ts = ndb * dseq
    aq, ak, av, bq, bk, bv, nak, nav, nbk, nbv = _even_inproj(xs, norm_g, w_in_p, gain_row, 1, ts, ts)
    pad_q = lambda a, rows: jnp.pad(a.reshape(ndb, dseq, a.shape[-1]), ((0, 0), (0, rows - dseq), (0, 0)))

    def window(cache, new, rows):
        w = cache.shape[-2] * cache.shape[-1]
        full = jnp.concatenate([cache.reshape(ndb, -1, w), new.reshape(ndb, dseq, w)], axis=1)
        buf = jnp.pad(full, ((0, 0), (0, rows - dseq), (0, 0))).astype(BF16)
        return full[:, dseq:].reshape(cache.shape), buf

    st_ak, kbuf_a = window(ck_a, nak, PAIR_ROWS)
    st_av, vbuf_a = window(cv_a, nav, PAIR_ROWS)
    st_bk, kbuf_b = window(ck_b, nbk, B_STEP_ROWS)
    st_bv, vbuf_b = window(cv_b, nbv, B_STEP_ROWS)
    ya = _attn_a(pad_q(aq, PAIR_ROWS), kbuf_a, vbuf_a, bias_a, sink, WINDOW, False)[:, :dseq]
    yb = _attn_b(pad_q(bq, B_STEP_ROWS), kbuf_b, vbuf_b, bias_b, B_REACH, False)[:, :dseq]
    xp, xs = _ffn(xp, xs, *ffn2, out_proj=(yap, ybp, ya.reshape(ts, A_Q), yb.reshape(ts, B_QKV), woa, wob))
    return xp, xs, st_p, (st_ak, st_av, st_bk, st_bv)


def _odd_layer(xp, xs, nb, seq, ndb, dseq, past, norm_g, ow, w_out, conv_prev, c_ckv, c_kpe, ffn2):
    w_out = _to_bf16(w_out)
    woc = w_out[:C_WIDTH]
    wod = w_out[C_WIDTH:]

    zero_init = jnp.zeros((nb, 8, C_WIDTH), F32)
    yc, q, k, v, ckv, kpe, cs = _odd_inproj(xp, norm_g, ow, zero_init, nb, 0)
    r3 = lambda a: a.reshape(nb, seq, a.shape[-1])
    ycp, ydp = yc, _mla_prompt(r3(q), r3(k), r3(v)).reshape(nb * seq, D_HEADS * D_V)
    st_p = (cs, ckv.reshape(nb, seq, D_KV_LORA), kpe.reshape(nb, seq, D_ROPE))

    ts = ndb * dseq
    init = jnp.pad(conv_prev.astype(F32), ((0, 0), (8 - (CONV_W - 1), 0), (0, 0)))
    yc, q, kn, vn, ckv, kpe, cs = _odd_inproj(xs, norm_g, ow, init, ndb, past)
    s3 = lambda a: a.reshape(ndb, dseq, a.shape[-1])
    yd = _mla_sample(s3(q), c_ckv, c_kpe, s3(kn), s3(vn), ow)
    xp, xs = _ffn(xp, xs, *ffn2, out_proj=(ycp, ydp, yc, yd.reshape(ts, D_HEADS * D_V), woc, wod))
    st_s = (cs, ckv.reshape(ndb, dseq, D_KV_LORA), kpe.reshape(ndb, dseq, D_ROPE))
    return xp, xs, st_p, st_s


def kernel(x_prompt, x_sample, cache_a_k, cache_a_v, cache_b_k, cache_b_v, state_c_conv, cache_d_ckv, cache_d_kpe, ff1_norm, ff1_w_gu, ff1_w_down, mix_norm, ff2_norm, ff2_w_gu, ff2_w_down, t5_bias_table, ev_w_in, ev_w_out, a_q_norm, a_k_norm, a_sinks, b_q_norm, b_k_norm, b_rel_bias, od_w_in, od_w_out, c_conv_w, d_q_a_norm, d_w_q_b, d_kv_a_norm, d_w_kv_b, d_q_nope_norm, d_q_rope_norm, d_k_nope_norm, d_k_rope_norm):
    nb, seq, _ = x_prompt.shape
    ndb, dseq, _ = x_sample.shape
    past = cache_d_ckv.shape[2]
    depth = ff1_norm.shape[0]
    assert seq % TOKEN_TILE == 0 and dseq == CHUNK and past % CHUNK == 0
    xp = x_prompt.reshape(nb * seq, D_MODEL)
    xs = x_sample.reshape(ndb * dseq, D_MODEL)
    even_p, even_s, odd_p, odd_s = [], [], [], []
    for l in range(depth):
        i = l // 2
        ffn2 = (ff2_norm[l], ff2_w_gu, ff2_w_down, l)
        xp, xs = _ffn(xp, xs, ff1_norm[l], ff1_w_gu, ff1_w_down, l)
        if l % 2 == 0:
            xp, xs, sp, ss = _even_layer(
                xp, xs, nb, seq, ndb, dseq, mix_norm[l], ev_w_in[i], ev_w_out[i], a_q_norm[i], a_k_norm[i],
                a_sinks[i], b_q_norm[i], b_k_norm[i], b_rel_bias[i], t5_bias_table,
                cache_a_k[i], cache_a_v[i], cache_b_k[i], cache_b_v[i], ffn2)
            even_p.append(sp)
            even_s.append(ss)
        else:
            ow = _odd_weights(od_w_in[i], c_conv_w[i], d_q_a_norm[i], d_w_q_b[i], d_kv_a_norm[i], d_w_kv_b[i],
                              d_q_nope_norm[i], d_q_rope_norm[i], d_k_nope_norm[i], d_k_rope_norm[i])
            xp, xs, sp, ss = _odd_layer(xp, xs, nb, seq, ndb, dseq, past, mix_norm[l], ow, od_w_out[i],
                                        state_c_conv[i], cache_d_ckv[i], cache_d_kpe[i], ffn2)
            odd_p.append(sp)
            odd_s.append(ss)
    stack = lambda group, j: group[0][j][None] if len(group) == 1 else jnp.stack([g[j] for g in group])
    return (xp.reshape(nb, seq, D_MODEL), xs.reshape(ndb, dseq, D_MODEL),
            stack(even_p, 0), stack(even_p, 1), stack(even_p, 2), stack(even_p, 3),
            stack(odd_p, 0), stack(odd_p, 1), stack(odd_p, 2),
            stack(even_s, 0), stack(even_s, 1), stack(even_s, 2), stack(even_s, 3),
            stack(odd_s, 0), stack(odd_s, 1), stack(odd_s, 2))
```

```python
import functools
import math

import jax
import jax.numpy as jnp
from jax import lax
from jax.experimental import pallas as pl
from jax.experimental.pallas import tpu as pltpu

F32 = jnp.float32
BF16 = jnp.bfloat16

D_MODEL = 1024
CHUNK = 64
HEAD_DIM = 64
EPS = 1e-6
A_HEADS = 8
A_KV_HEADS = 2
WINDOW = 128
T5_BUCKETS = 32
T5_MAX_DIST = 128
B_HEADS = 8
B_REACH = 512
B_MAX_REL = 128
C_WIDTH = 512
CONV_W = 3
D_HEADS = 8
D_Q_LORA = 256
D_KV_LORA = 128
D_NOPE = 64
D_ROPE = 32
D_V = 64
ROPE_THETA = 10000.0
FFN_DIM = 2816
A_Q = A_HEADS * HEAD_DIM
A_KV = A_KV_HEADS * HEAD_DIM
B_QKV = B_HEADS * HEAD_DIM
EVEN_IN = A_Q + 2 * A_KV + 3 * B_QKV

LANES = 128
TOKEN_TILE = 512
CAST_BLOCK_BYTES = 6 * 1024 * 1024
MXU_COLS = 256
FFN_CHUNK = MXU_COLS
PAIR_ROWS = 2 * CHUNK
MLA_QBLOCK = 512
VMEM_LIMIT_BYTES = 56 * 1024 * 1024
LOG2E = math.log2(math.e)
MLA_QSCALE = (D_NOPE + D_ROPE) ** -0.5 * LOG2E
NEG_INF = float("-inf")

_NT = (((1,), (1,)), ((), ()))


def _cparams(*sem):
    return pltpu.CompilerParams(dimension_semantics=sem, vmem_limit_bytes=VMEM_LIMIT_BYTES)


def _resident(shape):
    zeros = (0,) * len(shape)
    return pl.BlockSpec(shape, lambda *_: zeros, pipeline_mode=pl.Buffered(1))


def _rms(x, g):
    ms = jnp.mean(x * x, axis=-1, keepdims=True)
    return (x * lax.rsqrt(ms + EPS)) * g


def _dot(a, b):
    return jnp.dot(a, b, preferred_element_type=F32)


def _weight_stream(wgu_hbm, wd_hbm, wgu_ref, wd_ref, stage, sem):
    n_chunks = FFN_DIM // FFN_CHUNK
    g_stage, u_stage, d_stage = stage

    def copies(j):
        cols = pl.ds(j * FFN_CHUNK, FFN_CHUNK)
        slot = j % 2
        return (pltpu.make_async_copy(wgu_hbm.at[:, cols], g_stage.at[slot], sem.at[0, slot]),
                pltpu.make_async_copy(wgu_hbm.at[:, pl.ds(FFN_DIM + j * FFN_CHUNK, FFN_CHUNK)],
                                      u_stage.at[slot], sem.at[1, slot]),
                pltpu.make_async_copy(wd_hbm.at[cols, :], d_stage.at[slot], sem.at[2, slot]))

    def fetch(j):
        if j == 0:
            for cp in copies(0):
                cp.start()
        if j + 1 < n_chunks:
            for cp in copies(j + 1):
                cp.start()
        for cp in copies(j):
            cp.wait()
        slot = j % 2
        wgu_ref[:, j * FFN_CHUNK:(j + 1) * FFN_CHUNK] = g_stage[slot].astype(BF16)
        wgu_ref[:, FFN_DIM + j * FFN_CHUNK:FFN_DIM + (j + 1) * FFN_CHUNK] = u_stage[slot].astype(BF16)
        wd_ref[j * FFN_CHUNK:(j + 1) * FFN_CHUNK, :] = d_stage[slot].astype(BF16)

    return fetch


def _ffn_tile(x_ref, y_refs, wo_refs, g_ref, wgu_ref, wd_ref, o_ref, h_scr, a_scr, n_chunks, fetch=None):
    rows = x_ref.shape[0]
    if rows < h_scr.shape[0]:
        h_scr, a_scr = h_scr.at[pl.ds(0, rows)], a_scr.at[pl.ds(0, rows)]
    if y_refs:
        (ya_ref, yb_ref), (woa_ref, wob_ref) = y_refs, wo_refs
        y = _dot(ya_ref[...], woa_ref[...]) + _dot(yb_ref[...], wob_ref[...])
        o_ref[...] = x_ref[...] + y
        res_ref = o_ref
    else:
        res_ref = x_ref
    h_scr[...] = _rms(res_ref[...], g_ref[...]).astype(BF16)
    for j in range(n_chunks):
        if fetch is not None:
            fetch(j)
        h = h_scr[...]
        g = _dot(h, wgu_ref[:, j * FFN_CHUNK:(j + 1) * FFN_CHUNK])
        u = _dot(h, wgu_ref[:, FFN_DIM + j * FFN_CHUNK:FFN_DIM + (j + 1) * FFN_CHUNK])
        a_scr[:, j * FFN_CHUNK:(j + 1) * FFN_CHUNK] = ((g * jax.nn.sigmoid(g)) * u).astype(BF16)
    o_ref[...] = res_ref[...] + 0.5 * _dot(a_scr[...], wd_ref[...])


def _ffn_kernel(*refs, n_chunks, fused_out, layer, n_main):
    if fused_out:
        (xp_ref, yap_ref, ybp_ref, xs_ref, yas_ref, ybs_ref, woa_ref, wob_ref, g_ref, wgu_hbm, wd_hbm,
         op_ref, os_ref, h_scr, a_scr, wgu_ref, wd_ref, g_stage, u_stage, d_stage, sem) = refs
        yp, ys, wo = (yap_ref, ybp_ref), (yas_ref, ybs_ref), (woa_ref, wob_ref)
    else:
        (xp_ref, xs_ref, g_ref, wgu_hbm, wd_hbm,
         op_ref, os_ref, h_scr, a_scr, wgu_ref, wd_ref, g_stage, u_stage, d_stage, sem) = refs
        yp = ys = wo = None
    i = pl.program_id(0)
    tile = functools.partial(_ffn_tile, g_ref=g_ref, wgu_ref=wgu_ref, wd_ref=wd_ref, h_scr=h_scr, a_scr=a_scr,
                             n_chunks=n_chunks)

    @pl.when(i == 0)
    def _():
        fetch = _weight_stream(wgu_hbm.at[layer], wd_hbm.at[layer], wgu_ref, wd_ref,
                               (g_stage, u_stage, d_stage), sem)
        tile(xp_ref, yp, wo, o_ref=op_ref, fetch=fetch)

    @pl.when((i > 0) & (i < n_main))
    def _():
        tile(xp_ref, yp, wo, o_ref=op_ref)

    @pl.when(i == n_main)
    def _():
        tile(xs_ref, ys, wo, o_ref=os_ref)


def _ffn(xp, xs, norm_g, w_gu, w_down, layer, out_proj=None):
    tp, ts = xp.shape[0], xs.shape[0]
    tm = min(TOKEN_TILE, tp)
    n_main = tp // tm
    n_chunks = FFN_DIM // FFN_CHUNK
    assert n_main * tm == tp and ts <= tm and ts % 16 == 0
    main = lambda w: pl.BlockSpec((tm, w), lambda i: (jnp.minimum(i, n_main - 1), 0))
    side_in = lambda w: pl.BlockSpec((ts, w), lambda i: (0, 0), pipeline_mode=pl.Buffered(1))
    side_out = lambda w: pl.BlockSpec((ts, w), lambda i: (0, 0))
    in_hbm = pl.BlockSpec(memory_space=pl.ANY)
    if out_proj is not None:
        yap, ybp, yas, ybs, woa, wob = out_proj
        in_specs = [main(D_MODEL), main(yap.shape[1]), main(ybp.shape[1]),
                    side_in(D_MODEL), side_in(yas.shape[1]), side_in(ybs.shape[1]),
                    _resident(woa.shape), _resident(wob.shape)]
        args = [xp, yap, ybp, xs, yas, ybs, woa, wob]
    else:
        in_specs = [main(D_MODEL), side_in(D_MODEL)]
        args = [xp, xs]
    in_specs += [_resident((1, D_MODEL)), in_hbm, in_hbm]
    args += [norm_g.reshape(1, D_MODEL), w_gu, w_down]
    return pl.pallas_call(
        functools.partial(_ffn_kernel, n_chunks=n_chunks, fused_out=out_proj is not None, layer=layer,
                          n_main=n_main),
        grid=(n_main + 1,),
        in_specs=in_specs,
        out_specs=[main(D_MODEL), side_out(D_MODEL)],
        out_shape=[jax.ShapeDtypeStruct((tp, D_MODEL), F32), jax.ShapeDtypeStruct((ts, D_MODEL), F32)],
        scratch_shapes=[pltpu.VMEM((tm, D_MODEL), BF16), pltpu.VMEM((tm, FFN_DIM), BF16),
                        pltpu.VMEM((D_MODEL, 2 * FFN_DIM), BF16), pltpu.VMEM((FFN_DIM, D_MODEL), BF16),
                        pltpu.VMEM((2, D_MODEL, FFN_CHUNK), F32), pltpu.VMEM((2, D_MODEL, FFN_CHUNK), F32),
                        pltpu.VMEM((2, FFN_CHUNK, D_MODEL), F32),
                        pltpu.SemaphoreType.DMA((3, 2))],
        compiler_params=_cparams("arbitrary"),
        name="ffn_out" if out_proj is not None else "ffn",
    )(*args)


def _cast_kernel(x_ref, o_ref):
    o_ref[...] = x_ref[...].astype(BF16)


def _to_bf16(w, layer=None):
    r, c = w.shape[-2:]
    fits = [d for d in range(16, r + 1, 16) if r % d == 0 and d * c * 4 <= CAST_BLOCK_BYTES]
    tr = max(fits) if fits else r
    out_spec = pl.BlockSpec((tr, c), lambda i: (i, 0))
    in_spec = out_spec if layer is None else pl.BlockSpec((None, tr, c), lambda i: (layer, i, 0))
    return pl.pallas_call(
        _cast_kernel, grid=(r // tr,), in_specs=[in_spec], out_specs=out_spec,
        out_shape=jax.ShapeDtypeStruct((r, c), BF16),
        compiler_params=_cparams("arbitrary"), name="cast_bf16",
    )(w)


def _norm_halves(blk, gain):
    sq = blk * blk
    lo = lax.broadcasted_iota(jnp.int32, blk.shape, 1) < HEAD_DIM
    s_lo = jnp.sum(jnp.where(lo, sq, 0.0), axis=-1, keepdims=True)
    s_hi = jnp.sum(jnp.where(lo, 0.0, sq), axis=-1, keepdims=True)
    inv = jnp.where(lo, lax.rsqrt(s_lo * (1.0 / HEAD_DIM) + EPS), lax.rsqrt(s_hi * (1.0 / HEAD_DIM) + EPS))
    return (blk * inv) * gain


def _even_inproj_kernel(x_ref, g_ref, w_ref, gain_ref,
                        aq_ref, ak_ref, av_ref, bq_ref, bk_ref, bv_ref,
                        cak_ref, cav_ref, cbk_ref, cbv_ref, h_scr, z_scr, *, tm, tpb, rows_a, rows_b):
    i = pl.program_id(0)
    last = (i % tpb) == (tpb - 1)
    h_scr[...] = _rms(x_ref[...], g_ref[...]).astype(BF16)

    def lanes(c):
        return slice(c * LANES, (c + 1) * LANES)

    def emit(c):
        blk = z_scr[:, lanes(c)]
        gain = gain_ref[:, lanes(c)]
        if c < 4:
            aq_ref[:, lanes(c)] = _norm_halves(blk, gain).astype(BF16)
        elif c == 4:
            kn = _norm_halves(blk, gain)
            ak_ref[...] = kn.astype(BF16)
            z_scr[:, lanes(c)] = kn
        elif c == 5:
            av_ref[...] = blk.astype(BF16)
        elif c < 10:
            bq_ref[:, lanes(c - 6)] = _norm_halves(blk, gain).astype(BF16)
        elif c < 14:
            kn = _norm_halves(blk, gain)
            bk_ref[:, lanes(c - 10)] = kn.astype(BF16)
            z_scr[:, lanes(c)] = kn
        else:
            bv_ref[:, lanes(c - 14)] = blk.astype(BF16)

    for grp in range(EVEN_IN // MXU_COLS):
        cols = slice(grp * MXU_COLS, (grp + 1) * MXU_COLS)
        z_scr[:, cols] = _dot(h_scr[...], w_ref[:, cols])
        emit(2 * grp)
        emit(2 * grp + 1)

    @pl.when(last)
    def _():
        cak_ref[0] = z_scr[tm - rows_a:, lanes(4)]
        cav_ref[0] = z_scr[tm - rows_a:, lanes(5)]
        cbk_ref[0] = z_scr[tm - rows_b:, 10 * LANES:14 * LANES]
        cbv_ref[0] = z_scr[tm - rows_b:, 14 * LANES:18 * LANES]


def _even_inproj(x, norm_g, w_in, gain_row, n_batch, rows_a, rows_b):
    t = x.shape[0]
    tm = min(TOKEN_TILE, t)
    tpb = t // n_batch // tm
    assert tpb * tm * n_batch == t and rows_a <= tm and rows_b <= tm
    row = lambda w: pl.BlockSpec((tm, w), lambda i: (i, 0))
    cache = lambda r, w: pl.BlockSpec((1, r, w), lambda i: (i // tpb, 0, 0))
    bshape = lambda w: jax.ShapeDtypeStruct((t, w), BF16)
    cshape = lambda r, w: jax.ShapeDtypeStruct((n_batch, r, w), F32)
    return pl.pallas_call(
        functools.partial(_even_inproj_kernel, tm=tm, tpb=tpb, rows_a=rows_a, rows_b=rows_b),
        grid=(t // tm,),
        in_specs=[row(D_MODEL), _resident((1, D_MODEL)), _resident(w_in.shape), _resident(gain_row.shape)],
        out_specs=[row(A_Q), row(A_KV), row(A_KV), row(B_QKV), row(B_QKV), row(B_QKV),
                   cache(rows_a, A_KV), cache(rows_a, A_KV), cache(rows_b, B_QKV), cache(rows_b, B_QKV)],
        out_shape=[bshape(A_Q), bshape(A_KV), bshape(A_KV), bshape(B_QKV), bshape(B_QKV), bshape(B_QKV),
                   cshape(rows_a, A_KV), cshape(rows_a, A_KV), cshape(rows_b, B_QKV), cshape(rows_b, B_QKV)],
        scratch_shapes=[pltpu.VMEM((tm, D_MODEL), BF16), pltpu.VMEM((tm, EVEN_IN), F32)],
        compiler_params=_cparams("arbitrary"),
        name="even_inproj",
    )(x, norm_g.reshape(1, D_MODEL), w_in, gain_row)


def _pair_heads(a, axis):
    shape = a.shape
    split = shape[:axis] + (2, A_HEADS // 2, HEAD_DIM) + shape[axis + 1:]
    return a.reshape(split).swapaxes(axis, axis + 1).reshape(shape)


def _fill_padded(buf, src_ref, ctx):
    buf[0:ctx, :] = jnp.zeros((ctx, buf.shape[1]), buf.dtype)
    buf[ctx:, :] = src_ref[0]


def _windows(k_ref, v_ref, pad_scr, ctx, pad_front, step_rows, stream):
    win = ctx + step_rows
    if pad_front:
        kbuf, vbuf = pad_scr
        _fill_padded(kbuf, k_ref, ctx)
        _fill_padded(vbuf, v_ref, ctx)
        return (lambda r0: kbuf[pl.ds(r0, win), :]), (lambda r0: vbuf[pl.ds(r0, win), :])
    return (lambda r0: k_ref[stream, pl.ds(r0, win), :]), (lambda r0: v_ref[stream, pl.ds(r0, win), :])


def _fold_lanes(x, op):
    acc = x[:, :LANES]
    for c in range(1, x.shape[1] // LANES):
        acc = op(acc, x[:, c * LANES:(c + 1) * LANES])
    return acc


def _row_stat(reduce_fn, x):
    return jnp.broadcast_to(reduce_fn(x, axis=-1, keepdims=True), x.shape)


PAIR_SLOTS = 4


def _run_steps(make_stages, n_streams, n_steps, n_front, step_rows, lookahead):
    stages = [make_stages(stream) for stream in range(n_streams)]
    tasks = [(stream, t) for stream in range(n_streams) for t in range(n_steps)]

    def run(stage, k):
        stream, t = tasks[k]
        stages[stream][stage](t * step_rows, k % PAIR_SLOTS, t < n_front)

    n = len(tasks)
    if not lookahead:
        for k in range(n):
            for stage in range(3):
                run(stage, k)
        return
    run(0, 0)
    if n > 1:
        run(0, 1)
    run(1, 0)
    for k in range(n):
        if k + 2 < n:
            run(0, k + 2)
        if k + 1 < n:
            run(1, k + 1)
        run(2, k)


def _band_bias(ext_row, n_rows, ctx):
    win = ctx + n_rows
    period = ext_row.shape[1]
    t = pltpu.roll(jnp.broadcast_to(ext_row, (n_rows, period)), 0, 1, stride=1, stride_axis=0)[:, :win]
    chunk = lax.broadcasted_iota(jnp.int32, (n_rows, win), 0) // CHUNK
    col = lax.broadcasted_iota(jnp.int32, (n_rows, win), 1)
    visible = (col >= CHUNK * chunk) & (col < ctx + CHUNK * (chunk + 1))
    return jnp.where(visible, t, NEG_INF)


def _attn_a_kernel(q_ref, k_ref, v_ref, ext_ref, sink_ref, o_ref, bias_ref, s_scr, e_scr, *pad_scr,
                   nq, ctx, pad_front):
    win = ctx + PAIR_ROWS
    lo = lax.broadcasted_iota(jnp.int32, (PAIR_ROWS, LANES), 1) < HEAD_DIM

    @pl.when(pl.program_id(0) == 0)
    def _():
        for head in range(A_HEADS):
            bias_ref[head * PAIR_ROWS:(head + 1) * PAIR_ROWS, :] = _band_bias(
                ext_ref[head:head + 1, :], PAIR_ROWS, ctx)

    def make_stages(stream):
        kwin, vwin = _windows(k_ref, v_ref, pad_scr, ctx, pad_front, PAIR_ROWS, stream)
        return (functools.partial(scores, stream, kwin), functools.partial(softmax, stream),
                functools.partial(values, stream, vwin))

    def scores(stream, kwin, r0, slot, masked):
        q_lo, q_hi = [], []
        for p in range(4):
            qp = q_ref[stream, pl.ds(r0, PAIR_ROWS), p * LANES:(p + 1) * LANES]
            zero = jnp.zeros_like(qp)
            q_lo.append(jnp.where(lo, qp, zero))
            q_hi.append(jnp.where(lo, zero, qp))
        q2 = jnp.concatenate(q_lo + q_hi, axis=0)
        s_scr[slot] = lax.dot_general(q2, kwin(r0), _NT, preferred_element_type=F32)

    def softmax(stream, r0, slot, masked):
        for head in range(A_HEADS):
            rows = slice(head * PAIR_ROWS, (head + 1) * PAIR_ROWS)
            s = s_scr[slot, rows, :] + bias_ref[rows, :]
            if masked:
                col = lax.broadcasted_iota(jnp.int32, (PAIR_ROWS, win), 1)
                s = jnp.where(col + r0 >= ctx, s, NEG_INF)
            sk = sink_ref[rows, :]
            m = jnp.maximum(_row_stat(jnp.max, _fold_lanes(s, jnp.maximum)), sk)
            e = [jnp.exp2(s[:, c * LANES:(c + 1) * LANES] - m) for c in range(win // LANES)]
            den = _row_stat(jnp.sum, functools.reduce(jnp.add, e)) + jnp.exp2(sk - m)
            inv = 1.0 / den
            for c in range(win // LANES):
                e_scr[slot, rows, c * LANES:(c + 1) * LANES] = (e[c] * inv).astype(BF16)

    def values(stream, vwin, r0, slot, masked):
        o2 = _dot(e_scr[slot], vwin(r0))
        half = 4 * PAIR_ROWS
        for p in range(4):
            o = jnp.where(lo, o2[p * PAIR_ROWS:(p + 1) * PAIR_ROWS],
                          o2[half + p * PAIR_ROWS:half + (p + 1) * PAIR_ROWS])
            o_ref[stream, pl.ds(r0, PAIR_ROWS), p * LANES:(p + 1) * LANES] = o.astype(BF16)

    _run_steps(make_stages, q_ref.shape[0], nq // PAIR_ROWS, ctx // PAIR_ROWS if pad_front else 0, PAIR_ROWS,
               lookahead=False)


def _streams_per_block(nb, pad_front):
    return 1 if pad_front else nb


def _attn_a(q, k, v, ext, sink, ctx, pad_front):
    nb, nq, _ = q.shape
    nk = k.shape[1]
    assert nq % PAIR_ROWS == 0 and nk == (nq if pad_front else ctx + nq)
    bb = _streams_per_block(nb, pad_front)
    per_b = lambda r, w: pl.BlockSpec((bb, r, w), lambda b: (b, 0, 0))
    rows, win = A_HEADS * PAIR_ROWS, ctx + PAIR_ROWS
    scratch = [pltpu.VMEM((rows, win), F32),
               pltpu.VMEM((PAIR_SLOTS, rows, win), F32), pltpu.VMEM((PAIR_SLOTS, rows, win), BF16)]
    if pad_front:
        scratch += [pltpu.VMEM((ctx + nq, A_KV), BF16)] * 2
    return pl.pallas_call(
        functools.partial(_attn_a_kernel, nq=nq, ctx=ctx, pad_front=pad_front),
        grid=(nb // bb,),
        in_specs=[per_b(nq, A_Q), per_b(nk, A_KV), per_b(nk, A_KV), _resident(ext.shape), _resident(sink.shape)],
        out_specs=per_b(nq, A_Q),
        out_shape=jax.ShapeDtypeStruct((nb, nq, A_Q), BF16),
        scratch_shapes=scratch,
        compiler_params=_cparams("arbitrary"),
        name="attn_a",
    )(q, k, v, ext, sink)


B_STEP_ROWS = 4 * CHUNK


def _attn_b_kernel(q_ref, k_ref, v_ref, ext_ref, o_ref, bias_scr, s_scr, e_scr, r_scr, *pad_scr,
                   nq, ctx, pad_front):
    win = ctx + B_STEP_ROWS
    lo = lax.broadcasted_iota(jnp.int32, (B_STEP_ROWS, LANES), 1) < HEAD_DIM
    chunks = B_STEP_ROWS // CHUNK
    pair_idx = pl.program_id(1)

    @pl.when(pl.program_id(0) == 0)
    def _():
        for half in range(2):
            bias_scr[pair_idx, half * B_STEP_ROWS:(half + 1) * B_STEP_ROWS, :] = _band_bias(
                ext_ref[0, half:half + 1, :], B_STEP_ROWS, ctx)

    bias_ref = bias_scr.at[pair_idx]

    def make_stages(stream):
        kwin, vwin = _windows(k_ref, v_ref, pad_scr, ctx, pad_front, B_STEP_ROWS, stream)
        return (functools.partial(scores, stream, kwin), functools.partial(softmax, stream),
                functools.partial(values, stream, vwin))

    def scores(stream, kwin, r0, slot, masked):
        qp = q_ref[stream, pl.ds(r0, B_STEP_ROWS), :]
        zero = jnp.zeros_like(qp)
        q2 = jnp.concatenate([jnp.where(lo, qp, zero), jnp.where(lo, zero, qp)], axis=0)
        s_scr[slot] = lax.dot_general(q2, kwin(r0), _NT, preferred_element_type=F32)

    def softmax(stream, r0, slot, masked):
        for rb in range(2 * chunks):
            rows = slice(rb * CHUNK, (rb + 1) * CHUNK)
            ci = rb % chunks
            c_lo, c_hi = CHUNK * ci // LANES, -(-(ctx + CHUNK * (ci + 1)) // LANES)
            band = slice(c_lo * LANES, c_hi * LANES)
            s = s_scr[slot, rows, band] + bias_ref[rows, band]
            if masked:
                col = lax.broadcasted_iota(jnp.int32, s.shape, 1) + c_lo * LANES
                s = jnp.where(col + r0 >= ctx, s, NEG_INF)
            m = _row_stat(jnp.max, _fold_lanes(s, jnp.maximum))
            e = [jnp.exp2(s[:, c * LANES:(c + 1) * LANES] - m) for c in range(c_hi - c_lo)]
            for c in range(win // LANES):
                blk = e[c - c_lo].astype(BF16) if c_lo <= c < c_hi else jnp.zeros((CHUNK, LANES), BF16)
                e_scr[slot, rows, c * LANES:(c + 1) * LANES] = blk
            r_scr[slot, rows, :] = 1.0 / _row_stat(jnp.sum, functools.reduce(jnp.add, e))

    def values(stream, vwin, r0, slot, masked):
        o2 = _dot(e_scr[slot], vwin(r0)) * r_scr[slot]
        o = jnp.where(lo, o2[:B_STEP_ROWS], o2[B_STEP_ROWS:])
        o_ref[stream, pl.ds(r0, B_STEP_ROWS), :] = o.astype(BF16)

    _run_steps(make_stages, q_ref.shape[0], nq // B_STEP_ROWS,
               -(-ctx // B_STEP_ROWS) if pad_front else 0, B_STEP_ROWS, lookahead=True)


def _attn_b(q, k, v, ext, ctx, pad_front):
    nb, nq, _ = q.shape
    nk = k.shape[1]
    assert nq % B_STEP_ROWS == 0 and nk == (nq if pad_front else ctx + nq)
    bb = _streams_per_block(nb, pad_front)
    blk = lambda r: pl.BlockSpec((bb, r, LANES), lambda b, p: (b, 0, p))
    rows, win = 2 * B_STEP_ROWS, ctx + B_STEP_ROWS
    scratch = [pltpu.VMEM((B_HEADS // 2, rows, win), F32),
               pltpu.VMEM((PAIR_SLOTS, rows, win), F32), pltpu.VMEM((PAIR_SLOTS, rows, win), BF16),
               pltpu.VMEM((PAIR_SLOTS, rows, LANES), F32)]
    if pad_front:
        scratch += [pltpu.VMEM((ctx + nq, LANES), BF16)] * 2
    return pl.pallas_call(
        functools.partial(_attn_b_kernel, nq=nq, ctx=ctx, pad_front=pad_front),
        grid=(nb // bb, B_HEADS // 2),
        in_specs=[blk(nq), blk(nk), blk(nk),
                  pl.BlockSpec((1,) + ext.shape[1:], lambda b, p: (p, 0, 0))],
        out_specs=blk(nq),
        out_shape=jax.ShapeDtypeStruct((nb, nq, B_QKV), BF16),
        scratch_shapes=scratch,
        compiler_params=_cparams("arbitrary", "arbitrary"),
        name="attn_b",
    )(q, k, v, ext)


def _t5_bucket(rel):
    nb = T5_BUCKETS // 2
    max_exact = nb // 2
    n = -rel
    ret = jnp.where(n < 0, nb, 0)
    n = jnp.abs(n)
    nf = jnp.maximum(n, 1).astype(F32)
    large = max_exact + (jnp.log(nf / max_exact) / math.log(T5_MAX_DIST / max_exact)
                         * (nb - max_exact)).astype(jnp.int32)
    large = jnp.minimum(large, nb - 1)
    return ret + jnp.where(n < max_exact, n, large)


def _band_values(ctx, n_rows, value_of_rel):
    n_cols = ctx + n_rows
    period = -(-(n_cols + n_rows - 1) // LANES) * LANES
    d = jnp.concatenate([jnp.arange(0, period - (n_rows - 1)), jnp.arange(-(n_rows - 1), 0)])
    return value_of_rel(d - ctx)


def _bias_a(t5_table):
    return _band_values(WINDOW, PAIR_ROWS, lambda rel: t5_table.astype(F32)[_t5_bucket(rel)].T)


def _bias_b(b_rel):
    ext = _band_values(B_REACH, B_STEP_ROWS,
                       lambda rel: b_rel.astype(F32)[:, jnp.clip(rel, -B_MAX_REL, B_MAX_REL) + B_MAX_REL])
    return ext.reshape(B_HEADS // 2, 2, ext.shape[-1])


ODD_IN_PAD = 2048
ROPE_HALF = D_ROPE // 2
X1_LO = LANES // 2
NOPE_SPLIT = X1_LO - ROPE_HALF


def _head_lane_source():
    zero = D_NOPE + D_ROPE
    src = []
    for lane in range(LANES):
        if lane < ROPE_HALF:
            src.append(D_NOPE + ROPE_HALF + lane)
        elif lane < X1_LO:
            src.append(lane - ROPE_HALF)
        elif lane < X1_LO + ROPE_HALF:
            src.append(D_NOPE + lane - X1_LO)
        elif lane < X1_LO + ROPE_HALF + D_NOPE - NOPE_SPLIT:
            src.append(NOPE_SPLIT + lane - X1_LO - ROPE_HALF)
        else:
            src.append(zero)
    return src


def _to_head_lanes(a):
    padded = jnp.concatenate([a, jnp.zeros(a.shape[:-1] + (1,), a.dtype)], axis=-1)
    return padded[..., jnp.asarray(_head_lane_source(), dtype=jnp.int32)]


def _lane_masks(shape):
    lane = lax.broadcasted_iota(jnp.int32, shape, 1)
    rope = (lane < ROPE_HALF) | ((lane >= X1_LO) & (lane < X1_LO + ROPE_HALF))
    used = lane < X1_LO + ROPE_HALF + D_NOPE - NOPE_SPLIT
    return used & ~rope, rope


def _rope(y, cosf, sinf):
    return y * cosf + pltpu.roll(y, LANES // 2, 1) * sinf


ROW_BLOCK = 128


def _row_blocks(tm):
    rb = min(ROW_BLOCK, tm)
    return [slice(r, r + rb) for r in range(0, tm, rb)]


def _segment_mean_squares(src_scr, cols, seg_ref, sq_scr, ms_scr, slot):
    for rows in _row_blocks(src_scr.shape[0]):
        v = src_scr[rows, cols]
        sq_scr[slot, rows, :] = (v * v).astype(BF16)
    ms_scr[slot] = _dot(sq_scr[slot], seg_ref[...])


def _expand_kv(ckvb_ref, kpe_blk_ref, wkv_ref, kgain_ref, kseg_ref, kv_scr, sq_scr, ms_scr, k_ref, v_ref,
               between=None):
    k_cols = D_HEADS * LANES
    tm = kv_scr.shape[0]
    for grp in range(kv_scr.shape[1] // MXU_COLS):
        if between and grp in between:
            between[grp]()
        cols = slice(grp * MXU_COLS, (grp + 1) * MXU_COLS)
        kv_scr[:, cols] = _dot(ckvb_ref[...], wkv_ref[:, cols])
        if cols.start < k_cols:
            slot = grp % 2
            _segment_mean_squares(kv_scr, cols, kseg_ref, sq_scr, ms_scr, slot)
        for rows in _row_blocks(tm):
            if cols.start < k_cols:
                for half in range(2):
                    h = 2 * grp + half
                    kb = kv_scr[rows, h * LANES:(h + 1) * LANES]
                    ms = ms_scr[slot, rows, half * LANES:(half + 1) * LANES]
                    kn = (kb * lax.rsqrt(ms + EPS)) * kgain_ref[...]
                    k_ref[rows, h * LANES:(h + 1) * LANES] = (kn + kpe_blk_ref[rows, :]).astype(BF16)
            else:
                v_ref[rows, cols.start - k_cols:cols.stop - k_cols] = kv_scr[rows, cols].astype(BF16)


def _odd_inproj_kernel(x_ref, g_ref, w_ref, cinit_ref, convw_ref, qan_ref, wqb_ref, qgain_ref,
                       kvan_ref, krgain_ref, invf_ref, sgn_ref, wkv_ref, kgain_ref, qseg_ref, kseg_ref,
                       yc_ref, q_ref, k_ref, v_ref, ckv_ref, kpe_ref, cs_ref,
                       h_scr, z_scr, uext_scr, qn_scr, q_scr, kv_scr, rot_scr, ckvb_scr, kpe_scr, sq_scr, ms_scr,
                       *, tm, tpb, pos0):
    i = pl.program_id(0)
    tile = i % tpb
    off = pl.multiple_of(tile * tm, tm)
    blocks = _row_blocks(tm)
    rb = blocks[0].stop

    @pl.when(tile == 0)
    def _():
        uext_scr[0:8, :] = cinit_ref[0]

    @pl.when(i < tpb)
    def _():
        row = lax.broadcasted_iota(jnp.int32, (tm, LANES), 0)
        ang = (row + (tile * tm + pos0)).astype(F32) * invf_ref[...]
        rot_scr[0, pl.ds(off, tm), :] = jnp.cos(ang)
        rot_scr[1, pl.ds(off, tm), :] = jnp.sin(ang) * sgn_ref[...]

    h_scr[...] = _rms(x_ref[...], g_ref[...]).astype(BF16)
    low_lanes = lax.broadcasted_iota(jnp.int32, (rb, LANES), 1) < ROPE_HALF

    def proj(c0, c1):
        z_scr[:, c0:c1] = _dot(h_scr[...], w_ref[:, c0:c1])

    def rotary(y, rows):
        cosf = rot_scr[0, pl.ds(off + rows.start, rb), :]
        sinf = rot_scr[1, pl.ds(off + rows.start, rb), :]
        return _rope(y, cosf, sinf)

    def conv_group(grp):
        c0 = grp * MXU_COLS
        cols = slice(c0, c0 + MXU_COLS)
        for base in (0, C_WIDTH, 2 * C_WIDTH):
            proj(base + c0, base + c0 + MXU_COLS)
        for rows in blocks:
            up = slice(rows.start + 8, rows.stop + 8)
            uext_scr[up, cols] = (z_scr[rows, C_WIDTH + c0:C_WIDTH + c0 + MXU_COLS]
                                  * z_scr[rows, 2 * C_WIDTH + c0:2 * C_WIDTH + c0 + MXU_COLS])
        for rows in blocks:
            yconv = convw_ref[0:1, cols] * uext_scr[rows.start + 6:rows.stop + 6, cols]
            yconv = yconv + convw_ref[1:2, cols] * uext_scr[rows.start + 7:rows.stop + 7, cols]
            yconv = yconv + convw_ref[2:3, cols] * uext_scr[rows.start + 8:rows.stop + 8, cols]
            yc_ref[rows, cols] = (z_scr[rows, cols] * yconv).astype(BF16)

    proj(1536, 1792)
    proj(1792, 2048)

    for rows in blocks:
        qn_scr[rows, :] = _rms(z_scr[rows, 1536:1792], qan_ref[...]).astype(BF16)
        ckv = _rms(z_scr[rows, 1792:1920], kvan_ref[...])
        ckv_ref[rows, :] = ckv
        ckvb_scr[rows, :] = ckv.astype(BF16)

    for grp in range(D_HEADS * LANES // MXU_COLS):
        cols = slice(grp * MXU_COLS, (grp + 1) * MXU_COLS)
        q_scr[:, cols] = _dot(qn_scr[...], wqb_ref[:, cols])
        slot = grp % 2
        _segment_mean_squares(q_scr, cols, qseg_ref, sq_scr, ms_scr, slot)
        for half in range(2):
            h = 2 * grp + half
            for rows in blocks:
                blk = q_scr[rows, h * LANES:(h + 1) * LANES]
                inv = lax.rsqrt(ms_scr[slot, rows, half * LANES:(half + 1) * LANES] + EPS)
                y = rotary((blk * inv) * qgain_ref[...], rows)
                q_ref[rows, h * LANES:(h + 1) * LANES] = y.astype(BF16)

    for rows in blocks:
        kb = z_scr[rows, 1920:2048]
        ms = jnp.sum(kb * kb, axis=-1, keepdims=True) * (1.0 / D_ROPE)
        kpe_blk = rotary((kb * lax.rsqrt(ms + EPS)) * krgain_ref[...], rows)
        kpe_scr[rows, :] = kpe_blk
        x1_then_x2 = jnp.where(low_lanes, pltpu.roll(kpe_blk, LANES - X1_LO, 1), pltpu.roll(kpe_blk, ROPE_HALF, 1))
        kpe_ref[rows, :] = x1_then_x2[:, 0:D_ROPE]
    _expand_kv(ckvb_scr, kpe_scr, wkv_ref, kgain_ref, kseg_ref, kv_scr, sq_scr, ms_scr, k_ref, v_ref,
               between={0: lambda: conv_group(0), 4: lambda: conv_group(1)})

    cs_ref[0] = uext_scr[tm + 6:tm + 8, :]
    uext_scr[0:8, :] = uext_scr[tm:tm + 8, :]


def _odd_inproj(x, norm_g, ow, conv_init, n_batch, pos0):
    t = x.shape[0]
    tm = min(TOKEN_TILE, t // n_batch)
    tpb = t // n_batch // tm
    assert tpb * tm * n_batch == t
    row = lambda w: pl.BlockSpec((tm, w), lambda i: (i, 0))
    per_b = lambda r, w: pl.BlockSpec((1, r, w), lambda i: (i // tpb, 0, 0))
    small = [ow["conv_w"], ow["qan"], ow["wqb"], ow["qgain"], ow["kvan"], ow["krgain"],
             ow["invf"], ow["sgn"], ow["wkv"], ow["kgain"], ow["qseg"], ow["kseg"]]
    kv_w = D_HEADS * LANES
    v_w = D_HEADS * D_V
    return pl.pallas_call(
        functools.partial(_odd_inproj_kernel, tm=tm, tpb=tpb, pos0=pos0),
        grid=(t // tm,),
        in_specs=[row(D_MODEL), _resident((1, D_MODEL)), _resident(ow["w_in"].shape), per_b(8, C_WIDTH)]
                 + [_resident(a.shape) for a in small],
        out_specs=[row(C_WIDTH), row(kv_w), row(kv_w), row(v_w), row(D_KV_LORA), row(D_ROPE),
                   per_b(CONV_W - 1, C_WIDTH)],
        out_shape=[jax.ShapeDtypeStruct((t, C_WIDTH), BF16), jax.ShapeDtypeStruct((t, kv_w), BF16),
                   jax.ShapeDtypeStruct((t, kv_w), BF16), jax.ShapeDtypeStruct((t, v_w), BF16),
                   jax.ShapeDtypeStruct((t, D_KV_LORA), F32), jax.ShapeDtypeStruct((t, D_ROPE), F32),
                   jax.ShapeDtypeStruct((n_batch, CONV_W - 1, C_WIDTH), F32)],
        scratch_shapes=[pltpu.VMEM((tm, D_MODEL), BF16), pltpu.VMEM((tm, ODD_IN_PAD), F32),
                        pltpu.VMEM((tm + 8, C_WIDTH), F32), pltpu.VMEM((tm, D_Q_LORA), BF16),
                        pltpu.VMEM((tm, kv_w), F32), pltpu.VMEM((tm, kv_w + v_w), F32),
                        pltpu.VMEM((2, tpb * tm, LANES), F32), pltpu.VMEM((tm, D_KV_LORA), BF16),
                        pltpu.VMEM((tm, LANES), F32), pltpu.VMEM((2, tm, MXU_COLS), BF16),
                        pltpu.VMEM((2, tm, MXU_COLS), F32)],
        compiler_params=_cparams("arbitrary"),
        name="odd_inproj",
    )(x, norm_g.reshape(1, D_MODEL), ow["w_in"], conv_init, *small)


def _odd_weights(w_in, conv_w, q_a_norm, w_q_b, kv_a_norm, w_kv_b, qn_nope, qn_rope, kn_nope, kn_rope):
    z = lambda n: jnp.zeros((n,), F32)
    qk = D_NOPE + D_ROPE
    w_in = _to_bf16(w_in)
    kr_blk = _to_head_lanes(jnp.concatenate([jnp.zeros((D_MODEL, D_NOPE), BF16), w_in[:, 1920:]], axis=1))
    w_pad = jnp.concatenate([w_in[:, :1920], kr_blk], axis=1)
    wqb = _to_head_lanes(w_q_b.astype(BF16).reshape(D_Q_LORA, D_HEADS, qk)).reshape(D_Q_LORA, D_HEADS * LANES)
    kvb = w_kv_b.astype(BF16).reshape(D_KV_LORA, D_HEADS, D_NOPE + D_V)
    wk = _to_head_lanes(jnp.concatenate([kvb[:, :, :D_NOPE], jnp.zeros((D_KV_LORA, D_HEADS, D_ROPE), BF16)], axis=-1))
    wkv = jnp.concatenate([wk.reshape(D_KV_LORA, D_HEADS * LANES),
                           kvb[:, :, D_NOPE:].reshape(D_KV_LORA, D_HEADS * D_V)], axis=1)
    inv = 1.0 / (ROPE_THETA ** (jnp.arange(ROPE_HALF, dtype=F32) / ROPE_HALF))
    ones = jnp.ones((ROPE_HALF,), F32)
    lanes = lambda v: _to_head_lanes(v).reshape(1, LANES)
    in_nope = lanes(jnp.concatenate([jnp.ones((D_NOPE,), F32), z(D_ROPE)]))
    in_rope = lanes(jnp.concatenate([z(D_NOPE), jnp.ones((D_ROPE,), F32)]))
    seg_nope = in_nope.T * in_nope * (1.0 / D_NOPE)
    seg_rope = in_rope.T * in_rope * (1.0 / D_ROPE)
    two_heads = lambda m: jnp.kron(jnp.eye(MXU_COLS // LANES, dtype=F32), m).astype(BF16)
    return {
        "qseg": two_heads(seg_nope + seg_rope),
        "kseg": two_heads(seg_nope),
        "wk_dense": kvb[:, :, :D_NOPE].reshape(D_KV_LORA, D_HEADS * D_NOPE),
        "kseg_t": jnp.pad(jnp.kron(jnp.eye(D_HEADS, dtype=F32), jnp.full((1, D_NOPE), 1.0 / D_NOPE, F32)),
                          ((0, 16 - D_HEADS), (0, 0))).astype(BF16),
        "q_absorb": _to_head_lanes(jnp.concatenate(
            [w_kv_b.reshape(D_KV_LORA, D_HEADS, D_NOPE + D_V)[:, :, :D_NOPE] * kn_nope,
             jnp.zeros((D_KV_LORA, D_HEADS, D_ROPE), F32)], axis=-1)).transpose(1, 2, 0).astype(BF16),
        "wv_lanes": jnp.stack([jnp.pad(kvb[:, h, D_NOPE:], ((0, 0), ((h % 2) * D_V, LANES - D_V - (h % 2) * D_V)))
                               for h in range(D_HEADS)]),
        "w_in": w_pad,
        "conv_w": conv_w.astype(F32),
        "qan": q_a_norm.reshape(1, D_Q_LORA),
        "wqb": wqb,
        "qgain": lanes(jnp.concatenate([qn_nope, qn_rope])) * MLA_QSCALE,
        "kvan": kv_a_norm.reshape(1, D_KV_LORA),
        "krgain": lanes(jnp.concatenate([z(D_NOPE), kn_rope])),
        "invf": lanes(jnp.concatenate([z(D_NOPE), inv, inv])),
        "sgn": lanes(jnp.concatenate([z(D_NOPE), -ones, ones])),
        "wkv": wkv,
        "kgain": lanes(jnp.concatenate([kn_nope, z(D_ROPE)])),
        "place": _to_head_lanes(jnp.concatenate([jnp.zeros((D_ROPE, D_NOPE), BF16), jnp.eye(D_ROPE, dtype=BF16)], axis=1)),
    }


def _mla_prompt_kernel(q_ref, k_ref, v_ref, o_ref, s_scr, e_scr, r_scr, *, seq):
    qb = min(MLA_QBLOCK, seq)
    hb = qb // 2
    lo = lax.broadcasted_iota(jnp.int32, (qb, LANES), 1) < D_V
    first_half = lax.broadcasted_iota(jnp.int32, (CHUNK, LANES), 1) < CHUNK
    scores = lambda q, k: lax.dot_general(q, k, _NT, preferred_element_type=F32)
    top, bot = slice(0, hb), slice(hb, qb)

    def spans(i):
        tk = qb * (i + 1)
        return tk - qb, tk - hb, tk

    def qk(i, h):
        t0, t1, tk = spans(i)
        hl = slice(h * LANES, (h + 1) * LANES)
        q0 = i * qb
        if t0 > 0:
            s_scr[h, :, 0:t0] = scores(q_ref[0, q0:q0 + qb, hl], k_ref[0, 0:t0, hl])
        s_scr[h, top, t0:t1] = scores(q_ref[0, q0:q0 + hb, hl], k_ref[0, t0:t1, hl])
        s_scr[h, bot, t0:tk] = scores(q_ref[0, q0 + hb:q0 + qb, hl], k_ref[0, t0:tk, hl])

    def softmax(i, h):
        t0, t1, tk = spans(i)
        for rb in range(qb // CHUNK):
            rows = slice(rb * CHUNK, (rb + 1) * CHUNK)
            visible = t0 + CHUNK * (rb + 1)
            width = t1 if rb < hb // CHUNK else tk
            n_full, ragged = visible // LANES, visible % LANES != 0
            blk = lambda c: s_scr[h, rows, c * LANES:(c + 1) * LANES]
            cols = [blk(c) for c in range(n_full)]
            if ragged:
                cols.append(jnp.where(first_half, blk(n_full), NEG_INF))
            mm = cols[0]
            for c in cols[1:]:
                mm = jnp.maximum(mm, c)
            m = _row_stat(jnp.max, mm)
            acc = None
            for c in range(width // LANES):
                if c < len(cols):
                    e = jnp.exp2(blk(c) - m)
                    if c >= n_full:
                        e = jnp.where(first_half, e, 0.0)
                    acc = e if acc is None else acc + e
                else:
                    e = jnp.zeros((CHUNK, LANES), F32)
                e_scr[h, rows, c * LANES:(c + 1) * LANES] = e.astype(BF16)
            r_scr[h, rows, :] = 1.0 / _row_stat(jnp.sum, acc)

    def pv(i, h):
        t0, t1, tk = spans(i)
        out = jnp.concatenate([_dot(e_scr[h, top, t0:t1], v_ref[0, t0:t1, :]),
                               _dot(e_scr[h, bot, t0:tk], v_ref[0, t0:tk, :])], axis=0)
        if t0 > 0:
            out = out + _dot(e_scr[h, :, 0:t0], v_ref[0, 0:t0, :])
        return out * r_scr[h]

    tasks = [(i, h) for i in range(seq // qb) for h in range(2)]
    n = len(tasks)
    outs = {}
    qk(*tasks[0])
    if n > 1:
        qk(*tasks[1])
    softmax(*tasks[0])
    for k, (i, h) in enumerate(tasks):
        if k + 2 < n:
            qk(*tasks[k + 2])
        if k + 1 < n:
            softmax(*tasks[k + 1])
        outs[h] = pv(i, h)
        if h == 1:
            o_ref[0, i * qb:(i + 1) * qb, :] = jnp.where(lo, outs[0], outs[1]).astype(BF16)


def _mla_prompt(q, k, v):
    nb, seq, _ = q.shape
    qb = min(MLA_QBLOCK, seq)
    return pl.pallas_call(
        functools.partial(_mla_prompt_kernel, seq=seq),
        grid=(nb, D_HEADS // 2),
        in_specs=[pl.BlockSpec((1, seq, 2 * LANES), lambda b, p: (b, 0, p)),
                  pl.BlockSpec((1, seq, 2 * LANES), lambda b, p: (b, 0, p)),
                  pl.BlockSpec((1, seq, LANES), lambda b, p: (b, 0, p))],
        out_specs=pl.BlockSpec((1, seq, LANES), lambda b, p: (b, 0, p)),
        out_shape=jax.ShapeDtypeStruct((nb, seq, D_HEADS * D_V), BF16),
        scratch_shapes=[pltpu.VMEM((2, qb, seq), F32), pltpu.VMEM((2, qb, seq), BF16),
                        pltpu.VMEM((2, qb, LANES), F32)],
        compiler_params=_cparams("arbitrary", "arbitrary"),
        name="mla_prompt",
    )(q, k, v)


MLA_CACHE_ROWS = 512


def _mla_sample_kernel(q_ref, ckv_ref, kpe_ref, kn_ref, vn_ref, place_ref, wk_ref, kseg_ref, qabs_ref, wv_ref,
                       o_ref, ckvb_scr, kpeb_scr, sq_scr, s1_scr, s2_scr, e_scr):
    nq, past = q_ref.shape[1], ckv_ref.shape[1]
    lane_blocks = past // LANES
    lo = lax.broadcasted_iota(jnp.int32, (nq, LANES), 1) < D_V
    head = lambda h: slice(h * LANES, (h + 1) * LANES)

    for r0 in range(0, past, MLA_CACHE_ROWS):
        rows = slice(r0, r0 + MLA_CACHE_ROWS)
        c = ckv_ref[0, rows, :].astype(BF16)
        ckvb_scr[rows, :] = c
        kpeb_scr[rows, :] = _dot(kpe_ref[0, rows, :].astype(BF16), place_ref[...]).astype(BF16)
        kraw = _dot(c, wk_ref[...])
        sq_scr[rows, :] = (kraw * kraw).astype(BF16)
    r_t = lax.rsqrt(lax.dot_general(kseg_ref[...], sq_scr[...], _NT, preferred_element_type=F32) + EPS)

    group = D_HEADS // 2
    inv_den, e_new = {}, {}

    def scores(g):
        heads = range(g * group, (g + 1) * group)
        rows = slice(g * group * nq, (g + 1) * group * nq)
        q_abs = jnp.concatenate([_dot(q_ref[0, :, head(h)], qabs_ref[h]).astype(BF16) for h in heads], axis=0)
        q_all = jnp.concatenate([q_ref[0, :, head(h)] for h in heads], axis=0)
        s1_scr[rows, :] = lax.dot_general(q_abs, ckvb_scr[...], _NT, preferred_element_type=F32)
        s2_scr[rows, :] = lax.dot_general(q_all, kpeb_scr[...], _NT, preferred_element_type=F32)

    def softmax(g):
        for h in range(g * group, (g + 1) * group):
            rows = slice(h * nq, (h + 1) * nq)
            s_new = lax.dot_general(q_ref[0, :, head(h)], kn_ref[0, :, head(h)], _NT, preferred_element_type=F32)
            mm = None
            for c in range(lane_blocks):
                cl = slice(c * LANES, (c + 1) * LANES)
                blk = s1_scr[rows, cl] * r_t[h:h + 1, cl] + s2_scr[rows, cl]
                s1_scr[rows, cl] = blk
                mm = blk if mm is None else jnp.maximum(mm, blk)
            m = jnp.maximum(jnp.max(mm, axis=-1, keepdims=True), jnp.max(s_new, axis=-1, keepdims=True))
            acc = None
            for c in range(lane_blocks):
                cl = slice(c * LANES, (c + 1) * LANES)
                e = jnp.exp2(s1_scr[rows, cl] - m)
                acc = e if acc is None else acc + e
                e_scr[rows, cl] = e.astype(BF16)
            en = jnp.exp2(s_new - m)
            inv_den[h] = 1.0 / (jnp.sum(acc, axis=-1, keepdims=True) + jnp.sum(en, axis=-1, keepdims=True))
            e_new[h] = en.astype(BF16)

    def values(g):
        rows = slice(g * group * nq, (g + 1) * group * nq)
        latent = _dot(e_scr[rows, :], ckvb_scr[...])
        for p in range(g * group // 2, (g + 1) * group // 2):
            halves = []
            for half in range(2):
                h = 2 * p + half
                local = slice((h - g * group) * nq, (h - g * group + 1) * nq)
                o = _dot(latent[local].astype(BF16), wv_ref[h]) + _dot(e_new[h], vn_ref[0, :, head(p)])
                halves.append(o * inv_den[h])
            o_ref[0, :, head(p)] = jnp.where(lo, halves[0], halves[1]).astype(BF16)

    scores(0)
    scores(1)
    softmax(0)
    values(0)
    softmax(1)
    values(1)


def _mla_sample(q, ckv, kpe, kn, vn, ow):
    nb, nq, _ = q.shape
    past = ckv.shape[1]
    assert past % MLA_CACHE_ROWS == 0
    per_b = lambda r, w: pl.BlockSpec((1, r, w), lambda b: (b, 0, 0))
    consts = [ow["place"], ow["wk_dense"], ow["kseg_t"], ow["q_absorb"], ow["wv_lanes"]]
    rows = D_HEADS * nq
    return pl.pallas_call(
        _mla_sample_kernel,
        grid=(nb,),
        in_specs=[per_b(nq, D_HEADS * LANES), per_b(past, D_KV_LORA), per_b(past, D_ROPE),
                  per_b(nq, D_HEADS * LANES), per_b(nq, D_HEADS * D_V)] + [_resident(a.shape) for a in consts],
        out_specs=per_b(nq, D_HEADS * D_V),
        out_shape=jax.ShapeDtypeStruct((nb, nq, D_HEADS * D_V), BF16),
        scratch_shapes=[pltpu.VMEM((past, D_KV_LORA), BF16), pltpu.VMEM((past, LANES), BF16),
                        pltpu.VMEM((past, D_HEADS * D_NOPE), BF16), pltpu.VMEM((rows, past), F32),
                        pltpu.VMEM((rows, past), F32), pltpu.VMEM((rows, past), BF16)],
        compiler_params=_cparams("arbitrary"),
        name="mla_sample",
    )(q, ckv, kpe, kn, vn, *consts)


def _even_layer(xp, xs, nb, seq, ndb, dseq, norm_g, w_in, w_out, a_qn, a_kn, a_sinks, b_qn, b_kn, b_rel,
                t5_table, ck_a, cv_a, ck_b, cv_b, ffn2):
    w_in = _to_bf16(w_in)
    w_out = _to_bf16(w_out)
    w_in_p = jnp.concatenate([_pair_heads(w_in[:, :A_Q], 1), w_in[:, A_Q:]], axis=1)
    woa = _pair_heads(w_out[:A_Q], 0)
    wob = w_out[A_Q:]
    ones = lambda n: jnp.ones((n,), F32)
    scale = HEAD_DIM ** -0.5 * LOG2E
    gain_row = jnp.concatenate([jnp.tile(a_qn, A_HEADS) * scale, jnp.tile(a_kn, A_KV_HEADS), ones(A_KV),
                                jnp.tile(b_qn, B_HEADS) * scale, jnp.tile(b_kn, B_HEADS), ones(B_QKV)]
                               ).reshape(1, EVEN_IN).astype(F32)
    bias_a = _bias_a(t5_table) * LOG2E
    bias_b = _bias_b(b_rel) * LOG2E
    sink = jnp.broadcast_to(jnp.repeat(a_sinks.astype(F32) * LOG2E, PAIR_ROWS)[:, None],
                            (A_HEADS * PAIR_ROWS, LANES))
    la, lb = min(WINDOW, seq), min(B_REACH, seq)

    aq, ak, av, bq, bk, bv, cak, cav, cbk, cbv = _even_inproj(xp, norm_g, w_in_p, gain_row, nb, la, lb)
    r3 = lambda a: a.reshape(nb, seq, a.shape[-1])
    ya = _attn_a(r3(aq), r3(ak), r3(av), bias_a, sink, WINDOW, True)
    yb = _attn_b(r3(bq), r3(bk), r3(bv), bias_b, B_REACH, True)
    yap, ybp = ya.reshape(nb * seq, A_Q), yb.reshape(nb * seq, B_QKV)
    st_p = (cak.reshape(nb, la, A_KV_HEADS, HEAD_DIM), cav.reshape(nb, la, A_KV_HEADS, HEAD_DIM),
            cbk.reshape(nb, lb, B_HEADS, HEAD_DIM), cbv.reshape(nb, lb, B_HEADS, HEAD_DIM))

    ts = ndb * dseq
    aq, ak, av, bq, bk, bv, nak, nav, nbk, nbv = _even_inproj(xs, norm_g, w_in_p, gain_row, 1, ts, ts)
    pad_q = lambda a, rows: jnp.pad(a.reshape(ndb, dseq, a.shape[-1]), ((0, 0), (0, rows - dseq), (0, 0)))

    def window(cache, new, rows):
        w = cache.shape[-2] * cache.shape[-1]
        full = jnp.concatenate([cache.reshape(ndb, -1, w), new.reshape(ndb, dseq, w)], axis=1)
        buf = jnp.pad(full, ((0, 0), (0, rows - dseq), (0, 0))).astype(BF16)
        return full[:, dseq:].reshape(cache.shape), buf

    st_ak, kbuf_a = window(ck_a, nak, PAIR_ROWS)
    st_av, vbuf_a = window(cv_a, nav, PAIR_ROWS)
    st_bk, kbuf_b = window(ck_b, nbk, B_STEP_ROWS)
    st_bv, vbuf_b = window(cv_b, nbv, B_STEP_ROWS)
    ya = _attn_a(pad_q(aq, PAIR_ROWS), kbuf_a, vbuf_a, bias_a, sink, WINDOW, False)[:, :dseq]
    yb = _attn_b(pad_q(bq, B_STEP_ROWS), kbuf_b, vbuf_b, bias_b, B_REACH, False)[:, :dseq]
    xp, xs = _ffn(xp, xs, *ffn2, out_proj=(yap, ybp, ya.reshape(ts, A_Q), yb.reshape(ts, B_QKV), woa, wob))
    return xp, xs, st_p, (st_ak, st_av, st_bk, st_bv)


def _odd_layer(xp, xs, nb, seq, ndb, dseq, past, norm_g, ow, w_out, conv_prev, c_ckv, c_kpe, ffn2):
    w_out = _to_bf16(w_out)
    woc = w_out[:C_WIDTH]
    wod = w_out[C_WIDTH:]

    zero_init = jnp.zeros((nb, 8, C_WIDTH), F32)
    yc, q, k, v, ckv, kpe, cs = _odd_inproj(xp, norm_g, ow, zero_init, nb, 0)
    r3 = lambda a: a.reshape(nb, seq, a.shape[-1])
    ycp, ydp = yc, _mla_prompt(r3(q), r3(k), r3(v)).reshape(nb * seq, D_HEADS * D_V)
    st_p = (cs, ckv.reshape(nb, seq, D_KV_LORA), kpe.reshape(nb, seq, D_ROPE))

    ts = ndb * dseq
    init = jnp.pad(conv_prev.astype(F32), ((0, 0), (8 - (CONV_W - 1), 0), (0, 0)))
    yc, q, kn, vn, ckv, kpe, cs = _odd_inproj(xs, norm_g, ow, init, ndb, past)
    s3 = lambda a: a.reshape(ndb, dseq, a.shape[-1])
    yd = _mla_sample(s3(q), c_ckv, c_kpe, s3(kn), s3(vn), ow)
    xp, xs = _ffn(xp, xs, *ffn2, out_proj=(ycp, ydp, yc, yd.reshape(ts, D_HEADS * D_V), woc, wod))
    st_s = (cs, ckv.reshape(ndb, dseq, D_KV_LORA), kpe.reshape(ndb, dseq, D_ROPE))
    return xp, xs, st_p, st_s


def kernel(x_prompt, x_sample, cache_a_k, cache_a_v, cache_b_k, cache_b_v, state_c_conv, cache_d_ckv, cache_d_kpe, ff1_norm, ff1_w_gu, ff1_w_down, mix_norm, ff2_norm, ff2_w_gu, ff2_w_down, t5_bias_table, ev_w_in, ev_w_out, a_q_norm, a_k_norm, a_sinks, b_q_norm, b_k_norm, b_rel_bias, od_w_in, od_w_out, c_conv_w, d_q_a_norm, d_w_q_b, d_kv_a_norm, d_w_kv_b, d_q_nope_norm, d_q_rope_norm, d_k_nope_norm, d_k_rope_norm):
    nb, seq, _ = x_prompt.shape
    ndb, dseq, _ = x_sample.shape
    past = cache_d_ckv.shape[2]
    depth = ff1_norm.shape[0]
    assert seq % TOKEN_TILE == 0 and dseq == CHUNK and past % CHUNK == 0
    xp = x_prompt.reshape(nb * seq, D_MODEL)
    xs = x_sample.reshape(ndb * dseq, D_MODEL)
    even_p, even_s, odd_p, odd_s = [], [], [], []
    for l in range(depth):
        i = l // 2
        ffn2 = (ff2_norm[l], ff2_w_gu, ff2_w_down, l)
        xp, xs = _ffn(xp, xs, ff1_norm[l], ff1_w_gu, ff1_w_down, l)
        if l % 2 == 0:
            xp, xs, sp, ss = _even_layer(
                xp, xs, nb, seq, ndb, dseq, mix_norm[l], ev_w_in[i], ev_w_out[i], a_q_norm[i], a_k_norm[i],
                a_sinks[i], b_q_norm[i], b_k_norm[i], b_rel_bias[i], t5_bias_table,
                cache_a_k[i], cache_a_v[i], cache_b_k[i], cache_b_v[i], ffn2)
            even_p.append(sp)
            even_s.append(ss)
        else:
            ow = _odd_weights(od_w_in[i], c_conv_w[i], d_q_a_norm[i], d_w_q_b[i], d_kv_a_norm[i], d_w_kv_b[i],
                              d_q_nope_norm[i], d_q_rope_norm[i], d_k_nope_norm[i], d_k_rope_norm[i])
            xp, xs, sp, ss = _odd_layer(xp, xs, nb, seq, ndb, dseq, past, mix_norm[l], ow, od_w_out[i],
                                        state_c_conv[i], cache_d_ckv[i], cache_d_kpe[i], ffn2)
            odd_p.append(sp)
            odd_s.append(ss)
    stack = lambda group, j: group[0][j][None] if len(group) == 1 else jnp.stack([g[j] for g in group])
    return (xp.reshape(nb, seq, D_MODEL), xs.reshape(ndb, dseq, D_MODEL),
            stack(even_p, 0), stack(even_p, 1), stack(even_p, 2), stack(even_p, 3),
            stack(odd_p, 0), stack(odd_p, 1), stack(odd_p, 2),
            stack(even_s, 0), stack(even_s, 1), stack(even_s, 2), stack(even_s, 3),
            stack(odd_s, 0), stack(odd_s, 1), stack(odd_s, 2))
```

```python
import functools
import math

import jax
import jax.numpy as jnp
from jax import lax
from jax.experimental import pallas as pl
from jax.experimental.pallas import tpu as pltpu

F32 = jnp.float32
BF16 = jnp.bfloat16

D_MODEL = 1024
CHUNK = 64
HEAD_DIM = 64
EPS = 1e-6
A_HEADS = 8
A_KV_HEADS = 2
WINDOW = 128
T5_BUCKETS = 32
T5_MAX_DIST = 128
B_HEADS = 8
B_REACH = 512
B_MAX_REL = 128
C_WIDTH = 512
CONV_W = 3
D_HEADS = 8
D_Q_LORA = 256
D_KV_LORA = 128
D_NOPE = 64
D_ROPE = 32
D_V = 64
ROPE_THETA = 10000.0
FFN_DIM = 2816
A_Q = A_HEADS * HEAD_DIM
A_KV = A_KV_HEADS * HEAD_DIM
B_QKV = B_HEADS * HEAD_DIM
EVEN_IN = A_Q + 2 * A_KV + 3 * B_QKV

LANES = 128
TOKEN_TILE = 512
CAST_BLOCK_BYTES = 6 * 1024 * 1024
MXU_COLS = 256
FFN_CHUNK = MXU_COLS
PAIR_ROWS = 2 * CHUNK
MLA_QBLOCK = 512
VMEM_LIMIT_BYTES = 56 * 1024 * 1024
LOG2E = math.log2(math.e)
MLA_QSCALE = (D_NOPE + D_ROPE) ** -0.5 * LOG2E
NEG_INF = float("-inf")

_NT = (((1,), (1,)), ((), ()))


def _cparams(*sem):
    return pltpu.CompilerParams(dimension_semantics=sem, vmem_limit_bytes=VMEM_LIMIT_BYTES)


def _resident(shape):
    zeros = (0,) * len(shape)
    return pl.BlockSpec(shape, lambda *_: zeros, pipeline_mode=pl.Buffered(1))


def _rms(x, g):
    ms = jnp.mean(x * x, axis=-1, keepdims=True)
    return (x * lax.rsqrt(ms + EPS)) * g


def _dot(a, b):
    return jnp.dot(a, b, preferred_element_type=F32)


def _weight_stream(wgu_hbm, wd_hbm, wgu_ref, wd_ref, stage, sem):
    n_chunks = FFN_DIM // FFN_CHUNK
    g_stage, u_stage, d_stage = stage

    def copies(j):
        cols = pl.ds(j * FFN_CHUNK, FFN_CHUNK)
        slot = j % 2
        return (pltpu.make_async_copy(wgu_hbm.at[:, cols], g_stage.at[slot], sem.at[0, slot]),
                pltpu.make_async_copy(wgu_hbm.at[:, pl.ds(FFN_DIM + j * FFN_CHUNK, FFN_CHUNK)],
                                      u_stage.at[slot], sem.at[1, slot]),
                pltpu.make_async_copy(wd_hbm.at[cols, :], d_stage.at[slot], sem.at[2, slot]))

    def fetch(j):
        if j == 0:
            for cp in copies(0):
                cp.start()
        if j + 1 < n_chunks:
            for cp in copies(j + 1):
                cp.start()
        for cp in copies(j):
            cp.wait()
        slot = j % 2
        wgu_ref[:, j * FFN_CHUNK:(j + 1) * FFN_CHUNK] = g_stage[slot].astype(BF16)
        wgu_ref[:, FFN_DIM + j * FFN_CHUNK:FFN_DIM + (j + 1) * FFN_CHUNK] = u_stage[slot].astype(BF16)
        wd_ref[j * FFN_CHUNK:(j + 1) * FFN_CHUNK, :] = d_stage[slot].astype(BF16)

    return fetch


def _ffn_tile(x_ref, y_refs, wo_refs, g_ref, wgu_ref, wd_ref, o_ref, h_scr, a_scr, n_chunks, fetch=None):
    rows = x_ref.shape[0]
    if rows < h_scr.shape[0]:
        h_scr, a_scr = h_scr.at[pl.ds(0, rows)], a_scr.at[pl.ds(0, rows)]
    if y_refs:
        (ya_ref, yb_ref), (woa_ref, wob_ref) = y_refs, wo_refs
        y = _dot(ya_ref[...], woa_ref[...]) + _dot(yb_ref[...], wob_ref[...])
        o_ref[...] = x_ref[...] + y
        res_ref = o_ref
    else:
        res_ref = x_ref
    h_scr[...] = _rms(res_ref[...], g_ref[...]).astype(BF16)
    for j in range(n_chunks):
        if fetch is not None:
            fetch(j)
        h = h_scr[...]
        g = _dot(h, wgu_ref[:, j * FFN_CHUNK:(j + 1) * FFN_CHUNK])
        u = _dot(h, wgu_ref[:, FFN_DIM + j * FFN_CHUNK:FFN_DIM + (j + 1) * FFN_CHUNK])
        a_scr[:, j * FFN_CHUNK:(j + 1) * FFN_CHUNK] = ((g * jax.nn.sigmoid(g)) * u).astype(BF16)
    o_ref[...] = res_ref[...] + 0.5 * _dot(a_scr[...], wd_ref[...])


def _ffn_kernel(*refs, n_chunks, fused_out, layer, n_main):
    if fused_out:
        (xp_ref, yap_ref, ybp_ref, xs_ref, yas_ref, ybs_ref, woa_ref, wob_ref, g_ref, wgu_hbm, wd_hbm,
         op_ref, os_ref, h_scr, a_scr, wgu_ref, wd_ref, g_stage, u_stage, d_stage, sem) = refs
        yp, ys, wo = (yap_ref, ybp_ref), (yas_ref, ybs_ref), (woa_ref, wob_ref)
    else:
        (xp_ref, xs_ref, g_ref, wgu_hbm, wd_hbm,
         op_ref, os_ref, h_scr, a_scr, wgu_ref, wd_ref, g_stage, u_stage, d_stage, sem) = refs
        yp = ys = wo = None
    i = pl.program_id(0)
    tile = functools.partial(_ffn_tile, g_ref=g_ref, wgu_ref=wgu_ref, wd_ref=wd_ref, h_scr=h_scr, a_scr=a_scr,
                             n_chunks=n_chunks)

    @pl.when(i == 0)
    def _():
        fetch = _weight_stream(wgu_hbm.at[layer], wd_hbm.at[layer], wgu_ref, wd_ref,
                               (g_stage, u_stage, d_stage), sem)
        tile(xp_ref, yp, wo, o_ref=op_ref, fetch=fetch)

    @pl.when((i > 0) & (i < n_main))
    def _():
        tile(xp_ref, yp, wo, o_ref=op_ref)

    @pl.when(i == n_main)
    def _():
        tile(xs_ref, ys, wo, o_ref=os_ref)


def _ffn(xp, xs, norm_g, w_gu, w_down, layer, out_proj=None):
    tp, ts = xp.shape[0], xs.shape[0]
    tm = min(TOKEN_TILE, tp)
    n_main = tp // tm
    n_chunks = FFN_DIM // FFN_CHUNK
    assert n_main * tm == tp and ts <= tm and ts % 16 == 0
    main = lambda w: pl.BlockSpec((tm, w), lambda i: (jnp.minimum(i, n_main - 1), 0))
    side_in = lambda w: pl.BlockSpec((ts, w), lambda i: (0, 0), pipeline_mode=pl.Buffered(1))
    side_out = lambda w: pl.BlockSpec((ts, w), lambda i: (0, 0))
    in_hbm = pl.BlockSpec(memory_space=pl.ANY)
    if out_proj is not None:
        yap, ybp, yas, ybs, woa, wob = out_proj
        in_specs = [main(D_MODEL), main(yap.shape[1]), main(ybp.shape[1]),
                    side_in(D_MODEL), side_in(yas.shape[1]), side_in(ybs.shape[1]),
                    _resident(woa.shape), _resident(wob.shape)]
        args = [xp, yap, ybp, xs, yas, ybs, woa, wob]
    else:
        in_specs = [main(D_MODEL), side_in(D_MODEL)]
        args = [xp, xs]
    in_specs += [_resident((1, D_MODEL)), in_hbm, in_hbm]
    args += [norm_g.reshape(1, D_MODEL), w_gu, w_down]
    return pl.pallas_call(
        functools.partial(_ffn_kernel, n_chunks=n_chunks, fused_out=out_proj is not None, layer=layer,
                          n_main=n_main),
        grid=(n_main + 1,),
        in_specs=in_specs,
        out_specs=[main(D_MODEL), side_out(D_MODEL)],
        out_shape=[jax.ShapeDtypeStruct((tp, D_MODEL), F32), jax.ShapeDtypeStruct((ts, D_MODEL), F32)],
        scratch_shapes=[pltpu.VMEM((tm, D_MODEL), BF16), pltpu.VMEM((tm, FFN_DIM), BF16),
                        pltpu.VMEM((D_MODEL, 2 * FFN_DIM), BF16), pltpu.VMEM((FFN_DIM, D_MODEL), BF16),
                        pltpu.VMEM((2, D_MODEL, FFN_CHUNK), F32), pltpu.VMEM((2, D_MODEL, FFN_CHUNK), F32),
                        pltpu.VMEM((2, FFN_CHUNK, D_MODEL), F32),
                        pltpu.SemaphoreType.DMA((3, 2))],
        compiler_params=_cparams("arbitrary"),
        name="ffn_out" if out_proj is not None else "ffn",
    )(*args)


def _cast_kernel(x_ref, o_ref):
    o_ref[...] = x_ref[...].astype(BF16)


def _to_bf16(w, layer=None):
    r, c = w.shape[-2:]
    fits = [d for d in range(16, r + 1, 16) if r % d == 0 and d * c * 4 <= CAST_BLOCK_BYTES]
    tr = max(fits) if fits else r
    out_spec = pl.BlockSpec((tr, c), lambda i: (i, 0))
    in_spec = out_spec if layer is None else pl.BlockSpec((None, tr, c), lambda i: (layer, i, 0))
    return pl.pallas_call(
        _cast_kernel, grid=(r // tr,), in_specs=[in_spec], out_specs=out_spec,
        out_shape=jax.ShapeDtypeStruct((r, c), BF16),
        compiler_params=_cparams("arbitrary"), name="cast_bf16",
    )(w)


def _norm_halves(blk, gain):
    sq = blk * blk
    lo = lax.broadcasted_iota(jnp.int32, blk.shape, 1) < HEAD_DIM
    s_lo = jnp.sum(jnp.where(lo, sq, 0.0), axis=-1, keepdims=True)
    s_hi = jnp.sum(jnp.where(lo, 0.0, sq), axis=-1, keepdims=True)
    inv = jnp.where(lo, lax.rsqrt(s_lo * (1.0 / HEAD_DIM) + EPS), lax.rsqrt(s_hi * (1.0 / HEAD_DIM) + EPS))
    return (blk * inv) * gain


def _even_inproj_kernel(x_ref, g_ref, w_ref, gain_ref,
                        aq_ref, ak_ref, av_ref, bq_ref, bk_ref, bv_ref,
                        cak_ref, cav_ref, cbk_ref, cbv_ref, h_scr, z_scr, *, tm, tpb, rows_a, rows_b):
    i = pl.program_id(0)
    last = (i % tpb) == (tpb - 1)
    h_scr[...] = _rms(x_ref[...], g_ref[...]).astype(BF16)

    def lanes(c):
        return slice(c * LANES, (c + 1) * LANES)

    def emit(c):
        blk = z_scr[:, lanes(c)]
        gain = gain_ref[:, lanes(c)]
        if c < 4:
            aq_ref[:, lanes(c)] = _norm_halves(blk, gain).astype(BF16)
        elif c == 4:
            kn = _norm_halves(blk, gain)
            ak_ref[...] = kn.astype(BF16)
            z_scr[:, lanes(c)] = kn
        elif c == 5:
            av_ref[...] = blk.astype(BF16)
        elif c < 10:
            bq_ref[:, lanes(c - 6)] = _norm_halves(blk, gain).astype(BF16)
        elif c < 14:
            kn = _norm_halves(blk, gain)
            bk_ref[:, lanes(c - 10)] = kn.astype(BF16)
            z_scr[:, lanes(c)] = kn
        else:
            bv_ref[:, lanes(c - 14)] = blk.astype(BF16)

    for grp in range(EVEN_IN // MXU_COLS):
        cols = slice(grp * MXU_COLS, (grp + 1) * MXU_COLS)
        z_scr[:, cols] = _dot(h_scr[...], w_ref[:, cols])
        emit(2 * grp)
        emit(2 * grp + 1)

    @pl.when(last)
    def _():
        cak_ref[0] = z_scr[tm - rows_a:, lanes(4)]
        cav_ref[0] = z_scr[tm - rows_a:, lanes(5)]
        cbk_ref[0] = z_scr[tm - rows_b:, 10 * LANES:14 * LANES]
        cbv_ref[0] = z_scr[tm - rows_b:, 14 * LANES:18 * LANES]


def _even_inproj(x, norm_g, w_in, gain_row, n_batch, rows_a, rows_b):
    t = x.shape[0]
    tm = min(TOKEN_TILE, t)
    tpb = t // n_batch // tm
    assert tpb * tm * n_batch == t and rows_a <= tm and rows_b <= tm
    row = lambda w: pl.BlockSpec((tm, w), lambda i: (i, 0))
    cache = lambda r, w: pl.BlockSpec((1, r, w), lambda i: (i // tpb, 0, 0))
    bshape = lambda w: jax.ShapeDtypeStruct((t, w), BF16)
    cshape = lambda r, w: jax.ShapeDtypeStruct((n_batch, r, w), F32)
    return pl.pallas_call(
        functools.partial(_even_inproj_kernel, tm=tm, tpb=tpb, rows_a=rows_a, rows_b=rows_b),
        grid=(t // tm,),
        in_specs=[row(D_MODEL), _resident((1, D_MODEL)), _resident(w_in.shape), _resident(gain_row.shape)],
        out_specs=[row(A_Q), row(A_KV), row(A_KV), row(B_QKV), row(B_QKV), row(B_QKV),
                   cache(rows_a, A_KV), cache(rows_a, A_KV), cache(rows_b, B_QKV), cache(rows_b, B_QKV)],
        out_shape=[bshape(A_Q), bshape(A_KV), bshape(A_KV), bshape(B_QKV), bshape(B_QKV), bshape(B_QKV),
                   cshape(rows_a, A_KV), cshape(rows_a, A_KV), cshape(rows_b, B_QKV), cshape(rows_b, B_QKV)],
        scratch_shapes=[pltpu.VMEM((tm, D_MODEL), BF16), pltpu.VMEM((tm, EVEN_IN), F32)],
        compiler_params=_cparams("arbitrary"),
        name="even_inproj",
    )(x, norm_g.reshape(1, D_MODEL), w_in, gain_row)


def _pair_heads(a, axis):
    shape = a.shape
    split = shape[:axis] + (2, A_HEADS // 2, HEAD_DIM) + shape[axis + 1:]
    return a.reshape(split).swapaxes(axis, axis + 1).reshape(shape)


def _fill_padded(buf, src_ref, ctx):
    buf[0:ctx, :] = jnp.zeros((ctx, buf.shape[1]), buf.dtype)
    buf[ctx:, :] = src_ref[0]


def _windows(k_ref, v_ref, pad_scr, ctx, pad_front, step_rows, stream):
    win = ctx + step_rows
    if pad_front:
        kbuf, vbuf = pad_scr
        _fill_padded(kbuf, k_ref, ctx)
        _fill_padded(vbuf, v_ref, ctx)
        return (lambda r0: kbuf[pl.ds(r0, win), :]), (lambda r0: vbuf[pl.ds(r0, win), :])
    return (lambda r0: k_ref[stream, pl.ds(r0, win), :]), (lambda r0: v_ref[stream, pl.ds(r0, win), :])


def _fold_lanes(x, op):
    acc = x[:, :LANES]
    for c in range(1, x.shape[1] // LANES):
        acc = op(acc, x[:, c * LANES:(c + 1) * LANES])
    return acc


def _row_stat(reduce_fn, x):
    return jnp.broadcast_to(reduce_fn(x, axis=-1, keepdims=True), x.shape)


PAIR_SLOTS = 4


def _run_steps(make_stages, n_streams, n_steps, n_front, step_rows, lookahead):
    stages = [make_stages(stream) for stream in range(n_streams)]
    tasks = [(stream, t) for stream in range(n_streams) for t in range(n_steps)]

    def run(stage, k):
        stream, t = tasks[k]
        stages[stream][stage](t * step_rows, k % PAIR_SLOTS, t < n_front)

    n = len(tasks)
    if not lookahead:
        for k in range(n):
            for stage in range(3):
                run(stage, k)
        return
    run(0, 0)
    if n > 1:
        run(0, 1)
    run(1, 0)
    for k in range(n):
        if k + 2 < n:
            run(0, k + 2)
        if k + 1 < n:
            run(1, k + 1)
        run(2, k)


def _band_bias(ext_row, n_rows, ctx):
    win = ctx + n_rows
    period = ext_row.shape[1]
    t = pltpu.roll(jnp.broadcast_to(ext_row, (n_rows, period)), 0, 1, stride=1, stride_axis=0)[:, :win]
    chunk = lax.broadcasted_iota(jnp.int32, (n_rows, win), 0) // CHUNK
    col = lax.broadcasted_iota(jnp.int32, (n_rows, win), 1)
    visible = (col >= CHUNK * chunk) & (col < ctx + CHUNK * (chunk + 1))
    return jnp.where(visible, t, NEG_INF)


def _attn_a_kernel(q_ref, k_ref, v_ref, ext_ref, sink_ref, o_ref, bias_ref, s_scr, e_scr, *pad_scr,
                   nq, ctx, pad_front):
    win = ctx + PAIR_ROWS
    lo = lax.broadcasted_iota(jnp.int32, (PAIR_ROWS, LANES), 1) < HEAD_DIM

    @pl.when(pl.program_id(0) == 0)
    def _():
        for head in range(A_HEADS):
            bias_ref[head * PAIR_ROWS:(head + 1) * PAIR_ROWS, :] = _band_bias(
                ext_ref[head:head + 1, :], PAIR_ROWS, ctx)

    def make_stages(stream):
        kwin, vwin = _windows(k_ref, v_ref, pad_scr, ctx, pad_front, PAIR_ROWS, stream)
        return (functools.partial(scores, stream, kwin), functools.partial(softmax, stream),
                functools.partial(values, stream, vwin))

    def scores(stream, kwin, r0, slot, masked):
        q_lo, q_hi = [], []
        for p in range(4):
            qp = q_ref[stream, pl.ds(r0, PAIR_ROWS), p * LANES:(p + 1) * LANES]
            zero = jnp.zeros_like(qp)
            q_lo.append(jnp.where(lo, qp, zero))
            q_hi.append(jnp.where(lo, zero, qp))
        q2 = jnp.concatenate(q_lo + q_hi, axis=0)
        s_scr[slot] = lax.dot_general(q2, kwin(r0), _NT, preferred_element_type=F32)

    def softmax(stream, r0, slot, masked):
        for head in range(A_HEADS):
            rows = slice(head * PAIR_ROWS, (head + 1) * PAIR_ROWS)
            s = s_scr[slot, rows, :] + bias_ref[rows, :]
            if masked:
                col = lax.broadcasted_iota(jnp.int32, (PAIR_ROWS, win), 1)
                s = jnp.where(col + r0 >= ctx, s, NEG_INF)
            sk = sink_ref[rows, :]
            m = jnp.maximum(_row_stat(jnp.max, _fold_lanes(s, jnp.maximum)), sk)
            e = [jnp.exp2(s[:, c * LANES:(c + 1) * LANES] - m) for c in range(win // LANES)]
            den = _row_stat(jnp.sum, functools.reduce(jnp.add, e)) + jnp.exp2(sk - m)
            inv = 1.0 / den
            for c in range(win // LANES):
                e_scr[slot, rows, c * LANES:(c + 1) * LANES] = (e[c] * inv).astype(BF16)

    def values(stream, vwin, r0, slot, masked):
        o2 = _dot(e_scr[slot], vwin(r0))
        half = 4 * PAIR_ROWS
        for p in range(4):
            o = jnp.where(lo, o2[p * PAIR_ROWS:(p + 1) * PAIR_ROWS],
                          o2[half + p * PAIR_ROWS:half + (p + 1) * PAIR_ROWS])
            o_ref[stream, pl.ds(r0, PAIR_ROWS), p * LANES:(p + 1) * LANES] = o.astype(BF16)

    _run_steps(make_stages, q_ref.shape[0], nq // PAIR_ROWS, ctx // PAIR_ROWS if pad_front else 0, PAIR_ROWS,
               lookahead=False)


def _streams_per_block(nb, pad_front):
    return 1 if pad_front else nb


def _attn_a(q, k, v, ext, sink, ctx, pad_front):
    nb, nq, _ = q.shape
    nk = k.shape[1]
    assert nq % PAIR_ROWS == 0 and nk == (nq if pad_front else ctx + nq)
    bb = _streams_per_block(nb, pad_front)
    per_b = lambda r, w: pl.BlockSpec((bb, r, w), lambda b: (b, 0, 0))
    rows, win = A_HEADS * PAIR_ROWS, ctx + PAIR_ROWS
    scratch = [pltpu.VMEM((rows, win), F32),
               pltpu.VMEM((PAIR_SLOTS, rows, win), F32), pltpu.VMEM((PAIR_SLOTS, rows, win), BF16)]
    if pad_front:
        scratch += [pltpu.VMEM((ctx + nq, A_KV), BF16)] * 2
    return pl.pallas_call(
        functools.partial(_attn_a_kernel, nq=nq, ctx=ctx, pad_front=pad_front),
        grid=(nb // bb,),
        in_specs=[per_b(nq, A_Q), per_b(nk, A_KV), per_b(nk, A_KV), _resident(ext.shape), _resident(sink.shape)],
        out_specs=per_b(nq, A_Q),
        out_shape=jax.ShapeDtypeStruct((nb, nq, A_Q), BF16),
        scratch_shapes=scratch,
        compiler_params=_cparams("arbitrary"),
        name="attn_a",
    )(q, k, v, ext, sink)


B_STEP_ROWS = 4 * CHUNK


def _attn_b_kernel(q_ref, k_ref, v_ref, ext_ref, o_ref, bias_scr, s_scr, e_scr, r_scr, *pad_scr,
                   nq, ctx, pad_front):
    win = ctx + B_STEP_ROWS
    lo = lax.broadcasted_iota(jnp.int32, (B_STEP_ROWS, LANES), 1) < HEAD_DIM
    chunks = B_STEP_ROWS // CHUNK
    pair_idx = pl.program_id(1)

    @pl.when(pl.program_id(0) == 0)
    def _():
        for half in range(2):
            bias_scr[pair_idx, half * B_STEP_ROWS:(half + 1) * B_STEP_ROWS, :] = _band_bias(
                ext_ref[0, half:half + 1, :], B_STEP_ROWS, ctx)

    bias_ref = bias_scr.at[pair_idx]

    def make_stages(stream):
        kwin, vwin = _windows(k_ref, v_ref, pad_scr, ctx, pad_front, B_STEP_ROWS, stream)
        return (functools.partial(scores, stream, kwin), functools.partial(softmax, stream),
                functools.partial(values, stream, vwin))

    def scores(stream, kwin, r0, slot, masked):
        qp = q_ref[stream, pl.ds(r0, B_STEP_ROWS), :]
        zero = jnp.zeros_like(qp)
        q2 = jnp.concatenate([jnp.where(lo, qp, zero), jnp.where(lo, zero, qp)], axis=0)
        s_scr[slot] = lax.dot_general(q2, kwin(r0), _NT, preferred_element_type=F32)

    def softmax(stream, r0, slot, masked):
        for rb in range(2 * chunks):
            rows = slice(rb * CHUNK, (rb + 1) * CHUNK)
            ci = rb % chunks
            c_lo, c_hi = CHUNK * ci // LANES, -(-(ctx + CHUNK * (ci + 1)) // LANES)
            band = slice(c_lo * LANES, c_hi * LANES)
            s = s_scr[slot, rows, band] + bias_ref[rows, band]
            if masked:
                col = lax.broadcasted_iota(jnp.int32, s.shape, 1) + c_lo * LANES
                s = jnp.where(col + r0 >= ctx, s, NEG_INF)
            m = _row_stat(jnp.max, _fold_lanes(s, jnp.maximum))
            e = [jnp.exp2(s[:, c * LANES:(c + 1) * LANES] - m) for c in range(c_hi - c_lo)]
            for c in range(win // LANES):
                blk = e[c - c_lo].astype(BF16) if c_lo <= c < c_hi else jnp.zeros((CHUNK, LANES), BF16)
                e_scr[slot, rows, c * LANES:(c + 1) * LANES] = blk
            r_scr[slot, rows, :] = 1.0 / _row_stat(jnp.sum, functools.reduce(jnp.add, e))

    def values(stream, vwin, r0, slot, masked):
        o2 = _dot(e_scr[slot], vwin(r0)) * r_scr[slot]
        o = jnp.where(lo, o2[:B_STEP_ROWS], o2[B_STEP_ROWS:])
        o_ref[stream, pl.ds(r0, B_STEP_ROWS), :] = o.astype(BF16)

    _run_steps(make_stages, q_ref.shape[0], nq // B_STEP_ROWS,
               -(-ctx // B_STEP_ROWS) if pad_front else 0, B_STEP_ROWS, lookahead=True)


def _attn_b(q, k, v, ext, ctx, pad_front):
    nb, nq, _ = q.shape
    nk = k.shape[1]
    assert nq % B_STEP_ROWS == 0 and nk == (nq if pad_front else ctx + nq)
    bb = _streams_per_block(nb, pad_front)
    blk = lambda r: pl.BlockSpec((bb, r, LANES), lambda b, p: (b, 0, p))
    rows, win = 2 * B_STEP_ROWS, ctx + B_STEP_ROWS
    scratch = [pltpu.VMEM((B_HEADS // 2, rows, win), F32),
               pltpu.VMEM((PAIR_SLOTS, rows, win), F32), pltpu.VMEM((PAIR_SLOTS, rows, win), BF16),
               pltpu.VMEM((PAIR_SLOTS, rows, LANES), F32)]
    if pad_front:
        scratch += [pltpu.VMEM((ctx + nq, LANES), BF16)] * 2
    return pl.pallas_call(
        functools.partial(_attn_b_kernel, nq=nq, ctx=ctx, pad_front=pad_front),
        grid=(nb // bb, B_HEADS // 2),
        in_specs=[blk(nq), blk(nk), blk(nk),
                  pl.BlockSpec((1,) + ext.shape[1:], lambda b, p: (p, 0, 0))],
        out_specs=blk(nq),
        out_shape=jax.ShapeDtypeStruct((nb, nq, B_QKV), BF16),
        scratch_shapes=scratch,
        compiler_params=_cparams("arbitrary", "arbitrary"),
        name="attn_b",
    )(q, k, v, ext)


def _t5_bucket(rel):
    nb = T5_BUCKETS // 2
    max_exact = nb // 2
    n = -rel
    ret = jnp.where(n < 0, nb, 0)
    n = jnp.abs(n)
    nf = jnp.maximum(n, 1).astype(F32)
    large = max_exact + (jnp.log(nf / max_exact) / math.log(T5_MAX_DIST / max_exact)
                         * (nb - max_exact)).astype(jnp.int32)
    large = jnp.minimum(large, nb - 1)
    return ret + jnp.where(n < max_exact, n, large)


def _band_values(ctx, n_rows, value_of_rel):
    n_cols = ctx + n_rows
    period = -(-(n_cols + n_rows - 1) // LANES) * LANES
    d = jnp.concatenate([jnp.arange(0, period - (n_rows - 1)), jnp.arange(-(n_rows - 1), 0)])
    return value_of_rel(d - ctx)


def _bias_a(t5_table):
    return _band_values(WINDOW, PAIR_ROWS, lambda rel: t5_table.astype(F32)[_t5_bucket(rel)].T)


def _bias_b(b_rel):
    ext = _band_values(B_REACH, B_STEP_ROWS,
                       lambda rel: b_rel.astype(F32)[:, jnp.clip(rel, -B_MAX_REL, B_MAX_REL) + B_MAX_REL])
    return ext.reshape(B_HEADS // 2, 2, ext.shape[-1])


ODD_IN_PAD = 2048
ROPE_HALF = D_ROPE // 2
X1_LO = LANES // 2
NOPE_SPLIT = X1_LO - ROPE_HALF


def _head_lane_source():
    zero = D_NOPE + D_ROPE
    src = []
    for lane in range(LANES):
        if lane < ROPE_HALF:
            src.append(D_NOPE + ROPE_HALF + lane)
        elif lane < X1_LO:
            src.append(lane - ROPE_HALF)
        elif lane < X1_LO + ROPE_HALF:
            src.append(D_NOPE + lane - X1_LO)
        elif lane < X1_LO + ROPE_HALF + D_NOPE - NOPE_SPLIT:
            src.append(NOPE_SPLIT + lane - X1_LO - ROPE_HALF)
        else:
            src.append(zero)
    return src


def _to_head_lanes(a):
    padded = jnp.concatenate([a, jnp.zeros(a.shape[:-1] + (1,), a.dtype)], axis=-1)
    return padded[..., jnp.asarray(_head_lane_source(), dtype=jnp.int32)]


def _lane_masks(shape):
    lane = lax.broadcasted_iota(jnp.int32, shape, 1)
    rope = (lane < ROPE_HALF) | ((lane >= X1_LO) & (lane < X1_LO + ROPE_HALF))
    used = lane < X1_LO + ROPE_HALF + D_NOPE - NOPE_SPLIT
    return used & ~rope, rope


def _rope(y, cosf, sinf):
    return y * cosf + pltpu.roll(y, LANES // 2, 1) * sinf


ROW_BLOCK = 128


def _row_blocks(tm):
    rb = min(ROW_BLOCK, tm)
    return [slice(r, r + rb) for r in range(0, tm, rb)]


def _segment_mean_squares(src_scr, cols, seg_ref, sq_scr, ms_scr, slot):
    for rows in _row_blocks(src_scr.shape[0]):
        v = src_scr[rows, cols]
        sq_scr[slot, rows, :] = (v * v).astype(BF16)
    ms_scr[slot] = _dot(sq_scr[slot], seg_ref[...])


def _expand_kv(ckvb_ref, kpe_blk_ref, wkv_ref, kgain_ref, kv_scr, k_ref, v_ref, between=None):
    k_cols = D_HEADS * LANES
    tm = kv_scr.shape[0]
    for grp in range(kv_scr.shape[1] // MXU_COLS):
        if between and grp in between:
            between[grp]()
        cols = slice(grp * MXU_COLS, (grp + 1) * MXU_COLS)
        kv_scr[:, cols] = _dot(ckvb_ref[...], wkv_ref[:, cols])
        for rows in _row_blocks(tm):
            if cols.start < k_cols:
                for half in range(2):
                    h = 2 * grp + half
                    kb = kv_scr[rows, h * LANES:(h + 1) * LANES]
                    ms = jnp.sum(kb * kb, axis=-1, keepdims=True) * (1.0 / D_NOPE)
                    kn = (kb * lax.rsqrt(ms + EPS)) * kgain_ref[...]
                    k_ref[rows, h * LANES:(h + 1) * LANES] = (kn + kpe_blk_ref[rows, :]).astype(BF16)
            else:
                v_ref[rows, cols.start - k_cols:cols.stop - k_cols] = kv_scr[rows, cols].astype(BF16)


def _odd_inproj_kernel(x_ref, g_ref, w_ref, cinit_ref, convw_ref, qan_ref, wqb_ref, qgain_ref,
                       kvan_ref, krgain_ref, invf_ref, sgn_ref, wkv_ref, kgain_ref, qseg_ref,
                       yc_ref, q_ref, k_ref, v_ref, ckv_ref, kpe_ref, cs_ref,
                       h_scr, z_scr, uext_scr, qn_scr, q_scr, kv_scr, rot_scr, ckvb_scr, kpe_scr, sq_scr, ms_scr,
                       *, tm, tpb, pos0):
    i = pl.program_id(0)
    tile = i % tpb
    off = pl.multiple_of(tile * tm, tm)
    blocks = _row_blocks(tm)
    rb = blocks[0].stop

    @pl.when(tile == 0)
    def _():
        uext_scr[0:8, :] = cinit_ref[0]

    @pl.when(i < tpb)
    def _():
        row = lax.broadcasted_iota(jnp.int32, (tm, LANES), 0)
        ang = (row + (tile * tm + pos0)).astype(F32) * invf_ref[...]
        rot_scr[0, pl.ds(off, tm), :] = jnp.cos(ang)
        rot_scr[1, pl.ds(off, tm), :] = jnp.sin(ang) * sgn_ref[...]

    h_scr[...] = _rms(x_ref[...], g_ref[...]).astype(BF16)
    low_lanes = lax.broadcasted_iota(jnp.int32, (rb, LANES), 1) < ROPE_HALF

    def proj(c0, c1):
        z_scr[:, c0:c1] = _dot(h_scr[...], w_ref[:, c0:c1])

    def rotary(y, rows):
        cosf = rot_scr[0, pl.ds(off + rows.start, rb), :]
        sinf = rot_scr[1, pl.ds(off + rows.start, rb), :]
        return _rope(y, cosf, sinf)

    def conv_group(grp):
        c0 = grp * MXU_COLS
        cols = slice(c0, c0 + MXU_COLS)
        for base in (0, C_WIDTH, 2 * C_WIDTH):
            proj(base + c0, base + c0 + MXU_COLS)
        for rows in blocks:
            up = slice(rows.start + 8, rows.stop + 8)
            uext_scr[up, cols] = (z_scr[rows, C_WIDTH + c0:C_WIDTH + c0 + MXU_COLS]
                                  * z_scr[rows, 2 * C_WIDTH + c0:2 * C_WIDTH + c0 + MXU_COLS])
        for rows in blocks:
            yconv = convw_ref[0:1, cols] * uext_scr[rows.start + 6:rows.stop + 6, cols]
            yconv = yconv + convw_ref[1:2, cols] * uext_scr[rows.start + 7:rows.stop + 7, cols]
            yconv = yconv + convw_ref[2:3, cols] * uext_scr[rows.start + 8:rows.stop + 8, cols]
            yc_ref[rows, cols] = (z_scr[rows, cols] * yconv).astype(BF16)

    proj(1536, 1792)
    proj(1792, 2048)

    for rows in blocks:
        qn_scr[rows, :] = _rms(z_scr[rows, 1536:1792], qan_ref[...]).astype(BF16)
        ckv = _rms(z_scr[rows, 1792:1920], kvan_ref[...])
        ckv_ref[rows, :] = ckv
        ckvb_scr[rows, :] = ckv.astype(BF16)

    for grp in range(D_HEADS * LANES // MXU_COLS):
        cols = slice(grp * MXU_COLS, (grp + 1) * MXU_COLS)
        q_scr[:, cols] = _dot(qn_scr[...], wqb_ref[:, cols])
        slot = grp % 2
        _segment_mean_squares(q_scr, cols, qseg_ref, sq_scr, ms_scr, slot)
        for half in range(2):
            h = 2 * grp + half
            for rows in blocks:
                blk = q_scr[rows, h * LANES:(h + 1) * LANES]
                inv = lax.rsqrt(ms_scr[slot, rows, half * LANES:(half + 1) * LANES] + EPS)
                y = rotary((blk * inv) * qgain_ref[...], rows)
                q_ref[rows, h * LANES:(h + 1) * LANES] = y.astype(BF16)

    for rows in blocks:
        kb = z_scr[rows, 1920:2048]
        ms = jnp.sum(kb * kb, axis=-1, keepdims=True) * (1.0 / D_ROPE)
        kpe_blk = rotary((kb * lax.rsqrt(ms + EPS)) * krgain_ref[...], rows)
        kpe_scr[rows, :] = kpe_blk
        x1_then_x2 = jnp.where(low_lanes, pltpu.roll(kpe_blk, LANES - X1_LO, 1), pltpu.roll(kpe_blk, ROPE_HALF, 1))
        kpe_ref[rows, :] = x1_then_x2[:, 0:D_ROPE]
    _expand_kv(ckvb_scr, kpe_scr, wkv_ref, kgain_ref, kv_scr, k_ref, v_ref,
               between={0: lambda: conv_group(0), 4: lambda: conv_group(1)})

    cs_ref[0] = uext_scr[tm + 6:tm + 8, :]
    uext_scr[0:8, :] = uext_scr[tm:tm + 8, :]


def _odd_inproj(x, norm_g, ow, conv_init, n_batch, pos0):
    t = x.shape[0]
    tm = min(TOKEN_TILE, t // n_batch)
    tpb = t // n_batch // tm
    assert tpb * tm * n_batch == t
    row = lambda w: pl.BlockSpec((tm, w), lambda i: (i, 0))
    per_b = lambda r, w: pl.BlockSpec((1, r, w), lambda i: (i // tpb, 0, 0))
    small = [ow["conv_w"], ow["qan"], ow["wqb"], ow["qgain"], ow["kvan"], ow["krgain"],
             ow["invf"], ow["sgn"], ow["wkv"], ow["kgain"], ow["qseg"]]
    kv_w = D_HEADS * LANES
    v_w = D_HEADS * D_V
    return pl.pallas_call(
        functools.partial(_odd_inproj_kernel, tm=tm, tpb=tpb, pos0=pos0),
        grid=(t // tm,),
        in_specs=[row(D_MODEL), _resident((1, D_MODEL)), _resident(ow["w_in"].shape), per_b(8, C_WIDTH)]
                 + [_resident(a.shape) for a in small],
        out_specs=[row(C_WIDTH), row(kv_w), row(kv_w), row(v_w), row(D_KV_LORA), row(D_ROPE),
                   per_b(CONV_W - 1, C_WIDTH)],
        out_shape=[jax.ShapeDtypeStruct((t, C_WIDTH), BF16), jax.ShapeDtypeStruct((t, kv_w), BF16),
                   jax.ShapeDtypeStruct((t, kv_w), BF16), jax.ShapeDtypeStruct((t, v_w), BF16),
                   jax.ShapeDtypeStruct((t, D_KV_LORA), F32), jax.ShapeDtypeStruct((t, D_ROPE), F32),
                   jax.ShapeDtypeStruct((n_batch, CONV_W - 1, C_WIDTH), F32)],
        scratch_shapes=[pltpu.VMEM((tm, D_MODEL), BF16), pltpu.VMEM((tm, ODD_IN_PAD), F32),
                        pltpu.VMEM((tm + 8, C_WIDTH), F32), pltpu.VMEM((tm, D_Q_LORA), BF16),
                        pltpu.VMEM((tm, kv_w), F32), pltpu.VMEM((tm, kv_w + v_w), F32),
                        pltpu.VMEM((2, tpb * tm, LANES), F32), pltpu.VMEM((tm, D_KV_LORA), BF16),
                        pltpu.VMEM((tm, LANES), F32), pltpu.VMEM((2, tm, MXU_COLS), BF16),
                        pltpu.VMEM((2, tm, MXU_COLS), F32)],
        compiler_params=_cparams("arbitrary"),
        name="odd_inproj",
    )(x, norm_g.reshape(1, D_MODEL), ow["w_in"], conv_init, *small)


def _odd_weights(w_in, conv_w, q_a_norm, w_q_b, kv_a_norm, w_kv_b, qn_nope, qn_rope, kn_nope, kn_rope):
    z = lambda n: jnp.zeros((n,), F32)
    qk = D_NOPE + D_ROPE
    w_in = _to_bf16(w_in)
    kr_blk = _to_head_lanes(jnp.concatenate([jnp.zeros((D_MODEL, D_NOPE), BF16), w_in[:, 1920:]], axis=1))
    w_pad = jnp.concatenate([w_in[:, :1920], kr_blk], axis=1)
    wqb = _to_head_lanes(w_q_b.astype(BF16).reshape(D_Q_LORA, D_HEADS, qk)).reshape(D_Q_LORA, D_HEADS * LANES)
    kvb = w_kv_b.astype(BF16).reshape(D_KV_LORA, D_HEADS, D_NOPE + D_V)
    wk = _to_head_lanes(jnp.concatenate([kvb[:, :, :D_NOPE], jnp.zeros((D_KV_LORA, D_HEADS, D_ROPE), BF16)], axis=-1))
    wkv = jnp.concatenate([wk.reshape(D_KV_LORA, D_HEADS * LANES),
                           kvb[:, :, D_NOPE:].reshape(D_KV_LORA, D_HEADS * D_V)], axis=1)
    inv = 1.0 / (ROPE_THETA ** (jnp.arange(ROPE_HALF, dtype=F32) / ROPE_HALF))
    ones = jnp.ones((ROPE_HALF,), F32)
    lanes = lambda v: _to_head_lanes(v).reshape(1, LANES)
    in_nope = lanes(jnp.concatenate([jnp.ones((D_NOPE,), F32), z(D_ROPE)]))
    in_rope = lanes(jnp.concatenate([z(D_NOPE), jnp.ones((D_ROPE,), F32)]))
    seg_nope = in_nope.T * in_nope * (1.0 / D_NOPE)
    seg_rope = in_rope.T * in_rope * (1.0 / D_ROPE)
    two_heads = lambda m: jnp.kron(jnp.eye(MXU_COLS // LANES, dtype=F32), m).astype(BF16)
    return {
        "qseg": two_heads(seg_nope + seg_rope),
        "wk_dense": kvb[:, :, :D_NOPE].reshape(D_KV_LORA, D_HEADS * D_NOPE),
        "kseg_t": jnp.pad(jnp.kron(jnp.eye(D_HEADS, dtype=F32), jnp.full((1, D_NOPE), 1.0 / D_NOPE, F32)),
                          ((0, 16 - D_HEADS), (0, 0))).astype(BF16),
        "q_absorb": _to_head_lanes(jnp.concatenate(
            [w_kv_b.reshape(D_KV_LORA, D_HEADS, D_NOPE + D_V)[:, :, :D_NOPE] * kn_nope,
             jnp.zeros((D_KV_LORA, D_HEADS, D_ROPE), F32)], axis=-1)).transpose(1, 2, 0).astype(BF16),
        "wv_lanes": jnp.stack([jnp.pad(kvb[:, h, D_NOPE:], ((0, 0), ((h % 2) * D_V, LANES - D_V - (h % 2) * D_V)))
                               for h in range(D_HEADS)]),
        "w_in": w_pad,
        "conv_w": conv_w.astype(F32),
        "qan": q_a_norm.reshape(1, D_Q_LORA),
        "wqb": wqb,
        "qgain": lanes(jnp.concatenate([qn_nope, qn_rope])) * MLA_QSCALE,
        "kvan": kv_a_norm.reshape(1, D_KV_LORA),
        "krgain": lanes(jnp.concatenate([z(D_NOPE), kn_rope])),
        "invf": lanes(jnp.concatenate([z(D_NOPE), inv, inv])),
        "sgn": lanes(jnp.concatenate([z(D_NOPE), -ones, ones])),
        "wkv": wkv,
        "kgain": lanes(jnp.concatenate([kn_nope, z(D_ROPE)])),
        "place": _to_head_lanes(jnp.concatenate([jnp.zeros((D_ROPE, D_NOPE), BF16), jnp.eye(D_ROPE, dtype=BF16)], axis=1)),
    }


def _mla_prompt_kernel(q_ref, k_ref, v_ref, o_ref, s_scr, e_scr, r_scr, *, seq):
    qb = min(MLA_QBLOCK, seq)
    hb = qb // 2
    lo = lax.broadcasted_iota(jnp.int32, (qb, LANES), 1) < D_V
    first_half = lax.broadcasted_iota(jnp.int32, (CHUNK, LANES), 1) < CHUNK
    scores = lambda q, k: lax.dot_general(q, k, _NT, preferred_element_type=F32)
    top, bot = slice(0, hb), slice(hb, qb)

    def spans(i):
        tk = qb * (i + 1)
        return tk - qb, tk - hb, tk

    def qk(i, h):
        t0, t1, tk = spans(i)
        hl = slice(h * LANES, (h + 1) * LANES)
        q0 = i * qb
        if t0 > 0:
            s_scr[h, :, 0:t0] = scores(q_ref[0, q0:q0 + qb, hl], k_ref[0, 0:t0, hl])
        s_scr[h, top, t0:t1] = scores(q_ref[0, q0:q0 + hb, hl], k_ref[0, t0:t1, hl])
        s_scr[h, bot, t0:tk] = scores(q_ref[0, q0 + hb:q0 + qb, hl], k_ref[0, t0:tk, hl])

    def softmax(i, h):
        t0, t1, tk = spans(i)
        for rb in range(qb // CHUNK):
            rows = slice(rb * CHUNK, (rb + 1) * CHUNK)
            visible = t0 + CHUNK * (rb + 1)
            width = t1 if rb < hb // CHUNK else tk
            n_full, ragged = visible // LANES, visible % LANES != 0
            blk = lambda c: s_scr[h, rows, c * LANES:(c + 1) * LANES]
            cols = [blk(c) for c in range(n_full)]
            if ragged:
                cols.append(jnp.where(first_half, blk(n_full), NEG_INF))
            mm = cols[0]
            for c in cols[1:]:
                mm = jnp.maximum(mm, c)
            m = _row_stat(jnp.max, mm)
            acc = None
            for c in range(width // LANES):
                if c < len(cols):
                    e = jnp.exp2(blk(c) - m)
                    if c >= n_full:
                        e = jnp.where(first_half, e, 0.0)
                    acc = e if acc is None else acc + e
                else:
                    e = jnp.zeros((CHUNK, LANES), F32)
                e_scr[h, rows, c * LANES:(c + 1) * LANES] = e.astype(BF16)
            r_scr[h, rows, :] = 1.0 / _row_stat(jnp.sum, acc)

    def pv(i, h):
        t0, t1, tk = spans(i)
        out = jnp.concatenate([_dot(e_scr[h, top, t0:t1], v_ref[0, t0:t1, :]),
                               _dot(e_scr[h, bot, t0:tk], v_ref[0, t0:tk, :])], axis=0)
        if t0 > 0:
            out = out + _dot(e_scr[h, :, 0:t0], v_ref[0, 0:t0, :])
        return out * r_scr[h]

    tasks = [(i, h) for i in range(seq // qb) for h in range(2)]
    n = len(tasks)
    outs = {}
    qk(*tasks[0])
    if n > 1:
        qk(*tasks[1])
    softmax(*tasks[0])
    for k, (i, h) in enumerate(tasks):
        if k + 2 < n:
            qk(*tasks[k + 2])
        if k + 1 < n:
            softmax(*tasks[k + 1])
        outs[h] = pv(i, h)
        if h == 1:
            o_ref[0, i * qb:(i + 1) * qb, :] = jnp.where(lo, outs[0], outs[1]).astype(BF16)


def _mla_prompt(q, k, v):
    nb, seq, _ = q.shape
    qb = min(MLA_QBLOCK, seq)
    return pl.pallas_call(
        functools.partial(_mla_prompt_kernel, seq=seq),
        grid=(nb, D_HEADS // 2),
        in_specs=[pl.BlockSpec((1, seq, 2 * LANES), lambda b, p: (b, 0, p)),
                  pl.BlockSpec((1, seq, 2 * LANES), lambda b, p: (b, 0, p)),
                  pl.BlockSpec((1, seq, LANES), lambda b, p: (b, 0, p))],
        out_specs=pl.BlockSpec((1, seq, LANES), lambda b, p: (b, 0, p)),
        out_shape=jax.ShapeDtypeStruct((nb, seq, D_HEADS * D_V), BF16),
        scratch_shapes=[pltpu.VMEM((2, qb, seq), F32), pltpu.VMEM((2, qb, seq), BF16),
                        pltpu.VMEM((2, qb, LANES), F32)],
        compiler_params=_cparams("arbitrary", "arbitrary"),
        name="mla_prompt",
    )(q, k, v)


MLA_CACHE_ROWS = 512


def _mla_sample_kernel(q_ref, ckv_ref, kpe_ref, kn_ref, vn_ref, place_ref, wk_ref, kseg_ref, qabs_ref, wv_ref,
                       o_ref, ckvb_scr, kpeb_scr, sq_scr, s1_scr, s2_scr, e_scr):
    nq, past = q_ref.shape[1], ckv_ref.shape[1]
    lane_blocks = past // LANES
    lo = lax.broadcasted_iota(jnp.int32, (nq, LANES), 1) < D_V
    head = lambda h: slice(h * LANES, (h + 1) * LANES)

    for r0 in range(0, past, MLA_CACHE_ROWS):
        rows = slice(r0, r0 + MLA_CACHE_ROWS)
        c = ckv_ref[0, rows, :].astype(BF16)
        ckvb_scr[rows, :] = c
        kpeb_scr[rows, :] = _dot(kpe_ref[0, rows, :].astype(BF16), place_ref[...]).astype(BF16)
        kraw = _dot(c, wk_ref[...])
        sq_scr[rows, :] = (kraw * kraw).astype(BF16)
    r_t = lax.rsqrt(lax.dot_general(kseg_ref[...], sq_scr[...], _NT, preferred_element_type=F32) + EPS)

    group = D_HEADS // 2
    inv_den, e_new = {}, {}

    def scores(g):
        heads = range(g * group, (g + 1) * group)
        rows = slice(g * group * nq, (g + 1) * group * nq)
        q_abs = jnp.concatenate([_dot(q_ref[0, :, head(h)], qabs_ref[h]).astype(BF16) for h in heads], axis=0)
        q_all = jnp.concatenate([q_ref[0, :, head(h)] for h in heads], axis=0)
        s1_scr[rows, :] = lax.dot_general(q_abs, ckvb_scr[...], _NT, preferred_element_type=F32)
        s2_scr[rows, :] = lax.dot_general(q_all, kpeb_scr[...], _NT, preferred_element_type=F32)

    def softmax(g):
        for h in range(g * group, (g + 1) * group):
            rows = slice(h * nq, (h + 1) * nq)
            s_new = lax.dot_general(q_ref[0, :, head(h)], kn_ref[0, :, head(h)], _NT, preferred_element_type=F32)
            mm = None
            for c in range(lane_blocks):
                cl = slice(c * LANES, (c + 1) * LANES)
                blk = s1_scr[rows, cl] * r_t[h:h + 1, cl] + s2_scr[rows, cl]
                s1_scr[rows, cl] = blk
                mm = blk if mm is None else jnp.maximum(mm, blk)
            m = jnp.maximum(jnp.max(mm, axis=-1, keepdims=True), jnp.max(s_new, axis=-1, keepdims=True))
            acc = None
            for c in range(lane_blocks):
                cl = slice(c * LANES, (c + 1) * LANES)
                e = jnp.exp2(s1_scr[rows, cl] - m)
                acc = e if acc is None else acc + e
                e_scr[rows, cl] = e.astype(BF16)
            en = jnp.exp2(s_new - m)
            inv_den[h] = 1.0 / (jnp.sum(acc, axis=-1, keepdims=True) + jnp.sum(en, axis=-1, keepdims=True))
            e_new[h] = en.astype(BF16)

    def values(g):
        rows = slice(g * group * nq, (g + 1) * group * nq)
        latent = _dot(e_scr[rows, :], ckvb_scr[...])
        for p in range(g * group // 2, (g + 1) * group // 2):
            halves = []
            for half in range(2):
                h = 2 * p + half
                local = slice((h - g * group) * nq, (h - g * group + 1) * nq)
                o = _dot(latent[local].astype(BF16), wv_ref[h]) + _dot(e_new[h], vn_ref[0, :, head(p)])
                halves.append(o * inv_den[h])
            o_ref[0, :, head(p)] = jnp.where(lo, halves[0], halves[1]).astype(BF16)

    scores(0)
    scores(1)
    softmax(0)
    values(0)
    softmax(1)
    values(1)


def _mla_sample(q, ckv, kpe, kn, vn, ow):
    nb, nq, _ = q.shape
    past = ckv.shape[1]
    assert past % MLA_CACHE_ROWS == 0
    per_b = lambda r, w: pl.BlockSpec((1, r, w), lambda b: (b, 0, 0))
    consts = [ow["place"], ow["wk_dense"], ow["kseg_t"], ow["q_absorb"], ow["wv_lanes"]]
    rows = D_HEADS * nq
    return pl.pallas_call(
        _mla_sample_kernel,
        grid=(nb,),
        in_specs=[per_b(nq, D_HEADS * LANES), per_b(past, D_KV_LORA), per_b(past, D_ROPE),
                  per_b(nq, D_HEADS * LANES), per_b(nq, D_HEADS * D_V)] + [_resident(a.shape) for a in consts],
        out_specs=per_b(nq, D_HEADS * D_V),
        out_shape=jax.ShapeDtypeStruct((nb, nq, D_HEADS * D_V), BF16),
        scratch_shapes=[pltpu.VMEM((past, D_KV_LORA), BF16), pltpu.VMEM((past, LANES), BF16),
                        pltpu.VMEM((past, D_HEADS * D_NOPE), BF16), pltpu.VMEM((rows, past), F32),
                        pltpu.VMEM((rows, past), F32), pltpu.VMEM((rows, past), BF16)],
        compiler_params=_cparams("arbitrary"),
        name="mla_sample",
    )(q, ckv, kpe, kn, vn, *consts)


def _even_layer(xp, xs, nb, seq, ndb, dseq, norm_g, w_in, w_out, a_qn, a_kn, a_sinks, b_qn, b_kn, b_rel,
                t5_table, ck_a, cv_a, ck_b, cv_b, ffn2):
    w_in = _to_bf16(w_in)
    w_out = _to_bf16(w_out)
    w_in_p = jnp.concatenate([_pair_heads(w_in[:, :A_Q], 1), w_in[:, A_Q:]], axis=1)
    woa = _pair_heads(w_out[:A_Q], 0)
    wob = w_out[A_Q:]
    ones = lambda n: jnp.ones((n,), F32)
    scale = HEAD_DIM ** -0.5 * LOG2E
    gain_row = jnp.concatenate([jnp.tile(a_qn, A_HEADS) * scale, jnp.tile(a_kn, A_KV_HEADS), ones(A_KV),
                                jnp.tile(b_qn, B_HEADS) * scale, jnp.tile(b_kn, B_HEADS), ones(B_QKV)]
                               ).reshape(1, EVEN_IN).astype(F32)
    bias_a = _bias_a(t5_table) * LOG2E
    bias_b = _bias_b(b_rel) * LOG2E
    sink = jnp.broadcast_to(jnp.repeat(a_sinks.astype(F32) * LOG2E, PAIR_ROWS)[:, None],
                            (A_HEADS * PAIR_ROWS, LANES))
    la, lb = min(WINDOW, seq), min(B_REACH, seq)

    aq, ak, av, bq, bk, bv, cak, cav, cbk, cbv = _even_inproj(xp, norm_g, w_in_p, gain_row, nb, la, lb)
    r3 = lambda a: a.reshape(nb, seq, a.shape[-1])
    ya = _attn_a(r3(aq), r3(ak), r3(av), bias_a, sink, WINDOW, True)
    yb = _attn_b(r3(bq), r3(bk), r3(bv), bias_b, B_REACH, True)
    yap, ybp = ya.reshape(nb * seq, A_Q), yb.reshape(nb * seq, B_QKV)
    st_p = (cak.reshape(nb, la, A_KV_HEADS, HEAD_DIM), cav.reshape(nb, la, A_KV_HEADS, HEAD_DIM),
            cbk.reshape(nb, lb, B_HEADS, HEAD_DIM), cbv.reshape(nb, lb, B_HEADS, HEAD_DIM))

    ts = ndb * dseq
    aq, ak, av, bq, bk, bv, nak, nav, nbk, nbv = _even_inproj(xs, norm_g, w_in_p, gain_row, 1, ts, ts)
    pad_q = lambda a, rows: jnp.pad(a.reshape(ndb, dseq, a.shape[-1]), ((0, 0), (0, rows - dseq), (0, 0)))

    def window(cache, new, rows):
        w = cache.shape[-2] * cache.shape[-1]
        full = jnp.concatenate([cache.reshape(ndb, -1, w), new.reshape(ndb, dseq, w)], axis=1)
        buf = jnp.pad(full, ((0, 0), (0, rows - dseq), (0, 0))).astype(BF16)
        return full[:, dseq:].reshape(cache.shape), buf

    st_ak, kbuf_a = window(ck_a, nak, PAIR_ROWS)
    st_av, vbuf_a = window(cv_a, nav, PAIR_ROWS)
    st_bk, kbuf_b = window(ck_b, nbk, B_STEP_ROWS)
    st_bv, vbuf_b = window(cv_b, nbv, B_STEP_ROWS)
    ya = _attn_a(pad_q(aq, PAIR_ROWS), kbuf_a, vbuf_a, bias_a, sink, WINDOW, False)[:, :dseq]
    yb = _attn_b(pad_q(bq, B_STEP_ROWS), kbuf_b, vbuf_b, bias_b, B_REACH, False)[:, :dseq]
    xp, xs = _ffn(xp, xs, *ffn2, out_proj=(yap, ybp, ya.reshape(ts, A_Q), yb.reshape(ts, B_QKV), woa, wob))
    return xp, xs, st_p, (st_ak, st_av, st_bk, st_bv)


def _odd_layer(xp, xs, nb, seq, ndb, dseq, past, norm_g, ow, w_out, conv_prev, c_ckv, c_kpe, ffn2):
    w_out = _to_bf16(w_out)
    woc = w_out[:C_WIDTH]
    wod = w_out[C_WIDTH:]

    zero_init = jnp.zeros((nb, 8, C_WIDTH), F32)
    yc, q, k, v, ckv, kpe, cs = _odd_inproj(xp, norm_g, ow, zero_init, nb, 0)
    r3 = lambda a: a.reshape(nb, seq, a.shape[-1])
    ycp, ydp = yc, _mla_prompt(r3(q), r3(k), r3(v)).reshape(nb * seq, D_HEADS * D_V)
    st_p = (cs, ckv.reshape(nb, seq, D_KV_LORA), kpe.reshape(nb, seq, D_ROPE))

    ts = ndb * dseq
    init = jnp.pad(conv_prev.astype(F32), ((0, 0), (8 - (CONV_W - 1), 0), (0, 0)))
    yc, q, kn, vn, ckv, kpe, cs = _odd_inproj(xs, norm_g, ow, init, ndb, past)
    s3 = lambda a: a.reshape(ndb, dseq, a.shape[-1])
    yd = _mla_sample(s3(q), c_ckv, c_kpe, s3(kn), s3(vn), ow)
    xp, xs = _ffn(xp, xs, *ffn2, out_proj=(ycp, ydp, yc, yd.reshape(ts, D_HEADS * D_V), woc, wod))
    st_s = (cs, ckv.reshape(ndb, dseq, D_KV_LORA), kpe.reshape(ndb, dseq, D_ROPE))
    return xp, xs, st_p, st_s


def kernel(x_prompt, x_sample, cache_a_k, cache_a_v, cache_b_k, cache_b_v, state_c_conv, cache_d_ckv, cache_d_kpe, ff1_norm, ff1_w_gu, ff1_w_down, mix_norm, ff2_norm, ff2_w_gu, ff2_w_down, t5_bias_table, ev_w_in, ev_w_out, a_q_norm, a_k_norm, a_sinks, b_q_norm, b_k_norm, b_rel_bias, od_w_in, od_w_out, c_conv_w, d_q_a_norm, d_w_q_b, d_kv_a_norm, d_w_kv_b, d_q_nope_norm, d_q_rope_norm, d_k_nope_norm, d_k_rope_norm):
    nb, seq, _ = x_prompt.shape
    ndb, dseq, _ = x_sample.shape
    past = cache_d_ckv.shape[2]
    depth = ff1_norm.shape[0]
    assert seq % TOKEN_TILE == 0 and dseq == CHUNK and past % CHUNK == 0
    xp = x_prompt.reshape(nb * seq, D_MODEL)
    xs = x_sample.reshape(ndb * dseq, D_MODEL)
    even_p, even_s, odd_p, odd_s = [], [], [], []
    for l in range(depth):
        i = l // 2
        ffn2 = (ff2_norm[l], ff2_w_gu, ff2_w_down, l)
        xp, xs = _ffn(xp, xs, ff1_norm[l], ff1_w_gu, ff1_w_down, l)
        if l % 2 == 0:
            xp, xs, sp, ss = _even_layer(
                xp, xs, nb, seq, ndb, dseq, mix_norm[l], ev_w_in[i], ev_w_out[i], a_q_norm[i], a_k_norm[i],
                a_sinks[i], b_q_norm[i], b_k_norm[i], b_rel_bias[i], t5_bias_table,
                cache_a_k[i], cache_a_v[i], cache_b_k[i], cache_b_v[i], ffn2)
            even_p.append(sp)
            even_s.append(ss)
        else:
            ow = _odd_weights(od_w_in[i], c_conv_w[i], d_q_a_norm[i], d_w_q_b[i], d_kv_a_norm[i], d_w_kv_b[i],
                              d_q_nope_norm[i], d_q_rope_norm[i], d_k_nope_norm[i], d_k_rope_norm[i])
            xp, xs, sp, ss = _odd_layer(xp, xs, nb, seq, ndb, dseq, past, mix_norm[l], ow, od_w_out[i],
                                        state_c_conv[i], cache_d_ckv[i], cache_d_kpe[i], ffn2)
            odd_p.append(sp)
            odd_s.append(ss)
    stack = lambda group, j: group[0][j][None] if len(group) == 1 else jnp.stack([g[j] for g in group])
    return (xp.reshape(nb, seq, D_MODEL), xs.reshape(ndb, dseq, D_MODEL),
            stack(even_p, 0), stack(even_p, 1), stack(even_p, 2), stack(even_p, 3),
            stack(odd_p, 0), stack(odd_p, 1), stack(odd_p, 2),
            stack(even_s, 0), stack(even_s, 1), stack(even_s, 2), stack(even_s, 3),
            stack(odd_s, 0), stack(odd_s, 1), stack(odd_s, 2))
```

```python
import functools
import math

import jax
import jax.numpy as jnp
from jax import lax
from jax.experimental import pallas as pl
from jax.experimental.pallas import tpu as pltpu

F32 = jnp.float32
BF16 = jnp.bfloat16

D_MODEL = 1024
CHUNK = 64
HEAD_DIM = 64
EPS = 1e-6
A_HEADS = 8
A_KV_HEADS = 2
WINDOW = 128
T5_BUCKETS = 32
T5_MAX_DIST = 128
B_HEADS = 8
B_REACH = 512
B_MAX_REL = 128
C_WIDTH = 512
CONV_W = 3
D_HEADS = 8
D_Q_LORA = 256
D_KV_LORA = 128
D_NOPE = 64
D_ROPE = 32
D_V = 64
ROPE_THETA = 10000.0
FFN_DIM = 2816
A_Q = A_HEADS * HEAD_DIM
A_KV = A_KV_HEADS * HEAD_DIM
B_QKV = B_HEADS * HEAD_DIM
EVEN_IN = A_Q + 2 * A_KV + 3 * B_QKV

LANES = 128
TOKEN_TILE = 512
CAST_BLOCK_BYTES = 6 * 1024 * 1024
MXU_COLS = 256
FFN_CHUNK = MXU_COLS
PAIR_ROWS = 2 * CHUNK
MLA_QBLOCK = 512
VMEM_LIMIT_BYTES = 56 * 1024 * 1024
LOG2E = math.log2(math.e)
MLA_QSCALE = (D_NOPE + D_ROPE) ** -0.5 * LOG2E
NEG_INF = float("-inf")

_NT = (((1,), (1,)), ((), ()))


def _cparams(*sem):
    return pltpu.CompilerParams(dimension_semantics=sem, vmem_limit_bytes=VMEM_LIMIT_BYTES)


def _resident(shape):
    zeros = (0,) * len(shape)
    return pl.BlockSpec(shape, lambda *_: zeros, pipeline_mode=pl.Buffered(1))


def _rms(x, g):
    ms = jnp.mean(x * x, axis=-1, keepdims=True)
    return (x * lax.rsqrt(ms + EPS)) * g


def _dot(a, b):
    return jnp.dot(a, b, preferred_element_type=F32)


def _weight_stream(wgu_hbm, wd_hbm, wgu_ref, wd_ref, stage, sem):
    n_chunks = FFN_DIM // FFN_CHUNK
    g_stage, u_stage, d_stage = stage

    def copies(j):
        cols = pl.ds(j * FFN_CHUNK, FFN_CHUNK)
        slot = j % 2
        return (pltpu.make_async_copy(wgu_hbm.at[:, cols], g_stage.at[slot], sem.at[0, slot]),
                pltpu.make_async_copy(wgu_hbm.at[:, pl.ds(FFN_DIM + j * FFN_CHUNK, FFN_CHUNK)],
                                      u_stage.at[slot], sem.at[1, slot]),
                pltpu.make_async_copy(wd_hbm.at[cols, :], d_stage.at[slot], sem.at[2, slot]))

    def fetch(j):
        if j == 0:
            for cp in copies(0):
                cp.start()
        if j + 1 < n_chunks:
            for cp in copies(j + 1):
                cp.start()
        for cp in copies(j):
            cp.wait()
        slot = j % 2
        wgu_ref[:, j * FFN_CHUNK:(j + 1) * FFN_CHUNK] = g_stage[slot].astype(BF16)
        wgu_ref[:, FFN_DIM + j * FFN_CHUNK:FFN_DIM + (j + 1) * FFN_CHUNK] = u_stage[slot].astype(BF16)
        wd_ref[j * FFN_CHUNK:(j + 1) * FFN_CHUNK, :] = d_stage[slot].astype(BF16)

    return fetch


def _ffn_tile(x_ref, y_refs, wo_refs, g_ref, wgu_ref, wd_ref, o_ref, h_scr, a_scr, n_chunks, fetch=None):
    rows = x_ref.shape[0]
    if rows < h_scr.shape[0]:
        h_scr, a_scr = h_scr.at[pl.ds(0, rows)], a_scr.at[pl.ds(0, rows)]
    if y_refs:
        (ya_ref, yb_ref), (woa_ref, wob_ref) = y_refs, wo_refs
        y = _dot(ya_ref[...], woa_ref[...]) + _dot(yb_ref[...], wob_ref[...])
        o_ref[...] = x_ref[...] + y
        res_ref = o_ref
    else:
        res_ref = x_ref
    h_scr[...] = _rms(res_ref[...], g_ref[...]).astype(BF16)
    for j in range(n_chunks):
        if fetch is not None:
            fetch(j)
        h = h_scr[...]
        g = _dot(h, wgu_ref[:, j * FFN_CHUNK:(j + 1) * FFN_CHUNK])
        u = _dot(h, wgu_ref[:, FFN_DIM + j * FFN_CHUNK:FFN_DIM + (j + 1) * FFN_CHUNK])
        a_scr[:, j * FFN_CHUNK:(j + 1) * FFN_CHUNK] = ((g * jax.nn.sigmoid(g)) * u).astype(BF16)
    o_ref[...] = res_ref[...] + 0.5 * _dot(a_scr[...], wd_ref[...])


def _ffn_kernel(*refs, n_chunks, fused_out, layer, n_main):
    if fused_out:
        (xp_ref, yap_ref, ybp_ref, xs_ref, yas_ref, ybs_ref, woa_ref, wob_ref, g_ref, wgu_hbm, wd_hbm,
         op_ref, os_ref, h_scr, a_scr, wgu_ref, wd_ref, g_stage, u_stage, d_stage, sem) = refs
        yp, ys, wo = (yap_ref, ybp_ref), (yas_ref, ybs_ref), (woa_ref, wob_ref)
    else:
        (xp_ref, xs_ref, g_ref, wgu_hbm, wd_hbm,
         op_ref, os_ref, h_scr, a_scr, wgu_ref, wd_ref, g_stage, u_stage, d_stage, sem) = refs
        yp = ys = wo = None
    i = pl.program_id(0)
    tile = functools.partial(_ffn_tile, g_ref=g_ref, wgu_ref=wgu_ref, wd_ref=wd_ref, h_scr=h_scr, a_scr=a_scr,
                             n_chunks=n_chunks)

    @pl.when(i == 0)
    def _():
        fetch = _weight_stream(wgu_hbm.at[layer], wd_hbm.at[layer], wgu_ref, wd_ref,
                               (g_stage, u_stage, d_stage), sem)
        tile(xp_ref, yp, wo, o_ref=op_ref, fetch=fetch)

    @pl.when((i > 0) & (i < n_main))
    def _():
        tile(xp_ref, yp, wo, o_ref=op_ref)

    @pl.when(i == n_main)
    def _():
        tile(xs_ref, ys, wo, o_ref=os_ref)


def _ffn(xp, xs, norm_g, w_gu, w_down, layer, out_proj=None):
    tp, ts = xp.shape[0], xs.shape[0]
    tm = min(TOKEN_TILE, tp)
    n_main = tp // tm
    n_chunks = FFN_DIM // FFN_CHUNK
    assert n_main * tm == tp and ts <= tm and ts % 16 == 0
    main = lambda w: pl.BlockSpec((tm, w), lambda i: (jnp.minimum(i, n_main - 1), 0))
    side_in = lambda w: pl.BlockSpec((ts, w), lambda i: (0, 0), pipeline_mode=pl.Buffered(1))
    side_out = lambda w: pl.BlockSpec((ts, w), lambda i: (0, 0))
    in_hbm = pl.BlockSpec(memory_space=pl.ANY)
    if out_proj is not None:
        yap, ybp, yas, ybs, woa, wob = out_proj
        in_specs = [main(D_MODEL), main(yap.shape[1]), main(ybp.shape[1]),
                    side_in(D_MODEL), side_in(yas.shape[1]), side_in(ybs.shape[1]),
                    _resident(woa.shape), _resident(wob.shape)]
        args = [xp, yap, ybp, xs, yas, ybs, woa, wob]
    else:
        in_specs = [main(D_MODEL), side_in(D_MODEL)]
        args = [xp, xs]
    in_specs += [_resident((1, D_MODEL)), in_hbm, in_hbm]
    args += [norm_g.reshape(1, D_MODEL), w_gu, w_down]
    return pl.pallas_call(
        functools.partial(_ffn_kernel, n_chunks=n_chunks, fused_out=out_proj is not None, layer=layer,
                          n_main=n_main),
        grid=(n_main + 1,),
        in_specs=in_specs,
        out_specs=[main(D_MODEL), side_out(D_MODEL)],
        out_shape=[jax.ShapeDtypeStruct((tp, D_MODEL), F32), jax.ShapeDtypeStruct((ts, D_MODEL), F32)],
        scratch_shapes=[pltpu.VMEM((tm, D_MODEL), BF16), pltpu.VMEM((tm, FFN_DIM), BF16),
                        pltpu.VMEM((D_MODEL, 2 * FFN_DIM), BF16), pltpu.VMEM((FFN_DIM, D_MODEL), BF16),
                        pltpu.VMEM((2, D_MODEL, FFN_CHUNK), F32), pltpu.VMEM((2, D_MODEL, FFN_CHUNK), F32),
                        pltpu.VMEM((2, FFN_CHUNK, D_MODEL), F32),
                        pltpu.SemaphoreType.DMA((3, 2))],
        compiler_params=_cparams("arbitrary"),
        name="ffn_out" if out_proj is not None else "ffn",
    )(*args)


def _cast_kernel(x_ref, o_ref):
    o_ref[...] = x_ref[...].astype(BF16)


def _to_bf16(w, layer=None):
    r, c = w.shape[-2:]
    fits = [d for d in range(16, r + 1, 16) if r % d == 0 and d * c * 4 <= CAST_BLOCK_BYTES]
    tr = max(fits) if fits else r
    out_spec = pl.BlockSpec((tr, c), lambda i: (i, 0))
    in_spec = out_spec if layer is None else pl.BlockSpec((None, tr, c), lambda i: (layer, i, 0))
    return pl.pallas_call(
        _cast_kernel, grid=(r // tr,), in_specs=[in_spec], out_specs=out_spec,
        out_shape=jax.ShapeDtypeStruct((r, c), BF16),
        compiler_params=_cparams("arbitrary"), name="cast_bf16",
    )(w)


def _norm_halves(blk, gain):
    sq = blk * blk
    lo = lax.broadcasted_iota(jnp.int32, blk.shape, 1) < HEAD_DIM
    s_lo = jnp.sum(jnp.where(lo, sq, 0.0), axis=-1, keepdims=True)
    s_hi = jnp.sum(jnp.where(lo, 0.0, sq), axis=-1, keepdims=True)
    inv = jnp.where(lo, lax.rsqrt(s_lo * (1.0 / HEAD_DIM) + EPS), lax.rsqrt(s_hi * (1.0 / HEAD_DIM) + EPS))
    return (blk * inv) * gain


def _even_inproj_kernel(x_ref, g_ref, w_ref, gain_ref,
                        aq_ref, ak_ref, av_ref, bq_ref, bk_ref, bv_ref,
                        cak_ref, cav_ref, cbk_ref, cbv_ref, h_scr, z_scr, *, tm, tpb, rows_a, rows_b):
    i = pl.program_id(0)
    last = (i % tpb) == (tpb - 1)
    h_scr[...] = _rms(x_ref[...], g_ref[...]).astype(BF16)

    def lanes(c):
        return slice(c * LANES, (c + 1) * LANES)

    def emit(c):
        blk = z_scr[:, lanes(c)]
        gain = gain_ref[:, lanes(c)]
        if c < 4:
            aq_ref[:, lanes(c)] = _norm_halves(blk, gain).astype(BF16)
        elif c == 4:
            kn = _norm_halves(blk, gain)
            ak_ref[...] = kn.astype(BF16)
            z_scr[:, lanes(c)] = kn
        elif c == 5:
            av_ref[...] = blk.astype(BF16)
        elif c < 10:
            bq_ref[:, lanes(c - 6)] = _norm_halves(blk, gain).astype(BF16)
        elif c < 14:
            kn = _norm_halves(blk, gain)
            bk_ref[:, lanes(c - 10)] = kn.astype(BF16)
            z_scr[:, lanes(c)] = kn
        else:
            bv_ref[:, lanes(c - 14)] = blk.astype(BF16)

    for grp in range(EVEN_IN // MXU_COLS):
        cols = slice(grp * MXU_COLS, (grp + 1) * MXU_COLS)
        z_scr[:, cols] = _dot(h_scr[...], w_ref[:, cols])
        emit(2 * grp)
        emit(2 * grp + 1)

    @pl.when(last)
    def _():
        cak_ref[0] = z_scr[tm - rows_a:, lanes(4)]
        cav_ref[0] = z_scr[tm - rows_a:, lanes(5)]
        cbk_ref[0] = z_scr[tm - rows_b:, 10 * LANES:14 * LANES]
        cbv_ref[0] = z_scr[tm - rows_b:, 14 * LANES:18 * LANES]


def _even_inproj(x, norm_g, w_in, gain_row, n_batch, rows_a, rows_b):
    t = x.shape[0]
    tm = min(TOKEN_TILE, t)
    tpb = t // n_batch // tm
    assert tpb * tm * n_batch == t and rows_a <= tm and rows_b <= tm
    row = lambda w: pl.BlockSpec((tm, w), lambda i: (i, 0))
    cache = lambda r, w: pl.BlockSpec((1, r, w), lambda i: (i // tpb, 0, 0))
    bshape = lambda w: jax.ShapeDtypeStruct((t, w), BF16)
    cshape = lambda r, w: jax.ShapeDtypeStruct((n_batch, r, w), F32)
    return pl.pallas_call(
        functools.partial(_even_inproj_kernel, tm=tm, tpb=tpb, rows_a=rows_a, rows_b=rows_b),
        grid=(t // tm,),
        in_specs=[row(D_MODEL), _resident((1, D_MODEL)), _resident(w_in.shape), _resident(gain_row.shape)],
        out_specs=[row(A_Q), row(A_KV), row(A_KV), row(B_QKV), row(B_QKV), row(B_QKV),
                   cache(rows_a, A_KV), cache(rows_a, A_KV), cache(rows_b, B_QKV), cache(rows_b, B_QKV)],
        out_shape=[bshape(A_Q), bshape(A_KV), bshape(A_KV), bshape(B_QKV), bshape(B_QKV), bshape(B_QKV),
                   cshape(rows_a, A_KV), cshape(rows_a, A_KV), cshape(rows_b, B_QKV), cshape(rows_b, B_QKV)],
        scratch_shapes=[pltpu.VMEM((tm, D_MODEL), BF16), pltpu.VMEM((tm, EVEN_IN), F32)],
        compiler_params=_cparams("arbitrary"),
        name="even_inproj",
    )(x, norm_g.reshape(1, D_MODEL), w_in, gain_row)


def _pair_heads(a, axis):
    shape = a.shape
    split = shape[:axis] + (2, A_HEADS // 2, HEAD_DIM) + shape[axis + 1:]
    return a.reshape(split).swapaxes(axis, axis + 1).reshape(shape)


def _fill_padded(buf, src_ref, ctx):
    buf[0:ctx, :] = jnp.zeros((ctx, buf.shape[1]), buf.dtype)
    buf[ctx:, :] = src_ref[0]


def _windows(k_ref, v_ref, pad_scr, ctx, pad_front, step_rows, stream):
    win = ctx + step_rows
    if pad_front:
        kbuf, vbuf = pad_scr
        _fill_padded(kbuf, k_ref, ctx)
        _fill_padded(vbuf, v_ref, ctx)
        return (lambda r0: kbuf[pl.ds(r0, win), :]), (lambda r0: vbuf[pl.ds(r0, win), :])
    return (lambda r0: k_ref[stream, pl.ds(r0, win), :]), (lambda r0: v_ref[stream, pl.ds(r0, win), :])


def _fold_lanes(x, op):
    acc = x[:, :LANES]
    for c in range(1, x.shape[1] // LANES):
        acc = op(acc, x[:, c * LANES:(c + 1) * LANES])
    return acc


def _row_stat(reduce_fn, x):
    return jnp.broadcast_to(reduce_fn(x, axis=-1, keepdims=True), x.shape)


PAIR_SLOTS = 4


def _run_steps(make_stages, n_streams, n_steps, n_front, step_rows, lookahead):
    stages = [make_stages(stream) for stream in range(n_streams)]
    tasks = [(stream, t) for stream in range(n_streams) for t in range(n_steps)]

    def run(stage, k):
        stream, t = tasks[k]
        stages[stream][stage](t * step_rows, k % PAIR_SLOTS, t < n_front)

    n = len(tasks)
    if not lookahead:
        for k in range(n):
            for stage in range(3):
                run(stage, k)
        return
    run(0, 0)
    if n > 1:
        run(0, 1)
    run(1, 0)
    for k in range(n):
        if k + 2 < n:
            run(0, k + 2)
        if k + 1 < n:
            run(1, k + 1)
        run(2, k)


def _band_bias(ext_row, n_rows, ctx):
    win = ctx + n_rows
    period = ext_row.shape[1]
    t = pltpu.roll(jnp.broadcast_to(ext_row, (n_rows, period)), 0, 1, stride=1, stride_axis=0)[:, :win]
    chunk = lax.broadcasted_iota(jnp.int32, (n_rows, win), 0) // CHUNK
    col = lax.broadcasted_iota(jnp.int32, (n_rows, win), 1)
    visible = (col >= CHUNK * chunk) & (col < ctx + CHUNK * (chunk + 1))
    return jnp.where(visible, t, NEG_INF)


def _attn_a_kernel(q_ref, k_ref, v_ref, ext_ref, sink_ref, o_ref, bias_ref, s_scr, e_scr, *pad_scr,
                   nq, ctx, pad_front):
    win = ctx + PAIR_ROWS
    lo = lax.broadcasted_iota(jnp.int32, (PAIR_ROWS, LANES), 1) < HEAD_DIM

    @pl.when(pl.program_id(0) == 0)
    def _():
        for head in range(A_HEADS):
            bias_ref[head * PAIR_ROWS:(head + 1) * PAIR_ROWS, :] = _band_bias(
                ext_ref[head:head + 1, :], PAIR_ROWS, ctx)

    def make_stages(stream):
        kwin, vwin = _windows(k_ref, v_ref, pad_scr, ctx, pad_front, PAIR_ROWS, stream)
        return (functools.partial(scores, stream, kwin), functools.partial(softmax, stream),
                functools.partial(values, stream, vwin))

    def scores(stream, kwin, r0, slot, masked):
        q_lo, q_hi = [], []
        for p in range(4):
            qp = q_ref[stream, pl.ds(r0, PAIR_ROWS), p * LANES:(p + 1) * LANES]
            zero = jnp.zeros_like(qp)
            q_lo.append(jnp.where(lo, qp, zero))
            q_hi.append(jnp.where(lo, zero, qp))
        q2 = jnp.concatenate(q_lo + q_hi, axis=0)
        s_scr[slot] = lax.dot_general(q2, kwin(r0), _NT, preferred_element_type=F32)

    def softmax(stream, r0, slot, masked):
        for head in range(A_HEADS):
            rows = slice(head * PAIR_ROWS, (head + 1) * PAIR_ROWS)
            s = s_scr[slot, rows, :] + bias_ref[rows, :]
            if masked:
                col = lax.broadcasted_iota(jnp.int32, (PAIR_ROWS, win), 1)
                s = jnp.where(col + r0 >= ctx, s, NEG_INF)
            sk = sink_ref[rows, :]
            m = jnp.maximum(_row_stat(jnp.max, _fold_lanes(s, jnp.maximum)), sk)
            e = [jnp.exp2(s[:, c * LANES:(c + 1) * LANES] - m) for c in range(win // LANES)]
            den = _row_stat(jnp.sum, functools.reduce(jnp.add, e)) + jnp.exp2(sk - m)
            inv = 1.0 / den
            for c in range(win // LANES):
                e_scr[slot, rows, c * LANES:(c + 1) * LANES] = (e[c] * inv).astype(BF16)

    def values(stream, vwin, r0, slot, masked):
        o2 = _dot(e_scr[slot], vwin(r0))
        half = 4 * PAIR_ROWS
        for p in range(4):
            o = jnp.where(lo, o2[p * PAIR_ROWS:(p + 1) * PAIR_ROWS],
                          o2[half + p * PAIR_ROWS:half + (p + 1) * PAIR_ROWS])
            o_ref[stream, pl.ds(r0, PAIR_ROWS), p * LANES:(p + 1) * LANES] = o.astype(BF16)

    _run_steps(make_stages, q_ref.shape[0], nq // PAIR_ROWS, ctx // PAIR_ROWS if pad_front else 0, PAIR_ROWS,
               lookahead=False)


def _streams_per_block(nb, pad_front):
    return 1 if pad_front else nb


def _attn_a(q, k, v, ext, sink, ctx, pad_front):
    nb, nq, _ = q.shape
    nk = k.shape[1]
    assert nq % PAIR_ROWS == 0 and nk == (nq if pad_front else ctx + nq)
    bb = _streams_per_block(nb, pad_front)
    per_b = lambda r, w: pl.BlockSpec((bb, r, w), lambda b: (b, 0, 0))
    rows, win = A_HEADS * PAIR_ROWS, ctx + PAIR_ROWS
    scratch = [pltpu.VMEM((rows, win), F32),
               pltpu.VMEM((PAIR_SLOTS, rows, win), F32), pltpu.VMEM((PAIR_SLOTS, rows, win), BF16)]
    if pad_front:
        scratch += [pltpu.VMEM((ctx + nq, A_KV), BF16)] * 2
    return pl.pallas_call(
        functools.partial(_attn_a_kernel, nq=nq, ctx=ctx, pad_front=pad_front),
        grid=(nb // bb,),
        in_specs=[per_b(nq, A_Q), per_b(nk, A_KV), per_b(nk, A_KV), _resident(ext.shape), _resident(sink.shape)],
        out_specs=per_b(nq, A_Q),
        out_shape=jax.ShapeDtypeStruct((nb, nq, A_Q), BF16),
        scratch_shapes=scratch,
        compiler_params=_cparams("arbitrary"),
        name="attn_a",
    )(q, k, v, ext, sink)


B_STEP_ROWS = 4 * CHUNK


def _attn_b_kernel(q_ref, k_ref, v_ref, ext_ref, o_ref, bias_scr, s_scr, e_scr, r_scr, *pad_scr,
                   nq, ctx, pad_front):
    win = ctx + B_STEP_ROWS
    lo = lax.broadcasted_iota(jnp.int32, (B_STEP_ROWS, LANES), 1) < HEAD_DIM
    chunks = B_STEP_ROWS // CHUNK
    pair_idx = pl.program_id(1)

    @pl.when(pl.program_id(0) == 0)
    def _():
        for half in range(2):
            bias_scr[pair_idx, half * B_STEP_ROWS:(half + 1) * B_STEP_ROWS, :] = _band_bias(
                ext_ref[0, half:half + 1, :], B_STEP_ROWS, ctx)

    bias_ref = bias_scr.at[pair_idx]

    def make_stages(stream):
        kwin, vwin = _windows(k_ref, v_ref, pad_scr, ctx, pad_front, B_STEP_ROWS, stream)
        return (functools.partial(scores, stream, kwin), functools.partial(softmax, stream),
                functools.partial(values, stream, vwin))

    def scores(stream, kwin, r0, slot, masked):
        qp = q_ref[stream, pl.ds(r0, B_STEP_ROWS), :]
        zero = jnp.zeros_like(qp)
        q2 = jnp.concatenate([jnp.where(lo, qp, zero), jnp.where(lo, zero, qp)], axis=0)
        s_scr[slot] = lax.dot_general(q2, kwin(r0), _NT, preferred_element_type=F32)

    def softmax(stream, r0, slot, masked):
        for rb in range(2 * chunks):
            rows = slice(rb * CHUNK, (rb + 1) * CHUNK)
            ci = rb % chunks
            c_lo, c_hi = CHUNK * ci // LANES, -(-(ctx + CHUNK * (ci + 1)) // LANES)
            band = slice(c_lo * LANES, c_hi * LANES)
            s = s_scr[slot, rows, band] + bias_ref[rows, band]
            if masked:
                col = lax.broadcasted_iota(jnp.int32, s.shape, 1) + c_lo * LANES
                s = jnp.where(col + r0 >= ctx, s, NEG_INF)
            m = _row_stat(jnp.max, _fold_lanes(s, jnp.maximum))
            e = [jnp.exp2(s[:, c * LANES:(c + 1) * LANES] - m) for c in range(c_hi - c_lo)]
            for c in range(win // LANES):
                blk = e[c - c_lo].astype(BF16) if c_lo <= c < c_hi else jnp.zeros((CHUNK, LANES), BF16)
                e_scr[slot, rows, c * LANES:(c + 1) * LANES] = blk
            r_scr[slot, rows, :] = 1.0 / _row_stat(jnp.sum, functools.reduce(jnp.add, e))

    def values(stream, vwin, r0, slot, masked):
        o2 = _dot(e_scr[slot], vwin(r0)) * r_scr[slot]
        o = jnp.where(lo, o2[:B_STEP_ROWS], o2[B_STEP_ROWS:])
        o_ref[stream, pl.ds(r0, B_STEP_ROWS), :] = o.astype(BF16)

    _run_steps(make_stages, q_ref.shape[0], nq // B_STEP_ROWS,
               -(-ctx // B_STEP_ROWS) if pad_front else 0, B_STEP_ROWS, lookahead=True)


def _attn_b(q, k, v, ext, ctx, pad_front):
    nb, nq, _ = q.shape
    nk = k.shape[1]
    assert nq % B_STEP_ROWS == 0 and nk == (nq if pad_front else ctx + nq)
    bb = _streams_per_block(nb, pad_front)
    blk = lambda r: pl.BlockSpec((bb, r, LANES), lambda b, p: (b, 0, p))
    rows, win = 2 * B_STEP_ROWS, ctx + B_STEP_ROWS
    scratch = [pltpu.VMEM((B_HEADS // 2, rows, win), F32),
               pltpu.VMEM((PAIR_SLOTS, rows, win), F32), pltpu.VMEM((PAIR_SLOTS, rows, win), BF16),
               pltpu.VMEM((PAIR_SLOTS, rows, LANES), F32)]
    if pad_front:
        scratch += [pltpu.VMEM((ctx + nq, LANES), BF16)] * 2
    return pl.pallas_call(
        functools.partial(_attn_b_kernel, nq=nq, ctx=ctx, pad_front=pad_front),
        grid=(nb // bb, B_HEADS // 2),
        in_specs=[blk(nq), blk(nk), blk(nk),
                  pl.BlockSpec((1,) + ext.shape[1:], lambda b, p: (p, 0, 0))],
        out_specs=blk(nq),
        out_shape=jax.ShapeDtypeStruct((nb, nq, B_QKV), BF16),
        scratch_shapes=scratch,
        compiler_params=_cparams("arbitrary", "arbitrary"),
        name="attn_b",
    )(q, k, v, ext)


def _t5_bucket(rel):
    nb = T5_BUCKETS // 2
    max_exact = nb // 2
    n = -rel
    ret = jnp.where(n < 0, nb, 0)
    n = jnp.abs(n)
    nf = jnp.maximum(n, 1).astype(F32)
    large = max_exact + (jnp.log(nf / max_exact) / math.log(T5_MAX_DIST / max_exact)
                         * (nb - max_exact)).astype(jnp.int32)
    large = jnp.minimum(large, nb - 1)
    return ret + jnp.where(n < max_exact, n, large)


def _band_values(ctx, n_rows, value_of_rel):
    n_cols = ctx + n_rows
    period = -(-(n_cols + n_rows - 1) // LANES) * LANES
    d = jnp.concatenate([jnp.arange(0, period - (n_rows - 1)), jnp.arange(-(n_rows - 1), 0)])
    return value_of_rel(d - ctx)


def _bias_a(t5_table):
    return _band_values(WINDOW, PAIR_ROWS, lambda rel: t5_table.astype(F32)[_t5_bucket(rel)].T)


def _bias_b(b_rel):
    ext = _band_values(B_REACH, B_STEP_ROWS,
                       lambda rel: b_rel.astype(F32)[:, jnp.clip(rel, -B_MAX_REL, B_MAX_REL) + B_MAX_REL])
    return ext.reshape(B_HEADS // 2, 2, ext.shape[-1])


ODD_QA = 3 * C_WIDTH
ODD_KVA = ODD_QA + D_Q_LORA
ODD_KR = ODD_KVA + D_KV_LORA
ODD_IN_PAD = ODD_KR + LANES
ROPE_HALF = D_ROPE // 2
X1_LO = LANES // 2
NOPE_SPLIT = X1_LO - ROPE_HALF


def _head_lane_source():
    zero = D_NOPE + D_ROPE
    src = []
    for lane in range(LANES):
        if lane < ROPE_HALF:
            src.append(D_NOPE + ROPE_HALF + lane)
        elif lane < X1_LO:
            src.append(lane - ROPE_HALF)
        elif lane < X1_LO + ROPE_HALF:
            src.append(D_NOPE + lane - X1_LO)
        elif lane < X1_LO + ROPE_HALF + D_NOPE - NOPE_SPLIT:
            src.append(NOPE_SPLIT + lane - X1_LO - ROPE_HALF)
        else:
            src.append(zero)
    return src


def _to_head_lanes(a):
    padded = jnp.concatenate([a, jnp.zeros(a.shape[:-1] + (1,), a.dtype)], axis=-1)
    return padded[..., jnp.asarray(_head_lane_source(), dtype=jnp.int32)]


def _rope(y, cosf, sinf):
    return y * cosf + pltpu.roll(y, LANES // 2, 1) * sinf


ROW_BLOCK = 128


def _row_blocks(tm):
    rb = min(ROW_BLOCK, tm)
    return [slice(r, r + rb) for r in range(0, tm, rb)]


def _segment_mean_squares(src_scr, cols, seg_ref, sq_scr, ms_scr, slot):
    for rows in _row_blocks(src_scr.shape[0]):
        v = src_scr[rows, cols]
        sq_scr[slot, rows, :] = (v * v).astype(BF16)
    ms_scr[slot] = _dot(sq_scr[slot], seg_ref[...])


def _expand_kv(ckvb_ref, kpe_blk_ref, wkv_ref, kgain_ref, kv_scr, k_ref, v_ref, between=None):
    k_cols = D_HEADS * LANES
    tm = kv_scr.shape[0]
    for grp in range(kv_scr.shape[1] // MXU_COLS):
        if between and grp in between:
            between[grp]()
        cols = slice(grp * MXU_COLS, (grp + 1) * MXU_COLS)
        kv_scr[:, cols] = _dot(ckvb_ref[...], wkv_ref[:, cols])
        for rows in _row_blocks(tm):
            if cols.start < k_cols:
                for half in range(2):
                    h = 2 * grp + half
                    kb = kv_scr[rows, h * LANES:(h + 1) * LANES]
                    ms = jnp.sum(kb * kb, axis=-1, keepdims=True) * (1.0 / D_NOPE)
                    kn = (kb * lax.rsqrt(ms + EPS)) * kgain_ref[...]
                    k_ref[rows, h * LANES:(h + 1) * LANES] = (kn + kpe_blk_ref[rows, :]).astype(BF16)
            else:
                v_ref[rows, cols.start - k_cols:cols.stop - k_cols] = kv_scr[rows, cols].astype(BF16)


def _odd_inproj_kernel(x_ref, g_ref, w_ref, cinit_ref, convw_ref, qan_ref, wqb_ref, qgain_ref,
                       kvan_ref, krgain_ref, invf_ref, sgn_ref, wkv_ref, kgain_ref, qseg_ref,
                       yc_ref, q_ref, k_ref, v_ref, ckv_ref, kpe_ref, cs_ref,
                       h_scr, z_scr, uext_scr, qn_scr, q_scr, kv_scr, rot_scr, ckvb_scr, kpe_scr, sq_scr, ms_scr,
                       *, tm, tpb, pos0):
    i = pl.program_id(0)
    tile = i % tpb
    off = pl.multiple_of(tile * tm, tm)
    blocks = _row_blocks(tm)
    rb = blocks[0].stop

    @pl.when(tile == 0)
    def _():
        uext_scr[0:8, :] = cinit_ref[0]

    @pl.when(i < tpb)
    def _():
        row = lax.broadcasted_iota(jnp.int32, (tm, LANES), 0)
        ang = (row + (tile * tm + pos0)).astype(F32) * invf_ref[...]
        rot_scr[0, pl.ds(off, tm), :] = jnp.cos(ang)
        rot_scr[1, pl.ds(off, tm), :] = jnp.sin(ang) * sgn_ref[...]

    h_scr[...] = _rms(x_ref[...], g_ref[...]).astype(BF16)
    low_lanes = lax.broadcasted_iota(jnp.int32, (rb, LANES), 1) < ROPE_HALF

    def proj(c0, c1):
        z_scr[:, c0:c1] = _dot(h_scr[...], w_ref[:, c0:c1])

    def rotary(y, rows):
        cosf = rot_scr[0, pl.ds(off + rows.start, rb), :]
        sinf = rot_scr[1, pl.ds(off + rows.start, rb), :]
        return _rope(y, cosf, sinf)

    def conv_group(grp):
        c0 = grp * MXU_COLS
        cols = slice(c0, c0 + MXU_COLS)
        for base in (0, C_WIDTH, 2 * C_WIDTH):
            proj(base + c0, base + c0 + MXU_COLS)
        for rows in blocks:
            up = slice(rows.start + 8, rows.stop + 8)
            uext_scr[up, cols] = (z_scr[rows, C_WIDTH + c0:C_WIDTH + c0 + MXU_COLS]
                                  * z_scr[rows, 2 * C_WIDTH + c0:2 * C_WIDTH + c0 + MXU_COLS])
        for rows in blocks:
            yconv = convw_ref[0:1, cols] * uext_scr[rows.start + 6:rows.stop + 6, cols]
            yconv = yconv + convw_ref[1:2, cols] * uext_scr[rows.start + 7:rows.stop + 7, cols]
            yconv = yconv + convw_ref[2:3, cols] * uext_scr[rows.start + 8:rows.stop + 8, cols]
            yc_ref[rows, cols] = (z_scr[rows, cols] * yconv).astype(BF16)

    proj(ODD_QA, ODD_KVA)
    proj(ODD_KVA, ODD_IN_PAD)

    for rows in blocks:
        qn_scr[rows, :] = _rms(z_scr[rows, ODD_QA:ODD_KVA], qan_ref[...]).astype(BF16)
        ckv = _rms(z_scr[rows, ODD_KVA:ODD_KR], kvan_ref[...])
        ckv_ref[rows, :] = ckv
        ckvb_scr[rows, :] = ckv.astype(BF16)

    for grp in range(D_HEADS * LANES // MXU_COLS):
        cols = slice(grp * MXU_COLS, (grp + 1) * MXU_COLS)
        q_scr[:, cols] = _dot(qn_scr[...], wqb_ref[:, cols])
        slot = grp % 2
        _segment_mean_squares(q_scr, cols, qseg_ref, sq_scr, ms_scr, slot)
        for half in range(2):
            h = 2 * grp + half
            for rows in blocks:
                blk = q_scr[rows, h * LANES:(h + 1) * LANES]
                inv = lax.rsqrt(ms_scr[slot, rows, half * LANES:(half + 1) * LANES] + EPS)
                y = rotary((blk * inv) * qgain_ref[...], rows)
                q_ref[rows, h * LANES:(h + 1) * LANES] = y.astype(BF16)

    for rows in blocks:
        kb = z_scr[rows, ODD_KR:ODD_IN_PAD]
        ms = jnp.sum(kb * kb, axis=-1, keepdims=True) * (1.0 / D_ROPE)
        kpe_blk = rotary((kb * lax.rsqrt(ms + EPS)) * krgain_ref[...], rows)
        kpe_scr[rows, :] = kpe_blk
        x1_then_x2 = jnp.where(low_lanes, pltpu.roll(kpe_blk, LANES - X1_LO, 1), pltpu.roll(kpe_blk, ROPE_HALF, 1))
        kpe_ref[rows, :] = x1_then_x2[:, 0:D_ROPE]
    _expand_kv(ckvb_scr, kpe_scr, wkv_ref, kgain_ref, kv_scr, k_ref, v_ref,
               between={0: lambda: conv_group(0), 4: lambda: conv_group(1)})

    cs_ref[0] = uext_scr[tm + 6:tm + 8, :]
    uext_scr[0:8, :] = uext_scr[tm:tm + 8, :]


def _odd_inproj(x, norm_g, ow, conv_init, n_batch, pos0):
    t = x.shape[0]
    tm = min(TOKEN_TILE, t // n_batch)
    tpb = t // n_batch // tm
    assert tpb * tm * n_batch == t
    row = lambda w: pl.BlockSpec((tm, w), lambda i: (i, 0))
    per_b = lambda r, w: pl.BlockSpec((1, r, w), lambda i: (i // tpb, 0, 0))
    small = [ow["conv_w"], ow["qan"], ow["wqb"], ow["qgain"], ow["kvan"], ow["krgain"],
             ow["invf"], ow["sgn"], ow["wkv"], ow["kgain"], ow["qseg"]]
    kv_w = D_HEADS * LANES
    v_w = D_HEADS * D_V
    return pl.pallas_call(
        functools.partial(_odd_inproj_kernel, tm=tm, tpb=tpb, pos0=pos0),
        grid=(t // tm,),
        in_specs=[row(D_MODEL), _resident((1, D_MODEL)), _resident(ow["w_in"].shape), per_b(8, C_WIDTH)]
                 + [_resident(a.shape) for a in small],
        out_specs=[row(C_WIDTH), row(kv_w), row(kv_w), row(v_w), row(D_KV_LORA), row(D_ROPE),
                   per_b(CONV_W - 1, C_WIDTH)],
        out_shape=[jax.ShapeDtypeStruct((t, C_WIDTH), BF16), jax.ShapeDtypeStruct((t, kv_w), BF16),
                   jax.ShapeDtypeStruct((t, kv_w), BF16), jax.ShapeDtypeStruct((t, v_w), BF16),
                   jax.ShapeDtypeStruct((t, D_KV_LORA), F32), jax.ShapeDtypeStruct((t, D_ROPE), F32),
                   jax.ShapeDtypeStruct((n_batch, CONV_W - 1, C_WIDTH), F32)],
        scratch_shapes=[pltpu.VMEM((tm, D_MODEL), BF16), pltpu.VMEM((tm, ODD_IN_PAD), F32),
                        pltpu.VMEM((tm + 8, C_WIDTH), F32), pltpu.VMEM((tm, D_Q_LORA), BF16),
                        pltpu.VMEM((tm, kv_w), F32), pltpu.VMEM((tm, kv_w + v_w), F32),
                        pltpu.VMEM((2, tpb * tm, LANES), F32), pltpu.VMEM((tm, D_KV_LORA), BF16),
                        pltpu.VMEM((tm, LANES), F32), pltpu.VMEM((2, tm, MXU_COLS), BF16),
                        pltpu.VMEM((2, tm, MXU_COLS), F32)],
        compiler_params=_cparams("arbitrary"),
        name="odd_inproj",
    )(x, norm_g.reshape(1, D_MODEL), ow["w_in"], conv_init, *small)


def _odd_weights(w_in, conv_w, q_a_norm, w_q_b, kv_a_norm, w_kv_b, qn_nope, qn_rope, kn_nope, kn_rope):
    z = lambda n: jnp.zeros((n,), F32)
    qk = D_NOPE + D_ROPE
    w_in = _to_bf16(w_in)
    kr_blk = _to_head_lanes(jnp.concatenate([jnp.zeros((D_MODEL, D_NOPE), BF16), w_in[:, ODD_KR:]], axis=1))
    w_pad = jnp.concatenate([w_in[:, :ODD_KR], kr_blk], axis=1)
    wqb = _to_head_lanes(w_q_b.astype(BF16).reshape(D_Q_LORA, D_HEADS, qk)).reshape(D_Q_LORA, D_HEADS * LANES)
    kvb = w_kv_b.astype(BF16).reshape(D_KV_LORA, D_HEADS, D_NOPE + D_V)
    wk = _to_head_lanes(jnp.concatenate([kvb[:, :, :D_NOPE], jnp.zeros((D_KV_LORA, D_HEADS, D_ROPE), BF16)], axis=-1))
    wkv = jnp.concatenate([wk.reshape(D_KV_LORA, D_HEADS * LANES),
                           kvb[:, :, D_NOPE:].reshape(D_KV_LORA, D_HEADS * D_V)], axis=1)
    inv = 1.0 / (ROPE_THETA ** (jnp.arange(ROPE_HALF, dtype=F32) / ROPE_HALF))
    ones = jnp.ones((ROPE_HALF,), F32)
    lanes = lambda v: _to_head_lanes(v).reshape(1, LANES)
    in_nope = lanes(jnp.concatenate([jnp.ones((D_NOPE,), F32), z(D_ROPE)]))
    in_rope = lanes(jnp.concatenate([z(D_NOPE), jnp.ones((D_ROPE,), F32)]))
    seg_nope = in_nope.T * in_nope * (1.0 / D_NOPE)
    seg_rope = in_rope.T * in_rope * (1.0 / D_ROPE)
    two_heads = lambda m: jnp.kron(jnp.eye(MXU_COLS // LANES, dtype=F32), m).astype(BF16)
    return {
        "qseg": two_heads(seg_nope + seg_rope),
        "wk_dense": kvb[:, :, :D_NOPE].reshape(D_KV_LORA, D_HEADS * D_NOPE),
        "kseg_t": jnp.pad(jnp.kron(jnp.eye(D_HEADS, dtype=F32), jnp.full((1, D_NOPE), 1.0 / D_NOPE, F32)),
                          ((0, 16 - D_HEADS), (0, 0))).astype(BF16),
        "q_absorb": _to_head_lanes(jnp.concatenate(
            [w_kv_b.reshape(D_KV_LORA, D_HEADS, D_NOPE + D_V)[:, :, :D_NOPE] * kn_nope,
             jnp.zeros((D_KV_LORA, D_HEADS, D_ROPE), F32)], axis=-1)).transpose(1, 2, 0).astype(BF16),
        "wv_lanes": jnp.stack([jnp.pad(kvb[:, h, D_NOPE:], ((0, 0), ((h % 2) * D_V, LANES - D_V - (h % 2) * D_V)))
                               for h in range(D_HEADS)]),
        "w_in": w_pad,
        "conv_w": conv_w.astype(F32),
        "qan": q_a_norm.reshape(1, D_Q_LORA),
        "wqb": wqb,
        "qgain": lanes(jnp.concatenate([qn_nope, qn_rope])) * MLA_QSCALE,
        "kvan": kv_a_norm.reshape(1, D_KV_LORA),
        "krgain": lanes(jnp.concatenate([z(D_NOPE), kn_rope])),
        "invf": lanes(jnp.concatenate([z(D_NOPE), inv, inv])),
        "sgn": lanes(jnp.concatenate([z(D_NOPE), -ones, ones])),
        "wkv": wkv,
        "kgain": lanes(jnp.concatenate([kn_nope, z(D_ROPE)])),
        "place": _to_head_lanes(jnp.concatenate([jnp.zeros((D_ROPE, D_NOPE), BF16), jnp.eye(D_ROPE, dtype=BF16)], axis=1)),
    }


def _mla_prompt_kernel(q_ref, k_ref, v_ref, o_ref, s_scr, e_scr, r_scr, *, seq):
    qb = min(MLA_QBLOCK, seq)
    hb = qb // 2
    lo = lax.broadcasted_iota(jnp.int32, (qb, LANES), 1) < D_V
    first_half = lax.broadcasted_iota(jnp.int32, (CHUNK, LANES), 1) < CHUNK
    scores = lambda q, k: lax.dot_general(q, k, _NT, preferred_element_type=F32)
    top, bot = slice(0, hb), slice(hb, qb)

    def spans(i):
        tk = qb * (i + 1)
        return tk - qb, tk - hb, tk

    def qk(i, h):
        t0, t1, tk = spans(i)
        hl = slice(h * LANES, (h + 1) * LANES)
        q0 = i * qb
        if t0 > 0:
            s_scr[h, :, 0:t0] = scores(q_ref[0, q0:q0 + qb, hl], k_ref[0, 0:t0, hl])
        s_scr[h, top, t0:t1] = scores(q_ref[0, q0:q0 + hb, hl], k_ref[0, t0:t1, hl])
        s_scr[h, bot, t0:tk] = scores(q_ref[0, q0 + hb:q0 + qb, hl], k_ref[0, t0:tk, hl])

    def softmax(i, h):
        t0, t1, tk = spans(i)
        for rb in range(qb // CHUNK):
            rows = slice(rb * CHUNK, (rb + 1) * CHUNK)
            visible = t0 + CHUNK * (rb + 1)
            width = t1 if rb < hb // CHUNK else tk
            n_full, ragged = visible // LANES, visible % LANES != 0
            blk = lambda c: s_scr[h, rows, c * LANES:(c + 1) * LANES]
            cols = [blk(c) for c in range(n_full)]
            if ragged:
                cols.append(jnp.where(first_half, blk(n_full), NEG_INF))
            mm = cols[0]
            for c in cols[1:]:
                mm = jnp.maximum(mm, c)
            m = _row_stat(jnp.max, mm)
            acc = None
            for c in range(width // LANES):
                if c < len(cols):
                    e = jnp.exp2(blk(c) - m)
                    if c >= n_full:
                        e = jnp.where(first_half, e, 0.0)
                    acc = e if acc is None else acc + e
                else:
                    e = jnp.zeros((CHUNK, LANES), F32)
                e_scr[h, rows, c * LANES:(c + 1) * LANES] = e.astype(BF16)
            r_scr[h, rows, :] = 1.0 / _row_stat(jnp.sum, acc)

    def pv(i, h):
        t0, t1, tk = spans(i)
        out = jnp.concatenate([_dot(e_scr[h, top, t0:t1], v_ref[0, t0:t1, :]),
                               _dot(e_scr[h, bot, t0:tk], v_ref[0, t0:tk, :])], axis=0)
        if t0 > 0:
            out = out + _dot(e_scr[h, :, 0:t0], v_ref[0, 0:t0, :])
        return out * r_scr[h]

    tasks = [(i, h) for i in range(seq // qb) for h in range(2)]
    n = len(tasks)
    outs = {}
    qk(*tasks[0])
    if n > 1:
        qk(*tasks[1])
    softmax(*tasks[0])
    for k, (i, h) in enumerate(tasks):
        if k + 2 < n:
            qk(*tasks[k + 2])
        if k + 1 < n:
            softmax(*tasks[k + 1])
        outs[h] = pv(i, h)
        if h == 1:
            o_ref[0, i * qb:(i + 1) * qb, :] = jnp.where(lo, outs[0], outs[1]).astype(BF16)


def _mla_prompt(q, k, v):
    nb, seq, _ = q.shape
    qb = min(MLA_QBLOCK, seq)
    return pl.pallas_call(
        functools.partial(_mla_prompt_kernel, seq=seq),
        grid=(nb, D_HEADS // 2),
        in_specs=[pl.BlockSpec((1, seq, 2 * LANES), lambda b, p: (b, 0, p)),
                  pl.BlockSpec((1, seq, 2 * LANES), lambda b, p: (b, 0, p)),
                  pl.BlockSpec((1, seq, LANES), lambda b, p: (b, 0, p))],
        out_specs=pl.BlockSpec((1, seq, LANES), lambda b, p: (b, 0, p)),
        out_shape=jax.ShapeDtypeStruct((nb, seq, D_HEADS * D_V), BF16),
        scratch_shapes=[pltpu.VMEM((2, qb, seq), F32), pltpu.VMEM((2, qb, seq), BF16),
                        pltpu.VMEM((2, qb, LANES), F32)],
        compiler_params=_cparams("arbitrary", "arbitrary"),
        name="mla_prompt",
    )(q, k, v)


MLA_CACHE_ROWS = 512


def _mla_sample_kernel(q_ref, ckv_ref, kpe_ref, kn_ref, vn_ref, place_ref, wk_ref, kseg_ref, qabs_ref, wv_ref,
                       o_ref, ckvb_scr, kpeb_scr, sq_scr, s1_scr, s2_scr, e_scr):
    nq, past = q_ref.shape[1], ckv_ref.shape[1]
    lane_blocks = past // LANES
    lo = lax.broadcasted_iota(jnp.int32, (nq, LANES), 1) < D_V
    head = lambda h: slice(h * LANES, (h + 1) * LANES)

    for r0 in range(0, past, MLA_CACHE_ROWS):
        rows = slice(r0, r0 + MLA_CACHE_ROWS)
        c = ckv_ref[0, rows, :].astype(BF16)
        ckvb_scr[rows, :] = c
        kpeb_scr[rows, :] = _dot(kpe_ref[0, rows, :].astype(BF16), place_ref[...]).astype(BF16)
        kraw = _dot(c, wk_ref[...])
        sq_scr[rows, :] = (kraw * kraw).astype(BF16)
    r_t = lax.rsqrt(lax.dot_general(kseg_ref[...], sq_scr[...], _NT, preferred_element_type=F32) + EPS)

    group = D_HEADS // 2
    inv_den, e_new = {}, {}

    def scores(g):
        heads = range(g * group, (g + 1) * group)
        rows = slice(g * group * nq, (g + 1) * group * nq)
        q_abs = jnp.concatenate([_dot(q_ref[0, :, head(h)], qabs_ref[h]).astype(BF16) for h in heads], axis=0)
        q_all = jnp.concatenate([q_ref[0, :, head(h)] for h in heads], axis=0)
        s1_scr[rows, :] = lax.dot_general(q_abs, ckvb_scr[...], _NT, preferred_element_type=F32)
        s2_scr[rows, :] = lax.dot_general(q_all, kpeb_scr[...], _NT, preferred_element_type=F32)

    def softmax(g):
        for h in range(g * group, (g + 1) * group):
            rows = slice(h * nq, (h + 1) * nq)
            s_new = lax.dot_general(q_ref[0, :, head(h)], kn_ref[0, :, head(h)], _NT, preferred_element_type=F32)
            mm = None
            for c in range(lane_blocks):
                cl = slice(c * LANES, (c + 1) * LANES)
                blk = s1_scr[rows, cl] * r_t[h:h + 1, cl] + s2_scr[rows, cl]
                s1_scr[rows, cl] = blk
                mm = blk if mm is None else jnp.maximum(mm, blk)
            m = jnp.maximum(jnp.max(mm, axis=-1, keepdims=True), jnp.max(s_new, axis=-1, keepdims=True))
            acc = None
            for c in range(lane_blocks):
                cl = slice(c * LANES, (c + 1) * LANES)
                e = jnp.exp2(s1_scr[rows, cl] - m)
                acc = e if acc is None else acc + e
                e_scr[rows, cl] = e.astype(BF16)
            en = jnp.exp2(s_new - m)
            inv_den[h] = 1.0 / (jnp.sum(acc, axis=-1, keepdims=True) + jnp.sum(en, axis=-1, keepdims=True))
            e_new[h] = en.astype(BF16)

    def values(g):
        rows = slice(g * group * nq, (g + 1) * group * nq)
        latent = _dot(e_scr[rows, :], ckvb_scr[...])
        for p in range(g * group // 2, (g + 1) * group // 2):
            halves = []
            for half in range(2):
                h = 2 * p + half
                local = slice((h - g * group) * nq, (h - g * group + 1) * nq)
                o = _dot(latent[local].astype(BF16), wv_ref[h]) + _dot(e_new[h], vn_ref[0, :, head(p)])
                halves.append(o * inv_den[h])
            o_ref[0, :, head(p)] = jnp.where(lo, halves[0], halves[1]).astype(BF16)

    scores(0)
    scores(1)
    softmax(0)
    values(0)
    softmax(1)
    values(1)


def _mla_sample(q, ckv, kpe, kn, vn, ow):
    nb, nq, _ = q.shape
    past = ckv.shape[1]
    assert past % MLA_CACHE_ROWS == 0
    per_b = lambda r, w: pl.BlockSpec((1, r, w), lambda b: (b, 0, 0))
    consts = [ow["place"], ow["wk_dense"], ow["kseg_t"], ow["q_absorb"], ow["wv_lanes"]]
    rows = D_HEADS * nq
    return pl.pallas_call(
        _mla_sample_kernel,
        grid=(nb,),
        in_specs=[per_b(nq, D_HEADS * LANES), per_b(past, D_KV_LORA), per_b(past, D_ROPE),
                  per_b(nq, D_HEADS * LANES), per_b(nq, D_HEADS * D_V)] + [_resident(a.shape) for a in consts],
        out_specs=per_b(nq, D_HEADS * D_V),
        out_shape=jax.ShapeDtypeStruct((nb, nq, D_HEADS * D_V), BF16),
        scratch_shapes=[pltpu.VMEM((past, D_KV_LORA), BF16), pltpu.VMEM((past, LANES), BF16),
                        pltpu.VMEM((past, D_HEADS * D_NOPE), BF16), pltpu.VMEM((rows, past), F32),
                        pltpu.VMEM((rows, past), F32), pltpu.VMEM((rows, past), BF16)],
        compiler_params=_cparams("arbitrary"),
        name="mla_sample",
    )(q, ckv, kpe, kn, vn, *consts)


def _even_layer(xp, xs, nb, seq, ndb, dseq, norm_g, w_in, w_out, a_qn, a_kn, a_sinks, b_qn, b_kn, b_rel,
                t5_table, ck_a, cv_a, ck_b, cv_b, ffn2):
    w_in = _to_bf16(w_in)
    w_out = _to_bf16(w_out)
    w_in_p = jnp.concatenate([_pair_heads(w_in[:, :A_Q], 1), w_in[:, A_Q:]], axis=1)
    woa = _pair_heads(w_out[:A_Q], 0)
    wob = w_out[A_Q:]
    ones = lambda n: jnp.ones((n,), F32)
    scale = HEAD_DIM ** -0.5 * LOG2E
    gain_row = jnp.concatenate([jnp.tile(a_qn, A_HEADS) * scale, jnp.tile(a_kn, A_KV_HEADS), ones(A_KV),
                                jnp.tile(b_qn, B_HEADS) * scale, jnp.tile(b_kn, B_HEADS), ones(B_QKV)]
                               ).reshape(1, EVEN_IN).astype(F32)
    bias_a = _bias_a(t5_table) * LOG2E
    bias_b = _bias_b(b_rel) * LOG2E
    sink = jnp.broadcast_to(jnp.repeat(a_sinks.astype(F32) * LOG2E, PAIR_ROWS)[:, None],
                            (A_HEADS * PAIR_ROWS, LANES))
    la, lb = min(WINDOW, seq), min(B_REACH, seq)

    aq, ak, av, bq, bk, bv, cak, cav, cbk, cbv = _even_inproj(xp, norm_g, w_in_p, gain_row, nb, la, lb)
    r3 = lambda a: a.reshape(nb, seq, a.shape[-1])
    ya = _attn_a(r3(aq), r3(ak), r3(av), bias_a, sink, WINDOW, True)
    yb = _attn_b(r3(bq), r3(bk), r3(bv), bias_b, B_REACH, True)
    yap, ybp = ya.reshape(nb * seq, A_Q), yb.reshape(nb * seq, B_QKV)
    st_p = (cak.reshape(nb, la, A_KV_HEADS, HEAD_DIM), cav.reshape(nb, la, A_KV_HEADS, HEAD_DIM),
            cbk.reshape(nb, lb, B_HEADS, HEAD_DIM), cbv.reshape(nb, lb, B_HEADS, HEAD_DIM))

    ts = ndb * dseq
    aq, ak, av, bq, bk, bv, nak, nav, nbk, nbv = _even_inproj(xs, norm_g, w_in_p, gain_row, 1, ts, ts)
    pad_q = lambda a, rows: jnp.pad(a.reshape(ndb, dseq, a.shape[-1]), ((0, 0), (0, rows - dseq), (0, 0)))

    def window(cache, new, rows):
        w = cache.shape[-2] * cache.shape[-1]
        full = jnp.concatenate([cache.reshape(ndb, -1, w), new.reshape(ndb, dseq, w)], axis=1)
        buf = jnp.pad(full, ((0, 0), (0, rows - dseq), (0, 0))).astype(BF16)
        return full[:, dseq:].reshape(cache.shape), buf

    st_ak, kbuf_a = window(ck_a, nak, PAIR_ROWS)
    st_av, vbuf_a = window(cv_a, nav, PAIR_ROWS)
    st_bk, kbuf_b = window(ck_b, nbk, B_STEP_ROWS)
    st_bv, vbuf_b = window(cv_b, nbv, B_STEP_ROWS)
    ya = _attn_a(pad_q(aq, PAIR_ROWS), kbuf_a, vbuf_a, bias_a, sink, WINDOW, False)[:, :dseq]
    yb = _attn_b(pad_q(bq, B_STEP_ROWS), kbuf_b, vbuf_b, bias_b, B_REACH, False)[:, :dseq]
    xp, xs = _ffn(xp, xs, *ffn2, out_proj=(yap, ybp, ya.reshape(ts, A_Q), yb.reshape(ts, B_QKV), woa, wob))
    return xp, xs, st_p, (st_ak, st_av, st_bk, st_bv)


def _odd_layer(xp, xs, nb, seq, ndb, dseq, past, norm_g, ow, w_out, conv_prev, c_ckv, c_kpe, ffn2):
    w_out = _to_bf16(w_out)
    woc = w_out[:C_WIDTH]
    wod = w_out[C_WIDTH:]

    zero_init = jnp.zeros((nb, 8, C_WIDTH), F32)
    yc, q, k, v, ckv, kpe, cs = _odd_inproj(xp, norm_g, ow, zero_init, nb, 0)
    r3 = lambda a: a.reshape(nb, seq, a.shape[-1])
    ycp, ydp = yc, _mla_prompt(r3(q), r3(k), r3(v)).reshape(nb * seq, D_HEADS * D_V)
    st_p = (cs, ckv.reshape(nb, seq, D_KV_LORA), kpe.reshape(nb, seq, D_ROPE))

    ts = ndb * dseq
    init = jnp.pad(conv_prev.astype(F32), ((0, 0), (8 - (CONV_W - 1), 0), (0, 0)))
    yc, q, kn, vn, ckv, kpe, cs = _odd_inproj(xs, norm_g, ow, init, ndb, past)
    s3 = lambda a: a.reshape(ndb, dseq, a.shape[-1])
    yd = _mla_sample(s3(q), c_ckv, c_kpe, s3(kn), s3(vn), ow)
    xp, xs = _ffn(xp, xs, *ffn2, out_proj=(ycp, ydp, yc, yd.reshape(ts, D_HEADS * D_V), woc, wod))
    st_s = (cs, ckv.reshape(ndb, dseq, D_KV_LORA), kpe.reshape(ndb, dseq, D_ROPE))
    return xp, xs, st_p, st_s


def kernel(x_prompt, x_sample, cache_a_k, cache_a_v, cache_b_k, cache_b_v, state_c_conv, cache_d_ckv, cache_d_kpe, ff1_norm, ff1_w_gu, ff1_w_down, mix_norm, ff2_norm, ff2_w_gu, ff2_w_down, t5_bias_table, ev_w_in, ev_w_out, a_q_norm, a_k_norm, a_sinks, b_q_norm, b_k_norm, b_rel_bias, od_w_in, od_w_out, c_conv_w, d_q_a_norm, d_w_q_b, d_kv_a_norm, d_w_kv_b, d_q_nope_norm, d_q_rope_norm, d_k_nope_norm, d_k_rope_norm):
    nb, seq, _ = x_prompt.shape
    ndb, dseq, _ = x_sample.shape
    past = cache_d_ckv.shape[2]
    depth = ff1_norm.shape[0]
    assert seq % TOKEN_TILE == 0 and dseq == CHUNK and past % CHUNK == 0
    xp = x_prompt.reshape(nb * seq, D_MODEL)
    xs = x_sample.reshape(ndb * dseq, D_MODEL)
    even_p, even_s, odd_p, odd_s = [], [], [], []
    for l in range(depth):
        i = l // 2
        ffn2 = (ff2_norm[l], ff2_w_gu, ff2_w_down, l)
        xp, xs = _ffn(xp, xs, ff1_norm[l], ff1_w_gu, ff1_w_down, l)
        if l % 2 == 0:
            xp, xs, sp, ss = _even_layer(
                xp, xs, nb, seq, ndb, dseq, mix_norm[l], ev_w_in[i], ev_w_out[i], a_q_norm[i], a_k_norm[i],
                a_sinks[i], b_q_norm[i], b_k_norm[i], b_rel_bias[i], t5_bias_table,
                cache_a_k[i], cache_a_v[i], cache_b_k[i], cache_b_v[i], ffn2)
            even_p.append(sp)
            even_s.append(ss)
        else:
            ow = _odd_weights(od_w_in[i], c_conv_w[i], d_q_a_norm[i], d_w_q_b[i], d_kv_a_norm[i], d_w_kv_b[i],
                              d_q_nope_norm[i], d_q_rope_norm[i], d_k_nope_norm[i], d_k_rope_norm[i])
            xp, xs, sp, ss = _odd_layer(xp, xs, nb, seq, ndb, dseq, past, mix_norm[l], ow, od_w_out[i],
                                        state_c_conv[i], cache_d_ckv[i], cache_d_kpe[i], ffn2)
            odd_p.append(sp)
            odd_s.append(ss)
    stack = lambda group, j: group[0][j][None] if len(group) == 1 else jnp.stack([g[j] for g in group])
    return (xp.reshape(nb, seq, D_MODEL), xs.reshape(ndb, dseq, D_MODEL),
            stack(even_p, 0), stack(even_p, 1), stack(even_p, 2), stack(even_p, 3),
            stack(odd_p, 0), stack(odd_p, 1), stack(odd_p, 2),
            stack(even_s, 0), stack(even_s, 1), stack(even_s, 2), stack(even_s, 3),
            stack(odd_s, 0), stack(odd_s, 1), stack(odd_s, 2))
```

```python
import functools
import math

import jax
import jax.numpy as jnp
from jax import lax
from jax.experimental import pallas as pl
from jax.experimental.pallas import tpu as pltpu

F32 = jnp.float32
BF16 = jnp.bfloat16

D_MODEL = 1024
CHUNK = 64
HEAD_DIM = 64
EPS = 1e-6
A_HEADS = 8
A_KV_HEADS = 2
WINDOW = 128
T5_BUCKETS = 32
T5_MAX_DIST = 128
B_HEADS = 8
B_REACH = 512
B_MAX_REL = 128
C_WIDTH = 512
CONV_W = 3
D_HEADS = 8
D_Q_LORA = 256
D_KV_LORA = 128
D_NOPE = 64
D_ROPE = 32
D_V = 64
ROPE_THETA = 10000.0
FFN_DIM = 2816
A_Q = A_HEADS * HEAD_DIM
A_KV = A_KV_HEADS * HEAD_DIM
B_QKV = B_HEADS * HEAD_DIM
EVEN_IN = A_Q + 2 * A_KV + 3 * B_QKV

LANES = 128
TOKEN_TILE = 512
CAST_BLOCK_BYTES = 6 * 1024 * 1024
MXU_COLS = 256
FFN_CHUNK = MXU_COLS
PAIR_ROWS = 2 * CHUNK
MLA_QBLOCK = 512
VMEM_LIMIT_BYTES = 56 * 1024 * 1024
LOG2E = math.log2(math.e)
MLA_QSCALE = (D_NOPE + D_ROPE) ** -0.5 * LOG2E
NEG_INF = float("-inf")

_NT = (((1,), (1,)), ((), ()))


def _cparams(*sem):
    return pltpu.CompilerParams(dimension_semantics=sem, vmem_limit_bytes=VMEM_LIMIT_BYTES)


def _resident(shape):
    zeros = (0,) * len(shape)
    return pl.BlockSpec(shape, lambda *_: zeros, pipeline_mode=pl.Buffered(1))


def _rms(x, g):
    ms = jnp.mean(x * x, axis=-1, keepdims=True)
    return (x * lax.rsqrt(ms + EPS)) * g


def _dot(a, b):
    return jnp.dot(a, b, preferred_element_type=F32)


def _weight_stream(wgu_hbm, wd_hbm, wgu_ref, wd_ref, stage, sem):
    n_chunks = FFN_DIM // FFN_CHUNK
    g_stage, u_stage, d_stage = stage

    def copies(j):
        cols = pl.ds(j * FFN_CHUNK, FFN_CHUNK)
        slot = j % 2
        return (pltpu.make_async_copy(wgu_hbm.at[:, cols], g_stage.at[slot], sem.at[0, slot]),
                pltpu.make_async_copy(wgu_hbm.at[:, pl.ds(FFN_DIM + j * FFN_CHUNK, FFN_CHUNK)],
                                      u_stage.at[slot], sem.at[1, slot]),
                pltpu.make_async_copy(wd_hbm.at[cols, :], d_stage.at[slot], sem.at[2, slot]))

    def fetch(j):
        if j == 0:
            for cp in copies(0):
                cp.start()
        if j + 1 < n_chunks:
            for cp in copies(j + 1):
                cp.start()
        for cp in copies(j):
            cp.wait()
        slot = j % 2
        wgu_ref[:, j * FFN_CHUNK:(j + 1) * FFN_CHUNK] = g_stage[slot].astype(BF16)
        wgu_ref[:, FFN_DIM + j * FFN_CHUNK:FFN_DIM + (j + 1) * FFN_CHUNK] = u_stage[slot].astype(BF16)
        wd_ref[j * FFN_CHUNK:(j + 1) * FFN_CHUNK, :] = d_stage[slot].astype(BF16)

    return fetch


def _ffn_tile(x_ref, y_refs, wo_refs, g_ref, wgu_ref, wd_ref, o_ref, h_scr, a_scr, n_chunks, fetch=None):
    rows = x_ref.shape[0]
    if rows < h_scr.shape[0]:
        h_scr, a_scr = h_scr.at[pl.ds(0, rows)], a_scr.at[pl.ds(0, rows)]
    if y_refs:
        (ya_ref, yb_ref), (woa_ref, wob_ref) = y_refs, wo_refs
        y = _dot(ya_ref[...], woa_ref[...]) + _dot(yb_ref[...], wob_ref[...])
        o_ref[...] = x_ref[...] + y
        res_ref = o_ref
    else:
        res_ref = x_ref
    h_scr[...] = _rms(res_ref[...], g_ref[...]).astype(BF16)
    for j in range(n_chunks):
        if fetch is not None:
            fetch(j)
        h = h_scr[...]
        g = _dot(h, wgu_ref[:, j * FFN_CHUNK:(j + 1) * FFN_CHUNK])
        u = _dot(h, wgu_ref[:, FFN_DIM + j * FFN_CHUNK:FFN_DIM + (j + 1) * FFN_CHUNK])
        a_scr[:, j * FFN_CHUNK:(j + 1) * FFN_CHUNK] = ((g * jax.nn.sigmoid(g)) * u).astype(BF16)
    o_ref[...] = res_ref[...] + 0.5 * _dot(a_scr[...], wd_ref[...])


def _ffn_kernel(*refs, n_chunks, fused_out, layer, n_main):
    if fused_out:
        (xp_ref, yap_ref, ybp_ref, xs_ref, yas_ref, ybs_ref, woa_ref, wob_ref, g_ref, wgu_hbm, wd_hbm,
         op_ref, os_ref, h_scr, a_scr, wgu_ref, wd_ref, g_stage, u_stage, d_stage, sem) = refs
        yp, ys, wo = (yap_ref, ybp_ref), (yas_ref, ybs_ref), (woa_ref, wob_ref)
    else:
        (xp_ref, xs_ref, g_ref, wgu_hbm, wd_hbm,
         op_ref, os_ref, h_scr, a_scr, wgu_ref, wd_ref, g_stage, u_stage, d_stage, sem) = refs
        yp = ys = wo = None
    i = pl.program_id(0)
    tile = functools.partial(_ffn_tile, g_ref=g_ref, wgu_ref=wgu_ref, wd_ref=wd_ref, h_scr=h_scr, a_scr=a_scr,
                             n_chunks=n_chunks)

    @pl.when(i == 0)
    def _():
        fetch = _weight_stream(wgu_hbm.at[layer], wd_hbm.at[layer], wgu_ref, wd_ref,
                               (g_stage, u_stage, d_stage), sem)
        tile(xp_ref, yp, wo, o_ref=op_ref, fetch=fetch)

    @pl.when((i > 0) & (i < n_main))
    def _():
        tile(xp_ref, yp, wo, o_ref=op_ref)

    @pl.when(i == n_main)
    def _():
        tile(xs_ref, ys, wo, o_ref=os_ref)


def _ffn(xp, xs, norm_g, w_gu, w_down, layer, out_proj=None):
    tp, ts = xp.shape[0], xs.shape[0]
    tm = min(TOKEN_TILE, tp)
    n_main = tp // tm
    n_chunks = FFN_DIM // FFN_CHUNK
    assert n_main * tm == tp and ts <= tm and ts % 16 == 0
    main = lambda w: pl.BlockSpec((tm, w), lambda i: (jnp.minimum(i, n_main - 1), 0))
    side_in = lambda w: pl.BlockSpec((ts, w), lambda i: (0, 0), pipeline_mode=pl.Buffered(1))
    side_out = lambda w: pl.BlockSpec((ts, w), lambda i: (0, 0))
    in_hbm = pl.BlockSpec(memory_space=pl.ANY)
    if out_proj is not None:
        yap, ybp, yas, ybs, woa, wob = out_proj
        in_specs = [main(D_MODEL), main(yap.shape[1]), main(ybp.shape[1]),
                    side_in(D_MODEL), side_in(yas.shape[1]), side_in(ybs.shape[1]),
                    _resident(woa.shape), _resident(wob.shape)]
        args = [xp, yap, ybp, xs, yas, ybs, woa, wob]
    else:
        in_specs = [main(D_MODEL), side_in(D_MODEL)]
        args = [xp, xs]
    in_specs += [_resident((1, D_MODEL)), in_hbm, in_hbm]
    args += [norm_g.reshape(1, D_MODEL), w_gu, w_down]
    return pl.pallas_call(
        functools.partial(_ffn_kernel, n_chunks=n_chunks, fused_out=out_proj is not None, layer=layer,
                          n_main=n_main),
        grid=(n_main + 1,),
        in_specs=in_specs,
        out_specs=[main(D_MODEL), side_out(D_MODEL)],
        out_shape=[jax.ShapeDtypeStruct((tp, D_MODEL), F32), jax.ShapeDtypeStruct((ts, D_MODEL), F32)],
        scratch_shapes=[pltpu.VMEM((tm, D_MODEL), BF16), pltpu.VMEM((tm, FFN_DIM), BF16),
                        pltpu.VMEM((D_MODEL, 2 * FFN_DIM), BF16), pltpu.VMEM((FFN_DIM, D_MODEL), BF16),
                        pltpu.VMEM((2, D_MODEL, FFN_CHUNK), F32), pltpu.VMEM((2, D_MODEL, FFN_CHUNK), F32),
                        pltpu.VMEM((2, FFN_CHUNK, D_MODEL), F32),
                        pltpu.SemaphoreType.DMA((3, 2))],
        compiler_params=_cparams("arbitrary"),
        name="ffn_out" if out_proj is not None else "ffn",
    )(*args)


def _cast_kernel(x_ref, o_ref):
    o_ref[...] = x_ref[...].astype(BF16)


def _to_bf16(w, layer=None):
    r, c = w.shape[-2:]
    fits = [d for d in range(16, r + 1, 16) if r % d == 0 and d * c * 4 <= CAST_BLOCK_BYTES]
    tr = max(fits) if fits else r
    out_spec = pl.BlockSpec((tr, c), lambda i: (i, 0))
    in_spec = out_spec if layer is None else pl.BlockSpec((None, tr, c), lambda i: (layer, i, 0))
    return pl.pallas_call(
        _cast_kernel, grid=(r // tr,), in_specs=[in_spec], out_specs=out_spec,
        out_shape=jax.ShapeDtypeStruct((r, c), BF16),
        compiler_params=_cparams("arbitrary"), name="cast_bf16",
    )(w)


def _norm_halves(blk, gain):
    sq = blk * blk
    lo = lax.broadcasted_iota(jnp.int32, blk.shape, 1) < HEAD_DIM
    s_lo = jnp.sum(jnp.where(lo, sq, 0.0), axis=-1, keepdims=True)
    s_hi = jnp.sum(jnp.where(lo, 0.0, sq), axis=-1, keepdims=True)
    inv = jnp.where(lo, lax.rsqrt(s_lo * (1.0 / HEAD_DIM) + EPS), lax.rsqrt(s_hi * (1.0 / HEAD_DIM) + EPS))
    return (blk * inv) * gain


def _even_inproj_kernel(x_ref, g_ref, w_ref, gain_ref,
                        aq_ref, ak_ref, av_ref, bq_ref, bk_ref, bv_ref,
                        cak_ref, cav_ref, cbk_ref, cbv_ref, h_scr, z_scr, *, tm, tpb, rows_a, rows_b):
    i = pl.program_id(0)
    last = (i % tpb) == (tpb - 1)
    h_scr[...] = _rms(x_ref[...], g_ref[...]).astype(BF16)

    def lanes(c):
        return slice(c * LANES, (c + 1) * LANES)

    def emit(c):
        blk = z_scr[:, lanes(c)]
        gain = gain_ref[:, lanes(c)]
        if c < 4:
            aq_ref[:, lanes(c)] = _norm_halves(blk, gain).astype(BF16)
        elif c == 4:
            kn = _norm_halves(blk, gain)
            ak_ref[...] = kn.astype(BF16)
            z_scr[:, lanes(c)] = kn
        elif c == 5:
            av_ref[...] = blk.astype(BF16)
        elif c < 10:
            bq_ref[:, lanes(c - 6)] = _norm_halves(blk, gain).astype(BF16)
        elif c < 14:
            kn = _norm_halves(blk, gain)
            bk_ref[:, lanes(c - 10)] = kn.astype(BF16)
            z_scr[:, lanes(c)] = kn
        else:
            bv_ref[:, lanes(c - 14)] = blk.astype(BF16)

    for grp in range(EVEN_IN // MXU_COLS):
        cols = slice(grp * MXU_COLS, (grp + 1) * MXU_COLS)
        z_scr[:, cols] = _dot(h_scr[...], w_ref[:, cols])
        emit(2 * grp)
        emit(2 * grp + 1)

    @pl.when(last)
    def _():
        cak_ref[0] = z_scr[tm - rows_a:, lanes(4)]
        cav_ref[0] = z_scr[tm - rows_a:, lanes(5)]
        cbk_ref[0] = z_scr[tm - rows_b:, 10 * LANES:14 * LANES]
        cbv_ref[0] = z_scr[tm - rows_b:, 14 * LANES:18 * LANES]


def _even_inproj(x, norm_g, w_in, gain_row, n_batch, rows_a, rows_b):
    t = x.shape[0]
    tm = min(TOKEN_TILE, t)
    tpb = t // n_batch // tm
    assert tpb * tm * n_batch == t and rows_a <= tm and rows_b <= tm
    row = lambda w: pl.BlockSpec((tm, w), lambda i: (i, 0))
    cache = lambda r, w: pl.BlockSpec((1, r, w), lambda i: (i // tpb, 0, 0))
    bshape = lambda w: jax.ShapeDtypeStruct((t, w), BF16)
    cshape = lambda r, w: jax.ShapeDtypeStruct((n_batch, r, w), F32)
    return pl.pallas_call(
        functools.partial(_even_inproj_kernel, tm=tm, tpb=tpb, rows_a=rows_a, rows_b=rows_b),
        grid=(t // tm,),
        in_specs=[row(D_MODEL), _resident((1, D_MODEL)), _resident(w_in.shape), _resident(gain_row.shape)],
        out_specs=[row(A_Q), row(A_KV), row(A_KV), row(B_QKV), row(B_QKV), row(B_QKV),
                   cache(rows_a, A_KV), cache(rows_a, A_KV), cache(rows_b, B_QKV), cache(rows_b, B_QKV)],
        out_shape=[bshape(A_Q), bshape(A_KV), bshape(A_KV), bshape(B_QKV), bshape(B_QKV), bshape(B_QKV),
                   cshape(rows_a, A_KV), cshape(rows_a, A_KV), cshape(rows_b, B_QKV), cshape(rows_b, B_QKV)],
        scratch_shapes=[pltpu.VMEM((tm, D_MODEL), BF16), pltpu.VMEM((tm, EVEN_IN), F32)],
        compiler_params=_cparams("arbitrary"),
        name="even_inproj",
    )(x, norm_g.reshape(1, D_MODEL), w_in, gain_row)


def _pair_heads(a, axis):
    shape = a.shape
    split = shape[:axis] + (2, A_HEADS // 2, HEAD_DIM) + shape[axis + 1:]
    return a.reshape(split).swapaxes(axis, axis + 1).reshape(shape)


def _fill_padded(buf, src_ref, ctx):
    buf[0:ctx, :] = jnp.zeros((ctx, buf.shape[1]), buf.dtype)
    buf[ctx:, :] = src_ref[0]


def _windows(k_ref, v_ref, pad_scr, ctx, pad_front, step_rows, stream):
    win = ctx + step_rows
    if pad_front:
        kbuf, vbuf = pad_scr
        _fill_padded(kbuf, k_ref, ctx)
        _fill_padded(vbuf, v_ref, ctx)
        return (lambda r0: kbuf[pl.ds(r0, win), :]), (lambda r0: vbuf[pl.ds(r0, win), :])
    return (lambda r0: k_ref[stream, pl.ds(r0, win), :]), (lambda r0: v_ref[stream, pl.ds(r0, win), :])


def _fold_lanes(x, op):
    acc = x[:, :LANES]
    for c in range(1, x.shape[1] // LANES):
        acc = op(acc, x[:, c * LANES:(c + 1) * LANES])
    return acc


def _row_stat(reduce_fn, x):
    return jnp.broadcast_to(reduce_fn(x, axis=-1, keepdims=True), x.shape)


PAIR_SLOTS = 4


def _run_steps(make_stages, n_streams, n_steps, n_front, step_rows, lookahead):
    stages = [make_stages(stream) for stream in range(n_streams)]
    tasks = [(stream, t) for stream in range(n_streams) for t in range(n_steps)]

    def run(stage, k):
        stream, t = tasks[k]
        stages[stream][stage](t * step_rows, k % PAIR_SLOTS, t < n_front)

    n = len(tasks)
    if not lookahead:
        for k in range(n):
            for stage in range(3):
                run(stage, k)
        return
    run(0, 0)
    if n > 1:
        run(0, 1)
    run(1, 0)
    for k in range(n):
        if k + 2 < n:
            run(0, k + 2)
        if k + 1 < n:
            run(1, k + 1)
        run(2, k)


def _band_bias(ext_row, n_rows, ctx):
    win = ctx + n_rows
    period = ext_row.shape[1]
    t = pltpu.roll(jnp.broadcast_to(ext_row, (n_rows, period)), 0, 1, stride=1, stride_axis=0)[:, :win]
    chunk = lax.broadcasted_iota(jnp.int32, (n_rows, win), 0) // CHUNK
    col = lax.broadcasted_iota(jnp.int32, (n_rows, win), 1)
    visible = (col >= CHUNK * chunk) & (col < ctx + CHUNK * (chunk + 1))
    return jnp.where(visible, t, NEG_INF)


def _attn_a_kernel(q_ref, k_ref, v_ref, ext_ref, sink_ref, o_ref, bias_ref, s_scr, e_scr, *pad_scr,
                   nq, ctx, pad_front):
    win = ctx + PAIR_ROWS
    lo = lax.broadcasted_iota(jnp.int32, (PAIR_ROWS, LANES), 1) < HEAD_DIM

    @pl.when(pl.program_id(0) == 0)
    def _():
        for head in range(A_HEADS):
            bias_ref[head * PAIR_ROWS:(head + 1) * PAIR_ROWS, :] = _band_bias(
                ext_ref[head:head + 1, :], PAIR_ROWS, ctx)

    def make_stages(stream):
        kwin, vwin = _windows(k_ref, v_ref, pad_scr, ctx, pad_front, PAIR_ROWS, stream)
        return (functools.partial(scores, stream, kwin), functools.partial(softmax, stream),
                functools.partial(values, stream, vwin))

    def scores(stream, kwin, r0, slot, masked):
        q_lo, q_hi = [], []
        for p in range(4):
            qp = q_ref[stream, pl.ds(r0, PAIR_ROWS), p * LANES:(p + 1) * LANES]
            zero = jnp.zeros_like(qp)
            q_lo.append(jnp.where(lo, qp, zero))
            q_hi.append(jnp.where(lo, zero, qp))
        q2 = jnp.concatenate(q_lo + q_hi, axis=0)
        s_scr[slot] = lax.dot_general(q2, kwin(r0), _NT, preferred_element_type=F32)

    def softmax(stream, r0, slot, masked):
        for head in range(A_HEADS):
            rows = slice(head * PAIR_ROWS, (head + 1) * PAIR_ROWS)
            s = s_scr[slot, rows, :] + bias_ref[rows, :]
            if masked:
                col = lax.broadcasted_iota(jnp.int32, (PAIR_ROWS, win), 1)
                s = jnp.where(col + r0 >= ctx, s, NEG_INF)
            sk = sink_ref[rows, :]
            m = jnp.maximum(_row_stat(jnp.max, _fold_lanes(s, jnp.maximum)), sk)
            e = [jnp.exp2(s[:, c * LANES:(c + 1) * LANES] - m) for c in range(win // LANES)]
            den = _row_stat(jnp.sum, functools.reduce(jnp.add, e)) + jnp.exp2(sk - m)
            inv = 1.0 / den
            for c in range(win // LANES):
                e_scr[slot, rows, c * LANES:(c + 1) * LANES] = (e[c] * inv).astype(BF16)

    def values(stream, vwin, r0, slot, masked):
        o2 = _dot(e_scr[slot], vwin(r0))
        half = 4 * PAIR_ROWS
        for p in range(4):
            o = jnp.where(lo, o2[p * PAIR_ROWS:(p + 1) * PAIR_ROWS],
                          o2[half + p * PAIR_ROWS:half + (p + 1) * PAIR_ROWS])
            o_ref[stream, pl.ds(r0, PAIR_ROWS), p * LANES:(p + 1) * LANES] = o.astype(BF16)

    _run_steps(make_stages, q_ref.shape[0], nq // PAIR_ROWS, ctx // PAIR_ROWS if pad_front else 0, PAIR_ROWS,
               lookahead=False)


def _streams_per_block(nb, pad_front):
    return 1 if pad_front else nb


def _attn_a(q, k, v, ext, sink, ctx, pad_front):
    nb, nq, _ = q.shape
    nk = k.shape[1]
    assert nq % PAIR_ROWS == 0 and nk == (nq if pad_front else ctx + nq)
    bb = _streams_per_block(nb, pad_front)
    per_b = lambda r, w: pl.BlockSpec((bb, r, w), lambda b: (b, 0, 0))
    rows, win = A_HEADS * PAIR_ROWS, ctx + PAIR_ROWS
    scratch = [pltpu.VMEM((rows, win), F32),
               pltpu.VMEM((PAIR_SLOTS, rows, win), F32), pltpu.VMEM((PAIR_SLOTS, rows, win), BF16)]
    if pad_front:
        scratch += [pltpu.VMEM((ctx + nq, A_KV), BF16)] * 2
    return pl.pallas_call(
        functools.partial(_attn_a_kernel, nq=nq, ctx=ctx, pad_front=pad_front),
        grid=(nb // bb,),
        in_specs=[per_b(nq, A_Q), per_b(nk, A_KV), per_b(nk, A_KV), _resident(ext.shape), _resident(sink.shape)],
        out_specs=per_b(nq, A_Q),
        out_shape=jax.ShapeDtypeStruct((nb, nq, A_Q), BF16),
        scratch_shapes=scratch,
        compiler_params=_cparams("arbitrary"),
        name="attn_a",
    )(q, k, v, ext, sink)


B_STEP_ROWS = 4 * CHUNK


def _attn_b_kernel(q_ref, k_ref, v_ref, ext_ref, o_ref, bias_scr, s_scr, e_scr, r_scr, *pad_scr,
                   nq, ctx, pad_front):
    win = ctx + B_STEP_ROWS
    lo = lax.broadcasted_iota(jnp.int32, (B_STEP_ROWS, LANES), 1) < HEAD_DIM
    chunks = B_STEP_ROWS // CHUNK
    pair_idx = pl.program_id(1)

    @pl.when(pl.program_id(0) == 0)
    def _():
        for half in range(2):
            bias_scr[pair_idx, half * B_STEP_ROWS:(half + 1) * B_STEP_ROWS, :] = _band_bias(
                ext_ref[0, half:half + 1, :], B_STEP_ROWS, ctx)

    bias_ref = bias_scr.at[pair_idx]

    def make_stages(stream):
        kwin, vwin = _windows(k_ref, v_ref, pad_scr, ctx, pad_front, B_STEP_ROWS, stream)
        return (functools.partial(scores, stream, kwin), functools.partial(softmax, stream),
                functools.partial(values, stream, vwin))

    def scores(stream, kwin, r0, slot, masked):
        qp = q_ref[stream, pl.ds(r0, B_STEP_ROWS), :]
        zero = jnp.zeros_like(qp)
        q2 = jnp.concatenate([jnp.where(lo, qp, zero), jnp.where(lo, zero, qp)], axis=0)
        s_scr[slot] = lax.dot_general(q2, kwin(r0), _NT, preferred_element_type=F32)

    def softmax(stream, r0, slot, masked):
        for rb in range(2 * chunks):
            rows = slice(rb * CHUNK, (rb + 1) * CHUNK)
            ci = rb % chunks
            c_lo, c_hi = CHUNK * ci // LANES, -(-(ctx + CHUNK * (ci + 1)) // LANES)
            band = slice(c_lo * LANES, c_hi * LANES)
            s = s_scr[slot, rows, band] + bias_ref[rows, band]
            if masked:
                col = lax.broadcasted_iota(jnp.int32, s.shape, 1) + c_lo * LANES
                s = jnp.where(col + r0 >= ctx, s, NEG_INF)
            m = _row_stat(jnp.max, _fold_lanes(s, jnp.maximum))
            e = [jnp.exp2(s[:, c * LANES:(c + 1) * LANES] - m) for c in range(c_hi - c_lo)]
            for c in range(win // LANES):
                blk = e[c - c_lo].astype(BF16) if c_lo <= c < c_hi else jnp.zeros((CHUNK, LANES), BF16)
                e_scr[slot, rows, c * LANES:(c + 1) * LANES] = blk
            r_scr[slot, rows, :] = 1.0 / _row_stat(jnp.sum, functools.reduce(jnp.add, e))

    def values(stream, vwin, r0, slot, masked):
        o2 = _dot(e_scr[slot], vwin(r0)) * r_scr[slot]
        o = jnp.where(lo, o2[:B_STEP_ROWS], o2[B_STEP_ROWS:])
        o_ref[stream, pl.ds(r0, B_STEP_ROWS), :] = o.astype(BF16)

    _run_steps(make_stages, q_ref.shape[0], nq // B_STEP_ROWS,
               -(-ctx // B_STEP_ROWS) if pad_front else 0, B_STEP_ROWS, lookahead=True)


def _attn_b(q, k, v, ext, ctx, pad_front):
    nb, nq, _ = q.shape
    nk = k.shape[1]
    assert nq % B_STEP_ROWS == 0 and nk == (nq if pad_front else ctx + nq)
    bb = _streams_per_block(nb, pad_front)
    blk = lambda r: pl.BlockSpec((bb, r, LANES), lambda b, p: (b, 0, p))
    rows, win = 2 * B_STEP_ROWS, ctx + B_STEP_ROWS
    scratch = [pltpu.VMEM((B_HEADS // 2, rows, win), F32),
               pltpu.VMEM((PAIR_SLOTS, rows, win), F32), pltpu.VMEM((PAIR_SLOTS, rows, win), BF16),
               pltpu.VMEM((PAIR_SLOTS, rows, LANES), F32)]
    if pad_front:
        scratch += [pltpu.VMEM((ctx + nq, LANES), BF16)] * 2
    return pl.pallas_call(
        functools.partial(_attn_b_kernel, nq=nq, ctx=ctx, pad_front=pad_front),
        grid=(nb // bb, B_HEADS // 2),
        in_specs=[blk(nq), blk(nk), blk(nk),
                  pl.BlockSpec((1,) + ext.shape[1:], lambda b, p: (p, 0, 0))],
        out_specs=blk(nq),
        out_shape=jax.ShapeDtypeStruct((nb, nq, B_QKV), BF16),
        scratch_shapes=scratch,
        compiler_params=_cparams("arbitrary", "arbitrary"),
        name="attn_b",
    )(q, k, v, ext)


def _t5_bucket(rel):
    nb = T5_BUCKETS // 2
    max_exact = nb // 2
    n = -rel
    ret = jnp.where(n < 0, nb, 0)
    n = jnp.abs(n)
    nf = jnp.maximum(n, 1).astype(F32)
    large = max_exact + (jnp.log(nf / max_exact) / math.log(T5_MAX_DIST / max_exact)
                         * (nb - max_exact)).astype(jnp.int32)
    large = jnp.minimum(large, nb - 1)
    return ret + jnp.where(n < max_exact, n, large)


def _band_values(ctx, n_rows, value_of_rel):
    n_cols = ctx + n_rows
    period = -(-(n_cols + n_rows - 1) // LANES) * LANES
    d = jnp.concatenate([jnp.arange(0, period - (n_rows - 1)), jnp.arange(-(n_rows - 1), 0)])
    return value_of_rel(d - ctx)


def _bias_a(t5_table):
    return _band_values(WINDOW, PAIR_ROWS, lambda rel: t5_table.astype(F32)[_t5_bucket(rel)].T)


def _bias_b(b_rel):
    ext = _band_values(B_REACH, B_STEP_ROWS,
                       lambda rel: b_rel.astype(F32)[:, jnp.clip(rel, -B_MAX_REL, B_MAX_REL) + B_MAX_REL])
    return ext.reshape(B_HEADS // 2, 2, ext.shape[-1])


ODD_QA = 3 * C_WIDTH
ODD_KVA = ODD_QA + D_Q_LORA
ODD_KR = ODD_KVA + D_KV_LORA
ODD_IN_PAD = ODD_KR + LANES
ROPE_HALF = D_ROPE // 2
X1_LO = LANES // 2
NOPE_SPLIT = X1_LO - ROPE_HALF


def _head_lane_source():
    zero = D_NOPE + D_ROPE
    src = []
    for lane in range(LANES):
        if lane < ROPE_HALF:
            src.append(D_NOPE + ROPE_HALF + lane)
        elif lane < X1_LO:
            src.append(lane - ROPE_HALF)
        elif lane < X1_LO + ROPE_HALF:
            src.append(D_NOPE + lane - X1_LO)
        elif lane < X1_LO + ROPE_HALF + D_NOPE - NOPE_SPLIT:
            src.append(NOPE_SPLIT + lane - X1_LO - ROPE_HALF)
        else:
            src.append(zero)
    return src


def _to_head_lanes(a):
    padded = jnp.concatenate([a, jnp.zeros(a.shape[:-1] + (1,), a.dtype)], axis=-1)
    return padded[..., jnp.asarray(_head_lane_source(), dtype=jnp.int32)]


def _rope(y, cosf, sinf):
    return y * cosf + pltpu.roll(y, LANES // 2, 1) * sinf


ROW_BLOCK = 128


def _row_blocks(tm):
    rb = min(ROW_BLOCK, tm)
    return [slice(r, r + rb) for r in range(0, tm, rb)]


def _segment_mean_squares(src_scr, cols, seg_ref, sq_scr, ms_scr, slot):
    for rows in _row_blocks(src_scr.shape[0]):
        v = src_scr[rows, cols]
        sq_scr[slot, rows, :] = (v * v).astype(BF16)
    ms_scr[slot] = _dot(sq_scr[slot], seg_ref[...])


def _expand_kv(ckvb_ref, kpe_blk_ref, wkv_ref, kgain_ref, kv_scr, k_ref, v_ref, between=None):
    k_cols = D_HEADS * LANES
    tm = kv_scr.shape[0]
    for grp in range(kv_scr.shape[1] // MXU_COLS):
        if between and grp in between:
            between[grp]()
        cols = slice(grp * MXU_COLS, (grp + 1) * MXU_COLS)
        kv_scr[:, cols] = _dot(ckvb_ref[...], wkv_ref[:, cols])
        for rows in _row_blocks(tm):
            if cols.start < k_cols:
                for half in range(2):
                    h = 2 * grp + half
                    kb = kv_scr[rows, h * LANES:(h + 1) * LANES]
                    ms = jnp.sum(kb * kb, axis=-1, keepdims=True) * (1.0 / D_NOPE)
                    kn = (kb * lax.rsqrt(ms + EPS)) * kgain_ref[...]
                    k_ref[rows, h * LANES:(h + 1) * LANES] = (kn + kpe_blk_ref[rows, :]).astype(BF16)
            else:
                v_ref[rows, cols.start - k_cols:cols.stop - k_cols] = kv_scr[rows, cols].astype(BF16)


def _odd_inproj_kernel(x_ref, g_ref, w_ref, cinit_ref, convw_ref, qan_ref, wqb_ref, qgain_ref,
                       kvan_ref, krgain_ref, invf_ref, sgn_ref, wkv_ref, kgain_ref, qseg_ref,
                       yc_ref, q_ref, k_ref, v_ref, ckv_ref, kpe_ref, cs_ref,
                       h_scr, z_scr, uext_scr, qn_scr, q_scr, kv_scr, rot_scr, ckvb_scr, kpe_scr, sq_scr, ms_scr,
                       *, tm, tpb, pos0):
    i = pl.program_id(0)
    tile = i % tpb
    off = pl.multiple_of(tile * tm, tm)
    blocks = _row_blocks(tm)
    rb = blocks[0].stop

    @pl.when(tile == 0)
    def _():
        uext_scr[0:8, :] = cinit_ref[0]

    @pl.when(i < tpb)
    def _():
        row = lax.broadcasted_iota(jnp.int32, (tm, LANES), 0)
        ang = (row + (tile * tm + pos0)).astype(F32) * invf_ref[...]
        rot_scr[0, pl.ds(off, tm), :] = jnp.cos(ang)
        rot_scr[1, pl.ds(off, tm), :] = jnp.sin(ang) * sgn_ref[...]

    h_scr[...] = _rms(x_ref[...], g_ref[...]).astype(BF16)
    low_lanes = lax.broadcasted_iota(jnp.int32, (rb, LANES), 1) < ROPE_HALF

    def proj(c0, c1):
        z_scr[:, c0:c1] = _dot(h_scr[...], w_ref[:, c0:c1])

    def rotary(y, rows):
        cosf = rot_scr[0, pl.ds(off + rows.start, rb), :]
        sinf = rot_scr[1, pl.ds(off + rows.start, rb), :]
        return _rope(y, cosf, sinf)

    def conv_group(grp):
        c0 = grp * MXU_COLS
        cols = slice(c0, c0 + MXU_COLS)
        for base in (0, C_WIDTH, 2 * C_WIDTH):
            proj(base + c0, base + c0 + MXU_COLS)
        for rows in blocks:
            up = slice(rows.start + 8, rows.stop + 8)
            uext_scr[up, cols] = (z_scr[rows, C_WIDTH + c0:C_WIDTH + c0 + MXU_COLS]
                                  * z_scr[rows, 2 * C_WIDTH + c0:2 * C_WIDTH + c0 + MXU_COLS])
        for rows in blocks:
            yconv = convw_ref[0:1, cols] * uext_scr[rows.start + 6:rows.stop + 6, cols]
            yconv = yconv + convw_ref[1:2, cols] * uext_scr[rows.start + 7:rows.stop + 7, cols]
            yconv = yconv + convw_ref[2:3, cols] * uext_scr[rows.start + 8:rows.stop + 8, cols]
            yc_ref[rows, cols] = (z_scr[rows, cols] * yconv).astype(BF16)

    proj(ODD_QA, ODD_KVA)
    proj(ODD_KVA, ODD_IN_PAD)

    for rows in blocks:
        qn_scr[rows, :] = _rms(z_scr[rows, ODD_QA:ODD_KVA], qan_ref[...]).astype(BF16)
        ckv = _rms(z_scr[rows, ODD_KVA:ODD_KR], kvan_ref[...])
        ckv_ref[rows, :] = ckv
        ckvb_scr[rows, :] = ckv.astype(BF16)

    for grp in range(D_HEADS * LANES // MXU_COLS):
        cols = slice(grp * MXU_COLS, (grp + 1) * MXU_COLS)
        q_scr[:, cols] = _dot(qn_scr[...], wqb_ref[:, cols])
        slot = grp % 2
        _segment_mean_squares(q_scr, cols, qseg_ref, sq_scr, ms_scr, slot)
        for half in range(2):
            h = 2 * grp + half
            for rows in blocks:
                blk = q_scr[rows, h * LANES:(h + 1) * LANES]
                inv = lax.rsqrt(ms_scr[slot, rows, half * LANES:(half + 1) * LANES] + EPS)
                y = rotary((blk * inv) * qgain_ref[...], rows)
                q_ref[rows, h * LANES:(h + 1) * LANES] = y.astype(BF16)

    for rows in blocks:
        kb = z_scr[rows, ODD_KR:ODD_IN_PAD]
        ms = jnp.sum(kb * kb, axis=-1, keepdims=True) * (1.0 / D_ROPE)
        kpe_blk = rotary((kb * lax.rsqrt(ms + EPS)) * krgain_ref[...], rows)
        kpe_scr[rows, :] = kpe_blk
        x1_then_x2 = jnp.where(low_lanes, pltpu.roll(kpe_blk, LANES - X1_LO, 1), pltpu.roll(kpe_blk, ROPE_HALF, 1))
        kpe_ref[rows, :] = x1_then_x2[:, 0:D_ROPE]
    _expand_kv(ckvb_scr, kpe_scr, wkv_ref, kgain_ref, kv_scr, k_ref, v_ref,
               between={0: lambda: conv_group(0), 4: lambda: conv_group(1)})

    cs_ref[0] = uext_scr[tm + 6:tm + 8, :]
    uext_scr[0:8, :] = uext_scr[tm:tm + 8, :]


def _odd_inproj(x, norm_g, ow, conv_init, n_batch, pos0):
    t = x.shape[0]
    tm = min(TOKEN_TILE, t // n_batch)
    tpb = t // n_batch // tm
    assert tpb * tm * n_batch == t
    row = lambda w: pl.BlockSpec((tm, w), lambda i: (i, 0))
    per_b = lambda r, w: pl.BlockSpec((1, r, w), lambda i: (i // tpb, 0, 0))
    small = [ow["conv_w"], ow["qan"], ow["wqb"], ow["qgain"], ow["kvan"], ow["krgain"],
             ow["invf"], ow["sgn"], ow["wkv"], ow["kgain"], ow["qseg"]]
    kv_w = D_HEADS * LANES
    v_w = D_HEADS * D_V
    return pl.pallas_call(
        functools.partial(_odd_inproj_kernel, tm=tm, tpb=tpb, pos0=pos0),
        grid=(t // tm,),
        in_specs=[row(D_MODEL), _resident((1, D_MODEL)), _resident(ow["w_in"].shape), per_b(8, C_WIDTH)]
                 + [_resident(a.shape) for a in small],
        out_specs=[row(C_WIDTH), row(kv_w), row(kv_w), row(v_w), row(D_KV_LORA), row(D_ROPE),
                   per_b(CONV_W - 1, C_WIDTH)],
        out_shape=[jax.ShapeDtypeStruct((t, C_WIDTH), BF16), jax.ShapeDtypeStruct((t, kv_w), BF16),
                   jax.ShapeDtypeStruct((t, kv_w), BF16), jax.ShapeDtypeStruct((t, v_w), BF16),
                   jax.ShapeDtypeStruct((t, D_KV_LORA), F32), jax.ShapeDtypeStruct((t, D_ROPE), F32),
                   jax.ShapeDtypeStruct((n_batch, CONV_W - 1, C_WIDTH), F32)],
        scratch_shapes=[pltpu.VMEM((tm, D_MODEL), BF16), pltpu.VMEM((tm, ODD_IN_PAD), F32),
                        pltpu.VMEM((tm + 8, C_WIDTH), F32), pltpu.VMEM((tm, D_Q_LORA), BF16),
                        pltpu.VMEM((tm, kv_w), F32), pltpu.VMEM((tm, kv_w + v_w), F32),
                        pltpu.VMEM((2, tpb * tm, LANES), F32), pltpu.VMEM((tm, D_KV_LORA), BF16),
                        pltpu.VMEM((tm, LANES), F32), pltpu.VMEM((2, tm, MXU_COLS), BF16),
                        pltpu.VMEM((2, tm, MXU_COLS), F32)],
        compiler_params=_cparams("arbitrary"),
        name="odd_inproj",
    )(x, norm_g.reshape(1, D_MODEL), ow["w_in"], conv_init, *small)


def _odd_weights(w_in, conv_w, q_a_norm, w_q_b, kv_a_norm, w_kv_b, qn_nope, qn_rope, kn_nope, kn_rope):
    z = lambda n: jnp.zeros((n,), F32)
    qk = D_NOPE + D_ROPE
    w_in = _to_bf16(w_in)
    kr_blk = _to_head_lanes(jnp.concatenate([jnp.zeros((D_MODEL, D_NOPE), BF16), w_in[:, ODD_KR:]], axis=1))
    w_pad = jnp.concatenate([w_in[:, :ODD_KR], kr_blk], axis=1)
    wqb = _to_head_lanes(w_q_b.astype(BF16).reshape(D_Q_LORA, D_HEADS, qk)).reshape(D_Q_LORA, D_HEADS * LANES)
    kvb = w_kv_b.astype(BF16).reshape(D_KV_LORA, D_HEADS, D_NOPE + D_V)
    wk = _to_head_lanes(jnp.concatenate([kvb[:, :, :D_NOPE], jnp.zeros((D_KV_LORA, D_HEADS, D_ROPE), BF16)], axis=-1))
    wkv = jnp.concatenate([wk.reshape(D_KV_LORA, D_HEADS * LANES),
                           kvb[:, :, D_NOPE:].reshape(D_KV_LORA, D_HEADS * D_V)], axis=1)
    inv = 1.0 / (ROPE_THETA ** (jnp.arange(ROPE_HALF, dtype=F32) / ROPE_HALF))
    ones = jnp.ones((ROPE_HALF,), F32)
    lanes = lambda v: _to_head_lanes(v).reshape(1, LANES)
    in_nope = lanes(jnp.concatenate([jnp.ones((D_NOPE,), F32), z(D_ROPE)]))
    in_rope = lanes(jnp.concatenate([z(D_NOPE), jnp.ones((D_ROPE,), F32)]))
    seg_nope = in_nope.T * in_nope * (1.0 / D_NOPE)
    seg_rope = in_rope.T * in_rope * (1.0 / D_ROPE)
    two_heads = lambda m: jnp.kron(jnp.eye(MXU_COLS // LANES, dtype=F32), m).astype(BF16)
    return {
        "qseg": two_heads(seg_nope + seg_rope),
        "wk_dense": kvb[:, :, :D_NOPE].reshape(D_KV_LORA, D_HEADS * D_NOPE),
        "kseg_t": jnp.pad(jnp.kron(jnp.eye(D_HEADS, dtype=F32), jnp.full((1, D_NOPE), 1.0 / D_NOPE, F32)),
                          ((0, 16 - D_HEADS), (0, 0))).astype(BF16),
        "q_absorb": _to_head_lanes(jnp.concatenate(
            [w_kv_b.reshape(D_KV_LORA, D_HEADS, D_NOPE + D_V)[:, :, :D_NOPE] * kn_nope,
             jnp.zeros((D_KV_LORA, D_HEADS, D_ROPE), F32)], axis=-1)).transpose(1, 2, 0).astype(BF16),
        "wv_lanes": jnp.stack([jnp.pad(kvb[:, h, D_NOPE:], ((0, 0), ((h % 2) * D_V, LANES - D_V - (h % 2) * D_V)))
                               for h in range(D_HEADS)]),
        "w_in": w_pad,
        "conv_w": conv_w.astype(F32),
        "qan": q_a_norm.reshape(1, D_Q_LORA),
        "wqb": wqb,
        "qgain": lanes(jnp.concatenate([qn_nope, qn_rope])) * MLA_QSCALE,
        "kvan": kv_a_norm.reshape(1, D_KV_LORA),
        "krgain": lanes(jnp.concatenate([z(D_NOPE), kn_rope])),
        "invf": lanes(jnp.concatenate([z(D_NOPE), inv, inv])),
        "sgn": lanes(jnp.concatenate([z(D_NOPE), -ones, ones])),
        "wkv": wkv,
        "kgain": lanes(jnp.concatenate([kn_nope, z(D_ROPE)])),
        "place": _to_head_lanes(jnp.concatenate([jnp.zeros((D_ROPE, D_NOPE), BF16), jnp.eye(D_ROPE, dtype=BF16)], axis=1)),
    }


def _mla_prompt_kernel(q_ref, k_ref, v_ref, o_ref, s_scr, e_scr, r_scr, *, seq):
    qb = min(MLA_QBLOCK, seq)
    hb = qb // 2
    lo = lax.broadcasted_iota(jnp.int32, (qb, LANES), 1) < D_V
    first_half = lax.broadcasted_iota(jnp.int32, (CHUNK, LANES), 1) < CHUNK
    scores = lambda q, k: lax.dot_general(q, k, _NT, preferred_element_type=F32)
    top, bot = slice(0, hb), slice(hb, qb)

    def spans(i):
        tk = qb * (i + 1)
        return tk - qb, tk - hb, tk

    def qk(i, h):
        t0, t1, tk = spans(i)
        hl = slice(h * LANES, (h + 1) * LANES)
        q0 = i * qb
        if t0 > 0:
            s_scr[h, :, 0:t0] = scores(q_ref[0, q0:q0 + qb, hl], k_ref[0, 0:t0, hl])
        s_scr[h, top, t0:t1] = scores(q_ref[0, q0:q0 + hb, hl], k_ref[0, t0:t1, hl])
        s_scr[h, bot, t0:tk] = scores(q_ref[0, q0 + hb:q0 + qb, hl], k_ref[0, t0:tk, hl])

    def softmax(i, h):
        t0, t1, tk = spans(i)
        for rb in range(qb // CHUNK):
            rows = slice(rb * CHUNK, (rb + 1) * CHUNK)
            visible = t0 + CHUNK * (rb + 1)
            width = t1 if rb < hb // CHUNK else tk
            n_full, ragged = visible // LANES, visible % LANES != 0
            blk = lambda c: s_scr[h, rows, c * LANES:(c + 1) * LANES]
            cols = [blk(c) for c in range(n_full)]
            if ragged:
                cols.append(jnp.where(first_half, blk(n_full), NEG_INF))
            mm = cols[0]
            for c in cols[1:]:
                mm = jnp.maximum(mm, c)
            m = _row_stat(jnp.max, mm)
            acc = None
            for c in range(width // LANES):
                if c < len(cols):
                    e = jnp.exp2(blk(c) - m)
                    if c >= n_full:
                        e = jnp.where(first_half, e, 0.0)
                    acc = e if acc is None else acc + e
                else:
                    e = jnp.zeros((CHUNK, LANES), F32)
                e_scr[h, rows, c * LANES:(c + 1) * LANES] = e.astype(BF16)
            r_scr[h, rows, :] = 1.0 / _row_stat(jnp.sum, acc)

    def pv(i, h):
        t0, t1, tk = spans(i)
        out = jnp.concatenate([_dot(e_scr[h, top, t0:t1], v_ref[0, t0:t1, :]),
                               _dot(e_scr[h, bot, t0:tk], v_ref[0, t0:tk, :])], axis=0)
        if t0 > 0:
            out = out + _dot(e_scr[h, :, 0:t0], v_ref[0, 0:t0, :])
        return out * r_scr[h]

    tasks = [(i, h) for i in reversed(range(seq // qb)) for h in range(2)]
    n = len(tasks)
    outs = {}
    qk(*tasks[0])
    if n > 1:
        qk(*tasks[1])
    softmax(*tasks[0])
    for k, (i, h) in enumerate(tasks):
        if k + 2 < n:
            qk(*tasks[k + 2])
        if k + 1 < n:
            softmax(*tasks[k + 1])
        outs[h] = pv(i, h)
        if h == 1:
            o_ref[0, i * qb:(i + 1) * qb, :] = jnp.where(lo, outs[0], outs[1]).astype(BF16)


def _mla_prompt(q, k, v):
    nb, seq, _ = q.shape
    qb = min(MLA_QBLOCK, seq)
    return pl.pallas_call(
        functools.partial(_mla_prompt_kernel, seq=seq),
        grid=(nb, D_HEADS // 2),
        in_specs=[pl.BlockSpec((1, seq, 2 * LANES), lambda b, p: (b, 0, p)),
                  pl.BlockSpec((1, seq, 2 * LANES), lambda b, p: (b, 0, p)),
                  pl.BlockSpec((1, seq, LANES), lambda b, p: (b, 0, p))],
        out_specs=pl.BlockSpec((1, seq, LANES), lambda b, p: (b, 0, p)),
        out_shape=jax.ShapeDtypeStruct((nb, seq, D_HEADS * D_V), BF16),
        scratch_shapes=[pltpu.VMEM((2, qb, seq), F32), pltpu.VMEM((2, qb, seq), BF16),
                        pltpu.VMEM((2, qb, LANES), F32)],
        compiler_params=_cparams("arbitrary", "arbitrary"),
        name="mla_prompt",
    )(q, k, v)


MLA_CACHE_ROWS = 512


def _mla_sample_kernel(q_ref, ckv_ref, kpe_ref, kn_ref, vn_ref, place_ref, wk_ref, kseg_ref, qabs_ref, wv_ref,
                       o_ref, ckvb_scr, kpeb_scr, sq_scr, s1_scr, s2_scr, e_scr):
    nq, past = q_ref.shape[1], ckv_ref.shape[1]
    lane_blocks = past // LANES
    lo = lax.broadcasted_iota(jnp.int32, (nq, LANES), 1) < D_V
    head = lambda h: slice(h * LANES, (h + 1) * LANES)

    for r0 in range(0, past, MLA_CACHE_ROWS):
        rows = slice(r0, r0 + MLA_CACHE_ROWS)
        c = ckv_ref[0, rows, :].astype(BF16)
        ckvb_scr[rows, :] = c
        kpeb_scr[rows, :] = _dot(kpe_ref[0, rows, :].astype(BF16), place_ref[...]).astype(BF16)
        kraw = _dot(c, wk_ref[...])
        sq_scr[rows, :] = (kraw * kraw).astype(BF16)
    r_t = lax.rsqrt(lax.dot_general(kseg_ref[...], sq_scr[...], _NT, preferred_element_type=F32) + EPS)

    group = D_HEADS // 2
    inv_den, e_new = {}, {}

    def scores(g):
        heads = range(g * group, (g + 1) * group)
        rows = slice(g * group * nq, (g + 1) * group * nq)
        q_abs = jnp.concatenate([_dot(q_ref[0, :, head(h)], qabs_ref[h]).astype(BF16) for h in heads], axis=0)
        q_all = jnp.concatenate([q_ref[0, :, head(h)] for h in heads], axis=0)
        s1_scr[rows, :] = lax.dot_general(q_abs, ckvb_scr[...], _NT, preferred_element_type=F32)
        s2_scr[rows, :] = lax.dot_general(q_all, kpeb_scr[...], _NT, preferred_element_type=F32)

    def softmax(g):
        for h in range(g * group, (g + 1) * group):
            rows = slice(h * nq, (h + 1) * nq)
            s_new = lax.dot_general(q_ref[0, :, head(h)], kn_ref[0, :, head(h)], _NT, preferred_element_type=F32)
            mm = None
            for c in range(lane_blocks):
                cl = slice(c * LANES, (c + 1) * LANES)
                blk = s1_scr[rows, cl] * r_t[h:h + 1, cl] + s2_scr[rows, cl]
                s1_scr[rows, cl] = blk
                mm = blk if mm is None else jnp.maximum(mm, blk)
            m = jnp.maximum(jnp.max(mm, axis=-1, keepdims=True), jnp.max(s_new, axis=-1, keepdims=True))
            acc = None
            for c in range(lane_blocks):
                cl = slice(c * LANES, (c + 1) * LANES)
                e = jnp.exp2(s1_scr[rows, cl] - m)
                acc = e if acc is None else acc + e
                e_scr[rows, cl] = e.astype(BF16)
            en = jnp.exp2(s_new - m)
            inv_den[h] = 1.0 / (jnp.sum(acc, axis=-1, keepdims=True) + jnp.sum(en, axis=-1, keepdims=True))
            e_new[h] = en.astype(BF16)

    def values(g):
        rows = slice(g * group * nq, (g + 1) * group * nq)
        latent = _dot(e_scr[rows, :], ckvb_scr[...])
        for p in range(g * group // 2, (g + 1) * group // 2):
            halves = []
            for half in range(2):
                h = 2 * p + half
                local = slice((h - g * group) * nq, (h - g * group + 1) * nq)
                o = _dot(latent[local].astype(BF16), wv_ref[h]) + _dot(e_new[h], vn_ref[0, :, head(p)])
                halves.append(o * inv_den[h])
            o_ref[0, :, head(p)] = jnp.where(lo, halves[0], halves[1]).astype(BF16)

    scores(0)
    scores(1)
    softmax(0)
    values(0)
    softmax(1)
    values(1)


def _mla_sample(q, ckv, kpe, kn, vn, ow):
    nb, nq, _ = q.shape
    past = ckv.shape[1]
    assert past % MLA_CACHE_ROWS == 0
    per_b = lambda r, w: pl.BlockSpec((1, r, w), lambda b: (b, 0, 0))
    consts = [ow["place"], ow["wk_dense"], ow["kseg_t"], ow["q_absorb"], ow["wv_lanes"]]
    rows = D_HEADS * nq
    return pl.pallas_call(
        _mla_sample_kernel,
        grid=(nb,),
        in_specs=[per_b(nq, D_HEADS * LANES), per_b(past, D_KV_LORA), per_b(past, D_ROPE),
                  per_b(nq, D_HEADS * LANES), per_b(nq, D_HEADS * D_V)] + [_resident(a.shape) for a in consts],
        out_specs=per_b(nq, D_HEADS * D_V),
        out_shape=jax.ShapeDtypeStruct((nb, nq, D_HEADS * D_V), BF16),
        scratch_shapes=[pltpu.VMEM((past, D_KV_LORA), BF16), pltpu.VMEM((past, LANES), BF16),
                        pltpu.VMEM((past, D_HEADS * D_NOPE), BF16), pltpu.VMEM((rows, past), F32),
                        pltpu.VMEM((rows, past), F32), pltpu.VMEM((rows, past), BF16)],
        compiler_params=_cparams("arbitrary"),
        name="mla_sample",
    )(q, ckv, kpe, kn, vn, *consts)


def _even_layer(xp, xs, nb, seq, ndb, dseq, norm_g, w_in, w_out, a_qn, a_kn, a_sinks, b_qn, b_kn, b_rel,
                t5_table, ck_a, cv_a, ck_b, cv_b, ffn2):
    w_in = _to_bf16(w_in)
    w_out = _to_bf16(w_out)
    w_in_p = jnp.concatenate([_pair_heads(w_in[:, :A_Q], 1), w_in[:, A_Q:]], axis=1)
    woa = _pair_heads(w_out[:A_Q], 0)
    wob = w_out[A_Q:]
    ones = lambda n: jnp.ones((n,), F32)
    scale = HEAD_DIM ** -0.5 * LOG2E
    gain_row = jnp.concatenate([jnp.tile(a_qn, A_HEADS) * scale, jnp.tile(a_kn, A_KV_HEADS), ones(A_KV),
                                jnp.tile(b_qn, B_HEADS) * scale, jnp.tile(b_kn, B_HEADS), ones(B_QKV)]
                               ).reshape(1, EVEN_IN).astype(F32)
    bias_a = _bias_a(t5_table) * LOG2E
    bias_b = _bias_b(b_rel) * LOG2E
    sink = jnp.broadcast_to(jnp.repeat(a_sinks.astype(F32) * LOG2E, PAIR_ROWS)[:, None],
                            (A_HEADS * PAIR_ROWS, LANES))
    la, lb = min(WINDOW, seq), min(B_REACH, seq)

    aq, ak, av, bq, bk, bv, cak, cav, cbk, cbv = _even_inproj(xp, norm_g, w_in_p, gain_row, nb, la, lb)
    r3 = lambda a: a.reshape(nb, seq, a.shape[-1])
    ya = _attn_a(r3(aq), r3(ak), r3(av), bias_a, sink, WINDOW, True)
    yb = _attn_b(r3(bq), r3(bk), r3(bv), bias_b, B_REACH, True)
    yap, ybp = ya.reshape(nb * seq, A_Q), yb.reshape(nb * seq, B_QKV)
    st_p = (cak.reshape(nb, la, A_KV_HEADS, HEAD_DIM), cav.reshape(nb, la, A_KV_HEADS, HEAD_DIM),
            cbk.reshape(nb, lb, B_HEADS, HEAD_DIM), cbv.reshape(nb, lb, B_HEADS, HEAD_DIM))

    ts = ndb * dseq
    aq, ak, av, bq, bk, bv, nak, nav, nbk, nbv = _even_inproj(xs, norm_g, w_in_p, gain_row, 1, ts, ts)
    pad_q = lambda a, rows: jnp.pad(a.reshape(ndb, dseq, a.shape[-1]), ((0, 0), (0, rows - dseq), (0, 0)))

    def window(cache, new, rows):
        w = cache.shape[-2] * cache.shape[-1]
        full = jnp.concatenate([cache.reshape(ndb, -1, w), new.reshape(ndb, dseq, w)], axis=1)
        buf = jnp.pad(full, ((0, 0), (0, rows - dseq), (0, 0))).astype(BF16)
        return full[:, dseq:].reshape(cache.shape), buf

    st_ak, kbuf_a = window(ck_a, nak, PAIR_ROWS)
    st_av, vbuf_a = window(cv_a, nav, PAIR_ROWS)
    st_bk, kbuf_b = window(ck_b, nbk, B_STEP_ROWS)
    st_bv, vbuf_b = window(cv_b, nbv, B_STEP_ROWS)
    ya = _attn_a(pad_q(aq, PAIR_ROWS), kbuf_a, vbuf_a, bias_a, sink, WINDOW, False)[:, :dseq]
    yb = _attn_b(pad_q(bq, B_STEP_ROWS), kbuf_b, vbuf_b, bias_b, B_REACH, False)[:, :dseq]
    xp, xs = _ffn(xp, xs, *ffn2, out_proj=(yap, ybp, ya.reshape(ts, A_Q), yb.reshape(ts, B_QKV), woa, wob))
    return xp, xs, st_p, (st_ak, st_av, st_bk, st_bv)


def _odd_layer(xp, xs, nb, seq, ndb, dseq, past, norm_g, ow, w_out, conv_prev, c_ckv, c_kpe, ffn2):
    w_out = _to_bf16(w_out)
    woc = w_out[:C_WIDTH]
    wod = w_out[C_WIDTH:]

    zero_init = jnp.zeros((nb, 8, C_WIDTH), F32)
    yc, q, k, v, ckv, kpe, cs = _odd_inproj(xp, norm_g, ow, zero_init, nb, 0)
    r3 = lambda a: a.reshape(nb, seq, a.shape[-1])
    ycp, ydp = yc, _mla_prompt(r3(q), r3(k), r3(v)).reshape(nb * seq, D_HEADS * D_V)
    st_p = (cs, ckv.reshape(nb, seq, D_KV_LORA), kpe.reshape(nb, seq, D_ROPE))

    ts = ndb * dseq
    init = jnp.pad(conv_prev.astype(F32), ((0, 0), (8 - (CONV_W - 1), 0), (0, 0)))
    yc, q, kn, vn, ckv, kpe, cs = _odd_inproj(xs, norm_g, ow, init, ndb, past)
    s3 = lambda a: a.reshape(ndb, dseq, a.shape[-1])
    yd = _mla_sample(s3(q), c_ckv, c_kpe, s3(kn), s3(vn), ow)
    xp, xs = _ffn(xp, xs, *ffn2, out_proj=(ycp, ydp, yc, yd.reshape(ts, D_HEADS * D_V), woc, wod))
    st_s = (cs, ckv.reshape(ndb, dseq, D_KV_LORA), kpe.reshape(ndb, dseq, D_ROPE))
    return xp, xs, st_p, st_s


def kernel(x_prompt, x_sample, cache_a_k, cache_a_v, cache_b_k, cache_b_v, state_c_conv, cache_d_ckv, cache_d_kpe, ff1_norm, ff1_w_gu, ff1_w_down, mix_norm, ff2_norm, ff2_w_gu, ff2_w_down, t5_bias_table, ev_w_in, ev_w_out, a_q_norm, a_k_norm, a_sinks, b_q_norm, b_k_norm, b_rel_bias, od_w_in, od_w_out, c_conv_w, d_q_a_norm, d_w_q_b, d_kv_a_norm, d_w_kv_b, d_q_nope_norm, d_q_rope_norm, d_k_nope_norm, d_k_rope_norm):
    nb, seq, _ = x_prompt.shape
    ndb, dseq, _ = x_sample.shape
    past = cache_d_ckv.shape[2]
    depth = ff1_norm.shape[0]
    assert seq % TOKEN_TILE == 0 and dseq == CHUNK and past % CHUNK == 0
    xp = x_prompt.reshape(nb * seq, D_MODEL)
    xs = x_sample.reshape(ndb * dseq, D_MODEL)
    even_p, even_s, odd_p, odd_s = [], [], [], []
    for l in range(depth):
        i = l // 2
        ffn2 = (ff2_norm[l], ff2_w_gu, ff2_w_down, l)
        xp, xs = _ffn(xp, xs, ff1_norm[l], ff1_w_gu, ff1_w_down, l)
        if l % 2 == 0:
            xp, xs, sp, ss = _even_layer(
                xp, xs, nb, seq, ndb, dseq, mix_norm[l], ev_w_in[i], ev_w_out[i], a_q_norm[i], a_k_norm[i],
                a_sinks[i], b_q_norm[i], b_k_norm[i], b_rel_bias[i], t5_bias_table,
                cache_a_k[i], cache_a_v[i], cache_b_k[i], cache_b_v[i], ffn2)
            even_p.append(sp)
            even_s.append(ss)
        else:
            ow = _odd_weights(od_w_in[i], c_conv_w[i], d_q_a_norm[i], d_w_q_b[i], d_kv_a_norm[i], d_w_kv_b[i],
                              d_q_nope_norm[i], d_q_rope_norm[i], d_k_nope_norm[i], d_k_rope_norm[i])
            xp, xs, sp, ss = _odd_layer(xp, xs, nb, seq, ndb, dseq, past, mix_norm[l], ow, od_w_out[i],
                                        state_c_conv[i], cache_d_ckv[i], cache_d_kpe[i], ffn2)
            odd_p.append(sp)
            odd_s.append(ss)
    stack = lambda group, j: group[0][j][None] if len(group) == 1 else jnp.stack([g[j] for g in group])
    return (xp.reshape(nb, seq, D_MODEL), xs.reshape(ndb, dseq, D_MODEL),
            stack(even_p, 0), stack(even_p, 1), stack(even_p, 2), stack(even_p, 3),
            stack(odd_p, 0), stack(odd_p, 1), stack(odd_p, 2),
            stack(even_s, 0), stack(even_s, 1), stack(even_s, 2), stack(even_s, 3),
            stack(odd_s, 0), stack(odd_s, 1), stack(odd_s, 2))
```
